```python
import jax, jax.numpy as jnp
from jax import lax
import numpy as np

D_MODEL = 2048
BATCH = 8
SEQ = 4096
DEPTH = 2

GDN_HEADS = 8
GDN_HEAD_DIM = 128
GDN_WIDTH = GDN_HEADS * GDN_HEAD_DIM
GDN_CHUNK = 64
CONV_WIDTH = 4
LRU_WIDTH = 1024
LRU_BLOCKS = 8
LRU_BLOCK_DIM = LRU_WIDTH // LRU_BLOCKS
LRU_C = 8.0
D_FF = 4 * D_MODEL
N_BRANCHES = 2
RMS_EPS = 1e-6
L2_EPS = 1e-6

SPLIT_SIZES = (3 * GDN_WIDTH, GDN_WIDTH, GDN_HEADS, GDN_HEADS, LRU_WIDTH, LRU_WIDTH, D_MODEL, D_MODEL)
IN_COLS = 3 * GDN_WIDTH + GDN_WIDTH + 2 * GDN_HEADS + 2 * LRU_WIDTH + N_BRANCHES * D_MODEL

kernel_name = 'hybrid_gdn_rglru_gated_merge'


def rms_norm(x, gain):
    xf = x.astype(jnp.float32)
    y = xf * lax.rsqrt(jnp.mean(xf * xf, axis=-1, keepdims=True) + RMS_EPS)
    return (y * gain.astype(jnp.float32)).astype(x.dtype)


def l2_normalize(x):
    return x * lax.rsqrt(jnp.sum(x * x, axis=-1, keepdims=True) + L2_EPS)


def causal_depthwise_conv(x, w):
    width = w.shape[0]
    seq = x.shape[1]
    xp = jnp.pad(x, ((0, 0), (width - 1, 0), (0, 0)))
    y = xp[:, 0:seq, :] * w[0]
    for j in range(1, width):
        y = y + xp[:, j:j + seq, :] * w[j]
    return y


def gated_delta_rule_chunked(q, k, v, g, beta):
    bsz, heads, seq, dk = q.shape
    dv = v.shape[-1]
    c = GDN_CHUNK
    n = seq // c
    q = q * (dk ** -0.5)

    def chunks(t):
        return t.reshape((bsz, heads, n, c) + t.shape[3:])

    q, k, v, g, beta = chunks(q), chunks(k), chunks(v), chunks(g), chunks(beta)
    g = jnp.cumsum(g, axis=-1)
    k_beta = k * beta[..., None]
    v_beta = v * beta[..., None]
    causal = jnp.tril(jnp.ones((c, c), dtype=bool))
    strict = jnp.tril(jnp.ones((c, c), dtype=bool), -1)
    diff = g[..., :, None] - g[..., None, :]
    decay = jnp.where(causal, jnp.exp(jnp.where(causal, diff, 0.0)), 0.0)
    lower = jnp.where(strict, jnp.einsum('bhnid,bhnjd->bhnij', k_beta, k) * decay, 0.0)
    eye = jnp.eye(c, dtype=q.dtype)
    t_inv = lax.linalg.triangular_solve(eye + lower, jnp.broadcast_to(eye, lower.shape),
                                        left_side=True, lower=True, unit_diagonal=True)
    u = jnp.einsum('bhnij,bhnjv->bhniv', t_inv, v_beta)
    w = jnp.einsum('bhnij,bhnjk->bhnik', t_inv, k_beta * jnp.exp(g)[..., None])
    qk = jnp.where(causal, jnp.einsum('bhnid,bhnjd->bhnij', q, k) * decay, 0.0)
    q_dec = q * jnp.exp(g)[..., None]
    g_last = g[..., -1]
    k_dec = k * jnp.exp(g_last[..., None] - g)[..., None]

    xs = (jnp.moveaxis(u, 2, 0), jnp.moveaxis(w, 2, 0), jnp.moveaxis(qk, 2, 0),
          jnp.moveaxis(q_dec, 2, 0), jnp.moveaxis(k_dec, 2, 0), jnp.moveaxis(g_last, 2, 0))

    def step(state, inp):
        u_n, w_n, qk_n, qd_n, kd_n, gl_n = inp
        v_new = u_n - jnp.einsum('bhck,bhkv->bhcv', w_n, state)
        out = (jnp.einsum('bhck,bhkv->bhcv', qd_n, state)
               + jnp.einsum('bhij,bhjv->bhiv', qk_n, v_new))
        state = state * jnp.exp(gl_n)[..., None, None] + jnp.einsum('bhck,bhcv->bhkv', kd_n, v_new)
        return state, out

    state0 = jnp.zeros((bsz, heads, dk, dv), dtype=q.dtype)
    _, out = lax.scan(step, state0, xs)
    return jnp.moveaxis(out, 0, 2).reshape(bsz, heads, seq, dv)


def gated_deltanet_branch(qkv, z, a, b, conv_w, a_log, dt_bias, norm_gain):
    bsz, seq, _ = qkv.shape
    f32 = jnp.float32
    qkv = jax.nn.silu(causal_depthwise_conv(qkv, conv_w))
    q, k, v = jnp.split(qkv, 3, axis=-1)

    def to_heads(t):
        return t.reshape(bsz, seq, GDN_HEADS, GDN_HEAD_DIM).transpose(0, 2, 1, 3).astype(f32)

    q = l2_normalize(to_heads(q))
    k = l2_normalize(to_heads(k))
    v = to_heads(v)
    g = -jnp.exp(a_log.astype(f32)) * jax.nn.softplus(a.astype(f32) + dt_bias.astype(f32))
    beta = jax.nn.sigmoid(b.astype(f32))
    o = gated_delta_rule_chunked(q, k, v, g.transpose(0, 2, 1), beta.transpose(0, 2, 1))
    o = rms_norm(o.transpose(0, 2, 1, 3), norm_gain)
    o = o.astype(z.dtype) * jax.nn.silu(z).reshape(bsz, seq, GDN_HEADS, GDN_HEAD_DIM)
    return o.reshape(bsz, seq, GDN_WIDTH)


def rglru_branch(xb, yb, conv_w, conv_b, w_gate_a, b_gate_a, w_gate_x, b_gate_x, lam):
    bsz, seq, _ = xb.shape
    f32 = jnp.float32
    xc = causal_depthwise_conv(xb, conv_w) + conv_b
    xh = xc.reshape(bsz, seq, LRU_BLOCKS, LRU_BLOCK_DIM)
    r = jax.nn.sigmoid(jnp.einsum('bsni,nij->bsnj', xh, w_gate_a).reshape(bsz, seq, LRU_WIDTH) + b_gate_a)
    i = jax.nn.sigmoid(jnp.einsum('bsni,nij->bsnj', xh, w_gate_x).reshape(bsz, seq, LRU_WIDTH) + b_gate_x)
    log_a = -LRU_C * r.astype(f32) * jax.nn.softplus(-lam.astype(f32))
    a = jnp.exp(log_a)
    bterm = jnp.sqrt(-jnp.expm1(2.0 * log_a)) * (i * xc).astype(f32)

    def combine(left, right):
        a_l, b_l = left
        a_r, b_r = right
        return a_l * a_r, a_r * b_l + b_r

    _, h = lax.associative_scan(combine, (a, bterm), axis=1)
    return h.astype(xb.dtype) * jax.nn.gelu(yb)


def hybrid_layer(x, attn_norm, w_in, gdn_conv_w, gdn_a_log, gdn_dt_bias, gdn_norm,
                 lru_conv_w, lru_conv_b, lru_w_a, lru_b_a, lru_w_x, lru_b_x, lru_lambda,
                 w_branch_gdn, w_branch_lru, w_out, mlp_norm, w_up, w_down):
    h = rms_norm(x, attn_norm)
    proj = h @ w_in
    offsets = []
    acc = 0
    for s in SPLIT_SIZES[:-1]:
        acc += s
        offsets.append(acc)
    qkv, z, a, b, xb, yb, gate_gdn, gate_lru = jnp.split(proj, offsets, axis=-1)
    o_gdn = gated_deltanet_branch(qkv, z, a, b, gdn_conv_w, gdn_a_log, gdn_dt_bias, gdn_norm)
    o_lru = rglru_branch(xb, yb, lru_conv_w, lru_conv_b, lru_w_a, lru_b_a, lru_w_x, lru_b_x, lru_lambda)
    merged = (jax.nn.sigmoid(gate_gdn) * (o_gdn @ w_branch_gdn)
              + jax.nn.sigmoid(gate_lru) * (o_lru @ w_branch_lru))
    x = x + merged @ w_out
    h = rms_norm(x, mlp_norm)
    x = x + jnp.square(jax.nn.relu(h @ w_up)) @ w_down
    return x


def _fwd_setup_inputs(seed: int = 0) -> dict:
    key = jax.random.key(seed)
    ks = jax.random.split(key, 24)
    f32 = jnp.float32
    d, l = D_MODEL, DEPTH

    def normal(k, shape, scale):
        return jax.random.normal(k, shape, f32) * scale

    def gain(k, shape):
        return 1.0 + 0.02 * jax.random.normal(k, shape, f32)

    x = normal(ks[0], (BATCH, SEQ, d), 1.0)
    attn_norm = gain(ks[1], (l, d))
    w_in = normal(ks[2], (l, d, IN_COLS), d ** -0.5)
    gdn_conv_w = normal(ks[3], (l, CONV_WIDTH, 3 * GDN_WIDTH), CONV_WIDTH ** -0.5)
    gdn_a_log = jnp.log(jax.random.uniform(ks[4], (l, GDN_HEADS), f32, 1.0, 16.0))
    dt = jnp.exp(jax.random.uniform(ks[5], (l, GDN_HEADS), f32, np.log(1e-3), np.log(1e-1)))
    gdn_dt_bias = dt + jnp.log(-jnp.expm1(-dt))
    gdn_norm = gain(ks[6], (l, GDN_HEAD_DIM))
    lru_conv_w = normal(ks[7], (l, CONV_WIDTH, LRU_WIDTH), CONV_WIDTH ** -0.5)
    lru_conv_b = normal(ks[8], (l, LRU_WIDTH), 0.01)
    lru_w_a = normal(ks[9], (l, LRU_BLOCKS, LRU_BLOCK_DIM, LRU_BLOCK_DIM), LRU_BLOCK_DIM ** -0.5)
    lru_b_a = normal(ks[10], (l, LRU_WIDTH), 0.01)
    lru_w_x = normal(ks[11], (l, LRU_BLOCKS, LRU_BLOCK_DIM, LRU_BLOCK_DIM), LRU_BLOCK_DIM ** -0.5)
    lru_b_x = normal(ks[12], (l, LRU_WIDTH), 0.01)
    a0 = jax.random.uniform(ks[13], (l, LRU_WIDTH), f32, 0.9, 0.999)
    s0 = a0 ** (1.0 / LRU_C)
    lru_lambda = jnp.log(s0) - jnp.log1p(-s0)
    w_branch_gdn = normal(ks[14], (l, GDN_WIDTH, d), GDN_WIDTH ** -0.5)
    w_branch_lru = normal(ks[15], (l, LRU_WIDTH, d), LRU_WIDTH ** -0.5)
    w_out = normal(ks[16], (l, d, d), d ** -0.5)
    mlp_norm = gain(ks[17], (l, d))
    w_up = normal(ks[18], (l, d, D_FF), d ** -0.5)
    w_down = normal(ks[19], (l, D_FF, d), D_FF ** -0.5)
    final_norm = gain(ks[20], (d,))
    return {'x': x, 'attn_norm': attn_norm, 'w_in': w_in, 'gdn_conv_w': gdn_conv_w,
            'gdn_a_log': gdn_a_log, 'gdn_dt_bias': gdn_dt_bias, 'gdn_norm': gdn_norm,
            'lru_conv_w': lru_conv_w, 'lru_conv_b': lru_conv_b, 'lru_w_a': lru_w_a, 'lru_b_a': lru_b_a,
            'lru_w_x': lru_w_x, 'lru_b_x': lru_b_x, 'lru_lambda': lru_lambda,
            'w_branch_gdn': w_branch_gdn, 'w_branch_lru': w_branch_lru, 'w_out': w_out,
            'mlp_norm': mlp_norm, 'w_up': w_up, 'w_down': w_down, 'final_norm': final_norm}


def _fwd_reference(x, attn_norm, w_in, gdn_conv_w, gdn_a_log, gdn_dt_bias, gdn_norm,
              lru_conv_w, lru_conv_b, lru_w_a, lru_b_a, lru_w_x, lru_b_x, lru_lambda,
              w_branch_gdn, w_branch_lru, w_out, mlp_norm, w_up, w_down, final_norm):
    for layer in range(DEPTH):
        x = hybrid_layer(x, attn_norm[layer], w_in[layer], gdn_conv_w[layer], gdn_a_log[layer],
                         gdn_dt_bias[layer], gdn_norm[layer], lru_conv_w[layer], lru_conv_b[layer],
                         lru_w_a[layer], lru_b_a[layer], lru_w_x[layer], lru_b_x[layer], lru_lambda[layer],
                         w_branch_gdn[layer], w_branch_lru[layer], w_out[layer], mlp_norm[layer],
                         w_up[layer], w_down[layer])
    return rms_norm(x, final_norm)


import jax as _jax
import jax.numpy as _jnp

TWIN_FORMAT = 'train_step'
FWD_PARAMS = ['x', 'attn_norm', 'w_in', 'gdn_conv_w', 'gdn_a_log', 'gdn_dt_bias', 'gdn_norm', 'lru_conv_w', 'lru_conv_b', 'lru_w_a', 'lru_b_a', 'lru_w_x', 'lru_b_x', 'lru_lambda', 'w_branch_gdn', 'w_branch_lru', 'w_out', 'mlp_norm', 'w_up', 'w_down', 'final_norm']
TWIN_WEIGHTS = ['attn_norm', 'w_in', 'gdn_conv_w', 'gdn_a_log', 'gdn_dt_bias', 'gdn_norm', 'lru_conv_w', 'lru_conv_b', 'lru_w_a', 'lru_b_a', 'lru_w_x', 'lru_b_x', 'lru_lambda', 'w_branch_gdn', 'w_branch_lru', 'w_out', 'mlp_norm', 'w_up', 'w_down', 'final_norm']
TWIN_DIFF_INPUT = 'x'
TWIN_INPUTS = ['x', 'attn_norm', 'w_in', 'gdn_conv_w', 'gdn_a_log', 'gdn_dt_bias', 'gdn_norm', 'lru_conv_w', 'lru_conv_b', 'lru_w_a', 'lru_b_a', 'lru_w_x', 'lru_b_x', 'lru_lambda', 'w_branch_gdn', 'w_branch_lru', 'w_out', 'mlp_norm', 'w_up', 'w_down', 'final_norm', 'loss_target', 'm_attn_norm', 'm_w_in', 'm_gdn_conv_w', 'm_gdn_a_log', 'm_gdn_dt_bias', 'm_gdn_norm', 'm_lru_conv_w', 'm_lru_conv_b', 'm_lru_w_a', 'm_lru_b_a', 'm_lru_w_x', 'm_lru_b_x', 'm_lru_lambda', 'm_w_branch_gdn', 'm_w_branch_lru', 'm_w_out', 'm_mlp_norm', 'm_w_up', 'm_w_down', 'm_final_norm', 'v_attn_norm', 'v_w_in', 'v_gdn_conv_w', 'v_gdn_a_log', 'v_gdn_dt_bias', 'v_gdn_norm', 'v_lru_conv_w', 'v_lru_conv_b', 'v_lru_w_a', 'v_lru_b_a', 'v_lru_w_x', 'v_lru_b_x', 'v_lru_lambda', 'v_w_branch_gdn', 'v_w_branch_lru', 'v_w_out', 'v_mlp_norm', 'v_w_up', 'v_w_down', 'v_final_norm']
TWIN_OUTPUTS = ['loss', 'grad_x', 'grad_attn_norm', 'grad_w_in', 'grad_gdn_conv_w', 'grad_gdn_a_log', 'grad_gdn_dt_bias', 'grad_gdn_norm', 'grad_lru_conv_w', 'grad_lru_conv_b', 'grad_lru_w_a', 'grad_lru_b_a', 'grad_lru_w_x', 'grad_lru_b_x', 'grad_lru_lambda', 'grad_w_branch_gdn', 'grad_w_branch_lru', 'grad_w_out', 'grad_mlp_norm', 'grad_w_up', 'grad_w_down', 'grad_final_norm', 'delta_attn_norm', 'delta_w_in', 'delta_gdn_conv_w', 'delta_gdn_a_log', 'delta_gdn_dt_bias', 'delta_gdn_norm', 'delta_lru_conv_w', 'delta_lru_conv_b', 'delta_lru_w_a', 'delta_lru_b_a', 'delta_lru_w_x', 'delta_lru_b_x', 'delta_lru_lambda', 'delta_w_branch_gdn', 'delta_w_branch_lru', 'delta_w_out', 'delta_mlp_norm', 'delta_w_up', 'delta_w_down', 'delta_final_norm', 'new_m_attn_norm', 'new_m_w_in', 'new_m_gdn_conv_w', 'new_m_gdn_a_log', 'new_m_gdn_dt_bias', 'new_m_gdn_norm', 'new_m_lru_conv_w', 'new_m_lru_conv_b', 'new_m_lru_w_a', 'new_m_lru_b_a', 'new_m_lru_w_x', 'new_m_lru_b_x', 'new_m_lru_lambda', 'new_m_w_branch_gdn', 'new_m_w_branch_lru', 'new_m_w_out', 'new_m_mlp_norm', 'new_m_w_up', 'new_m_w_down', 'new_m_final_norm', 'new_v_attn_norm', 'new_v_w_in', 'new_v_gdn_conv_w', 'new_v_gdn_a_log', 'new_v_gdn_dt_bias', 'new_v_gdn_norm', 'new_v_lru_conv_w', 'new_v_lru_conv_b', 'new_v_lru_w_a', 'new_v_lru_b_a', 'new_v_lru_w_x', 'new_v_lru_b_x', 'new_v_lru_lambda', 'new_v_w_branch_gdn', 'new_v_w_branch_lru', 'new_v_w_out', 'new_v_mlp_norm', 'new_v_w_up', 'new_v_w_down', 'new_v_final_norm']
TWIN_LEAF_KINDS = {'loss': 'loss', 'grad_x': 'grad_x', 'grad_attn_norm': 'grad_w', 'grad_w_in': 'grad_w', 'grad_gdn_conv_w': 'grad_w', 'grad_gdn_a_log': 'grad_w', 'grad_gdn_dt_bias': 'grad_w', 'grad_gdn_norm': 'grad_w', 'grad_lru_conv_w': 'grad_w', 'grad_lru_conv_b': 'grad_w', 'grad_lru_w_a': 'grad_w', 'grad_lru_b_a': 'grad_w', 'grad_lru_w_x': 'grad_w', 'grad_lru_b_x': 'grad_w', 'grad_lru_lambda': 'grad_w', 'grad_w_branch_gdn': 'grad_w', 'grad_w_branch_lru': 'grad_w', 'grad_w_out': 'grad_w', 'grad_mlp_norm': 'grad_w', 'grad_w_up': 'grad_w', 'grad_w_down': 'grad_w', 'grad_final_norm': 'grad_w', 'delta_attn_norm': 'delta_w', 'delta_w_in': 'delta_w', 'delta_gdn_conv_w': 'delta_w', 'delta_gdn_a_log': 'delta_w', 'delta_gdn_dt_bias': 'delta_w', 'delta_gdn_norm': 'delta_w', 'delta_lru_conv_w': 'delta_w', 'delta_lru_conv_b': 'delta_w', 'delta_lru_w_a': 'delta_w', 'delta_lru_b_a': 'delta_w', 'delta_lru_w_x': 'delta_w', 'delta_lru_b_x': 'delta_w', 'delta_lru_lambda': 'delta_w', 'delta_w_branch_gdn': 'delta_w', 'delta_w_branch_lru': 'delta_w', 'delta_w_out': 'delta_w', 'delta_mlp_norm': 'delta_w', 'delta_w_up': 'delta_w', 'delta_w_down': 'delta_w', 'delta_final_norm': 'delta_w', 'new_m_attn_norm': 'new_m', 'new_m_w_in': 'new_m', 'new_m_gdn_conv_w': 'new_m', 'new_m_gdn_a_log': 'new_m', 'new_m_gdn_dt_bias': 'new_m', 'new_m_gdn_norm': 'new_m', 'new_m_lru_conv_w': 'new_m', 'new_m_lru_conv_b': 'new_m', 'new_m_lru_w_a': 'new_m', 'new_m_lru_b_a': 'new_m', 'new_m_lru_w_x': 'new_m', 'new_m_lru_b_x': 'new_m', 'new_m_lru_lambda': 'new_m', 'new_m_w_branch_gdn': 'new_m', 'new_m_w_branch_lru': 'new_m', 'new_m_w_out': 'new_m', 'new_m_mlp_norm': 'new_m', 'new_m_w_up': 'new_m', 'new_m_w_down': 'new_m', 'new_m_final_norm': 'new_m', 'new_v_attn_norm': 'new_v', 'new_v_w_in': 'new_v', 'new_v_gdn_conv_w': 'new_v', 'new_v_gdn_a_log': 'new_v', 'new_v_gdn_dt_bias': 'new_v', 'new_v_gdn_norm': 'new_v', 'new_v_lru_conv_w': 'new_v', 'new_v_lru_conv_b': 'new_v', 'new_v_lru_w_a': 'new_v', 'new_v_lru_b_a': 'new_v', 'new_v_lru_w_x': 'new_v', 'new_v_lru_b_x': 'new_v', 'new_v_lru_lambda': 'new_v', 'new_v_w_branch_gdn': 'new_v', 'new_v_w_branch_lru': 'new_v', 'new_v_w_out': 'new_v', 'new_v_mlp_norm': 'new_v', 'new_v_w_up': 'new_v', 'new_v_w_down': 'new_v', 'new_v_final_norm': 'new_v'}


def _forward(args):
    return _fwd_reference(*[args[k] for k in FWD_PARAMS])


def _output_shape():
    def fwd():
        inp = _fwd_setup_inputs(0)
        return _fwd_reference(*[inp[k] for k in FWD_PARAMS])
    out = _jax.eval_shape(fwd)
    return out.shape, out.dtype

N_MICROBATCH = 1
ADAM_LR = 0.001
ADAM_B1 = 0.9
ADAM_B2 = 0.999
ADAM_EPS = 1e-08
ADAM_WD = 0.01
ADAM_STEP = 10
PER_EXAMPLE_BATCH_AXIS = {'x': 0, 'loss_target': 0}
SHARED_INPUTS = []
_WEIGHT_DTYPES = {'attn_norm': _jnp.float32, 'w_in': _jnp.float32, 'gdn_conv_w': _jnp.float32, 'gdn_a_log': _jnp.float32, 'gdn_dt_bias': _jnp.float32, 'gdn_norm': _jnp.float32, 'lru_conv_w': _jnp.float32, 'lru_conv_b': _jnp.float32, 'lru_w_a': _jnp.float32, 'lru_b_a': _jnp.float32, 'lru_w_x': _jnp.float32, 'lru_b_x': _jnp.float32, 'lru_lambda': _jnp.float32, 'w_branch_gdn': _jnp.float32, 'w_branch_lru': _jnp.float32, 'w_out': _jnp.float32, 'mlp_norm': _jnp.float32, 'w_up': _jnp.float32, 'w_down': _jnp.float32, 'final_norm': _jnp.float32}
MOMENT_SCALE = {'attn_norm': 7.141652e-02, 'w_in': 3.069683e-02, 'gdn_conv_w': 2.799824e-02, 'gdn_a_log': 1.365555e-01, 'gdn_dt_bias': 1.284342e-01, 'gdn_norm': 1.014112e-01, 'lru_conv_w': 6.395193e-02, 'lru_conv_b': 3.482947e-01, 'lru_w_a': 1.014879e-02, 'lru_b_a': 1.443101e-02, 'lru_w_x': 1.925438e-02, 'lru_b_x': 2.478685e-02, 'lru_lambda': 3.677241e-02, 'w_branch_gdn': 2.649288e-02, 'w_branch_lru': 3.957935e-02, 'w_out': 4.600025e-02, 'mlp_norm': 7.533582e-02, 'w_up': 3.748057e-02, 'w_down': 8.483317e-02, 'final_norm': 1.627940e+01}


def _to_microbatches(a, axis):
    t = _jnp.moveaxis(a, axis, 0)
    t = t.reshape((N_MICROBATCH, t.shape[0] // N_MICROBATCH) + t.shape[1:])
    return _jnp.moveaxis(t, 1, axis + 1)


def setup_inputs(seed: int = 0) -> dict:
    inp = _fwd_setup_inputs(seed)
    key = _jax.random.fold_in(_jax.random.key(seed), 7919)
    shape, _ = _output_shape()
    out = dict(inp)
    out["loss_target"] = _jax.random.normal(_jax.random.fold_in(key, 0), shape, _jnp.float32)
    for i, name in enumerate(TWIN_WEIGHTS):
        w = inp[name].astype(_jnp.float32)
        if MOMENT_SCALE is None:
            s = _jnp.sqrt(_jnp.mean(_jnp.square(w)) + 1e-30)
        else:
            s = MOMENT_SCALE[name]
        km, kv = _jax.random.split(_jax.random.fold_in(key, i + 1))
        out[name] = w
        out["m_" + name] = s * _jax.random.normal(km, w.shape, _jnp.float32)
        out["v_" + name] = (s * s) * _jax.random.uniform(kv, w.shape, _jnp.float32, 0.5, 1.5)
    if N_MICROBATCH > 1:
        for name, axis in PER_EXAMPLE_BATCH_AXIS.items():
            out[name] = _to_microbatches(out[name], axis)
    return {'x': out['x'], 'attn_norm': out['attn_norm'], 'w_in': out['w_in'], 'gdn_conv_w': out['gdn_conv_w'], 'gdn_a_log': out['gdn_a_log'], 'gdn_dt_bias': out['gdn_dt_bias'], 'gdn_norm': out['gdn_norm'], 'lru_conv_w': out['lru_conv_w'], 'lru_conv_b': out['lru_conv_b'], 'lru_w_a': out['lru_w_a'], 'lru_b_a': out['lru_b_a'], 'lru_w_x': out['lru_w_x'], 'lru_b_x': out['lru_b_x'], 'lru_lambda': out['lru_lambda'], 'w_branch_gdn': out['w_branch_gdn'], 'w_branch_lru': out['w_branch_lru'], 'w_out': out['w_out'], 'mlp_norm': out['mlp_norm'], 'w_up': out['w_up'], 'w_down': out['w_down'], 'final_norm': out['final_norm'], 'loss_target': out['loss_target'], 'm_attn_norm': out['m_attn_norm'], 'm_w_in': out['m_w_in'], 'm_gdn_conv_w': out['m_gdn_conv_w'], 'm_gdn_a_log': out['m_gdn_a_log'], 'm_gdn_dt_bias': out['m_gdn_dt_bias'], 'm_gdn_norm': out['m_gdn_norm'], 'm_lru_conv_w': out['m_lru_conv_w'], 'm_lru_conv_b': out['m_lru_conv_b'], 'm_lru_w_a': out['m_lru_w_a'], 'm_lru_b_a': out['m_lru_b_a'], 'm_lru_w_x': out['m_lru_w_x'], 'm_lru_b_x': out['m_lru_b_x'], 'm_lru_lambda': out['m_lru_lambda'], 'm_w_branch_gdn': out['m_w_branch_gdn'], 'm_w_branch_lru': out['m_w_branch_lru'], 'm_w_out': out['m_w_out'], 'm_mlp_norm': out['m_mlp_norm'], 'm_w_up': out['m_w_up'], 'm_w_down': out['m_w_down'], 'm_final_norm': out['m_final_norm'], 'v_attn_norm': out['v_attn_norm'], 'v_w_in': out['v_w_in'], 'v_gdn_conv_w': out['v_gdn_conv_w'], 'v_gdn_a_log': out['v_gdn_a_log'], 'v_gdn_dt_bias': out['v_gdn_dt_bias'], 'v_gdn_norm': out['v_gdn_norm'], 'v_lru_conv_w': out['v_lru_conv_w'], 'v_lru_conv_b': out['v_lru_conv_b'], 'v_lru_w_a': out['v_lru_w_a'], 'v_lru_b_a': out['v_lru_b_a'], 'v_lru_w_x': out['v_lru_w_x'], 'v_lru_b_x': out['v_lru_b_x'], 'v_lru_lambda': out['v_lru_lambda'], 'v_w_branch_gdn': out['v_w_branch_gdn'], 'v_w_branch_lru': out['v_w_branch_lru'], 'v_w_out': out['v_w_out'], 'v_mlp_norm': out['v_mlp_norm'], 'v_w_up': out['v_w_up'], 'v_w_down': out['v_w_down'], 'v_final_norm': out['v_final_norm']}


def _loss(weights, diff, rest, loss_target):
    with _jax.named_scope("forward"):
        args = {**rest, TWIN_DIFF_INPUT: diff, **{k: w.astype(_WEIGHT_DTYPES[k]) for k, w in weights.items()}}
        y = _forward(args)
    with _jax.named_scope("loss_head"):
        err = _jnp.square(y.astype(_jnp.float32) - loss_target)
        return 0.5 * _jnp.sum(_jnp.mean(err, axis=-1)) if err.ndim else 0.5 * err


def _adamw(w, g, m, v):
    m = ADAM_B1 * m + (1.0 - ADAM_B1) * g
    v = ADAM_B2 * v + (1.0 - ADAM_B2) * _jnp.square(g)
    m_hat = m / (1.0 - ADAM_B1 ** ADAM_STEP)
    v_hat = v / (1.0 - ADAM_B2 ** ADAM_STEP)
    delta = -ADAM_LR * (m_hat / (_jnp.sqrt(v_hat) + ADAM_EPS) + ADAM_WD * w)
    return delta, m, v


def reference(x, attn_norm, w_in, gdn_conv_w, gdn_a_log, gdn_dt_bias, gdn_norm, lru_conv_w, lru_conv_b, lru_w_a, lru_b_a, lru_w_x, lru_b_x, lru_lambda, w_branch_gdn, w_branch_lru, w_out, mlp_norm, w_up, w_down, final_norm, loss_target, m_attn_norm, m_w_in, m_gdn_conv_w, m_gdn_a_log, m_gdn_dt_bias, m_gdn_norm, m_lru_conv_w, m_lru_conv_b, m_lru_w_a, m_lru_b_a, m_lru_w_x, m_lru_b_x, m_lru_lambda, m_w_branch_gdn, m_w_branch_lru, m_w_out, m_mlp_norm, m_w_up, m_w_down, m_final_norm, v_attn_norm, v_w_in, v_gdn_conv_w, v_gdn_a_log, v_gdn_dt_bias, v_gdn_norm, v_lru_conv_w, v_lru_conv_b, v_lru_w_a, v_lru_b_a, v_lru_w_x, v_lru_b_x, v_lru_lambda, v_w_branch_gdn, v_w_branch_lru, v_w_out, v_mlp_norm, v_w_up, v_w_down, v_final_norm):
    given = dict(x=x, attn_norm=attn_norm, w_in=w_in, gdn_conv_w=gdn_conv_w, gdn_a_log=gdn_a_log, gdn_dt_bias=gdn_dt_bias, gdn_norm=gdn_norm, lru_conv_w=lru_conv_w, lru_conv_b=lru_conv_b, lru_w_a=lru_w_a, lru_b_a=lru_b_a, lru_w_x=lru_w_x, lru_b_x=lru_b_x, lru_lambda=lru_lambda, w_branch_gdn=w_branch_gdn, w_branch_lru=w_branch_lru, w_out=w_out, mlp_norm=mlp_norm, w_up=w_up, w_down=w_down, final_norm=final_norm, loss_target=loss_target, m_attn_norm=m_attn_norm, m_w_in=m_w_in, m_gdn_conv_w=m_gdn_conv_w, m_gdn_a_log=m_gdn_a_log, m_gdn_dt_bias=m_gdn_dt_bias, m_gdn_norm=m_gdn_norm, m_lru_conv_w=m_lru_conv_w, m_lru_conv_b=m_lru_conv_b, m_lru_w_a=m_lru_w_a, m_lru_b_a=m_lru_b_a, m_lru_w_x=m_lru_w_x, m_lru_b_x=m_lru_b_x, m_lru_lambda=m_lru_lambda, m_w_branch_gdn=m_w_branch_gdn, m_w_branch_lru=m_w_branch_lru, m_w_out=m_w_out, m_mlp_norm=m_mlp_norm, m_w_up=m_w_up, m_w_down=m_w_down, m_final_norm=m_final_norm, v_attn_norm=v_attn_norm, v_w_in=v_w_in, v_gdn_conv_w=v_gdn_conv_w, v_gdn_a_log=v_gdn_a_log, v_gdn_dt_bias=v_gdn_dt_bias, v_gdn_norm=v_gdn_norm, v_lru_conv_w=v_lru_conv_w, v_lru_conv_b=v_lru_conv_b, v_lru_w_a=v_lru_w_a, v_lru_b_a=v_lru_b_a, v_lru_w_x=v_lru_w_x, v_lru_b_x=v_lru_b_x, v_lru_lambda=v_lru_lambda, v_w_branch_gdn=v_w_branch_gdn, v_w_branch_lru=v_w_branch_lru, v_w_out=v_w_out, v_mlp_norm=v_mlp_norm, v_w_up=v_w_up, v_w_down=v_w_down, v_final_norm=v_final_norm)
    weights = {n: given[n] for n in TWIN_WEIGHTS}
    shared = {n: given[n] for n in SHARED_INPUTS}
    per_example = {n: given[n] for n in ['x']}
    grad_fn = _jax.value_and_grad(_loss, argnums=(0, 1))

    def one_microbatch(ex, loss_target):
        ex = dict(ex)
        diff = ex.pop(TWIN_DIFF_INPUT)
        return grad_fn(weights, diff, {**shared, **ex}, loss_target)

    if N_MICROBATCH == 1:
        loss, (grad_w, grad_x) = one_microbatch(per_example, given["loss_target"])
    else:
        def body(carry, xs):
            loss_sum, grad_sum = carry
            l_k, (gw_k, gx_k) = one_microbatch(xs[0], xs[1])
            with _jax.named_scope("update"):
                return (loss_sum + l_k, _jax.tree.map(_jnp.add, grad_sum, gw_k)), gx_k

        init = (_jnp.zeros((), _jnp.float32), _jax.tree.map(_jnp.zeros_like, weights))
        (loss, grad_w), grad_x = _jax.lax.scan(body, init, (per_example, given["loss_target"]))
    with _jax.named_scope("update"):
        delta_w, new_m, new_v = {}, {}, {}
        for n in TWIN_WEIGHTS:
            delta_w[n], new_m[n], new_v[n] = _adamw(weights[n], grad_w[n], given["m_" + n], given["v_" + n])
    return (loss, grad_x, *[grad_w[n] for n in TWIN_WEIGHTS], *[delta_w[n] for n in TWIN_WEIGHTS],
            *[new_m[n] for n in TWIN_WEIGHTS], *[new_v[n] for n in TWIN_WEIGHTS])
```

```python
import functools

import jax
import jax.numpy as jnp
from jax import lax
from jax.experimental import pallas as pl
from jax.experimental.pallas import tpu as pltpu

F32 = jnp.float32
BF16 = jnp.bfloat16

LANES = 128
SUBLANES = 8
VMEM_BYTES = 64 * 1024 * 1024
GDN_CHUNK = 64
CONV_WIDTH = 4
RMS_EPS = 1e-6
L2_EPS = 1e-6
LRU_C = 8.0
ADAM_LR = 0.001
ADAM_B1 = 0.9
ADAM_B2 = 0.999
ADAM_EPS = 1e-08
ADAM_WD = 0.01
ADAM_STEP = 10
MESH_AXES = ("x", "y", "c")
N_CHIPS = 4
N_DEVICES = 8

INPUT_NAMES = ['x', 'attn_norm', 'w_in', 'gdn_conv_w', 'gdn_a_log', 'gdn_dt_bias', 'gdn_norm', 'lru_conv_w',
               'lru_conv_b', 'lru_w_a', 'lru_b_a', 'lru_w_x', 'lru_b_x', 'lru_lambda', 'w_branch_gdn',
               'w_branch_lru', 'w_out', 'mlp_norm', 'w_up', 'w_down', 'final_norm']
WEIGHT_NAMES = INPUT_NAMES[1:]
BIG_SHARD_AXIS = {'w_in': 2, 'w_branch_gdn': 2, 'w_branch_lru': 2, 'w_out': 1, 'w_up': 2, 'w_down': 1}
CONV_SHARD_AXIS = {'gdn_conv_w': 2, 'lru_conv_w': 2}
SMALL_NAMES = [n for n in WEIGHT_NAMES if n not in BIG_SHARD_AXIS]


def _tile(n, target, unit=LANES):
    best = None
    t = unit
    while t <= min(n, target):
        if n % t == 0:
            best = t
        t += unit
    return n if best is None else best


def _vmem_limit(block_bytes):
    return int(min(max(3 * block_bytes + (8 << 20), 24 << 20), VMEM_BYTES - (8 << 20)))


def _nbytes(shape, dtype):
    n = 1
    for s in shape:
        n *= s
    return n * jnp.dtype(dtype).itemsize


def _pcall(body, *, name, grid, in_specs, out_specs, out_shape, scratch_shapes=(), semantics=None, block_bytes=0):
    params = dict(vmem_limit_bytes=_vmem_limit(block_bytes))
    if semantics is not None:
        params['dimension_semantics'] = semantics
    return pl.pallas_call(body, name=name, grid=grid, in_specs=in_specs, out_specs=out_specs, out_shape=out_shape,
                          scratch_shapes=list(scratch_shapes), compiler_params=pltpu.CompilerParams(**params))


def _dot(a, b):
    return jnp.dot(a.astype(BF16), b.astype(BF16), preferred_element_type=F32)


def _dot_nt(a, b):
    return lax.dot_general(a.astype(BF16), b.astype(BF16), (((1,), (1,)), ((), ())), preferred_element_type=F32)


def _dot_tn(a, b):
    return lax.dot_general(a.astype(BF16), b.astype(BF16), (((0,), (0,)), ((), ())), preferred_element_type=F32)


def _sigmoid(x):
    return 1.0 / (1.0 + jnp.exp(-x))


def _log1p(u):
    return jnp.where(u < 1e-3, u * (1.0 - u * (0.5 - u * (1.0 / 3.0))), jnp.log(1.0 + u))


def _softplus(x):
    return jnp.maximum(x, 0.0) + _log1p(jnp.exp(-jnp.abs(x)))


_GELU_K = 0.7978845608028654


def _gelu_and_grad(x):
    inner = _GELU_K * (x + 0.044715 * x * x * x)
    th = jnp.tanh(inner)
    g = 0.5 * x * (1.0 + th)
    dg = 0.5 * (1.0 + th) + 0.5 * x * (1.0 - th * th) * _GELU_K * (1.0 + 3.0 * 0.044715 * x * x)
    return g, dg


def _matmul(a, b, *, mode, name, out_dtype=F32, add=None, epilogue=None, extra=None, tm=512, tn=1024, tk=512):
    if mode == 'nn':
        (m, k), (k2, n) = a.shape, b.shape
    elif mode == 'nt':
        (m, k), (n, k2) = a.shape, b.shape
    else:
        (k, m), (k2, n) = a.shape, b.shape
    assert k == k2, (a.shape, b.shape, mode)
    tm, tn, tk = _tile(m, tm), _tile(n, tn), _tile(k, tk)
    nk = k // tk
    dims = {'nn': (((1,), (0,)), ((), ())), 'nt': (((1,), (1,)), ((), ())), 'tn': (((0,), (0,)), ((), ()))}[mode]
    a_spec = (pl.BlockSpec((tk, tm), lambda i, j, kk: (kk, i)) if mode == 'tn'
              else pl.BlockSpec((tm, tk), lambda i, j, kk: (i, kk)))
    b_spec = (pl.BlockSpec((tn, tk), lambda i, j, kk: (j, kk)) if mode == 'nt'
              else pl.BlockSpec((tk, tn), lambda i, j, kk: (kk, j)))
    o_spec = pl.BlockSpec((tm, tn), lambda i, j, kk: (i, j))
    operands, in_specs = [a, b], [a_spec, b_spec]
    if add is not None:
        operands.append(add)
        in_specs.append(o_spec)
    if extra is not None:
        operands.append(extra)
        in_specs.append(o_spec)
    n_in = len(operands)
    if epilogue == 'relu2':
        out_shape = (jax.ShapeDtypeStruct((m, n), BF16), jax.ShapeDtypeStruct((m, n), BF16))
        out_specs = (o_spec, o_spec)
    else:
        out_shape = jax.ShapeDtypeStruct((m, n), out_dtype)
        out_specs = o_spec

    def body(*refs):
        a_ref, b_ref = refs[0], refs[1]
        acc_ref = refs[-1]
        outs = refs[n_in:-1]
        kk = pl.program_id(2)

        @pl.when(kk == 0)
        def _():
            acc_ref[...] = jnp.zeros_like(acc_ref)

        acc_ref[...] += lax.dot_general(a_ref[...].astype(BF16), b_ref[...].astype(BF16), dims,
                                        preferred_element_type=F32)

        @pl.when(kk == nk - 1)
        def _():
            p = acc_ref[...]
            if add is not None:
                p = p + refs[2][...]
            if epilogue == 'relu2':
                ur = jnp.maximum(p, 0.0)
                outs[0][...] = ur.astype(BF16)
                outs[1][...] = (ur * ur).astype(BF16)
            elif epilogue == 'mul2x':
                outs[0][...] = (p * 2.0 * refs[n_in - 1][...].astype(F32)).astype(out_dtype)
            else:
                outs[0][...] = p.astype(out_dtype)

    bb = (_nbytes((tm, tk), a.dtype) + _nbytes((tk, tn), b.dtype) + 4 * _nbytes((tm, tn), F32))
    return _pcall(body, name=name, grid=(m // tm, n // tn, nk), in_specs=in_specs, out_specs=out_specs,
                  out_shape=out_shape, scratch_shapes=[pltpu.VMEM((tm, tn), F32)],
                  semantics=("parallel", "parallel", "arbitrary"), block_bytes=bb)(*operands)


def _row_tile(s, d, target_bytes=1 << 20):
    return _tile(s, max(SUBLANES, target_bytes // (4 * d)), SUBLANES)


def _rms_fwd(x, gain, name):
    s, d = x.shape
    tr = _row_tile(s, d)

    def body(x_ref, g_ref, h_ref):
        xv = x_ref[...]
        r = lax.rsqrt(jnp.mean(xv * xv, axis=-1, keepdims=True) + RMS_EPS)
        h_ref[...] = (xv * r * g_ref[...]).astype(BF16)

    row = pl.BlockSpec((tr, d), lambda i: (i, 0))
    return _pcall(body, name=name, grid=(s // tr,), in_specs=[row, pl.BlockSpec((1, d), lambda i: (0, 0))],
                  out_specs=row, out_shape=jax.ShapeDtypeStruct((s, d), BF16), semantics=("parallel",),
                  block_bytes=2 * tr * d * 4)(x, gain.reshape(1, d))


def _rms_bwd(x, gain, dh, dres, name):
    s, d = x.shape
    tr = _row_tile(s, d, 1 << 19)

    def body(x_ref, g_ref, dh_ref, dres_ref, dx_ref, dg_ref):
        xv = x_ref[...]
        r = lax.rsqrt(jnp.mean(xv * xv, axis=-1, keepdims=True) + RMS_EPS)
        xh = xv * r
        dhv = dh_ref[...]
        dxh = dhv * g_ref[...]
        dx_ref[...] = dres_ref[...] + r * (dxh - xh * jnp.mean(dxh * xh, axis=-1, keepdims=True))

        @pl.when(pl.program_id(0) == 0)
        def _():
            dg_ref[...] = jnp.zeros_like(dg_ref)

        dg_ref[...] += jnp.sum(dhv * xh, axis=0, keepdims=True)

    row = pl.BlockSpec((tr, d), lambda i: (i, 0))
    vec = pl.BlockSpec((1, d), lambda i: (0, 0))
    return _pcall(body, name=name, grid=(s // tr,), in_specs=[row, vec, row, row], out_specs=(row, vec),
                  out_shape=(jax.ShapeDtypeStruct((s, d), F32), jax.ShapeDtypeStruct((1, d), F32)),
                  semantics=("arbitrary",), block_bytes=4 * tr * d * 4)(x, gain.reshape(1, d), dh, dres)


def _loss_head(x, gain, target, name):
    s, d = x.shape
    tr = _row_tile(s, d, 1 << 19)

    def body(x_ref, g_ref, t_ref, loss_ref, dx_ref, dg_ref):
        xv = x_ref[...]
        r = lax.rsqrt(jnp.mean(xv * xv, axis=-1, keepdims=True) + RMS_EPS)
        xh = xv * r
        gv = g_ref[...]
        err = xh * gv - t_ref[...]
        dy = err * (1.0 / d)
        dxh = dy * gv
        dx_ref[...] = r * (dxh - xh * jnp.mean(dxh * xh, axis=-1, keepdims=True))

        @pl.when(pl.program_id(0) == 0)
        def _():
            dg_ref[...] = jnp.zeros_like(dg_ref)
            loss_ref[...] = jnp.zeros_like(loss_ref)

        dg_ref[...] += jnp.sum(dy * xh, axis=0, keepdims=True)
        part = jnp.sum(jnp.sum(err * err, axis=-1, keepdims=True), axis=0, keepdims=True) * (0.5 / d)
        loss_ref[...] += jnp.broadcast_to(part, loss_ref.shape)

    row = pl.BlockSpec((tr, d), lambda i: (i, 0))
    vec = pl.BlockSpec((1, d), lambda i: (0, 0))
    lspec = pl.BlockSpec((SUBLANES, LANES), lambda i: (0, 0))
    return _pcall(body, name=name, grid=(s // tr,), in_specs=[row, vec, row], out_specs=(lspec, row, vec),
                  out_shape=(jax.ShapeDtypeStruct((SUBLANES, LANES), F32), jax.ShapeDtypeStruct((s, d), F32),
                             jax.ShapeDtypeStruct((1, d), F32)),
                  semantics=("arbitrary",), block_bytes=3 * tr * d * 4)(x, gain.reshape(1, d), target)


def _shift_down(xc, xp, s):
    tr = xc.shape[0]
    r = pltpu.roll(xc, s, 0)
    p = pltpu.roll(xp, s, 0)
    row8 = lax.broadcasted_iota(jnp.int32, (SUBLANES, xc.shape[1]), 0)
    head = jnp.where(row8 < s, p, r[:SUBLANES])
    if tr == SUBLANES:
        return head
    return jnp.concatenate([head, r[SUBLANES:]], axis=0)


def _shift_up(yc, yn, s):
    tr = yc.shape[0]
    u = pltpu.roll(yc, tr - s, 0)
    n = pltpu.roll(yn, SUBLANES - s, 0)
    row8 = lax.broadcasted_iota(jnp.int32, (SUBLANES, yc.shape[1]), 0)
    tail = jnp.where(row8 >= SUBLANES - s, n, u[tr - SUBLANES:])
    if tr == SUBLANES:
        return tail
    return jnp.concatenate([u[:tr - SUBLANES], tail], axis=0)


def _conv_apply(xc, xp, w):
    y = xc * w[CONV_WIDTH - 1:CONV_WIDTH, :]
    for s in range(1, CONV_WIDTH):
        y = y + _shift_down(xc, xp, s) * w[CONV_WIDTH - 1 - s:CONV_WIDTH - s, :]
    return y


def _halo_specs(tr, col_of):
    per = tr // SUBLANES
    cur = pl.BlockSpec((tr, LANES), lambda j, i: (i, col_of(j)))
    prev = pl.BlockSpec((SUBLANES, LANES), lambda j, i: (jnp.maximum(i * per - 1, 0), col_of(j)))
    return cur, prev


def _conv_bias_fwd(x_arr, x_col0, w, bias, name):
    s = x_arr.shape[0]
    ncb = w.shape[1] // LANES
    tr = _tile(s, 512, SUBLANES)

    def body(cur_ref, prev_ref, w_ref, b_ref, o_ref):
        i = pl.program_id(1)
        xp = prev_ref[...] * (i > 0).astype(F32)
        o_ref[...] = _conv_apply(cur_ref[...], xp, w_ref[...]) + b_ref[...]

    cur, prev = _halo_specs(tr, lambda j: x_col0 + j)
    return _pcall(body, name=name, grid=(ncb, s // tr),
                  in_specs=[cur, prev, pl.BlockSpec((CONV_WIDTH, LANES), lambda j, i: (0, j)),
                            pl.BlockSpec((1, LANES), lambda j, i: (0, j))],
                  out_specs=pl.BlockSpec((tr, LANES), lambda j, i: (i, j)),
                  out_shape=jax.ShapeDtypeStruct((s, w.shape[1]), F32), semantics=("parallel", "parallel"),
                  block_bytes=3 * tr * LANES * 4)(x_arr, x_arr, w, bias.reshape(1, -1))


def _conv_bwd(dy, x_arr, x_col0, w, name):
    s, c = dy.shape
    ncb = c // LANES
    tr = _tile(s, 512, SUBLANES)
    per = tr // SUBLANES
    ni = s // tr

    def body(dy_ref, dyn_ref, cur_ref, prev_ref, w_ref, dx_ref, dw_ref, db_ref):
        i = pl.program_id(1)
        dyv = dy_ref[...]
        dn = dyn_ref[...] * (i < ni - 1).astype(F32)
        xc = cur_ref[...]
        xp = prev_ref[...] * (i > 0).astype(F32)
        wv = w_ref[...]

        @pl.when(i == 0)
        def _():
            dw_ref[...] = jnp.zeros_like(dw_ref)
            db_ref[...] = jnp.zeros_like(db_ref)

        dx = dyv * wv[CONV_WIDTH - 1:CONV_WIDTH, :]
        dw_ref[CONV_WIDTH - 1:CONV_WIDTH, :] += jnp.sum(dyv * xc, axis=0, keepdims=True)
        for sh in range(1, CONV_WIDTH):
            j = CONV_WIDTH - 1 - sh
            dx = dx + _shift_up(dyv, dn, sh) * wv[j:j + 1, :]
            dw_ref[j:j + 1, :] += jnp.sum(dyv * _shift_down(xc, xp, sh), axis=0, keepdims=True)
        dx_ref[...] = dx
        db_ref[...] += jnp.sum(dyv, axis=0, keepdims=True)

    cur, prev = _halo_specs(tr, lambda j: x_col0 + j)
    dcur = pl.BlockSpec((tr, LANES), lambda j, i: (i, j))
    dnext = pl.BlockSpec((SUBLANES, LANES), lambda j, i: (jnp.minimum((i + 1) * per, s // SUBLANES - 1), j))
    return _pcall(body, name=name, grid=(ncb, ni),
                  in_specs=[dcur, dnext, cur, prev, pl.BlockSpec((CONV_WIDTH, LANES), lambda j, i: (0, j))],
                  out_specs=(dcur, pl.BlockSpec((CONV_WIDTH, LANES), lambda j, i: (0, j)),
                             pl.BlockSpec((1, LANES), lambda j, i: (0, j))),
                  out_shape=(jax.ShapeDtypeStruct((s, c), F32), jax.ShapeDtypeStruct((CONV_WIDTH, c), F32),
                             jax.ShapeDtypeStruct((1, c), F32)),
                  semantics=("parallel", "arbitrary"), block_bytes=4 * tr * LANES * 4)(dy, dy, x_arr, x_arr, w)


def _gdn_pre_fwd(proj, conv_w, heads, name):
    s = proj.shape[0]
    ncb = conv_w.shape[1] // LANES
    tr = _tile(s, 512, SUBLANES)
    qscale = float(LANES) ** -0.5

    def body(cur_ref, prev_ref, w_ref, o_ref):
        j, i = pl.program_id(0), pl.program_id(1)
        xp = prev_ref[...] * (i > 0).astype(F32)
        cv = _conv_apply(cur_ref[...], xp, w_ref[...])
        sv = cv * _sigmoid(cv)
        nrm = lax.rsqrt(jnp.sum(sv * sv, axis=-1, keepdims=True) + L2_EPS)
        scale = jnp.where(j < heads, qscale, 1.0)
        o_ref[...] = jnp.where(j < 2 * heads, sv * nrm * scale, sv)

    cur, prev = _halo_specs(tr, lambda j: j)
    return _pcall(body, name=name, grid=(ncb, s // tr),
                  in_specs=[cur, prev, pl.BlockSpec((CONV_WIDTH, LANES), lambda j, i: (0, j))],
                  out_specs=pl.BlockSpec((tr, LANES), lambda j, i: (i, j)),
                  out_shape=jax.ShapeDtypeStruct((s, conv_w.shape[1]), F32), semantics=("parallel", "parallel"),
                  block_bytes=3 * tr * LANES * 4)(proj, proj, conv_w)


def _gdn_pre_bwd(proj, conv_w, dqkv, heads, name):
    s = proj.shape[0]
    ncb = conv_w.shape[1] // LANES
    tr = _tile(s, 512, SUBLANES)
    qscale = float(LANES) ** -0.5

    def body(cur_ref, prev_ref, w_ref, d_ref, o_ref):
        j, i = pl.program_id(0), pl.program_id(1)
        xp = prev_ref[...] * (i > 0).astype(F32)
        cv = _conv_apply(cur_ref[...], xp, w_ref[...])
        sg = _sigmoid(cv)
        sv = cv * sg
        nrm = lax.rsqrt(jnp.sum(sv * sv, axis=-1, keepdims=True) + L2_EPS)
        dv = d_ref[...]
        scale = jnp.where(j < heads, qscale, 1.0)
        dsn = scale * nrm * (dv - sv * (nrm * nrm) * jnp.sum(dv * sv, axis=-1, keepdims=True))
        ds = jnp.where(j < 2 * heads, dsn, dv)
        o_ref[...] = ds * (sg * (1.0 + cv * (1.0 - sg)))

    cur, prev = _halo_specs(tr, lambda j: j)
    blk = pl.BlockSpec((tr, LANES), lambda j, i: (i, j))
    return _pcall(body, name=name, grid=(ncb, s // tr),
                  in_specs=[cur, prev, pl.BlockSpec((CONV_WIDTH, LANES), lambda j, i: (0, j)), blk],
                  out_specs=blk, out_shape=jax.ShapeDtypeStruct((s, conv_w.shape[1]), F32),
                  semantics=("parallel", "parallel"), block_bytes=4 * tr * LANES * 4)(proj, proj, conv_w, dqkv)


def _tri_inverse(a_strict):
    c = a_strict.shape[0]
    ri = lax.broadcasted_iota(jnp.int32, (c, c), 0)
    ci = lax.broadcasted_iota(jnp.int32, (c, c), 1)
    p = jnp.where(ri == ci, 1.0, 0.0) - a_strict
    xp = a_strict
    span = 1
    while 2 * span < c:
        xp = _dot(xp, xp)
        p = p + _dot(p, xp)
        span *= 2
    return p


def _gdn_chunk_terms(q, k, v, ab, alog, dtb, head, heads):
    c = q.shape[0]
    lane = lax.broadcasted_iota(jnp.int32, (1, LANES), 1)
    a_col = jnp.sum(jnp.where(lane == head, ab, 0.0), axis=1, keepdims=True)
    b_col = jnp.sum(jnp.where(lane == heads + head, ab, 0.0), axis=1, keepdims=True)
    alog_h = jnp.sum(jnp.where(lane == head, alog, 0.0), axis=1, keepdims=True)
    dtb_h = jnp.sum(jnp.where(lane == head, dtb, 0.0), axis=1, keepdims=True)
    pre = a_col + dtb_h
    neg_ea = -jnp.exp(alog_h)
    g = neg_ea * _softplus(pre)
    beta = _sigmoid(b_col)
    ri = lax.broadcasted_iota(jnp.int32, (c, c), 0)
    ci = lax.broadcasted_iota(jnp.int32, (c, c), 1)
    gc_col = jnp.sum(jnp.where(ci <= ri, jnp.sum(jnp.where(ri == ci, g, 0.0), axis=0, keepdims=True), 0.0),
                     axis=1, keepdims=True)
    gc_row = jnp.sum(jnp.where(ri <= ci, g, 0.0), axis=0, keepdims=True)
    causal = ri >= ci
    strict = ri > ci
    decay = jnp.where(causal, jnp.exp(jnp.where(causal, gc_col - gc_row, 0.0)), 0.0)
    g_last = jnp.sum(g, axis=0, keepdims=True)
    egc = jnp.exp(gc_col)
    ekl = jnp.exp(g_last - gc_col)
    kb = k * beta
    vb = v * beta
    kk = _dot_nt(kb, k)
    a_strict = jnp.where(strict, kk * decay, 0.0)
    return dict(pre=pre, neg_ea=neg_ea, g=g, beta=beta, ri=ri, ci=ci, causal=causal, strict=strict, decay=decay,
                g_last=g_last, egc=egc, ekl=ekl, kb=kb, vb=vb, kk=kk, a_strict=a_strict, lane=lane)


def _gdn_fwd(qkv, proj, alog, dtb, heads, name):
    s = qkv.shape[0]
    c = GDN_CHUNK
    nc = s // c
    ab_blk = proj.shape[1] // LANES - 1

    def body(q_ref, k_ref, v_ref, ab_ref, alog_ref, dtb_ref, o_ref, t_ref, s0_ref, state_ref):
        head, ch = pl.program_id(0), pl.program_id(1)

        @pl.when(ch == 0)
        def _():
            state_ref[...] = jnp.zeros_like(state_ref)

        q, k, v = q_ref[...], k_ref[...], v_ref[...]
        tm = _gdn_chunk_terms(q, k, v, ab_ref[...], alog_ref[...], dtb_ref[...], head, heads)
        t_inv = _tri_inverse(tm['a_strict'])
        u = _dot(t_inv, tm['vb'])
        w = _dot(t_inv, tm['kb'] * tm['egc'])
        qk = jnp.where(tm['causal'], _dot_nt(q, k) * tm['decay'], 0.0)
        st = state_ref[...]
        v_new = u - _dot(w, st)
        o_ref[...] = _dot(q * tm['egc'], st) + _dot(qk, v_new)
        t_ref[...] = t_inv
        s0_ref[...] = st
        state_ref[...] = st * jnp.exp(tm['g_last']) + _dot_tn(k * tm['ekl'], v_new)

    def blk(off):
        return pl.BlockSpec((c, LANES), lambda h, n: (n, off + h))

    vec = pl.BlockSpec((1, LANES), lambda h, n: (0, 0))
    return _pcall(
        body, name=name, grid=(heads, nc),
        in_specs=[blk(0), blk(heads), blk(2 * heads), pl.BlockSpec((c, LANES), lambda h, n: (n, ab_blk)), vec, vec],
        out_specs=(pl.BlockSpec((c, LANES), lambda h, n: (n, h)),
                   pl.BlockSpec((None, None, c, c), lambda h, n: (h, n, 0, 0)),
                   pl.BlockSpec((None, None, LANES, LANES), lambda h, n: (h, n, 0, 0))),
        out_shape=(jax.ShapeDtypeStruct((s, heads * LANES), F32), jax.ShapeDtypeStruct((heads, nc, c, c), F32),
                   jax.ShapeDtypeStruct((heads, nc, LANES, LANES), F32)),
        scratch_shapes=[pltpu.VMEM((LANES, LANES), F32)], semantics=("parallel", "arbitrary"),
        block_bytes=8 * c * LANES * 4 + 2 * LANES * LANES * 4)(qkv, qkv, qkv, proj, alog, dtb)


def _gdn_bwd(qkv, proj, alog, dtb, t_all, s0_all, d_o, heads, name):
    s = qkv.shape[0]
    c = GDN_CHUNK
    nc = s // c
    ab_blk = proj.shape[1] // LANES - 1

    def body(q_ref, k_ref, v_ref, ab_ref, alog_ref, dtb_ref, t_ref, s0_ref, do_ref,
             dq_ref, dk_ref, dv_ref, dgb_ref, ds_ref):
        head, step = pl.program_id(0), pl.program_id(1)

        @pl.when(step == 0)
        def _():
            ds_ref[...] = jnp.zeros_like(ds_ref)

        q, k, v = q_ref[...], k_ref[...], v_ref[...]
        tm = _gdn_chunk_terms(q, k, v, ab_ref[...], alog_ref[...], dtb_ref[...], head, heads)
        ri, ci, causal, strict, decay = tm['ri'], tm['ci'], tm['causal'], tm['strict'], tm['decay']
        egc, ekl, kb, vb, beta, g_last = tm['egc'], tm['ekl'], tm['kb'], tm['vb'], tm['beta'], tm['g_last']
        t_inv = t_ref[...]
        st = s0_ref[...]
        do = do_ref[...]
        ds_next = ds_ref[...]
        kbg = kb * egc
        u = _dot(t_inv, vb)
        w = _dot(t_inv, kbg)
        qkm = _dot_nt(q, k)
        qk = jnp.where(causal, qkm * decay, 0.0)
        v_new = u - _dot(w, st)
        qd = q * egc
        kd = k * ekl
        e_last = jnp.exp(g_last)

        dqd = _dot_nt(do, st)
        dqk = jnp.where(causal, _dot_nt(do, v_new), 0.0)
        dvn = _dot_tn(qk, do) + _dot(kd, ds_next)
        dkd = _dot_nt(v_new, ds_next)
        dgl = jnp.sum(jnp.sum(st * ds_next, axis=1, keepdims=True), axis=0, keepdims=True) * e_last
        dw = -_dot_nt(dvn, st)
        ds_ref[...] = _dot_tn(qd, do) + e_last * ds_next - _dot_tn(w, dvn)
        dt = _dot_nt(dvn, vb) + _dot_nt(dw, kbg)
        dvb = _dot_tn(t_inv, dvn)
        dkbg = _dot_tn(t_inv, dw)
        da_m = jnp.where(strict, -_dot_tn(t_inv, _dot_nt(dt, t_inv)), 0.0)
        dad = da_m * decay
        dkb = _dot(dad, k) + dkbg * egc
        dqkd = dqk * decay
        dq = _dot(dqkd, k) + dqd * egc
        dk = _dot_tn(dad, kb) + _dot_tn(dqkd, q) + dkd * ekl + dkb * beta
        e_mat = (da_m * tm['kk'] + dqk * qkm) * decay
        s_kd = jnp.sum(dkd * kd, axis=1, keepdims=True)
        dgl = dgl + jnp.sum(s_kd, axis=0, keepdims=True)
        col_sum = jnp.sum(e_mat, axis=0, keepdims=True)
        col_sum_c = jnp.sum(jnp.where(ri == ci, col_sum, 0.0), axis=1, keepdims=True)
        dgc = (jnp.sum(e_mat, axis=1, keepdims=True) - col_sum_c + jnp.sum(dqd * qd, axis=1, keepdims=True)
               - s_kd + jnp.sum(dkbg * kbg, axis=1, keepdims=True))
        row_c = lax.broadcasted_iota(jnp.int32, (c, 1), 0)
        dgc = dgc + jnp.where(row_c == c - 1, dgl, 0.0)
        dgc_row = jnp.sum(jnp.where(ri == ci, dgc, 0.0), axis=0, keepdims=True)
        dg = jnp.sum(jnp.where(ci >= ri, dgc_row, 0.0), axis=1, keepdims=True)
        dbeta = jnp.sum(dkb * k, axis=1, keepdims=True) + jnp.sum(dvb * v, axis=1, keepdims=True)
        da_pre = dg * tm['neg_ea'] * _sigmoid(tm['pre'])
        db_pre = dbeta * beta * (1.0 - beta)
        lane = tm['lane']
        dq_ref[...] = dq
        dk_ref[...] = dk
        dv_ref[...] = dvb * beta
        dgb_ref[...] = (jnp.where(lane == head, da_pre, 0.0) + jnp.where(lane == heads + head, db_pre, 0.0)
                        + jnp.where(lane == 2 * heads + head, dg * tm['g'], 0.0))

    def blk(off):
        return pl.BlockSpec((c, LANES), lambda h, n: (nc - 1 - n, off + h))

    vec = pl.BlockSpec((1, LANES), lambda h, n: (0, 0))
    gw = heads * LANES
    dq, dk, dv, dgb = _pcall(
        body, name=name, grid=(heads, nc),
        in_specs=[blk(0), blk(heads), blk(2 * heads), pl.BlockSpec((c, LANES), lambda h, n: (nc - 1 - n, ab_blk)),
                  vec, vec, pl.BlockSpec((None, None, c, c), lambda h, n: (h, nc - 1 - n, 0, 0)),
                  pl.BlockSpec((None, None, LANES, LANES), lambda h, n: (h, nc - 1 - n, 0, 0)), blk(0)],
        out_specs=(blk(0), blk(0), blk(0), blk(0)),
        out_shape=tuple(jax.ShapeDtypeStruct((s, gw), F32) for _ in range(4)),
        scratch_shapes=[pltpu.VMEM((LANES, LANES), F32)], semantics=("parallel", "arbitrary"),
        block_bytes=12 * c * LANES * 4 + 2 * LANES * LANES * 4)(qkv, qkv, qkv, proj, alog, dtb, t_all, s0_all, d_o)
    return dq, dk, dv, dgb


def _gdn_post_fwd(o, proj, z_col0, gain, name):
    s, gw = o.shape
    heads = gw // LANES
    tr = _tile(s, 512, SUBLANES)

    def body(o_ref, z_ref, g_ref, y_ref):
        ov, zv = o_ref[...], z_ref[...]
        r = lax.rsqrt(jnp.mean(ov * ov, axis=-1, keepdims=True) + RMS_EPS)
        y_ref[...] = (ov * r * g_ref[...] * (zv * _sigmoid(zv))).astype(BF16)

    blk = pl.BlockSpec((tr, LANES), lambda i, h: (i, h))
    return _pcall(body, name=name, grid=(s // tr, heads),
                  in_specs=[blk, pl.BlockSpec((tr, LANES), lambda i, h: (i, z_col0 + h)),
                            pl.BlockSpec((1, LANES), lambda i, h: (0, 0))],
                  out_specs=blk, out_shape=jax.ShapeDtypeStruct((s, gw), BF16), semantics=("parallel", "parallel"),
                  block_bytes=3 * tr * LANES * 4)(o, proj, gain.reshape(1, LANES))


def _gdn_post_bwd(o, proj, z_col0, gain, dy, name):
    s, gw = o.shape
    heads = gw // LANES
    tr = _tile(s, 512, SUBLANES)

    def body(o_ref, z_ref, g_ref, dy_ref, do_ref, dz_ref, dg_ref):
        ov, zv, gv, dyv = o_ref[...], z_ref[...], g_ref[...], dy_ref[...]
        r = lax.rsqrt(jnp.mean(ov * ov, axis=-1, keepdims=True) + RMS_EPS)
        nv = ov * r
        sg = _sigmoid(zv)
        sz = zv * sg
        dn = dyv * gv * sz
        do_ref[...] = r * (dn - nv * jnp.mean(dn * nv, axis=-1, keepdims=True))
        dz_ref[...] = dyv * nv * gv * (sg * (1.0 + zv * (1.0 - sg)))

        @pl.when((pl.program_id(0) == 0) & (pl.program_id(1) == 0))
        def _():
            dg_ref[...] = jnp.zeros_like(dg_ref)

        dg_ref[...] += jnp.sum(dyv * nv * sz, axis=0, keepdims=True)

    blk = pl.BlockSpec((tr, LANES), lambda i, h: (i, h))
    vec = pl.BlockSpec((1, LANES), lambda i, h: (0, 0))
    return _pcall(body, name=name, grid=(s // tr, heads),
                  in_specs=[blk, pl.BlockSpec((tr, LANES), lambda i, h: (i, z_col0 + h)), vec, blk],
                  out_specs=(blk, blk, vec),
                  out_shape=(jax.ShapeDtypeStruct((s, gw), F32), jax.ShapeDtypeStruct((s, gw), F32),
                             jax.ShapeDtypeStruct((1, LANES), F32)),
                  semantics=("arbitrary", "arbitrary"), block_bytes=6 * tr * LANES * 4)(
                      o, proj, gain.reshape(1, LANES), dy)


def _dab_reduce(dgb, name):
    s, gw = dgb.shape
    heads = gw // LANES
    tr = _tile(s, 512, SUBLANES)

    def body(d_ref, o_ref, cs_ref):
        acc = d_ref[:, 0:LANES]
        for h in range(1, heads):
            acc = acc + d_ref[:, h * LANES:(h + 1) * LANES]
        o_ref[...] = acc

        @pl.when(pl.program_id(0) == 0)
        def _():
            cs_ref[...] = jnp.zeros_like(cs_ref)

        cs_ref[...] += jnp.sum(acc, axis=0, keepdims=True)

    return _pcall(body, name=name, grid=(s // tr,), in_specs=[pl.BlockSpec((tr, gw), lambda i: (i, 0))],
                  out_specs=(pl.BlockSpec((tr, LANES), lambda i: (i, 0)), pl.BlockSpec((1, LANES), lambda i: (0, 0))),
                  out_shape=(jax.ShapeDtypeStruct((s, LANES), F32), jax.ShapeDtypeStruct((1, LANES), F32)),
                  semantics=("arbitrary",), block_bytes=tr * gw * 4)(dgb)


def _lru_gates(xc, wa, wx, ba, bx, lam):
    r = _sigmoid(_dot(xc, wa) + ba)
    ig = _sigmoid(_dot(xc, wx) + bx)
    sp = _softplus(-lam)
    log_a = -LRU_C * r * sp
    a = jnp.exp(log_a)
    e2 = jnp.exp(2.0 * log_a)
    mult = jnp.sqrt(jnp.maximum(1.0 - e2, 0.0))
    return r, ig, sp, a, e2, mult


def _lru_fwd(xc, proj, y_col0, wa, wx, ba, bx, lam, name):
    s, lw = xc.shape
    nb = lw // LANES
    tr = _tile(s, 256, SUBLANES)

    def body(xc_ref, y_ref, wa_ref, wx_ref, ba_ref, bx_ref, lam_ref, h_ref, o_ref, carry_ref):
        @pl.when(pl.program_id(1) == 0)
        def _():
            carry_ref[...] = jnp.zeros_like(carry_ref)

        xv = xc_ref[...]
        _, ig, _, a, _, mult = _lru_gates(xv, wa_ref[...], wx_ref[...], ba_ref[...], bx_ref[...], lam_ref[...])
        b = mult * (ig * xv)
        row = lax.broadcasted_iota(jnp.int32, (tr, LANES), 0)
        sh = 1
        while sh < tr:
            keep = row >= sh
            b = a * jnp.where(keep, pltpu.roll(b, sh, 0), 0.0) + b
            a = a * jnp.where(keep, pltpu.roll(a, sh, 0), 1.0)
            sh *= 2
        hv = a * carry_ref[0:1, :] + b
        h_ref[...] = hv
        carry_ref[...] = jnp.broadcast_to(hv[tr - 1:tr, :], carry_ref.shape)
        gy, _ = _gelu_and_grad(y_ref[...])
        o_ref[...] = (hv * gy).astype(BF16)

    blk = pl.BlockSpec((tr, LANES), lambda n, i: (i, n))
    wspec = pl.BlockSpec((None, LANES, LANES), lambda n, i: (n, 0, 0))
    vec = pl.BlockSpec((1, LANES), lambda n, i: (0, n))
    return _pcall(body, name=name, grid=(nb, s // tr),
                  in_specs=[blk, pl.BlockSpec((tr, LANES), lambda n, i: (i, y_col0 + n)), wspec, wspec, vec, vec, vec],
                  out_specs=(blk, blk),
                  out_shape=(jax.ShapeDtypeStruct((s, lw), F32), jax.ShapeDtypeStruct((s, lw), BF16)),
                  scratch_shapes=[pltpu.VMEM((SUBLANES, LANES), F32)], semantics=("parallel", "arbitrary"),
                  block_bytes=8 * tr * LANES * 4)(xc, proj, wa, wx, ba.reshape(1, lw), bx.reshape(1, lw),
                                                  lam.reshape(1, lw))


def _lru_bwd(d_out, xc, hseq, proj, y_col0, wa, wx, ba, bx, lam, name):
    s, lw = xc.shape
    nb = lw // LANES
    tr = _tile(s, 256, SUBLANES)
    per = tr // SUBLANES
    ni = s // tr
    nrow8 = s // SUBLANES

    def body(do_ref, xc_ref, xn_ref, h_ref, hp_ref, y_ref, wa_ref, wx_ref, ba_ref, bx_ref, lam_ref,
             dxc_ref, dy_ref, dwa_ref, dwx_ref, dba_ref, dbx_ref, dlam_ref, carry_ref):
        step = pl.program_id(1)
        tile = ni - 1 - step

        @pl.when(step == 0)
        def _():
            carry_ref[...] = jnp.zeros_like(carry_ref)
            dwa_ref[...] = jnp.zeros_like(dwa_ref)
            dwx_ref[...] = jnp.zeros_like(dwx_ref)
            dba_ref[...] = jnp.zeros_like(dba_ref)
            dbx_ref[...] = jnp.zeros_like(dbx_ref)
            dlam_ref[...] = jnp.zeros_like(dlam_ref)

        wav, wxv, bav, bxv, lamv = wa_ref[...], wx_ref[...], ba_ref[...], bx_ref[...], lam_ref[...]
        xv = xc_ref[...]
        r, ig, sp, a, e2, mult = _lru_gates(xv, wav, wxv, bav, bxv, lamv)
        a_next = _lru_gates(xn_ref[...], wav, wxv, bav, bxv, lamv)[3] * (tile < ni - 1).astype(F32)
        hv = h_ref[...]
        h_prev = _shift_down(hv, hp_ref[...] * (tile > 0).astype(F32), 1)
        yv = y_ref[...]
        gy, dgy = _gelu_and_grad(yv)
        dov = do_ref[...]
        dy_ref[...] = dov * hv * dgy
        coef = _shift_up(a, a_next, 1)
        bb = dov * gy
        row = lax.broadcasted_iota(jnp.int32, (tr, LANES), 0)
        sh = 1
        while sh < tr:
            keep = row < tr - sh
            bb = coef * jnp.where(keep, pltpu.roll(bb, tr - sh, 0), 0.0) + bb
            coef = coef * jnp.where(keep, pltpu.roll(coef, tr - sh, 0), 1.0)
            sh *= 2
        lam_t = coef * carry_ref[0:1, :] + bb
        carry_ref[...] = jnp.broadcast_to(lam_t[0:1, :], carry_ref.shape)
        d_a = lam_t * h_prev
        d_mult = lam_t * (ig * xv)
        d_ix = lam_t * mult
        d_la = d_a * a - d_mult * e2 / jnp.maximum(mult, 1e-30)
        d_r = d_la * (-LRU_C * sp)
        dlam_ref[...] += jnp.sum(d_la * (LRU_C * r) * _sigmoid(-lamv), axis=0, keepdims=True)
        d_pa = d_r * r * (1.0 - r)
        d_px = (d_ix * xv) * ig * (1.0 - ig)
        dxc_ref[...] = d_ix * ig + _dot_nt(d_pa, wav) + _dot_nt(d_px, wxv)
        dwa_ref[...] += _dot_tn(xv, d_pa)
        dwx_ref[...] += _dot_tn(xv, d_px)
        dba_ref[...] += jnp.sum(d_pa, axis=0, keepdims=True)
        dbx_ref[...] += jnp.sum(d_px, axis=0, keepdims=True)

    blk = pl.BlockSpec((tr, LANES), lambda n, i: (ni - 1 - i, n))
    nxt = pl.BlockSpec((SUBLANES, LANES), lambda n, i: (jnp.minimum((ni - i) * per, nrow8 - 1), n))
    prv = pl.BlockSpec((SUBLANES, LANES), lambda n, i: (jnp.maximum((ni - 1 - i) * per - 1, 0), n))
    wspec = pl.BlockSpec((None, LANES, LANES), lambda n, i: (n, 0, 0))
    vec = pl.BlockSpec((1, LANES), lambda n, i: (0, n))
    return _pcall(
        body, name=name, grid=(nb, ni),
        in_specs=[blk, blk, nxt, blk, prv, pl.BlockSpec((tr, LANES), lambda n, i: (ni - 1 - i, y_col0 + n)),
                  wspec, wspec, vec, vec, vec],
        out_specs=(blk, blk, wspec, wspec, vec, vec, vec),
        out_shape=(jax.ShapeDtypeStruct((s, lw), F32), jax.ShapeDtypeStruct((s, lw), F32),
                   jax.ShapeDtypeStruct((nb, LANES, LANES), F32), jax.ShapeDtypeStruct((nb, LANES, LANES), F32),
                   jax.ShapeDtypeStruct((1, lw), F32), jax.ShapeDtypeStruct((1, lw), F32),
                   jax.ShapeDtypeStruct((1, lw), F32)),
        scratch_shapes=[pltpu.VMEM((SUBLANES, LANES), F32)], semantics=("parallel", "arbitrary"),
        block_bytes=12 * tr * LANES * 4)(d_out, xc, xc, hseq, hseq, proj, wa, wx, ba.reshape(1, lw),
                                         bx.reshape(1, lw), lam.reshape(1, lw))


def _merge_fwd(proj, gg_col0, gl_col0, bg, bl, name):
    s, d = bg.shape
    tr, tc = _tile(s, 256, SUBLANES), _tile(d, 1024)
    cb = tc // LANES

    def body(gg_ref, gl_ref, bg_ref, bl_ref, o_ref):
        o_ref[...] = (_sigmoid(gg_ref[...]) * bg_ref[...] + _sigmoid(gl_ref[...]) * bl_ref[...]).astype(BF16)

    blk = pl.BlockSpec((tr, tc), lambda i, j: (i, j))
    return _pcall(body, name=name, grid=(s // tr, d // tc),
                  in_specs=[pl.BlockSpec((tr, tc), lambda i, j: (i, gg_col0 // cb + j)),
                            pl.BlockSpec((tr, tc), lambda i, j: (i, gl_col0 // cb + j)), blk, blk],
                  out_specs=blk, out_shape=jax.ShapeDtypeStruct((s, d), BF16), semantics=("parallel", "parallel"),
                  block_bytes=5 * tr * tc * 4)(proj, proj, bg, bl)


def _merge_bwd(proj, gg_col0, gl_col0, bg, bl, dm, name):
    s, d = bg.shape
    tr, tc = _tile(s, 256, SUBLANES), _tile(d, 1024)
    cb = tc // LANES

    def body(gg_ref, gl_ref, bg_ref, bl_ref, dm_ref, dgg_ref, dgl_ref, dbg_ref, dbl_ref):
        dmv = dm_ref[...]
        sg, sl = _sigmoid(gg_ref[...]), _sigmoid(gl_ref[...])
        dgg_ref[...] = (dmv * bg_ref[...] * sg * (1.0 - sg)).astype(BF16)
        dgl_ref[...] = (dmv * bl_ref[...] * sl * (1.0 - sl)).astype(BF16)
        dbg_ref[...] = (dmv * sg).astype(BF16)
        dbl_ref[...] = (dmv * sl).astype(BF16)

    blk = pl.BlockSpec((tr, tc), lambda i, j: (i, j))
    sh = jax.ShapeDtypeStruct((s, d), BF16)
    return _pcall(body, name=name, grid=(s // tr, d // tc),
                  in_specs=[pl.BlockSpec((tr, tc), lambda i, j: (i, gg_col0 // cb + j)),
                            pl.BlockSpec((tr, tc), lambda i, j: (i, gl_col0 // cb + j)), blk, blk, blk],
                  out_specs=(blk, blk, blk, blk), out_shape=(sh, sh, sh, sh), semantics=("parallel", "parallel"),
                  block_bytes=8 * tr * tc * 4)(proj, proj, bg, bl, dm)


def _as2d(a, lead=0):
    return a.reshape(a.shape[:lead] + (-1, a.shape[-1]))


def _sum_slots(slots, name):
    n, r, c = slots.shape
    tr = _tile(r, max(2 * SUBLANES, (1 << 19) // (c * 4)), 2 * SUBLANES)

    def body(s_ref, o_ref):
        acc = s_ref[0].astype(F32)
        for q in range(1, n):
            acc = acc + s_ref[q].astype(F32)
        o_ref[...] = acc

    return _pcall(body, name=name, grid=(r // tr,), in_specs=[pl.BlockSpec((n, tr, c), lambda i: (0, i, 0))],
                  out_specs=pl.BlockSpec((tr, c), lambda i: (i, 0)), out_shape=jax.ShapeDtypeStruct((r, c), F32),
                  semantics=("parallel",), block_bytes=(n + 1) * tr * c * 4)(slots)


def _adamw(w, g_parts, m, v, name):
    r, c = w.shape
    np_ = len(g_parts)
    tr = _tile(r, max(SUBLANES, (1 << 20) // (c * 4)), SUBLANES)
    c1 = 1.0 - ADAM_B1 ** ADAM_STEP
    c2 = 1.0 - ADAM_B2 ** ADAM_STEP

    def body(*refs):
        w_ref, m_ref, v_ref = refs[0], refs[1 + np_], refs[2 + np_]
        g_ref, d_ref, nm_ref, nv_ref = refs[3 + np_:]
        g = refs[1][...]
        for p in range(1, np_):
            g = g + refs[1 + p][...]
        nm = ADAM_B1 * m_ref[...] + (1.0 - ADAM_B1) * g
        nv = ADAM_B2 * v_ref[...] + (1.0 - ADAM_B2) * (g * g)
        g_ref[...] = g
        nm_ref[...] = nm
        nv_ref[...] = nv
        d_ref[...] = -ADAM_LR * ((nm / c1) / (jnp.sqrt(nv / c2) + ADAM_EPS) + ADAM_WD * w_ref[...])

    blk = pl.BlockSpec((tr, c), lambda i: (i, 0))
    sh = jax.ShapeDtypeStruct((r, c), F32)
    return _pcall(body, name=name, grid=(r // tr,), in_specs=[blk] * (3 + np_), out_specs=(blk,) * 4,
                  out_shape=(sh,) * 4, semantics=("parallel",), block_bytes=(7 + np_) * tr * c * 4)(
                      w, *g_parts, m, v)


HBM_SPEC = pl.BlockSpec(memory_space=pltpu.HBM)


def _chip_exchange(arrs, scatter, name):
    n = len(arrs)
    shard_shapes = [a.shape[1:] if scatter else a.shape for a in arrs]

    def body(*refs):
        ins, outs = refs[:n], refs[n:2 * n]
        send_sems, recv_sems, local_sems = refs[2 * n:]
        x, y, c = lax.axis_index("x"), lax.axis_index("y"), lax.axis_index("c")
        me = 2 * x + y
        chips = [(1 - x, y), (x, 1 - y), (1 - x, 1 - y)]

        def remote(t, j, dst_slot):
            px, py = chips[j]
            src = ins[t].at[2 * px + py] if scatter else ins[t]
            return pltpu.make_async_remote_copy(
                src_ref=src, dst_ref=outs[t].at[dst_slot], send_sem=send_sems.at[3 * t + j],
                recv_sem=recv_sems.at[3 * t + j], device_id=(px, py, c), device_id_type=pl.DeviceIdType.MESH)

        local = []
        sends = []
        for t in range(n):
            lc = pltpu.make_async_copy(ins[t].at[me] if scatter else ins[t], outs[t].at[me], local_sems.at[t])
            lc.start()
            local.append(lc)
            for j in range(3):
                cp = remote(t, j, me)
                cp.start()
                sends.append(cp)
        for t in range(n):
            for j in range(3):
                px, py = chips[j]
                remote(t, j, 2 * px + py).wait_recv()
        for cp in sends:
            cp.wait_send()
        for lc in local:
            lc.wait()

    out_shape = tuple(jax.ShapeDtypeStruct((N_CHIPS,) + tuple(sh), a.dtype) for sh, a in zip(shard_shapes, arrs))
    return pl.pallas_call(
        body, name=name, in_specs=[HBM_SPEC] * n, out_specs=(HBM_SPEC,) * n, out_shape=out_shape,
        scratch_shapes=[pltpu.SemaphoreType.DMA((3 * n,)), pltpu.SemaphoreType.DMA((3 * n,)),
                        pltpu.SemaphoreType.DMA((n,))])(*arrs)


def _sibling_exchange(arrs, name):
    n = len(arrs)

    def body(*refs):
        ins, outs = refs[:n], refs[n:2 * n]
        send_sems, recv_sems = refs[2 * n:]
        sib = (lax.axis_index("x"), lax.axis_index("y"), 1 - lax.axis_index("c"))
        copies = [pltpu.make_async_remote_copy(src_ref=ins[t], dst_ref=outs[t], send_sem=send_sems.at[t],
                                               recv_sem=recv_sems.at[t], device_id=sib,
                                               device_id_type=pl.DeviceIdType.MESH) for t in range(n)]
        for cp in copies:
            cp.start()
        for cp in copies:
            cp.wait_recv()
        for cp in copies:
            cp.wait_send()

    return pl.pallas_call(
        body, name=name, in_specs=[HBM_SPEC] * n, out_specs=(HBM_SPEC,) * n,
        out_shape=tuple(jax.ShapeDtypeStruct(a.shape, a.dtype) for a in arrs),
        scratch_shapes=[pltpu.SemaphoreType.DMA((n,)), pltpu.SemaphoreType.DMA((n,))])(*arrs)


def _all_devices_gather(buf, name):
    def body(in_ref, out_ref, send_sems, recv_sems, local_sem):
        x, y, c = lax.axis_index("x"), lax.axis_index("y"), lax.axis_index("c")
        me = 4 * x + 2 * y + c

        def peer(mask):
            px = 1 - x if mask & 4 else x
            py = 1 - y if mask & 2 else y
            pc = 1 - c if mask & 1 else c
            return px, py, pc

        def remote(mask, dst_slot):
            return pltpu.make_async_remote_copy(
                src_ref=in_ref, dst_ref=out_ref.at[dst_slot], send_sem=send_sems.at[mask - 1],
                recv_sem=recv_sems.at[mask - 1], device_id=peer(mask), device_id_type=pl.DeviceIdType.MESH)

        lc = pltpu.make_async_copy(in_ref, out_ref.at[me], local_sem)
        lc.start()
        sends = [remote(mask, me) for mask in range(1, N_DEVICES)]
        for cp in sends:
            cp.start()
        for mask in range(1, N_DEVICES):
            px, py, pc = peer(mask)
            remote(mask, 4 * px + 2 * py + pc).wait_recv()
        for cp in sends:
            cp.wait_send()
        lc.wait()

    return pl.pallas_call(
        body, name=name, in_specs=[HBM_SPEC], out_specs=HBM_SPEC,
        out_shape=jax.ShapeDtypeStruct((N_DEVICES,) + buf.shape, buf.dtype),
        scratch_shapes=[pltpu.SemaphoreType.DMA((N_DEVICES - 1,)), pltpu.SemaphoreType.DMA((N_DEVICES - 1,)),
                        pltpu.SemaphoreType.DMA])(buf)


def _pad_lanes(vec):
    return jnp.pad(vec.astype(F32), (0, LANES - vec.shape[0])).reshape(1, LANES)


def _layer_fwd(x, wl, dm, tag):
    heads, gw, lw, d = dm['heads'], dm['gw'], dm['lw'], dm['d']
    h = _rms_fwd(x, wl['attn_norm'], f"rms1_fwd{tag}")
    proj = _matmul(h, wl['w_in_p'], mode='nn', name=f"proj{tag}")
    alog, dtb = _pad_lanes(wl['gdn_a_log']), _pad_lanes(wl['gdn_dt_bias'])
    qkv = _gdn_pre_fwd(proj, wl['gdn_conv_w'], heads, f"gdn_pre_fwd{tag}")
    o, t_all, s0_all = _gdn_fwd(qkv, proj, alog, dtb, heads, f"gdn_fwd{tag}")
    o_gdn = _gdn_post_fwd(o, proj, dm['z_blk'], wl['gdn_norm'], f"gdn_post_fwd{tag}")
    xc = _conv_bias_fwd(proj, dm['xb_blk'], wl['lru_conv_w'], wl['lru_conv_b'], f"lru_conv_fwd{tag}")
    hseq, o_lru = _lru_fwd(xc, proj, dm['yb_blk'], wl['lru_w_a'], wl['lru_w_x'], wl['lru_b_a'], wl['lru_b_x'],
                           wl['lru_lambda'], f"lru_fwd{tag}")
    bg = _matmul(o_gdn, wl['w_branch_gdn'], mode='nn', name=f"branch_gdn{tag}")
    bl = _matmul(o_lru, wl['w_branch_lru'], mode='nn', name=f"branch_lru{tag}")
    merged = _merge_fwd(proj, dm['gg_blk'], dm['gl_blk'], bg, bl, f"merge_fwd{tag}")
    x_mid = _matmul(merged, wl['w_out'], mode='nn', add=x, name=f"out_proj{tag}")
    h2 = _rms_fwd(x_mid, wl['mlp_norm'], f"rms2_fwd{tag}")
    ur, act = _matmul(h2, wl['w_up'], mode='nn', epilogue='relu2', name=f"mlp_up{tag}")
    x_out = _matmul(act, wl['w_down'], mode='nn', add=x_mid, name=f"mlp_down{tag}")
    saved = dict(x=x, h=h, proj=proj, qkv=qkv, o=o, t_all=t_all, s0_all=s0_all, o_gdn=o_gdn, xc=xc, hseq=hseq,
                 o_lru=o_lru, bg=bg, bl=bl, merged=merged, x_mid=x_mid, h2=h2, ur=ur, act=act, alog=alog, dtb=dtb)
    return x_out, saved


def _layer_bwd(dx_out, wl, sv, dm, tag):
    heads, gw, lw, d = dm['heads'], dm['gw'], dm['lw'], dm['d']
    g = {}
    du = _matmul(dx_out, wl['w_down'], mode='nt', epilogue='mul2x', extra=sv['ur'], out_dtype=BF16,
                 name=f"d_mlp_act{tag}")
    g['w_down'] = _matmul(sv['act'], dx_out, mode='tn', name=f"dw_down{tag}")
    g['w_up'] = _matmul(sv['h2'], du, mode='tn', name=f"dw_up{tag}")
    dh2 = _matmul(du, wl['w_up'], mode='nt', name=f"d_h2{tag}")
    dx_mid, g['mlp_norm'] = _rms_bwd(sv['x_mid'], wl['mlp_norm'], dh2, dx_out, f"rms2_bwd{tag}")
    dmerged = _matmul(dx_mid, wl['w_out'], mode='nt', name=f"d_merged{tag}")
    g['w_out'] = _matmul(sv['merged'], dx_mid, mode='tn', name=f"dw_out{tag}")
    dgg, dgl, dbg, dbl = _merge_bwd(sv['proj'], dm['gg_blk'], dm['gl_blk'], sv['bg'], sv['bl'], dmerged,
                                    f"merge_bwd{tag}")
    g['w_branch_gdn'] = _matmul(sv['o_gdn'], dbg, mode='tn', name=f"dw_branch_gdn{tag}")
    g['w_branch_lru'] = _matmul(sv['o_lru'], dbl, mode='tn', name=f"dw_branch_lru{tag}")
    do_gdn = _matmul(dbg, wl['w_branch_gdn'], mode='nt', name=f"d_o_gdn{tag}")
    do_lru = _matmul(dbl, wl['w_branch_lru'], mode='nt', name=f"d_o_lru{tag}")
    d_o, dz, dgn = _gdn_post_bwd(sv['o'], sv['proj'], dm['z_blk'], wl['gdn_norm'], do_gdn, f"gdn_post_bwd{tag}")
    g['gdn_norm'] = dgn.reshape(-1)
    dq, dk, dv, dgb = _gdn_bwd(sv['qkv'], sv['proj'], sv['alog'], sv['dtb'], sv['t_all'], sv['s0_all'], d_o, heads,
                               f"gdn_bwd{tag}")
    dqkv_n = jnp.concatenate([dq, dk, dv], axis=1)
    dconv = _gdn_pre_bwd(sv['proj'], wl['gdn_conv_w'], dqkv_n, heads, f"gdn_pre_bwd{tag}")
    dqkv, g['gdn_conv_w'], _ = _conv_bwd(dconv, sv['proj'], 0, wl['gdn_conv_w'], f"gdn_conv_bwd{tag}")
    dab, dab_sum = _dab_reduce(dgb, f"dab_reduce{tag}")
    g['gdn_dt_bias'] = dab_sum[0, :heads]
    g['gdn_a_log'] = dab_sum[0, 2 * heads:3 * heads]
    dxc, dyb, g['lru_w_a'], g['lru_w_x'], dba, dbx, dlam = _lru_bwd(
        do_lru, sv['xc'], sv['hseq'], sv['proj'], dm['yb_blk'], wl['lru_w_a'], wl['lru_w_x'], wl['lru_b_a'],
        wl['lru_b_x'], wl['lru_lambda'], f"lru_bwd{tag}")
    g['lru_b_a'], g['lru_b_x'], g['lru_lambda'] = dba.reshape(-1), dbx.reshape(-1), dlam.reshape(-1)
    dxb, g['lru_conv_w'], dcb = _conv_bwd(dxc, sv['proj'], dm['xb_blk'], wl['lru_conv_w'], f"lru_conv_bwd{tag}")
    g['lru_conv_b'] = dcb.reshape(-1)
    dproj = jnp.concatenate([dqkv.astype(BF16), dz.astype(BF16), dxb.astype(BF16), dyb.astype(BF16), dgg, dgl,
                             dab.astype(BF16)], axis=1)
    g['w_in_p'] = _matmul(sv['h'], dproj, mode='tn', name=f"dw_in{tag}")
    dh = _matmul(dproj, wl['w_in_p'], mode='nt', name=f"d_h{tag}")
    dx_in, g['attn_norm'] = _rms_bwd(sv['x'], wl['attn_norm'], dh, dx_mid, f"rms1_bwd{tag}")
    g['attn_norm'] = g['attn_norm'].reshape(-1)
    g['mlp_norm'] = g['mlp_norm'].reshape(-1)
    return dx_in, g


def _dims(d, heads, lw):
    gw = heads * LANES
    nab = 2 * heads
    blk = dict(z_blk=3 * heads, xb_blk=4 * heads, yb_blk=4 * heads + lw // LANES)
    gg0 = 4 * gw + 2 * lw
    return dict(d=d, heads=heads, gw=gw, lw=lw, nab=nab, gg_blk=gg0 // LANES, gl_blk=(gg0 + d) // LANES,
                main=gg0 + 2 * d, np=gg0 + 2 * d + LANES, **blk)


def _pad_w_in(w_in, dm):
    c0 = 4 * dm['gw']
    nab = dm['nab']
    return jnp.concatenate([w_in[:, :c0], w_in[:, c0 + nab:], w_in[:, c0:c0 + nab],
                            jnp.zeros((w_in.shape[0], LANES - nab), w_in.dtype)], axis=1)


def _unpad_w_in(gp, dm):
    c0 = 4 * dm['gw']
    nab = dm['nab']
    main = dm['main']
    return jnp.concatenate([gp[:, :c0], gp[:, main:main + nab], gp[:, c0:main]], axis=1)


def _local_step(x, target, layers, final_norm, dm):
    saved = []
    cur = x
    for li, wl in enumerate(layers):
        cur, sv = _layer_fwd(cur, wl, dm, f"_l{li}")
        saved.append(sv)
    loss_blk, dx, dfin = _loss_head(cur, final_norm, target, "loss_head")
    grads = [None] * len(layers)
    for li in reversed(range(len(layers))):
        dx, grads[li] = _layer_bwd(dx, layers[li], saved[li], dm, f"_l{li}")
    return loss_blk[0, 0], dx, grads, dfin.reshape(-1)


def _pieces(full, axis):
    return jnp.stack(jnp.split(full, N_CHIPS, axis=axis), axis=0)


def _unpieces(slots, axis):
    return jnp.concatenate([slots[q] for q in range(N_CHIPS)], axis=axis)


def kernel(x, attn_norm, w_in, gdn_conv_w, gdn_a_log, gdn_dt_bias, gdn_norm, lru_conv_w, lru_conv_b, lru_w_a, lru_b_a, lru_w_x, lru_b_x, lru_lambda, w_branch_gdn, w_branch_lru, w_out, mlp_norm, w_up, w_down, final_norm, loss_target, m_attn_norm, m_w_in, m_gdn_conv_w, m_gdn_a_log, m_gdn_dt_bias, m_gdn_norm, m_lru_conv_w, m_lru_conv_b, m_lru_w_a, m_lru_b_a, m_lru_w_x, m_lru_b_x, m_lru_lambda, m_w_branch_gdn, m_w_branch_lru, m_w_out, m_mlp_norm, m_w_up, m_w_down, m_final_norm, v_attn_norm, v_w_in, v_gdn_conv_w, v_gdn_a_log, v_gdn_dt_bias, v_gdn_norm, v_lru_conv_w, v_lru_conv_b, v_lru_w_a, v_lru_b_a, v_lru_w_x, v_lru_b_x, v_lru_lambda, v_w_branch_gdn, v_w_branch_lru, v_w_out, v_mlp_norm, v_w_up, v_w_down, v_final_norm):
    w = dict(attn_norm=attn_norm, w_in=w_in, gdn_conv_w=gdn_conv_w, gdn_a_log=gdn_a_log, gdn_dt_bias=gdn_dt_bias,
             gdn_norm=gdn_norm, lru_conv_w=lru_conv_w, lru_conv_b=lru_conv_b, lru_w_a=lru_w_a, lru_b_a=lru_b_a,
             lru_w_x=lru_w_x, lru_b_x=lru_b_x, lru_lambda=lru_lambda, w_branch_gdn=w_branch_gdn,
             w_branch_lru=w_branch_lru, w_out=w_out, mlp_norm=mlp_norm, w_up=w_up, w_down=w_down,
             final_norm=final_norm)
    m = dict(attn_norm=m_attn_norm, w_in=m_w_in, gdn_conv_w=m_gdn_conv_w, gdn_a_log=m_gdn_a_log,
             gdn_dt_bias=m_gdn_dt_bias, gdn_norm=m_gdn_norm, lru_conv_w=m_lru_conv_w, lru_conv_b=m_lru_conv_b,
             lru_w_a=m_lru_w_a, lru_b_a=m_lru_b_a, lru_w_x=m_lru_w_x, lru_b_x=m_lru_b_x, lru_lambda=m_lru_lambda,
             w_branch_gdn=m_w_branch_gdn, w_branch_lru=m_w_branch_lru, w_out=m_w_out, mlp_norm=m_mlp_norm,
             w_up=m_w_up, w_down=m_w_down, final_norm=m_final_norm)
    v = dict(attn_norm=v_attn_norm, w_in=v_w_in, gdn_conv_w=v_gdn_conv_w, gdn_a_log=v_gdn_a_log,
             gdn_dt_bias=v_gdn_dt_bias, gdn_norm=v_gdn_norm, lru_conv_w=v_lru_conv_w, lru_conv_b=v_lru_conv_b,
             lru_w_a=v_lru_w_a, lru_b_a=v_lru_b_a, lru_w_x=v_lru_w_x, lru_b_x=v_lru_b_x, lru_lambda=v_lru_lambda,
             w_branch_gdn=v_w_branch_gdn, w_branch_lru=v_w_branch_lru, w_out=v_w_out, mlp_norm=v_mlp_norm,
             w_up=v_w_up, w_down=v_w_down, final_norm=v_final_norm)
    n_layers = attn_norm.shape[0]
    d = x.shape[-1]
    heads = gdn_a_log.shape[-1]
    lw = lru_conv_b.shape[-1]
    dm = _dims(d, heads, lw)
    big_names = list(BIG_SHARD_AXIS)
    conv_names = list(CONV_SHARD_AXIS)
    chip = 2 * lax.axis_index("x") + lax.axis_index("y")

    shards = [w[n].astype(BF16) for n in big_names] + [w[n] for n in conv_names]
    gathered = _chip_exchange(shards, False, "weight_allgather")
    full = {n: _unpieces(gt, ax) for n, gt, ax in zip(
        big_names + conv_names, gathered, list(BIG_SHARD_AXIS.values()) + list(CONV_SHARD_AXIS.values()))}
    layers = []
    for li in range(n_layers):
        wl = {n: w[n][li] for n in SMALL_NAMES if n != 'final_norm' and n not in CONV_SHARD_AXIS}
        for n in conv_names + ['w_branch_gdn', 'w_branch_lru', 'w_out', 'w_up', 'w_down']:
            wl[n] = full[n][li]
        wl['w_in_p'] = _pad_w_in(full['w_in'][li], dm)
        layers.append(wl)

    loss_local, dx, grads, dfin = _local_step(x[0], loss_target[0], layers, final_norm, dm)
    loss = lax.psum(loss_local, MESH_AXES)

    gfull = {}
    for n in big_names:
        src = 'w_in_p' if n == 'w_in' else n
        per_layer = [(_unpad_w_in(grads[li][src], dm) if n == 'w_in' else grads[li][src]) for li in range(n_layers)]
        gfull[n] = jnp.stack(per_layer, axis=0)
    contrib = [_pieces(gfull[n], BIG_SHARD_AXIS[n]).astype(BF16) for n in big_names]
    landed = _chip_exchange(contrib, True, "grad_reduce_scatter")
    part = [_sum_slots(_as2d(lt, 1), f"grad_chip_sum_{n}") for n, lt in zip(big_names, landed)]
    other = _sibling_exchange(part, "grad_core_exchange")

    small_g = {n: jnp.stack([grads[li][n] for li in range(n_layers)], axis=0)
               for n in SMALL_NAMES if n != 'final_norm'}
    small_g['final_norm'] = dfin
    flat = jnp.concatenate([small_g[n].reshape(-1) for n in SMALL_NAMES])
    n_flat = flat.shape[0]
    row_unit = 32 * SUBLANES
    rows = -(-n_flat // (row_unit * LANES)) * row_unit
    buf = jnp.pad(flat, (0, rows * LANES - n_flat)).reshape(rows, LANES)
    everyone = _all_devices_gather(buf, "small_grad_allgather")
    small_sum = _sum_slots(everyone, "small_grad_sum").reshape(-1)
    small_red = {}
    off = 0
    for n in SMALL_NAMES:
        size = small_g[n].size
        small_red[n] = small_sum[off:off + size].reshape(small_g[n].shape)
        off += size
    for n, ax in CONV_SHARD_AXIS.items():
        width = w[n].shape[ax]
        small_red[n] = lax.dynamic_slice_in_dim(small_red[n], chip * width, width, axis=ax)

    out_g, out_d, out_m, out_v = {}, {}, {}, {}
    for n, p_mine, p_other in zip(big_names, part, other):
        res = _adamw(_as2d(w[n]), [p_mine, p_other], _as2d(m[n]), _as2d(v[n]), f"adamw_{n}")
        out_g[n], out_d[n], out_m[n], out_v[n] = (r.reshape(w[n].shape) for r in res)

    def pack(tree):
        fl = jnp.concatenate([tree[n].reshape(-1) for n in SMALL_NAMES])
        return jnp.pad(fl, (0, rows * LANES - fl.shape[0])).reshape(rows, LANES)

    res = _adamw(pack(w), [pack(small_red)], pack(m), pack(v), "adamw_small")
    for r, dst in zip(res, (out_g, out_d, out_m, out_v)):
        fl = r.reshape(-1)
        off = 0
        for n in SMALL_NAMES:
            dst[n] = fl[off:off + w[n].size].reshape(w[n].shape)
            off += w[n].size

    return (loss, dx[None], *[out_g[n] for n in WEIGHT_NAMES], *[out_d[n] for n in WEIGHT_NAMES],
            *[out_m[n] for n in WEIGHT_NAMES], *[out_v[n] for n in WEIGHT_NAMES])
```

```python
import functools

import jax
import jax.numpy as jnp
from jax import lax
from jax.experimental import pallas as pl
from jax.experimental.pallas import tpu as pltpu

F32 = jnp.float32
BF16 = jnp.bfloat16

LANES = 128
SUBLANES = 8
VMEM_BYTES = 64 * 1024 * 1024
GDN_CHUNK = 64
CONV_WIDTH = 4
RMS_EPS = 1e-6
L2_EPS = 1e-6
LRU_C = 8.0
ADAM_LR = 0.001
ADAM_B1 = 0.9
ADAM_B2 = 0.999
ADAM_EPS = 1e-08
ADAM_WD = 0.01
ADAM_STEP = 10
MESH_AXES = ("x", "y", "c")
N_CHIPS = 4
N_DEVICES = 8

INPUT_NAMES = ['x', 'attn_norm', 'w_in', 'gdn_conv_w', 'gdn_a_log', 'gdn_dt_bias', 'gdn_norm', 'lru_conv_w',
               'lru_conv_b', 'lru_w_a', 'lru_b_a', 'lru_w_x', 'lru_b_x', 'lru_lambda', 'w_branch_gdn',
               'w_branch_lru', 'w_out', 'mlp_norm', 'w_up', 'w_down', 'final_norm']
WEIGHT_NAMES = INPUT_NAMES[1:]
BIG_SHARD_AXIS = {'w_in': 2, 'w_branch_gdn': 2, 'w_branch_lru': 2, 'w_out': 1, 'w_up': 2, 'w_down': 1}
CONV_SHARD_AXIS = {'gdn_conv_w': 2, 'lru_conv_w': 2}
SMALL_NAMES = [n for n in WEIGHT_NAMES if n not in BIG_SHARD_AXIS]


def _tile(n, target, unit=LANES):
    best = None
    t = unit
    while t <= min(n, target):
        if n % t == 0:
            best = t
        t += unit
    return n if best is None else best


def _vmem_limit(block_bytes):
    return int(min(max(3 * block_bytes + (8 << 20), 24 << 20), VMEM_BYTES - (8 << 20)))


def _nbytes(shape, dtype):
    n = 1
    for s in shape:
        n *= s
    return n * jnp.dtype(dtype).itemsize


def _pcall(body, *, name, grid, in_specs, out_specs, out_shape, scratch_shapes=(), semantics=None, block_bytes=0,
           scalar_prefetch=0):
    params = dict(vmem_limit_bytes=_vmem_limit(block_bytes))
    if semantics is not None:
        params['dimension_semantics'] = semantics
    if scalar_prefetch:
        grid_spec = pltpu.PrefetchScalarGridSpec(num_scalar_prefetch=scalar_prefetch, grid=grid, in_specs=in_specs,
                                                 out_specs=out_specs, scratch_shapes=list(scratch_shapes))
        return pl.pallas_call(body, name=name, grid_spec=grid_spec, out_shape=out_shape,
                              compiler_params=pltpu.CompilerParams(**params))
    return pl.pallas_call(body, name=name, grid=grid, in_specs=in_specs, out_specs=out_specs, out_shape=out_shape,
                          scratch_shapes=list(scratch_shapes), compiler_params=pltpu.CompilerParams(**params))


def _dot(a, b):
    return jnp.dot(a.astype(BF16), b.astype(BF16), preferred_element_type=F32)


def _dot_nt(a, b):
    return lax.dot_general(a.astype(BF16), b.astype(BF16), (((1,), (1,)), ((), ())), preferred_element_type=F32)


def _dot_tn(a, b):
    return lax.dot_general(a.astype(BF16), b.astype(BF16), (((0,), (0,)), ((), ())), preferred_element_type=F32)


def _sigmoid(x):
    return 1.0 / (1.0 + jnp.exp(-x))


def _log1p(u):
    return jnp.where(u < 1e-3, u * (1.0 - u * (0.5 - u * (1.0 / 3.0))), jnp.log(1.0 + u))


def _softplus(x):
    return jnp.maximum(x, 0.0) + _log1p(jnp.exp(-jnp.abs(x)))


_GELU_K = 0.7978845608028654


def _gelu_and_grad(x):
    inner = _GELU_K * (x + 0.044715 * x * x * x)
    th = jnp.tanh(inner)
    g = 0.5 * x * (1.0 + th)
    dg = 0.5 * (1.0 + th) + 0.5 * x * (1.0 - th * th) * _GELU_K * (1.0 + 3.0 * 0.044715 * x * x)
    return g, dg


MATMUL_TK_MAX = 3584


def _matmul(a, b, *, mode, name, out_dtype=F32, add=None, epilogue=None, extra=None, tm=512, tn=1024, tk=2048):
    if mode == 'nn':
        (m, k), (k2, n) = a.shape, b.shape
    elif mode == 'nt':
        (m, k), (n, k2) = a.shape, b.shape
    else:
        (k, m), (k2, n) = a.shape, b.shape
    assert k == k2, (a.shape, b.shape, mode)
    tm, tn = _tile(m, tm), _tile(n, tn)
    tk = _tile(k, tk)
    if k // tk > 2 * (-(-k // MATMUL_TK_MAX)):
        tk = _tile(k, MATMUL_TK_MAX)
    nk = k // tk
    dims = {'nn': (((1,), (0,)), ((), ())), 'nt': (((1,), (1,)), ((), ())), 'tn': (((0,), (0,)), ((), ()))}[mode]
    a_bytes, b_bytes = _nbytes(a.shape, a.dtype), _nbytes(b.shape, b.dtype)
    rows_outer = nk > 1 or a_bytes + (m // tm) * b_bytes <= b_bytes + (n // tn) * a_bytes

    def ij(g0, g1):
        return (g0, g1) if rows_outer else (g1, g0)

    def spec(shape, pick):
        return pl.BlockSpec(shape, lambda g0, g1, kk: pick(*ij(g0, g1), kk))

    a_spec = spec((tk, tm), lambda i, j, kk: (kk, i)) if mode == 'tn' else spec((tm, tk), lambda i, j, kk: (i, kk))
    b_spec = spec((tn, tk), lambda i, j, kk: (j, kk)) if mode == 'nt' else spec((tk, tn), lambda i, j, kk: (kk, j))
    o_spec = spec((tm, tn), lambda i, j, kk: (i, j))
    operands, in_specs = [a, b], [a_spec, b_spec]
    if add is not None:
        operands.append(add)
        in_specs.append(o_spec)
    if extra is not None:
        operands.append(extra)
        in_specs.append(o_spec)
    n_in = len(operands)
    if epilogue == 'relu2':
        out_shape = (jax.ShapeDtypeStruct((m, n), BF16), jax.ShapeDtypeStruct((m, n), BF16))
        out_specs = (o_spec, o_spec)
    else:
        out_shape = jax.ShapeDtypeStruct((m, n), out_dtype)
        out_specs = o_spec

    def body(*refs):
        a_ref, b_ref = refs[0], refs[1]
        outs = refs[n_in:n_in + n_out]

        def finish(p):
            if add is not None:
                p = p + refs[2][...]
            if epilogue == 'relu2':
                ur = jnp.maximum(p, 0.0)
                outs[0][...] = ur.astype(BF16)
                outs[1][...] = (ur * ur).astype(BF16)
            elif epilogue == 'mul2x':
                outs[0][...] = (p * 2.0 * refs[n_in - 1][...].astype(F32)).astype(out_dtype)
            else:
                outs[0][...] = p.astype(out_dtype)

        prod = lax.dot_general(a_ref[...].astype(BF16), b_ref[...].astype(BF16), dims, preferred_element_type=F32)
        if nk == 1:
            finish(prod)
            return
        acc_ref = refs[-1]
        kk = pl.program_id(2)

        @pl.when(kk == 0)
        def _():
            acc_ref[...] = prod

        @pl.when((kk > 0) & (kk < nk - 1))
        def _():
            acc_ref[...] += prod

        @pl.when(kk == nk - 1)
        def _():
            finish(acc_ref[...] + prod)

    n_out = 2 if epilogue == 'relu2' else 1
    bb = (_nbytes((tm, tk), a.dtype) + _nbytes((tk, tn), b.dtype) + 3 * _nbytes((tm, tn), F32))
    grid = (m // tm, n // tn, nk) if rows_outer else (n // tn, m // tm, nk)
    return _pcall(body, name=name, grid=grid, in_specs=in_specs, out_specs=out_specs, out_shape=out_shape,
                  scratch_shapes=[pltpu.VMEM((tm, tn), F32)] if nk > 1 else [],
                  semantics=("parallel", "parallel", "arbitrary"), block_bytes=bb)(*operands)


def _row_tile(s, d, target_bytes=1 << 20):
    return _tile(s, max(SUBLANES, target_bytes // (4 * d)), SUBLANES)


def _rms_fwd(x, gain, name):
    s, d = x.shape
    tr = _row_tile(s, d)

    def body(x_ref, g_ref, h_ref):
        xv = x_ref[...]
        r = lax.rsqrt(jnp.mean(xv * xv, axis=-1, keepdims=True) + RMS_EPS)
        h_ref[...] = (xv * r * g_ref[...]).astype(BF16)

    row = pl.BlockSpec((tr, d), lambda i: (i, 0))
    return _pcall(body, name=name, grid=(s // tr,), in_specs=[row, pl.BlockSpec((1, d), lambda i: (0, 0))],
                  out_specs=row, out_shape=jax.ShapeDtypeStruct((s, d), BF16), semantics=("parallel",),
                  block_bytes=2 * tr * d * 4)(x, gain.reshape(1, d))


def _rms_bwd(x, gain, dh, dres, name):
    s, d = x.shape
    tr = _row_tile(s, d, 1 << 19)

    def body(x_ref, g_ref, dh_ref, dres_ref, dx_ref, dxb_ref, dg_ref):
        xv = x_ref[...]
        r = lax.rsqrt(jnp.mean(xv * xv, axis=-1, keepdims=True) + RMS_EPS)
        xh = xv * r
        dhv = dh_ref[...]
        dxh = dhv * g_ref[...]
        dx = dres_ref[...] + r * (dxh - xh * jnp.mean(dxh * xh, axis=-1, keepdims=True))
        dx_ref[...] = dx
        dxb_ref[...] = dx.astype(BF16)

        @pl.when(pl.program_id(0) == 0)
        def _():
            dg_ref[...] = jnp.zeros_like(dg_ref)

        dg_ref[...] += jnp.sum(dhv * xh, axis=0, keepdims=True)

    row = pl.BlockSpec((tr, d), lambda i: (i, 0))
    vec = pl.BlockSpec((1, d), lambda i: (0, 0))
    return _pcall(body, name=name, grid=(s // tr,), in_specs=[row, vec, row, row], out_specs=(row, row, vec),
                  out_shape=(jax.ShapeDtypeStruct((s, d), F32), jax.ShapeDtypeStruct((s, d), BF16),
                             jax.ShapeDtypeStruct((1, d), F32)),
                  semantics=("arbitrary",), block_bytes=5 * tr * d * 4)(x, gain.reshape(1, d), dh, dres)


def _loss_head(x, gain, target, name):
    s, d = x.shape
    tr = _row_tile(s, d, 1 << 19)

    def body(x_ref, g_ref, t_ref, loss_ref, dx_ref, dxb_ref, dg_ref):
        xv = x_ref[...]
        r = lax.rsqrt(jnp.mean(xv * xv, axis=-1, keepdims=True) + RMS_EPS)
        xh = xv * r
        gv = g_ref[...]
        err = xh * gv - t_ref[...]
        dy = err * (1.0 / d)
        dxh = dy * gv
        dx = r * (dxh - xh * jnp.mean(dxh * xh, axis=-1, keepdims=True))
        dx_ref[...] = dx
        dxb_ref[...] = dx.astype(BF16)

        @pl.when(pl.program_id(0) == 0)
        def _():
            dg_ref[...] = jnp.zeros_like(dg_ref)
            loss_ref[...] = jnp.zeros_like(loss_ref)

        dg_ref[...] += jnp.sum(dy * xh, axis=0, keepdims=True)
        part = jnp.sum(jnp.sum(err * err, axis=-1, keepdims=True), axis=0, keepdims=True) * (0.5 / d)
        loss_ref[...] += jnp.broadcast_to(part, loss_ref.shape)

    row = pl.BlockSpec((tr, d), lambda i: (i, 0))
    vec = pl.BlockSpec((1, d), lambda i: (0, 0))
    lspec = pl.BlockSpec((SUBLANES, LANES), lambda i: (0, 0))
    return _pcall(body, name=name, grid=(s // tr,), in_specs=[row, vec, row], out_specs=(lspec, row, row, vec),
                  out_shape=(jax.ShapeDtypeStruct((SUBLANES, LANES), F32), jax.ShapeDtypeStruct((s, d), F32),
                             jax.ShapeDtypeStruct((s, d), BF16), jax.ShapeDtypeStruct((1, d), F32)),
                  semantics=("arbitrary",), block_bytes=4 * tr * d * 4)(x, gain.reshape(1, d), target)


def _shift_down(xc, xp, s):
    tr = xc.shape[0]
    r = pltpu.roll(xc, s, 0)
    p = pltpu.roll(xp, s, 0)
    row8 = lax.broadcasted_iota(jnp.int32, (SUBLANES, xc.shape[1]), 0)
    head = jnp.where(row8 < s, p, r[:SUBLANES])
    if tr == SUBLANES:
        return head
    return jnp.concatenate([head, r[SUBLANES:]], axis=0)


def _shift_up(yc, yn, s):
    tr = yc.shape[0]
    u = pltpu.roll(yc, tr - s, 0)
    n = pltpu.roll(yn, SUBLANES - s, 0)
    row8 = lax.broadcasted_iota(jnp.int32, (SUBLANES, yc.shape[1]), 0)
    tail = jnp.where(row8 >= SUBLANES - s, n, u[tr - SUBLANES:])
    if tr == SUBLANES:
        return tail
    return jnp.concatenate([u[:tr - SUBLANES], tail], axis=0)


def _conv_apply(xc, xp, w):
    y = xc * w[CONV_WIDTH - 1:CONV_WIDTH, :]
    for s in range(1, CONV_WIDTH):
        y = y + _shift_down(xc, xp, s) * w[CONV_WIDTH - 1 - s:CONV_WIDTH - s, :]
    return y


def _halo_specs(tr, col_of):
    per = tr // SUBLANES
    cur = pl.BlockSpec((tr, LANES), lambda j, i: (i, col_of(j)))
    prev = pl.BlockSpec((SUBLANES, LANES), lambda j, i: (jnp.maximum(i * per - 1, 0), col_of(j)))
    return cur, prev


def _conv_bias_fwd(x_arr, x_col0, w, bias, name):
    s = x_arr.shape[0]
    ncb = w.shape[1] // LANES
    tr = _tile(s, 512, SUBLANES)

    def body(cur_ref, prev_ref, w_ref, b_ref, o_ref):
        i = pl.program_id(1)
        xp = prev_ref[...] * (i > 0).astype(F32)
        o_ref[...] = _conv_apply(cur_ref[...], xp, w_ref[...]) + b_ref[...]

    cur, prev = _halo_specs(tr, lambda j: x_col0 + j)
    return _pcall(body, name=name, grid=(ncb, s // tr),
                  in_specs=[cur, prev, pl.BlockSpec((CONV_WIDTH, LANES), lambda j, i: (0, j)),
                            pl.BlockSpec((1, LANES), lambda j, i: (0, j))],
                  out_specs=pl.BlockSpec((tr, LANES), lambda j, i: (i, j)),
                  out_shape=jax.ShapeDtypeStruct((s, w.shape[1]), F32), semantics=("parallel", "parallel"),
                  block_bytes=3 * tr * LANES * 4)(x_arr, x_arr, w, bias.reshape(1, -1))


def _conv_bwd(dy, x_arr, x_col0, w, name):
    s, c = dy.shape
    ncb = c // LANES
    tr = _tile(s, 512, SUBLANES)
    per = tr // SUBLANES
    ni = s // tr

    def body(dy_ref, dyn_ref, cur_ref, prev_ref, w_ref, dx_ref, dw_ref, db_ref):
        i = pl.program_id(1)
        dyv = dy_ref[...]
        dn = dyn_ref[...] * (i < ni - 1).astype(F32)
        xc = cur_ref[...]
        xp = prev_ref[...] * (i > 0).astype(F32)
        wv = w_ref[...]

        @pl.when(i == 0)
        def _():
            dw_ref[...] = jnp.zeros_like(dw_ref)
            db_ref[...] = jnp.zeros_like(db_ref)

        dx = dyv * wv[CONV_WIDTH - 1:CONV_WIDTH, :]
        dw_ref[CONV_WIDTH - 1:CONV_WIDTH, :] += jnp.sum(dyv * xc, axis=0, keepdims=True)
        for sh in range(1, CONV_WIDTH):
            j = CONV_WIDTH - 1 - sh
            dx = dx + _shift_up(dyv, dn, sh) * wv[j:j + 1, :]
            dw_ref[j:j + 1, :] += jnp.sum(dyv * _shift_down(xc, xp, sh), axis=0, keepdims=True)
        dx_ref[...] = dx
        db_ref[...] += jnp.sum(dyv, axis=0, keepdims=True)

    cur, prev = _halo_specs(tr, lambda j: x_col0 + j)
    dcur = pl.BlockSpec((tr, LANES), lambda j, i: (i, j))
    dnext = pl.BlockSpec((SUBLANES, LANES), lambda j, i: (jnp.minimum((i + 1) * per, s // SUBLANES - 1), j))
    return _pcall(body, name=name, grid=(ncb, ni),
                  in_specs=[dcur, dnext, cur, prev, pl.BlockSpec((CONV_WIDTH, LANES), lambda j, i: (0, j))],
                  out_specs=(dcur, pl.BlockSpec((CONV_WIDTH, LANES), lambda j, i: (0, j)),
                             pl.BlockSpec((1, LANES), lambda j, i: (0, j))),
                  out_shape=(jax.ShapeDtypeStruct((s, c), F32), jax.ShapeDtypeStruct((CONV_WIDTH, c), F32),
                             jax.ShapeDtypeStruct((1, c), F32)),
                  semantics=("parallel", "arbitrary"), block_bytes=4 * tr * LANES * 4)(dy, dy, x_arr, x_arr, w)


def _gdn_pre_fwd(proj, conv_w, heads, name):
    s = proj.shape[0]
    ncb = conv_w.shape[1] // LANES
    tr = _tile(s, 512, SUBLANES)
    qscale = float(LANES) ** -0.5

    def body(cur_ref, prev_ref, w_ref, o_ref):
        j, i = pl.program_id(0), pl.program_id(1)
        xp = prev_ref[...] * (i > 0).astype(F32)
        cv = _conv_apply(cur_ref[...], xp, w_ref[...])
        sv = cv * _sigmoid(cv)
        nrm = lax.rsqrt(jnp.sum(sv * sv, axis=-1, keepdims=True) + L2_EPS)
        scale = jnp.where(j < heads, qscale, 1.0)
        o_ref[...] = jnp.where(j < 2 * heads, sv * nrm * scale, sv)

    cur, prev = _halo_specs(tr, lambda j: j)
    return _pcall(body, name=name, grid=(ncb, s // tr),
                  in_specs=[cur, prev, pl.BlockSpec((CONV_WIDTH, LANES), lambda j, i: (0, j))],
                  out_specs=pl.BlockSpec((tr, LANES), lambda j, i: (i, j)),
                  out_shape=jax.ShapeDtypeStruct((s, conv_w.shape[1]), F32), semantics=("parallel", "parallel"),
                  block_bytes=3 * tr * LANES * 4)(proj, proj, conv_w)


def _gdn_pre_bwd(proj, conv_w, dqkv, heads, name):
    s = proj.shape[0]
    ncb = conv_w.shape[1] // LANES
    tr = _tile(s, 512, SUBLANES)
    qscale = float(LANES) ** -0.5

    def body(cur_ref, prev_ref, w_ref, d_ref, o_ref):
        j, i = pl.program_id(0), pl.program_id(1)
        xp = prev_ref[...] * (i > 0).astype(F32)
        cv = _conv_apply(cur_ref[...], xp, w_ref[...])
        sg = _sigmoid(cv)
        sv = cv * sg
        nrm = lax.rsqrt(jnp.sum(sv * sv, axis=-1, keepdims=True) + L2_EPS)
        dv = d_ref[...]
        scale = jnp.where(j < heads, qscale, 1.0)
        dsn = scale * nrm * (dv - sv * (nrm * nrm) * jnp.sum(dv * sv, axis=-1, keepdims=True))
        ds = jnp.where(j < 2 * heads, dsn, dv)
        o_ref[...] = ds * (sg * (1.0 + cv * (1.0 - sg)))

    cur, prev = _halo_specs(tr, lambda j: j)
    blk = pl.BlockSpec((tr, LANES), lambda j, i: (i, j))
    return _pcall(body, name=name, grid=(ncb, s // tr),
                  in_specs=[cur, prev, pl.BlockSpec((CONV_WIDTH, LANES), lambda j, i: (0, j)), blk],
                  out_specs=blk, out_shape=jax.ShapeDtypeStruct((s, conv_w.shape[1]), F32),
                  semantics=("parallel", "parallel"), block_bytes=4 * tr * LANES * 4)(proj, proj, conv_w, dqkv)


def _tri_inverse(a_strict):
    c = a_strict.shape[0]
    ri = lax.broadcasted_iota(jnp.int32, (c, c), 0)
    ci = lax.broadcasted_iota(jnp.int32, (c, c), 1)
    p = jnp.where(ri == ci, 1.0, 0.0) - a_strict
    xp = a_strict
    span = 1
    while 2 * span < c:
        xp = _dot(xp, xp)
        p = p + _dot(p, xp)
        span *= 2
    return p


def _gdn_chunk_terms(q, k, v, ab, alog, dtb, head, heads):
    c = q.shape[0]
    lane = lax.broadcasted_iota(jnp.int32, (1, LANES), 1)
    a_col = jnp.sum(jnp.where(lane == head, ab, 0.0), axis=1, keepdims=True)
    b_col = jnp.sum(jnp.where(lane == heads + head, ab, 0.0), axis=1, keepdims=True)
    alog_h = jnp.sum(jnp.where(lane == head, alog, 0.0), axis=1, keepdims=True)
    dtb_h = jnp.sum(jnp.where(lane == head, dtb, 0.0), axis=1, keepdims=True)
    pre = a_col + dtb_h
    neg_ea = -jnp.exp(alog_h)
    g = neg_ea * _softplus(pre)
    beta = _sigmoid(b_col)
    ri = lax.broadcasted_iota(jnp.int32, (c, c), 0)
    ci = lax.broadcasted_iota(jnp.int32, (c, c), 1)
    gc_col = jnp.sum(jnp.where(ci <= ri, jnp.sum(jnp.where(ri == ci, g, 0.0), axis=0, keepdims=True), 0.0),
                     axis=1, keepdims=True)
    gc_row = jnp.sum(jnp.where(ri <= ci, g, 0.0), axis=0, keepdims=True)
    causal = ri >= ci
    strict = ri > ci
    decay = jnp.where(causal, jnp.exp(jnp.where(causal, gc_col - gc_row, 0.0)), 0.0)
    g_last = jnp.sum(g, axis=0, keepdims=True)
    egc = jnp.exp(gc_col)
    ekl = jnp.exp(g_last - gc_col)
    kb = k * beta
    vb = v * beta
    kk = _dot_nt(kb, k)
    a_strict = jnp.where(strict, kk * decay, 0.0)
    return dict(pre=pre, neg_ea=neg_ea, g=g, beta=beta, ri=ri, ci=ci, causal=causal, strict=strict, decay=decay,
                g_last=g_last, egc=egc, ekl=ekl, kb=kb, vb=vb, kk=kk, a_strict=a_strict, lane=lane)


def _gdn_fwd(qkv, proj, alog, dtb, heads, name):
    s = qkv.shape[0]
    c = GDN_CHUNK
    nc = s // c
    ab_blk = proj.shape[1] // LANES - 1

    def body(q_ref, k_ref, v_ref, ab_ref, alog_ref, dtb_ref, o_ref, t_ref, s0_ref, state_ref):
        head, ch = pl.program_id(0), pl.program_id(1)

        @pl.when(ch == 0)
        def _():
            state_ref[...] = jnp.zeros_like(state_ref)

        q, k, v = q_ref[...], k_ref[...], v_ref[...]
        tm = _gdn_chunk_terms(q, k, v, ab_ref[...], alog_ref[...], dtb_ref[...], head, heads)
        t_inv = _tri_inverse(tm['a_strict'])
        u = _dot(t_inv, tm['vb'])
        w = _dot(t_inv, tm['kb'] * tm['egc'])
        qk = jnp.where(tm['causal'], _dot_nt(q, k) * tm['decay'], 0.0)
        st = state_ref[...]
        v_new = u - _dot(w, st)
        o_ref[...] = _dot(q * tm['egc'], st) + _dot(qk, v_new)
        t_ref[...] = t_inv
        s0_ref[...] = st
        state_ref[...] = st * jnp.exp(tm['g_last']) + _dot_tn(k * tm['ekl'], v_new)

    def blk(off):
        return pl.BlockSpec((c, LANES), lambda h, n: (n, off + h))

    vec = pl.BlockSpec((1, LANES), lambda h, n: (0, 0))
    return _pcall(
        body, name=name, grid=(heads, nc),
        in_specs=[blk(0), blk(heads), blk(2 * heads), pl.BlockSpec((c, LANES), lambda h, n: (n, ab_blk)), vec, vec],
        out_specs=(pl.BlockSpec((c, LANES), lambda h, n: (n, h)),
                   pl.BlockSpec((None, None, c, c), lambda h, n: (h, n, 0, 0)),
                   pl.BlockSpec((None, None, LANES, LANES), lambda h, n: (h, n, 0, 0))),
        out_shape=(jax.ShapeDtypeStruct((s, heads * LANES), F32), jax.ShapeDtypeStruct((heads, nc, c, c), F32),
                   jax.ShapeDtypeStruct((heads, nc, LANES, LANES), F32)),
        scratch_shapes=[pltpu.VMEM((LANES, LANES), F32)], semantics=("parallel", "arbitrary"),
        block_bytes=8 * c * LANES * 4 + 2 * LANES * LANES * 4)(qkv, qkv, qkv, proj, alog, dtb)


def _gdn_bwd(qkv, proj, alog, dtb, t_all, s0_all, d_o, heads, name):
    s = qkv.shape[0]
    c = GDN_CHUNK
    nc = s // c
    ab_blk = proj.shape[1] // LANES - 1

    def body(q_ref, k_ref, v_ref, ab_ref, alog_ref, dtb_ref, t_ref, s0_ref, do_ref,
             dq_ref, dk_ref, dv_ref, dgb_ref, ds_ref):
        head, step = pl.program_id(0), pl.program_id(1)

        @pl.when(step == 0)
        def _():
            ds_ref[...] = jnp.zeros_like(ds_ref)

        q, k, v = q_ref[...], k_ref[...], v_ref[...]
        tm = _gdn_chunk_terms(q, k, v, ab_ref[...], alog_ref[...], dtb_ref[...], head, heads)
        ri, ci, causal, strict, decay = tm['ri'], tm['ci'], tm['causal'], tm['strict'], tm['decay']
        egc, ekl, kb, vb, beta, g_last = tm['egc'], tm['ekl'], tm['kb'], tm['vb'], tm['beta'], tm['g_last']
        t_inv = t_ref[...]
        st = s0_ref[...]
        do = do_ref[...]
        ds_next = ds_ref[...]
        kbg = kb * egc
        u = _dot(t_inv, vb)
        w = _dot(t_inv, kbg)
        qkm = _dot_nt(q, k)
        qk = jnp.where(causal, qkm * decay, 0.0)
        v_new = u - _dot(w, st)
        qd = q * egc
        kd = k * ekl
        e_last = jnp.exp(g_last)

        dqd = _dot_nt(do, st)
        dqk = jnp.where(causal, _dot_nt(do, v_new), 0.0)
        dvn = _dot_tn(qk, do) + _dot(kd, ds_next)
        dkd = _dot_nt(v_new, ds_next)
        dgl = jnp.sum(jnp.sum(st * ds_next, axis=1, keepdims=True), axis=0, keepdims=True) * e_last
        dw = -_dot_nt(dvn, st)
        ds_ref[...] = _dot_tn(qd, do) + e_last * ds_next - _dot_tn(w, dvn)
        dt = _dot_nt(dvn, vb) + _dot_nt(dw, kbg)
        dvb = _dot_tn(t_inv, dvn)
        dkbg = _dot_tn(t_inv, dw)
        da_m = jnp.where(strict, -_dot_tn(t_inv, _dot_nt(dt, t_inv)), 0.0)
        dad = da_m * decay
        dkb = _dot(dad, k) + dkbg * egc
        dqkd = dqk * decay
        dq = _dot(dqkd, k) + dqd * egc
        dk = _dot_tn(dad, kb) + _dot_tn(dqkd, q) + dkd * ekl + dkb * beta
        e_mat = (da_m * tm['kk'] + dqk * qkm) * decay
        s_kd = jnp.sum(dkd * kd, axis=1, keepdims=True)
        dgl = dgl + jnp.sum(s_kd, axis=0, keepdims=True)
        col_sum = jnp.sum(e_mat, axis=0, keepdims=True)
        col_sum_c = jnp.sum(jnp.where(ri == ci, col_sum, 0.0), axis=1, keepdims=True)
        dgc = (jnp.sum(e_mat, axis=1, keepdims=True) - col_sum_c + jnp.sum(dqd * qd, axis=1, keepdims=True)
               - s_kd + jnp.sum(dkbg * kbg, axis=1, keepdims=True))
        row_c = lax.broadcasted_iota(jnp.int32, (c, 1), 0)
        dgc = dgc + jnp.where(row_c == c - 1, dgl, 0.0)
        dgc_row = jnp.sum(jnp.where(ri == ci, dgc, 0.0), axis=0, keepdims=True)
        dg = jnp.sum(jnp.where(ci >= ri, dgc_row, 0.0), axis=1, keepdims=True)
        dbeta = jnp.sum(dkb * k, axis=1, keepdims=True) + jnp.sum(dvb * v, axis=1, keepdims=True)
        da_pre = dg * tm['neg_ea'] * _sigmoid(tm['pre'])
        db_pre = dbeta * beta * (1.0 - beta)
        lane = tm['lane']
        dq_ref[...] = dq
        dk_ref[...] = dk
        dv_ref[...] = dvb * beta
        dgb_ref[...] = (jnp.where(lane == head, da_pre, 0.0) + jnp.where(lane == heads + head, db_pre, 0.0)
                        + jnp.where(lane == 2 * heads + head, dg * tm['g'], 0.0))

    def blk(off):
        return pl.BlockSpec((c, LANES), lambda h, n: (nc - 1 - n, off + h))

    vec = pl.BlockSpec((1, LANES), lambda h, n: (0, 0))
    gw = heads * LANES
    dq, dk, dv, dgb = _pcall(
        body, name=name, grid=(heads, nc),
        in_specs=[blk(0), blk(heads), blk(2 * heads), pl.BlockSpec((c, LANES), lambda h, n: (nc - 1 - n, ab_blk)),
                  vec, vec, pl.BlockSpec((None, None, c, c), lambda h, n: (h, nc - 1 - n, 0, 0)),
                  pl.BlockSpec((None, None, LANES, LANES), lambda h, n: (h, nc - 1 - n, 0, 0)), blk(0)],
        out_specs=(blk(0), blk(0), blk(0), blk(0)),
        out_shape=tuple(jax.ShapeDtypeStruct((s, gw), F32) for _ in range(4)),
        scratch_shapes=[pltpu.VMEM((LANES, LANES), F32)], semantics=("parallel", "arbitrary"),
        block_bytes=12 * c * LANES * 4 + 2 * LANES * LANES * 4)(qkv, qkv, qkv, proj, alog, dtb, t_all, s0_all, d_o)
    return dq, dk, dv, dgb


def _gdn_post_fwd(o, proj, z_col0, gain, name):
    s, gw = o.shape
    heads = gw // LANES
    tr = _tile(s, 512, SUBLANES)

    def body(o_ref, z_ref, g_ref, y_ref):
        ov, zv = o_ref[...], z_ref[...]
        r = lax.rsqrt(jnp.mean(ov * ov, axis=-1, keepdims=True) + RMS_EPS)
        y_ref[...] = (ov * r * g_ref[...] * (zv * _sigmoid(zv))).astype(BF16)

    blk = pl.BlockSpec((tr, LANES), lambda i, h: (i, h))
    return _pcall(body, name=name, grid=(s // tr, heads),
                  in_specs=[blk, pl.BlockSpec((tr, LANES), lambda i, h: (i, z_col0 + h)),
                            pl.BlockSpec((1, LANES), lambda i, h: (0, 0))],
                  out_specs=blk, out_shape=jax.ShapeDtypeStruct((s, gw), BF16), semantics=("parallel", "parallel"),
                  block_bytes=3 * tr * LANES * 4)(o, proj, gain.reshape(1, LANES))


def _gdn_post_bwd(o, proj, z_col0, gain, dy, name):
    s, gw = o.shape
    heads = gw // LANES
    tr = _tile(s, 512, SUBLANES)

    def body(o_ref, z_ref, g_ref, dy_ref, do_ref, dz_ref, dg_ref):
        ov, zv, gv, dyv = o_ref[...], z_ref[...], g_ref[...], dy_ref[...]
        r = lax.rsqrt(jnp.mean(ov * ov, axis=-1, keepdims=True) + RMS_EPS)
        nv = ov * r
        sg = _sigmoid(zv)
        sz = zv * sg
        dn = dyv * gv * sz
        do_ref[...] = r * (dn - nv * jnp.mean(dn * nv, axis=-1, keepdims=True))
        dz_ref[...] = dyv * nv * gv * (sg * (1.0 + zv * (1.0 - sg)))

        @pl.when((pl.program_id(0) == 0) & (pl.program_id(1) == 0))
        def _():
            dg_ref[...] = jnp.zeros_like(dg_ref)

        dg_ref[...] += jnp.sum(dyv * nv * sz, axis=0, keepdims=True)

    blk = pl.BlockSpec((tr, LANES), lambda i, h: (i, h))
    vec = pl.BlockSpec((1, LANES), lambda i, h: (0, 0))
    return _pcall(body, name=name, grid=(s // tr, heads),
                  in_specs=[blk, pl.BlockSpec((tr, LANES), lambda i, h: (i, z_col0 + h)), vec, blk],
                  out_specs=(blk, blk, vec),
                  out_shape=(jax.ShapeDtypeStruct((s, gw), F32), jax.ShapeDtypeStruct((s, gw), F32),
                             jax.ShapeDtypeStruct((1, LANES), F32)),
                  semantics=("arbitrary", "arbitrary"), block_bytes=6 * tr * LANES * 4)(
                      o, proj, gain.reshape(1, LANES), dy)


def _dab_reduce(dgb, name):
    s, gw = dgb.shape
    heads = gw // LANES
    tr = _tile(s, 512, SUBLANES)

    def body(d_ref, o_ref, cs_ref):
        acc = d_ref[:, 0:LANES]
        for h in range(1, heads):
            acc = acc + d_ref[:, h * LANES:(h + 1) * LANES]
        o_ref[...] = acc

        @pl.when(pl.program_id(0) == 0)
        def _():
            cs_ref[...] = jnp.zeros_like(cs_ref)

        cs_ref[...] += jnp.sum(acc, axis=0, keepdims=True)

    return _pcall(body, name=name, grid=(s // tr,), in_specs=[pl.BlockSpec((tr, gw), lambda i: (i, 0))],
                  out_specs=(pl.BlockSpec((tr, LANES), lambda i: (i, 0)), pl.BlockSpec((1, LANES), lambda i: (0, 0))),
                  out_shape=(jax.ShapeDtypeStruct((s, LANES), F32), jax.ShapeDtypeStruct((1, LANES), F32)),
                  semantics=("arbitrary",), block_bytes=tr * gw * 4)(dgb)


def _lru_gates(xc, wa, wx, ba, bx, lam):
    r = _sigmoid(_dot(xc, wa) + ba)
    ig = _sigmoid(_dot(xc, wx) + bx)
    sp = _softplus(-lam)
    log_a = -LRU_C * r * sp
    a = jnp.exp(log_a)
    e2 = jnp.exp(2.0 * log_a)
    mult = jnp.sqrt(jnp.maximum(1.0 - e2, 0.0))
    return r, ig, sp, a, e2, mult


def _lru_fwd(xc, proj, y_col0, wa, wx, ba, bx, lam, name):
    s, lw = xc.shape
    nb = lw // LANES
    tr = _tile(s, 256, SUBLANES)

    def body(xc_ref, y_ref, wa_ref, wx_ref, ba_ref, bx_ref, lam_ref, h_ref, o_ref, carry_ref):
        @pl.when(pl.program_id(1) == 0)
        def _():
            carry_ref[...] = jnp.zeros_like(carry_ref)

        xv = xc_ref[...]
        _, ig, _, a, _, mult = _lru_gates(xv, wa_ref[...], wx_ref[...], ba_ref[...], bx_ref[...], lam_ref[...])
        b = mult * (ig * xv)
        row = lax.broadcasted_iota(jnp.int32, (tr, LANES), 0)
        sh = 1
        while sh < tr:
            keep = row >= sh
            b = a * jnp.where(keep, pltpu.roll(b, sh, 0), 0.0) + b
            a = a * jnp.where(keep, pltpu.roll(a, sh, 0), 1.0)
            sh *= 2
        hv = a * carry_ref[0:1, :] + b
        h_ref[...] = hv
        carry_ref[...] = jnp.broadcast_to(hv[tr - 1:tr, :], carry_ref.shape)
        gy, _ = _gelu_and_grad(y_ref[...])
        o_ref[...] = (hv * gy).astype(BF16)

    blk = pl.BlockSpec((tr, LANES), lambda n, i: (i, n))
    wspec = pl.BlockSpec((None, LANES, LANES), lambda n, i: (n, 0, 0))
    vec = pl.BlockSpec((1, LANES), lambda n, i: (0, n))
    return _pcall(body, name=name, grid=(nb, s // tr),
                  in_specs=[blk, pl.BlockSpec((tr, LANES), lambda n, i: (i, y_col0 + n)), wspec, wspec, vec, vec, vec],
                  out_specs=(blk, blk),
                  out_shape=(jax.ShapeDtypeStruct((s, lw), F32), jax.ShapeDtypeStruct((s, lw), BF16)),
                  scratch_shapes=[pltpu.VMEM((SUBLANES, LANES), F32)], semantics=("parallel", "arbitrary"),
                  block_bytes=8 * tr * LANES * 4)(xc, proj, wa, wx, ba.reshape(1, lw), bx.reshape(1, lw),
                                                  lam.reshape(1, lw))


def _lru_bwd(d_out, xc, hseq, proj, y_col0, wa, wx, ba, bx, lam, name):
    s, lw = xc.shape
    nb = lw // LANES
    tr = _tile(s, 256, SUBLANES)
    per = tr // SUBLANES
    ni = s // tr
    nrow8 = s // SUBLANES

    def body(do_ref, xc_ref, xn_ref, h_ref, hp_ref, y_ref, wa_ref, wx_ref, ba_ref, bx_ref, lam_ref,
             dxc_ref, dy_ref, dwa_ref, dwx_ref, dba_ref, dbx_ref, dlam_ref, carry_ref):
        step = pl.program_id(1)
        tile = ni - 1 - step

        @pl.when(step == 0)
        def _():
            carry_ref[...] = jnp.zeros_like(carry_ref)
            dwa_ref[...] = jnp.zeros_like(dwa_ref)
            dwx_ref[...] = jnp.zeros_like(dwx_ref)
            dba_ref[...] = jnp.zeros_like(dba_ref)
            dbx_ref[...] = jnp.zeros_like(dbx_ref)
            dlam_ref[...] = jnp.zeros_like(dlam_ref)

        wav, wxv, bav, bxv, lamv = wa_ref[...], wx_ref[...], ba_ref[...], bx_ref[...], lam_ref[...]
        xv = xc_ref[...]
        r, ig, sp, a, e2, mult = _lru_gates(xv, wav, wxv, bav, bxv, lamv)
        a_next = _lru_gates(xn_ref[...], wav, wxv, bav, bxv, lamv)[3] * (tile < ni - 1).astype(F32)
        hv = h_ref[...]
        h_prev = _shift_down(hv, hp_ref[...] * (tile > 0).astype(F32), 1)
        yv = y_ref[...]
        gy, dgy = _gelu_and_grad(yv)
        dov = do_ref[...]
        dy_ref[...] = dov * hv * dgy
        coef = _shift_up(a, a_next, 1)
        bb = dov * gy
        row = lax.broadcasted_iota(jnp.int32, (tr, LANES), 0)
        sh = 1
        while sh < tr:
            keep = row < tr - sh
            bb = coef * jnp.where(keep, pltpu.roll(bb, tr - sh, 0), 0.0) + bb
            coef = coef * jnp.where(keep, pltpu.roll(coef, tr - sh, 0), 1.0)
            sh *= 2
        lam_t = coef * carry_ref[0:1, :] + bb
        carry_ref[...] = jnp.broadcast_to(lam_t[0:1, :], carry_ref.shape)
        d_a = lam_t * h_prev
        d_mult = lam_t * (ig * xv)
        d_ix = lam_t * mult
        d_la = d_a * a - d_mult * e2 / jnp.maximum(mult, 1e-30)
        d_r = d_la * (-LRU_C * sp)
        dlam_ref[...] += jnp.sum(d_la * (LRU_C * r) * _sigmoid(-lamv), axis=0, keepdims=True)
        d_pa = d_r * r * (1.0 - r)
        d_px = (d_ix * xv) * ig * (1.0 - ig)
        dxc_ref[...] = d_ix * ig + _dot_nt(d_pa, wav) + _dot_nt(d_px, wxv)
        dwa_ref[...] += _dot_tn(xv, d_pa)
        dwx_ref[...] += _dot_tn(xv, d_px)
        dba_ref[...] += jnp.sum(d_pa, axis=0, keepdims=True)
        dbx_ref[...] += jnp.sum(d_px, axis=0, keepdims=True)

    blk = pl.BlockSpec((tr, LANES), lambda n, i: (ni - 1 - i, n))
    nxt = pl.BlockSpec((SUBLANES, LANES), lambda n, i: (jnp.minimum((ni - i) * per, nrow8 - 1), n))
    prv = pl.BlockSpec((SUBLANES, LANES), lambda n, i: (jnp.maximum((ni - 1 - i) * per - 1, 0), n))
    wspec = pl.BlockSpec((None, LANES, LANES), lambda n, i: (n, 0, 0))
    vec = pl.BlockSpec((1, LANES), lambda n, i: (0, n))
    return _pcall(
        body, name=name, grid=(nb, ni),
        in_specs=[blk, blk, nxt, blk, prv, pl.BlockSpec((tr, LANES), lambda n, i: (ni - 1 - i, y_col0 + n)),
                  wspec, wspec, vec, vec, vec],
        out_specs=(blk, blk, wspec, wspec, vec, vec, vec),
        out_shape=(jax.ShapeDtypeStruct((s, lw), F32), jax.ShapeDtypeStruct((s, lw), F32),
                   jax.ShapeDtypeStruct((nb, LANES, LANES), F32), jax.ShapeDtypeStruct((nb, LANES, LANES), F32),
                   jax.ShapeDtypeStruct((1, lw), F32), jax.ShapeDtypeStruct((1, lw), F32),
                   jax.ShapeDtypeStruct((1, lw), F32)),
        scratch_shapes=[pltpu.VMEM((SUBLANES, LANES), F32)], semantics=("parallel", "arbitrary"),
        block_bytes=12 * tr * LANES * 4)(d_out, xc, xc, hseq, hseq, proj, wa, wx, ba.reshape(1, lw),
                                         bx.reshape(1, lw), lam.reshape(1, lw))


def _merge_fwd(proj, gg_col0, gl_col0, bg, bl, name):
    s, d = bg.shape
    tr, tc = _tile(s, 256, SUBLANES), _tile(d, 1024)
    cb = tc // LANES

    def body(gg_ref, gl_ref, bg_ref, bl_ref, o_ref):
        o_ref[...] = (_sigmoid(gg_ref[...]) * bg_ref[...] + _sigmoid(gl_ref[...]) * bl_ref[...]).astype(BF16)

    blk = pl.BlockSpec((tr, tc), lambda i, j: (i, j))
    return _pcall(body, name=name, grid=(s // tr, d // tc),
                  in_specs=[pl.BlockSpec((tr, tc), lambda i, j: (i, gg_col0 // cb + j)),
                            pl.BlockSpec((tr, tc), lambda i, j: (i, gl_col0 // cb + j)), blk, blk],
                  out_specs=blk, out_shape=jax.ShapeDtypeStruct((s, d), BF16), semantics=("parallel", "parallel"),
                  block_bytes=5 * tr * tc * 4)(proj, proj, bg, bl)


def _merge_bwd(proj, gg_col0, gl_col0, bg, bl, dm, name):
    s, d = bg.shape
    tr, tc = _tile(s, 256, SUBLANES), _tile(d, 1024)
    cb = tc // LANES

    def body(gg_ref, gl_ref, bg_ref, bl_ref, dm_ref, dgg_ref, dgl_ref, dbg_ref, dbl_ref):
        dmv = dm_ref[...]
        sg, sl = _sigmoid(gg_ref[...]), _sigmoid(gl_ref[...])
        dgg_ref[...] = (dmv * bg_ref[...] * sg * (1.0 - sg)).astype(BF16)
        dgl_ref[...] = (dmv * bl_ref[...] * sl * (1.0 - sl)).astype(BF16)
        dbg_ref[...] = (dmv * sg).astype(BF16)
        dbl_ref[...] = (dmv * sl).astype(BF16)

    blk = pl.BlockSpec((tr, tc), lambda i, j: (i, j))
    sh = jax.ShapeDtypeStruct((s, d), BF16)
    return _pcall(body, name=name, grid=(s // tr, d // tc),
                  in_specs=[pl.BlockSpec((tr, tc), lambda i, j: (i, gg_col0 // cb + j)),
                            pl.BlockSpec((tr, tc), lambda i, j: (i, gl_col0 // cb + j)), blk, blk, blk],
                  out_specs=(blk, blk, blk, blk), out_shape=(sh, sh, sh, sh), semantics=("parallel", "parallel"),
                  block_bytes=8 * tr * tc * 4)(proj, proj, bg, bl, dm)


def _sum_slots(slots, name):
    n, r, c = slots.shape
    tr = _tile(r, max(2 * SUBLANES, (1 << 19) // (c * 4)), 2 * SUBLANES)

    def body(s_ref, o_ref):
        acc = s_ref[0].astype(F32)
        for q in range(1, n):
            acc = acc + s_ref[q].astype(F32)
        o_ref[...] = acc

    return _pcall(body, name=name, grid=(r // tr,), in_specs=[pl.BlockSpec((n, tr, c), lambda i: (0, i, 0))],
                  out_specs=pl.BlockSpec((tr, c), lambda i: (i, 0)), out_shape=jax.ShapeDtypeStruct((r, c), F32),
                  semantics=("parallel",), block_bytes=(n + 1) * tr * c * 4)(slots)


def _adamw(w, g_parts, m, v, name):
    r, c = w.shape
    np_ = len(g_parts)
    tr = _tile(r, max(SUBLANES, (1 << 20) // (c * 4)), SUBLANES)
    c1 = 1.0 - ADAM_B1 ** ADAM_STEP
    c2 = 1.0 - ADAM_B2 ** ADAM_STEP

    def body(*refs):
        w_ref, m_ref, v_ref = refs[0], refs[1 + np_], refs[2 + np_]
        g_ref, d_ref, nm_ref, nv_ref = refs[3 + np_:]
        g = refs[1][...]
        for p in range(1, np_):
            g = g + refs[1 + p][...]
        nm = ADAM_B1 * m_ref[...] + (1.0 - ADAM_B1) * g
        nv = ADAM_B2 * v_ref[...] + (1.0 - ADAM_B2) * (g * g)
        g_ref[...] = g
        nm_ref[...] = nm
        nv_ref[...] = nv
        d_ref[...] = -ADAM_LR * ((nm / c1) / (jnp.sqrt(nv / c2) + ADAM_EPS) + ADAM_WD * w_ref[...])

    blk = pl.BlockSpec((tr, c), lambda i: (i, 0))
    sh = jax.ShapeDtypeStruct((r, c), F32)
    return _pcall(body, name=name, grid=(r // tr,), in_specs=[blk] * (3 + np_), out_specs=(blk,) * 4,
                  out_shape=(sh,) * 4, semantics=("parallel",), block_bytes=(7 + np_) * tr * c * 4)(
                      w, *g_parts, m, v)


def _pair_sum(core, mine, theirs, name):
    _, n, r, c = mine.shape
    tr = _tile(r, max(2 * SUBLANES, (1 << 19) // (c * 4)), 2 * SUBLANES)

    def body(core_ref, a_ref, b_ref, o_ref):
        o_ref[...] = (a_ref[...].astype(F32) + b_ref[...].astype(F32)).astype(BF16)

    return _pcall(body, name=name, grid=(n, r // tr),
                  in_specs=[pl.BlockSpec((None, None, tr, c), lambda q, i, core_ref: (core_ref[0], q, i, 0)),
                            pl.BlockSpec((None, tr, c), lambda q, i, core_ref: (q, i, 0))],
                  out_specs=pl.BlockSpec((None, tr, c), lambda q, i, core_ref: (q, i, 0)),
                  out_shape=jax.ShapeDtypeStruct((n, r, c), BF16), semantics=("parallel", "parallel"),
                  block_bytes=3 * tr * c * 4, scalar_prefetch=1)(core, mine, theirs)


def _adamw_layers(core, w, g_core, g_other, m, v, name):
    nl, r, c = w.shape
    tr = _tile(r, max(SUBLANES, (1 << 20) // (c * 4)), SUBLANES)
    c1 = 1.0 - ADAM_B1 ** ADAM_STEP
    c2 = 1.0 - ADAM_B2 ** ADAM_STEP

    def body(core_ref, w_ref, gc_ref, go_ref, m_ref, v_ref, g_ref, d_ref, nm_ref, nv_ref):
        g = jnp.where(pl.program_id(0) == core_ref[0], gc_ref[...], go_ref[...])
        nm = ADAM_B1 * m_ref[...] + (1.0 - ADAM_B1) * g
        nv = ADAM_B2 * v_ref[...] + (1.0 - ADAM_B2) * (g * g)
        g_ref[...] = g
        nm_ref[...] = nm
        nv_ref[...] = nv
        d_ref[...] = -ADAM_LR * ((nm / c1) / (jnp.sqrt(nv / c2) + ADAM_EPS) + ADAM_WD * w_ref[...])

    blk = pl.BlockSpec((None, tr, c), lambda l, i, core_ref: (l, i, 0))
    gblk = pl.BlockSpec((tr, c), lambda l, i, core_ref: (i, 0))
    sh = jax.ShapeDtypeStruct((nl, r, c), F32)
    return _pcall(body, name=name, grid=(nl, r // tr), in_specs=[blk, gblk, gblk, blk, blk], out_specs=(blk,) * 4,
                  out_shape=(sh,) * 4, semantics=("parallel", "parallel"), block_bytes=9 * tr * c * 4,
                  scalar_prefetch=1)(core, w, g_core, g_other, m, v)


HBM_SPEC = pl.BlockSpec(memory_space=pltpu.HBM)


def _other_chips(x, y):
    return [(1 - x, y), (x, 1 - y), (1 - x, 1 - y)]


def _weight_allgather(shards, name):
    n = len(shards)

    def body(*refs):
        ins, outs = refs[:n], refs[n:2 * n]
        ici_send, ici_recv, d2d_send, d2d_recv, local_sems = refs[2 * n:]
        x, y, c = lax.axis_index("x"), lax.axis_index("y"), lax.axis_index("c")
        me = 2 * x + y
        chips = _other_chips(x, y)

        def ici(t, j, dst_slot):
            px, py = chips[j]
            return pltpu.make_async_remote_copy(
                src_ref=ins[t].at[c], dst_ref=outs[t].at[dst_slot].at[c], send_sem=ici_send.at[3 * t + j],
                recv_sem=ici_recv.at[3 * t + j], device_id=(px, py, c), device_id_type=pl.DeviceIdType.MESH)

        def d2d(t, j, layer):
            px, py = chips[j]
            place = outs[t].at[2 * px + py].at[layer]
            return pltpu.make_async_remote_copy(
                src_ref=place, dst_ref=place, send_sem=d2d_send.at[3 * t + j], recv_sem=d2d_recv.at[3 * t + j],
                device_id=(x, y, 1 - c), device_id_type=pl.DeviceIdType.MESH)

        local, sends = [], []
        for t in range(n):
            lc = pltpu.make_async_copy(ins[t], outs[t].at[me], local_sems.at[t])
            lc.start()
            local.append(lc)
            for j in range(3):
                cp = ici(t, j, me)
                cp.start()
                sends.append(cp)
        for t in range(n):
            for j in range(3):
                px, py = chips[j]
                ici(t, j, 2 * px + py).wait_recv()
                fw = d2d(t, j, c)
                fw.start()
                sends.append(fw)
        for t in range(n):
            for j in range(3):
                d2d(t, j, 1 - c).wait_recv()
        for cp in sends:
            cp.wait_send()
        for lc in local:
            lc.wait()

    out_shape = tuple(jax.ShapeDtypeStruct((N_CHIPS,) + a.shape, a.dtype) for a in shards)
    return pl.pallas_call(
        body, name=name, in_specs=[HBM_SPEC] * n, out_specs=(HBM_SPEC,) * n, out_shape=out_shape,
        scratch_shapes=[pltpu.SemaphoreType.DMA((3 * n,)), pltpu.SemaphoreType.DMA((3 * n,)),
                        pltpu.SemaphoreType.DMA((3 * n,)), pltpu.SemaphoreType.DMA((3 * n,)),
                        pltpu.SemaphoreType.DMA((n,))])(*shards)


def _chip_scatter(arrs, name):
    n = len(arrs)

    def body(*refs):
        ins, outs = refs[:n], refs[n:2 * n]
        send_sems, recv_sems, local_sems = refs[2 * n:]
        x, y, c = lax.axis_index("x"), lax.axis_index("y"), lax.axis_index("c")
        me = 2 * x + y
        chips = _other_chips(x, y)

        def remote(t, j, dst_slot):
            px, py = chips[j]
            return pltpu.make_async_remote_copy(
                src_ref=ins[t].at[2 * px + py], dst_ref=outs[t].at[dst_slot], send_sem=send_sems.at[3 * t + j],
                recv_sem=recv_sems.at[3 * t + j], device_id=(px, py, c), device_id_type=pl.DeviceIdType.MESH)

        local, sends = [], []
        for t in range(n):
            lc = pltpu.make_async_copy(ins[t].at[me], outs[t].at[me], local_sems.at[t])
            lc.start()
            local.append(lc)
            for j in range(3):
                cp = remote(t, j, me)
                cp.start()
                sends.append(cp)
        for t in range(n):
            for j in range(3):
                px, py = chips[j]
                remote(t, j, 2 * px + py).wait_recv()
        for cp in sends:
            cp.wait_send()
        for lc in local:
            lc.wait()

    return pl.pallas_call(
        body, name=name, in_specs=[HBM_SPEC] * n, out_specs=(HBM_SPEC,) * n,
        out_shape=tuple(jax.ShapeDtypeStruct(a.shape, a.dtype) for a in arrs),
        scratch_shapes=[pltpu.SemaphoreType.DMA((3 * n,)), pltpu.SemaphoreType.DMA((3 * n,)),
                        pltpu.SemaphoreType.DMA((n,))])(*arrs)


def _sibling_exchange(arrs, other_layer, name):
    n = len(arrs)

    def body(*refs):
        ins, outs = refs[:n], refs[n:2 * n]
        send_sems, recv_sems = refs[2 * n:]
        c = lax.axis_index("c")
        sib = (lax.axis_index("x"), lax.axis_index("y"), 1 - c)
        copies = [pltpu.make_async_remote_copy(src_ref=ins[t].at[1 - c] if other_layer else ins[t], dst_ref=outs[t],
                                               send_sem=send_sems.at[t], recv_sem=recv_sems.at[t], device_id=sib,
                                               device_id_type=pl.DeviceIdType.MESH) for t in range(n)]
        for cp in copies:
            cp.start()
        for cp in copies:
            cp.wait_recv()
        for cp in copies:
            cp.wait_send()

    return pl.pallas_call(
        body, name=name, in_specs=[HBM_SPEC] * n, out_specs=(HBM_SPEC,) * n,
        out_shape=tuple(jax.ShapeDtypeStruct(a.shape[1:] if other_layer else a.shape, a.dtype) for a in arrs),
        scratch_shapes=[pltpu.SemaphoreType.DMA((n,)), pltpu.SemaphoreType.DMA((n,))])(*arrs)


def _all_devices_gather(buf, name):
    def body(in_ref, out_ref, send_sems, recv_sems, local_sem):
        x, y, c = lax.axis_index("x"), lax.axis_index("y"), lax.axis_index("c")
        me = 4 * x + 2 * y + c

        def peer(mask):
            px = 1 - x if mask & 4 else x
            py = 1 - y if mask & 2 else y
            pc = 1 - c if mask & 1 else c
            return px, py, pc

        def remote(mask, dst_slot):
            return pltpu.make_async_remote_copy(
                src_ref=in_ref, dst_ref=out_ref.at[dst_slot], send_sem=send_sems.at[mask - 1],
                recv_sem=recv_sems.at[mask - 1], device_id=peer(mask), device_id_type=pl.DeviceIdType.MESH)

        lc = pltpu.make_async_copy(in_ref, out_ref.at[me], local_sem)
        lc.start()
        sends = [remote(mask, me) for mask in range(1, N_DEVICES)]
        for cp in sends:
            cp.start()
        for mask in range(1, N_DEVICES):
            px, py, pc = peer(mask)
            remote(mask, 4 * px + 2 * py + pc).wait_recv()
        for cp in sends:
            cp.wait_send()
        lc.wait()

    return pl.pallas_call(
        body, name=name, in_specs=[HBM_SPEC], out_specs=HBM_SPEC,
        out_shape=jax.ShapeDtypeStruct((N_DEVICES,) + buf.shape, buf.dtype),
        scratch_shapes=[pltpu.SemaphoreType.DMA((N_DEVICES - 1,)), pltpu.SemaphoreType.DMA((N_DEVICES - 1,)),
                        pltpu.SemaphoreType.DMA])(buf)


def _pad_lanes(vec):
    return jnp.pad(vec.astype(F32), (0, LANES - vec.shape[0])).reshape(1, LANES)


def _layer_fwd(x, wl, dm, tag):
    heads, gw, lw, d = dm['heads'], dm['gw'], dm['lw'], dm['d']
    h = _rms_fwd(x, wl['attn_norm'], f"rms1_fwd{tag}")
    proj = _matmul(h, wl['w_in_p'], mode='nn', name=f"proj{tag}")
    alog, dtb = _pad_lanes(wl['gdn_a_log']), _pad_lanes(wl['gdn_dt_bias'])
    qkv = _gdn_pre_fwd(proj, wl['gdn_conv_w'], heads, f"gdn_pre_fwd{tag}")
    o, t_all, s0_all = _gdn_fwd(qkv, proj, alog, dtb, heads, f"gdn_fwd{tag}")
    o_gdn = _gdn_post_fwd(o, proj, dm['z_blk'], wl['gdn_norm'], f"gdn_post_fwd{tag}")
    xc = _conv_bias_fwd(proj, dm['xb_blk'], wl['lru_conv_w'], wl['lru_conv_b'], f"lru_conv_fwd{tag}")
    hseq, o_lru = _lru_fwd(xc, proj, dm['yb_blk'], wl['lru_w_a'], wl['lru_w_x'], wl['lru_b_a'], wl['lru_b_x'],
                           wl['lru_lambda'], f"lru_fwd{tag}")
    bg = _matmul(o_gdn, wl['w_branch_gdn'], mode='nn', name=f"branch_gdn{tag}")
    bl = _matmul(o_lru, wl['w_branch_lru'], mode='nn', name=f"branch_lru{tag}")
    merged = _merge_fwd(proj, dm['gg_blk'], dm['gl_blk'], bg, bl, f"merge_fwd{tag}")
    x_mid = _matmul(merged, wl['w_out'], mode='nn', add=x, name=f"out_proj{tag}")
    h2 = _rms_fwd(x_mid, wl['mlp_norm'], f"rms2_fwd{tag}")
    ur, act = _matmul(h2, wl['w_up'], mode='nn', epilogue='relu2', name=f"mlp_up{tag}")
    x_out = _matmul(act, wl['w_down'], mode='nn', add=x_mid, name=f"mlp_down{tag}")
    saved = dict(x=x, h=h, proj=proj, qkv=qkv, o=o, t_all=t_all, s0_all=s0_all, o_gdn=o_gdn, xc=xc, hseq=hseq,
                 o_lru=o_lru, bg=bg, bl=bl, merged=merged, x_mid=x_mid, h2=h2, ur=ur, act=act, alog=alog, dtb=dtb)
    return x_out, saved


def _layer_bwd(dx_out, dx_out_b, wl, sv, dm, tag):
    heads, gw, lw, d = dm['heads'], dm['gw'], dm['lw'], dm['d']
    g = {}
    du = _matmul(dx_out_b, wl['w_down'], mode='nt', epilogue='mul2x', extra=sv['ur'], out_dtype=BF16,
                 name=f"d_mlp_act{tag}")
    g['w_down'] = _matmul(sv['act'], dx_out_b, mode='tn', name=f"dw_down{tag}")
    g['w_up'] = _matmul(sv['h2'], du, mode='tn', name=f"dw_up{tag}")
    dh2 = _matmul(du, wl['w_up'], mode='nt', name=f"d_h2{tag}")
    dx_mid, dx_mid_b, g['mlp_norm'] = _rms_bwd(sv['x_mid'], wl['mlp_norm'], dh2, dx_out, f"rms2_bwd{tag}")
    dmerged = _matmul(dx_mid_b, wl['w_out'], mode='nt', name=f"d_merged{tag}")
    g['w_out'] = _matmul(sv['merged'], dx_mid_b, mode='tn', name=f"dw_out{tag}")
    dgg, dgl, dbg, dbl = _merge_bwd(sv['proj'], dm['gg_blk'], dm['gl_blk'], sv['bg'], sv['bl'], dmerged,
                                    f"merge_bwd{tag}")
    g['w_branch_gdn'] = _matmul(sv['o_gdn'], dbg, mode='tn', name=f"dw_branch_gdn{tag}")
    g['w_branch_lru'] = _matmul(sv['o_lru'], dbl, mode='tn', name=f"dw_branch_lru{tag}")
    do_gdn = _matmul(dbg, wl['w_branch_gdn'], mode='nt', name=f"d_o_gdn{tag}")
    do_lru = _matmul(dbl, wl['w_branch_lru'], mode='nt', name=f"d_o_lru{tag}")
    d_o, dz, dgn = _gdn_post_bwd(sv['o'], sv['proj'], dm['z_blk'], wl['gdn_norm'], do_gdn, f"gdn_post_bwd{tag}")
    g['gdn_norm'] = dgn.reshape(-1)
    dq, dk, dv, dgb = _gdn_bwd(sv['qkv'], sv['proj'], sv['alog'], sv['dtb'], sv['t_all'], sv['s0_all'], d_o, heads,
                               f"gdn_bwd{tag}")
    dqkv_n = jnp.concatenate([dq, dk, dv], axis=1)
    dconv = _gdn_pre_bwd(sv['proj'], wl['gdn_conv_w'], dqkv_n, heads, f"gdn_pre_bwd{tag}")
    dqkv, g['gdn_conv_w'], _ = _conv_bwd(dconv, sv['proj'], 0, wl['gdn_conv_w'], f"gdn_conv_bwd{tag}")
    dab, dab_sum = _dab_reduce(dgb, f"dab_reduce{tag}")
    g['gdn_dt_bias'] = dab_sum[0, :heads]
    g['gdn_a_log'] = dab_sum[0, 2 * heads:3 * heads]
    dxc, dyb, g['lru_w_a'], g['lru_w_x'], dba, dbx, dlam = _lru_bwd(
        do_lru, sv['xc'], sv['hseq'], sv['proj'], dm['yb_blk'], wl['lru_w_a'], wl['lru_w_x'], wl['lru_b_a'],
        wl['lru_b_x'], wl['lru_lambda'], f"lru_bwd{tag}")
    g['lru_b_a'], g['lru_b_x'], g['lru_lambda'] = dba.reshape(-1), dbx.reshape(-1), dlam.reshape(-1)
    dxb, g['lru_conv_w'], dcb = _conv_bwd(dxc, sv['proj'], dm['xb_blk'], wl['lru_conv_w'], f"lru_conv_bwd{tag}")
    g['lru_conv_b'] = dcb.reshape(-1)
    dproj = jnp.concatenate([dqkv.astype(BF16), dz.astype(BF16), dxb.astype(BF16), dyb.astype(BF16), dgg, dgl,
                             dab.astype(BF16)], axis=1)
    g['w_in_p'] = _matmul(sv['h'], dproj, mode='tn', name=f"dw_in{tag}")
    dh = _matmul(dproj, wl['w_in_p'], mode='nt', name=f"d_h{tag}")
    dx_in, dx_in_b, g['attn_norm'] = _rms_bwd(sv['x'], wl['attn_norm'], dh, dx_mid, f"rms1_bwd{tag}")
    g['attn_norm'] = g['attn_norm'].reshape(-1)
    g['mlp_norm'] = g['mlp_norm'].reshape(-1)
    return dx_in, dx_in_b, g


def _dims(d, heads, lw):
    gw = heads * LANES
    nab = 2 * heads
    blk = dict(z_blk=3 * heads, xb_blk=4 * heads, yb_blk=4 * heads + lw // LANES)
    gg0 = 4 * gw + 2 * lw
    return dict(d=d, heads=heads, gw=gw, lw=lw, nab=nab, gg_blk=gg0 // LANES, gl_blk=(gg0 + d) // LANES,
                main=gg0 + 2 * d, np=gg0 + 2 * d + LANES, **blk)


def _pad_w_in(w_in, dm):
    c0 = 4 * dm['gw']
    nab = dm['nab']
    return jnp.concatenate([w_in[:, :c0], w_in[:, c0 + nab:], w_in[:, c0:c0 + nab],
                            jnp.zeros((w_in.shape[0], LANES - nab), w_in.dtype)], axis=1)


def _unpad_w_in(gp, dm):
    c0 = 4 * dm['gw']
    nab = dm['nab']
    main = dm['main']
    return jnp.concatenate([gp[:, :c0], gp[:, main:main + nab], gp[:, c0:main]], axis=1)


def _local_step(x, target, layers, final_norm, dm):
    saved = []
    cur = x
    for li, wl in enumerate(layers):
        cur, sv = _layer_fwd(cur, wl, dm, f"_l{li}")
        saved.append(sv)
    loss_blk, dx, dx_b, dfin = _loss_head(cur, final_norm, target, "loss_head")
    grads = [None] * len(layers)
    for li in reversed(range(len(layers))):
        dx, dx_b, grads[li] = _layer_bwd(dx, dx_b, layers[li], saved[li], dm, f"_l{li}")
    return loss_blk[0, 0], dx, grads, dfin.reshape(-1)


def kernel(x, attn_norm, w_in, gdn_conv_w, gdn_a_log, gdn_dt_bias, gdn_norm, lru_conv_w, lru_conv_b, lru_w_a, lru_b_a, lru_w_x, lru_b_x, lru_lambda, w_branch_gdn, w_branch_lru, w_out, mlp_norm, w_up, w_down, final_norm, loss_target, m_attn_norm, m_w_in, m_gdn_conv_w, m_gdn_a_log, m_gdn_dt_bias, m_gdn_norm, m_lru_conv_w, m_lru_conv_b, m_lru_w_a, m_lru_b_a, m_lru_w_x, m_lru_b_x, m_lru_lambda, m_w_branch_gdn, m_w_branch_lru, m_w_out, m_mlp_norm, m_w_up, m_w_down, m_final_norm, v_attn_norm, v_w_in, v_gdn_conv_w, v_gdn_a_log, v_gdn_dt_bias, v_gdn_norm, v_lru_conv_w, v_lru_conv_b, v_lru_w_a, v_lru_b_a, v_lru_w_x, v_lru_b_x, v_lru_lambda, v_w_branch_gdn, v_w_branch_lru, v_w_out, v_mlp_norm, v_w_up, v_w_down, v_final_norm):
    w = dict(attn_norm=attn_norm, w_in=w_in, gdn_conv_w=gdn_conv_w, gdn_a_log=gdn_a_log, gdn_dt_bias=gdn_dt_bias,
             gdn_norm=gdn_norm, lru_conv_w=lru_conv_w, lru_conv_b=lru_conv_b, lru_w_a=lru_w_a, lru_b_a=lru_b_a,
             lru_w_x=lru_w_x, lru_b_x=lru_b_x, lru_lambda=lru_lambda, w_branch_gdn=w_branch_gdn,
             w_branch_lru=w_branch_lru, w_out=w_out, mlp_norm=mlp_norm, w_up=w_up, w_down=w_down,
             final_norm=final_norm)
    m = dict(attn_norm=m_attn_norm, w_in=m_w_in, gdn_conv_w=m_gdn_conv_w, gdn_a_log=m_gdn_a_log,
             gdn_dt_bias=m_gdn_dt_bias, gdn_norm=m_gdn_norm, lru_conv_w=m_lru_conv_w, lru_conv_b=m_lru_conv_b,
             lru_w_a=m_lru_w_a, lru_b_a=m_lru_b_a, lru_w_x=m_lru_w_x, lru_b_x=m_lru_b_x, lru_lambda=m_lru_lambda,
             w_branch_gdn=m_w_branch_gdn, w_branch_lru=m_w_branch_lru, w_out=m_w_out, mlp_norm=m_mlp_norm,
             w_up=m_w_up, w_down=m_w_down, final_norm=m_final_norm)
    v = dict(attn_norm=v_attn_norm, w_in=v_w_in, gdn_conv_w=v_gdn_conv_w, gdn_a_log=v_gdn_a_log,
             gdn_dt_bias=v_gdn_dt_bias, gdn_norm=v_gdn_norm, lru_conv_w=v_lru_conv_w, lru_conv_b=v_lru_conv_b,
             lru_w_a=v_lru_w_a, lru_b_a=v_lru_b_a, lru_w_x=v_lru_w_x, lru_b_x=v_lru_b_x, lru_lambda=v_lru_lambda,
             w_branch_gdn=v_w_branch_gdn, w_branch_lru=v_w_branch_lru, w_out=v_w_out, mlp_norm=v_mlp_norm,
             w_up=v_w_up, w_down=v_w_down, final_norm=v_final_norm)
    n_layers = attn_norm.shape[0]
    d = x.shape[-1]
    heads = gdn_a_log.shape[-1]
    lw = lru_conv_b.shape[-1]
    dm = _dims(d, heads, lw)
    big_names = list(BIG_SHARD_AXIS)
    conv_names = list(CONV_SHARD_AXIS)
    chip = 2 * lax.axis_index("x") + lax.axis_index("y")

    shards = [w[n].astype(BF16) for n in big_names] + [w[n] for n in conv_names]
    gathered = dict(zip(big_names + conv_names, _weight_allgather(shards, "weight_allgather")))
    shard_axis = {**BIG_SHARD_AXIS, **CONV_SHARD_AXIS}
    layers = []
    for li in range(n_layers):
        wl = {n: w[n][li] for n in SMALL_NAMES if n != 'final_norm' and n not in CONV_SHARD_AXIS}
        for n in conv_names + big_names:
            wl[n] = jnp.concatenate([gathered[n][q, li] for q in range(N_CHIPS)], axis=shard_axis[n] - 1)
        wl['w_in_p'] = _pad_w_in(wl.pop('w_in'), dm)
        layers.append(wl)

    loss_local, dx, grads, dfin = _local_step(x[0], loss_target[0], layers, final_norm, dm)
    loss = lax.psum(loss_local, MESH_AXES)

    core = lax.axis_index("c").astype(jnp.int32).reshape(1)
    contrib = []
    for n in big_names:
        per_layer = [(_unpad_w_in(grads[li]['w_in_p'], dm) if n == 'w_in' else grads[li][n])
                     for li in range(n_layers)]
        contrib.append(jnp.stack([jnp.stack(jnp.split(g, N_CHIPS, axis=BIG_SHARD_AXIS[n] - 1), axis=0)
                                  for g in per_layer], axis=0).astype(BF16))
    theirs = _sibling_exchange(contrib, True, "grad_core_send")
    chip_sum = [_pair_sum(core, mine, th, f"grad_chip_sum_{n}") for n, mine, th in zip(big_names, contrib, theirs)]
    landed = _chip_scatter(chip_sum, "grad_reduce_scatter")
    total = [_sum_slots(lt, f"grad_total_{n}") for n, lt in zip(big_names, landed)]
    other = _sibling_exchange(total, False, "grad_core_exchange")

    small_g = {n: jnp.stack([grads[li][n] for li in range(n_layers)], axis=0)
               for n in SMALL_NAMES if n != 'final_norm'}
    small_g['final_norm'] = dfin
    flat = jnp.concatenate([small_g[n].reshape(-1) for n in SMALL_NAMES])
    n_flat = flat.shape[0]
    row_unit = 32 * SUBLANES
    rows = -(-n_flat // (row_unit * LANES)) * row_unit
    buf = jnp.pad(flat, (0, rows * LANES - n_flat)).reshape(rows, LANES)
    everyone = _all_devices_gather(buf, "small_grad_allgather")
    small_sum = _sum_slots(everyone, "small_grad_sum").reshape(-1)
    small_red = {}
    off = 0
    for n in SMALL_NAMES:
        size = small_g[n].size
        small_red[n] = small_sum[off:off + size].reshape(small_g[n].shape)
        off += size
    for n, ax in CONV_SHARD_AXIS.items():
        width = w[n].shape[ax]
        small_red[n] = lax.dynamic_slice_in_dim(small_red[n], chip * width, width, axis=ax)

    out_g, out_d, out_m, out_v = {}, {}, {}, {}
    for n, g_core, g_other in zip(big_names, total, other):
        out_g[n], out_d[n], out_m[n], out_v[n] = _adamw_layers(core, w[n], g_core, g_other, m[n], v[n], f"adamw_{n}")

    def pack(tree):
        fl = jnp.concatenate([tree[n].reshape(-1) for n in SMALL_NAMES])
        return jnp.pad(fl, (0, rows * LANES - fl.shape[0])).reshape(rows, LANES)

    res = _adamw(pack(w), [pack(small_red)], pack(m), pack(v), "adamw_small")
    for r, dst in zip(res, (out_g, out_d, out_m, out_v)):
        fl = r.reshape(-1)
        off = 0
        for n in SMALL_NAMES:
            dst[n] = fl[off:off + w[n].size].reshape(w[n].shape)
            off += w[n].size

    return (loss, dx[None], *[out_g[n] for n in WEIGHT_NAMES], *[out_d[n] for n in WEIGHT_NAMES],
            *[out_m[n] for n in WEIGHT_NAMES], *[out_v[n] for n in WEIGHT_NAMES])
```

```python
import functools

import jax
import jax.numpy as jnp
from jax import lax
from jax.experimental import pallas as pl
from jax.experimental.pallas import tpu as pltpu

F32 = jnp.float32
BF16 = jnp.bfloat16

LANES = 128
SUBLANES = 8
VMEM_BYTES = 64 * 1024 * 1024
GDN_CHUNK = 64
CONV_WIDTH = 4
RMS_EPS = 1e-6
L2_EPS = 1e-6
LRU_C = 8.0
ADAM_LR = 0.001
ADAM_B1 = 0.9
ADAM_B2 = 0.999
ADAM_EPS = 1e-08
ADAM_WD = 0.01
ADAM_STEP = 10
MESH_AXES = ("x", "y", "c")
N_CHIPS = 4
N_DEVICES = 8

INPUT_NAMES = ['x', 'attn_norm', 'w_in', 'gdn_conv_w', 'gdn_a_log', 'gdn_dt_bias', 'gdn_norm', 'lru_conv_w',
               'lru_conv_b', 'lru_w_a', 'lru_b_a', 'lru_w_x', 'lru_b_x', 'lru_lambda', 'w_branch_gdn',
               'w_branch_lru', 'w_out', 'mlp_norm', 'w_up', 'w_down', 'final_norm']
WEIGHT_NAMES = INPUT_NAMES[1:]
BIG_SHARD_AXIS = {'w_in': 2, 'w_branch_gdn': 2, 'w_branch_lru': 2, 'w_out': 1, 'w_up': 2, 'w_down': 1}
CONV_SHARD_AXIS = {'gdn_conv_w': 2, 'lru_conv_w': 2}
SMALL_NAMES = [n for n in WEIGHT_NAMES if n not in BIG_SHARD_AXIS]


def _tile(n, target, unit=LANES):
    best = None
    t = unit
    while t <= min(n, target):
        if n % t == 0:
            best = t
        t += unit
    return n if best is None else best


def _vmem_limit(block_bytes):
    return int(min(max(3 * block_bytes + (8 << 20), 24 << 20), VMEM_BYTES - (8 << 20)))


def _nbytes(shape, dtype):
    n = 1
    for s in shape:
        n *= s
    return n * jnp.dtype(dtype).itemsize


def _pcall(body, *, name, grid, in_specs, out_specs, out_shape, scratch_shapes=(), semantics=None, block_bytes=0,
           scalar_prefetch=0):
    params = dict(vmem_limit_bytes=_vmem_limit(block_bytes))
    if semantics is not None:
        params['dimension_semantics'] = semantics
    if scalar_prefetch:
        grid_spec = pltpu.PrefetchScalarGridSpec(num_scalar_prefetch=scalar_prefetch, grid=grid, in_specs=in_specs,
                                                 out_specs=out_specs, scratch_shapes=list(scratch_shapes))
        return pl.pallas_call(body, name=name, grid_spec=grid_spec, out_shape=out_shape,
                              compiler_params=pltpu.CompilerParams(**params))
    return pl.pallas_call(body, name=name, grid=grid, in_specs=in_specs, out_specs=out_specs, out_shape=out_shape,
                          scratch_shapes=list(scratch_shapes), compiler_params=pltpu.CompilerParams(**params))


def _dot(a, b):
    return jnp.dot(a.astype(BF16), b.astype(BF16), preferred_element_type=F32)


def _dot_nt(a, b):
    return lax.dot_general(a.astype(BF16), b.astype(BF16), (((1,), (1,)), ((), ())), preferred_element_type=F32)


def _dot_tn(a, b):
    return lax.dot_general(a.astype(BF16), b.astype(BF16), (((0,), (0,)), ((), ())), preferred_element_type=F32)


def _sigmoid(x):
    return 1.0 / (1.0 + jnp.exp(-x))


def _log1p(u):
    return jnp.where(u < 1e-3, u * (1.0 - u * (0.5 - u * (1.0 / 3.0))), jnp.log(1.0 + u))


def _softplus(x):
    return jnp.maximum(x, 0.0) + _log1p(jnp.exp(-jnp.abs(x)))


_GELU_K = 0.7978845608028654


def _gelu_and_grad(x):
    inner = _GELU_K * (x + 0.044715 * x * x * x)
    th = jnp.tanh(inner)
    g = 0.5 * x * (1.0 + th)
    dg = 0.5 * (1.0 + th) + 0.5 * x * (1.0 - th * th) * _GELU_K * (1.0 + 3.0 * 0.044715 * x * x)
    return g, dg


MATMUL_TK_MAX = 3584


def _matmul(a, b, *, mode, name, out_dtype=F32, add=None, epilogue=None, extra=None, tm=512, tn=1024, tk=2048):
    if mode == 'nn':
        (m, k), (k2, n) = a.shape, b.shape
    elif mode == 'nt':
        (m, k), (n, k2) = a.shape, b.shape
    else:
        (k, m), (k2, n) = a.shape, b.shape
    assert k == k2, (a.shape, b.shape, mode)
    tm, tn = _tile(m, tm), _tile(n, tn)
    tk = _tile(k, tk)
    if k // tk > 2 * (-(-k // MATMUL_TK_MAX)):
        tk = _tile(k, MATMUL_TK_MAX)
    nk = k // tk
    dims = {'nn': (((1,), (0,)), ((), ())), 'nt': (((1,), (1,)), ((), ())), 'tn': (((0,), (0,)), ((), ()))}[mode]
    a_bytes, b_bytes = _nbytes(a.shape, a.dtype), _nbytes(b.shape, b.dtype)
    rows_outer = nk > 1 or a_bytes + (m // tm) * b_bytes <= b_bytes + (n // tn) * a_bytes

    def ij(g0, g1):
        return (g0, g1) if rows_outer else (g1, g0)

    def spec(shape, pick):
        return pl.BlockSpec(shape, lambda g0, g1, kk: pick(*ij(g0, g1), kk))

    a_spec = spec((tk, tm), lambda i, j, kk: (kk, i)) if mode == 'tn' else spec((tm, tk), lambda i, j, kk: (i, kk))
    b_spec = spec((tn, tk), lambda i, j, kk: (j, kk)) if mode == 'nt' else spec((tk, tn), lambda i, j, kk: (kk, j))
    o_spec = spec((tm, tn), lambda i, j, kk: (i, j))
    operands, in_specs = [a, b], [a_spec, b_spec]
    if add is not None:
        operands.append(add)
        in_specs.append(o_spec)
    if extra is not None:
        operands.append(extra)
        in_specs.append(o_spec)
    n_in = len(operands)
    if epilogue == 'relu2':
        out_shape = (jax.ShapeDtypeStruct((m, n), BF16), jax.ShapeDtypeStruct((m, n), BF16))
        out_specs = (o_spec, o_spec)
    else:
        out_shape = jax.ShapeDtypeStruct((m, n), out_dtype)
        out_specs = o_spec

    def body(*refs):
        a_ref, b_ref = refs[0], refs[1]
        outs = refs[n_in:n_in + n_out]

        def finish(p):
            if add is not None:
                p = p + refs[2][...]
            if epilogue == 'relu2':
                ur = jnp.maximum(p, 0.0)
                outs[0][...] = ur.astype(BF16)
                outs[1][...] = (ur * ur).astype(BF16)
            elif epilogue == 'mul2x':
                outs[0][...] = (p * 2.0 * refs[n_in - 1][...].astype(F32)).astype(out_dtype)
            else:
                outs[0][...] = p.astype(out_dtype)

        prod = lax.dot_general(a_ref[...].astype(BF16), b_ref[...].astype(BF16), dims, preferred_element_type=F32)
        if nk == 1:
            finish(prod)
            return
        acc_ref = refs[-1]
        kk = pl.program_id(2)

        @pl.when(kk == 0)
        def _():
            acc_ref[...] = prod

        @pl.when((kk > 0) & (kk < nk - 1))
        def _():
            acc_ref[...] += prod

        @pl.when(kk == nk - 1)
        def _():
            finish(acc_ref[...] + prod)

    n_out = 2 if epilogue == 'relu2' else 1
    bb = (_nbytes((tm, tk), a.dtype) + _nbytes((tk, tn), b.dtype) + 3 * _nbytes((tm, tn), F32))
    grid = (m // tm, n // tn, nk) if rows_outer else (n // tn, m // tm, nk)
    return _pcall(body, name=name, grid=grid, in_specs=in_specs, out_specs=out_specs, out_shape=out_shape,
                  scratch_shapes=[pltpu.VMEM((tm, tn), F32)] if nk > 1 else [],
                  semantics=("parallel", "parallel", "arbitrary"), block_bytes=bb)(*operands)


def _row_tile(s, d, target_bytes=1 << 20):
    return _tile(s, max(SUBLANES, target_bytes // (4 * d)), SUBLANES)


def _rms_fwd(x, gain, name):
    s, d = x.shape
    tr = _row_tile(s, d)

    def body(x_ref, g_ref, h_ref):
        xv = x_ref[...]
        r = lax.rsqrt(jnp.mean(xv * xv, axis=-1, keepdims=True) + RMS_EPS)
        h_ref[...] = (xv * r * g_ref[...]).astype(BF16)

    row = pl.BlockSpec((tr, d), lambda i: (i, 0))
    return _pcall(body, name=name, grid=(s // tr,), in_specs=[row, pl.BlockSpec((1, d), lambda i: (0, 0))],
                  out_specs=row, out_shape=jax.ShapeDtypeStruct((s, d), BF16), semantics=("parallel",),
                  block_bytes=2 * tr * d * 4)(x, gain.reshape(1, d))


def _rms_bwd(x, gain, dh, dres, name):
    s, d = x.shape
    tr = _row_tile(s, d, 1 << 19)

    def body(x_ref, g_ref, dh_ref, dres_ref, dx_ref, dxb_ref, dg_ref):
        xv = x_ref[...]
        r = lax.rsqrt(jnp.mean(xv * xv, axis=-1, keepdims=True) + RMS_EPS)
        xh = xv * r
        dhv = dh_ref[...]
        dxh = dhv * g_ref[...]
        dx = dres_ref[...] + r * (dxh - xh * jnp.mean(dxh * xh, axis=-1, keepdims=True))
        dx_ref[...] = dx
        dxb_ref[...] = dx.astype(BF16)

        @pl.when(pl.program_id(0) == 0)
        def _():
            dg_ref[...] = jnp.zeros_like(dg_ref)

        dg_ref[...] += jnp.sum(dhv * xh, axis=0, keepdims=True)

    row = pl.BlockSpec((tr, d), lambda i: (i, 0))
    vec = pl.BlockSpec((1, d), lambda i: (0, 0))
    return _pcall(body, name=name, grid=(s // tr,), in_specs=[row, vec, row, row], out_specs=(row, row, vec),
                  out_shape=(jax.ShapeDtypeStruct((s, d), F32), jax.ShapeDtypeStruct((s, d), BF16),
                             jax.ShapeDtypeStruct((1, d), F32)),
                  semantics=("arbitrary",), block_bytes=5 * tr * d * 4)(x, gain.reshape(1, d), dh, dres)


def _loss_head(x, gain, target, name):
    s, d = x.shape
    tr = _row_tile(s, d, 1 << 19)

    def body(x_ref, g_ref, t_ref, loss_ref, dx_ref, dxb_ref, dg_ref):
        xv = x_ref[...]
        r = lax.rsqrt(jnp.mean(xv * xv, axis=-1, keepdims=True) + RMS_EPS)
        xh = xv * r
        gv = g_ref[...]
        err = xh * gv - t_ref[...]
        dy = err * (1.0 / d)
        dxh = dy * gv
        dx = r * (dxh - xh * jnp.mean(dxh * xh, axis=-1, keepdims=True))
        dx_ref[...] = dx
        dxb_ref[...] = dx.astype(BF16)

        @pl.when(pl.program_id(0) == 0)
        def _():
            dg_ref[...] = jnp.zeros_like(dg_ref)
            loss_ref[...] = jnp.zeros_like(loss_ref)

        dg_ref[...] += jnp.sum(dy * xh, axis=0, keepdims=True)
        part = jnp.sum(jnp.sum(err * err, axis=-1, keepdims=True), axis=0, keepdims=True) * (0.5 / d)
        loss_ref[...] += jnp.broadcast_to(part, loss_ref.shape)

    row = pl.BlockSpec((tr, d), lambda i: (i, 0))
    vec = pl.BlockSpec((1, d), lambda i: (0, 0))
    lspec = pl.BlockSpec((SUBLANES, LANES), lambda i: (0, 0))
    return _pcall(body, name=name, grid=(s // tr,), in_specs=[row, vec, row], out_specs=(lspec, row, row, vec),
                  out_shape=(jax.ShapeDtypeStruct((SUBLANES, LANES), F32), jax.ShapeDtypeStruct((s, d), F32),
                             jax.ShapeDtypeStruct((s, d), BF16), jax.ShapeDtypeStruct((1, d), F32)),
                  semantics=("arbitrary",), block_bytes=4 * tr * d * 4)(x, gain.reshape(1, d), target)


def _shift_down(xc, xp, s):
    tr = xc.shape[0]
    r = pltpu.roll(xc, s, 0)
    p = pltpu.roll(xp, s, 0)
    row8 = lax.broadcasted_iota(jnp.int32, (SUBLANES, xc.shape[1]), 0)
    head = jnp.where(row8 < s, p, r[:SUBLANES])
    if tr == SUBLANES:
        return head
    return jnp.concatenate([head, r[SUBLANES:]], axis=0)


def _shift_up(yc, yn, s):
    tr = yc.shape[0]
    u = pltpu.roll(yc, tr - s, 0)
    n = pltpu.roll(yn, SUBLANES - s, 0)
    row8 = lax.broadcasted_iota(jnp.int32, (SUBLANES, yc.shape[1]), 0)
    tail = jnp.where(row8 >= SUBLANES - s, n, u[tr - SUBLANES:])
    if tr == SUBLANES:
        return tail
    return jnp.concatenate([u[:tr - SUBLANES], tail], axis=0)


def _conv_apply(xc, xp, w):
    y = xc * w[CONV_WIDTH - 1:CONV_WIDTH, :]
    for s in range(1, CONV_WIDTH):
        y = y + _shift_down(xc, xp, s) * w[CONV_WIDTH - 1 - s:CONV_WIDTH - s, :]
    return y


def _halo_specs(tr, col_of):
    per = tr // SUBLANES
    cur = pl.BlockSpec((tr, LANES), lambda j, i: (i, col_of(j)))
    prev = pl.BlockSpec((SUBLANES, LANES), lambda j, i: (jnp.maximum(i * per - 1, 0), col_of(j)))
    return cur, prev


def _conv_bias_fwd(x_arr, x_col0, w, bias, name):
    s = x_arr.shape[0]
    ncb = w.shape[1] // LANES
    tr = _tile(s, 512, SUBLANES)

    def body(cur_ref, prev_ref, w_ref, b_ref, o_ref):
        i = pl.program_id(1)
        xp = prev_ref[...] * (i > 0).astype(F32)
        o_ref[...] = _conv_apply(cur_ref[...], xp, w_ref[...]) + b_ref[...]

    cur, prev = _halo_specs(tr, lambda j: x_col0 + j)
    return _pcall(body, name=name, grid=(ncb, s // tr),
                  in_specs=[cur, prev, pl.BlockSpec((CONV_WIDTH, LANES), lambda j, i: (0, j)),
                            pl.BlockSpec((1, LANES), lambda j, i: (0, j))],
                  out_specs=pl.BlockSpec((tr, LANES), lambda j, i: (i, j)),
                  out_shape=jax.ShapeDtypeStruct((s, w.shape[1]), F32), semantics=("parallel", "parallel"),
                  block_bytes=3 * tr * LANES * 4)(x_arr, x_arr, w, bias.reshape(1, -1))


def _conv_bwd(dy, x_arr, x_col0, w, name):
    s, c = dy.shape
    ncb = c // LANES
    tr = _tile(s, 512, SUBLANES)
    per = tr // SUBLANES
    ni = s // tr

    def body(dy_ref, dyn_ref, cur_ref, prev_ref, w_ref, dx_ref, dw_ref, db_ref):
        i = pl.program_id(1)
        dyv = dy_ref[...]
        dn = dyn_ref[...] * (i < ni - 1).astype(F32)
        xc = cur_ref[...]
        xp = prev_ref[...] * (i > 0).astype(F32)
        wv = w_ref[...]

        @pl.when(i == 0)
        def _():
            dw_ref[...] = jnp.zeros_like(dw_ref)
            db_ref[...] = jnp.zeros_like(db_ref)

        dx = dyv * wv[CONV_WIDTH - 1:CONV_WIDTH, :]
        dw_ref[CONV_WIDTH - 1:CONV_WIDTH, :] += jnp.sum(dyv * xc, axis=0, keepdims=True)
        for sh in range(1, CONV_WIDTH):
            j = CONV_WIDTH - 1 - sh
            dx = dx + _shift_up(dyv, dn, sh) * wv[j:j + 1, :]
            dw_ref[j:j + 1, :] += jnp.sum(dyv * _shift_down(xc, xp, sh), axis=0, keepdims=True)
        dx_ref[...] = dx
        db_ref[...] += jnp.sum(dyv, axis=0, keepdims=True)

    cur, prev = _halo_specs(tr, lambda j: x_col0 + j)
    dcur = pl.BlockSpec((tr, LANES), lambda j, i: (i, j))
    dnext = pl.BlockSpec((SUBLANES, LANES), lambda j, i: (jnp.minimum((i + 1) * per, s // SUBLANES - 1), j))
    return _pcall(body, name=name, grid=(ncb, ni),
                  in_specs=[dcur, dnext, cur, prev, pl.BlockSpec((CONV_WIDTH, LANES), lambda j, i: (0, j))],
                  out_specs=(dcur, pl.BlockSpec((CONV_WIDTH, LANES), lambda j, i: (0, j)),
                             pl.BlockSpec((1, LANES), lambda j, i: (0, j))),
                  out_shape=(jax.ShapeDtypeStruct((s, c), F32), jax.ShapeDtypeStruct((CONV_WIDTH, c), F32),
                             jax.ShapeDtypeStruct((1, c), F32)),
                  semantics=("parallel", "arbitrary"), block_bytes=4 * tr * LANES * 4)(dy, dy, x_arr, x_arr, w)


def _gdn_pre_fwd(proj, conv_w, heads, name):
    s = proj.shape[0]
    ncb = conv_w.shape[1] // LANES
    tr = _tile(s, 512, SUBLANES)
    qscale = float(LANES) ** -0.5

    def body(cur_ref, prev_ref, w_ref, o_ref):
        j, i = pl.program_id(0), pl.program_id(1)
        xp = prev_ref[...] * (i > 0).astype(F32)
        cv = _conv_apply(cur_ref[...], xp, w_ref[...])
        sv = cv * _sigmoid(cv)
        nrm = lax.rsqrt(jnp.sum(sv * sv, axis=-1, keepdims=True) + L2_EPS)
        scale = jnp.where(j < heads, qscale, 1.0)
        o_ref[...] = jnp.where(j < 2 * heads, sv * nrm * scale, sv)

    cur, prev = _halo_specs(tr, lambda j: j)
    return _pcall(body, name=name, grid=(ncb, s // tr),
                  in_specs=[cur, prev, pl.BlockSpec((CONV_WIDTH, LANES), lambda j, i: (0, j))],
                  out_specs=pl.BlockSpec((tr, LANES), lambda j, i: (i, j)),
                  out_shape=jax.ShapeDtypeStruct((s, conv_w.shape[1]), F32), semantics=("parallel", "parallel"),
                  block_bytes=3 * tr * LANES * 4)(proj, proj, conv_w)


def _gdn_pre_bwd(proj, conv_w, dqkv, heads, name):
    s = proj.shape[0]
    ncb = conv_w.shape[1] // LANES
    tr = _tile(s, 512, SUBLANES)
    qscale = float(LANES) ** -0.5

    def body(cur_ref, prev_ref, w_ref, d_ref, o_ref):
        j, i = pl.program_id(0), pl.program_id(1)
        xp = prev_ref[...] * (i > 0).astype(F32)
        cv = _conv_apply(cur_ref[...], xp, w_ref[...])
        sg = _sigmoid(cv)
        sv = cv * sg
        nrm = lax.rsqrt(jnp.sum(sv * sv, axis=-1, keepdims=True) + L2_EPS)
        dv = d_ref[...]
        scale = jnp.where(j < heads, qscale, 1.0)
        dsn = scale * nrm * (dv - sv * (nrm * nrm) * jnp.sum(dv * sv, axis=-1, keepdims=True))
        ds = jnp.where(j < 2 * heads, dsn, dv)
        o_ref[...] = ds * (sg * (1.0 + cv * (1.0 - sg)))

    cur, prev = _halo_specs(tr, lambda j: j)
    blk = pl.BlockSpec((tr, LANES), lambda j, i: (i, j))
    return _pcall(body, name=name, grid=(ncb, s // tr),
                  in_specs=[cur, prev, pl.BlockSpec((CONV_WIDTH, LANES), lambda j, i: (0, j)), blk],
                  out_specs=blk, out_shape=jax.ShapeDtypeStruct((s, conv_w.shape[1]), F32),
                  semantics=("parallel", "parallel"), block_bytes=4 * tr * LANES * 4)(proj, proj, conv_w, dqkv)


def _tri_inverse(a_strict, block):
    n = a_strict.shape[0]
    ri = lax.broadcasted_iota(jnp.int32, (n, n), 0)
    ci = lax.broadcasted_iota(jnp.int32, (n, n), 1)
    p = jnp.where(ri == ci, 1.0, 0.0) - a_strict
    if block <= 2:
        return p
    xp = _dot(a_strict, a_strict)
    span = 2
    while True:
        p_next = p + _dot(p, xp)
        span *= 2
        if span >= block:
            return p_next
        xp = _dot(xp, xp)
        p = p_next


GDN_HEAD_GROUP = 4
_CHUNK_SHIFT = GDN_CHUNK.bit_length() - 1
_LANE_SHIFT = LANES.bit_length() - 1


def _stack_heads(ref, hb):
    return jnp.concatenate([ref[:, i * LANES:(i + 1) * LANES] for i in range(hb)], axis=0)


def _diag_blocks(x, hb):
    c = GDN_CHUNK
    return jnp.concatenate([x[i * c:(i + 1) * c, i * LANES:(i + 1) * LANES] for i in range(hb)], axis=0)


def _expand_blocks(y, hb):
    row_blk = lax.shift_right_logical(lax.broadcasted_iota(jnp.int32, y.shape, 0), _CHUNK_SHIFT)
    return jnp.concatenate([jnp.where(row_blk == j, y, 0.0) for j in range(hb)], axis=1)


def _gdn_group_terms(q, k, v, ab, alog, dtb, head0, hb, heads):
    c = GDN_CHUNK
    r = hb * c
    lane = lax.broadcasted_iota(jnp.int32, (1, LANES), 1)

    def column(lane0):
        return jnp.concatenate([jnp.sum(jnp.where(lane == lane0 + head0 + i, ab, 0.0), axis=1, keepdims=True)
                                for i in range(hb)], axis=0)

    def per_head(vec):
        return jnp.concatenate([jnp.broadcast_to(jnp.sum(jnp.where(lane == head0 + i, vec, 0.0), axis=1,
                                                         keepdims=True), (c, 1)) for i in range(hb)], axis=0)

    pre = column(0) + per_head(dtb)
    neg_ea = -jnp.exp(per_head(alog))
    g = neg_ea * _softplus(pre)
    beta = _sigmoid(column(heads))
    ri = lax.broadcasted_iota(jnp.int32, (r, r), 0)
    ci = lax.broadcasted_iota(jnp.int32, (r, r), 1)
    same = lax.shift_right_logical(ri, _CHUNK_SHIFT) == lax.shift_right_logical(ci, _CHUNK_SHIFT)
    eye = ri == ci
    causal = same & (ri >= ci)
    strict = same & (ri > ci)
    g_row = jnp.sum(jnp.where(eye, g, 0.0), axis=0, keepdims=True)
    gc_col = jnp.sum(jnp.where(causal, g_row, 0.0), axis=1, keepdims=True)
    gc_row = jnp.sum(jnp.where(same & (ri <= ci), g, 0.0), axis=0, keepdims=True)
    gl_col = jnp.sum(jnp.where(same, g_row, 0.0), axis=1, keepdims=True)
    decay = jnp.where(causal, jnp.exp(jnp.where(causal, gc_col - gc_row, 0.0)), 0.0)
    e_last_col = jnp.exp(gl_col)
    e_last_lanes = jnp.concatenate([jnp.broadcast_to(e_last_col[i * c:i * c + 1, :], (1, LANES))
                                    for i in range(hb)], axis=1)
    egc = jnp.exp(gc_col)
    ekl = jnp.exp(gl_col - gc_col)
    kb = k * beta
    vb = v * beta
    kk = _dot_nt(kb, k)
    a_strict = jnp.where(strict, kk * decay, 0.0)
    return dict(pre=pre, neg_ea=neg_ea, g=g, beta=beta, ri=ri, ci=ci, same=same, eye=eye, causal=causal,
                strict=strict, decay=decay, e_last_col=e_last_col, e_last_lanes=e_last_lanes, egc=egc, ekl=ekl,
                kb=kb, vb=vb, kk=kk, a_strict=a_strict, lane=lane)


def _gdn_head_group(heads):
    hb = GDN_HEAD_GROUP
    while heads % hb:
        hb //= 2
    return hb


def _gdn_fwd(qkv, proj, alog, dtb, heads, name):
    s = qkv.shape[0]
    c = GDN_CHUNK
    nc = s // c
    ab_blk = proj.shape[1] // LANES - 1
    hb = _gdn_head_group(heads)
    ng = heads // hb
    r = hb * c

    def body(q_ref, k_ref, v_ref, ab_ref, alog_ref, dtb_ref, o_ref, t_ref, s0_ref, state_ref):
        grp, ch = pl.program_id(0), pl.program_id(1)

        @pl.when(ch == 0)
        def _():
            state_ref[...] = jnp.zeros_like(state_ref)

        q, k, v = _stack_heads(q_ref, hb), _stack_heads(k_ref, hb), _stack_heads(v_ref, hb)
        tm = _gdn_group_terms(q, k, v, ab_ref[...], alog_ref[...], dtb_ref[...], grp * hb, hb, heads)
        t_inv = _tri_inverse(tm['a_strict'], c)
        u = _dot(t_inv, tm['vb'])
        w = _dot(t_inv, tm['kb'] * tm['egc'])
        qk = jnp.where(tm['causal'], _dot_nt(q, k) * tm['decay'], 0.0)
        st = state_ref[...]
        v_new = u - _diag_blocks(_dot(w, st), hb)
        out = _diag_blocks(_dot(q * tm['egc'], st), hb) + _dot(qk, v_new)
        for i in range(hb):
            o_ref[:, i * LANES:(i + 1) * LANES] = out[i * c:(i + 1) * c, :]
        t_ref[...] = t_inv
        s0_ref[...] = st
        state_ref[...] = st * tm['e_last_lanes'] + _dot_tn(k * tm['ekl'], _expand_blocks(v_new, hb))

    def blk(off):
        return pl.BlockSpec((c, hb * LANES), lambda g, n: (n, off * ng + g))

    vec = pl.BlockSpec((1, LANES), lambda g, n: (0, 0))
    return _pcall(
        body, name=name, grid=(ng, nc),
        in_specs=[blk(0), blk(1), blk(2), pl.BlockSpec((c, LANES), lambda g, n: (n, ab_blk)), vec, vec],
        out_specs=(blk(0), pl.BlockSpec((None, None, r, r), lambda g, n: (g, n, 0, 0)),
                   pl.BlockSpec((None, None, LANES, hb * LANES), lambda g, n: (g, n, 0, 0))),
        out_shape=(jax.ShapeDtypeStruct((s, heads * LANES), F32), jax.ShapeDtypeStruct((ng, nc, r, r), F32),
                   jax.ShapeDtypeStruct((ng, nc, LANES, hb * LANES), F32)),
        scratch_shapes=[pltpu.VMEM((LANES, hb * LANES), F32)], semantics=("parallel", "arbitrary"),
        block_bytes=8 * r * LANES * 4 + 2 * r * r * 4 + 2 * LANES * hb * LANES * 4)(qkv, qkv, qkv, proj, alog, dtb)


def _gdn_bwd(qkv, proj, alog, dtb, t_all, s0_all, d_o, heads, name):
    s = qkv.shape[0]
    c = GDN_CHUNK
    nc = s // c
    ab_blk = proj.shape[1] // LANES - 1

    hb = _gdn_head_group(heads)
    ng = heads // hb
    r = hb * c

    def body(q_ref, k_ref, v_ref, ab_ref, alog_ref, dtb_ref, t_ref, s0_ref, do_ref,
             dq_ref, dk_ref, dv_ref, dgb_ref, ds_ref):
        grp, step = pl.program_id(0), pl.program_id(1)

        @pl.when(step == 0)
        def _():
            ds_ref[...] = jnp.zeros_like(ds_ref)

        q, k, v = _stack_heads(q_ref, hb), _stack_heads(k_ref, hb), _stack_heads(v_ref, hb)
        do = _stack_heads(do_ref, hb)
        tm = _gdn_group_terms(q, k, v, ab_ref[...], alog_ref[...], dtb_ref[...], grp * hb, hb, heads)
        ri, ci, same, eye = tm['ri'], tm['ci'], tm['same'], tm['eye']
        causal, strict, decay = tm['causal'], tm['strict'], tm['decay']
        egc, ekl, kb, vb, beta = tm['egc'], tm['ekl'], tm['kb'], tm['vb'], tm['beta']
        t_inv = t_ref[...]
        st = s0_ref[...]
        ds_next = ds_ref[...]
        kbg = kb * egc
        u = _dot(t_inv, vb)
        w = _dot(t_inv, kbg)
        qkm = _dot_nt(q, k)
        qk = jnp.where(causal, qkm * decay, 0.0)
        v_new = u - _diag_blocks(_dot(w, st), hb)
        qd = q * egc
        kd = k * ekl
        do_x = _expand_blocks(do, hb)

        dqd = _dot_nt(do_x, st)
        dqk = jnp.where(causal, _dot_nt(do, v_new), 0.0)
        dvn = _dot_tn(qk, do) + _diag_blocks(_dot(kd, ds_next), hb)
        dkd = _dot_nt(_expand_blocks(v_new, hb), ds_next)
        sd = jnp.sum(st * ds_next, axis=0, keepdims=True)
        dgl = jnp.concatenate([jnp.broadcast_to(jnp.sum(sd[:, i * LANES:(i + 1) * LANES], axis=1, keepdims=True),
                                                (c, 1)) for i in range(hb)], axis=0) * tm['e_last_col']
        dvn_x = _expand_blocks(dvn, hb)
        dw = -_dot_nt(dvn_x, st)
        ds_ref[...] = _dot_tn(qd, do_x) + tm['e_last_lanes'] * ds_next - _dot_tn(w, dvn_x)
        dt = _dot_nt(dvn, vb) + _dot_nt(dw, kbg)
        dvb = _dot_tn(t_inv, dvn)
        dkbg = _dot_tn(t_inv, dw)
        da_m = jnp.where(strict, -_dot_tn(t_inv, _dot_nt(dt, t_inv)), 0.0)
        dad = da_m * decay
        dkb = _dot(dad, k) + dkbg * egc
        dqkd = dqk * decay
        dq = _dot(dqkd, k) + dqd * egc
        dk = _dot_tn(dad, kb) + _dot_tn(dqkd, q) + dkd * ekl + dkb * beta
        e_mat = (da_m * tm['kk'] + dqk * qkm) * decay
        s_kd = jnp.sum(dkd * kd, axis=1, keepdims=True)
        s_kd_row = jnp.sum(jnp.where(eye, s_kd, 0.0), axis=0, keepdims=True)
        dgl = dgl + jnp.sum(jnp.where(same, s_kd_row, 0.0), axis=1, keepdims=True)
        col_sum = jnp.sum(e_mat, axis=0, keepdims=True)
        col_sum_c = jnp.sum(jnp.where(eye, col_sum, 0.0), axis=1, keepdims=True)
        dgc = (jnp.sum(e_mat, axis=1, keepdims=True) - col_sum_c + jnp.sum(dqd * qd, axis=1, keepdims=True)
               - s_kd + jnp.sum(dkbg * kbg, axis=1, keepdims=True))
        row_c = lax.broadcasted_iota(jnp.int32, (r, 1), 0)
        dgc = dgc + jnp.where((row_c & (c - 1)) == c - 1, dgl, 0.0)
        dgc_row = jnp.sum(jnp.where(eye, dgc, 0.0), axis=0, keepdims=True)
        dg = jnp.sum(jnp.where(same & (ci >= ri), dgc_row, 0.0), axis=1, keepdims=True)
        dbeta = jnp.sum(dkb * k, axis=1, keepdims=True) + jnp.sum(dvb * v, axis=1, keepdims=True)
        da_pre = dg * tm['neg_ea'] * _sigmoid(tm['pre'])
        db_pre = dbeta * beta * (1.0 - beta)
        lane = tm['lane']
        head_row = grp * hb + lax.shift_right_logical(row_c, _CHUNK_SHIFT)
        dgb = (jnp.where(lane == head_row, da_pre, 0.0) + jnp.where(lane == heads + head_row, db_pre, 0.0)
               + jnp.where(lane == 2 * heads + head_row, dg * tm['g'], 0.0))
        dvv = dvb * beta
        for i in range(hb):
            cols, rows = slice(i * LANES, (i + 1) * LANES), slice(i * c, (i + 1) * c)
            dq_ref[:, cols] = dq[rows, :]
            dk_ref[:, cols] = dk[rows, :]
            dv_ref[:, cols] = dvv[rows, :]
            dgb_ref[:, cols] = dgb[rows, :]

    def blk(off):
        return pl.BlockSpec((c, hb * LANES), lambda g, n: (nc - 1 - n, off * ng + g))

    vec = pl.BlockSpec((1, LANES), lambda g, n: (0, 0))
    gw = heads * LANES
    dq, dk, dv, dgb = _pcall(
        body, name=name, grid=(ng, nc),
        in_specs=[blk(0), blk(1), blk(2), pl.BlockSpec((c, LANES), lambda g, n: (nc - 1 - n, ab_blk)),
                  vec, vec, pl.BlockSpec((None, None, r, r), lambda g, n: (g, nc - 1 - n, 0, 0)),
                  pl.BlockSpec((None, None, LANES, hb * LANES), lambda g, n: (g, nc - 1 - n, 0, 0)), blk(0)],
        out_specs=(blk(0), blk(0), blk(0), blk(0)),
        out_shape=tuple(jax.ShapeDtypeStruct((s, gw), F32) for _ in range(4)),
        scratch_shapes=[pltpu.VMEM((LANES, hb * LANES), F32)], semantics=("parallel", "arbitrary"),
        block_bytes=12 * r * LANES * 4 + 2 * r * r * 4 + 2 * LANES * hb * LANES * 4)(
            qkv, qkv, qkv, proj, alog, dtb, t_all, s0_all, d_o)
    return dq, dk, dv, dgb


def _gdn_post_fwd(o, proj, z_col0, gain, name):
    s, gw = o.shape
    heads = gw // LANES
    tr = _tile(s, 512, SUBLANES)

    def body(o_ref, z_ref, g_ref, y_ref):
        ov, zv = o_ref[...], z_ref[...]
        r = lax.rsqrt(jnp.mean(ov * ov, axis=-1, keepdims=True) + RMS_EPS)
        y_ref[...] = (ov * r * g_ref[...] * (zv * _sigmoid(zv))).astype(BF16)

    blk = pl.BlockSpec((tr, LANES), lambda i, h: (i, h))
    return _pcall(body, name=name, grid=(s // tr, heads),
                  in_specs=[blk, pl.BlockSpec((tr, LANES), lambda i, h: (i, z_col0 + h)),
                            pl.BlockSpec((1, LANES), lambda i, h: (0, 0))],
                  out_specs=blk, out_shape=jax.ShapeDtypeStruct((s, gw), BF16), semantics=("parallel", "parallel"),
                  block_bytes=3 * tr * LANES * 4)(o, proj, gain.reshape(1, LANES))


def _gdn_post_bwd(o, proj, z_col0, gain, dy, name):
    s, gw = o.shape
    heads = gw // LANES
    tr = _tile(s, 512, SUBLANES)

    def body(o_ref, z_ref, g_ref, dy_ref, do_ref, dz_ref, dg_ref):
        ov, zv, gv, dyv = o_ref[...], z_ref[...], g_ref[...], dy_ref[...]
        r = lax.rsqrt(jnp.mean(ov * ov, axis=-1, keepdims=True) + RMS_EPS)
        nv = ov * r
        sg = _sigmoid(zv)
        sz = zv * sg
        dn = dyv * gv * sz
        do_ref[...] = r * (dn - nv * jnp.mean(dn * nv, axis=-1, keepdims=True))
        dz_ref[...] = dyv * nv * gv * (sg * (1.0 + zv * (1.0 - sg)))

        @pl.when((pl.program_id(0) == 0) & (pl.program_id(1) == 0))
        def _():
            dg_ref[...] = jnp.zeros_like(dg_ref)

        dg_ref[...] += jnp.sum(dyv * nv * sz, axis=0, keepdims=True)

    blk = pl.BlockSpec((tr, LANES), lambda i, h: (i, h))
    vec = pl.BlockSpec((1, LANES), lambda i, h: (0, 0))
    return _pcall(body, name=name, grid=(s // tr, heads),
                  in_specs=[blk, pl.BlockSpec((tr, LANES), lambda i, h: (i, z_col0 + h)), vec, blk],
                  out_specs=(blk, blk, vec),
                  out_shape=(jax.ShapeDtypeStruct((s, gw), F32), jax.ShapeDtypeStruct((s, gw), F32),
                             jax.ShapeDtypeStruct((1, LANES), F32)),
                  semantics=("arbitrary", "arbitrary"), block_bytes=6 * tr * LANES * 4)(
                      o, proj, gain.reshape(1, LANES), dy)


def _dab_reduce(dgb, name):
    s, gw = dgb.shape
    heads = gw // LANES
    tr = _tile(s, 512, SUBLANES)

    def body(d_ref, o_ref, cs_ref):
        acc = d_ref[:, 0:LANES]
        for h in range(1, heads):
            acc = acc + d_ref[:, h * LANES:(h + 1) * LANES]
        o_ref[...] = acc

        @pl.when(pl.program_id(0) == 0)
        def _():
            cs_ref[...] = jnp.zeros_like(cs_ref)

        cs_ref[...] += jnp.sum(acc, axis=0, keepdims=True)

    return _pcall(body, name=name, grid=(s // tr,), in_specs=[pl.BlockSpec((tr, gw), lambda i: (i, 0))],
                  out_specs=(pl.BlockSpec((tr, LANES), lambda i: (i, 0)), pl.BlockSpec((1, LANES), lambda i: (0, 0))),
                  out_shape=(jax.ShapeDtypeStruct((s, LANES), F32), jax.ShapeDtypeStruct((1, LANES), F32)),
                  semantics=("arbitrary",), block_bytes=tr * gw * 4)(dgb)


def _lru_gates(xc, wa, wx, ba, bx, lam):
    r = _sigmoid(_dot(xc, wa) + ba)
    ig = _sigmoid(_dot(xc, wx) + bx)
    sp = _softplus(-lam)
    log_a = -LRU_C * r * sp
    a = jnp.exp(log_a)
    e2 = jnp.exp(2.0 * log_a)
    mult = jnp.sqrt(jnp.maximum(1.0 - e2, 0.0))
    return r, ig, sp, a, e2, mult


def _lru_fwd(xc, proj, y_col0, wa, wx, ba, bx, lam, name):
    s, lw = xc.shape
    nb = lw // LANES
    tr = _tile(s, 256, SUBLANES)

    def body(xc_ref, y_ref, wa_ref, wx_ref, ba_ref, bx_ref, lam_ref, h_ref, o_ref, carry_ref):
        @pl.when(pl.program_id(1) == 0)
        def _():
            carry_ref[...] = jnp.zeros_like(carry_ref)

        xv = xc_ref[...]
        _, ig, _, a, _, mult = _lru_gates(xv, wa_ref[...], wx_ref[...], ba_ref[...], bx_ref[...], lam_ref[...])
        b = mult * (ig * xv)
        row = lax.broadcasted_iota(jnp.int32, (tr, LANES), 0)
        sh = 1
        while sh < tr:
            keep = row >= sh
            b = a * jnp.where(keep, pltpu.roll(b, sh, 0), 0.0) + b
            a = a * jnp.where(keep, pltpu.roll(a, sh, 0), 1.0)
            sh *= 2
        hv = a * carry_ref[0:1, :] + b
        h_ref[...] = hv
        carry_ref[...] = jnp.broadcast_to(hv[tr - 1:tr, :], carry_ref.shape)
        gy, _ = _gelu_and_grad(y_ref[...])
        o_ref[...] = (hv * gy).astype(BF16)

    blk = pl.BlockSpec((tr, LANES), lambda n, i: (i, n))
    wspec = pl.BlockSpec((None, LANES, LANES), lambda n, i: (n, 0, 0))
    vec = pl.BlockSpec((1, LANES), lambda n, i: (0, n))
    return _pcall(body, name=name, grid=(nb, s // tr),
                  in_specs=[blk, pl.BlockSpec((tr, LANES), lambda n, i: (i, y_col0 + n)), wspec, wspec, vec, vec, vec],
                  out_specs=(blk, blk),
                  out_shape=(jax.ShapeDtypeStruct((s, lw), F32), jax.ShapeDtypeStruct((s, lw), BF16)),
                  scratch_shapes=[pltpu.VMEM((SUBLANES, LANES), F32)], semantics=("parallel", "arbitrary"),
                  block_bytes=8 * tr * LANES * 4)(xc, proj, wa, wx, ba.reshape(1, lw), bx.reshape(1, lw),
                                                  lam.reshape(1, lw))


def _lru_bwd(d_out, xc, hseq, proj, y_col0, wa, wx, ba, bx, lam, name):
    s, lw = xc.shape
    nb = lw // LANES
    tr = _tile(s, 256, SUBLANES)
    per = tr // SUBLANES
    ni = s // tr
    nrow8 = s // SUBLANES

    def body(do_ref, xc_ref, xn_ref, h_ref, hp_ref, y_ref, wa_ref, wx_ref, ba_ref, bx_ref, lam_ref,
             dxc_ref, dy_ref, dwa_ref, dwx_ref, dba_ref, dbx_ref, dlam_ref, carry_ref):
        step = pl.program_id(1)
        tile = ni - 1 - step

        @pl.when(step == 0)
        def _():
            carry_ref[...] = jnp.zeros_like(carry_ref)
            dwa_ref[...] = jnp.zeros_like(dwa_ref)
            dwx_ref[...] = jnp.zeros_like(dwx_ref)
            dba_ref[...] = jnp.zeros_like(dba_ref)
            dbx_ref[...] = jnp.zeros_like(dbx_ref)
            dlam_ref[...] = jnp.zeros_like(dlam_ref)

        wav, wxv, bav, bxv, lamv = wa_ref[...], wx_ref[...], ba_ref[...], bx_ref[...], lam_ref[...]
        xv = xc_ref[...]
        r, ig, sp, a, e2, mult = _lru_gates(xv, wav, wxv, bav, bxv, lamv)
        a_next = _lru_gates(xn_ref[...], wav, wxv, bav, bxv, lamv)[3] * (tile < ni - 1).astype(F32)
        hv = h_ref[...]
        h_prev = _shift_down(hv, hp_ref[...] * (tile > 0).astype(F32), 1)
        yv = y_ref[...]
        gy, dgy = _gelu_and_grad(yv)
        dov = do_ref[...]
        dy_ref[...] = dov * hv * dgy
        coef = _shift_up(a, a_next, 1)
        bb = dov * gy
        row = lax.broadcasted_iota(jnp.int32, (tr, LANES), 0)
        sh = 1
        while sh < tr:
            keep = row < tr - sh
            bb = coef * jnp.where(keep, pltpu.roll(bb, tr - sh, 0), 0.0) + bb
            coef = coef * jnp.where(keep, pltpu.roll(coef, tr - sh, 0), 1.0)
            sh *= 2
        lam_t = coef * carry_ref[0:1, :] + bb
        carry_ref[...] = jnp.broadcast_to(lam_t[0:1, :], carry_ref.shape)
        d_a = lam_t * h_prev
        d_mult = lam_t * (ig * xv)
        d_ix = lam_t * mult
        d_la = d_a * a - d_mult * e2 / jnp.maximum(mult, 1e-30)
        d_r = d_la * (-LRU_C * sp)
        dlam_ref[...] += jnp.sum(d_la * (LRU_C * r) * _sigmoid(-lamv), axis=0, keepdims=True)
        d_pa = d_r * r * (1.0 - r)
        d_px = (d_ix * xv) * ig * (1.0 - ig)
        dxc_ref[...] = d_ix * ig + _dot_nt(d_pa, wav) + _dot_nt(d_px, wxv)
        dwa_ref[...] += _dot_tn(xv, d_pa)
        dwx_ref[...] += _dot_tn(xv, d_px)
        dba_ref[...] += jnp.sum(d_pa, axis=0, keepdims=True)
        dbx_ref[...] += jnp.sum(d_px, axis=0, keepdims=True)

    blk = pl.BlockSpec((tr, LANES), lambda n, i: (ni - 1 - i, n))
    nxt = pl.BlockSpec((SUBLANES, LANES), lambda n, i: (jnp.minimum((ni - i) * per, nrow8 - 1), n))
    prv = pl.BlockSpec((SUBLANES, LANES), lambda n, i: (jnp.maximum((ni - 1 - i) * per - 1, 0), n))
    wspec = pl.BlockSpec((None, LANES, LANES), lambda n, i: (n, 0, 0))
    vec = pl.BlockSpec((1, LANES), lambda n, i: (0, n))
    return _pcall(
        body, name=name, grid=(nb, ni),
        in_specs=[blk, blk, nxt, blk, prv, pl.BlockSpec((tr, LANES), lambda n, i: (ni - 1 - i, y_col0 + n)),
                  wspec, wspec, vec, vec, vec],
        out_specs=(blk, blk, wspec, wspec, vec, vec, vec),
        out_shape=(jax.ShapeDtypeStruct((s, lw), F32), jax.ShapeDtypeStruct((s, lw), F32),
                   jax.ShapeDtypeStruct((nb, LANES, LANES), F32), jax.ShapeDtypeStruct((nb, LANES, LANES), F32),
                   jax.ShapeDtypeStruct((1, lw), F32), jax.ShapeDtypeStruct((1, lw), F32),
                   jax.ShapeDtypeStruct((1, lw), F32)),
        scratch_shapes=[pltpu.VMEM((SUBLANES, LANES), F32)], semantics=("parallel", "arbitrary"),
        block_bytes=12 * tr * LANES * 4)(d_out, xc, xc, hseq, hseq, proj, wa, wx, ba.reshape(1, lw),
                                         bx.reshape(1, lw), lam.reshape(1, lw))


def _merge_fwd(proj, gg_col0, gl_col0, bg, bl, name):
    s, d = bg.shape
    tr, tc = _tile(s, 256, SUBLANES), _tile(d, 1024)
    cb = tc // LANES

    def body(gg_ref, gl_ref, bg_ref, bl_ref, o_ref):
        o_ref[...] = (_sigmoid(gg_ref[...]) * bg_ref[...] + _sigmoid(gl_ref[...]) * bl_ref[...]).astype(BF16)

    blk = pl.BlockSpec((tr, tc), lambda i, j: (i, j))
    return _pcall(body, name=name, grid=(s // tr, d // tc),
                  in_specs=[pl.BlockSpec((tr, tc), lambda i, j: (i, gg_col0 // cb + j)),
                            pl.BlockSpec((tr, tc), lambda i, j: (i, gl_col0 // cb + j)), blk, blk],
                  out_specs=blk, out_shape=jax.ShapeDtypeStruct((s, d), BF16), semantics=("parallel", "parallel"),
                  block_bytes=5 * tr * tc * 4)(proj, proj, bg, bl)


def _merge_bwd(proj, gg_col0, gl_col0, bg, bl, dm, name):
    s, d = bg.shape
    tr, tc = _tile(s, 256, SUBLANES), _tile(d, 1024)
    cb = tc // LANES

    def body(gg_ref, gl_ref, bg_ref, bl_ref, dm_ref, dgg_ref, dgl_ref, dbg_ref, dbl_ref):
        dmv = dm_ref[...]
        sg, sl = _sigmoid(gg_ref[...]), _sigmoid(gl_ref[...])
        dgg_ref[...] = (dmv * bg_ref[...] * sg * (1.0 - sg)).astype(BF16)
        dgl_ref[...] = (dmv * bl_ref[...] * sl * (1.0 - sl)).astype(BF16)
        dbg_ref[...] = (dmv * sg).astype(BF16)
        dbl_ref[...] = (dmv * sl).astype(BF16)

    blk = pl.BlockSpec((tr, tc), lambda i, j: (i, j))
    sh = jax.ShapeDtypeStruct((s, d), BF16)
    return _pcall(body, name=name, grid=(s // tr, d // tc),
                  in_specs=[pl.BlockSpec((tr, tc), lambda i, j: (i, gg_col0 // cb + j)),
                            pl.BlockSpec((tr, tc), lambda i, j: (i, gl_col0 // cb + j)), blk, blk, blk],
                  out_specs=(blk, blk, blk, blk), out_shape=(sh, sh, sh, sh), semantics=("parallel", "parallel"),
                  block_bytes=8 * tr * tc * 4)(proj, proj, bg, bl, dm)


def _sum_slots(slots, name):
    n, r, c = slots.shape
    tr = _tile(r, max(2 * SUBLANES, (1 << 19) // (c * 4)), 2 * SUBLANES)

    def body(s_ref, o_ref):
        acc = s_ref[0].astype(F32)
        for q in range(1, n):
            acc = acc + s_ref[q].astype(F32)
        o_ref[...] = acc

    return _pcall(body, name=name, grid=(r // tr,), in_specs=[pl.BlockSpec((n, tr, c), lambda i: (0, i, 0))],
                  out_specs=pl.BlockSpec((tr, c), lambda i: (i, 0)), out_shape=jax.ShapeDtypeStruct((r, c), F32),
                  semantics=("parallel",), block_bytes=(n + 1) * tr * c * 4)(slots)


def _adamw(w, g_parts, m, v, name):
    r, c = w.shape
    np_ = len(g_parts)
    tr = _tile(r, max(SUBLANES, (1 << 20) // (c * 4)), SUBLANES)
    c1 = 1.0 - ADAM_B1 ** ADAM_STEP
    c2 = 1.0 - ADAM_B2 ** ADAM_STEP

    def body(*refs):
        w_ref, m_ref, v_ref = refs[0], refs[1 + np_], refs[2 + np_]
        g_ref, d_ref, nm_ref, nv_ref = refs[3 + np_:]
        g = refs[1][...]
        for p in range(1, np_):
            g = g + refs[1 + p][...]
        nm = ADAM_B1 * m_ref[...] + (1.0 - ADAM_B1) * g
        nv = ADAM_B2 * v_ref[...] + (1.0 - ADAM_B2) * (g * g)
        g_ref[...] = g
        nm_ref[...] = nm
        nv_ref[...] = nv
        d_ref[...] = -ADAM_LR * ((nm / c1) / (jnp.sqrt(nv / c2) + ADAM_EPS) + ADAM_WD * w_ref[...])

    blk = pl.BlockSpec((tr, c), lambda i: (i, 0))
    sh = jax.ShapeDtypeStruct((r, c), F32)
    return _pcall(body, name=name, grid=(r // tr,), in_specs=[blk] * (3 + np_), out_specs=(blk,) * 4,
                  out_shape=(sh,) * 4, semantics=("parallel",), block_bytes=(7 + np_) * tr * c * 4)(
                      w, *g_parts, m, v)


def _pair_sum(core, mine, theirs, name):
    _, n, r, c = mine.shape
    tr = _tile(r, max(2 * SUBLANES, (1 << 19) // (c * 4)), 2 * SUBLANES)

    def body(core_ref, a_ref, b_ref, o_ref):
        o_ref[...] = (a_ref[...].astype(F32) + b_ref[...].astype(F32)).astype(BF16)

    return _pcall(body, name=name, grid=(n, r // tr),
                  in_specs=[pl.BlockSpec((None, None, tr, c), lambda q, i, core_ref: (core_ref[0], q, i, 0)),
                            pl.BlockSpec((None, tr, c), lambda q, i, core_ref: (q, i, 0))],
                  out_specs=pl.BlockSpec((None, tr, c), lambda q, i, core_ref: (q, i, 0)),
                  out_shape=jax.ShapeDtypeStruct((n, r, c), BF16), semantics=("parallel", "parallel"),
                  block_bytes=3 * tr * c * 4, scalar_prefetch=1)(core, mine, theirs)


def _adamw_layers(core, w, g_core, g_other, m, v, name):
    nl, r, c = w.shape
    tr = _tile(r, max(SUBLANES, (1 << 20) // (c * 4)), SUBLANES)
    c1 = 1.0 - ADAM_B1 ** ADAM_STEP
    c2 = 1.0 - ADAM_B2 ** ADAM_STEP

    def body(core_ref, w_ref, gc_ref, go_ref, m_ref, v_ref, g_ref, d_ref, nm_ref, nv_ref):
        g = jnp.where(pl.program_id(0) == core_ref[0], gc_ref[...], go_ref[...])
        nm = ADAM_B1 * m_ref[...] + (1.0 - ADAM_B1) * g
        nv = ADAM_B2 * v_ref[...] + (1.0 - ADAM_B2) * (g * g)
        g_ref[...] = g
        nm_ref[...] = nm
        nv_ref[...] = nv
        d_ref[...] = -ADAM_LR * ((nm / c1) / (jnp.sqrt(nv / c2) + ADAM_EPS) + ADAM_WD * w_ref[...])

    blk = pl.BlockSpec((None, tr, c), lambda l, i, core_ref: (l, i, 0))
    gblk = pl.BlockSpec((tr, c), lambda l, i, core_ref: (i, 0))
    sh = jax.ShapeDtypeStruct((nl, r, c), F32)
    return _pcall(body, name=name, grid=(nl, r // tr), in_specs=[blk, gblk, gblk, blk, blk], out_specs=(blk,) * 4,
                  out_shape=(sh,) * 4, semantics=("parallel", "parallel"), block_bytes=9 * tr * c * 4,
                  scalar_prefetch=1)(core, w, g_core, g_other, m, v)


HBM_SPEC = pl.BlockSpec(memory_space=pltpu.HBM)


def _other_chips(x, y):
    return [(1 - x, y), (x, 1 - y), (1 - x, 1 - y)]


def _weight_allgather(shards, name):
    n = len(shards)

    def body(*refs):
        ins, outs = refs[:n], refs[n:2 * n]
        ici_send, ici_recv, d2d_send, d2d_recv, local_sems = refs[2 * n:]
        x, y, c = lax.axis_index("x"), lax.axis_index("y"), lax.axis_index("c")
        me = 2 * x + y
        chips = _other_chips(x, y)

        def ici(t, j, dst_slot):
            px, py = chips[j]
            return pltpu.make_async_remote_copy(
                src_ref=ins[t].at[c], dst_ref=outs[t].at[dst_slot].at[c], send_sem=ici_send.at[3 * t + j],
                recv_sem=ici_recv.at[3 * t + j], device_id=(px, py, c), device_id_type=pl.DeviceIdType.MESH)

        def d2d(t, j, layer):
            px, py = chips[j]
            place = outs[t].at[2 * px + py].at[layer]
            return pltpu.make_async_remote_copy(
                src_ref=place, dst_ref=place, send_sem=d2d_send.at[3 * t + j], recv_sem=d2d_recv.at[3 * t + j],
                device_id=(x, y, 1 - c), device_id_type=pl.DeviceIdType.MESH)

        local, sends = [], []
        for t in range(n):
            lc = pltpu.make_async_copy(ins[t], outs[t].at[me], local_sems.at[t])
            lc.start()
            local.append(lc)
            for j in range(3):
                cp = ici(t, j, me)
                cp.start()
                sends.append(cp)
        for t in range(n):
            for j in range(3):
                px, py = chips[j]
                ici(t, j, 2 * px + py).wait_recv()
                fw = d2d(t, j, c)
                fw.start()
                sends.append(fw)
        for t in range(n):
            for j in range(3):
                d2d(t, j, 1 - c).wait_recv()
        for cp in sends:
            cp.wait_send()
        for lc in local:
            lc.wait()

    out_shape = tuple(jax.ShapeDtypeStruct((N_CHIPS,) + a.shape, a.dtype) for a in shards)
    return pl.pallas_call(
        body, name=name, in_specs=[HBM_SPEC] * n, out_specs=(HBM_SPEC,) * n, out_shape=out_shape,
        scratch_shapes=[pltpu.SemaphoreType.DMA((3 * n,)), pltpu.SemaphoreType.DMA((3 * n,)),
                        pltpu.SemaphoreType.DMA((3 * n,)), pltpu.SemaphoreType.DMA((3 * n,)),
                        pltpu.SemaphoreType.DMA((n,))])(*shards)


def _chip_scatter(arrs, name):
    n = len(arrs)

    def body(*refs):
        ins, outs = refs[:n], refs[n:2 * n]
        send_sems, recv_sems, local_sems = refs[2 * n:]
        x, y, c = lax.axis_index("x"), lax.axis_index("y"), lax.axis_index("c")
        me = 2 * x + y
        chips = _other_chips(x, y)

        def remote(t, j, dst_slot):
            px, py = chips[j]
            return pltpu.make_async_remote_copy(
                src_ref=ins[t].at[2 * px + py], dst_ref=outs[t].at[dst_slot], send_sem=send_sems.at[3 * t + j],
                recv_sem=recv_sems.at[3 * t + j], device_id=(px, py, c), device_id_type=pl.DeviceIdType.MESH)

        local, sends = [], []
        for t in range(n):
            lc = pltpu.make_async_copy(ins[t].at[me], outs[t].at[me], local_sems.at[t])
            lc.start()
            local.append(lc)
            for j in range(3):
                cp = remote(t, j, me)
                cp.start()
                sends.append(cp)
        for t in range(n):
            for j in range(3):
                px, py = chips[j]
                remote(t, j, 2 * px + py).wait_recv()
        for cp in sends:
            cp.wait_send()
        for lc in local:
            lc.wait()

    return pl.pallas_call(
        body, name=name, in_specs=[HBM_SPEC] * n, out_specs=(HBM_SPEC,) * n,
        out_shape=tuple(jax.ShapeDtypeStruct(a.shape, a.dtype) for a in arrs),
        scratch_shapes=[pltpu.SemaphoreType.DMA((3 * n,)), pltpu.SemaphoreType.DMA((3 * n,)),
                        pltpu.SemaphoreType.DMA((n,))])(*arrs)


def _sibling_exchange(arrs, other_layer, name):
    n = len(arrs)

    def body(*refs):
        ins, outs = refs[:n], refs[n:2 * n]
        send_sems, recv_sems = refs[2 * n:]
        c = lax.axis_index("c")
        sib = (lax.axis_index("x"), lax.axis_index("y"), 1 - c)
        copies = [pltpu.make_async_remote_copy(src_ref=ins[t].at[1 - c] if other_layer else ins[t], dst_ref=outs[t],
                                               send_sem=send_sems.at[t], recv_sem=recv_sems.at[t], device_id=sib,
                                               device_id_type=pl.DeviceIdType.MESH) for t in range(n)]
        for cp in copies:
            cp.start()
        for cp in copies:
            cp.wait_recv()
        for cp in copies:
            cp.wait_send()

    return pl.pallas_call(
        body, name=name, in_specs=[HBM_SPEC] * n, out_specs=(HBM_SPEC,) * n,
        out_shape=tuple(jax.ShapeDtypeStruct(a.shape[1:] if other_layer else a.shape, a.dtype) for a in arrs),
        scratch_shapes=[pltpu.SemaphoreType.DMA((n,)), pltpu.SemaphoreType.DMA((n,))])(*arrs)


def _all_devices_gather(buf, name):
    def body(in_ref, out_ref, send_sems, recv_sems, local_sem):
        x, y, c = lax.axis_index("x"), lax.axis_index("y"), lax.axis_index("c")
        me = 4 * x + 2 * y + c

        def peer(mask):
            px = 1 - x if mask & 4 else x
            py = 1 - y if mask & 2 else y
            pc = 1 - c if mask & 1 else c
            return px, py, pc

        def remote(mask, dst_slot):
            return pltpu.make_async_remote_copy(
                src_ref=in_ref, dst_ref=out_ref.at[dst_slot], send_sem=send_sems.at[mask - 1],
                recv_sem=recv_sems.at[mask - 1], device_id=peer(mask), device_id_type=pl.DeviceIdType.MESH)

        lc = pltpu.make_async_copy(in_ref, out_ref.at[me], local_sem)
        lc.start()
        sends = [remote(mask, me) for mask in range(1, N_DEVICES)]
        for cp in sends:
            cp.start()
        for mask in range(1, N_DEVICES):
            px, py, pc = peer(mask)
            remote(mask, 4 * px + 2 * py + pc).wait_recv()
        for cp in sends:
            cp.wait_send()
        lc.wait()

    return pl.pallas_call(
        body, name=name, in_specs=[HBM_SPEC], out_specs=HBM_SPEC,
        out_shape=jax.ShapeDtypeStruct((N_DEVICES,) + buf.shape, buf.dtype),
        scratch_shapes=[pltpu.SemaphoreType.DMA((N_DEVICES - 1,)), pltpu.SemaphoreType.DMA((N_DEVICES - 1,)),
                        pltpu.SemaphoreType.DMA])(buf)


def _pad_lanes(vec):
    return jnp.pad(vec.astype(F32), (0, LANES - vec.shape[0])).reshape(1, LANES)


def _layer_fwd(x, wl, dm, tag):
    heads, gw, lw, d = dm['heads'], dm['gw'], dm['lw'], dm['d']
    h = _rms_fwd(x, wl['attn_norm'], f"rms1_fwd{tag}")
    proj = _matmul(h, wl['w_in_p'], mode='nn', name=f"proj{tag}")
    alog, dtb = _pad_lanes(wl['gdn_a_log']), _pad_lanes(wl['gdn_dt_bias'])
    qkv = _gdn_pre_fwd(proj, wl['gdn_conv_w'], heads, f"gdn_pre_fwd{tag}")
    o, t_all, s0_all = _gdn_fwd(qkv, proj, alog, dtb, heads, f"gdn_fwd{tag}")
    o_gdn = _gdn_post_fwd(o, proj, dm['z_blk'], wl['gdn_norm'], f"gdn_post_fwd{tag}")
    xc = _conv_bias_fwd(proj, dm['xb_blk'], wl['lru_conv_w'], wl['lru_conv_b'], f"lru_conv_fwd{tag}")
    hseq, o_lru = _lru_fwd(xc, proj, dm['yb_blk'], wl['lru_w_a'], wl['lru_w_x'], wl['lru_b_a'], wl['lru_b_x'],
                           wl['lru_lambda'], f"lru_fwd{tag}")
    bg = _matmul(o_gdn, wl['w_branch_gdn'], mode='nn', name=f"branch_gdn{tag}")
    bl = _matmul(o_lru, wl['w_branch_lru'], mode='nn', name=f"branch_lru{tag}")
    merged = _merge_fwd(proj, dm['gg_blk'], dm['gl_blk'], bg, bl, f"merge_fwd{tag}")
    x_mid = _matmul(merged, wl['w_out'], mode='nn', add=x, name=f"out_proj{tag}")
    h2 = _rms_fwd(x_mid, wl['mlp_norm'], f"rms2_fwd{tag}")
    ur, act = _matmul(h2, wl['w_up'], mode='nn', epilogue='relu2', name=f"mlp_up{tag}")
    x_out = _matmul(act, wl['w_down'], mode='nn', add=x_mid, name=f"mlp_down{tag}")
    saved = dict(x=x, h=h, proj=proj, qkv=qkv, o=o, t_all=t_all, s0_all=s0_all, o_gdn=o_gdn, xc=xc, hseq=hseq,
                 o_lru=o_lru, bg=bg, bl=bl, merged=merged, x_mid=x_mid, h2=h2, ur=ur, act=act, alog=alog, dtb=dtb)
    return x_out, saved


def _layer_bwd(dx_out, dx_out_b, wl, sv, dm, tag):
    heads, gw, lw, d = dm['heads'], dm['gw'], dm['lw'], dm['d']
    g = {}
    du = _matmul(dx_out_b, wl['w_down'], mode='nt', epilogue='mul2x', extra=sv['ur'], out_dtype=BF16,
                 name=f"d_mlp_act{tag}")
    g['w_down'] = _matmul(sv['act'], dx_out_b, mode='tn', out_dtype=BF16, name=f"dw_down{tag}")
    g['w_up'] = _matmul(sv['h2'], du, mode='tn', out_dtype=BF16, name=f"dw_up{tag}")
    dh2 = _matmul(du, wl['w_up'], mode='nt', name=f"d_h2{tag}")
    dx_mid, dx_mid_b, g['mlp_norm'] = _rms_bwd(sv['x_mid'], wl['mlp_norm'], dh2, dx_out, f"rms2_bwd{tag}")
    dmerged = _matmul(dx_mid_b, wl['w_out'], mode='nt', name=f"d_merged{tag}")
    g['w_out'] = _matmul(sv['merged'], dx_mid_b, mode='tn', out_dtype=BF16, name=f"dw_out{tag}")
    dgg, dgl, dbg, dbl = _merge_bwd(sv['proj'], dm['gg_blk'], dm['gl_blk'], sv['bg'], sv['bl'], dmerged,
                                    f"merge_bwd{tag}")
    g['w_branch_gdn'] = _matmul(sv['o_gdn'], dbg, mode='tn', out_dtype=BF16, name=f"dw_branch_gdn{tag}")
    g['w_branch_lru'] = _matmul(sv['o_lru'], dbl, mode='tn', out_dtype=BF16, name=f"dw_branch_lru{tag}")
    do_gdn = _matmul(dbg, wl['w_branch_gdn'], mode='nt', name=f"d_o_gdn{tag}")
    do_lru = _matmul(dbl, wl['w_branch_lru'], mode='nt', name=f"d_o_lru{tag}")
    d_o, dz, dgn = _gdn_post_bwd(sv['o'], sv['proj'], dm['z_blk'], wl['gdn_norm'], do_gdn, f"gdn_post_bwd{tag}")
    g['gdn_norm'] = dgn.reshape(-1)
    dq, dk, dv, dgb = _gdn_bwd(sv['qkv'], sv['proj'], sv['alog'], sv['dtb'], sv['t_all'], sv['s0_all'], d_o, heads,
                               f"gdn_bwd{tag}")
    dqkv_n = jnp.concatenate([dq, dk, dv], axis=1)
    dconv = _gdn_pre_bwd(sv['proj'], wl['gdn_conv_w'], dqkv_n, heads, f"gdn_pre_bwd{tag}")
    dqkv, g['gdn_conv_w'], _ = _conv_bwd(dconv, sv['proj'], 0, wl['gdn_conv_w'], f"gdn_conv_bwd{tag}")
    dab, dab_sum = _dab_reduce(dgb, f"dab_reduce{tag}")
    g['gdn_dt_bias'] = dab_sum[0, :heads]
    g['gdn_a_log'] = dab_sum[0, 2 * heads:3 * heads]
    dxc, dyb, g['lru_w_a'], g['lru_w_x'], dba, dbx, dlam = _lru_bwd(
        do_lru, sv['xc'], sv['hseq'], sv['proj'], dm['yb_blk'], wl['lru_w_a'], wl['lru_w_x'], wl['lru_b_a'],
        wl['lru_b_x'], wl['lru_lambda'], f"lru_bwd{tag}")
    g['lru_b_a'], g['lru_b_x'], g['lru_lambda'] = dba.reshape(-1), dbx.reshape(-1), dlam.reshape(-1)
    dxb, g['lru_conv_w'], dcb = _conv_bwd(dxc, sv['proj'], dm['xb_blk'], wl['lru_conv_w'], f"lru_conv_bwd{tag}")
    g['lru_conv_b'] = dcb.reshape(-1)
    dproj = jnp.concatenate([dqkv.astype(BF16), dz.astype(BF16), dxb.astype(BF16), dyb.astype(BF16), dgg, dgl,
                             dab.astype(BF16)], axis=1)
    g['w_in_p'] = _matmul(sv['h'], dproj, mode='tn', out_dtype=BF16, name=f"dw_in{tag}")
    dh = _matmul(dproj, wl['w_in_p'], mode='nt', name=f"d_h{tag}")
    dx_in, dx_in_b, g['attn_norm'] = _rms_bwd(sv['x'], wl['attn_norm'], dh, dx_mid, f"rms1_bwd{tag}")
    g['attn_norm'] = g['attn_norm'].reshape(-1)
    g['mlp_norm'] = g['mlp_norm'].reshape(-1)
    return dx_in, dx_in_b, g


def _dims(d, heads, lw):
    gw = heads * LANES
    nab = 2 * heads
    blk = dict(z_blk=3 * heads, xb_blk=4 * heads, yb_blk=4 * heads + lw // LANES)
    gg0 = 4 * gw + 2 * lw
    return dict(d=d, heads=heads, gw=gw, lw=lw, nab=nab, gg_blk=gg0 // LANES, gl_blk=(gg0 + d) // LANES,
                main=gg0 + 2 * d, np=gg0 + 2 * d + LANES, **blk)


def _pad_w_in(w_in, dm):
    c0 = 4 * dm['gw']
    nab = dm['nab']
    return jnp.concatenate([w_in[:, :c0], w_in[:, c0 + nab:], w_in[:, c0:c0 + nab],
                            jnp.zeros((w_in.shape[0], LANES - nab), w_in.dtype)], axis=1)


def _unpad_w_in(gp, dm):
    c0 = 4 * dm['gw']
    nab = dm['nab']
    main = dm['main']
    return jnp.concatenate([gp[:, :c0], gp[:, main:main + nab], gp[:, c0:main]], axis=1)


def _local_step(x, target, layers, final_norm, dm):
    saved = []
    cur = x
    for li, wl in enumerate(layers):
        cur, sv = _layer_fwd(cur, wl, dm, f"_l{li}")
        saved.append(sv)
    loss_blk, dx, dx_b, dfin = _loss_head(cur, final_norm, target, "loss_head")
    grads = [None] * len(layers)
    for li in reversed(range(len(layers))):
        dx, dx_b, grads[li] = _layer_bwd(dx, dx_b, layers[li], saved[li], dm, f"_l{li}")
    return loss_blk[0, 0], dx, grads, dfin.reshape(-1)


def kernel(x, attn_norm, w_in, gdn_conv_w, gdn_a_log, gdn_dt_bias, gdn_norm, lru_conv_w, lru_conv_b, lru_w_a, lru_b_a, lru_w_x, lru_b_x, lru_lambda, w_branch_gdn, w_branch_lru, w_out, mlp_norm, w_up, w_down, final_norm, loss_target, m_attn_norm, m_w_in, m_gdn_conv_w, m_gdn_a_log, m_gdn_dt_bias, m_gdn_norm, m_lru_conv_w, m_lru_conv_b, m_lru_w_a, m_lru_b_a, m_lru_w_x, m_lru_b_x, m_lru_lambda, m_w_branch_gdn, m_w_branch_lru, m_w_out, m_mlp_norm, m_w_up, m_w_down, m_final_norm, v_attn_norm, v_w_in, v_gdn_conv_w, v_gdn_a_log, v_gdn_dt_bias, v_gdn_norm, v_lru_conv_w, v_lru_conv_b, v_lru_w_a, v_lru_b_a, v_lru_w_x, v_lru_b_x, v_lru_lambda, v_w_branch_gdn, v_w_branch_lru, v_w_out, v_mlp_norm, v_w_up, v_w_down, v_final_norm):
    w = dict(attn_norm=attn_norm, w_in=w_in, gdn_conv_w=gdn_conv_w, gdn_a_log=gdn_a_log, gdn_dt_bias=gdn_dt_bias,
             gdn_norm=gdn_norm, lru_conv_w=lru_conv_w, lru_conv_b=lru_conv_b, lru_w_a=lru_w_a, lru_b_a=lru_b_a,
             lru_w_x=lru_w_x, lru_b_x=lru_b_x, lru_lambda=lru_lambda, w_branch_gdn=w_branch_gdn,
             w_branch_lru=w_branch_lru, w_out=w_out, mlp_norm=mlp_norm, w_up=w_up, w_down=w_down,
             final_norm=final_norm)
    m = dict(attn_norm=m_attn_norm, w_in=m_w_in, gdn_conv_w=m_gdn_conv_w, gdn_a_log=m_gdn_a_log,
             gdn_dt_bias=m_gdn_dt_bias, gdn_norm=m_gdn_norm, lru_conv_w=m_lru_conv_w, lru_conv_b=m_lru_conv_b,
             lru_w_a=m_lru_w_a, lru_b_a=m_lru_b_a, lru_w_x=m_lru_w_x, lru_b_x=m_lru_b_x, lru_lambda=m_lru_lambda,
             w_branch_gdn=m_w_branch_gdn, w_branch_lru=m_w_branch_lru, w_out=m_w_out, mlp_norm=m_mlp_norm,
             w_up=m_w_up, w_down=m_w_down, final_norm=m_final_norm)
    v = dict(attn_norm=v_attn_norm, w_in=v_w_in, gdn_conv_w=v_gdn_conv_w, gdn_a_log=v_gdn_a_log,
             gdn_dt_bias=v_gdn_dt_bias, gdn_norm=v_gdn_norm, lru_conv_w=v_lru_conv_w, lru_conv_b=v_lru_conv_b,
             lru_w_a=v_lru_w_a, lru_b_a=v_lru_b_a, lru_w_x=v_lru_w_x, lru_b_x=v_lru_b_x, lru_lambda=v_lru_lambda,
             w_branch_gdn=v_w_branch_gdn, w_branch_lru=v_w_branch_lru, w_out=v_w_out, mlp_norm=v_mlp_norm,
             w_up=v_w_up, w_down=v_w_down, final_norm=v_final_norm)
    n_layers = attn_norm.shape[0]
    d = x.shape[-1]
    heads = gdn_a_log.shape[-1]
    lw = lru_conv_b.shape[-1]
    dm = _dims(d, heads, lw)
    big_names = list(BIG_SHARD_AXIS)
    conv_names = list(CONV_SHARD_AXIS)
    chip = 2 * lax.axis_index("x") + lax.axis_index("y")

    shards = [w[n].astype(BF16) for n in big_names] + [w[n] for n in conv_names]
    gathered = dict(zip(big_names + conv_names, _weight_allgather(shards, "weight_allgather")))
    shard_axis = {**BIG_SHARD_AXIS, **CONV_SHARD_AXIS}
    layers = []
    for li in range(n_layers):
        wl = {n: w[n][li] for n in SMALL_NAMES if n != 'final_norm' and n not in CONV_SHARD_AXIS}
        for n in conv_names + big_names:
            wl[n] = jnp.concatenate([gathered[n][q, li] for q in range(N_CHIPS)], axis=shard_axis[n] - 1)
        wl['w_in_p'] = _pad_w_in(wl.pop('w_in'), dm)
        layers.append(wl)

    loss_local, dx, grads, dfin = _local_step(x[0], loss_target[0], layers, final_norm, dm)
    loss = lax.psum(loss_local, MESH_AXES)

    core = lax.axis_index("c").astype(jnp.int32).reshape(1)
    contrib = []
    for n in big_names:
        per_layer = [(_unpad_w_in(grads[li]['w_in_p'], dm) if n == 'w_in' else grads[li][n])
                     for li in range(n_layers)]
        contrib.append(jnp.stack([jnp.stack(jnp.split(g, N_CHIPS, axis=BIG_SHARD_AXIS[n] - 1), axis=0)
                                  for g in per_layer], axis=0).astype(BF16))
    theirs = _sibling_exchange(contrib, True, "grad_core_send")
    chip_sum = [_pair_sum(core, mine, th, f"grad_chip_sum_{n}") for n, mine, th in zip(big_names, contrib, theirs)]
    landed = _chip_scatter(chip_sum, "grad_reduce_scatter")
    total = [_sum_slots(lt, f"grad_total_{n}") for n, lt in zip(big_names, landed)]
    other = _sibling_exchange(total, False, "grad_core_exchange")

    small_g = {n: jnp.stack([grads[li][n] for li in range(n_layers)], axis=0)
               for n in SMALL_NAMES if n != 'final_norm'}
    small_g['final_norm'] = dfin
    flat = jnp.concatenate([small_g[n].reshape(-1) for n in SMALL_NAMES])
    n_flat = flat.shape[0]
    row_unit = 32 * SUBLANES
    rows = -(-n_flat // (row_unit * LANES)) * row_unit
    buf = jnp.pad(flat, (0, rows * LANES - n_flat)).reshape(rows, LANES)
    everyone = _all_devices_gather(buf, "small_grad_allgather")
    small_sum = _sum_slots(everyone, "small_grad_sum").reshape(-1)
    small_red = {}
    off = 0
    for n in SMALL_NAMES:
        size = small_g[n].size
        small_red[n] = small_sum[off:off + size].reshape(small_g[n].shape)
        off += size
    for n, ax in CONV_SHARD_AXIS.items():
        width = w[n].shape[ax]
        small_red[n] = lax.dynamic_slice_in_dim(small_red[n], chip * width, width, axis=ax)

    out_g, out_d, out_m, out_v = {}, {}, {}, {}
    for n, g_core, g_other in zip(big_names, total, other):
        out_g[n], out_d[n], out_m[n], out_v[n] = _adamw_layers(core, w[n], g_core, g_other, m[n], v[n], f"adamw_{n}")

    def pack(tree):
        fl = jnp.concatenate([tree[n].reshape(-1) for n in SMALL_NAMES])
        return jnp.pad(fl, (0, rows * LANES - fl.shape[0])).reshape(rows, LANES)

    res = _adamw(pack(w), [pack(small_red)], pack(m), pack(v), "adamw_small")
    for r, dst in zip(res, (out_g, out_d, out_m, out_v)):
        fl = r.reshape(-1)
        off = 0
        for n in SMALL_NAMES:
            dst[n] = fl[off:off + w[n].size].reshape(w[n].shape)
            off += w[n].size

    return (loss, dx[None], *[out_g[n] for n in WEIGHT_NAMES], *[out_d[n] for n in WEIGHT_NAMES],
            *[out_m[n] for n in WEIGHT_NAMES], *[out_v[n] for n in WEIGHT_NAMES])
```

```python
import functools

import jax
import jax.numpy as jnp
from jax import lax
from jax.experimental import pallas as pl
from jax.experimental.pallas import tpu as pltpu

F32 = jnp.float32
BF16 = jnp.bfloat16

LANES = 128
SUBLANES = 8
VMEM_BYTES = 64 * 1024 * 1024
GDN_CHUNK = 64
CONV_WIDTH = 4
RMS_EPS = 1e-6
L2_EPS = 1e-6
LRU_C = 8.0
ADAM_LR = 0.001
ADAM_B1 = 0.9
ADAM_B2 = 0.999
ADAM_EPS = 1e-08
ADAM_WD = 0.01
ADAM_STEP = 10
MESH_AXES = ("x", "y", "c")
N_CHIPS = 4
N_DEVICES = 8

INPUT_NAMES = ['x', 'attn_norm', 'w_in', 'gdn_conv_w', 'gdn_a_log', 'gdn_dt_bias', 'gdn_norm', 'lru_conv_w',
               'lru_conv_b', 'lru_w_a', 'lru_b_a', 'lru_w_x', 'lru_b_x', 'lru_lambda', 'w_branch_gdn',
               'w_branch_lru', 'w_out', 'mlp_norm', 'w_up', 'w_down', 'final_norm']
WEIGHT_NAMES = INPUT_NAMES[1:]
BIG_SHARD_AXIS = {'w_in': 2, 'w_branch_gdn': 2, 'w_branch_lru': 2, 'w_out': 1, 'w_up': 2, 'w_down': 1}
CONV_SHARD_AXIS = {'gdn_conv_w': 2, 'lru_conv_w': 2}
GATHER_GROUPS = {'in': ['w_in'], 'mix': ['w_branch_gdn', 'w_branch_lru', 'w_out'], 'mlp': ['w_up', 'w_down']}
SMALL_NAMES = [n for n in WEIGHT_NAMES if n not in BIG_SHARD_AXIS]


def _tile(n, target, unit=LANES):
    best = None
    t = unit
    while t <= min(n, target):
        if n % t == 0:
            best = t
        t += unit
    return n if best is None else best


def _vmem_limit(block_bytes):
    return int(min(max(3 * block_bytes + (8 << 20), 24 << 20), VMEM_BYTES - (8 << 20)))


def _nbytes(shape, dtype):
    n = 1
    for s in shape:
        n *= s
    return n * jnp.dtype(dtype).itemsize


def _pcall(body, *, name, grid, in_specs, out_specs, out_shape, scratch_shapes=(), semantics=None, block_bytes=0,
           scalar_prefetch=0):
    params = dict(vmem_limit_bytes=_vmem_limit(block_bytes))
    if semantics is not None:
        params['dimension_semantics'] = semantics
    if scalar_prefetch:
        grid_spec = pltpu.PrefetchScalarGridSpec(num_scalar_prefetch=scalar_prefetch, grid=grid, in_specs=in_specs,
                                                 out_specs=out_specs, scratch_shapes=list(scratch_shapes))
        return pl.pallas_call(body, name=name, grid_spec=grid_spec, out_shape=out_shape,
                              compiler_params=pltpu.CompilerParams(**params))
    return pl.pallas_call(body, name=name, grid=grid, in_specs=in_specs, out_specs=out_specs, out_shape=out_shape,
                          scratch_shapes=list(scratch_shapes), compiler_params=pltpu.CompilerParams(**params))


def _dot(a, b):
    return jnp.dot(a.astype(BF16), b.astype(BF16), preferred_element_type=F32)


def _dot_nt(a, b):
    return lax.dot_general(a.astype(BF16), b.astype(BF16), (((1,), (1,)), ((), ())), preferred_element_type=F32)


def _dot_tn(a, b):
    return lax.dot_general(a.astype(BF16), b.astype(BF16), (((0,), (0,)), ((), ())), preferred_element_type=F32)


def _sigmoid(x):
    return 1.0 / (1.0 + jnp.exp(-x))


def _log1p(u):
    return jnp.where(u < 1e-3, u * (1.0 - u * (0.5 - u * (1.0 / 3.0))), jnp.log(1.0 + u))


def _softplus(x):
    return jnp.maximum(x, 0.0) + _log1p(jnp.exp(-jnp.abs(x)))


_GELU_K = 0.7978845608028654


def _gelu_and_grad(x):
    inner = _GELU_K * (x + 0.044715 * x * x * x)
    th = jnp.tanh(inner)
    g = 0.5 * x * (1.0 + th)
    dg = 0.5 * (1.0 + th) + 0.5 * x * (1.0 - th * th) * _GELU_K * (1.0 + 3.0 * 0.044715 * x * x)
    return g, dg


MATMUL_TK_MAX = 3584


def _matmul(a, b, *, mode, name, out_dtype=F32, add=None, epilogue=None, extra=None, tm=512, tn=1024, tk=2048):
    if mode == 'nn':
        (m, k), (k2, n) = a.shape, b.shape
    elif mode == 'nt':
        (m, k), (n, k2) = a.shape, b.shape
    else:
        (k, m), (k2, n) = a.shape, b.shape
    assert k == k2, (a.shape, b.shape, mode)
    tm, tn = _tile(m, tm), _tile(n, tn)
    tk = _tile(k, tk)
    if k // tk > 2 * (-(-k // MATMUL_TK_MAX)):
        tk = _tile(k, MATMUL_TK_MAX)
    nk = k // tk
    dims = {'nn': (((1,), (0,)), ((), ())), 'nt': (((1,), (1,)), ((), ())), 'tn': (((0,), (0,)), ((), ()))}[mode]
    a_bytes, b_bytes = _nbytes(a.shape, a.dtype), _nbytes(b.shape, b.dtype)
    rows_outer = nk > 1 or a_bytes + (m // tm) * b_bytes <= b_bytes + (n // tn) * a_bytes

    def ij(g0, g1):
        return (g0, g1) if rows_outer else (g1, g0)

    def spec(shape, pick):
        return pl.BlockSpec(shape, lambda g0, g1, kk: pick(*ij(g0, g1), kk))

    a_spec = spec((tk, tm), lambda i, j, kk: (kk, i)) if mode == 'tn' else spec((tm, tk), lambda i, j, kk: (i, kk))
    b_spec = spec((tn, tk), lambda i, j, kk: (j, kk)) if mode == 'nt' else spec((tk, tn), lambda i, j, kk: (kk, j))
    o_spec = spec((tm, tn), lambda i, j, kk: (i, j))
    operands, in_specs = [a, b], [a_spec, b_spec]
    if add is not None:
        operands.append(add)
        in_specs.append(o_spec)
    if extra is not None:
        operands.append(extra)
        in_specs.append(o_spec)
    n_in = len(operands)
    if epilogue == 'relu2':
        out_shape = (jax.ShapeDtypeStruct((m, n), BF16), jax.ShapeDtypeStruct((m, n), BF16))
        out_specs = (o_spec, o_spec)
    else:
        out_shape = jax.ShapeDtypeStruct((m, n), out_dtype)
        out_specs = o_spec

    def body(*refs):
        a_ref, b_ref = refs[0], refs[1]
        outs = refs[n_in:n_in + n_out]

        def finish(p):
            if add is not None:
                p = p + refs[2][...]
            if epilogue == 'relu2':
                ur = jnp.maximum(p, 0.0)
                outs[0][...] = ur.astype(BF16)
                outs[1][...] = (ur * ur).astype(BF16)
            elif epilogue == 'mul2x':
                outs[0][...] = (p * 2.0 * refs[n_in - 1][...].astype(F32)).astype(out_dtype)
            else:
                outs[0][...] = p.astype(out_dtype)

        prod = lax.dot_general(a_ref[...].astype(BF16), b_ref[...].astype(BF16), dims, preferred_element_type=F32)
        if nk == 1:
            finish(prod)
            return
        acc_ref = refs[-1]
        kk = pl.program_id(2)

        @pl.when(kk == 0)
        def _():
            acc_ref[...] = prod

        @pl.when((kk > 0) & (kk < nk - 1))
        def _():
            acc_ref[...] += prod

        @pl.when(kk == nk - 1)
        def _():
            finish(acc_ref[...] + prod)

    n_out = 2 if epilogue == 'relu2' else 1
    bb = (_nbytes((tm, tk), a.dtype) + _nbytes((tk, tn), b.dtype) + 3 * _nbytes((tm, tn), F32))
    grid = (m // tm, n // tn, nk) if rows_outer else (n // tn, m // tm, nk)
    return _pcall(body, name=name, grid=grid, in_specs=in_specs, out_specs=out_specs, out_shape=out_shape,
                  scratch_shapes=[pltpu.VMEM((tm, tn), F32)] if nk > 1 else [],
                  semantics=("parallel", "parallel", "arbitrary"), block_bytes=bb)(*operands)


def _row_tile(s, d, target_bytes=1 << 20):
    return _tile(s, max(SUBLANES, target_bytes // (4 * d)), SUBLANES)


def _rms_fwd(x, gain, name):
    s, d = x.shape
    tr = _row_tile(s, d)

    def body(x_ref, g_ref, h_ref):
        xv = x_ref[...]
        r = lax.rsqrt(jnp.mean(xv * xv, axis=-1, keepdims=True) + RMS_EPS)
        h_ref[...] = (xv * r * g_ref[...]).astype(BF16)

    row = pl.BlockSpec((tr, d), lambda i: (i, 0))
    return _pcall(body, name=name, grid=(s // tr,), in_specs=[row, pl.BlockSpec((1, d), lambda i: (0, 0))],
                  out_specs=row, out_shape=jax.ShapeDtypeStruct((s, d), BF16), semantics=("parallel",),
                  block_bytes=2 * tr * d * 4)(x, gain.reshape(1, d))


def _rms_bwd(x, gain, dh, dres, name):
    s, d = x.shape
    tr = _row_tile(s, d, 1 << 19)

    def body(x_ref, g_ref, dh_ref, dres_ref, dx_ref, dxb_ref, dg_ref):
        xv = x_ref[...]
        r = lax.rsqrt(jnp.mean(xv * xv, axis=-1, keepdims=True) + RMS_EPS)
        xh = xv * r
        dhv = dh_ref[...]
        dxh = dhv * g_ref[...]
        dx = dres_ref[...] + r * (dxh - xh * jnp.mean(dxh * xh, axis=-1, keepdims=True))
        dx_ref[...] = dx
        dxb_ref[...] = dx.astype(BF16)

        @pl.when(pl.program_id(0) == 0)
        def _():
            dg_ref[...] = jnp.zeros_like(dg_ref)

        dg_ref[...] += jnp.sum(dhv * xh, axis=0, keepdims=True)

    row = pl.BlockSpec((tr, d), lambda i: (i, 0))
    vec = pl.BlockSpec((1, d), lambda i: (0, 0))
    return _pcall(body, name=name, grid=(s // tr,), in_specs=[row, vec, row, row], out_specs=(row, row, vec),
                  out_shape=(jax.ShapeDtypeStruct((s, d), F32), jax.ShapeDtypeStruct((s, d), BF16),
                             jax.ShapeDtypeStruct((1, d), F32)),
                  semantics=("arbitrary",), block_bytes=5 * tr * d * 4)(x, gain.reshape(1, d), dh, dres)


def _loss_head(x, gain, target, name):
    s, d = x.shape
    tr = _row_tile(s, d, 1 << 19)

    def body(x_ref, g_ref, t_ref, loss_ref, dx_ref, dxb_ref, dg_ref):
        xv = x_ref[...]
        r = lax.rsqrt(jnp.mean(xv * xv, axis=-1, keepdims=True) + RMS_EPS)
        xh = xv * r
        gv = g_ref[...]
        err = xh * gv - t_ref[...]
        dy = err * (1.0 / d)
        dxh = dy * gv
        dx = r * (dxh - xh * jnp.mean(dxh * xh, axis=-1, keepdims=True))
        dx_ref[...] = dx
        dxb_ref[...] = dx.astype(BF16)

        @pl.when(pl.program_id(0) == 0)
        def _():
            dg_ref[...] = jnp.zeros_like(dg_ref)
            loss_ref[...] = jnp.zeros_like(loss_ref)

        dg_ref[...] += jnp.sum(dy * xh, axis=0, keepdims=True)
        part = jnp.sum(jnp.sum(err * err, axis=-1, keepdims=True), axis=0, keepdims=True) * (0.5 / d)
        loss_ref[...] += jnp.broadcast_to(part, loss_ref.shape)

    row = pl.BlockSpec((tr, d), lambda i: (i, 0))
    vec = pl.BlockSpec((1, d), lambda i: (0, 0))
    lspec = pl.BlockSpec((SUBLANES, LANES), lambda i: (0, 0))
    return _pcall(body, name=name, grid=(s // tr,), in_specs=[row, vec, row], out_specs=(lspec, row, row, vec),
                  out_shape=(jax.ShapeDtypeStruct((SUBLANES, LANES), F32), jax.ShapeDtypeStruct((s, d), F32),
                             jax.ShapeDtypeStruct((s, d), BF16), jax.ShapeDtypeStruct((1, d), F32)),
                  semantics=("arbitrary",), block_bytes=4 * tr * d * 4)(x, gain.reshape(1, d), target)


def _shift_down(xc, xp, s):
    tr = xc.shape[0]
    r = pltpu.roll(xc, s, 0)
    p = pltpu.roll(xp, s, 0)
    row8 = lax.broadcasted_iota(jnp.int32, (SUBLANES, xc.shape[1]), 0)
    head = jnp.where(row8 < s, p, r[:SUBLANES])
    if tr == SUBLANES:
        return head
    return jnp.concatenate([head, r[SUBLANES:]], axis=0)


def _shift_up(yc, yn, s):
    tr = yc.shape[0]
    u = pltpu.roll(yc, tr - s, 0)
    n = pltpu.roll(yn, SUBLANES - s, 0)
    row8 = lax.broadcasted_iota(jnp.int32, (SUBLANES, yc.shape[1]), 0)
    tail = jnp.where(row8 >= SUBLANES - s, n, u[tr - SUBLANES:])
    if tr == SUBLANES:
        return tail
    return jnp.concatenate([u[:tr - SUBLANES], tail], axis=0)


def _conv_apply(xc, xp, w):
    y = xc * w[CONV_WIDTH - 1:CONV_WIDTH, :]
    for s in range(1, CONV_WIDTH):
        y = y + _shift_down(xc, xp, s) * w[CONV_WIDTH - 1 - s:CONV_WIDTH - s, :]
    return y


def _halo_specs(tr, col_of):
    per = tr // SUBLANES
    cur = pl.BlockSpec((tr, LANES), lambda j, i: (i, col_of(j)))
    prev = pl.BlockSpec((SUBLANES, LANES), lambda j, i: (jnp.maximum(i * per - 1, 0), col_of(j)))
    return cur, prev


def _conv_bias_fwd(x_arr, x_col0, w, bias, name):
    s = x_arr.shape[0]
    ncb = w.shape[1] // LANES
    tr = _tile(s, 512, SUBLANES)

    def body(cur_ref, prev_ref, w_ref, b_ref, o_ref):
        i = pl.program_id(1)
        xp = prev_ref[...] * (i > 0).astype(F32)
        o_ref[...] = _conv_apply(cur_ref[...], xp, w_ref[...]) + b_ref[...]

    cur, prev = _halo_specs(tr, lambda j: x_col0 + j)
    return _pcall(body, name=name, grid=(ncb, s // tr),
                  in_specs=[cur, prev, pl.BlockSpec((CONV_WIDTH, LANES), lambda j, i: (0, j)),
                            pl.BlockSpec((1, LANES), lambda j, i: (0, j))],
                  out_specs=pl.BlockSpec((tr, LANES), lambda j, i: (i, j)),
                  out_shape=jax.ShapeDtypeStruct((s, w.shape[1]), F32), semantics=("parallel", "parallel"),
                  block_bytes=3 * tr * LANES * 4)(x_arr, x_arr, w, bias.reshape(1, -1))


def _conv_bwd(dy, x_arr, x_col0, w, name):
    s, c = dy.shape
    ncb = c // LANES
    tr = _tile(s, 512, SUBLANES)
    per = tr // SUBLANES
    ni = s // tr

    def body(dy_ref, dyn_ref, cur_ref, prev_ref, w_ref, dx_ref, dw_ref, db_ref):
        i = pl.program_id(1)
        dyv = dy_ref[...]
        dn = dyn_ref[...] * (i < ni - 1).astype(F32)
        xc = cur_ref[...]
        xp = prev_ref[...] * (i > 0).astype(F32)
        wv = w_ref[...]

        @pl.when(i == 0)
        def _():
            dw_ref[...] = jnp.zeros_like(dw_ref)
            db_ref[...] = jnp.zeros_like(db_ref)

        dx = dyv * wv[CONV_WIDTH - 1:CONV_WIDTH, :]
        dw_ref[CONV_WIDTH - 1:CONV_WIDTH, :] += jnp.sum(dyv * xc, axis=0, keepdims=True)
        for sh in range(1, CONV_WIDTH):
            j = CONV_WIDTH - 1 - sh
            dx = dx + _shift_up(dyv, dn, sh) * wv[j:j + 1, :]
            dw_ref[j:j + 1, :] += jnp.sum(dyv * _shift_down(xc, xp, sh), axis=0, keepdims=True)
        dx_ref[...] = dx
        db_ref[...] += jnp.sum(dyv, axis=0, keepdims=True)

    cur, prev = _halo_specs(tr, lambda j: x_col0 + j)
    dcur = pl.BlockSpec((tr, LANES), lambda j, i: (i, j))
    dnext = pl.BlockSpec((SUBLANES, LANES), lambda j, i: (jnp.minimum((i + 1) * per, s // SUBLANES - 1), j))
    return _pcall(body, name=name, grid=(ncb, ni),
                  in_specs=[dcur, dnext, cur, prev, pl.BlockSpec((CONV_WIDTH, LANES), lambda j, i: (0, j))],
                  out_specs=(dcur, pl.BlockSpec((CONV_WIDTH, LANES), lambda j, i: (0, j)),
                             pl.BlockSpec((1, LANES), lambda j, i: (0, j))),
                  out_shape=(jax.ShapeDtypeStruct((s, c), F32), jax.ShapeDtypeStruct((CONV_WIDTH, c), F32),
                             jax.ShapeDtypeStruct((1, c), F32)),
                  semantics=("parallel", "arbitrary"), block_bytes=4 * tr * LANES * 4)(dy, dy, x_arr, x_arr, w)


def _gdn_pre_fwd(proj, conv_w, heads, name):
    s = proj.shape[0]
    ncb = conv_w.shape[1] // LANES
    tr = _tile(s, 512, SUBLANES)
    qscale = float(LANES) ** -0.5

    def body(cur_ref, prev_ref, w_ref, o_ref):
        j, i = pl.program_id(0), pl.program_id(1)
        xp = prev_ref[...] * (i > 0).astype(F32)
        cv = _conv_apply(cur_ref[...], xp, w_ref[...])
        sv = cv * _sigmoid(cv)
        nrm = lax.rsqrt(jnp.sum(sv * sv, axis=-1, keepdims=True) + L2_EPS)
        scale = jnp.where(j < heads, qscale, 1.0)
        o_ref[...] = jnp.where(j < 2 * heads, sv * nrm * scale, sv)

    cur, prev = _halo_specs(tr, lambda j: j)
    return _pcall(body, name=name, grid=(ncb, s // tr),
                  in_specs=[cur, prev, pl.BlockSpec((CONV_WIDTH, LANES), lambda j, i: (0, j))],
                  out_specs=pl.BlockSpec((tr, LANES), lambda j, i: (i, j)),
                  out_shape=jax.ShapeDtypeStruct((s, conv_w.shape[1]), F32), semantics=("parallel", "parallel"),
                  block_bytes=3 * tr * LANES * 4)(proj, proj, conv_w)


def _gdn_pre_bwd(proj, conv_w, dqkv, heads, name):
    s = proj.shape[0]
    ncb = conv_w.shape[1] // LANES
    tr = _tile(s, 512, SUBLANES)
    qscale = float(LANES) ** -0.5

    def body(cur_ref, prev_ref, w_ref, d_ref, o_ref):
        j, i = pl.program_id(0), pl.program_id(1)
        xp = prev_ref[...] * (i > 0).astype(F32)
        cv = _conv_apply(cur_ref[...], xp, w_ref[...])
        sg = _sigmoid(cv)
        sv = cv * sg
        nrm = lax.rsqrt(jnp.sum(sv * sv, axis=-1, keepdims=True) + L2_EPS)
        dv = d_ref[...]
        scale = jnp.where(j < heads, qscale, 1.0)
        dsn = scale * nrm * (dv - sv * (nrm * nrm) * jnp.sum(dv * sv, axis=-1, keepdims=True))
        ds = jnp.where(j < 2 * heads, dsn, dv)
        o_ref[...] = ds * (sg * (1.0 + cv * (1.0 - sg)))

    cur, prev = _halo_specs(tr, lambda j: j)
    blk = pl.BlockSpec((tr, LANES), lambda j, i: (i, j))
    return _pcall(body, name=name, grid=(ncb, s // tr),
                  in_specs=[cur, prev, pl.BlockSpec((CONV_WIDTH, LANES), lambda j, i: (0, j)), blk],
                  out_specs=blk, out_shape=jax.ShapeDtypeStruct((s, conv_w.shape[1]), F32),
                  semantics=("parallel", "parallel"), block_bytes=4 * tr * LANES * 4)(proj, proj, conv_w, dqkv)


def _tri_inverse(a_strict, block):
    n = a_strict.shape[0]
    ri = lax.broadcasted_iota(jnp.int32, (n, n), 0)
    ci = lax.broadcasted_iota(jnp.int32, (n, n), 1)
    p = jnp.where(ri == ci, 1.0, 0.0) - a_strict
    if block <= 2:
        return p
    xp = _dot(a_strict, a_strict)
    span = 2
    while True:
        p_next = p + _dot(p, xp)
        span *= 2
        if span >= block:
            return p_next
        xp = _dot(xp, xp)
        p = p_next


GDN_HEAD_GROUP = 4
_CHUNK_SHIFT = GDN_CHUNK.bit_length() - 1
_LANE_SHIFT = LANES.bit_length() - 1


def _stack_heads(ref, hb):
    return jnp.concatenate([ref[:, i * LANES:(i + 1) * LANES] for i in range(hb)], axis=0)


def _diag_blocks(x, hb):
    c = GDN_CHUNK
    return jnp.concatenate([x[i * c:(i + 1) * c, i * LANES:(i + 1) * LANES] for i in range(hb)], axis=0)


def _expand_blocks(y, hb):
    row_blk = lax.shift_right_logical(lax.broadcasted_iota(jnp.int32, y.shape, 0), _CHUNK_SHIFT)
    return jnp.concatenate([jnp.where(row_blk == j, y, 0.0) for j in range(hb)], axis=1)


def _gdn_group_terms(q, k, v, ab, alog, dtb, head0, hb, heads):
    c = GDN_CHUNK
    r = hb * c
    lane = lax.broadcasted_iota(jnp.int32, (1, LANES), 1)

    def column(lane0):
        return jnp.concatenate([jnp.sum(jnp.where(lane == lane0 + head0 + i, ab, 0.0), axis=1, keepdims=True)
                                for i in range(hb)], axis=0)

    def per_head(vec):
        return jnp.concatenate([jnp.broadcast_to(jnp.sum(jnp.where(lane == head0 + i, vec, 0.0), axis=1,
                                                         keepdims=True), (c, 1)) for i in range(hb)], axis=0)

    pre = column(0) + per_head(dtb)
    neg_ea = -jnp.exp(per_head(alog))
    g = neg_ea * _softplus(pre)
    beta = _sigmoid(column(heads))
    ri = lax.broadcasted_iota(jnp.int32, (r, r), 0)
    ci = lax.broadcasted_iota(jnp.int32, (r, r), 1)
    same = lax.shift_right_logical(ri, _CHUNK_SHIFT) == lax.shift_right_logical(ci, _CHUNK_SHIFT)
    eye = ri == ci
    causal = same & (ri >= ci)
    strict = same & (ri > ci)
    g_row = jnp.sum(jnp.where(eye, g, 0.0), axis=0, keepdims=True)
    gc_col = jnp.sum(jnp.where(causal, g_row, 0.0), axis=1, keepdims=True)
    gc_row = jnp.sum(jnp.where(same & (ri <= ci), g, 0.0), axis=0, keepdims=True)
    gl_col = jnp.sum(jnp.where(same, g_row, 0.0), axis=1, keepdims=True)
    decay = jnp.where(causal, jnp.exp(jnp.where(causal, gc_col - gc_row, 0.0)), 0.0)
    e_last_col = jnp.exp(gl_col)
    e_last_lanes = jnp.concatenate([jnp.broadcast_to(e_last_col[i * c:i * c + 1, :], (1, LANES))
                                    for i in range(hb)], axis=1)
    egc = jnp.exp(gc_col)
    ekl = jnp.exp(gl_col - gc_col)
    kb = k * beta
    vb = v * beta
    kk = _dot_nt(kb, k)
    a_strict = jnp.where(strict, kk * decay, 0.0)
    return dict(pre=pre, neg_ea=neg_ea, g=g, beta=beta, ri=ri, ci=ci, same=same, eye=eye, causal=causal,
                strict=strict, decay=decay, e_last_col=e_last_col, e_last_lanes=e_last_lanes, egc=egc, ekl=ekl,
                kb=kb, vb=vb, kk=kk, a_strict=a_strict, lane=lane)


def _gdn_head_group(heads):
    hb = GDN_HEAD_GROUP
    while heads % hb:
        hb //= 2
    return hb


def _gdn_fwd(qkv, proj, alog, dtb, heads, name):
    s = qkv.shape[0]
    c = GDN_CHUNK
    nc = s // c
    ab_blk = proj.shape[1] // LANES - 1
    hb = _gdn_head_group(heads)
    ng = heads // hb
    r = hb * c

    def body(q_ref, k_ref, v_ref, ab_ref, alog_ref, dtb_ref, o_ref, t_ref, s0_ref, state_ref):
        grp, ch = pl.program_id(0), pl.program_id(1)

        @pl.when(ch == 0)
        def _():
            state_ref[...] = jnp.zeros_like(state_ref)

        q, k, v = _stack_heads(q_ref, hb), _stack_heads(k_ref, hb), _stack_heads(v_ref, hb)
        tm = _gdn_group_terms(q, k, v, ab_ref[...], alog_ref[...], dtb_ref[...], grp * hb, hb, heads)
        t_inv = _tri_inverse(tm['a_strict'], c)
        u = _dot(t_inv, tm['vb'])
        w = _dot(t_inv, tm['kb'] * tm['egc'])
        qk = jnp.where(tm['causal'], _dot_nt(q, k) * tm['decay'], 0.0)
        st = state_ref[...]
        v_new = u - _diag_blocks(_dot(w, st), hb)
        out = _diag_blocks(_dot(q * tm['egc'], st), hb) + _dot(qk, v_new)
        for i in range(hb):
            o_ref[:, i * LANES:(i + 1) * LANES] = out[i * c:(i + 1) * c, :]
        t_ref[...] = t_inv
        s0_ref[...] = st
        state_ref[...] = st * tm['e_last_lanes'] + _dot_tn(k * tm['ekl'], _expand_blocks(v_new, hb))

    def blk(off):
        return pl.BlockSpec((c, hb * LANES), lambda g, n: (n, off * ng + g))

    vec = pl.BlockSpec((1, LANES), lambda g, n: (0, 0))
    return _pcall(
        body, name=name, grid=(ng, nc),
        in_specs=[blk(0), blk(1), blk(2), pl.BlockSpec((c, LANES), lambda g, n: (n, ab_blk)), vec, vec],
        out_specs=(blk(0), pl.BlockSpec((None, None, r, r), lambda g, n: (g, n, 0, 0)),
                   pl.BlockSpec((None, None, LANES, hb * LANES), lambda g, n: (g, n, 0, 0))),
        out_shape=(jax.ShapeDtypeStruct((s, heads * LANES), F32), jax.ShapeDtypeStruct((ng, nc, r, r), F32),
                   jax.ShapeDtypeStruct((ng, nc, LANES, hb * LANES), F32)),
        scratch_shapes=[pltpu.VMEM((LANES, hb * LANES), F32)], semantics=("parallel", "arbitrary"),
        block_bytes=8 * r * LANES * 4 + 2 * r * r * 4 + 2 * LANES * hb * LANES * 4)(qkv, qkv, qkv, proj, alog, dtb)


def _gdn_bwd(qkv, proj, alog, dtb, t_all, s0_all, d_o, heads, name):
    s = qkv.shape[0]
    c = GDN_CHUNK
    nc = s // c
    ab_blk = proj.shape[1] // LANES - 1

    hb = _gdn_head_group(heads)
    ng = heads // hb
    r = hb * c

    def body(q_ref, k_ref, v_ref, ab_ref, alog_ref, dtb_ref, t_ref, s0_ref, do_ref,
             dq_ref, dk_ref, dv_ref, dgb_ref, ds_ref):
        grp, step = pl.program_id(0), pl.program_id(1)

        @pl.when(step == 0)
        def _():
            ds_ref[...] = jnp.zeros_like(ds_ref)

        q, k, v = _stack_heads(q_ref, hb), _stack_heads(k_ref, hb), _stack_heads(v_ref, hb)
        do = _stack_heads(do_ref, hb)
        tm = _gdn_group_terms(q, k, v, ab_ref[...], alog_ref[...], dtb_ref[...], grp * hb, hb, heads)
        ri, ci, same, eye = tm['ri'], tm['ci'], tm['same'], tm['eye']
        causal, strict, decay = tm['causal'], tm['strict'], tm['decay']
        egc, ekl, kb, vb, beta = tm['egc'], tm['ekl'], tm['kb'], tm['vb'], tm['beta']
        t_inv = t_ref[...]
        st = s0_ref[...]
        ds_next = ds_ref[...]
        kbg = kb * egc
        u = _dot(t_inv, vb)
        w = _dot(t_inv, kbg)
        qkm = _dot_nt(q, k)
        qk = jnp.where(causal, qkm * decay, 0.0)
        v_new = u - _diag_blocks(_dot(w, st), hb)
        qd = q * egc
        kd = k * ekl
        do_x = _expand_blocks(do, hb)

        dqd = _dot_nt(do_x, st)
        dqk = jnp.where(causal, _dot_nt(do, v_new), 0.0)
        dvn = _dot_tn(qk, do) + _diag_blocks(_dot(kd, ds_next), hb)
        dkd = _dot_nt(_expand_blocks(v_new, hb), ds_next)
        sd = jnp.sum(st * ds_next, axis=0, keepdims=True)
        dgl = jnp.concatenate([jnp.broadcast_to(jnp.sum(sd[:, i * LANES:(i + 1) * LANES], axis=1, keepdims=True),
                                                (c, 1)) for i in range(hb)], axis=0) * tm['e_last_col']
        dvn_x = _expand_blocks(dvn, hb)
        dw = -_dot_nt(dvn_x, st)
        ds_ref[...] = _dot_tn(qd, do_x) + tm['e_last_lanes'] * ds_next - _dot_tn(w, dvn_x)
        dt = _dot_nt(dvn, vb) + _dot_nt(dw, kbg)
        dvb = _dot_tn(t_inv, dvn)
        dkbg = _dot_tn(t_inv, dw)
        da_m = jnp.where(strict, -_dot_tn(t_inv, _dot_nt(dt, t_inv)), 0.0)
        dad = da_m * decay
        dkb = _dot(dad, k) + dkbg * egc
        dqkd = dqk * decay
        dq = _dot(dqkd, k) + dqd * egc
        dk = _dot_tn(dad, kb) + _dot_tn(dqkd, q) + dkd * ekl + dkb * beta
        e_mat = (da_m * tm['kk'] + dqk * qkm) * decay
        s_kd = jnp.sum(dkd * kd, axis=1, keepdims=True)
        s_kd_row = jnp.sum(jnp.where(eye, s_kd, 0.0), axis=0, keepdims=True)
        dgl = dgl + jnp.sum(jnp.where(same, s_kd_row, 0.0), axis=1, keepdims=True)
        col_sum = jnp.sum(e_mat, axis=0, keepdims=True)
        col_sum_c = jnp.sum(jnp.where(eye, col_sum, 0.0), axis=1, keepdims=True)
        dgc = (jnp.sum(e_mat, axis=1, keepdims=True) - col_sum_c + jnp.sum(dqd * qd, axis=1, keepdims=True)
               - s_kd + jnp.sum(dkbg * kbg, axis=1, keepdims=True))
        row_c = lax.broadcasted_iota(jnp.int32, (r, 1), 0)
        dgc = dgc + jnp.where((row_c & (c - 1)) == c - 1, dgl, 0.0)
        dgc_row = jnp.sum(jnp.where(eye, dgc, 0.0), axis=0, keepdims=True)
        dg = jnp.sum(jnp.where(same & (ci >= ri), dgc_row, 0.0), axis=1, keepdims=True)
        dbeta = jnp.sum(dkb * k, axis=1, keepdims=True) + jnp.sum(dvb * v, axis=1, keepdims=True)
        da_pre = dg * tm['neg_ea'] * _sigmoid(tm['pre'])
        db_pre = dbeta * beta * (1.0 - beta)
        lane = tm['lane']
        head_row = grp * hb + lax.shift_right_logical(row_c, _CHUNK_SHIFT)
        dgb = (jnp.where(lane == head_row, da_pre, 0.0) + jnp.where(lane == heads + head_row, db_pre, 0.0)
               + jnp.where(lane == 2 * heads + head_row, dg * tm['g'], 0.0))
        dvv = dvb * beta
        for i in range(hb):
            cols, rows = slice(i * LANES, (i + 1) * LANES), slice(i * c, (i + 1) * c)
            dq_ref[:, cols] = dq[rows, :]
            dk_ref[:, cols] = dk[rows, :]
            dv_ref[:, cols] = dvv[rows, :]
            dgb_ref[:, cols] = dgb[rows, :]

    def blk(off):
        return pl.BlockSpec((c, hb * LANES), lambda g, n: (nc - 1 - n, off * ng + g))

    vec = pl.BlockSpec((1, LANES), lambda g, n: (0, 0))
    gw = heads * LANES
    dq, dk, dv, dgb = _pcall(
        body, name=name, grid=(ng, nc),
        in_specs=[blk(0), blk(1), blk(2), pl.BlockSpec((c, LANES), lambda g, n: (nc - 1 - n, ab_blk)),
                  vec, vec, pl.BlockSpec((None, None, r, r), lambda g, n: (g, nc - 1 - n, 0, 0)),
                  pl.BlockSpec((None, None, LANES, hb * LANES), lambda g, n: (g, nc - 1 - n, 0, 0)), blk(0)],
        out_specs=(blk(0), blk(0), blk(0), blk(0)),
        out_shape=tuple(jax.ShapeDtypeStruct((s, gw), F32) for _ in range(4)),
        scratch_shapes=[pltpu.VMEM((LANES, hb * LANES), F32)], semantics=("parallel", "arbitrary"),
        block_bytes=12 * r * LANES * 4 + 2 * r * r * 4 + 2 * LANES * hb * LANES * 4)(
            qkv, qkv, qkv, proj, alog, dtb, t_all, s0_all, d_o)
    return dq, dk, dv, dgb


def _gdn_post_fwd(o, proj, z_col0, gain, name):
    s, gw = o.shape
    heads = gw // LANES
    tr = _tile(s, 512, SUBLANES)

    def body(o_ref, z_ref, g_ref, y_ref):
        ov, zv = o_ref[...], z_ref[...]
        r = lax.rsqrt(jnp.mean(ov * ov, axis=-1, keepdims=True) + RMS_EPS)
        y_ref[...] = (ov * r * g_ref[...] * (zv * _sigmoid(zv))).astype(BF16)

    blk = pl.BlockSpec((tr, LANES), lambda i, h: (i, h))
    return _pcall(body, name=name, grid=(s // tr, heads),
                  in_specs=[blk, pl.BlockSpec((tr, LANES), lambda i, h: (i, z_col0 + h)),
                            pl.BlockSpec((1, LANES), lambda i, h: (0, 0))],
                  out_specs=blk, out_shape=jax.ShapeDtypeStruct((s, gw), BF16), semantics=("parallel", "parallel"),
                  block_bytes=3 * tr * LANES * 4)(o, proj, gain.reshape(1, LANES))


def _gdn_post_bwd(o, proj, z_col0, gain, dy, name):
    s, gw = o.shape
    heads = gw // LANES
    tr = _tile(s, 512, SUBLANES)

    def body(o_ref, z_ref, g_ref, dy_ref, do_ref, dz_ref, dg_ref):
        ov, zv, gv, dyv = o_ref[...], z_ref[...], g_ref[...], dy_ref[...]
        r = lax.rsqrt(jnp.mean(ov * ov, axis=-1, keepdims=True) + RMS_EPS)
        nv = ov * r
        sg = _sigmoid(zv)
        sz = zv * sg
        dn = dyv * gv * sz
        do_ref[...] = r * (dn - nv * jnp.mean(dn * nv, axis=-1, keepdims=True))
        dz_ref[...] = dyv * nv * gv * (sg * (1.0 + zv * (1.0 - sg)))

        @pl.when((pl.program_id(0) == 0) & (pl.program_id(1) == 0))
        def _():
            dg_ref[...] = jnp.zeros_like(dg_ref)

        dg_ref[...] += jnp.sum(dyv * nv * sz, axis=0, keepdims=True)

    blk = pl.BlockSpec((tr, LANES), lambda i, h: (i, h))
    vec = pl.BlockSpec((1, LANES), lambda i, h: (0, 0))
    return _pcall(body, name=name, grid=(s // tr, heads),
                  in_specs=[blk, pl.BlockSpec((tr, LANES), lambda i, h: (i, z_col0 + h)), vec, blk],
                  out_specs=(blk, blk, vec),
                  out_shape=(jax.ShapeDtypeStruct((s, gw), F32), jax.ShapeDtypeStruct((s, gw), F32),
                             jax.ShapeDtypeStruct((1, LANES), F32)),
                  semantics=("arbitrary", "arbitrary"), block_bytes=6 * tr * LANES * 4)(
                      o, proj, gain.reshape(1, LANES), dy)


def _dab_reduce(dgb, name):
    s, gw = dgb.shape
    heads = gw // LANES
    tr = _tile(s, 512, SUBLANES)

    def body(d_ref, o_ref, cs_ref):
        acc = d_ref[:, 0:LANES]
        for h in range(1, heads):
            acc = acc + d_ref[:, h * LANES:(h + 1) * LANES]
        o_ref[...] = acc

        @pl.when(pl.program_id(0) == 0)
        def _():
            cs_ref[...] = jnp.zeros_like(cs_ref)

        cs_ref[...] += jnp.sum(acc, axis=0, keepdims=True)

    return _pcall(body, name=name, grid=(s // tr,), in_specs=[pl.BlockSpec((tr, gw), lambda i: (i, 0))],
                  out_specs=(pl.BlockSpec((tr, LANES), lambda i: (i, 0)), pl.BlockSpec((1, LANES), lambda i: (0, 0))),
                  out_shape=(jax.ShapeDtypeStruct((s, LANES), F32), jax.ShapeDtypeStruct((1, LANES), F32)),
                  semantics=("arbitrary",), block_bytes=tr * gw * 4)(dgb)


def _lru_gates(xc, wa, wx, ba, bx, lam):
    r = _sigmoid(_dot(xc, wa) + ba)
    ig = _sigmoid(_dot(xc, wx) + bx)
    sp = _softplus(-lam)
    log_a = -LRU_C * r * sp
    a = jnp.exp(log_a)
    e2 = jnp.exp(2.0 * log_a)
    mult = jnp.sqrt(jnp.maximum(1.0 - e2, 0.0))
    return r, ig, sp, a, e2, mult


def _lru_fwd(xc, proj, y_col0, wa, wx, ba, bx, lam, name):
    s, lw = xc.shape
    nb = lw // LANES
    tr = _tile(s, 256, SUBLANES)

    def body(xc_ref, y_ref, wa_ref, wx_ref, ba_ref, bx_ref, lam_ref, h_ref, o_ref, carry_ref):
        @pl.when(pl.program_id(1) == 0)
        def _():
            carry_ref[...] = jnp.zeros_like(carry_ref)

        xv = xc_ref[...]
        _, ig, _, a, _, mult = _lru_gates(xv, wa_ref[...], wx_ref[...], ba_ref[...], bx_ref[...], lam_ref[...])
        b = mult * (ig * xv)
        row = lax.broadcasted_iota(jnp.int32, (tr, LANES), 0)
        sh = 1
        while sh < tr:
            keep = row >= sh
            b = a * jnp.where(keep, pltpu.roll(b, sh, 0), 0.0) + b
            a = a * jnp.where(keep, pltpu.roll(a, sh, 0), 1.0)
            sh *= 2
        hv = a * carry_ref[0:1, :] + b
        h_ref[...] = hv
        carry_ref[...] = jnp.broadcast_to(hv[tr - 1:tr, :], carry_ref.shape)
        gy, _ = _gelu_and_grad(y_ref[...])
        o_ref[...] = (hv * gy).astype(BF16)

    blk = pl.BlockSpec((tr, LANES), lambda n, i: (i, n))
    wspec = pl.BlockSpec((None, LANES, LANES), lambda n, i: (n, 0, 0))
    vec = pl.BlockSpec((1, LANES), lambda n, i: (0, n))
    return _pcall(body, name=name, grid=(nb, s // tr),
                  in_specs=[blk, pl.BlockSpec((tr, LANES), lambda n, i: (i, y_col0 + n)), wspec, wspec, vec, vec, vec],
                  out_specs=(blk, blk),
                  out_shape=(jax.ShapeDtypeStruct((s, lw), F32), jax.ShapeDtypeStruct((s, lw), BF16)),
                  scratch_shapes=[pltpu.VMEM((SUBLANES, LANES), F32)], semantics=("parallel", "arbitrary"),
                  block_bytes=8 * tr * LANES * 4)(xc, proj, wa, wx, ba.reshape(1, lw), bx.reshape(1, lw),
                                                  lam.reshape(1, lw))


def _lru_bwd(d_out, xc, hseq, proj, y_col0, wa, wx, ba, bx, lam, name):
    s, lw = xc.shape
    nb = lw // LANES
    tr = _tile(s, 256, SUBLANES)
    per = tr // SUBLANES
    ni = s // tr
    nrow8 = s // SUBLANES

    def body(do_ref, xc_ref, xn_ref, h_ref, hp_ref, y_ref, wa_ref, wx_ref, ba_ref, bx_ref, lam_ref,
             dxc_ref, dy_ref, dwa_ref, dwx_ref, dba_ref, dbx_ref, dlam_ref, carry_ref):
        step = pl.program_id(1)
        tile = ni - 1 - step

        @pl.when(step == 0)
        def _():
            carry_ref[...] = jnp.zeros_like(carry_ref)
            dwa_ref[...] = jnp.zeros_like(dwa_ref)
            dwx_ref[...] = jnp.zeros_like(dwx_ref)
            dba_ref[...] = jnp.zeros_like(dba_ref)
            dbx_ref[...] = jnp.zeros_like(dbx_ref)
            dlam_ref[...] = jnp.zeros_like(dlam_ref)

        wav, wxv, bav, bxv, lamv = wa_ref[...], wx_ref[...], ba_ref[...], bx_ref[...], lam_ref[...]
        xv = xc_ref[...]
        r, ig, sp, a, e2, mult = _lru_gates(xv, wav, wxv, bav, bxv, lamv)
        a_next = _lru_gates(xn_ref[...], wav, wxv, bav, bxv, lamv)[3] * (tile < ni - 1).astype(F32)
        hv = h_ref[...]
        h_prev = _shift_down(hv, hp_ref[...] * (tile > 0).astype(F32), 1)
        yv = y_ref[...]
        gy, dgy = _gelu_and_grad(yv)
        dov = do_ref[...]
        dy_ref[...] = dov * hv * dgy
        coef = _shift_up(a, a_next, 1)
        bb = dov * gy
        row = lax.broadcasted_iota(jnp.int32, (tr, LANES), 0)
        sh = 1
        while sh < tr:
            keep = row < tr - sh
            bb = coef * jnp.where(keep, pltpu.roll(bb, tr - sh, 0), 0.0) + bb
            coef = coef * jnp.where(keep, pltpu.roll(coef, tr - sh, 0), 1.0)
            sh *= 2
        lam_t = coef * carry_ref[0:1, :] + bb
        carry_ref[...] = jnp.broadcast_to(lam_t[0:1, :], carry_ref.shape)
        d_a = lam_t * h_prev
        d_mult = lam_t * (ig * xv)
        d_ix = lam_t * mult
        d_la = d_a * a - d_mult * e2 / jnp.maximum(mult, 1e-30)
        d_r = d_la * (-LRU_C * sp)
        dlam_ref[...] += jnp.sum(d_la * (LRU_C * r) * _sigmoid(-lamv), axis=0, keepdims=True)
        d_pa = d_r * r * (1.0 - r)
        d_px = (d_ix * xv) * ig * (1.0 - ig)
        dxc_ref[...] = d_ix * ig + _dot_nt(d_pa, wav) + _dot_nt(d_px, wxv)
        dwa_ref[...] += _dot_tn(xv, d_pa)
        dwx_ref[...] += _dot_tn(xv, d_px)
        dba_ref[...] += jnp.sum(d_pa, axis=0, keepdims=True)
        dbx_ref[...] += jnp.sum(d_px, axis=0, keepdims=True)

    blk = pl.BlockSpec((tr, LANES), lambda n, i: (ni - 1 - i, n))
    nxt = pl.BlockSpec((SUBLANES, LANES), lambda n, i: (jnp.minimum((ni - i) * per, nrow8 - 1), n))
    prv = pl.BlockSpec((SUBLANES, LANES), lambda n, i: (jnp.maximum((ni - 1 - i) * per - 1, 0), n))
    wspec = pl.BlockSpec((None, LANES, LANES), lambda n, i: (n, 0, 0))
    vec = pl.BlockSpec((1, LANES), lambda n, i: (0, n))
    return _pcall(
        body, name=name, grid=(nb, ni),
        in_specs=[blk, blk, nxt, blk, prv, pl.BlockSpec((tr, LANES), lambda n, i: (ni - 1 - i, y_col0 + n)),
                  wspec, wspec, vec, vec, vec],
        out_specs=(blk, blk, wspec, wspec, vec, vec, vec),
        out_shape=(jax.ShapeDtypeStruct((s, lw), F32), jax.ShapeDtypeStruct((s, lw), F32),
                   jax.ShapeDtypeStruct((nb, LANES, LANES), F32), jax.ShapeDtypeStruct((nb, LANES, LANES), F32),
                   jax.ShapeDtypeStruct((1, lw), F32), jax.ShapeDtypeStruct((1, lw), F32),
                   jax.ShapeDtypeStruct((1, lw), F32)),
        scratch_shapes=[pltpu.VMEM((SUBLANES, LANES), F32)], semantics=("parallel", "arbitrary"),
        block_bytes=12 * tr * LANES * 4)(d_out, xc, xc, hseq, hseq, proj, wa, wx, ba.reshape(1, lw),
                                         bx.reshape(1, lw), lam.reshape(1, lw))


def _merge_fwd(proj, gg_col0, gl_col0, bg, bl, name):
    s, d = bg.shape
    tr, tc = _tile(s, 256, SUBLANES), _tile(d, 1024)
    cb = tc // LANES

    def body(gg_ref, gl_ref, bg_ref, bl_ref, o_ref):
        o_ref[...] = (_sigmoid(gg_ref[...]) * bg_ref[...] + _sigmoid(gl_ref[...]) * bl_ref[...]).astype(BF16)

    blk = pl.BlockSpec((tr, tc), lambda i, j: (i, j))
    return _pcall(body, name=name, grid=(s // tr, d // tc),
                  in_specs=[pl.BlockSpec((tr, tc), lambda i, j: (i, gg_col0 // cb + j)),
                            pl.BlockSpec((tr, tc), lambda i, j: (i, gl_col0 // cb + j)), blk, blk],
                  out_specs=blk, out_shape=jax.ShapeDtypeStruct((s, d), BF16), semantics=("parallel", "parallel"),
                  block_bytes=5 * tr * tc * 4)(proj, proj, bg, bl)


def _merge_bwd(proj, gg_col0, gl_col0, bg, bl, dm, name):
    s, d = bg.shape
    tr, tc = _tile(s, 256, SUBLANES), _tile(d, 1024)
    cb = tc // LANES

    def body(gg_ref, gl_ref, bg_ref, bl_ref, dm_ref, dgg_ref, dgl_ref, dbg_ref, dbl_ref):
        dmv = dm_ref[...]
        sg, sl = _sigmoid(gg_ref[...]), _sigmoid(gl_ref[...])
        dgg_ref[...] = (dmv * bg_ref[...] * sg * (1.0 - sg)).astype(BF16)
        dgl_ref[...] = (dmv * bl_ref[...] * sl * (1.0 - sl)).astype(BF16)
        dbg_ref[...] = (dmv * sg).astype(BF16)
        dbl_ref[...] = (dmv * sl).astype(BF16)

    blk = pl.BlockSpec((tr, tc), lambda i, j: (i, j))
    sh = jax.ShapeDtypeStruct((s, d), BF16)
    return _pcall(body, name=name, grid=(s // tr, d // tc),
                  in_specs=[pl.BlockSpec((tr, tc), lambda i, j: (i, gg_col0 // cb + j)),
                            pl.BlockSpec((tr, tc), lambda i, j: (i, gl_col0 // cb + j)), blk, blk, blk],
                  out_specs=(blk, blk, blk, blk), out_shape=(sh, sh, sh, sh), semantics=("parallel", "parallel"),
                  block_bytes=8 * tr * tc * 4)(proj, proj, bg, bl, dm)


def _sum_slots(slots, name):
    n, r, c = slots.shape
    tr = _tile(r, max(2 * SUBLANES, (1 << 19) // (c * 4)), 2 * SUBLANES)

    def body(s_ref, o_ref):
        acc = s_ref[0].astype(F32)
        for q in range(1, n):
            acc = acc + s_ref[q].astype(F32)
        o_ref[...] = acc

    return _pcall(body, name=name, grid=(r // tr,), in_specs=[pl.BlockSpec((n, tr, c), lambda i: (0, i, 0))],
                  out_specs=pl.BlockSpec((tr, c), lambda i: (i, 0)), out_shape=jax.ShapeDtypeStruct((r, c), F32),
                  semantics=("parallel",), block_bytes=(n + 1) * tr * c * 4)(slots)


def _adamw(w, g_parts, m, v, name):
    r, c = w.shape
    np_ = len(g_parts)
    tr = _tile(r, max(SUBLANES, (1 << 20) // (c * 4)), SUBLANES)
    c1 = 1.0 - ADAM_B1 ** ADAM_STEP
    c2 = 1.0 - ADAM_B2 ** ADAM_STEP

    def body(*refs):
        w_ref, m_ref, v_ref = refs[0], refs[1 + np_], refs[2 + np_]
        g_ref, d_ref, nm_ref, nv_ref = refs[3 + np_:]
        g = refs[1][...]
        for p in range(1, np_):
            g = g + refs[1 + p][...]
        nm = ADAM_B1 * m_ref[...] + (1.0 - ADAM_B1) * g
        nv = ADAM_B2 * v_ref[...] + (1.0 - ADAM_B2) * (g * g)
        g_ref[...] = g
        nm_ref[...] = nm
        nv_ref[...] = nv
        d_ref[...] = -ADAM_LR * ((nm / c1) / (jnp.sqrt(nv / c2) + ADAM_EPS) + ADAM_WD * w_ref[...])

    blk = pl.BlockSpec((tr, c), lambda i: (i, 0))
    sh = jax.ShapeDtypeStruct((r, c), F32)
    return _pcall(body, name=name, grid=(r // tr,), in_specs=[blk] * (3 + np_), out_specs=(blk,) * 4,
                  out_shape=(sh,) * 4, semantics=("parallel",), block_bytes=(7 + np_) * tr * c * 4)(
                      w, *g_parts, m, v)


def _pair_sum(core, mine, theirs, name):
    _, n, r, c = mine.shape
    tr = _tile(r, max(2 * SUBLANES, (1 << 19) // (c * 4)), 2 * SUBLANES)

    def body(core_ref, a_ref, b_ref, o_ref):
        o_ref[...] = (a_ref[...].astype(F32) + b_ref[...].astype(F32)).astype(BF16)

    return _pcall(body, name=name, grid=(n, r // tr),
                  in_specs=[pl.BlockSpec((None, None, tr, c), lambda q, i, core_ref: (core_ref[0], q, i, 0)),
                            pl.BlockSpec((None, tr, c), lambda q, i, core_ref: (q, i, 0))],
                  out_specs=pl.BlockSpec((None, tr, c), lambda q, i, core_ref: (q, i, 0)),
                  out_shape=jax.ShapeDtypeStruct((n, r, c), BF16), semantics=("parallel", "parallel"),
                  block_bytes=3 * tr * c * 4, scalar_prefetch=1)(core, mine, theirs)


def _adamw_layers(core, w, g_core, g_other, m, v, name):
    nl, r, c = w.shape
    tr = _tile(r, max(SUBLANES, (1 << 20) // (c * 4)), SUBLANES)
    c1 = 1.0 - ADAM_B1 ** ADAM_STEP
    c2 = 1.0 - ADAM_B2 ** ADAM_STEP

    def body(core_ref, w_ref, gc_ref, go_ref, m_ref, v_ref, g_ref, d_ref, nm_ref, nv_ref):
        g = jnp.where(pl.program_id(0) == core_ref[0], gc_ref[...], go_ref[...])
        nm = ADAM_B1 * m_ref[...] + (1.0 - ADAM_B1) * g
        nv = ADAM_B2 * v_ref[...] + (1.0 - ADAM_B2) * (g * g)
        g_ref[...] = g
        nm_ref[...] = nm
        nv_ref[...] = nv
        d_ref[...] = -ADAM_LR * ((nm / c1) / (jnp.sqrt(nv / c2) + ADAM_EPS) + ADAM_WD * w_ref[...])

    blk = pl.BlockSpec((None, tr, c), lambda l, i, core_ref: (l, i, 0))
    gblk = pl.BlockSpec((tr, c), lambda l, i, core_ref: (i, 0))
    sh = jax.ShapeDtypeStruct((nl, r, c), F32)
    return _pcall(body, name=name, grid=(nl, r // tr), in_specs=[blk, gblk, gblk, blk, blk], out_specs=(blk,) * 4,
                  out_shape=(sh,) * 4, semantics=("parallel", "parallel"), block_bytes=9 * tr * c * 4,
                  scalar_prefetch=1)(core, w, g_core, g_other, m, v)


HBM_SPEC = pl.BlockSpec(memory_space=pltpu.HBM)


def _other_chips(x, y):
    return [(1 - x, y), (x, 1 - y), (1 - x, 1 - y)]


SEM_SPEC = pl.BlockSpec(memory_space=pltpu.SEMAPHORE)
DATAFLOW_EFFECT = pltpu.SideEffectType.DATAFLOW_SIDE_EFFECTING


def _split_start(name, bufs, n_copies, build):
    nb = len(bufs)

    def body(*refs):
        starts, _ = build(refs[:nb], refs[nb], refs[nb + 1])
        for cp in starts:
            cp.start()
        refs[-1][...] = jnp.zeros_like(refs[-1])

    out = pl.pallas_call(
        body, name=name,
        out_shape=(pltpu.SemaphoreType.DMA((n_copies,)), pltpu.SemaphoreType.DMA((n_copies,)),
                   *[pltpu.HBM(b.shape, b.dtype) for b in bufs], jax.ShapeDtypeStruct((SUBLANES, LANES), F32)),
        in_specs=[HBM_SPEC] * nb,
        out_specs=(SEM_SPEC, SEM_SPEC, *[HBM_SPEC] * nb, pl.BlockSpec(memory_space=pltpu.VMEM)),
        input_output_aliases={i: 2 + i for i in range(nb)},
        compiler_params=pltpu.CompilerParams(has_side_effects=DATAFLOW_EFFECT),
    )(*[pltpu.with_memory_space_constraint(b, pltpu.HBM) for b in bufs])
    return out[0], out[1], list(out[2:2 + nb]), out[2 + nb]


def _split_wait(name, send_sems, recv_sems, bufs, after, build):
    nb = len(bufs)

    def body(*refs):
        starts, waits = build(refs[:nb], refs[nb], refs[nb + 1])
        for cp in starts:
            cp.wait_send()
        for cp in waits:
            cp.wait_recv()

    out = pl.pallas_call(
        body, name=name, out_shape=tuple(pltpu.HBM(b.shape, b.dtype) for b in bufs),
        in_specs=[HBM_SPEC] * nb + [SEM_SPEC, SEM_SPEC, pl.BlockSpec(memory_space=pl.ANY)],
        out_specs=tuple([HBM_SPEC] * nb), input_output_aliases={i: i for i in range(nb)},
        compiler_params=pltpu.CompilerParams(has_side_effects=DATAFLOW_EFFECT),
    )(*bufs, send_sems, recv_sems, after)
    return list(out)


def _gather_ici_copies(nt, refs, send_sems, recv_sems):
    srcs, lands = refs[:nt], refs[nt:]
    x, y, c = lax.axis_index("x"), lax.axis_index("y"), lax.axis_index("c")
    me = 2 * x + y
    starts, waits = [], []
    for t in range(nt):
        for j, (px, py) in enumerate(_other_chips(x, y)):
            def copy(slot, t=t, j=j, px=px, py=py):
                return pltpu.make_async_remote_copy(
                    src_ref=srcs[t].at[c], dst_ref=lands[t].at[slot].at[c], send_sem=send_sems.at[3 * t + j],
                    recv_sem=recv_sems.at[3 * t + j], device_id=(px, py, c), device_id_type=pl.DeviceIdType.MESH)
            starts.append(copy(me))
            waits.append(copy(2 * px + py))
    return starts, waits


def _gather_d2d_copies(nt, refs, send_sems, recv_sems):
    x, y, c = lax.axis_index("x"), lax.axis_index("y"), lax.axis_index("c")
    starts, waits = [], []
    for t in range(nt):
        for j, (px, py) in enumerate(_other_chips(x, y)):
            def copy(half, t=t, j=j, px=px, py=py):
                place = refs[t].at[2 * px + py].at[half]
                return pltpu.make_async_remote_copy(
                    src_ref=place, dst_ref=place, send_sem=send_sems.at[3 * t + j], recv_sem=recv_sems.at[3 * t + j],
                    device_id=(x, y, 1 - c), device_id_type=pl.DeviceIdType.MESH)
            starts.append(copy(c))
            waits.append(copy(1 - c))
    return starts, waits


def _chip_scatter(arrs, name):
    n = len(arrs)

    def body(*refs):
        ins, outs = refs[:n], refs[n:2 * n]
        send_sems, recv_sems, local_sems = refs[2 * n:]
        x, y, c = lax.axis_index("x"), lax.axis_index("y"), lax.axis_index("c")
        me = 2 * x + y
        chips = _other_chips(x, y)

        def remote(t, j, dst_slot):
            px, py = chips[j]
            return pltpu.make_async_remote_copy(
                src_ref=ins[t].at[2 * px + py], dst_ref=outs[t].at[dst_slot], send_sem=send_sems.at[3 * t + j],
                recv_sem=recv_sems.at[3 * t + j], device_id=(px, py, c), device_id_type=pl.DeviceIdType.MESH)

        local, sends = [], []
        for t in range(n):
            lc = pltpu.make_async_copy(ins[t].at[me], outs[t].at[me], local_sems.at[t])
            lc.start()
            local.append(lc)
            for j in range(3):
                cp = remote(t, j, me)
                cp.start()
                sends.append(cp)
        for t in range(n):
            for j in range(3):
                px, py = chips[j]
                remote(t, j, 2 * px + py).wait_recv()
        for cp in sends:
            cp.wait_send()
        for lc in local:
            lc.wait()

    return pl.pallas_call(
        body, name=name, in_specs=[HBM_SPEC] * n, out_specs=(HBM_SPEC,) * n,
        out_shape=tuple(jax.ShapeDtypeStruct(a.shape, a.dtype) for a in arrs),
        scratch_shapes=[pltpu.SemaphoreType.DMA((3 * n,)), pltpu.SemaphoreType.DMA((3 * n,)),
                        pltpu.SemaphoreType.DMA((n,))])(*arrs)


def _sibling_exchange(arrs, other_layer, name):
    n = len(arrs)

    def body(*refs):
        ins, outs = refs[:n], refs[n:2 * n]
        send_sems, recv_sems = refs[2 * n:]
        c = lax.axis_index("c")
        sib = (lax.axis_index("x"), lax.axis_index("y"), 1 - c)
        copies = [pltpu.make_async_remote_copy(src_ref=ins[t].at[1 - c] if other_layer else ins[t], dst_ref=outs[t],
                                               send_sem=send_sems.at[t], recv_sem=recv_sems.at[t], device_id=sib,
                                               device_id_type=pl.DeviceIdType.MESH) for t in range(n)]
        for cp in copies:
            cp.start()
        for cp in copies:
            cp.wait_recv()
        for cp in copies:
            cp.wait_send()

    return pl.pallas_call(
        body, name=name, in_specs=[HBM_SPEC] * n, out_specs=(HBM_SPEC,) * n,
        out_shape=tuple(jax.ShapeDtypeStruct(a.shape[1:] if other_layer else a.shape, a.dtype) for a in arrs),
        scratch_shapes=[pltpu.SemaphoreType.DMA((n,)), pltpu.SemaphoreType.DMA((n,))])(*arrs)


def _all_devices_gather(buf, name):
    def body(in_ref, out_ref, send_sems, recv_sems, local_sem):
        x, y, c = lax.axis_index("x"), lax.axis_index("y"), lax.axis_index("c")
        me = 4 * x + 2 * y + c

        def peer(mask):
            px = 1 - x if mask & 4 else x
            py = 1 - y if mask & 2 else y
            pc = 1 - c if mask & 1 else c
            return px, py, pc

        def remote(mask, dst_slot):
            return pltpu.make_async_remote_copy(
                src_ref=in_ref, dst_ref=out_ref.at[dst_slot], send_sem=send_sems.at[mask - 1],
                recv_sem=recv_sems.at[mask - 1], device_id=peer(mask), device_id_type=pl.DeviceIdType.MESH)

        lc = pltpu.make_async_copy(in_ref, out_ref.at[me], local_sem)
        lc.start()
        sends = [remote(mask, me) for mask in range(1, N_DEVICES)]
        for cp in sends:
            cp.start()
        for mask in range(1, N_DEVICES):
            px, py, pc = peer(mask)
            remote(mask, 4 * px + 2 * py + pc).wait_recv()
        for cp in sends:
            cp.wait_send()
        lc.wait()

    return pl.pallas_call(
        body, name=name, in_specs=[HBM_SPEC], out_specs=HBM_SPEC,
        out_shape=jax.ShapeDtypeStruct((N_DEVICES,) + buf.shape, buf.dtype),
        scratch_shapes=[pltpu.SemaphoreType.DMA((N_DEVICES - 1,)), pltpu.SemaphoreType.DMA((N_DEVICES - 1,)),
                        pltpu.SemaphoreType.DMA])(buf)


def _pad_lanes(vec):
    return jnp.pad(vec.astype(F32), (0, LANES - vec.shape[0])).reshape(1, LANES)


def _layer_fwd(x, wl, fetch, dm, tag):
    heads, gw, lw, d = dm['heads'], dm['gw'], dm['lw'], dm['d']
    h = _rms_fwd(x, wl['attn_norm'], f"rms1_fwd{tag}")
    wl.update(fetch('in', h))
    proj = _matmul(h, wl['w_in_p'], mode='nn', name=f"proj{tag}")
    alog, dtb = _pad_lanes(wl['gdn_a_log']), _pad_lanes(wl['gdn_dt_bias'])
    qkv = _gdn_pre_fwd(proj, wl['gdn_conv_w'], heads, f"gdn_pre_fwd{tag}")
    o, t_all, s0_all = _gdn_fwd(qkv, proj, alog, dtb, heads, f"gdn_fwd{tag}")
    o_gdn = _gdn_post_fwd(o, proj, dm['z_blk'], wl['gdn_norm'], f"gdn_post_fwd{tag}")
    xc = _conv_bias_fwd(proj, dm['xb_blk'], wl['lru_conv_w'], wl['lru_conv_b'], f"lru_conv_fwd{tag}")
    hseq, o_lru = _lru_fwd(xc, proj, dm['yb_blk'], wl['lru_w_a'], wl['lru_w_x'], wl['lru_b_a'], wl['lru_b_x'],
                           wl['lru_lambda'], f"lru_fwd{tag}")
    wl.update(fetch('mix', o))
    bg = _matmul(o_gdn, wl['w_branch_gdn'], mode='nn', name=f"branch_gdn{tag}")
    bl = _matmul(o_lru, wl['w_branch_lru'], mode='nn', name=f"branch_lru{tag}")
    merged = _merge_fwd(proj, dm['gg_blk'], dm['gl_blk'], bg, bl, f"merge_fwd{tag}")
    wl.update(fetch('mlp', bg))
    x_mid = _matmul(merged, wl['w_out'], mode='nn', add=x, name=f"out_proj{tag}")
    h2 = _rms_fwd(x_mid, wl['mlp_norm'], f"rms2_fwd{tag}")
    ur, act = _matmul(h2, wl['w_up'], mode='nn', epilogue='relu2', name=f"mlp_up{tag}")
    x_out = _matmul(act, wl['w_down'], mode='nn', add=x_mid, name=f"mlp_down{tag}")
    saved = dict(x=x, h=h, proj=proj, qkv=qkv, o=o, t_all=t_all, s0_all=s0_all, o_gdn=o_gdn, xc=xc, hseq=hseq,
                 o_lru=o_lru, bg=bg, bl=bl, merged=merged, x_mid=x_mid, h2=h2, ur=ur, act=act, alog=alog, dtb=dtb)
    return x_out, saved


def _layer_bwd(dx_out, dx_out_b, wl, sv, dm, tag):
    heads, gw, lw, d = dm['heads'], dm['gw'], dm['lw'], dm['d']
    g = {}
    du = _matmul(dx_out_b, wl['w_down'], mode='nt', epilogue='mul2x', extra=sv['ur'], out_dtype=BF16,
                 name=f"d_mlp_act{tag}")
    g['w_down'] = _matmul(sv['act'], dx_out_b, mode='tn', out_dtype=BF16, name=f"dw_down{tag}")
    g['w_up'] = _matmul(sv['h2'], du, mode='tn', out_dtype=BF16, name=f"dw_up{tag}")
    dh2 = _matmul(du, wl['w_up'], mode='nt', name=f"d_h2{tag}")
    dx_mid, dx_mid_b, g['mlp_norm'] = _rms_bwd(sv['x_mid'], wl['mlp_norm'], dh2, dx_out, f"rms2_bwd{tag}")
    dmerged = _matmul(dx_mid_b, wl['w_out'], mode='nt', name=f"d_merged{tag}")
    g['w_out'] = _matmul(sv['merged'], dx_mid_b, mode='tn', out_dtype=BF16, name=f"dw_out{tag}")
    dgg, dgl, dbg, dbl = _merge_bwd(sv['proj'], dm['gg_blk'], dm['gl_blk'], sv['bg'], sv['bl'], dmerged,
                                    f"merge_bwd{tag}")
    g['w_branch_gdn'] = _matmul(sv['o_gdn'], dbg, mode='tn', out_dtype=BF16, name=f"dw_branch_gdn{tag}")
    g['w_branch_lru'] = _matmul(sv['o_lru'], dbl, mode='tn', out_dtype=BF16, name=f"dw_branch_lru{tag}")
    do_gdn = _matmul(dbg, wl['w_branch_gdn'], mode='nt', name=f"d_o_gdn{tag}")
    do_lru = _matmul(dbl, wl['w_branch_lru'], mode='nt', name=f"d_o_lru{tag}")
    d_o, dz, dgn = _gdn_post_bwd(sv['o'], sv['proj'], dm['z_blk'], wl['gdn_norm'], do_gdn, f"gdn_post_bwd{tag}")
    g['gdn_norm'] = dgn.reshape(-1)
    dq, dk, dv, dgb = _gdn_bwd(sv['qkv'], sv['proj'], sv['alog'], sv['dtb'], sv['t_all'], sv['s0_all'], d_o, heads,
                               f"gdn_bwd{tag}")
    dqkv_n = jnp.concatenate([dq, dk, dv], axis=1)
    dconv = _gdn_pre_bwd(sv['proj'], wl['gdn_conv_w'], dqkv_n, heads, f"gdn_pre_bwd{tag}")
    dqkv, g['gdn_conv_w'], _ = _conv_bwd(dconv, sv['proj'], 0, wl['gdn_conv_w'], f"gdn_conv_bwd{tag}")
    dab, dab_sum = _dab_reduce(dgb, f"dab_reduce{tag}")
    g['gdn_dt_bias'] = dab_sum[0, :heads]
    g['gdn_a_log'] = dab_sum[0, 2 * heads:3 * heads]
    dxc, dyb, g['lru_w_a'], g['lru_w_x'], dba, dbx, dlam = _lru_bwd(
        do_lru, sv['xc'], sv['hseq'], sv['proj'], dm['yb_blk'], wl['lru_w_a'], wl['lru_w_x'], wl['lru_b_a'],
        wl['lru_b_x'], wl['lru_lambda'], f"lru_bwd{tag}")
    g['lru_b_a'], g['lru_b_x'], g['lru_lambda'] = dba.reshape(-1), dbx.reshape(-1), dlam.reshape(-1)
    dxb, g['lru_conv_w'], dcb = _conv_bwd(dxc, sv['proj'], dm['xb_blk'], wl['lru_conv_w'], f"lru_conv_bwd{tag}")
    g['lru_conv_b'] = dcb.reshape(-1)
    dproj = jnp.concatenate([dqkv.astype(BF16), dz.astype(BF16), dxb.astype(BF16), dyb.astype(BF16), dgg, dgl,
                             dab.astype(BF16)], axis=1)
    g['w_in_p'] = _matmul(sv['h'], dproj, mode='tn', out_dtype=BF16, name=f"dw_in{tag}")
    dh = _matmul(dproj, wl['w_in_p'], mode='nt', name=f"d_h{tag}")
    dx_in, dx_in_b, g['attn_norm'] = _rms_bwd(sv['x'], wl['attn_norm'], dh, dx_mid, f"rms1_bwd{tag}")
    g['attn_norm'] = g['attn_norm'].reshape(-1)
    g['mlp_norm'] = g['mlp_norm'].reshape(-1)
    return dx_in, dx_in_b, g


def _dims(d, heads, lw):
    gw = heads * LANES
    nab = 2 * heads
    blk = dict(z_blk=3 * heads, xb_blk=4 * heads, yb_blk=4 * heads + lw // LANES)
    gg0 = 4 * gw + 2 * lw
    return dict(d=d, heads=heads, gw=gw, lw=lw, nab=nab, gg_blk=gg0 // LANES, gl_blk=(gg0 + d) // LANES,
                main=gg0 + 2 * d, np=gg0 + 2 * d + LANES, **blk)


def _pad_w_in(w_in, dm):
    c0 = 4 * dm['gw']
    nab = dm['nab']
    return jnp.concatenate([w_in[:, :c0], w_in[:, c0 + nab:], w_in[:, c0:c0 + nab],
                            jnp.zeros((w_in.shape[0], LANES - nab), w_in.dtype)], axis=1)


def _unpad_w_in(gp, dm):
    c0 = 4 * dm['gw']
    nab = dm['nab']
    main = dm['main']
    return jnp.concatenate([gp[:, :c0], gp[:, main:main + nab], gp[:, c0:main]], axis=1)


def _local_step(x, target, layers, fetchers, final_norm, dm):
    saved = []
    cur = x
    for li, wl in enumerate(layers):
        cur, sv = _layer_fwd(cur, wl, fetchers[li], dm, f"_l{li}")
        saved.append(sv)
    loss_blk, dx, dx_b, dfin = _loss_head(cur, final_norm, target, "loss_head")
    grads = [None] * len(layers)
    for li in reversed(range(len(layers))):
        dx, dx_b, grads[li] = _layer_bwd(dx, dx_b, layers[li], saved[li], dm, f"_l{li}")
    return loss_blk[0, 0], dx, grads, dfin.reshape(-1)


def kernel(x, attn_norm, w_in, gdn_conv_w, gdn_a_log, gdn_dt_bias, gdn_norm, lru_conv_w, lru_conv_b, lru_w_a, lru_b_a, lru_w_x, lru_b_x, lru_lambda, w_branch_gdn, w_branch_lru, w_out, mlp_norm, w_up, w_down, final_norm, loss_target, m_attn_norm, m_w_in, m_gdn_conv_w, m_gdn_a_log, m_gdn_dt_bias, m_gdn_norm, m_lru_conv_w, m_lru_conv_b, m_lru_w_a, m_lru_b_a, m_lru_w_x, m_lru_b_x, m_lru_lambda, m_w_branch_gdn, m_w_branch_lru, m_w_out, m_mlp_norm, m_w_up, m_w_down, m_final_norm, v_attn_norm, v_w_in, v_gdn_conv_w, v_gdn_a_log, v_gdn_dt_bias, v_gdn_norm, v_lru_conv_w, v_lru_conv_b, v_lru_w_a, v_lru_b_a, v_lru_w_x, v_lru_b_x, v_lru_lambda, v_w_branch_gdn, v_w_branch_lru, v_w_out, v_mlp_norm, v_w_up, v_w_down, v_final_norm):
    w = dict(attn_norm=attn_norm, w_in=w_in, gdn_conv_w=gdn_conv_w, gdn_a_log=gdn_a_log, gdn_dt_bias=gdn_dt_bias,
             gdn_norm=gdn_norm, lru_conv_w=lru_conv_w, lru_conv_b=lru_conv_b, lru_w_a=lru_w_a, lru_b_a=lru_b_a,
             lru_w_x=lru_w_x, lru_b_x=lru_b_x, lru_lambda=lru_lambda, w_branch_gdn=w_branch_gdn,
             w_branch_lru=w_branch_lru, w_out=w_out, mlp_norm=mlp_norm, w_up=w_up, w_down=w_down,
             final_norm=final_norm)
    m = dict(attn_norm=m_attn_norm, w_in=m_w_in, gdn_conv_w=m_gdn_conv_w, gdn_a_log=m_gdn_a_log,
             gdn_dt_bias=m_gdn_dt_bias, gdn_norm=m_gdn_norm, lru_conv_w=m_lru_conv_w, lru_conv_b=m_lru_conv_b,
             lru_w_a=m_lru_w_a, lru_b_a=m_lru_b_a, lru_w_x=m_lru_w_x, lru_b_x=m_lru_b_x, lru_lambda=m_lru_lambda,
             w_branch_gdn=m_w_branch_gdn, w_branch_lru=m_w_branch_lru, w_out=m_w_out, mlp_norm=m_mlp_norm,
             w_up=m_w_up, w_down=m_w_down, final_norm=m_final_norm)
    v = dict(attn_norm=v_attn_norm, w_in=v_w_in, gdn_conv_w=v_gdn_conv_w, gdn_a_log=v_gdn_a_log,
             gdn_dt_bias=v_gdn_dt_bias, gdn_norm=v_gdn_norm, lru_conv_w=v_lru_conv_w, lru_conv_b=v_lru_conv_b,
             lru_w_a=v_lru_w_a, lru_b_a=v_lru_b_a, lru_w_x=v_lru_w_x, lru_b_x=v_lru_b_x, lru_lambda=v_lru_lambda,
             w_branch_gdn=v_w_branch_gdn, w_branch_lru=v_w_branch_lru, w_out=v_w_out, mlp_norm=v_mlp_norm,
             w_up=v_w_up, w_down=v_w_down, final_norm=v_final_norm)
    n_layers = attn_norm.shape[0]
    d = x.shape[-1]
    heads = gdn_a_log.shape[-1]
    lw = lru_conv_b.shape[-1]
    dm = _dims(d, heads, lw)
    big_names = list(BIG_SHARD_AXIS)
    conv_names = list(CONV_SHARD_AXIS)
    chip = 2 * lax.axis_index("x") + lax.axis_index("y")

    conv_flat = jnp.concatenate([w[n].reshape(-1) for n in conv_names])
    conv_rows = -(-conv_flat.shape[0] // (SUBLANES * LANES)) * SUBLANES
    conv_buf = jnp.pad(conv_flat, (0, conv_rows * LANES - conv_flat.shape[0])).reshape(conv_rows, LANES)
    conv_all = _all_devices_gather(conv_buf, "conv_allgather").reshape(N_CHIPS, 2, -1)[:, 0]
    conv_full, off = {}, 0
    for n in conv_names:
        shard = w[n]
        parts = conv_all[:, off:off + shard.size].reshape((N_CHIPS,) + shard.shape)
        conv_full[n] = jnp.concatenate([parts[q] for q in range(N_CHIPS)], axis=CONV_SHARD_AXIS[n])
        off += shard.size

    def start_gather(li, group):
        halves, lands = [], []
        for n in GATHER_GROUPS[group]:
            s = w[n][li].astype(BF16)
            hv = s.reshape((2, s.shape[0] // 2) + s.shape[1:])
            halves.append(hv)
            lands.append(lax.dynamic_update_index_in_dim(lax.empty((N_CHIPS,) + hv.shape, BF16), hv, chip, 0))
        nt = len(halves)
        return _split_start(f"wgather_{group}_l{li}_ici_start", halves + lands, 3 * nt,
                            functools.partial(_gather_ici_copies, nt))

    pending = {(li, group): start_gather(li, group) for li in range(n_layers) for group in GATHER_GROUPS}

    def make_fetch(li):
        def fetch(group, after):
            names = GATHER_GROUPS[group]
            nt = len(names)
            send, recv, bufs, _ = pending.pop((li, group))
            bufs = _split_wait(f"wgather_{group}_l{li}_ici_wait", send, recv, bufs, after,
                               functools.partial(_gather_ici_copies, nt))
            send, recv, lands, token = _split_start(f"wgather_{group}_l{li}_d2d_start", bufs[nt:], 3 * nt,
                                                    functools.partial(_gather_d2d_copies, nt))
            lands = _split_wait(f"wgather_{group}_l{li}_d2d_wait", send, recv, lands, token,
                                functools.partial(_gather_d2d_copies, nt))
            out = {}
            for n, land in zip(names, lands):
                slots = land.reshape((N_CHIPS, 2 * land.shape[2]) + land.shape[3:])
                out[n] = jnp.concatenate([slots[q] for q in range(N_CHIPS)], axis=BIG_SHARD_AXIS[n] - 1)
            if 'w_in' in out:
                out['w_in_p'] = _pad_w_in(out.pop('w_in'), dm)
            return out
        return fetch

    layers = []
    for li in range(n_layers):
        wl = {n: w[n][li] for n in SMALL_NAMES if n != 'final_norm' and n not in CONV_SHARD_AXIS}
        for n in conv_names:
            wl[n] = conv_full[n][li]
        layers.append(wl)

    loss_local, dx, grads, dfin = _local_step(x[0], loss_target[0], layers, [make_fetch(li) for li in range(n_layers)],
                                              final_norm, dm)
    loss = lax.psum(loss_local, MESH_AXES)

    core = lax.axis_index("c").astype(jnp.int32).reshape(1)
    contrib = []
    for n in big_names:
        per_layer = [(_unpad_w_in(grads[li]['w_in_p'], dm) if n == 'w_in' else grads[li][n])
                     for li in range(n_layers)]
        contrib.append(jnp.stack([jnp.stack(jnp.split(g, N_CHIPS, axis=BIG_SHARD_AXIS[n] - 1), axis=0)
                                  for g in per_layer], axis=0).astype(BF16))
    theirs = _sibling_exchange(contrib, True, "grad_core_send")
    chip_sum = [_pair_sum(core, mine, th, f"grad_chip_sum_{n}") for n, mine, th in zip(big_names, contrib, theirs)]
    landed = _chip_scatter(chip_sum, "grad_reduce_scatter")
    total = [_sum_slots(lt, f"grad_total_{n}") for n, lt in zip(big_names, landed)]
    other = _sibling_exchange(total, False, "grad_core_exchange")

    small_g = {n: jnp.stack([grads[li][n] for li in range(n_layers)], axis=0)
               for n in SMALL_NAMES if n != 'final_norm'}
    small_g['final_norm'] = dfin
    flat = jnp.concatenate([small_g[n].reshape(-1) for n in SMALL_NAMES])
    n_flat = flat.shape[0]
    row_unit = 32 * SUBLANES
    rows = -(-n_flat // (row_unit * LANES)) * row_unit
    buf = jnp.pad(flat, (0, rows * LANES - n_flat)).reshape(rows, LANES)
    everyone = _all_devices_gather(buf, "small_grad_allgather")
    small_sum = _sum_slots(everyone, "small_grad_sum").reshape(-1)
    small_red = {}
    off = 0
    for n in SMALL_NAMES:
        size = small_g[n].size
        small_red[n] = small_sum[off:off + size].reshape(small_g[n].shape)
        off += size
    for n, ax in CONV_SHARD_AXIS.items():
        width = w[n].shape[ax]
        small_red[n] = lax.dynamic_slice_in_dim(small_red[n], chip * width, width, axis=ax)

    out_g, out_d, out_m, out_v = {}, {}, {}, {}
    for n, g_core, g_other in zip(big_names, total, other):
        out_g[n], out_d[n], out_m[n], out_v[n] = _adamw_layers(core, w[n], g_core, g_other, m[n], v[n], f"adamw_{n}")

    def pack(tree):
        fl = jnp.concatenate([tree[n].reshape(-1) for n in SMALL_NAMES])
        return jnp.pad(fl, (0, rows * LANES - fl.shape[0])).reshape(rows, LANES)

    res = _adamw(pack(w), [pack(small_red)], pack(m), pack(v), "adamw_small")
    for r, dst in zip(res, (out_g, out_d, out_m, out_v)):
        fl = r.reshape(-1)
        off = 0
        for n in SMALL_NAMES:
            dst[n] = fl[off:off + w[n].size].reshape(w[n].shape)
            off += w[n].size

    return (loss, dx[None], *[out_g[n] for n in WEIGHT_NAMES], *[out_d[n] for n in WEIGHT_NAMES],
            *[out_m[n] for n in WEIGHT_NAMES], *[out_v[n] for n in WEIGHT_NAMES])
```

```python
import functools

import jax
import jax.numpy as jnp
from jax import lax
from jax.experimental import pallas as pl
from jax.experimental.pallas import tpu as pltpu

F32 = jnp.float32
BF16 = jnp.bfloat16

LANES = 128
SUBLANES = 8
VMEM_BYTES = 64 * 1024 * 1024
GDN_CHUNK = 64
CONV_WIDTH = 4
RMS_EPS = 1e-6
L2_EPS = 1e-6
LRU_C = 8.0
ADAM_LR = 0.001
ADAM_B1 = 0.9
ADAM_B2 = 0.999
ADAM_EPS = 1e-08
ADAM_WD = 0.01
ADAM_STEP = 10
MESH_AXES = ("x", "y", "c")
N_CHIPS = 4
N_DEVICES = 8

INPUT_NAMES = ['x', 'attn_norm', 'w_in', 'gdn_conv_w', 'gdn_a_log', 'gdn_dt_bias', 'gdn_norm', 'lru_conv_w',
               'lru_conv_b', 'lru_w_a', 'lru_b_a', 'lru_w_x', 'lru_b_x', 'lru_lambda', 'w_branch_gdn',
               'w_branch_lru', 'w_out', 'mlp_norm', 'w_up', 'w_down', 'final_norm']
WEIGHT_NAMES = INPUT_NAMES[1:]
BIG_SHARD_AXIS = {'w_in': 2, 'w_branch_gdn': 2, 'w_branch_lru': 2, 'w_out': 1, 'w_up': 2, 'w_down': 1}
CONV_SHARD_AXIS = {'gdn_conv_w': 2, 'lru_conv_w': 2}
GATHER_GROUPS = {'in': ['w_in'], 'mix': ['w_branch_gdn', 'w_branch_lru', 'w_out'], 'mlp': ['w_up', 'w_down']}
SMALL_NAMES = [n for n in WEIGHT_NAMES if n not in BIG_SHARD_AXIS]


def _tile(n, target, unit=LANES):
    best = None
    t = unit
    while t <= min(n, target):
        if n % t == 0:
            best = t
        t += unit
    return n if best is None else best


def _vmem_limit(block_bytes):
    return int(min(max(3 * block_bytes + (8 << 20), 24 << 20), VMEM_BYTES - (8 << 20)))


def _nbytes(shape, dtype):
    n = 1
    for s in shape:
        n *= s
    return n * jnp.dtype(dtype).itemsize


def _pcall(body, *, name, grid, in_specs, out_specs, out_shape, scratch_shapes=(), semantics=None, block_bytes=0,
           scalar_prefetch=0):
    params = dict(vmem_limit_bytes=_vmem_limit(block_bytes))
    if semantics is not None:
        params['dimension_semantics'] = semantics
    if scalar_prefetch:
        grid_spec = pltpu.PrefetchScalarGridSpec(num_scalar_prefetch=scalar_prefetch, grid=grid, in_specs=in_specs,
                                                 out_specs=out_specs, scratch_shapes=list(scratch_shapes))
        return pl.pallas_call(body, name=name, grid_spec=grid_spec, out_shape=out_shape,
                              compiler_params=pltpu.CompilerParams(**params))
    return pl.pallas_call(body, name=name, grid=grid, in_specs=in_specs, out_specs=out_specs, out_shape=out_shape,
                          scratch_shapes=list(scratch_shapes), compiler_params=pltpu.CompilerParams(**params))


def _dot(a, b):
    return jnp.dot(a.astype(BF16), b.astype(BF16), preferred_element_type=F32)


def _dot_nt(a, b):
    return lax.dot_general(a.astype(BF16), b.astype(BF16), (((1,), (1,)), ((), ())), preferred_element_type=F32)


def _dot_tn(a, b):
    return lax.dot_general(a.astype(BF16), b.astype(BF16), (((0,), (0,)), ((), ())), preferred_element_type=F32)


def _sigmoid(x):
    return 1.0 / (1.0 + jnp.exp(-x))


def _log1p(u):
    return jnp.where(u < 1e-3, u * (1.0 - u * (0.5 - u * (1.0 / 3.0))), jnp.log(1.0 + u))


def _softplus(x):
    return jnp.maximum(x, 0.0) + _log1p(jnp.exp(-jnp.abs(x)))


_GELU_K = 0.7978845608028654


def _gelu_and_grad(x):
    inner = _GELU_K * (x + 0.044715 * x * x * x)
    th = jnp.tanh(inner)
    g = 0.5 * x * (1.0 + th)
    dg = 0.5 * (1.0 + th) + 0.5 * x * (1.0 - th * th) * _GELU_K * (1.0 + 3.0 * 0.044715 * x * x)
    return g, dg


MATMUL_TK_MAX = 3584


def _matmul(a, b, *, mode, name, out_dtype=F32, add=None, epilogue=None, extra=None, tm=512, tn=1024, tk=2048):
    if mode == 'nn':
        (m, k), (k2, n) = a.shape, b.shape
    elif mode == 'nt':
        (m, k), (n, k2) = a.shape, b.shape
    else:
        (k, m), (k2, n) = a.shape, b.shape
    assert k == k2, (a.shape, b.shape, mode)
    tm, tn = _tile(m, tm), _tile(n, tn)
    tk = _tile(k, tk)
    if k // tk > 2 * (-(-k // MATMUL_TK_MAX)):
        tk = _tile(k, MATMUL_TK_MAX)
    nk = k // tk
    dims = {'nn': (((1,), (0,)), ((), ())), 'nt': (((1,), (1,)), ((), ())), 'tn': (((0,), (0,)), ((), ()))}[mode]
    a_bytes, b_bytes = _nbytes(a.shape, a.dtype), _nbytes(b.shape, b.dtype)
    rows_outer = nk > 1 or a_bytes + (m // tm) * b_bytes <= b_bytes + (n // tn) * a_bytes

    def ij(g0, g1):
        return (g0, g1) if rows_outer else (g1, g0)

    def spec(shape, pick):
        return pl.BlockSpec(shape, lambda g0, g1, kk: pick(*ij(g0, g1), kk))

    a_spec = spec((tk, tm), lambda i, j, kk: (kk, i)) if mode == 'tn' else spec((tm, tk), lambda i, j, kk: (i, kk))
    b_spec = spec((tn, tk), lambda i, j, kk: (j, kk)) if mode == 'nt' else spec((tk, tn), lambda i, j, kk: (kk, j))
    o_spec = spec((tm, tn), lambda i, j, kk: (i, j))
    operands, in_specs = [a, b], [a_spec, b_spec]
    if add is not None:
        operands.append(add)
        in_specs.append(o_spec)
    if extra is not None:
        operands.append(extra)
        in_specs.append(o_spec)
    n_in = len(operands)
    if epilogue == 'relu2':
        out_shape = (jax.ShapeDtypeStruct((m, n), BF16), jax.ShapeDtypeStruct((m, n), BF16))
        out_specs = (o_spec, o_spec)
    else:
        out_shape = jax.ShapeDtypeStruct((m, n), out_dtype)
        out_specs = o_spec

    def body(*refs):
        a_ref, b_ref = refs[0], refs[1]
        outs = refs[n_in:n_in + n_out]

        def finish(p):
            if add is not None:
                p = p + refs[2][...]
            if epilogue == 'relu2':
                ur = jnp.maximum(p, 0.0)
                outs[0][...] = ur.astype(BF16)
                outs[1][...] = (ur * ur).astype(BF16)
            elif epilogue == 'mul2x':
                outs[0][...] = (p * 2.0 * refs[n_in - 1][...].astype(F32)).astype(out_dtype)
            else:
                outs[0][...] = p.astype(out_dtype)

        prod = lax.dot_general(a_ref[...].astype(BF16), b_ref[...].astype(BF16), dims, preferred_element_type=F32)
        if nk == 1:
            finish(prod)
            return
        acc_ref = refs[-1]
        kk = pl.program_id(2)

        @pl.when(kk == 0)
        def _():
            acc_ref[...] = prod

        @pl.when((kk > 0) & (kk < nk - 1))
        def _():
            acc_ref[...] += prod

        @pl.when(kk == nk - 1)
        def _():
            finish(acc_ref[...] + prod)

    n_out = 2 if epilogue == 'relu2' else 1
    bb = (_nbytes((tm, tk), a.dtype) + _nbytes((tk, tn), b.dtype) + 3 * _nbytes((tm, tn), F32))
    grid = (m // tm, n // tn, nk) if rows_outer else (n // tn, m // tm, nk)
    return _pcall(body, name=name, grid=grid, in_specs=in_specs, out_specs=out_specs, out_shape=out_shape,
                  scratch_shapes=[pltpu.VMEM((tm, tn), F32)] if nk > 1 else [],
                  semantics=("parallel", "parallel", "arbitrary"), block_bytes=bb)(*operands)


def _row_tile(s, d, target_bytes=1 << 20):
    return _tile(s, max(SUBLANES, target_bytes // (4 * d)), SUBLANES)


def _rms_fwd(x, gain, name):
    s, d = x.shape
    tr = _row_tile(s, d)

    def body(x_ref, g_ref, h_ref):
        xv = x_ref[...]
        r = lax.rsqrt(jnp.mean(xv * xv, axis=-1, keepdims=True) + RMS_EPS)
        h_ref[...] = (xv * r * g_ref[...]).astype(BF16)

    row = pl.BlockSpec((tr, d), lambda i: (i, 0))
    return _pcall(body, name=name, grid=(s // tr,), in_specs=[row, pl.BlockSpec((1, d), lambda i: (0, 0))],
                  out_specs=row, out_shape=jax.ShapeDtypeStruct((s, d), BF16), semantics=("parallel",),
                  block_bytes=2 * tr * d * 4)(x, gain.reshape(1, d))


def _rms_bwd(x, gain, dh, dres, name):
    s, d = x.shape
    tr = _row_tile(s, d, 1 << 19)

    def body(x_ref, g_ref, dh_ref, dres_ref, dx_ref, dxb_ref, dg_ref):
        xv = x_ref[...]
        r = lax.rsqrt(jnp.mean(xv * xv, axis=-1, keepdims=True) + RMS_EPS)
        xh = xv * r
        dhv = dh_ref[...]
        dxh = dhv * g_ref[...]
        dx = dres_ref[...] + r * (dxh - xh * jnp.mean(dxh * xh, axis=-1, keepdims=True))
        dx_ref[...] = dx
        dxb_ref[...] = dx.astype(BF16)

        @pl.when(pl.program_id(0) == 0)
        def _():
            dg_ref[...] = jnp.zeros_like(dg_ref)

        dg_ref[...] += jnp.sum(dhv * xh, axis=0, keepdims=True)

    row = pl.BlockSpec((tr, d), lambda i: (i, 0))
    vec = pl.BlockSpec((1, d), lambda i: (0, 0))
    return _pcall(body, name=name, grid=(s // tr,), in_specs=[row, vec, row, row], out_specs=(row, row, vec),
                  out_shape=(jax.ShapeDtypeStruct((s, d), F32), jax.ShapeDtypeStruct((s, d), BF16),
                             jax.ShapeDtypeStruct((1, d), F32)),
                  semantics=("arbitrary",), block_bytes=5 * tr * d * 4)(x, gain.reshape(1, d), dh, dres)


def _loss_head(x, gain, target, name):
    s, d = x.shape
    tr = _row_tile(s, d, 1 << 19)

    def body(x_ref, g_ref, t_ref, loss_ref, dx_ref, dxb_ref, dg_ref):
        xv = x_ref[...]
        r = lax.rsqrt(jnp.mean(xv * xv, axis=-1, keepdims=True) + RMS_EPS)
        xh = xv * r
        gv = g_ref[...]
        err = xh * gv - t_ref[...]
        dy = err * (1.0 / d)
        dxh = dy * gv
        dx = r * (dxh - xh * jnp.mean(dxh * xh, axis=-1, keepdims=True))
        dx_ref[...] = dx
        dxb_ref[...] = dx.astype(BF16)

        @pl.when(pl.program_id(0) == 0)
        def _():
            dg_ref[...] = jnp.zeros_like(dg_ref)
            loss_ref[...] = jnp.zeros_like(loss_ref)

        dg_ref[...] += jnp.sum(dy * xh, axis=0, keepdims=True)
        part = jnp.sum(jnp.sum(err * err, axis=-1, keepdims=True), axis=0, keepdims=True) * (0.5 / d)
        loss_ref[...] += jnp.broadcast_to(part, loss_ref.shape)

    row = pl.BlockSpec((tr, d), lambda i: (i, 0))
    vec = pl.BlockSpec((1, d), lambda i: (0, 0))
    lspec = pl.BlockSpec((SUBLANES, LANES), lambda i: (0, 0))
    return _pcall(body, name=name, grid=(s // tr,), in_specs=[row, vec, row], out_specs=(lspec, row, row, vec),
                  out_shape=(jax.ShapeDtypeStruct((SUBLANES, LANES), F32), jax.ShapeDtypeStruct((s, d), F32),
                             jax.ShapeDtypeStruct((s, d), BF16), jax.ShapeDtypeStruct((1, d), F32)),
                  semantics=("arbitrary",), block_bytes=4 * tr * d * 4)(x, gain.reshape(1, d), target)


def _shift_down(xc, xp, s):
    tr = xc.shape[0]
    r = pltpu.roll(xc, s, 0)
    p = pltpu.roll(xp, s, 0)
    row8 = lax.broadcasted_iota(jnp.int32, (SUBLANES, xc.shape[1]), 0)
    head = jnp.where(row8 < s, p, r[:SUBLANES])
    if tr == SUBLANES:
        return head
    return jnp.concatenate([head, r[SUBLANES:]], axis=0)


def _shift_up(yc, yn, s):
    tr = yc.shape[0]
    u = pltpu.roll(yc, tr - s, 0)
    n = pltpu.roll(yn, SUBLANES - s, 0)
    row8 = lax.broadcasted_iota(jnp.int32, (SUBLANES, yc.shape[1]), 0)
    tail = jnp.where(row8 >= SUBLANES - s, n, u[tr - SUBLANES:])
    if tr == SUBLANES:
        return tail
    return jnp.concatenate([u[:tr - SUBLANES], tail], axis=0)


def _conv_apply(xc, xp, w):
    y = xc * w[CONV_WIDTH - 1:CONV_WIDTH, :]
    for s in range(1, CONV_WIDTH):
        y = y + _shift_down(xc, xp, s) * w[CONV_WIDTH - 1 - s:CONV_WIDTH - s, :]
    return y


def _halo_specs(tr, col_of):
    per = tr // SUBLANES
    cur = pl.BlockSpec((tr, LANES), lambda j, i: (i, col_of(j)))
    prev = pl.BlockSpec((SUBLANES, LANES), lambda j, i: (jnp.maximum(i * per - 1, 0), col_of(j)))
    return cur, prev


def _conv_bias_fwd(x_arr, x_col0, w, bias, name):
    s = x_arr.shape[0]
    ncb = w.shape[1] // LANES
    tr = _tile(s, 512, SUBLANES)

    def body(cur_ref, prev_ref, w_ref, b_ref, o_ref):
        i = pl.program_id(1)
        xp = prev_ref[...] * (i > 0).astype(F32)
        o_ref[...] = _conv_apply(cur_ref[...], xp, w_ref[...]) + b_ref[...]

    cur, prev = _halo_specs(tr, lambda j: x_col0 + j)
    return _pcall(body, name=name, grid=(ncb, s // tr),
                  in_specs=[cur, prev, pl.BlockSpec((CONV_WIDTH, LANES), lambda j, i: (0, j)),
                            pl.BlockSpec((1, LANES), lambda j, i: (0, j))],
                  out_specs=pl.BlockSpec((tr, LANES), lambda j, i: (i, j)),
                  out_shape=jax.ShapeDtypeStruct((s, w.shape[1]), F32), semantics=("parallel", "parallel"),
                  block_bytes=3 * tr * LANES * 4)(x_arr, x_arr, w, bias.reshape(1, -1))


def _conv_bwd(dy, x_arr, x_col0, w, name):
    s, c = dy.shape
    ncb = c // LANES
    tr = _tile(s, 512, SUBLANES)
    per = tr // SUBLANES
    ni = s // tr

    def body(dy_ref, dyn_ref, cur_ref, prev_ref, w_ref, dx_ref, dw_ref, db_ref):
        i = pl.program_id(1)
        dyv = dy_ref[...]
        dn = dyn_ref[...] * (i < ni - 1).astype(F32)
        xc = cur_ref[...]
        xp = prev_ref[...] * (i > 0).astype(F32)
        wv = w_ref[...]

        @pl.when(i == 0)
        def _():
            dw_ref[...] = jnp.zeros_like(dw_ref)
            db_ref[...] = jnp.zeros_like(db_ref)

        dx = dyv * wv[CONV_WIDTH - 1:CONV_WIDTH, :]
        dw_ref[CONV_WIDTH - 1:CONV_WIDTH, :] += jnp.sum(dyv * xc, axis=0, keepdims=True)
        for sh in range(1, CONV_WIDTH):
            j = CONV_WIDTH - 1 - sh
            dx = dx + _shift_up(dyv, dn, sh) * wv[j:j + 1, :]
            dw_ref[j:j + 1, :] += jnp.sum(dyv * _shift_down(xc, xp, sh), axis=0, keepdims=True)
        dx_ref[...] = dx
        db_ref[...] += jnp.sum(dyv, axis=0, keepdims=True)

    cur, prev = _halo_specs(tr, lambda j: x_col0 + j)
    dcur = pl.BlockSpec((tr, LANES), lambda j, i: (i, j))
    dnext = pl.BlockSpec((SUBLANES, LANES), lambda j, i: (jnp.minimum((i + 1) * per, s // SUBLANES - 1), j))
    return _pcall(body, name=name, grid=(ncb, ni),
                  in_specs=[dcur, dnext, cur, prev, pl.BlockSpec((CONV_WIDTH, LANES), lambda j, i: (0, j))],
                  out_specs=(dcur, pl.BlockSpec((CONV_WIDTH, LANES), lambda j, i: (0, j)),
                             pl.BlockSpec((1, LANES), lambda j, i: (0, j))),
                  out_shape=(jax.ShapeDtypeStruct((s, c), F32), jax.ShapeDtypeStruct((CONV_WIDTH, c), F32),
                             jax.ShapeDtypeStruct((1, c), F32)),
                  semantics=("parallel", "arbitrary"), block_bytes=4 * tr * LANES * 4)(dy, dy, x_arr, x_arr, w)


def _gdn_pre_fwd(proj, conv_w, heads, name):
    s = proj.shape[0]
    ncb = conv_w.shape[1] // LANES
    tr = _tile(s, 512, SUBLANES)
    qscale = float(LANES) ** -0.5

    def body(cur_ref, prev_ref, w_ref, o_ref):
        j, i = pl.program_id(0), pl.program_id(1)
        xp = prev_ref[...] * (i > 0).astype(F32)
        cv = _conv_apply(cur_ref[...], xp, w_ref[...])
        sv = cv * _sigmoid(cv)
        nrm = lax.rsqrt(jnp.sum(sv * sv, axis=-1, keepdims=True) + L2_EPS)
        scale = jnp.where(j < heads, qscale, 1.0)
        o_ref[...] = jnp.where(j < 2 * heads, sv * nrm * scale, sv)

    cur, prev = _halo_specs(tr, lambda j: j)
    return _pcall(body, name=name, grid=(ncb, s // tr),
                  in_specs=[cur, prev, pl.BlockSpec((CONV_WIDTH, LANES), lambda j, i: (0, j))],
                  out_specs=pl.BlockSpec((tr, LANES), lambda j, i: (i, j)),
                  out_shape=jax.ShapeDtypeStruct((s, conv_w.shape[1]), F32), semantics=("parallel", "parallel"),
                  block_bytes=3 * tr * LANES * 4)(proj, proj, conv_w)


def _gdn_pre_bwd(proj, conv_w, dqkv, heads, name):
    s = proj.shape[0]
    ncb = conv_w.shape[1] // LANES
    tr = _tile(s, 512, SUBLANES)
    qscale = float(LANES) ** -0.5

    def body(cur_ref, prev_ref, w_ref, d_ref, o_ref):
        j, i = pl.program_id(0), pl.program_id(1)
        xp = prev_ref[...] * (i > 0).astype(F32)
        cv = _conv_apply(cur_ref[...], xp, w_ref[...])
        sg = _sigmoid(cv)
        sv = cv * sg
        nrm = lax.rsqrt(jnp.sum(sv * sv, axis=-1, keepdims=True) + L2_EPS)
        dv = d_ref[...]
        scale = jnp.where(j < heads, qscale, 1.0)
        dsn = scale * nrm * (dv - sv * (nrm * nrm) * jnp.sum(dv * sv, axis=-1, keepdims=True))
        ds = jnp.where(j < 2 * heads, dsn, dv)
        o_ref[...] = ds * (sg * (1.0 + cv * (1.0 - sg)))

    cur, prev = _halo_specs(tr, lambda j: j)
    blk = pl.BlockSpec((tr, LANES), lambda j, i: (i, j))
    return _pcall(body, name=name, grid=(ncb, s // tr),
                  in_specs=[cur, prev, pl.BlockSpec((CONV_WIDTH, LANES), lambda j, i: (0, j)), blk],
                  out_specs=blk, out_shape=jax.ShapeDtypeStruct((s, conv_w.shape[1]), F32),
                  semantics=("parallel", "parallel"), block_bytes=4 * tr * LANES * 4)(proj, proj, conv_w, dqkv)


def _tri_inverse(a_strict, block):
    n = a_strict.shape[0]
    ri = lax.broadcasted_iota(jnp.int32, (n, n), 0)
    ci = lax.broadcasted_iota(jnp.int32, (n, n), 1)
    p = jnp.where(ri == ci, 1.0, 0.0) - a_strict
    if block <= 2:
        return p
    xp = _dot(a_strict, a_strict)
    span = 2
    while True:
        p_next = p + _dot(p, xp)
        span *= 2
        if span >= block:
            return p_next
        xp = _dot(xp, xp)
        p = p_next


GDN_HEAD_GROUP = 4
_CHUNK_SHIFT = GDN_CHUNK.bit_length() - 1
_LANE_SHIFT = LANES.bit_length() - 1


def _stack_heads(ref, hb):
    return jnp.concatenate([ref[:, i * LANES:(i + 1) * LANES] for i in range(hb)], axis=0)


def _diag_blocks(x, hb):
    c = GDN_CHUNK
    return jnp.concatenate([x[i * c:(i + 1) * c, i * LANES:(i + 1) * LANES] for i in range(hb)], axis=0)


def _expand_blocks(y, hb):
    row_blk = lax.shift_right_logical(lax.broadcasted_iota(jnp.int32, y.shape, 0), _CHUNK_SHIFT)
    return jnp.concatenate([jnp.where(row_blk == j, y, 0.0) for j in range(hb)], axis=1)


def _gdn_group_terms(q, k, v, ab, alog, dtb, head0, hb, heads):
    c = GDN_CHUNK
    r = hb * c
    lane = lax.broadcasted_iota(jnp.int32, (1, LANES), 1)

    def column(lane0):
        return jnp.concatenate([jnp.sum(jnp.where(lane == lane0 + head0 + i, ab, 0.0), axis=1, keepdims=True)
                                for i in range(hb)], axis=0)

    def per_head(vec):
        return jnp.concatenate([jnp.broadcast_to(jnp.sum(jnp.where(lane == head0 + i, vec, 0.0), axis=1,
                                                         keepdims=True), (c, 1)) for i in range(hb)], axis=0)

    pre = column(0) + per_head(dtb)
    neg_ea = -jnp.exp(per_head(alog))
    g = neg_ea * _softplus(pre)
    beta = _sigmoid(column(heads))
    ri = lax.broadcasted_iota(jnp.int32, (r, r), 0)
    ci = lax.broadcasted_iota(jnp.int32, (r, r), 1)
    same = lax.shift_right_logical(ri, _CHUNK_SHIFT) == lax.shift_right_logical(ci, _CHUNK_SHIFT)
    eye = ri == ci
    causal = same & (ri >= ci)
    strict = same & (ri > ci)
    g_row = jnp.sum(jnp.where(eye, g, 0.0), axis=0, keepdims=True)
    gc_col = jnp.sum(jnp.where(causal, g_row, 0.0), axis=1, keepdims=True)
    gc_row = jnp.sum(jnp.where(same & (ri <= ci), g, 0.0), axis=0, keepdims=True)
    gl_col = jnp.sum(jnp.where(same, g_row, 0.0), axis=1, keepdims=True)
    decay = jnp.where(causal, jnp.exp(jnp.where(causal, gc_col - gc_row, 0.0)), 0.0)
    e_last_col = jnp.exp(gl_col)
    e_last_lanes = jnp.concatenate([jnp.broadcast_to(e_last_col[i * c:i * c + 1, :], (1, LANES))
                                    for i in range(hb)], axis=1)
    egc = jnp.exp(gc_col)
    ekl = jnp.exp(gl_col - gc_col)
    kb = k * beta
    vb = v * beta
    kk = _dot_nt(kb, k)
    a_strict = jnp.where(strict, kk * decay, 0.0)
    return dict(pre=pre, neg_ea=neg_ea, g=g, beta=beta, ri=ri, ci=ci, same=same, eye=eye, causal=causal,
                strict=strict, decay=decay, e_last_col=e_last_col, e_last_lanes=e_last_lanes, egc=egc, ekl=ekl,
                kb=kb, vb=vb, kk=kk, a_strict=a_strict, lane=lane)


def _gdn_head_group(heads):
    hb = GDN_HEAD_GROUP
    while heads % hb:
        hb //= 2
    return hb


def _gdn_fwd(qkv, proj, alog, dtb, heads, name):
    s = qkv.shape[0]
    c = GDN_CHUNK
    nc = s // c
    ab_blk = proj.shape[1] // LANES - 1
    hb = _gdn_head_group(heads)
    ng = heads // hb
    r = hb * c

    def body(q_ref, k_ref, v_ref, ab_ref, alog_ref, dtb_ref, o_ref, t_ref, s0_ref, state_ref):
        grp, ch = pl.program_id(0), pl.program_id(1)

        @pl.when(ch == 0)
        def _():
            state_ref[...] = jnp.zeros_like(state_ref)

        q, k, v = _stack_heads(q_ref, hb), _stack_heads(k_ref, hb), _stack_heads(v_ref, hb)
        tm = _gdn_group_terms(q, k, v, ab_ref[...], alog_ref[...], dtb_ref[...], grp * hb, hb, heads)
        t_inv = _tri_inverse(tm['a_strict'], c)
        u = _dot(t_inv, tm['vb'])
        w = _dot(t_inv, tm['kb'] * tm['egc'])
        qk = jnp.where(tm['causal'], _dot_nt(q, k) * tm['decay'], 0.0)
        st = state_ref[...]
        v_new = u - _diag_blocks(_dot(w, st), hb)
        out = _diag_blocks(_dot(q * tm['egc'], st), hb) + _dot(qk, v_new)
        for i in range(hb):
            o_ref[:, i * LANES:(i + 1) * LANES] = out[i * c:(i + 1) * c, :]
        t_ref[...] = t_inv
        s0_ref[...] = st
        state_ref[...] = st * tm['e_last_lanes'] + _dot_tn(k * tm['ekl'], _expand_blocks(v_new, hb))

    def blk(off):
        return pl.BlockSpec((c, hb * LANES), lambda g, n: (n, off * ng + g))

    vec = pl.BlockSpec((1, LANES), lambda g, n: (0, 0))
    return _pcall(
        body, name=name, grid=(ng, nc),
        in_specs=[blk(0), blk(1), blk(2), pl.BlockSpec((c, LANES), lambda g, n: (n, ab_blk)), vec, vec],
        out_specs=(blk(0), pl.BlockSpec((None, None, r, r), lambda g, n: (g, n, 0, 0)),
                   pl.BlockSpec((None, None, LANES, hb * LANES), lambda g, n: (g, n, 0, 0))),
        out_shape=(jax.ShapeDtypeStruct((s, heads * LANES), F32), jax.ShapeDtypeStruct((ng, nc, r, r), F32),
                   jax.ShapeDtypeStruct((ng, nc, LANES, hb * LANES), F32)),
        scratch_shapes=[pltpu.VMEM((LANES, hb * LANES), F32)], semantics=("parallel", "arbitrary"),
        block_bytes=8 * r * LANES * 4 + 2 * r * r * 4 + 2 * LANES * hb * LANES * 4)(qkv, qkv, qkv, proj, alog, dtb)


def _gdn_bwd(qkv, proj, alog, dtb, t_all, s0_all, d_o, heads, name):
    s = qkv.shape[0]
    c = GDN_CHUNK
    nc = s // c
    ab_blk = proj.shape[1] // LANES - 1

    hb = _gdn_head_group(heads)
    ng = heads // hb
    r = hb * c

    def body(q_ref, k_ref, v_ref, ab_ref, alog_ref, dtb_ref, t_ref, s0_ref, do_ref,
             dq_ref, dk_ref, dv_ref, dgb_ref, ds_ref):
        grp, step = pl.program_id(0), pl.program_id(1)

        @pl.when(step == 0)
        def _():
            ds_ref[...] = jnp.zeros_like(ds_ref)

        q, k, v = _stack_heads(q_ref, hb), _stack_heads(k_ref, hb), _stack_heads(v_ref, hb)
        do = _stack_heads(do_ref, hb)
        tm = _gdn_group_terms(q, k, v, ab_ref[...], alog_ref[...], dtb_ref[...], grp * hb, hb, heads)
        ri, ci, same, eye = tm['ri'], tm['ci'], tm['same'], tm['eye']
        causal, strict, decay = tm['causal'], tm['strict'], tm['decay']
        egc, ekl, kb, vb, beta = tm['egc'], tm['ekl'], tm['kb'], tm['vb'], tm['beta']
        t_inv = t_ref[...]
        st = s0_ref[...]
        ds_next = ds_ref[...]
        kbg = kb * egc
        u = _dot(t_inv, vb)
        w = _dot(t_inv, kbg)
        qkm = _dot_nt(q, k)
        qk = jnp.where(causal, qkm * decay, 0.0)
        v_new = u - _diag_blocks(_dot(w, st), hb)
        qd = q * egc
        kd = k * ekl
        do_x = _expand_blocks(do, hb)

        dqd = _dot_nt(do_x, st)
        dqk = jnp.where(causal, _dot_nt(do, v_new), 0.0)
        dvn = _dot_tn(qk, do) + _diag_blocks(_dot(kd, ds_next), hb)
        dkd = _dot_nt(_expand_blocks(v_new, hb), ds_next)
        sd = jnp.sum(st * ds_next, axis=0, keepdims=True)
        dgl = jnp.concatenate([jnp.broadcast_to(jnp.sum(sd[:, i * LANES:(i + 1) * LANES], axis=1, keepdims=True),
                                                (c, 1)) for i in range(hb)], axis=0) * tm['e_last_col']
        dvn_x = _expand_blocks(dvn, hb)
        dw = -_dot_nt(dvn_x, st)
        ds_ref[...] = _dot_tn(qd, do_x) + tm['e_last_lanes'] * ds_next - _dot_tn(w, dvn_x)
        dt = _dot_nt(dvn, vb) + _dot_nt(dw, kbg)
        dvb = _dot_tn(t_inv, dvn)
        dkbg = _dot_tn(t_inv, dw)
        da_m = jnp.where(strict, -_dot_tn(t_inv, _dot_nt(dt, t_inv)), 0.0)
        dad = da_m * decay
        dkb = _dot(dad, k) + dkbg * egc
        dqkd = dqk * decay
        dq = _dot(dqkd, k) + dqd * egc
        dk = _dot_tn(dad, kb) + _dot_tn(dqkd, q) + dkd * ekl + dkb * beta
        e_mat = (da_m * tm['kk'] + dqk * qkm) * decay
        s_kd = jnp.sum(dkd * kd, axis=1, keepdims=True)
        s_kd_row = jnp.sum(jnp.where(eye, s_kd, 0.0), axis=0, keepdims=True)
        dgl = dgl + jnp.sum(jnp.where(same, s_kd_row, 0.0), axis=1, keepdims=True)
        col_sum = jnp.sum(e_mat, axis=0, keepdims=True)
        col_sum_c = jnp.sum(jnp.where(eye, col_sum, 0.0), axis=1, keepdims=True)
        dgc = (jnp.sum(e_mat, axis=1, keepdims=True) - col_sum_c + jnp.sum(dqd * qd, axis=1, keepdims=True)
               - s_kd + jnp.sum(dkbg * kbg, axis=1, keepdims=True))
        row_c = lax.broadcasted_iota(jnp.int32, (r, 1), 0)
        dgc = dgc + jnp.where((row_c & (c - 1)) == c - 1, dgl, 0.0)
        dgc_row = jnp.sum(jnp.where(eye, dgc, 0.0), axis=0, keepdims=True)
        dg = jnp.sum(jnp.where(same & (ci >= ri), dgc_row, 0.0), axis=1, keepdims=True)
        dbeta = jnp.sum(dkb * k, axis=1, keepdims=True) + jnp.sum(dvb * v, axis=1, keepdims=True)
        da_pre = dg * tm['neg_ea'] * _sigmoid(tm['pre'])
        db_pre = dbeta * beta * (1.0 - beta)
        lane = tm['lane']
        head_row = grp * hb + lax.shift_right_logical(row_c, _CHUNK_SHIFT)
        dgb = (jnp.where(lane == head_row, da_pre, 0.0) + jnp.where(lane == heads + head_row, db_pre, 0.0)
               + jnp.where(lane == 2 * heads + head_row, dg * tm['g'], 0.0))
        dvv = dvb * beta
        for i in range(hb):
            cols, rows = slice(i * LANES, (i + 1) * LANES), slice(i * c, (i + 1) * c)
            dq_ref[:, cols] = dq[rows, :]
            dk_ref[:, cols] = dk[rows, :]
            dv_ref[:, cols] = dvv[rows, :]
            dgb_ref[:, cols] = dgb[rows, :]

    def blk(off):
        return pl.BlockSpec((c, hb * LANES), lambda g, n: (nc - 1 - n, off * ng + g))

    vec = pl.BlockSpec((1, LANES), lambda g, n: (0, 0))
    gw = heads * LANES
    dq, dk, dv, dgb = _pcall(
        body, name=name, grid=(ng, nc),
        in_specs=[blk(0), blk(1), blk(2), pl.BlockSpec((c, LANES), lambda g, n: (nc - 1 - n, ab_blk)),
                  vec, vec, pl.BlockSpec((None, None, r, r), lambda g, n: (g, nc - 1 - n, 0, 0)),
                  pl.BlockSpec((None, None, LANES, hb * LANES), lambda g, n: (g, nc - 1 - n, 0, 0)), blk(0)],
        out_specs=(blk(0), blk(0), blk(0), blk(0)),
        out_shape=tuple(jax.ShapeDtypeStruct((s, gw), F32) for _ in range(4)),
        scratch_shapes=[pltpu.VMEM((LANES, hb * LANES), F32)], semantics=("parallel", "arbitrary"),
        block_bytes=12 * r * LANES * 4 + 2 * r * r * 4 + 2 * LANES * hb * LANES * 4)(
            qkv, qkv, qkv, proj, alog, dtb, t_all, s0_all, d_o)
    return dq, dk, dv, dgb


def _gdn_post_fwd(o, proj, z_col0, gain, name):
    s, gw = o.shape
    heads = gw // LANES
    tr = _tile(s, 512, SUBLANES)

    def body(o_ref, z_ref, g_ref, y_ref):
        ov, zv = o_ref[...], z_ref[...]
        r = lax.rsqrt(jnp.mean(ov * ov, axis=-1, keepdims=True) + RMS_EPS)
        y_ref[...] = (ov * r * g_ref[...] * (zv * _sigmoid(zv))).astype(BF16)

    blk = pl.BlockSpec((tr, LANES), lambda i, h: (i, h))
    return _pcall(body, name=name, grid=(s // tr, heads),
                  in_specs=[blk, pl.BlockSpec((tr, LANES), lambda i, h: (i, z_col0 + h)),
                            pl.BlockSpec((1, LANES), lambda i, h: (0, 0))],
                  out_specs=blk, out_shape=jax.ShapeDtypeStruct((s, gw), BF16), semantics=("parallel", "parallel"),
                  block_bytes=3 * tr * LANES * 4)(o, proj, gain.reshape(1, LANES))


def _gdn_post_bwd(o, proj, z_col0, gain, dy, name):
    s, gw = o.shape
    heads = gw // LANES
    tr = _tile(s, 512, SUBLANES)

    def body(o_ref, z_ref, g_ref, dy_ref, do_ref, dz_ref, dg_ref):
        ov, zv, gv, dyv = o_ref[...], z_ref[...], g_ref[...], dy_ref[...]
        r = lax.rsqrt(jnp.mean(ov * ov, axis=-1, keepdims=True) + RMS_EPS)
        nv = ov * r
        sg = _sigmoid(zv)
        sz = zv * sg
        dn = dyv * gv * sz
        do_ref[...] = r * (dn - nv * jnp.mean(dn * nv, axis=-1, keepdims=True))
        dz_ref[...] = dyv * nv * gv * (sg * (1.0 + zv * (1.0 - sg)))

        @pl.when((pl.program_id(0) == 0) & (pl.program_id(1) == 0))
        def _():
            dg_ref[...] = jnp.zeros_like(dg_ref)

        dg_ref[...] += jnp.sum(dyv * nv * sz, axis=0, keepdims=True)

    blk = pl.BlockSpec((tr, LANES), lambda i, h: (i, h))
    vec = pl.BlockSpec((1, LANES), lambda i, h: (0, 0))
    return _pcall(body, name=name, grid=(s // tr, heads),
                  in_specs=[blk, pl.BlockSpec((tr, LANES), lambda i, h: (i, z_col0 + h)), vec, blk],
                  out_specs=(blk, blk, vec),
                  out_shape=(jax.ShapeDtypeStruct((s, gw), F32), jax.ShapeDtypeStruct((s, gw), F32),
                             jax.ShapeDtypeStruct((1, LANES), F32)),
                  semantics=("arbitrary", "arbitrary"), block_bytes=6 * tr * LANES * 4)(
                      o, proj, gain.reshape(1, LANES), dy)


def _dab_reduce(dgb, name):
    s, gw = dgb.shape
    heads = gw // LANES
    tr = _tile(s, 512, SUBLANES)

    def body(d_ref, o_ref, cs_ref):
        acc = d_ref[:, 0:LANES]
        for h in range(1, heads):
            acc = acc + d_ref[:, h * LANES:(h + 1) * LANES]
        o_ref[...] = acc

        @pl.when(pl.program_id(0) == 0)
        def _():
            cs_ref[...] = jnp.zeros_like(cs_ref)

        cs_ref[...] += jnp.sum(acc, axis=0, keepdims=True)

    return _pcall(body, name=name, grid=(s // tr,), in_specs=[pl.BlockSpec((tr, gw), lambda i: (i, 0))],
                  out_specs=(pl.BlockSpec((tr, LANES), lambda i: (i, 0)), pl.BlockSpec((1, LANES), lambda i: (0, 0))),
                  out_shape=(jax.ShapeDtypeStruct((s, LANES), F32), jax.ShapeDtypeStruct((1, LANES), F32)),
                  semantics=("arbitrary",), block_bytes=tr * gw * 4)(dgb)


def _lru_gates(xc, wa, wx, ba, bx, lam):
    r = _sigmoid(_dot(xc, wa) + ba)
    ig = _sigmoid(_dot(xc, wx) + bx)
    sp = _softplus(-lam)
    log_a = -LRU_C * r * sp
    a = jnp.exp(log_a)
    e2 = jnp.exp(2.0 * log_a)
    mult = jnp.sqrt(jnp.maximum(1.0 - e2, 0.0))
    return r, ig, sp, a, e2, mult


def _lru_fwd(xc, proj, y_col0, wa, wx, ba, bx, lam, name):
    s, lw = xc.shape
    nb = lw // LANES
    tr = _tile(s, 256, SUBLANES)

    def body(xc_ref, y_ref, wa_ref, wx_ref, ba_ref, bx_ref, lam_ref, h_ref, o_ref, carry_ref):
        @pl.when(pl.program_id(1) == 0)
        def _():
            carry_ref[...] = jnp.zeros_like(carry_ref)

        xv = xc_ref[...]
        _, ig, _, a, _, mult = _lru_gates(xv, wa_ref[...], wx_ref[...], ba_ref[...], bx_ref[...], lam_ref[...])
        b = mult * (ig * xv)
        row = lax.broadcasted_iota(jnp.int32, (tr, LANES), 0)
        sh = 1
        while sh < tr:
            keep = row >= sh
            b = a * jnp.where(keep, pltpu.roll(b, sh, 0), 0.0) + b
            a = a * jnp.where(keep, pltpu.roll(a, sh, 0), 1.0)
            sh *= 2
        hv = a * carry_ref[0:1, :] + b
        h_ref[...] = hv
        carry_ref[...] = jnp.broadcast_to(hv[tr - 1:tr, :], carry_ref.shape)
        gy, _ = _gelu_and_grad(y_ref[...])
        o_ref[...] = (hv * gy).astype(BF16)

    blk = pl.BlockSpec((tr, LANES), lambda n, i: (i, n))
    wspec = pl.BlockSpec((None, LANES, LANES), lambda n, i: (n, 0, 0))
    vec = pl.BlockSpec((1, LANES), lambda n, i: (0, n))
    return _pcall(body, name=name, grid=(nb, s // tr),
                  in_specs=[blk, pl.BlockSpec((tr, LANES), lambda n, i: (i, y_col0 + n)), wspec, wspec, vec, vec, vec],
                  out_specs=(blk, blk),
                  out_shape=(jax.ShapeDtypeStruct((s, lw), F32), jax.ShapeDtypeStruct((s, lw), BF16)),
                  scratch_shapes=[pltpu.VMEM((SUBLANES, LANES), F32)], semantics=("parallel", "arbitrary"),
                  block_bytes=8 * tr * LANES * 4)(xc, proj, wa, wx, ba.reshape(1, lw), bx.reshape(1, lw),
                                                  lam.reshape(1, lw))


def _lru_bwd(d_out, xc, hseq, proj, y_col0, wa, wx, ba, bx, lam, name):
    s, lw = xc.shape
    nb = lw // LANES
    tr = _tile(s, 256, SUBLANES)
    per = tr // SUBLANES
    ni = s // tr
    nrow8 = s // SUBLANES

    def body(do_ref, xc_ref, xn_ref, h_ref, hp_ref, y_ref, wa_ref, wx_ref, ba_ref, bx_ref, lam_ref,
             dxc_ref, dy_ref, dwa_ref, dwx_ref, dba_ref, dbx_ref, dlam_ref, carry_ref):
        step = pl.program_id(1)
        tile = ni - 1 - step

        @pl.when(step == 0)
        def _():
            carry_ref[...] = jnp.zeros_like(carry_ref)
            dwa_ref[...] = jnp.zeros_like(dwa_ref)
            dwx_ref[...] = jnp.zeros_like(dwx_ref)
            dba_ref[...] = jnp.zeros_like(dba_ref)
            dbx_ref[...] = jnp.zeros_like(dbx_ref)
            dlam_ref[...] = jnp.zeros_like(dlam_ref)

        wav, wxv, bav, bxv, lamv = wa_ref[...], wx_ref[...], ba_ref[...], bx_ref[...], lam_ref[...]
        xv = xc_ref[...]
        r, ig, sp, a, e2, mult = _lru_gates(xv, wav, wxv, bav, bxv, lamv)
        a_next = _lru_gates(xn_ref[...], wav, wxv, bav, bxv, lamv)[3] * (tile < ni - 1).astype(F32)
        hv = h_ref[...]
        h_prev = _shift_down(hv, hp_ref[...] * (tile > 0).astype(F32), 1)
        yv = y_ref[...]
        gy, dgy = _gelu_and_grad(yv)
        dov = do_ref[...]
        dy_ref[...] = dov * hv * dgy
        coef = _shift_up(a, a_next, 1)
        bb = dov * gy
        row = lax.broadcasted_iota(jnp.int32, (tr, LANES), 0)
        sh = 1
        while sh < tr:
            keep = row < tr - sh
            bb = coef * jnp.where(keep, pltpu.roll(bb, tr - sh, 0), 0.0) + bb
            coef = coef * jnp.where(keep, pltpu.roll(coef, tr - sh, 0), 1.0)
            sh *= 2
        lam_t = coef * carry_ref[0:1, :] + bb
        carry_ref[...] = jnp.broadcast_to(lam_t[0:1, :], carry_ref.shape)
        d_a = lam_t * h_prev
        d_mult = lam_t * (ig * xv)
        d_ix = lam_t * mult
        d_la = d_a * a - d_mult * e2 / jnp.maximum(mult, 1e-30)
        d_r = d_la * (-LRU_C * sp)
        dlam_ref[...] += jnp.sum(d_la * (LRU_C * r) * _sigmoid(-lamv), axis=0, keepdims=True)
        d_pa = d_r * r * (1.0 - r)
        d_px = (d_ix * xv) * ig * (1.0 - ig)
        dxc_ref[...] = d_ix * ig + _dot_nt(d_pa, wav) + _dot_nt(d_px, wxv)
        dwa_ref[...] += _dot_tn(xv, d_pa)
        dwx_ref[...] += _dot_tn(xv, d_px)
        dba_ref[...] += jnp.sum(d_pa, axis=0, keepdims=True)
        dbx_ref[...] += jnp.sum(d_px, axis=0, keepdims=True)

    blk = pl.BlockSpec((tr, LANES), lambda n, i: (ni - 1 - i, n))
    nxt = pl.BlockSpec((SUBLANES, LANES), lambda n, i: (jnp.minimum((ni - i) * per, nrow8 - 1), n))
    prv = pl.BlockSpec((SUBLANES, LANES), lambda n, i: (jnp.maximum((ni - 1 - i) * per - 1, 0), n))
    wspec = pl.BlockSpec((None, LANES, LANES), lambda n, i: (n, 0, 0))
    vec = pl.BlockSpec((1, LANES), lambda n, i: (0, n))
    return _pcall(
        body, name=name, grid=(nb, ni),
        in_specs=[blk, blk, nxt, blk, prv, pl.BlockSpec((tr, LANES), lambda n, i: (ni - 1 - i, y_col0 + n)),
                  wspec, wspec, vec, vec, vec],
        out_specs=(blk, blk, wspec, wspec, vec, vec, vec),
        out_shape=(jax.ShapeDtypeStruct((s, lw), F32), jax.ShapeDtypeStruct((s, lw), F32),
                   jax.ShapeDtypeStruct((nb, LANES, LANES), F32), jax.ShapeDtypeStruct((nb, LANES, LANES), F32),
                   jax.ShapeDtypeStruct((1, lw), F32), jax.ShapeDtypeStruct((1, lw), F32),
                   jax.ShapeDtypeStruct((1, lw), F32)),
        scratch_shapes=[pltpu.VMEM((SUBLANES, LANES), F32)], semantics=("parallel", "arbitrary"),
        block_bytes=12 * tr * LANES * 4)(d_out, xc, xc, hseq, hseq, proj, wa, wx, ba.reshape(1, lw),
                                         bx.reshape(1, lw), lam.reshape(1, lw))


def _merge_fwd(proj, gg_col0, gl_col0, bg, bl, name):
    s, d = bg.shape
    tr, tc = _tile(s, 256, SUBLANES), _tile(d, 1024)
    cb = tc // LANES

    def body(gg_ref, gl_ref, bg_ref, bl_ref, o_ref):
        o_ref[...] = (_sigmoid(gg_ref[...]) * bg_ref[...] + _sigmoid(gl_ref[...]) * bl_ref[...]).astype(BF16)

    blk = pl.BlockSpec((tr, tc), lambda i, j: (i, j))
    return _pcall(body, name=name, grid=(s // tr, d // tc),
                  in_specs=[pl.BlockSpec((tr, tc), lambda i, j: (i, gg_col0 // cb + j)),
                            pl.BlockSpec((tr, tc), lambda i, j: (i, gl_col0 // cb + j)), blk, blk],
                  out_specs=blk, out_shape=jax.ShapeDtypeStruct((s, d), BF16), semantics=("parallel", "parallel"),
                  block_bytes=5 * tr * tc * 4)(proj, proj, bg, bl)


def _merge_bwd(proj, gg_col0, gl_col0, bg, bl, dm, name):
    s, d = bg.shape
    tr, tc = _tile(s, 256, SUBLANES), _tile(d, 1024)
    cb = tc // LANES

    def body(gg_ref, gl_ref, bg_ref, bl_ref, dm_ref, dgg_ref, dgl_ref, dbg_ref, dbl_ref):
        dmv = dm_ref[...]
        sg, sl = _sigmoid(gg_ref[...]), _sigmoid(gl_ref[...])
        dgg_ref[...] = (dmv * bg_ref[...] * sg * (1.0 - sg)).astype(BF16)
        dgl_ref[...] = (dmv * bl_ref[...] * sl * (1.0 - sl)).astype(BF16)
        dbg_ref[...] = (dmv * sg).astype(BF16)
        dbl_ref[...] = (dmv * sl).astype(BF16)

    blk = pl.BlockSpec((tr, tc), lambda i, j: (i, j))
    sh = jax.ShapeDtypeStruct((s, d), BF16)
    return _pcall(body, name=name, grid=(s // tr, d // tc),
                  in_specs=[pl.BlockSpec((tr, tc), lambda i, j: (i, gg_col0 // cb + j)),
                            pl.BlockSpec((tr, tc), lambda i, j: (i, gl_col0 // cb + j)), blk, blk, blk],
                  out_specs=(blk, blk, blk, blk), out_shape=(sh, sh, sh, sh), semantics=("parallel", "parallel"),
                  block_bytes=8 * tr * tc * 4)(proj, proj, bg, bl, dm)


def _sum_slots(slots, name):
    n, r, c = slots.shape
    tr = _tile(r, max(2 * SUBLANES, (1 << 19) // (c * 4)), 2 * SUBLANES)

    def body(s_ref, o_ref):
        acc = s_ref[0].astype(F32)
        for q in range(1, n):
            acc = acc + s_ref[q].astype(F32)
        o_ref[...] = acc

    return _pcall(body, name=name, grid=(r // tr,), in_specs=[pl.BlockSpec((n, tr, c), lambda i: (0, i, 0))],
                  out_specs=pl.BlockSpec((tr, c), lambda i: (i, 0)), out_shape=jax.ShapeDtypeStruct((r, c), F32),
                  semantics=("parallel",), block_bytes=(n + 1) * tr * c * 4)(slots)


def _adamw(w, g_parts, m, v, name):
    r, c = w.shape
    np_ = len(g_parts)
    tr = _tile(r, max(SUBLANES, (1 << 20) // (c * 4)), SUBLANES)
    c1 = 1.0 - ADAM_B1 ** ADAM_STEP
    c2 = 1.0 - ADAM_B2 ** ADAM_STEP

    def body(*refs):
        w_ref, m_ref, v_ref = refs[0], refs[1 + np_], refs[2 + np_]
        g_ref, d_ref, nm_ref, nv_ref = refs[3 + np_:]
        g = refs[1][...]
        for p in range(1, np_):
            g = g + refs[1 + p][...]
        nm = ADAM_B1 * m_ref[...] + (1.0 - ADAM_B1) * g
        nv = ADAM_B2 * v_ref[...] + (1.0 - ADAM_B2) * (g * g)
        g_ref[...] = g
        nm_ref[...] = nm
        nv_ref[...] = nv
        d_ref[...] = -ADAM_LR * ((nm / c1) / (jnp.sqrt(nv / c2) + ADAM_EPS) + ADAM_WD * w_ref[...])

    blk = pl.BlockSpec((tr, c), lambda i: (i, 0))
    sh = jax.ShapeDtypeStruct((r, c), F32)
    return _pcall(body, name=name, grid=(r // tr,), in_specs=[blk] * (3 + np_), out_specs=(blk,) * 4,
                  out_shape=(sh,) * 4, semantics=("parallel",), block_bytes=(7 + np_) * tr * c * 4)(
                      w, *g_parts, m, v)


def _pair_sum(core, mine, theirs, name):
    _, n, r, c = mine.shape
    tr = _tile(r, max(2 * SUBLANES, (1 << 19) // (c * 4)), 2 * SUBLANES)

    def body(core_ref, a_ref, b_ref, o_ref):
        o_ref[...] = (a_ref[...].astype(F32) + b_ref[...].astype(F32)).astype(BF16)

    return _pcall(body, name=name, grid=(n, r // tr),
                  in_specs=[pl.BlockSpec((None, None, tr, c), lambda q, i, core_ref: (core_ref[0], q, i, 0)),
                            pl.BlockSpec((None, tr, c), lambda q, i, core_ref: (q, i, 0))],
                  out_specs=pl.BlockSpec((None, tr, c), lambda q, i, core_ref: (q, i, 0)),
                  out_shape=jax.ShapeDtypeStruct((n, r, c), BF16), semantics=("parallel", "parallel"),
                  block_bytes=3 * tr * c * 4, scalar_prefetch=1)(core, mine, theirs)


def _sum_landed(chip, landed, own, name):
    n, r, c = landed.shape
    tr = _tile(r, max(2 * SUBLANES, (1 << 19) // (c * 4)), 2 * SUBLANES)

    def body(chip_ref, l_ref, o_ref, t_ref):
        acc = o_ref[...].astype(F32)
        for q in range(n):
            acc = acc + l_ref[q].astype(F32)
        t_ref[...] = acc

    return _pcall(body, name=name, grid=(r // tr,),
                  in_specs=[pl.BlockSpec((n, tr, c), lambda i, chip_ref: (0, i, 0)),
                            pl.BlockSpec((None, tr, c), lambda i, chip_ref: (chip_ref[0], i, 0))],
                  out_specs=pl.BlockSpec((tr, c), lambda i, chip_ref: (i, 0)),
                  out_shape=jax.ShapeDtypeStruct((r, c), F32), semantics=("parallel",),
                  block_bytes=(n + 3) * tr * c * 4, scalar_prefetch=1)(chip, landed, own)


def _adamw_quarters(core, w, g_mine, g_other, m, v, name):
    nl, nh, r, c = w.shape
    tr = _tile(r, max(SUBLANES, (1 << 19) // (c * 4)), SUBLANES)
    c1 = 1.0 - ADAM_B1 ** ADAM_STEP
    c2 = 1.0 - ADAM_B2 ** ADAM_STEP

    def body(core_ref, w_ref, *refs):
        g_refs, (m_ref, v_ref, g_ref, d_ref, nm_ref, nv_ref) = refs[:2 * nl], refs[2 * nl:]
        mine = pl.program_id(1) == core_ref[0]
        g = jnp.where(mine, g_refs[0][...], g_refs[nl][...])
        for l in range(1, nl):
            g = jnp.where(pl.program_id(0) == l, jnp.where(mine, g_refs[l][...], g_refs[nl + l][...]), g)
        nm = ADAM_B1 * m_ref[...] + (1.0 - ADAM_B1) * g
        nv = ADAM_B2 * v_ref[...] + (1.0 - ADAM_B2) * (g * g)
        g_ref[...] = g
        nm_ref[...] = nm
        nv_ref[...] = nv
        d_ref[...] = -ADAM_LR * ((nm / c1) / (jnp.sqrt(nv / c2) + ADAM_EPS) + ADAM_WD * w_ref[...])

    blk = pl.BlockSpec((None, None, tr, c), lambda l, hf, i, core_ref: (l, hf, i, 0))
    gblk = pl.BlockSpec((tr, c), lambda l, hf, i, core_ref: (i, 0))
    sh = jax.ShapeDtypeStruct(w.shape, F32)
    return _pcall(body, name=name, grid=(nl, nh, r // tr), in_specs=[blk] + [gblk] * (2 * nl) + [blk, blk],
                  out_specs=(blk,) * 4, out_shape=(sh,) * 4, semantics=("parallel", "parallel", "parallel"),
                  block_bytes=(7 + 2 * nl) * tr * c * 4, scalar_prefetch=1)(core, w, *g_mine, *g_other, m, v)


HBM_SPEC = pl.BlockSpec(memory_space=pltpu.HBM)


def _other_chips(x, y):
    return [(1 - x, y), (x, 1 - y), (1 - x, 1 - y)]


SEM_SPEC = pl.BlockSpec(memory_space=pltpu.SEMAPHORE)
DATAFLOW_EFFECT = pltpu.SideEffectType.DATAFLOW_SIDE_EFFECTING


def _split_start(name, bufs, n_copies, build):
    nb = len(bufs)

    def body(*refs):
        starts, _ = build(refs[:nb], refs[nb], refs[nb + 1])
        for cp in starts:
            cp.start()
        refs[-1][...] = jnp.zeros_like(refs[-1])

    out = pl.pallas_call(
        body, name=name,
        out_shape=(pltpu.SemaphoreType.DMA((n_copies,)), pltpu.SemaphoreType.DMA((n_copies,)),
                   *[pltpu.HBM(b.shape, b.dtype) for b in bufs], jax.ShapeDtypeStruct((SUBLANES, LANES), F32)),
        in_specs=[HBM_SPEC] * nb,
        out_specs=(SEM_SPEC, SEM_SPEC, *[HBM_SPEC] * nb, pl.BlockSpec(memory_space=pltpu.VMEM)),
        input_output_aliases={i: 2 + i for i in range(nb)},
        compiler_params=pltpu.CompilerParams(has_side_effects=DATAFLOW_EFFECT),
    )(*[pltpu.with_memory_space_constraint(b, pltpu.HBM) for b in bufs])
    return out[0], out[1], list(out[2:2 + nb]), out[2 + nb]


def _split_wait(name, send_sems, recv_sems, bufs, after, build):
    nb = len(bufs)

    def body(*refs):
        starts, waits = build(refs[:nb], refs[nb], refs[nb + 1])
        for cp in starts:
            cp.wait_send()
        for cp in waits:
            cp.wait_recv()

    out = pl.pallas_call(
        body, name=name, out_shape=tuple(pltpu.HBM(b.shape, b.dtype) for b in bufs),
        in_specs=[HBM_SPEC] * nb + [SEM_SPEC, SEM_SPEC, pl.BlockSpec(memory_space=pl.ANY)],
        out_specs=tuple([HBM_SPEC] * nb), input_output_aliases={i: i for i in range(nb)},
        compiler_params=pltpu.CompilerParams(has_side_effects=DATAFLOW_EFFECT),
    )(*bufs, send_sems, recv_sems, after)
    return list(out)


def _gather_ici_copies(nt, refs, send_sems, recv_sems):
    srcs, lands = refs[:nt], refs[nt:]
    x, y, c = lax.axis_index("x"), lax.axis_index("y"), lax.axis_index("c")
    me = 2 * x + y
    starts, waits = [], []
    for t in range(nt):
        for j, (px, py) in enumerate(_other_chips(x, y)):
            def copy(slot, t=t, j=j, px=px, py=py):
                return pltpu.make_async_remote_copy(
                    src_ref=srcs[t].at[c], dst_ref=lands[t].at[slot].at[c], send_sem=send_sems.at[3 * t + j],
                    recv_sem=recv_sems.at[3 * t + j], device_id=(px, py, c), device_id_type=pl.DeviceIdType.MESH)
            starts.append(copy(me))
            waits.append(copy(2 * px + py))
    return starts, waits


def _gather_d2d_copies(nt, refs, send_sems, recv_sems):
    x, y, c = lax.axis_index("x"), lax.axis_index("y"), lax.axis_index("c")
    starts, waits = [], []
    for t in range(nt):
        for j, (px, py) in enumerate(_other_chips(x, y)):
            def copy(half, t=t, j=j, px=px, py=py):
                place = refs[t].at[2 * px + py].at[half]
                return pltpu.make_async_remote_copy(
                    src_ref=place, dst_ref=place, send_sem=send_sems.at[3 * t + j], recv_sem=recv_sems.at[3 * t + j],
                    device_id=(x, y, 1 - c), device_id_type=pl.DeviceIdType.MESH)
            starts.append(copy(c))
            waits.append(copy(1 - c))
    return starts, waits


def _scatter_ici_copies(nt, refs, send_sems, recv_sems):
    srcs, lands = refs[:nt], refs[nt:]
    x, y, c = lax.axis_index("x"), lax.axis_index("y"), lax.axis_index("c")
    me = 2 * x + y
    starts, waits = [], []
    for t in range(nt):
        for j, (px, py) in enumerate(_other_chips(x, y)):
            def copy(slot, t=t, j=j, px=px, py=py):
                return pltpu.make_async_remote_copy(
                    src_ref=srcs[t].at[2 * px + py], dst_ref=lands[t].at[slot], send_sem=send_sems.at[3 * t + j],
                    recv_sem=recv_sems.at[3 * t + j], device_id=(px, py, c), device_id_type=pl.DeviceIdType.MESH)
            starts.append(copy(me))
            waits.append(copy(2 * px + py))
    return starts, waits


def _sibling_exchange(arrs, other_layer, name):
    n = len(arrs)

    def body(*refs):
        ins, outs = refs[:n], refs[n:2 * n]
        send_sems, recv_sems = refs[2 * n:]
        c = lax.axis_index("c")
        sib = (lax.axis_index("x"), lax.axis_index("y"), 1 - c)
        copies = [pltpu.make_async_remote_copy(src_ref=ins[t].at[1 - c] if other_layer else ins[t], dst_ref=outs[t],
                                               send_sem=send_sems.at[t], recv_sem=recv_sems.at[t], device_id=sib,
                                               device_id_type=pl.DeviceIdType.MESH) for t in range(n)]
        for cp in copies:
            cp.start()
        for cp in copies:
            cp.wait_recv()
        for cp in copies:
            cp.wait_send()

    return pl.pallas_call(
        body, name=name, in_specs=[HBM_SPEC] * n, out_specs=(HBM_SPEC,) * n,
        out_shape=tuple(jax.ShapeDtypeStruct(a.shape[1:] if other_layer else a.shape, a.dtype) for a in arrs),
        scratch_shapes=[pltpu.SemaphoreType.DMA((n,)), pltpu.SemaphoreType.DMA((n,))])(*arrs)


def _all_devices_gather(buf, name):
    def body(in_ref, out_ref, send_sems, recv_sems, local_sem):
        x, y, c = lax.axis_index("x"), lax.axis_index("y"), lax.axis_index("c")
        me = 4 * x + 2 * y + c

        def peer(mask):
            px = 1 - x if mask & 4 else x
            py = 1 - y if mask & 2 else y
            pc = 1 - c if mask & 1 else c
            return px, py, pc

        def remote(mask, dst_slot):
            return pltpu.make_async_remote_copy(
                src_ref=in_ref, dst_ref=out_ref.at[dst_slot], send_sem=send_sems.at[mask - 1],
                recv_sem=recv_sems.at[mask - 1], device_id=peer(mask), device_id_type=pl.DeviceIdType.MESH)

        lc = pltpu.make_async_copy(in_ref, out_ref.at[me], local_sem)
        lc.start()
        sends = [remote(mask, me) for mask in range(1, N_DEVICES)]
        for cp in sends:
            cp.start()
        for mask in range(1, N_DEVICES):
            px, py, pc = peer(mask)
            remote(mask, 4 * px + 2 * py + pc).wait_recv()
        for cp in sends:
            cp.wait_send()
        lc.wait()

    return pl.pallas_call(
        body, name=name, in_specs=[HBM_SPEC], out_specs=HBM_SPEC,
        out_shape=jax.ShapeDtypeStruct((N_DEVICES,) + buf.shape, buf.dtype),
        scratch_shapes=[pltpu.SemaphoreType.DMA((N_DEVICES - 1,)), pltpu.SemaphoreType.DMA((N_DEVICES - 1,)),
                        pltpu.SemaphoreType.DMA])(buf)


def _pad_lanes(vec):
    return jnp.pad(vec.astype(F32), (0, LANES - vec.shape[0])).reshape(1, LANES)


def _layer_fwd(x, wl, fetch, dm, tag):
    heads, gw, lw, d = dm['heads'], dm['gw'], dm['lw'], dm['d']
    h = _rms_fwd(x, wl['attn_norm'], f"rms1_fwd{tag}")
    wl.update(fetch('in', h))
    proj = _matmul(h, wl['w_in_p'], mode='nn', name=f"proj{tag}")
    alog, dtb = _pad_lanes(wl['gdn_a_log']), _pad_lanes(wl['gdn_dt_bias'])
    qkv = _gdn_pre_fwd(proj, wl['gdn_conv_w'], heads, f"gdn_pre_fwd{tag}")
    o, t_all, s0_all = _gdn_fwd(qkv, proj, alog, dtb, heads, f"gdn_fwd{tag}")
    o_gdn = _gdn_post_fwd(o, proj, dm['z_blk'], wl['gdn_norm'], f"gdn_post_fwd{tag}")
    xc = _conv_bias_fwd(proj, dm['xb_blk'], wl['lru_conv_w'], wl['lru_conv_b'], f"lru_conv_fwd{tag}")
    hseq, o_lru = _lru_fwd(xc, proj, dm['yb_blk'], wl['lru_w_a'], wl['lru_w_x'], wl['lru_b_a'], wl['lru_b_x'],
                           wl['lru_lambda'], f"lru_fwd{tag}")
    wl.update(fetch('mix', o))
    bg = _matmul(o_gdn, wl['w_branch_gdn'], mode='nn', name=f"branch_gdn{tag}")
    bl = _matmul(o_lru, wl['w_branch_lru'], mode='nn', name=f"branch_lru{tag}")
    merged = _merge_fwd(proj, dm['gg_blk'], dm['gl_blk'], bg, bl, f"merge_fwd{tag}")
    wl.update(fetch('mlp', bg))
    x_mid = _matmul(merged, wl['w_out'], mode='nn', add=x, name=f"out_proj{tag}")
    h2 = _rms_fwd(x_mid, wl['mlp_norm'], f"rms2_fwd{tag}")
    ur, act = _matmul(h2, wl['w_up'], mode='nn', epilogue='relu2', name=f"mlp_up{tag}")
    x_out = _matmul(act, wl['w_down'], mode='nn', add=x_mid, name=f"mlp_down{tag}")
    saved = dict(x=x, h=h, proj=proj, qkv=qkv, o=o, t_all=t_all, s0_all=s0_all, o_gdn=o_gdn, xc=xc, hseq=hseq,
                 o_lru=o_lru, bg=bg, bl=bl, merged=merged, x_mid=x_mid, h2=h2, ur=ur, act=act, alog=alog, dtb=dtb)
    return x_out, saved


def _layer_bwd(dx_out, dx_out_b, wl, sv, hook, dm, tag):
    heads, gw, lw, d = dm['heads'], dm['gw'], dm['lw'], dm['d']
    g = {}
    du = _matmul(dx_out_b, wl['w_down'], mode='nt', epilogue='mul2x', extra=sv['ur'], out_dtype=BF16,
                 name=f"d_mlp_act{tag}")
    g['w_down'] = _matmul(sv['act'], dx_out_b, mode='tn', out_dtype=BF16, name=f"dw_down{tag}")
    g['w_up'] = _matmul(sv['h2'], du, mode='tn', out_dtype=BF16, name=f"dw_up{tag}")
    du = hook('mlp', g, du)
    dh2 = _matmul(du, wl['w_up'], mode='nt', name=f"d_h2{tag}")
    dx_mid, dx_mid_b, g['mlp_norm'] = _rms_bwd(sv['x_mid'], wl['mlp_norm'], dh2, dx_out, f"rms2_bwd{tag}")
    dmerged = _matmul(dx_mid_b, wl['w_out'], mode='nt', name=f"d_merged{tag}")
    g['w_out'] = _matmul(sv['merged'], dx_mid_b, mode='tn', out_dtype=BF16, name=f"dw_out{tag}")
    dgg, dgl, dbg, dbl = _merge_bwd(sv['proj'], dm['gg_blk'], dm['gl_blk'], sv['bg'], sv['bl'], dmerged,
                                    f"merge_bwd{tag}")
    g['w_branch_gdn'] = _matmul(sv['o_gdn'], dbg, mode='tn', out_dtype=BF16, name=f"dw_branch_gdn{tag}")
    g['w_branch_lru'] = _matmul(sv['o_lru'], dbl, mode='tn', out_dtype=BF16, name=f"dw_branch_lru{tag}")
    dbg = hook('mix', g, dbg)
    do_gdn = _matmul(dbg, wl['w_branch_gdn'], mode='nt', name=f"d_o_gdn{tag}")
    do_lru = _matmul(dbl, wl['w_branch_lru'], mode='nt', name=f"d_o_lru{tag}")
    d_o, dz, dgn = _gdn_post_bwd(sv['o'], sv['proj'], dm['z_blk'], wl['gdn_norm'], do_gdn, f"gdn_post_bwd{tag}")
    g['gdn_norm'] = dgn.reshape(-1)
    dq, dk, dv, dgb = _gdn_bwd(sv['qkv'], sv['proj'], sv['alog'], sv['dtb'], sv['t_all'], sv['s0_all'], d_o, heads,
                               f"gdn_bwd{tag}")
    dq = hook('gdn_done', g, dq)
    dqkv_n = jnp.concatenate([dq, dk, dv], axis=1)
    dconv = _gdn_pre_bwd(sv['proj'], wl['gdn_conv_w'], dqkv_n, heads, f"gdn_pre_bwd{tag}")
    dqkv, g['gdn_conv_w'], _ = _conv_bwd(dconv, sv['proj'], 0, wl['gdn_conv_w'], f"gdn_conv_bwd{tag}")
    dab, dab_sum = _dab_reduce(dgb, f"dab_reduce{tag}")
    g['gdn_dt_bias'] = dab_sum[0, :heads]
    g['gdn_a_log'] = dab_sum[0, 2 * heads:3 * heads]
    dxc, dyb, g['lru_w_a'], g['lru_w_x'], dba, dbx, dlam = _lru_bwd(
        do_lru, sv['xc'], sv['hseq'], sv['proj'], dm['yb_blk'], wl['lru_w_a'], wl['lru_w_x'], wl['lru_b_a'],
        wl['lru_b_x'], wl['lru_lambda'], f"lru_bwd{tag}")
    g['lru_b_a'], g['lru_b_x'], g['lru_lambda'] = dba.reshape(-1), dbx.reshape(-1), dlam.reshape(-1)
    dxb, g['lru_conv_w'], dcb = _conv_bwd(dxc, sv['proj'], dm['xb_blk'], wl['lru_conv_w'], f"lru_conv_bwd{tag}")
    g['lru_conv_b'] = dcb.reshape(-1)
    dproj = jnp.concatenate([dqkv.astype(BF16), dz.astype(BF16), dxb.astype(BF16), dyb.astype(BF16), dgg, dgl,
                             dab.astype(BF16)], axis=1)
    g['w_in_p'] = _matmul(sv['h'], dproj, mode='tn', out_dtype=BF16, name=f"dw_in{tag}")
    dproj = hook('in', g, dproj)
    dh = _matmul(dproj, wl['w_in_p'], mode='nt', name=f"d_h{tag}")
    dx_in, dx_in_b, g['attn_norm'] = _rms_bwd(sv['x'], wl['attn_norm'], dh, dx_mid, f"rms1_bwd{tag}")
    g['attn_norm'] = g['attn_norm'].reshape(-1)
    g['mlp_norm'] = g['mlp_norm'].reshape(-1)
    return dx_in, dx_in_b, g


def _dims(d, heads, lw):
    gw = heads * LANES
    nab = 2 * heads
    blk = dict(z_blk=3 * heads, xb_blk=4 * heads, yb_blk=4 * heads + lw // LANES)
    gg0 = 4 * gw + 2 * lw
    return dict(d=d, heads=heads, gw=gw, lw=lw, nab=nab, gg_blk=gg0 // LANES, gl_blk=(gg0 + d) // LANES,
                main=gg0 + 2 * d, np=gg0 + 2 * d + LANES, **blk)


def _pad_w_in(w_in, dm):
    c0 = 4 * dm['gw']
    nab = dm['nab']
    return jnp.concatenate([w_in[:, :c0], w_in[:, c0 + nab:], w_in[:, c0:c0 + nab],
                            jnp.zeros((w_in.shape[0], LANES - nab), w_in.dtype)], axis=1)


def _unpad_w_in(gp, dm):
    c0 = 4 * dm['gw']
    nab = dm['nab']
    main = dm['main']
    return jnp.concatenate([gp[:, :c0], gp[:, main:main + nab], gp[:, c0:main]], axis=1)


def _local_step(x, target, layers, fetchers, hooks, final_norm, dm):
    saved = []
    cur = x
    for li, wl in enumerate(layers):
        cur, sv = _layer_fwd(cur, wl, fetchers[li], dm, f"_l{li}")
        saved.append(sv)
    loss_blk, dx, dx_b, dfin = _loss_head(cur, final_norm, target, "loss_head")
    grads = [None] * len(layers)
    for li in reversed(range(len(layers))):
        dx, dx_b, grads[li] = _layer_bwd(dx, dx_b, layers[li], saved[li], hooks[li], dm, f"_l{li}")
    return loss_blk[0, 0], dx, grads, dfin.reshape(-1)


def kernel(x, attn_norm, w_in, gdn_conv_w, gdn_a_log, gdn_dt_bias, gdn_norm, lru_conv_w, lru_conv_b, lru_w_a, lru_b_a, lru_w_x, lru_b_x, lru_lambda, w_branch_gdn, w_branch_lru, w_out, mlp_norm, w_up, w_down, final_norm, loss_target, m_attn_norm, m_w_in, m_gdn_conv_w, m_gdn_a_log, m_gdn_dt_bias, m_gdn_norm, m_lru_conv_w, m_lru_conv_b, m_lru_w_a, m_lru_b_a, m_lru_w_x, m_lru_b_x, m_lru_lambda, m_w_branch_gdn, m_w_branch_lru, m_w_out, m_mlp_norm, m_w_up, m_w_down, m_final_norm, v_attn_norm, v_w_in, v_gdn_conv_w, v_gdn_a_log, v_gdn_dt_bias, v_gdn_norm, v_lru_conv_w, v_lru_conv_b, v_lru_w_a, v_lru_b_a, v_lru_w_x, v_lru_b_x, v_lru_lambda, v_w_branch_gdn, v_w_branch_lru, v_w_out, v_mlp_norm, v_w_up, v_w_down, v_final_norm):
    w = dict(attn_norm=attn_norm, w_in=w_in, gdn_conv_w=gdn_conv_w, gdn_a_log=gdn_a_log, gdn_dt_bias=gdn_dt_bias,
             gdn_norm=gdn_norm, lru_conv_w=lru_conv_w, lru_conv_b=lru_conv_b, lru_w_a=lru_w_a, lru_b_a=lru_b_a,
             lru_w_x=lru_w_x, lru_b_x=lru_b_x, lru_lambda=lru_lambda, w_branch_gdn=w_branch_gdn,
             w_branch_lru=w_branch_lru, w_out=w_out, mlp_norm=mlp_norm, w_up=w_up, w_down=w_down,
             final_norm=final_norm)
    m = dict(attn_norm=m_attn_norm, w_in=m_w_in, gdn_conv_w=m_gdn_conv_w, gdn_a_log=m_gdn_a_log,
             gdn_dt_bias=m_gdn_dt_bias, gdn_norm=m_gdn_norm, lru_conv_w=m_lru_conv_w, lru_conv_b=m_lru_conv_b,
             lru_w_a=m_lru_w_a, lru_b_a=m_lru_b_a, lru_w_x=m_lru_w_x, lru_b_x=m_lru_b_x, lru_lambda=m_lru_lambda,
             w_branch_gdn=m_w_branch_gdn, w_branch_lru=m_w_branch_lru, w_out=m_w_out, mlp_norm=m_mlp_norm,
             w_up=m_w_up, w_down=m_w_down, final_norm=m_final_norm)
    v = dict(attn_norm=v_attn_norm, w_in=v_w_in, gdn_conv_w=v_gdn_conv_w, gdn_a_log=v_gdn_a_log,
             gdn_dt_bias=v_gdn_dt_bias, gdn_norm=v_gdn_norm, lru_conv_w=v_lru_conv_w, lru_conv_b=v_lru_conv_b,
             lru_w_a=v_lru_w_a, lru_b_a=v_lru_b_a, lru_w_x=v_lru_w_x, lru_b_x=v_lru_b_x, lru_lambda=v_lru_lambda,
             w_branch_gdn=v_w_branch_gdn, w_branch_lru=v_w_branch_lru, w_out=v_w_out, mlp_norm=v_mlp_norm,
             w_up=v_w_up, w_down=v_w_down, final_norm=v_final_norm)
    n_layers = attn_norm.shape[0]
    d = x.shape[-1]
    heads = gdn_a_log.shape[-1]
    lw = lru_conv_b.shape[-1]
    dm = _dims(d, heads, lw)
    big_names = list(BIG_SHARD_AXIS)
    conv_names = list(CONV_SHARD_AXIS)
    chip = 2 * lax.axis_index("x") + lax.axis_index("y")

    conv_flat = jnp.concatenate([w[n].reshape(-1) for n in conv_names])
    conv_rows = -(-conv_flat.shape[0] // (SUBLANES * LANES)) * SUBLANES
    conv_buf = jnp.pad(conv_flat, (0, conv_rows * LANES - conv_flat.shape[0])).reshape(conv_rows, LANES)
    conv_all = _all_devices_gather(conv_buf, "conv_allgather").reshape(N_CHIPS, 2, -1)[:, 0]
    conv_full, off = {}, 0
    for n in conv_names:
        shard = w[n]
        parts = conv_all[:, off:off + shard.size].reshape((N_CHIPS,) + shard.shape)
        conv_full[n] = jnp.concatenate([parts[q] for q in range(N_CHIPS)], axis=CONV_SHARD_AXIS[n])
        off += shard.size

    def start_gather(li, group):
        halves, lands = [], []
        for n in GATHER_GROUPS[group]:
            s = w[n][li].astype(BF16)
            hv = s.reshape((2, s.shape[0] // 2) + s.shape[1:])
            halves.append(hv)
            lands.append(lax.dynamic_update_index_in_dim(lax.empty((N_CHIPS,) + hv.shape, BF16), hv, chip, 0))
        nt = len(halves)
        return _split_start(f"wgather_{group}_l{li}_ici_start", halves + lands, 3 * nt,
                            functools.partial(_gather_ici_copies, nt))

    pending = {(li, group): start_gather(li, group) for li in range(n_layers) for group in GATHER_GROUPS}

    def make_fetch(li):
        def fetch(group, after):
            names = GATHER_GROUPS[group]
            nt = len(names)
            send, recv, bufs, _ = pending.pop((li, group))
            bufs = _split_wait(f"wgather_{group}_l{li}_ici_wait", send, recv, bufs, after,
                               functools.partial(_gather_ici_copies, nt))
            send, recv, lands, token = _split_start(f"wgather_{group}_l{li}_d2d_start", bufs[nt:], 3 * nt,
                                                    functools.partial(_gather_d2d_copies, nt))
            lands = _split_wait(f"wgather_{group}_l{li}_d2d_wait", send, recv, lands, token,
                                functools.partial(_gather_d2d_copies, nt))
            out = {}
            for n, land in zip(names, lands):
                slots = land.reshape((N_CHIPS, 2 * land.shape[2]) + land.shape[3:])
                out[n] = jnp.concatenate([slots[q] for q in range(N_CHIPS)], axis=BIG_SHARD_AXIS[n] - 1)
            if 'w_in' in out:
                out['w_in_p'] = _pad_w_in(out.pop('w_in'), dm)
            return out
        return fetch

    layers = []
    for li in range(n_layers):
        wl = {n: w[n][li] for n in SMALL_NAMES if n != 'final_norm' and n not in CONV_SHARD_AXIS}
        for n in conv_names:
            wl[n] = conv_full[n][li]
        layers.append(wl)
    layers[0]['attn_norm'] = lax.optimization_barrier(
        (layers[0]['attn_norm'], *[handle[3] for handle in pending.values()]))[0]

    core = lax.axis_index("c").astype(jnp.int32).reshape(1)
    chip_op = chip.astype(jnp.int32).reshape(1)
    in_flight, reduced = {}, {}

    def reduce_begin(group, li, g):
        names = GATHER_GROUPS[group]
        nt = len(names)
        contrib = []
        for n in names:
            full = _unpad_w_in(g['w_in_p'], dm) if n == 'w_in' else g[n]
            pieces = jnp.stack(jnp.split(full, N_CHIPS, axis=BIG_SHARD_AXIS[n] - 1), axis=0)
            rows_half = pieces.shape[1] // 2
            contrib.append(jnp.swapaxes(pieces.reshape((N_CHIPS, 2, rows_half) + pieces.shape[2:]), 0, 1))
        theirs = _sibling_exchange(contrib, True, f"gsend_{group}_l{li}")
        sums = [_pair_sum(core, mine, th, f"gpair_{n}_l{li}") for n, mine, th in zip(names, contrib, theirs)]
        lands = [jnp.zeros(sm.shape, BF16) for sm in sums]
        send, recv, bufs, token = _split_start(f"gscatter_{group}_l{li}_start", sums + lands, 3 * nt,
                                               functools.partial(_scatter_ici_copies, nt))
        in_flight[(group, li)] = (send, recv, bufs)
        return token

    def reduce_end(group, li, after):
        names = GATHER_GROUPS[group]
        nt = len(names)
        send, recv, bufs = in_flight.pop((group, li))
        bufs = _split_wait(f"gscatter_{group}_l{li}_wait", send, recv, bufs, after,
                           functools.partial(_scatter_ici_copies, nt))
        totals = [_sum_landed(chip_op, land, own, f"gtotal_{n}_l{li}")
                  for n, own, land in zip(names, bufs[:nt], bufs[nt:])]
        others = _sibling_exchange(totals, False, f"gswap_{group}_l{li}")
        for n, mine, other in zip(names, totals, others):
            reduced[(n, li)] = (mine, other)

    finish_at = {('gdn_done', li): [('mlp', li)] for li in range(n_layers)}
    finish_at.update({('in', li): [('mix', li)] for li in range(n_layers)})
    finish_at.update({('mix', li): [('in', li + 1)] for li in range(n_layers - 1)})

    def make_hook(li):
        def hook(event, g, value):
            after = g[GATHER_GROUPS[event][-1] if event != 'in' else 'w_in_p'] if event in GATHER_GROUPS else value
            for job in finish_at.get((event, li), []):
                reduce_end(*job, after)
            if event in GATHER_GROUPS:
                value = lax.optimization_barrier((value, reduce_begin(event, li, g)))[0]
            return value
        return hook

    loss_local, dx, grads, dfin = _local_step(x[0], loss_target[0], layers, [make_fetch(li) for li in range(n_layers)],
                                              [make_hook(li) for li in range(n_layers)], final_norm, dm)
    loss = lax.psum(loss_local, MESH_AXES)
    reduce_end('in', 0, dx)

    small_g = {n: jnp.stack([grads[li][n] for li in range(n_layers)], axis=0)
               for n in SMALL_NAMES if n != 'final_norm'}
    small_g['final_norm'] = dfin
    flat = jnp.concatenate([small_g[n].reshape(-1) for n in SMALL_NAMES])
    n_flat = flat.shape[0]
    row_unit = 32 * SUBLANES
    rows = -(-n_flat // (row_unit * LANES)) * row_unit
    buf = jnp.pad(flat, (0, rows * LANES - n_flat)).reshape(rows, LANES)
    everyone = _all_devices_gather(buf, "small_grad_allgather")
    small_sum = _sum_slots(everyone, "small_grad_sum").reshape(-1)
    small_red = {}
    off = 0
    for n in SMALL_NAMES:
        size = small_g[n].size
        small_red[n] = small_sum[off:off + size].reshape(small_g[n].shape)
        off += size
    for n, ax in CONV_SHARD_AXIS.items():
        width = w[n].shape[ax]
        small_red[n] = lax.dynamic_slice_in_dim(small_red[n], chip * width, width, axis=ax)

    out_g, out_d, out_m, out_v = {}, {}, {}, {}
    for n in big_names:
        quarters = (n_layers, 2, w[n].shape[1] // 2, w[n].shape[2])
        res = _adamw_quarters(core, w[n].reshape(quarters), [reduced[(n, li)][0] for li in range(n_layers)],
                              [reduced[(n, li)][1] for li in range(n_layers)], m[n].reshape(quarters),
                              v[n].reshape(quarters), f"adamw_{n}")
        out_g[n], out_d[n], out_m[n], out_v[n] = (r.reshape(w[n].shape) for r in res)

    def pack(tree):
        fl = jnp.concatenate([tree[n].reshape(-1) for n in SMALL_NAMES])
        return jnp.pad(fl, (0, rows * LANES - fl.shape[0])).reshape(rows, LANES)

    res = _adamw(pack(w), [pack(small_red)], pack(m), pack(v), "adamw_small")
    for r, dst in zip(res, (out_g, out_d, out_m, out_v)):
        fl = r.reshape(-1)
        off = 0
        for n in SMALL_NAMES:
            dst[n] = fl[off:off + w[n].size].reshape(w[n].shape)
            off += w[n].size

    return (loss, dx[None], *[out_g[n] for n in WEIGHT_NAMES], *[out_d[n] for n in WEIGHT_NAMES],
            *[out_m[n] for n in WEIGHT_NAMES], *[out_v[n] for n in WEIGHT_NAMES])
```

```python
import functools

import jax
import jax.numpy as jnp
from jax import lax
from jax.experimental import pallas as pl
from jax.experimental.pallas import tpu as pltpu

F32 = jnp.float32
BF16 = jnp.bfloat16

LANES = 128
SUBLANES = 8
VMEM_BYTES = 64 * 1024 * 1024
GDN_CHUNK = 64
CONV_WIDTH = 4
RMS_EPS = 1e-6
L2_EPS = 1e-6
LRU_C = 8.0
ADAM_LR = 0.001
ADAM_B1 = 0.9
ADAM_B2 = 0.999
ADAM_EPS = 1e-08
ADAM_WD = 0.01
ADAM_STEP = 10
MESH_AXES = ("x", "y", "c")
N_CHIPS = 4
N_DEVICES = 8

INPUT_NAMES = ['x', 'attn_norm', 'w_in', 'gdn_conv_w', 'gdn_a_log', 'gdn_dt_bias', 'gdn_norm', 'lru_conv_w',
               'lru_conv_b', 'lru_w_a', 'lru_b_a', 'lru_w_x', 'lru_b_x', 'lru_lambda', 'w_branch_gdn',
               'w_branch_lru', 'w_out', 'mlp_norm', 'w_up', 'w_down', 'final_norm']
WEIGHT_NAMES = INPUT_NAMES[1:]
BIG_SHARD_AXIS = {'w_in': 2, 'w_branch_gdn': 2, 'w_branch_lru': 2, 'w_out': 1, 'w_up': 2, 'w_down': 1}
CONV_SHARD_AXIS = {'gdn_conv_w': 2, 'lru_conv_w': 2}
GATHER_GROUPS = {'in': ['w_in'], 'mix': ['w_branch_gdn', 'w_branch_lru', 'w_out'], 'mlp': ['w_up', 'w_down']}
SMALL_NAMES = [n for n in WEIGHT_NAMES if n not in BIG_SHARD_AXIS]


def _tile(n, target, unit=LANES):
    best = None
    t = unit
    while t <= min(n, target):
        if n % t == 0:
            best = t
        t += unit
    return n if best is None else best


def _vmem_limit(block_bytes):
    return int(min(max(3 * block_bytes + (8 << 20), 24 << 20), VMEM_BYTES - (8 << 20)))


def _nbytes(shape, dtype):
    n = 1
    for s in shape:
        n *= s
    return n * jnp.dtype(dtype).itemsize


def _pcall(body, *, name, grid, in_specs, out_specs, out_shape, scratch_shapes=(), semantics=None, block_bytes=0,
           scalar_prefetch=0):
    params = dict(vmem_limit_bytes=_vmem_limit(block_bytes))
    if semantics is not None:
        params['dimension_semantics'] = semantics
    if scalar_prefetch:
        grid_spec = pltpu.PrefetchScalarGridSpec(num_scalar_prefetch=scalar_prefetch, grid=grid, in_specs=in_specs,
                                                 out_specs=out_specs, scratch_shapes=list(scratch_shapes))
        return pl.pallas_call(body, name=name, grid_spec=grid_spec, out_shape=out_shape,
                              compiler_params=pltpu.CompilerParams(**params))
    return pl.pallas_call(body, name=name, grid=grid, in_specs=in_specs, out_specs=out_specs, out_shape=out_shape,
                          scratch_shapes=list(scratch_shapes), compiler_params=pltpu.CompilerParams(**params))


def _dot(a, b):
    return jnp.dot(a.astype(BF16), b.astype(BF16), preferred_element_type=F32)


def _dot_nt(a, b):
    return lax.dot_general(a.astype(BF16), b.astype(BF16), (((1,), (1,)), ((), ())), preferred_element_type=F32)


def _dot_tn(a, b):
    return lax.dot_general(a.astype(BF16), b.astype(BF16), (((0,), (0,)), ((), ())), preferred_element_type=F32)


def _sigmoid(x):
    return 1.0 / (1.0 + jnp.exp(-x))


def _log1p(u):
    return jnp.where(u < 1e-3, u * (1.0 - u * (0.5 - u * (1.0 / 3.0))), jnp.log(1.0 + u))


def _softplus(x):
    return jnp.maximum(x, 0.0) + _log1p(jnp.exp(-jnp.abs(x)))


_GELU_K = 0.7978845608028654


def _gelu_and_grad(x):
    inner = _GELU_K * (x + 0.044715 * x * x * x)
    th = jnp.tanh(inner)
    g = 0.5 * x * (1.0 + th)
    dg = 0.5 * (1.0 + th) + 0.5 * x * (1.0 - th * th) * _GELU_K * (1.0 + 3.0 * 0.044715 * x * x)
    return g, dg


MATMUL_TK_MAX = 3584


def _matmul(a, b, *, mode, name, out_dtype=F32, add=None, epilogue=None, extra=None, tm=512, tn=1024, tk=2048):
    if mode == 'nn':
        (m, k), (k2, n) = a.shape, b.shape
    elif mode == 'nt':
        (m, k), (n, k2) = a.shape, b.shape
    else:
        (k, m), (k2, n) = a.shape, b.shape
    assert k == k2, (a.shape, b.shape, mode)
    tm, tn = _tile(m, tm), _tile(n, tn)
    tk = _tile(k, tk)
    if k // tk > 2 * (-(-k // MATMUL_TK_MAX)):
        tk = _tile(k, MATMUL_TK_MAX)
    nk = k // tk
    dims = {'nn': (((1,), (0,)), ((), ())), 'nt': (((1,), (1,)), ((), ())), 'tn': (((0,), (0,)), ((), ()))}[mode]
    a_bytes, b_bytes = _nbytes(a.shape, a.dtype), _nbytes(b.shape, b.dtype)
    rows_outer = nk > 1 or a_bytes + (m // tm) * b_bytes <= b_bytes + (n // tn) * a_bytes

    def ij(g0, g1):
        return (g0, g1) if rows_outer else (g1, g0)

    def spec(shape, pick):
        return pl.BlockSpec(shape, lambda g0, g1, kk: pick(*ij(g0, g1), kk))

    a_spec = spec((tk, tm), lambda i, j, kk: (kk, i)) if mode == 'tn' else spec((tm, tk), lambda i, j, kk: (i, kk))
    b_spec = spec((tn, tk), lambda i, j, kk: (j, kk)) if mode == 'nt' else spec((tk, tn), lambda i, j, kk: (kk, j))
    o_spec = spec((tm, tn), lambda i, j, kk: (i, j))
    operands, in_specs = [a, b], [a_spec, b_spec]
    if add is not None:
        operands.append(add)
        in_specs.append(o_spec)
    if extra is not None:
        operands.append(extra)
        in_specs.append(o_spec)
    n_in = len(operands)
    if epilogue == 'relu2':
        out_shape = (jax.ShapeDtypeStruct((m, n), BF16), jax.ShapeDtypeStruct((m, n), BF16))
        out_specs = (o_spec, o_spec)
    else:
        out_shape = jax.ShapeDtypeStruct((m, n), out_dtype)
        out_specs = o_spec

    def body(*refs):
        a_ref, b_ref = refs[0], refs[1]
        outs = refs[n_in:n_in + n_out]

        def finish(p):
            if add is not None:
                p = p + refs[2][...]
            if epilogue == 'relu2':
                ur = jnp.maximum(p, 0.0)
                outs[0][...] = ur.astype(BF16)
                outs[1][...] = (ur * ur).astype(BF16)
            elif epilogue == 'mul2x':
                outs[0][...] = (p * 2.0 * refs[n_in - 1][...].astype(F32)).astype(out_dtype)
            else:
                outs[0][...] = p.astype(out_dtype)

        prod = lax.dot_general(a_ref[...].astype(BF16), b_ref[...].astype(BF16), dims, preferred_element_type=F32)
        if nk == 1:
            finish(prod)
            return
        acc_ref = refs[-1]
        kk = pl.program_id(2)

        @pl.when(kk == 0)
        def _():
            acc_ref[...] = prod

        @pl.when((kk > 0) & (kk < nk - 1))
        def _():
            acc_ref[...] += prod

        @pl.when(kk == nk - 1)
        def _():
            finish(acc_ref[...] + prod)

    n_out = 2 if epilogue == 'relu2' else 1
    bb = (_nbytes((tm, tk), a.dtype) + _nbytes((tk, tn), b.dtype) + 3 * _nbytes((tm, tn), F32))
    grid = (m // tm, n // tn, nk) if rows_outer else (n // tn, m // tm, nk)
    return _pcall(body, name=name, grid=grid, in_specs=in_specs, out_specs=out_specs, out_shape=out_shape,
                  scratch_shapes=[pltpu.VMEM((tm, tn), F32)] if nk > 1 else [],
                  semantics=("parallel", "parallel", "arbitrary"), block_bytes=bb)(*operands)


def _row_tile(s, d, target_bytes=1 << 20):
    return _tile(s, max(SUBLANES, target_bytes // (4 * d)), SUBLANES)


def _rms_fwd(x, gain, name):
    s, d = x.shape
    tr = _row_tile(s, d)

    def body(x_ref, g_ref, h_ref):
        xv = x_ref[...]
        r = lax.rsqrt(jnp.mean(xv * xv, axis=-1, keepdims=True) + RMS_EPS)
        h_ref[...] = (xv * r * g_ref[...]).astype(BF16)

    row = pl.BlockSpec((tr, d), lambda i: (i, 0))
    return _pcall(body, name=name, grid=(s // tr,), in_specs=[row, pl.BlockSpec((1, d), lambda i: (0, 0))],
                  out_specs=row, out_shape=jax.ShapeDtypeStruct((s, d), BF16), semantics=("parallel",),
                  block_bytes=2 * tr * d * 4)(x, gain.reshape(1, d))


def _rms_bwd(x, gain, dh, dres, name):
    s, d = x.shape
    tr = _row_tile(s, d, 1 << 19)

    def body(x_ref, g_ref, dh_ref, dres_ref, dx_ref, dxb_ref, dg_ref):
        xv = x_ref[...]
        r = lax.rsqrt(jnp.mean(xv * xv, axis=-1, keepdims=True) + RMS_EPS)
        xh = xv * r
        dhv = dh_ref[...]
        dxh = dhv * g_ref[...]
        dx = dres_ref[...] + r * (dxh - xh * jnp.mean(dxh * xh, axis=-1, keepdims=True))
        dx_ref[...] = dx
        dxb_ref[...] = dx.astype(BF16)

        @pl.when(pl.program_id(0) == 0)
        def _():
            dg_ref[...] = jnp.zeros_like(dg_ref)

        dg_ref[...] += jnp.sum(dhv * xh, axis=0, keepdims=True)

    row = pl.BlockSpec((tr, d), lambda i: (i, 0))
    vec = pl.BlockSpec((1, d), lambda i: (0, 0))
    return _pcall(body, name=name, grid=(s // tr,), in_specs=[row, vec, row, row], out_specs=(row, row, vec),
                  out_shape=(jax.ShapeDtypeStruct((s, d), F32), jax.ShapeDtypeStruct((s, d), BF16),
                             jax.ShapeDtypeStruct((1, d), F32)),
                  semantics=("arbitrary",), block_bytes=5 * tr * d * 4)(x, gain.reshape(1, d), dh, dres)


def _loss_head(x, gain, target, name):
    s, d = x.shape
    tr = _row_tile(s, d, 1 << 19)

    def body(x_ref, g_ref, t_ref, loss_ref, dx_ref, dxb_ref, dg_ref):
        xv = x_ref[...]
        r = lax.rsqrt(jnp.mean(xv * xv, axis=-1, keepdims=True) + RMS_EPS)
        xh = xv * r
        gv = g_ref[...]
        err = xh * gv - t_ref[...]
        dy = err * (1.0 / d)
        dxh = dy * gv
        dx = r * (dxh - xh * jnp.mean(dxh * xh, axis=-1, keepdims=True))
        dx_ref[...] = dx
        dxb_ref[...] = dx.astype(BF16)

        @pl.when(pl.program_id(0) == 0)
        def _():
            dg_ref[...] = jnp.zeros_like(dg_ref)
            loss_ref[...] = jnp.zeros_like(loss_ref)

        dg_ref[...] += jnp.sum(dy * xh, axis=0, keepdims=True)
        part = jnp.sum(jnp.sum(err * err, axis=-1, keepdims=True), axis=0, keepdims=True) * (0.5 / d)
        loss_ref[...] += jnp.broadcast_to(part, loss_ref.shape)

    row = pl.BlockSpec((tr, d), lambda i: (i, 0))
    vec = pl.BlockSpec((1, d), lambda i: (0, 0))
    lspec = pl.BlockSpec((SUBLANES, LANES), lambda i: (0, 0))
    return _pcall(body, name=name, grid=(s // tr,), in_specs=[row, vec, row], out_specs=(lspec, row, row, vec),
                  out_shape=(jax.ShapeDtypeStruct((SUBLANES, LANES), F32), jax.ShapeDtypeStruct((s, d), F32),
                             jax.ShapeDtypeStruct((s, d), BF16), jax.ShapeDtypeStruct((1, d), F32)),
                  semantics=("arbitrary",), block_bytes=4 * tr * d * 4)(x, gain.reshape(1, d), target)


def _shift_down(xc, xp, s):
    tr = xc.shape[0]
    r = pltpu.roll(xc, s, 0)
    p = pltpu.roll(xp, s, 0)
    row8 = lax.broadcasted_iota(jnp.int32, (SUBLANES, xc.shape[1]), 0)
    head = jnp.where(row8 < s, p, r[:SUBLANES])
    if tr == SUBLANES:
        return head
    return jnp.concatenate([head, r[SUBLANES:]], axis=0)


def _shift_up(yc, yn, s):
    tr = yc.shape[0]
    u = pltpu.roll(yc, tr - s, 0)
    n = pltpu.roll(yn, SUBLANES - s, 0)
    row8 = lax.broadcasted_iota(jnp.int32, (SUBLANES, yc.shape[1]), 0)
    tail = jnp.where(row8 >= SUBLANES - s, n, u[tr - SUBLANES:])
    if tr == SUBLANES:
        return tail
    return jnp.concatenate([u[:tr - SUBLANES], tail], axis=0)


def _conv_apply(xc, xp, w):
    y = xc * w[CONV_WIDTH - 1:CONV_WIDTH, :]
    for s in range(1, CONV_WIDTH):
        y = y + _shift_down(xc, xp, s) * w[CONV_WIDTH - 1 - s:CONV_WIDTH - s, :]
    return y


def _halo_specs(tr, col_of):
    per = tr // SUBLANES
    cur = pl.BlockSpec((tr, LANES), lambda j, i: (i, col_of(j)))
    prev = pl.BlockSpec((SUBLANES, LANES), lambda j, i: (jnp.maximum(i * per - 1, 0), col_of(j)))
    return cur, prev


def _conv_bias_fwd(x_arr, x_col0, w, bias, name):
    s = x_arr.shape[0]
    ncb = w.shape[1] // LANES
    tr = _tile(s, 512, SUBLANES)

    def body(cur_ref, prev_ref, w_ref, b_ref, o_ref):
        i = pl.program_id(1)
        xp = prev_ref[...] * (i > 0).astype(F32)
        o_ref[...] = _conv_apply(cur_ref[...], xp, w_ref[...]) + b_ref[...]

    cur, prev = _halo_specs(tr, lambda j: x_col0 + j)
    return _pcall(body, name=name, grid=(ncb, s // tr),
                  in_specs=[cur, prev, pl.BlockSpec((CONV_WIDTH, LANES), lambda j, i: (0, j)),
                            pl.BlockSpec((1, LANES), lambda j, i: (0, j))],
                  out_specs=pl.BlockSpec((tr, LANES), lambda j, i: (i, j)),
                  out_shape=jax.ShapeDtypeStruct((s, w.shape[1]), F32), semantics=("parallel", "parallel"),
                  block_bytes=3 * tr * LANES * 4)(x_arr, x_arr, w, bias.reshape(1, -1))


def _conv_bwd(dy, x_arr, x_col0, w, name):
    s, c = dy.shape
    ncb = c // LANES
    tr = _tile(s, 512, SUBLANES)
    per = tr // SUBLANES
    ni = s // tr

    def body(dy_ref, dyn_ref, cur_ref, prev_ref, w_ref, dx_ref, dw_ref, db_ref):
        i = pl.program_id(1)
        dyv = dy_ref[...]
        dn = dyn_ref[...] * (i < ni - 1).astype(F32)
        xc = cur_ref[...]
        xp = prev_ref[...] * (i > 0).astype(F32)
        wv = w_ref[...]

        @pl.when(i == 0)
        def _():
            dw_ref[...] = jnp.zeros_like(dw_ref)
            db_ref[...] = jnp.zeros_like(db_ref)

        dx = dyv * wv[CONV_WIDTH - 1:CONV_WIDTH, :]
        dw_ref[CONV_WIDTH - 1:CONV_WIDTH, :] += jnp.sum(dyv * xc, axis=0, keepdims=True)
        for sh in range(1, CONV_WIDTH):
            j = CONV_WIDTH - 1 - sh
            dx = dx + _shift_up(dyv, dn, sh) * wv[j:j + 1, :]
            dw_ref[j:j + 1, :] += jnp.sum(dyv * _shift_down(xc, xp, sh), axis=0, keepdims=True)
        dx_ref[...] = dx
        db_ref[...] += jnp.sum(dyv, axis=0, keepdims=True)

    cur, prev = _halo_specs(tr, lambda j: x_col0 + j)
    dcur = pl.BlockSpec((tr, LANES), lambda j, i: (i, j))
    dnext = pl.BlockSpec((SUBLANES, LANES), lambda j, i: (jnp.minimum((i + 1) * per, s // SUBLANES - 1), j))
    return _pcall(body, name=name, grid=(ncb, ni),
                  in_specs=[dcur, dnext, cur, prev, pl.BlockSpec((CONV_WIDTH, LANES), lambda j, i: (0, j))],
                  out_specs=(dcur, pl.BlockSpec((CONV_WIDTH, LANES), lambda j, i: (0, j)),
                             pl.BlockSpec((1, LANES), lambda j, i: (0, j))),
                  out_shape=(jax.ShapeDtypeStruct((s, c), F32), jax.ShapeDtypeStruct((CONV_WIDTH, c), F32),
                             jax.ShapeDtypeStruct((1, c), F32)),
                  semantics=("parallel", "arbitrary"), block_bytes=4 * tr * LANES * 4)(dy, dy, x_arr, x_arr, w)


def _gdn_pre_fwd(proj, conv_w, heads, name):
    s = proj.shape[0]
    ncb = conv_w.shape[1] // LANES
    tr = _tile(s, 512, SUBLANES)
    qscale = float(LANES) ** -0.5

    def body(cur_ref, prev_ref, w_ref, o_ref):
        j, i = pl.program_id(0), pl.program_id(1)
        xp = prev_ref[...] * (i > 0).astype(F32)
        cv = _conv_apply(cur_ref[...], xp, w_ref[...])
        sv = cv * _sigmoid(cv)
        nrm = lax.rsqrt(jnp.sum(sv * sv, axis=-1, keepdims=True) + L2_EPS)
        scale = jnp.where(j < heads, qscale, 1.0)
        o_ref[...] = jnp.where(j < 2 * heads, sv * nrm * scale, sv)

    cur, prev = _halo_specs(tr, lambda j: j)
    return _pcall(body, name=name, grid=(ncb, s // tr),
                  in_specs=[cur, prev, pl.BlockSpec((CONV_WIDTH, LANES), lambda j, i: (0, j))],
                  out_specs=pl.BlockSpec((tr, LANES), lambda j, i: (i, j)),
                  out_shape=jax.ShapeDtypeStruct((s, conv_w.shape[1]), F32), semantics=("parallel", "parallel"),
                  block_bytes=3 * tr * LANES * 4)(proj, proj, conv_w)


def _gdn_pre_bwd(proj, conv_w, dqkv, heads, name):
    s = proj.shape[0]
    ncb = conv_w.shape[1] // LANES
    tr = _tile(s, 512, SUBLANES)
    qscale = float(LANES) ** -0.5

    def body(cur_ref, prev_ref, w_ref, d_ref, o_ref):
        j, i = pl.program_id(0), pl.program_id(1)
        xp = prev_ref[...] * (i > 0).astype(F32)
        cv = _conv_apply(cur_ref[...], xp, w_ref[...])
        sg = _sigmoid(cv)
        sv = cv * sg
        nrm = lax.rsqrt(jnp.sum(sv * sv, axis=-1, keepdims=True) + L2_EPS)
        dv = d_ref[...]
        scale = jnp.where(j < heads, qscale, 1.0)
        dsn = scale * nrm * (dv - sv * (nrm * nrm) * jnp.sum(dv * sv, axis=-1, keepdims=True))
        ds = jnp.where(j < 2 * heads, dsn, dv)
        o_ref[...] = ds * (sg * (1.0 + cv * (1.0 - sg)))

    cur, prev = _halo_specs(tr, lambda j: j)
    blk = pl.BlockSpec((tr, LANES), lambda j, i: (i, j))
    return _pcall(body, name=name, grid=(ncb, s // tr),
                  in_specs=[cur, prev, pl.BlockSpec((CONV_WIDTH, LANES), lambda j, i: (0, j)), blk],
                  out_specs=blk, out_shape=jax.ShapeDtypeStruct((s, conv_w.shape[1]), F32),
                  semantics=("parallel", "parallel"), block_bytes=4 * tr * LANES * 4)(proj, proj, conv_w, dqkv)


def _tri_inverse(a_strict, block):
    n = a_strict.shape[0]
    ri = lax.broadcasted_iota(jnp.int32, (n, n), 0)
    ci = lax.broadcasted_iota(jnp.int32, (n, n), 1)
    p = jnp.where(ri == ci, 1.0, 0.0) - a_strict
    if block <= 2:
        return p
    xp = _dot(a_strict, a_strict)
    span = 2
    while True:
        p_next = p + _dot(p, xp)
        span *= 2
        if span >= block:
            return p_next
        xp = _dot(xp, xp)
        p = p_next


GDN_HEAD_GROUP = 4
_CHUNK_SHIFT = GDN_CHUNK.bit_length() - 1
_LANE_SHIFT = LANES.bit_length() - 1


def _stack_heads(ref, hb):
    return jnp.concatenate([ref[:, i * LANES:(i + 1) * LANES] for i in range(hb)], axis=0)


def _diag_blocks(x, hb):
    c = GDN_CHUNK
    return jnp.concatenate([x[i * c:(i + 1) * c, i * LANES:(i + 1) * LANES] for i in range(hb)], axis=0)


def _expand_blocks(y, hb):
    row_blk = lax.shift_right_logical(lax.broadcasted_iota(jnp.int32, y.shape, 0), _CHUNK_SHIFT)
    return jnp.concatenate([jnp.where(row_blk == j, y, 0.0) for j in range(hb)], axis=1)


def _gdn_group_terms(q, k, v, ab, alog, dtb, head0, hb, heads):
    c = GDN_CHUNK
    r = hb * c
    lane = lax.broadcasted_iota(jnp.int32, (1, LANES), 1)

    def column(lane0):
        return jnp.concatenate([jnp.sum(jnp.where(lane == lane0 + head0 + i, ab, 0.0), axis=1, keepdims=True)
                                for i in range(hb)], axis=0)

    def per_head(vec):
        return jnp.concatenate([jnp.broadcast_to(jnp.sum(jnp.where(lane == head0 + i, vec, 0.0), axis=1,
                                                         keepdims=True), (c, 1)) for i in range(hb)], axis=0)

    pre = column(0) + per_head(dtb)
    neg_ea = -jnp.exp(per_head(alog))
    g = neg_ea * _softplus(pre)
    beta = _sigmoid(column(heads))
    ri = lax.broadcasted_iota(jnp.int32, (r, r), 0)
    ci = lax.broadcasted_iota(jnp.int32, (r, r), 1)
    same = lax.shift_right_logical(ri, _CHUNK_SHIFT) == lax.shift_right_logical(ci, _CHUNK_SHIFT)
    eye = ri == ci
    causal = same & (ri >= ci)
    strict = same & (ri > ci)
    g_row = jnp.sum(jnp.where(eye, g, 0.0), axis=0, keepdims=True)
    gc_col = jnp.sum(jnp.where(causal, g_row, 0.0), axis=1, keepdims=True)
    gc_row = jnp.sum(jnp.where(same & (ri <= ci), g, 0.0), axis=0, keepdims=True)
    gl_col = jnp.sum(jnp.where(same, g_row, 0.0), axis=1, keepdims=True)
    decay = jnp.where(causal, jnp.exp(jnp.where(causal, gc_col - gc_row, 0.0)), 0.0)
    e_last_col = jnp.exp(gl_col)
    e_last_lanes = jnp.concatenate([jnp.broadcast_to(e_last_col[i * c:i * c + 1, :], (1, LANES))
                                    for i in range(hb)], axis=1)
    egc = jnp.exp(gc_col)
    ekl = jnp.exp(gl_col - gc_col)
    kb = k * beta
    vb = v * beta
    kk = _dot_nt(kb, k)
    a_strict = jnp.where(strict, kk * decay, 0.0)
    return dict(pre=pre, neg_ea=neg_ea, g=g, beta=beta, ri=ri, ci=ci, same=same, eye=eye, causal=causal,
                strict=strict, decay=decay, e_last_col=e_last_col, e_last_lanes=e_last_lanes, egc=egc, ekl=ekl,
                kb=kb, vb=vb, kk=kk, a_strict=a_strict, lane=lane)


def _gdn_head_group(heads):
    hb = GDN_HEAD_GROUP
    while heads % hb:
        hb //= 2
    return hb


def _gdn_fwd(qkv, proj, alog, dtb, heads, name):
    s = qkv.shape[0]
    c = GDN_CHUNK
    nc = s // c
    ab_blk = proj.shape[1] // LANES - 1
    hb = _gdn_head_group(heads)
    ng = heads // hb
    r = hb * c

    def body(q_ref, k_ref, v_ref, ab_ref, alog_ref, dtb_ref, o_ref, t_ref, s0_ref, state_ref):
        grp, ch = pl.program_id(0), pl.program_id(1)

        @pl.when(ch == 0)
        def _():
            state_ref[...] = jnp.zeros_like(state_ref)

        q, k, v = _stack_heads(q_ref, hb), _stack_heads(k_ref, hb), _stack_heads(v_ref, hb)
        tm = _gdn_group_terms(q, k, v, ab_ref[...], alog_ref[...], dtb_ref[...], grp * hb, hb, heads)
        t_inv = _tri_inverse(tm['a_strict'], c)
        u = _dot(t_inv, tm['vb'])
        w = _dot(t_inv, tm['kb'] * tm['egc'])
        qk = jnp.where(tm['causal'], _dot_nt(q, k) * tm['decay'], 0.0)
        st = state_ref[...]
        v_new = u - _diag_blocks(_dot(w, st), hb)
        out = _diag_blocks(_dot(q * tm['egc'], st), hb) + _dot(qk, v_new)
        for i in range(hb):
            o_ref[:, i * LANES:(i + 1) * LANES] = out[i * c:(i + 1) * c, :]
        t_ref[...] = t_inv
        s0_ref[...] = st
        state_ref[...] = st * tm['e_last_lanes'] + _dot_tn(k * tm['ekl'], _expand_blocks(v_new, hb))

    def blk(off):
        return pl.BlockSpec((c, hb * LANES), lambda g, n: (n, off * ng + g))

    vec = pl.BlockSpec((1, LANES), lambda g, n: (0, 0))
    return _pcall(
        body, name=name, grid=(ng, nc),
        in_specs=[blk(0), blk(1), blk(2), pl.BlockSpec((c, LANES), lambda g, n: (n, ab_blk)), vec, vec],
        out_specs=(blk(0), pl.BlockSpec((None, None, r, r), lambda g, n: (g, n, 0, 0)),
                   pl.BlockSpec((None, None, LANES, hb * LANES), lambda g, n: (g, n, 0, 0))),
        out_shape=(jax.ShapeDtypeStruct((s, heads * LANES), F32), jax.ShapeDtypeStruct((ng, nc, r, r), F32),
                   jax.ShapeDtypeStruct((ng, nc, LANES, hb * LANES), F32)),
        scratch_shapes=[pltpu.VMEM((LANES, hb * LANES), F32)], semantics=("parallel", "arbitrary"),
        block_bytes=8 * r * LANES * 4 + 2 * r * r * 4 + 2 * LANES * hb * LANES * 4)(qkv, qkv, qkv, proj, alog, dtb)


def _gdn_bwd(qkv, proj, alog, dtb, t_all, s0_all, d_o, heads, name):
    s = qkv.shape[0]
    c = GDN_CHUNK
    nc = s // c
    ab_blk = proj.shape[1] // LANES - 1

    hb = _gdn_head_group(heads)
    ng = heads // hb
    r = hb * c

    def body(q_ref, k_ref, v_ref, ab_ref, alog_ref, dtb_ref, t_ref, s0_ref, do_ref,
             dq_ref, dk_ref, dv_ref, dgb_ref, ds_ref):
        grp, step = pl.program_id(0), pl.program_id(1)

        @pl.when(step == 0)
        def _():
            ds_ref[...] = jnp.zeros_like(ds_ref)

        q, k, v = _stack_heads(q_ref, hb), _stack_heads(k_ref, hb), _stack_heads(v_ref, hb)
        do = _stack_heads(do_ref, hb)
        tm = _gdn_group_terms(q, k, v, ab_ref[...], alog_ref[...], dtb_ref[...], grp * hb, hb, heads)
        ri, ci, same, eye = tm['ri'], tm['ci'], tm['same'], tm['eye']
        causal, strict, decay = tm['causal'], tm['strict'], tm['decay']
        egc, ekl, kb, vb, beta = tm['egc'], tm['ekl'], tm['kb'], tm['vb'], tm['beta']
        t_inv = t_ref[...]
        st = s0_ref[...]
        ds_next = ds_ref[...]
        kbg = kb * egc
        u = _dot(t_inv, vb)
        w = _dot(t_inv, kbg)
        qkm = _dot_nt(q, k)
        qk = jnp.where(causal, qkm * decay, 0.0)
        v_new = u - _diag_blocks(_dot(w, st), hb)
        qd = q * egc
        kd = k * ekl
        do_x = _expand_blocks(do, hb)

        dqd = _dot_nt(do_x, st)
        dqk = jnp.where(causal, _dot_nt(do, v_new), 0.0)
        dvn = _dot_tn(qk, do) + _diag_blocks(_dot(kd, ds_next), hb)
        dkd = _dot_nt(_expand_blocks(v_new, hb), ds_next)
        sd = jnp.sum(st * ds_next, axis=0, keepdims=True)
        dgl = jnp.concatenate([jnp.broadcast_to(jnp.sum(sd[:, i * LANES:(i + 1) * LANES], axis=1, keepdims=True),
                                                (c, 1)) for i in range(hb)], axis=0) * tm['e_last_col']
        dvn_x = _expand_blocks(dvn, hb)
        dw = -_dot_nt(dvn_x, st)
        ds_ref[...] = _dot_tn(qd, do_x) + tm['e_last_lanes'] * ds_next - _dot_tn(w, dvn_x)
        dt = _dot_nt(dvn, vb) + _dot_nt(dw, kbg)
        dvb = _dot_tn(t_inv, dvn)
        dkbg = _dot_tn(t_inv, dw)
        da_m = jnp.where(strict, -_dot_tn(t_inv, _dot_nt(dt, t_inv)), 0.0)
        dad = da_m * decay
        dkb = _dot(dad, k) + dkbg * egc
        dqkd = dqk * decay
        dq = _dot(dqkd, k) + dqd * egc
        dk = _dot_tn(dad, kb) + _dot_tn(dqkd, q) + dkd * ekl + dkb * beta
        e_mat = (da_m * tm['kk'] + dqk * qkm) * decay
        s_kd = jnp.sum(dkd * kd, axis=1, keepdims=True)
        s_kd_row = jnp.sum(jnp.where(eye, s_kd, 0.0), axis=0, keepdims=True)
        dgl = dgl + jnp.sum(jnp.where(same, s_kd_row, 0.0), axis=1, keepdims=True)
        col_sum = jnp.sum(e_mat, axis=0, keepdims=True)
        col_sum_c = jnp.sum(jnp.where(eye, col_sum, 0.0), axis=1, keepdims=True)
        dgc = (jnp.sum(e_mat, axis=1, keepdims=True) - col_sum_c + jnp.sum(dqd * qd, axis=1, keepdims=True)
               - s_kd + jnp.sum(dkbg * kbg, axis=1, keepdims=True))
        row_c = lax.broadcasted_iota(jnp.int32, (r, 1), 0)
        dgc = dgc + jnp.where((row_c & (c - 1)) == c - 1, dgl, 0.0)
        dgc_row = jnp.sum(jnp.where(eye, dgc, 0.0), axis=0, keepdims=True)
        dg = jnp.sum(jnp.where(same & (ci >= ri), dgc_row, 0.0), axis=1, keepdims=True)
        dbeta = jnp.sum(dkb * k, axis=1, keepdims=True) + jnp.sum(dvb * v, axis=1, keepdims=True)
        da_pre = dg * tm['neg_ea'] * _sigmoid(tm['pre'])
        db_pre = dbeta * beta * (1.0 - beta)
        lane = tm['lane']
        head_row = grp * hb + lax.shift_right_logical(row_c, _CHUNK_SHIFT)
        dgb = (jnp.where(lane == head_row, da_pre, 0.0) + jnp.where(lane == heads + head_row, db_pre, 0.0)
               + jnp.where(lane == 2 * heads + head_row, dg * tm['g'], 0.0))
        dvv = dvb * beta
        for i in range(hb):
            cols, rows = slice(i * LANES, (i + 1) * LANES), slice(i * c, (i + 1) * c)
            dq_ref[:, cols] = dq[rows, :]
            dk_ref[:, cols] = dk[rows, :]
            dv_ref[:, cols] = dvv[rows, :]
            dgb_ref[:, cols] = dgb[rows, :]

    def blk(off):
        return pl.BlockSpec((c, hb * LANES), lambda g, n: (nc - 1 - n, off * ng + g))

    vec = pl.BlockSpec((1, LANES), lambda g, n: (0, 0))
    gw = heads * LANES
    dq, dk, dv, dgb = _pcall(
        body, name=name, grid=(ng, nc),
        in_specs=[blk(0), blk(1), blk(2), pl.BlockSpec((c, LANES), lambda g, n: (nc - 1 - n, ab_blk)),
                  vec, vec, pl.BlockSpec((None, None, r, r), lambda g, n: (g, nc - 1 - n, 0, 0)),
                  pl.BlockSpec((None, None, LANES, hb * LANES), lambda g, n: (g, nc - 1 - n, 0, 0)), blk(0)],
        out_specs=(blk(0), blk(0), blk(0), blk(0)),
        out_shape=tuple(jax.ShapeDtypeStruct((s, gw), F32) for _ in range(4)),
        scratch_shapes=[pltpu.VMEM((LANES, hb * LANES), F32)], semantics=("parallel", "arbitrary"),
        block_bytes=12 * r * LANES * 4 + 2 * r * r * 4 + 2 * LANES * hb * LANES * 4)(
            qkv, qkv, qkv, proj, alog, dtb, t_all, s0_all, d_o)
    return dq, dk, dv, dgb


def _gdn_post_fwd(o, proj, z_col0, gain, name):
    s, gw = o.shape
    heads = gw // LANES
    tr = _tile(s, 512, SUBLANES)

    def body(o_ref, z_ref, g_ref, y_ref):
        ov, zv = o_ref[...], z_ref[...]
        r = lax.rsqrt(jnp.mean(ov * ov, axis=-1, keepdims=True) + RMS_EPS)
        y_ref[...] = (ov * r * g_ref[...] * (zv * _sigmoid(zv))).astype(BF16)

    blk = pl.BlockSpec((tr, LANES), lambda i, h: (i, h))
    return _pcall(body, name=name, grid=(s // tr, heads),
                  in_specs=[blk, pl.BlockSpec((tr, LANES), lambda i, h: (i, z_col0 + h)),
                            pl.BlockSpec((1, LANES), lambda i, h: (0, 0))],
                  out_specs=blk, out_shape=jax.ShapeDtypeStruct((s, gw), BF16), semantics=("parallel", "parallel"),
                  block_bytes=3 * tr * LANES * 4)(o, proj, gain.reshape(1, LANES))


def _gdn_post_bwd(o, proj, z_col0, gain, dy, name):
    s, gw = o.shape
    heads = gw // LANES
    tr = _tile(s, 512, SUBLANES)

    def body(o_ref, z_ref, g_ref, dy_ref, do_ref, dz_ref, dg_ref):
        ov, zv, gv, dyv = o_ref[...], z_ref[...], g_ref[...], dy_ref[...]
        r = lax.rsqrt(jnp.mean(ov * ov, axis=-1, keepdims=True) + RMS_EPS)
        nv = ov * r
        sg = _sigmoid(zv)
        sz = zv * sg
        dn = dyv * gv * sz
        do_ref[...] = r * (dn - nv * jnp.mean(dn * nv, axis=-1, keepdims=True))
        dz_ref[...] = dyv * nv * gv * (sg * (1.0 + zv * (1.0 - sg)))

        @pl.when((pl.program_id(0) == 0) & (pl.program_id(1) == 0))
        def _():
            dg_ref[...] = jnp.zeros_like(dg_ref)

        dg_ref[...] += jnp.sum(dyv * nv * sz, axis=0, keepdims=True)

    blk = pl.BlockSpec((tr, LANES), lambda i, h: (i, h))
    vec = pl.BlockSpec((1, LANES), lambda i, h: (0, 0))
    return _pcall(body, name=name, grid=(s // tr, heads),
                  in_specs=[blk, pl.BlockSpec((tr, LANES), lambda i, h: (i, z_col0 + h)), vec, blk],
                  out_specs=(blk, blk, vec),
                  out_shape=(jax.ShapeDtypeStruct((s, gw), F32), jax.ShapeDtypeStruct((s, gw), F32),
                             jax.ShapeDtypeStruct((1, LANES), F32)),
                  semantics=("arbitrary", "arbitrary"), block_bytes=6 * tr * LANES * 4)(
                      o, proj, gain.reshape(1, LANES), dy)


def _dab_reduce(dgb, name):
    s, gw = dgb.shape
    heads = gw // LANES
    tr = _tile(s, 512, SUBLANES)

    def body(d_ref, o_ref, cs_ref):
        acc = d_ref[:, 0:LANES]
        for h in range(1, heads):
            acc = acc + d_ref[:, h * LANES:(h + 1) * LANES]
        o_ref[...] = acc

        @pl.when(pl.program_id(0) == 0)
        def _():
            cs_ref[...] = jnp.zeros_like(cs_ref)

        cs_ref[...] += jnp.sum(acc, axis=0, keepdims=True)

    return _pcall(body, name=name, grid=(s // tr,), in_specs=[pl.BlockSpec((tr, gw), lambda i: (i, 0))],
                  out_specs=(pl.BlockSpec((tr, LANES), lambda i: (i, 0)), pl.BlockSpec((1, LANES), lambda i: (0, 0))),
                  out_shape=(jax.ShapeDtypeStruct((s, LANES), F32), jax.ShapeDtypeStruct((1, LANES), F32)),
                  semantics=("arbitrary",), block_bytes=tr * gw * 4)(dgb)


def _lru_gates(xc, wa, wx, ba, bx, lam):
    r = _sigmoid(_dot(xc, wa) + ba)
    ig = _sigmoid(_dot(xc, wx) + bx)
    sp = _softplus(-lam)
    log_a = -LRU_C * r * sp
    a = jnp.exp(log_a)
    e2 = jnp.exp(2.0 * log_a)
    mult = jnp.sqrt(jnp.maximum(1.0 - e2, 0.0))
    return r, ig, sp, a, e2, mult


def _lru_fwd(xc, proj, y_col0, wa, wx, ba, bx, lam, name):
    s, lw = xc.shape
    nb = lw // LANES
    tr = _tile(s, 256, SUBLANES)

    def body(xc_ref, y_ref, wa_ref, wx_ref, ba_ref, bx_ref, lam_ref, h_ref, o_ref, carry_ref):
        @pl.when(pl.program_id(1) == 0)
        def _():
            carry_ref[...] = jnp.zeros_like(carry_ref)

        xv = xc_ref[...]
        _, ig, _, a, _, mult = _lru_gates(xv, wa_ref[...], wx_ref[...], ba_ref[...], bx_ref[...], lam_ref[...])
        b = mult * (ig * xv)
        row = lax.broadcasted_iota(jnp.int32, (tr, LANES), 0)
        sh = 1
        while sh < tr:
            keep = row >= sh
            b = a * jnp.where(keep, pltpu.roll(b, sh, 0), 0.0) + b
            a = a * jnp.where(keep, pltpu.roll(a, sh, 0), 1.0)
            sh *= 2
        hv = a * carry_ref[0:1, :] + b
        h_ref[...] = hv
        carry_ref[...] = jnp.broadcast_to(hv[tr - 1:tr, :], carry_ref.shape)
        gy, _ = _gelu_and_grad(y_ref[...])
        o_ref[...] = (hv * gy).astype(BF16)

    blk = pl.BlockSpec((tr, LANES), lambda n, i: (i, n))
    wspec = pl.BlockSpec((None, LANES, LANES), lambda n, i: (n, 0, 0))
    vec = pl.BlockSpec((1, LANES), lambda n, i: (0, n))
    return _pcall(body, name=name, grid=(nb, s // tr),
                  in_specs=[blk, pl.BlockSpec((tr, LANES), lambda n, i: (i, y_col0 + n)), wspec, wspec, vec, vec, vec],
                  out_specs=(blk, blk),
                  out_shape=(jax.ShapeDtypeStruct((s, lw), F32), jax.ShapeDtypeStruct((s, lw), BF16)),
                  scratch_shapes=[pltpu.VMEM((SUBLANES, LANES), F32)], semantics=("parallel", "arbitrary"),
                  block_bytes=8 * tr * LANES * 4)(xc, proj, wa, wx, ba.reshape(1, lw), bx.reshape(1, lw),
                                                  lam.reshape(1, lw))


def _lru_bwd(d_out, xc, hseq, proj, y_col0, wa, wx, ba, bx, lam, name):
    s, lw = xc.shape
    nb = lw // LANES
    tr = _tile(s, 256, SUBLANES)
    per = tr // SUBLANES
    ni = s // tr
    nrow8 = s // SUBLANES

    def body(do_ref, xc_ref, xn_ref, h_ref, hp_ref, y_ref, wa_ref, wx_ref, ba_ref, bx_ref, lam_ref,
             dxc_ref, dy_ref, dwa_ref, dwx_ref, dba_ref, dbx_ref, dlam_ref, carry_ref):
        step = pl.program_id(1)
        tile = ni - 1 - step

        @pl.when(step == 0)
        def _():
            carry_ref[...] = jnp.zeros_like(carry_ref)
            dwa_ref[...] = jnp.zeros_like(dwa_ref)
            dwx_ref[...] = jnp.zeros_like(dwx_ref)
            dba_ref[...] = jnp.zeros_like(dba_ref)
            dbx_ref[...] = jnp.zeros_like(dbx_ref)
            dlam_ref[...] = jnp.zeros_like(dlam_ref)

        wav, wxv, bav, bxv, lamv = wa_ref[...], wx_ref[...], ba_ref[...], bx_ref[...], lam_ref[...]
        xv = xc_ref[...]
        r, ig, sp, a, e2, mult = _lru_gates(xv, wav, wxv, bav, bxv, lamv)
        a_next = _lru_gates(xn_ref[...], wav, wxv, bav, bxv, lamv)[3] * (tile < ni - 1).astype(F32)
        hv = h_ref[...]
        h_prev = _shift_down(hv, hp_ref[...] * (tile > 0).astype(F32), 1)
        yv = y_ref[...]
        gy, dgy = _gelu_and_grad(yv)
        dov = do_ref[...]
        dy_ref[...] = dov * hv * dgy
        coef = _shift_up(a, a_next, 1)
        bb = dov * gy
        row = lax.broadcasted_iota(jnp.int32, (tr, LANES), 0)
        sh = 1
        while sh < tr:
            keep = row < tr - sh
            bb = coef * jnp.where(keep, pltpu.roll(bb, tr - sh, 0), 0.0) + bb
            coef = coef * jnp.where(keep, pltpu.roll(coef, tr - sh, 0), 1.0)
            sh *= 2
        lam_t = coef * carry_ref[0:1, :] + bb
        carry_ref[...] = jnp.broadcast_to(lam_t[0:1, :], carry_ref.shape)
        d_a = lam_t * h_prev
        d_mult = lam_t * (ig * xv)
        d_ix = lam_t * mult
        d_la = d_a * a - d_mult * e2 / jnp.maximum(mult, 1e-30)
        d_r = d_la * (-LRU_C * sp)
        dlam_ref[...] += jnp.sum(d_la * (LRU_C * r) * _sigmoid(-lamv), axis=0, keepdims=True)
        d_pa = d_r * r * (1.0 - r)
        d_px = (d_ix * xv) * ig * (1.0 - ig)
        dxc_ref[...] = d_ix * ig + _dot_nt(d_pa, wav) + _dot_nt(d_px, wxv)
        dwa_ref[...] += _dot_tn(xv, d_pa)
        dwx_ref[...] += _dot_tn(xv, d_px)
        dba_ref[...] += jnp.sum(d_pa, axis=0, keepdims=True)
        dbx_ref[...] += jnp.sum(d_px, axis=0, keepdims=True)

    blk = pl.BlockSpec((tr, LANES), lambda n, i: (ni - 1 - i, n))
    nxt = pl.BlockSpec((SUBLANES, LANES), lambda n, i: (jnp.minimum((ni - i) * per, nrow8 - 1), n))
    prv = pl.BlockSpec((SUBLANES, LANES), lambda n, i: (jnp.maximum((ni - 1 - i) * per - 1, 0), n))
    wspec = pl.BlockSpec((None, LANES, LANES), lambda n, i: (n, 0, 0))
    vec = pl.BlockSpec((1, LANES), lambda n, i: (0, n))
    return _pcall(
        body, name=name, grid=(nb, ni),
        in_specs=[blk, blk, nxt, blk, prv, pl.BlockSpec((tr, LANES), lambda n, i: (ni - 1 - i, y_col0 + n)),
                  wspec, wspec, vec, vec, vec],
        out_specs=(blk, blk, wspec, wspec, vec, vec, vec),
        out_shape=(jax.ShapeDtypeStruct((s, lw), F32), jax.ShapeDtypeStruct((s, lw), F32),
                   jax.ShapeDtypeStruct((nb, LANES, LANES), F32), jax.ShapeDtypeStruct((nb, LANES, LANES), F32),
                   jax.ShapeDtypeStruct((1, lw), F32), jax.ShapeDtypeStruct((1, lw), F32),
                   jax.ShapeDtypeStruct((1, lw), F32)),
        scratch_shapes=[pltpu.VMEM((SUBLANES, LANES), F32)], semantics=("parallel", "arbitrary"),
        block_bytes=12 * tr * LANES * 4)(d_out, xc, xc, hseq, hseq, proj, wa, wx, ba.reshape(1, lw),
                                         bx.reshape(1, lw), lam.reshape(1, lw))


def _merge_fwd(proj, gg_col0, gl_col0, bg, bl, name):
    s, d = bg.shape
    tr, tc = _tile(s, 256, SUBLANES), _tile(d, 1024)
    cb = tc // LANES

    def body(gg_ref, gl_ref, bg_ref, bl_ref, o_ref):
        o_ref[...] = (_sigmoid(gg_ref[...]) * bg_ref[...] + _sigmoid(gl_ref[...]) * bl_ref[...]).astype(BF16)

    blk = pl.BlockSpec((tr, tc), lambda i, j: (i, j))
    return _pcall(body, name=name, grid=(s // tr, d // tc),
                  in_specs=[pl.BlockSpec((tr, tc), lambda i, j: (i, gg_col0 // cb + j)),
                            pl.BlockSpec((tr, tc), lambda i, j: (i, gl_col0 // cb + j)), blk, blk],
                  out_specs=blk, out_shape=jax.ShapeDtypeStruct((s, d), BF16), semantics=("parallel", "parallel"),
                  block_bytes=5 * tr * tc * 4)(proj, proj, bg, bl)


def _merge_bwd(proj, gg_col0, gl_col0, bg, bl, dm, name):
    s, d = bg.shape
    tr, tc = _tile(s, 256, SUBLANES), _tile(d, 1024)
    cb = tc // LANES

    def body(gg_ref, gl_ref, bg_ref, bl_ref, dm_ref, dgg_ref, dgl_ref, dbg_ref, dbl_ref):
        dmv = dm_ref[...]
        sg, sl = _sigmoid(gg_ref[...]), _sigmoid(gl_ref[...])
        dgg_ref[...] = (dmv * bg_ref[...] * sg * (1.0 - sg)).astype(BF16)
        dgl_ref[...] = (dmv * bl_ref[...] * sl * (1.0 - sl)).astype(BF16)
        dbg_ref[...] = (dmv * sg).astype(BF16)
        dbl_ref[...] = (dmv * sl).astype(BF16)

    blk = pl.BlockSpec((tr, tc), lambda i, j: (i, j))
    sh = jax.ShapeDtypeStruct((s, d), BF16)
    return _pcall(body, name=name, grid=(s // tr, d // tc),
                  in_specs=[pl.BlockSpec((tr, tc), lambda i, j: (i, gg_col0 // cb + j)),
                            pl.BlockSpec((tr, tc), lambda i, j: (i, gl_col0 // cb + j)), blk, blk, blk],
                  out_specs=(blk, blk, blk, blk), out_shape=(sh, sh, sh, sh), semantics=("parallel", "parallel"),
                  block_bytes=8 * tr * tc * 4)(proj, proj, bg, bl, dm)


def _sum_slots(slots, name):
    n, r, c = slots.shape
    tr = _tile(r, max(2 * SUBLANES, (1 << 19) // (c * 4)), 2 * SUBLANES)

    def body(s_ref, o_ref):
        acc = s_ref[0].astype(F32)
        for q in range(1, n):
            acc = acc + s_ref[q].astype(F32)
        o_ref[...] = acc

    return _pcall(body, name=name, grid=(r // tr,), in_specs=[pl.BlockSpec((n, tr, c), lambda i: (0, i, 0))],
                  out_specs=pl.BlockSpec((tr, c), lambda i: (i, 0)), out_shape=jax.ShapeDtypeStruct((r, c), F32),
                  semantics=("parallel",), block_bytes=(n + 1) * tr * c * 4)(slots)


def _adamw(w, g_parts, m, v, name):
    r, c = w.shape
    np_ = len(g_parts)
    tr = _tile(r, max(SUBLANES, (1 << 20) // (c * 4)), SUBLANES)
    c1 = 1.0 - ADAM_B1 ** ADAM_STEP
    c2 = 1.0 - ADAM_B2 ** ADAM_STEP

    def body(*refs):
        w_ref, m_ref, v_ref = refs[0], refs[1 + np_], refs[2 + np_]
        g_ref, d_ref, nm_ref, nv_ref = refs[3 + np_:]
        g = refs[1][...]
        for p in range(1, np_):
            g = g + refs[1 + p][...]
        nm = ADAM_B1 * m_ref[...] + (1.0 - ADAM_B1) * g
        nv = ADAM_B2 * v_ref[...] + (1.0 - ADAM_B2) * (g * g)
        g_ref[...] = g
        nm_ref[...] = nm
        nv_ref[...] = nv
        d_ref[...] = -ADAM_LR * ((nm / c1) / (jnp.sqrt(nv / c2) + ADAM_EPS) + ADAM_WD * w_ref[...])

    blk = pl.BlockSpec((tr, c), lambda i: (i, 0))
    sh = jax.ShapeDtypeStruct((r, c), F32)
    return _pcall(body, name=name, grid=(r // tr,), in_specs=[blk] * (3 + np_), out_specs=(blk,) * 4,
                  out_shape=(sh,) * 4, semantics=("parallel",), block_bytes=(7 + np_) * tr * c * 4)(
                      w, *g_parts, m, v)


def _pair_sum(core, mine, theirs, name):
    _, n, r, c = mine.shape
    tr = _tile(r, max(2 * SUBLANES, (1 << 19) // (c * 4)), 2 * SUBLANES)

    def body(core_ref, a_ref, b_ref, o_ref):
        o_ref[...] = (a_ref[...].astype(F32) + b_ref[...].astype(F32)).astype(BF16)

    return _pcall(body, name=name, grid=(n, r // tr),
                  in_specs=[pl.BlockSpec((None, None, tr, c), lambda q, i, core_ref: (core_ref[0], q, i, 0)),
                            pl.BlockSpec((None, tr, c), lambda q, i, core_ref: (q, i, 0))],
                  out_specs=pl.BlockSpec((None, tr, c), lambda q, i, core_ref: (q, i, 0)),
                  out_shape=jax.ShapeDtypeStruct((n, r, c), BF16), semantics=("parallel", "parallel"),
                  block_bytes=3 * tr * c * 4, scalar_prefetch=1)(core, mine, theirs)


def _sum_landed(chip, landed, own, name):
    n, r, c = landed.shape
    tr = _tile(r, max(2 * SUBLANES, (1 << 19) // (c * 4)), 2 * SUBLANES)

    def body(chip_ref, l_ref, o_ref, t_ref):
        acc = o_ref[...].astype(F32)
        for q in range(n):
            acc = acc + l_ref[q].astype(F32)
        t_ref[...] = acc

    return _pcall(body, name=name, grid=(r // tr,),
                  in_specs=[pl.BlockSpec((n, tr, c), lambda i, chip_ref: (0, i, 0)),
                            pl.BlockSpec((None, tr, c), lambda i, chip_ref: (chip_ref[0], i, 0))],
                  out_specs=pl.BlockSpec((tr, c), lambda i, chip_ref: (i, 0)),
                  out_shape=jax.ShapeDtypeStruct((r, c), F32), semantics=("parallel",),
                  block_bytes=(n + 3) * tr * c * 4, scalar_prefetch=1)(chip, landed, own)


def _adamw_quarters(core, w, g_mine, g_other, m, v, name):
    nl, nh, r, c = w.shape
    tr = _tile(r, max(SUBLANES, (1 << 19) // (c * 4)), SUBLANES)
    c1 = 1.0 - ADAM_B1 ** ADAM_STEP
    c2 = 1.0 - ADAM_B2 ** ADAM_STEP

    def body(core_ref, w_ref, *refs):
        g_refs, (m_ref, v_ref, g_ref, d_ref, nm_ref, nv_ref) = refs[:2 * nl], refs[2 * nl:]
        mine = pl.program_id(1) == core_ref[0]
        g = jnp.where(mine, g_refs[0][...], g_refs[nl][...])
        for l in range(1, nl):
            g = jnp.where(pl.program_id(0) == l, jnp.where(mine, g_refs[l][...], g_refs[nl + l][...]), g)
        nm = ADAM_B1 * m_ref[...] + (1.0 - ADAM_B1) * g
        nv = ADAM_B2 * v_ref[...] + (1.0 - ADAM_B2) * (g * g)
        g_ref[...] = g
        nm_ref[...] = nm
        nv_ref[...] = nv
        d_ref[...] = -ADAM_LR * ((nm / c1) / (jnp.sqrt(nv / c2) + ADAM_EPS) + ADAM_WD * w_ref[...])

    blk = pl.BlockSpec((None, None, tr, c), lambda l, hf, i, core_ref: (l, hf, i, 0))
    gblk = pl.BlockSpec((tr, c), lambda l, hf, i, core_ref: (i, 0))
    sh = jax.ShapeDtypeStruct(w.shape, F32)
    return _pcall(body, name=name, grid=(nl, nh, r // tr), in_specs=[blk] + [gblk] * (2 * nl) + [blk, blk],
                  out_specs=(blk,) * 4, out_shape=(sh,) * 4, semantics=("parallel", "parallel", "parallel"),
                  block_bytes=(7 + 2 * nl) * tr * c * 4, scalar_prefetch=1)(core, w, *g_mine, *g_other, m, v)


HBM_SPEC = pl.BlockSpec(memory_space=pltpu.HBM)


def _other_chips(x, y):
    return [(1 - x, y), (x, 1 - y), (1 - x, 1 - y)]


SEM_SPEC = pl.BlockSpec(memory_space=pltpu.SEMAPHORE)
DATAFLOW_EFFECT = pltpu.SideEffectType.DATAFLOW_SIDE_EFFECTING


def _split_start(name, bufs, n_copies, build):
    nb = len(bufs)

    def body(*refs):
        starts, _ = build(refs[:nb], refs[nb], refs[nb + 1])
        for cp in starts:
            cp.start()
        refs[-1][...] = jnp.zeros_like(refs[-1])

    out = pl.pallas_call(
        body, name=name,
        out_shape=(pltpu.SemaphoreType.DMA((n_copies,)), pltpu.SemaphoreType.DMA((n_copies,)),
                   *[pltpu.HBM(b.shape, b.dtype) for b in bufs], jax.ShapeDtypeStruct((SUBLANES, LANES), F32)),
        in_specs=[HBM_SPEC] * nb,
        out_specs=(SEM_SPEC, SEM_SPEC, *[HBM_SPEC] * nb, pl.BlockSpec(memory_space=pltpu.VMEM)),
        input_output_aliases={i: 2 + i for i in range(nb)},
        compiler_params=pltpu.CompilerParams(has_side_effects=DATAFLOW_EFFECT),
    )(*[pltpu.with_memory_space_constraint(b, pltpu.HBM) for b in bufs])
    return out[0], out[1], list(out[2:2 + nb]), out[2 + nb]


def _split_wait(name, send_sems, recv_sems, bufs, after, build):
    nb = len(bufs)

    def body(*refs):
        starts, waits = build(refs[:nb], refs[nb], refs[nb + 1])
        for cp in starts:
            cp.wait_send()
        for cp in waits:
            cp.wait_recv()

    out = pl.pallas_call(
        body, name=name, out_shape=tuple(pltpu.HBM(b.shape, b.dtype) for b in bufs),
        in_specs=[HBM_SPEC] * nb + [SEM_SPEC, SEM_SPEC, pl.BlockSpec(memory_space=pl.ANY)],
        out_specs=tuple([HBM_SPEC] * nb), input_output_aliases={i: i for i in range(nb)},
        compiler_params=pltpu.CompilerParams(has_side_effects=DATAFLOW_EFFECT),
    )(*bufs, send_sems, recv_sems, after)
    return list(out)


def _gather_ici_copies(nt, refs, send_sems, recv_sems):
    srcs, lands = refs[:nt], refs[nt:]
    x, y, c = lax.axis_index("x"), lax.axis_index("y"), lax.axis_index("c")
    me = 2 * x + y
    starts, waits = [], []
    for t in range(nt):
        for j, (px, py) in enumerate(_other_chips(x, y)):
            def copy(slot, t=t, j=j, px=px, py=py):
                return pltpu.make_async_remote_copy(
                    src_ref=srcs[t].at[c], dst_ref=lands[t].at[slot].at[c], send_sem=send_sems.at[3 * t + j],
                    recv_sem=recv_sems.at[3 * t + j], device_id=(px, py, c), device_id_type=pl.DeviceIdType.MESH)
            starts.append(copy(me))
            waits.append(copy(2 * px + py))
    return starts, waits


def _gather_d2d_copies(nt, refs, send_sems, recv_sems):
    x, y, c = lax.axis_index("x"), lax.axis_index("y"), lax.axis_index("c")
    starts, waits = [], []
    for t in range(nt):
        for j, (px, py) in enumerate(_other_chips(x, y)):
            def copy(half, t=t, j=j, px=px, py=py):
                place = refs[t].at[2 * px + py].at[half]
                return pltpu.make_async_remote_copy(
                    src_ref=place, dst_ref=place, send_sem=send_sems.at[3 * t + j], recv_sem=recv_sems.at[3 * t + j],
                    device_id=(x, y, 1 - c), device_id_type=pl.DeviceIdType.MESH)
            starts.append(copy(c))
            waits.append(copy(1 - c))
    return starts, waits


def _scatter_ici_copies(nt, refs, send_sems, recv_sems):
    srcs, lands = refs[:nt], refs[nt:]
    x, y, c = lax.axis_index("x"), lax.axis_index("y"), lax.axis_index("c")
    me = 2 * x + y
    starts, waits = [], []
    for t in range(nt):
        for j, (px, py) in enumerate(_other_chips(x, y)):
            def copy(slot, t=t, j=j, px=px, py=py):
                return pltpu.make_async_remote_copy(
                    src_ref=srcs[t].at[2 * px + py], dst_ref=lands[t].at[slot], send_sem=send_sems.at[3 * t + j],
                    recv_sem=recv_sems.at[3 * t + j], device_id=(px, py, c), device_id_type=pl.DeviceIdType.MESH)
            starts.append(copy(me))
            waits.append(copy(2 * px + py))
    return starts, waits


def _sibling_exchange(arrs, other_layer, name):
    n = len(arrs)

    def body(*refs):
        ins, outs = refs[:n], refs[n:2 * n]
        send_sems, recv_sems = refs[2 * n:]
        c = lax.axis_index("c")
        sib = (lax.axis_index("x"), lax.axis_index("y"), 1 - c)
        copies = [pltpu.make_async_remote_copy(src_ref=ins[t].at[1 - c] if other_layer else ins[t], dst_ref=outs[t],
                                               send_sem=send_sems.at[t], recv_sem=recv_sems.at[t], device_id=sib,
                                               device_id_type=pl.DeviceIdType.MESH) for t in range(n)]
        for cp in copies:
            cp.start()
        for cp in copies:
            cp.wait_recv()
        for cp in copies:
            cp.wait_send()

    return pl.pallas_call(
        body, name=name, in_specs=[HBM_SPEC] * n, out_specs=(HBM_SPEC,) * n,
        out_shape=tuple(jax.ShapeDtypeStruct(a.shape[1:] if other_layer else a.shape, a.dtype) for a in arrs),
        scratch_shapes=[pltpu.SemaphoreType.DMA((n,)), pltpu.SemaphoreType.DMA((n,))])(*arrs)


def _all_devices_gather(buf, name):
    def body(in_ref, out_ref, send_sems, recv_sems, local_sem):
        x, y, c = lax.axis_index("x"), lax.axis_index("y"), lax.axis_index("c")
        me = 4 * x + 2 * y + c

        def peer(mask):
            px = 1 - x if mask & 4 else x
            py = 1 - y if mask & 2 else y
            pc = 1 - c if mask & 1 else c
            return px, py, pc

        def remote(mask, dst_slot):
            return pltpu.make_async_remote_copy(
                src_ref=in_ref, dst_ref=out_ref.at[dst_slot], send_sem=send_sems.at[mask - 1],
                recv_sem=recv_sems.at[mask - 1], device_id=peer(mask), device_id_type=pl.DeviceIdType.MESH)

        lc = pltpu.make_async_copy(in_ref, out_ref.at[me], local_sem)
        lc.start()
        sends = [remote(mask, me) for mask in range(1, N_DEVICES)]
        for cp in sends:
            cp.start()
        for mask in range(1, N_DEVICES):
            px, py, pc = peer(mask)
            remote(mask, 4 * px + 2 * py + pc).wait_recv()
        for cp in sends:
            cp.wait_send()
        lc.wait()

    return pl.pallas_call(
        body, name=name, in_specs=[HBM_SPEC], out_specs=HBM_SPEC,
        out_shape=jax.ShapeDtypeStruct((N_DEVICES,) + buf.shape, buf.dtype),
        scratch_shapes=[pltpu.SemaphoreType.DMA((N_DEVICES - 1,)), pltpu.SemaphoreType.DMA((N_DEVICES - 1,)),
                        pltpu.SemaphoreType.DMA])(buf)


def _pad_lanes(vec):
    return jnp.pad(vec.astype(F32), (0, LANES - vec.shape[0])).reshape(1, LANES)


def _layer_fwd(x, wl, fetch, dm, tag):
    heads, gw, lw, d = dm['heads'], dm['gw'], dm['lw'], dm['d']
    h = _rms_fwd(x, wl['attn_norm'], f"rms1_fwd{tag}")
    wl.update(fetch('in', h))
    proj = _matmul(h, wl['w_in_p'], mode='nn', name=f"proj{tag}")
    alog, dtb = _pad_lanes(wl['gdn_a_log']), _pad_lanes(wl['gdn_dt_bias'])
    qkv = _gdn_pre_fwd(proj, wl['gdn_conv_w'], heads, f"gdn_pre_fwd{tag}")
    o, t_all, s0_all = _gdn_fwd(qkv, proj, alog, dtb, heads, f"gdn_fwd{tag}")
    o_gdn = _gdn_post_fwd(o, proj, dm['z_blk'], wl['gdn_norm'], f"gdn_post_fwd{tag}")
    xc = _conv_bias_fwd(proj, dm['xb_blk'], wl['lru_conv_w'], wl['lru_conv_b'], f"lru_conv_fwd{tag}")
    hseq, o_lru = _lru_fwd(xc, proj, dm['yb_blk'], wl['lru_w_a'], wl['lru_w_x'], wl['lru_b_a'], wl['lru_b_x'],
                           wl['lru_lambda'], f"lru_fwd{tag}")
    wl.update(fetch('mix', o))
    bg = _matmul(o_gdn, wl['w_branch_gdn'], mode='nn', name=f"branch_gdn{tag}")
    bl = _matmul(o_lru, wl['w_branch_lru'], mode='nn', name=f"branch_lru{tag}")
    merged = _merge_fwd(proj, dm['gg_blk'], dm['gl_blk'], bg, bl, f"merge_fwd{tag}")
    wl.update(fetch('mlp', bg))
    x_mid = _matmul(merged, wl['w_out'], mode='nn', add=x, name=f"out_proj{tag}")
    h2 = _rms_fwd(x_mid, wl['mlp_norm'], f"rms2_fwd{tag}")
    ur, act = _matmul(h2, wl['w_up'], mode='nn', epilogue='relu2', name=f"mlp_up{tag}")
    x_out = _matmul(act, wl['w_down'], mode='nn', add=x_mid, name=f"mlp_down{tag}")
    saved = dict(x=x, h=h, proj=proj, qkv=qkv, o=o, t_all=t_all, s0_all=s0_all, o_gdn=o_gdn, xc=xc, hseq=hseq,
                 o_lru=o_lru, bg=bg, bl=bl, merged=merged, x_mid=x_mid, h2=h2, ur=ur, act=act, alog=alog, dtb=dtb)
    return x_out, saved


def _layer_bwd(dx_out, dx_out_b, wl, sv, hook, dm, tag):
    heads, gw, lw, d = dm['heads'], dm['gw'], dm['lw'], dm['d']
    g = {}
    du = _matmul(dx_out_b, wl['w_down'], mode='nt', epilogue='mul2x', extra=sv['ur'], out_dtype=BF16,
                 name=f"d_mlp_act{tag}")
    g['w_down'] = _matmul(sv['act'], dx_out_b, mode='tn', out_dtype=BF16, name=f"dw_down{tag}")
    g['w_up'] = _matmul(sv['h2'], du, mode='tn', out_dtype=BF16, name=f"dw_up{tag}")
    hook('mlp', g, wl, 'mlp_norm')
    dh2 = _matmul(du, wl['w_up'], mode='nt', name=f"d_h2{tag}")
    dx_mid, dx_mid_b, g['mlp_norm'] = _rms_bwd(sv['x_mid'], wl['mlp_norm'], dh2, dx_out, f"rms2_bwd{tag}")
    dmerged = _matmul(dx_mid_b, wl['w_out'], mode='nt', name=f"d_merged{tag}")
    g['w_out'] = _matmul(sv['merged'], dx_mid_b, mode='tn', out_dtype=BF16, name=f"dw_out{tag}")
    dgg, dgl, dbg, dbl = _merge_bwd(sv['proj'], dm['gg_blk'], dm['gl_blk'], sv['bg'], sv['bl'], dmerged,
                                    f"merge_bwd{tag}")
    g['w_branch_gdn'] = _matmul(sv['o_gdn'], dbg, mode='tn', out_dtype=BF16, name=f"dw_branch_gdn{tag}")
    g['w_branch_lru'] = _matmul(sv['o_lru'], dbl, mode='tn', out_dtype=BF16, name=f"dw_branch_lru{tag}")
    hook('mix', g, wl, 'gdn_norm')
    do_gdn = _matmul(dbg, wl['w_branch_gdn'], mode='nt', name=f"d_o_gdn{tag}")
    do_lru = _matmul(dbl, wl['w_branch_lru'], mode='nt', name=f"d_o_lru{tag}")
    d_o, dz, dgn = _gdn_post_bwd(sv['o'], sv['proj'], dm['z_blk'], wl['gdn_norm'], do_gdn, f"gdn_post_bwd{tag}")
    g['gdn_norm'] = dgn.reshape(-1)
    dq, dk, dv, dgb = _gdn_bwd(sv['qkv'], sv['proj'], sv['alog'], sv['dtb'], sv['t_all'], sv['s0_all'], d_o, heads,
                               f"gdn_bwd{tag}")
    dqkv_n = jnp.concatenate([dq, dk, dv], axis=1)
    dconv = _gdn_pre_bwd(sv['proj'], wl['gdn_conv_w'], dqkv_n, heads, f"gdn_pre_bwd{tag}")
    dqkv, g['gdn_conv_w'], _ = _conv_bwd(dconv, sv['proj'], 0, wl['gdn_conv_w'], f"gdn_conv_bwd{tag}")
    dab, dab_sum = _dab_reduce(dgb, f"dab_reduce{tag}")
    g['gdn_dt_bias'] = dab_sum[0, :heads]
    g['gdn_a_log'] = dab_sum[0, 2 * heads:3 * heads]
    dxc, dyb, g['lru_w_a'], g['lru_w_x'], dba, dbx, dlam = _lru_bwd(
        do_lru, sv['xc'], sv['hseq'], sv['proj'], dm['yb_blk'], wl['lru_w_a'], wl['lru_w_x'], wl['lru_b_a'],
        wl['lru_b_x'], wl['lru_lambda'], f"lru_bwd{tag}")
    g['lru_b_a'], g['lru_b_x'], g['lru_lambda'] = dba.reshape(-1), dbx.reshape(-1), dlam.reshape(-1)
    dxb, g['lru_conv_w'], dcb = _conv_bwd(dxc, sv['proj'], dm['xb_blk'], wl['lru_conv_w'], f"lru_conv_bwd{tag}")
    g['lru_conv_b'] = dcb.reshape(-1)
    dproj = jnp.concatenate([dqkv.astype(BF16), dz.astype(BF16), dxb.astype(BF16), dyb.astype(BF16), dgg, dgl,
                             dab.astype(BF16)], axis=1)
    g['w_in_p'] = _matmul(sv['h'], dproj, mode='tn', out_dtype=BF16, name=f"dw_in{tag}")
    hook('in', g, wl, 'attn_norm')
    dh = _matmul(dproj, wl['w_in_p'], mode='nt', name=f"d_h{tag}")
    dx_in, dx_in_b, g['attn_norm'] = _rms_bwd(sv['x'], wl['attn_norm'], dh, dx_mid, f"rms1_bwd{tag}")
    g['attn_norm'] = g['attn_norm'].reshape(-1)
    g['mlp_norm'] = g['mlp_norm'].reshape(-1)
    return dx_in, dx_in_b, g


def _dims(d, heads, lw):
    gw = heads * LANES
    nab = 2 * heads
    blk = dict(z_blk=3 * heads, xb_blk=4 * heads, yb_blk=4 * heads + lw // LANES)
    gg0 = 4 * gw + 2 * lw
    return dict(d=d, heads=heads, gw=gw, lw=lw, nab=nab, gg_blk=gg0 // LANES, gl_blk=(gg0 + d) // LANES,
                main=gg0 + 2 * d, np=gg0 + 2 * d + LANES, **blk)


def _pad_w_in(w_in, dm):
    c0 = 4 * dm['gw']
    nab = dm['nab']
    return jnp.concatenate([w_in[:, :c0], w_in[:, c0 + nab:], w_in[:, c0:c0 + nab],
                            jnp.zeros((w_in.shape[0], LANES - nab), w_in.dtype)], axis=1)


def _unpad_w_in(gp, dm):
    c0 = 4 * dm['gw']
    nab = dm['nab']
    main = dm['main']
    return jnp.concatenate([gp[:, :c0], gp[:, main:main + nab], gp[:, c0:main]], axis=1)


def _local_step(x, target, layers, fetchers, hooks, final_norm, dm):
    saved = []
    cur = x
    for li, wl in enumerate(layers):
        cur, sv = _layer_fwd(cur, wl, fetchers[li], dm, f"_l{li}")
        saved.append(sv)
    loss_blk, dx, dx_b, dfin = _loss_head(cur, final_norm, target, "loss_head")
    grads = [None] * len(layers)
    for li in reversed(range(len(layers))):
        dx, dx_b, grads[li] = _layer_bwd(dx, dx_b, layers[li], saved[li], hooks[li], dm, f"_l{li}")
    return loss_blk[0, 0], dx, grads, dfin.reshape(-1)


def kernel(x, attn_norm, w_in, gdn_conv_w, gdn_a_log, gdn_dt_bias, gdn_norm, lru_conv_w, lru_conv_b, lru_w_a, lru_b_a, lru_w_x, lru_b_x, lru_lambda, w_branch_gdn, w_branch_lru, w_out, mlp_norm, w_up, w_down, final_norm, loss_target, m_attn_norm, m_w_in, m_gdn_conv_w, m_gdn_a_log, m_gdn_dt_bias, m_gdn_norm, m_lru_conv_w, m_lru_conv_b, m_lru_w_a, m_lru_b_a, m_lru_w_x, m_lru_b_x, m_lru_lambda, m_w_branch_gdn, m_w_branch_lru, m_w_out, m_mlp_norm, m_w_up, m_w_down, m_final_norm, v_attn_norm, v_w_in, v_gdn_conv_w, v_gdn_a_log, v_gdn_dt_bias, v_gdn_norm, v_lru_conv_w, v_lru_conv_b, v_lru_w_a, v_lru_b_a, v_lru_w_x, v_lru_b_x, v_lru_lambda, v_w_branch_gdn, v_w_branch_lru, v_w_out, v_mlp_norm, v_w_up, v_w_down, v_final_norm):
    w = dict(attn_norm=attn_norm, w_in=w_in, gdn_conv_w=gdn_conv_w, gdn_a_log=gdn_a_log, gdn_dt_bias=gdn_dt_bias,
             gdn_norm=gdn_norm, lru_conv_w=lru_conv_w, lru_conv_b=lru_conv_b, lru_w_a=lru_w_a, lru_b_a=lru_b_a,
             lru_w_x=lru_w_x, lru_b_x=lru_b_x, lru_lambda=lru_lambda, w_branch_gdn=w_branch_gdn,
             w_branch_lru=w_branch_lru, w_out=w_out, mlp_norm=mlp_norm, w_up=w_up, w_down=w_down,
             final_norm=final_norm)
    m = dict(attn_norm=m_attn_norm, w_in=m_w_in, gdn_conv_w=m_gdn_conv_w, gdn_a_log=m_gdn_a_log,
             gdn_dt_bias=m_gdn_dt_bias, gdn_norm=m_gdn_norm, lru_conv_w=m_lru_conv_w, lru_conv_b=m_lru_conv_b,
             lru_w_a=m_lru_w_a, lru_b_a=m_lru_b_a, lru_w_x=m_lru_w_x, lru_b_x=m_lru_b_x, lru_lambda=m_lru_lambda,
             w_branch_gdn=m_w_branch_gdn, w_branch_lru=m_w_branch_lru, w_out=m_w_out, mlp_norm=m_mlp_norm,
             w_up=m_w_up, w_down=m_w_down, final_norm=m_final_norm)
    v = dict(attn_norm=v_attn_norm, w_in=v_w_in, gdn_conv_w=v_gdn_conv_w, gdn_a_log=v_gdn_a_log,
             gdn_dt_bias=v_gdn_dt_bias, gdn_norm=v_gdn_norm, lru_conv_w=v_lru_conv_w, lru_conv_b=v_lru_conv_b,
             lru_w_a=v_lru_w_a, lru_b_a=v_lru_b_a, lru_w_x=v_lru_w_x, lru_b_x=v_lru_b_x, lru_lambda=v_lru_lambda,
             w_branch_gdn=v_w_branch_gdn, w_branch_lru=v_w_branch_lru, w_out=v_w_out, mlp_norm=v_mlp_norm,
             w_up=v_w_up, w_down=v_w_down, final_norm=v_final_norm)
    n_layers = attn_norm.shape[0]
    d = x.shape[-1]
    heads = gdn_a_log.shape[-1]
    lw = lru_conv_b.shape[-1]
    dm = _dims(d, heads, lw)
    big_names = list(BIG_SHARD_AXIS)
    conv_names = list(CONV_SHARD_AXIS)
    chip = 2 * lax.axis_index("x") + lax.axis_index("y")

    conv_flat = jnp.concatenate([w[n].reshape(-1) for n in conv_names])
    conv_rows = -(-conv_flat.shape[0] // (SUBLANES * LANES)) * SUBLANES
    conv_buf = jnp.pad(conv_flat, (0, conv_rows * LANES - conv_flat.shape[0])).reshape(conv_rows, LANES)
    conv_all = _all_devices_gather(conv_buf, "conv_allgather").reshape(N_CHIPS, 2, -1)[:, 0]
    conv_full, off = {}, 0
    for n in conv_names:
        shard = w[n]
        parts = conv_all[:, off:off + shard.size].reshape((N_CHIPS,) + shard.shape)
        conv_full[n] = jnp.concatenate([parts[q] for q in range(N_CHIPS)], axis=CONV_SHARD_AXIS[n])
        off += shard.size

    def start_gather(li, group):
        halves, lands = [], []
        for n in GATHER_GROUPS[group]:
            s = w[n][li].astype(BF16)
            hv = s.reshape((2, s.shape[0] // 2) + s.shape[1:])
            halves.append(hv)
            lands.append(lax.dynamic_update_index_in_dim(lax.empty((N_CHIPS,) + hv.shape, BF16), hv, chip, 0))
        nt = len(halves)
        return _split_start(f"wgather_{group}_l{li}_ici_start", halves + lands, 3 * nt,
                            functools.partial(_gather_ici_copies, nt))

    pending = {(li, group): start_gather(li, group) for li in range(n_layers) for group in GATHER_GROUPS}

    def make_fetch(li):
        def fetch(group, after):
            names = GATHER_GROUPS[group]
            nt = len(names)
            send, recv, bufs, _ = pending.pop((li, group))
            bufs = _split_wait(f"wgather_{group}_l{li}_ici_wait", send, recv, bufs, after,
                               functools.partial(_gather_ici_copies, nt))
            send, recv, lands, token = _split_start(f"wgather_{group}_l{li}_d2d_start", bufs[nt:], 3 * nt,
                                                    functools.partial(_gather_d2d_copies, nt))
            lands = _split_wait(f"wgather_{group}_l{li}_d2d_wait", send, recv, lands, token,
                                functools.partial(_gather_d2d_copies, nt))
            out = {}
            for n, land in zip(names, lands):
                slots = land.reshape((N_CHIPS, 2 * land.shape[2]) + land.shape[3:])
                out[n] = jnp.concatenate([slots[q] for q in range(N_CHIPS)], axis=BIG_SHARD_AXIS[n] - 1)
            if 'w_in' in out:
                out['w_in_p'] = _pad_w_in(out.pop('w_in'), dm)
            return out
        return fetch

    layers = []
    for li in range(n_layers):
        wl = {n: w[n][li] for n in SMALL_NAMES if n != 'final_norm' and n not in CONV_SHARD_AXIS}
        for n in conv_names:
            wl[n] = conv_full[n][li]
        layers.append(wl)
    layers[0]['attn_norm'] = layers[0]['attn_norm'] + sum(handle[3][0, 0] for handle in pending.values())

    core = lax.axis_index("c").astype(jnp.int32).reshape(1)
    chip_op = chip.astype(jnp.int32).reshape(1)
    in_flight, reduced = {}, {}

    def reduce_begin(group, li, g):
        names = GATHER_GROUPS[group]
        nt = len(names)
        contrib = []
        for n in names:
            full = _unpad_w_in(g['w_in_p'], dm) if n == 'w_in' else g[n]
            pieces = jnp.stack(jnp.split(full, N_CHIPS, axis=BIG_SHARD_AXIS[n] - 1), axis=0)
            rows_half = pieces.shape[1] // 2
            contrib.append(jnp.swapaxes(pieces.reshape((N_CHIPS, 2, rows_half) + pieces.shape[2:]), 0, 1))
        theirs = _sibling_exchange(contrib, True, f"gsend_{group}_l{li}")
        sums = [_pair_sum(core, mine, th, f"gpair_{n}_l{li}") for n, mine, th in zip(names, contrib, theirs)]
        lands = [jnp.zeros(sm.shape, BF16) for sm in sums]
        send, recv, bufs, token = _split_start(f"gscatter_{group}_l{li}_start", sums + lands, 3 * nt,
                                               functools.partial(_scatter_ici_copies, nt))
        in_flight[(group, li)] = (send, recv, bufs)
        return token

    def reduce_end(group, li, after):
        names = GATHER_GROUPS[group]
        nt = len(names)
        send, recv, bufs = in_flight.pop((group, li))
        bufs = _split_wait(f"gscatter_{group}_l{li}_wait", send, recv, bufs, after,
                           functools.partial(_scatter_ici_copies, nt))
        totals = [_sum_landed(chip_op, land, own, f"gtotal_{n}_l{li}")
                  for n, own, land in zip(names, bufs[:nt], bufs[nt:])]
        others = _sibling_exchange(totals, False, f"gswap_{group}_l{li}")
        for n, mine, other in zip(names, totals, others):
            reduced[(n, li)] = (mine, other)

    def make_hook(li):
        def hook(group, g, wl, gain):
            wl[gain] = wl[gain] + reduce_begin(group, li, g)[0, 0]
        return hook

    loss_local, dx, grads, dfin = _local_step(x[0], loss_target[0], layers, [make_fetch(li) for li in range(n_layers)],
                                              [make_hook(li) for li in range(n_layers)], final_norm, dm)
    loss = lax.psum(loss_local, MESH_AXES)
    for li in reversed(range(n_layers)):
        for group in reversed(list(GATHER_GROUPS)):
            reduce_end(group, li, dx)

    small_g = {n: jnp.stack([grads[li][n] for li in range(n_layers)], axis=0)
               for n in SMALL_NAMES if n != 'final_norm'}
    small_g['final_norm'] = dfin
    flat = jnp.concatenate([small_g[n].reshape(-1) for n in SMALL_NAMES])
    n_flat = flat.shape[0]
    row_unit = 32 * SUBLANES
    rows = -(-n_flat // (row_unit * LANES)) * row_unit
    buf = jnp.pad(flat, (0, rows * LANES - n_flat)).reshape(rows, LANES)
    everyone = _all_devices_gather(buf, "small_grad_allgather")
    small_sum = _sum_slots(everyone, "small_grad_sum").reshape(-1)
    small_red = {}
    off = 0
    for n in SMALL_NAMES:
        size = small_g[n].size
        small_red[n] = small_sum[off:off + size].reshape(small_g[n].shape)
        off += size
    for n, ax in CONV_SHARD_AXIS.items():
        width = w[n].shape[ax]
        small_red[n] = lax.dynamic_slice_in_dim(small_red[n], chip * width, width, axis=ax)

    out_g, out_d, out_m, out_v = {}, {}, {}, {}
    for n in big_names:
        quarters = (n_layers, 2, w[n].shape[1] // 2, w[n].shape[2])
        res = _adamw_quarters(core, w[n].reshape(quarters), [reduced[(n, li)][0] for li in range(n_layers)],
                              [reduced[(n, li)][1] for li in range(n_layers)], m[n].reshape(quarters),
                              v[n].reshape(quarters), f"adamw_{n}")
        out_g[n], out_d[n], out_m[n], out_v[n] = (r.reshape(w[n].shape) for r in res)

    def pack(tree):
        fl = jnp.concatenate([tree[n].reshape(-1) for n in SMALL_NAMES])
        return jnp.pad(fl, (0, rows * LANES - fl.shape[0])).reshape(rows, LANES)

    res = _adamw(pack(w), [pack(small_red)], pack(m), pack(v), "adamw_small")
    for r, dst in zip(res, (out_g, out_d, out_m, out_v)):
        fl = r.reshape(-1)
        off = 0
        for n in SMALL_NAMES:
            dst[n] = fl[off:off + w[n].size].reshape(w[n].shape)
            off += w[n].size

    return (loss, dx[None], *[out_g[n] for n in WEIGHT_NAMES], *[out_d[n] for n in WEIGHT_NAMES],
            *[out_m[n] for n in WEIGHT_NAMES], *[out_v[n] for n in WEIGHT_NAMES])
```

```python
import functools

import jax
import jax.numpy as jnp
from jax import lax
from jax.experimental import pallas as pl
from jax.experimental.pallas import tpu as pltpu

F32 = jnp.float32
BF16 = jnp.bfloat16

LANES = 128
SUBLANES = 8
VMEM_BYTES = 64 * 1024 * 1024
GDN_CHUNK = 64
CONV_WIDTH = 4
CONV_ROW_TILE = 2048
RMS_EPS = 1e-6
L2_EPS = 1e-6
LRU_C = 8.0
ADAM_LR = 0.001
ADAM_B1 = 0.9
ADAM_B2 = 0.999
ADAM_EPS = 1e-08
ADAM_WD = 0.01
ADAM_STEP = 10
MESH_AXES = ("x", "y", "c")
N_CHIPS = 4
N_DEVICES = 8

INPUT_NAMES = ['x', 'attn_norm', 'w_in', 'gdn_conv_w', 'gdn_a_log', 'gdn_dt_bias', 'gdn_norm', 'lru_conv_w',
               'lru_conv_b', 'lru_w_a', 'lru_b_a', 'lru_w_x', 'lru_b_x', 'lru_lambda', 'w_branch_gdn',
               'w_branch_lru', 'w_out', 'mlp_norm', 'w_up', 'w_down', 'final_norm']
WEIGHT_NAMES = INPUT_NAMES[1:]
BIG_SHARD_AXIS = {'w_in': 2, 'w_branch_gdn': 2, 'w_branch_lru': 2, 'w_out': 1, 'w_up': 2, 'w_down': 1}
CONV_SHARD_AXIS = {'gdn_conv_w': 2, 'lru_conv_w': 2}
GATHER_GROUPS = {'in': ['w_in'], 'mix': ['w_branch_gdn', 'w_branch_lru', 'w_out'], 'mlp': ['w_up', 'w_down']}
SMALL_NAMES = [n for n in WEIGHT_NAMES if n not in BIG_SHARD_AXIS]


def _tile(n, target, unit=LANES):
    best = None
    t = unit
    while t <= min(n, target):
        if n % t == 0:
            best = t
        t += unit
    return n if best is None else best


def _vmem_limit(block_bytes):
    return int(min(max(3 * block_bytes + (8 << 20), 24 << 20), VMEM_BYTES - (8 << 20)))


def _nbytes(shape, dtype):
    n = 1
    for s in shape:
        n *= s
    return n * jnp.dtype(dtype).itemsize


def _pcall(body, *, name, grid, in_specs, out_specs, out_shape, scratch_shapes=(), semantics=None, block_bytes=0,
           scalar_prefetch=0):
    params = dict(vmem_limit_bytes=_vmem_limit(block_bytes))
    if semantics is not None:
        params['dimension_semantics'] = semantics
    if scalar_prefetch:
        grid_spec = pltpu.PrefetchScalarGridSpec(num_scalar_prefetch=scalar_prefetch, grid=grid, in_specs=in_specs,
                                                 out_specs=out_specs, scratch_shapes=list(scratch_shapes))
        return pl.pallas_call(body, name=name, grid_spec=grid_spec, out_shape=out_shape,
                              compiler_params=pltpu.CompilerParams(**params))
    return pl.pallas_call(body, name=name, grid=grid, in_specs=in_specs, out_specs=out_specs, out_shape=out_shape,
                          scratch_shapes=list(scratch_shapes), compiler_params=pltpu.CompilerParams(**params))


def _dot(a, b):
    return jnp.dot(a.astype(BF16), b.astype(BF16), preferred_element_type=F32)


def _dot_nt(a, b):
    return lax.dot_general(a.astype(BF16), b.astype(BF16), (((1,), (1,)), ((), ())), preferred_element_type=F32)


def _dot_tn(a, b):
    return lax.dot_general(a.astype(BF16), b.astype(BF16), (((0,), (0,)), ((), ())), preferred_element_type=F32)


def _sigmoid(x):
    return 1.0 / (1.0 + jnp.exp(-x))


def _log1p(u):
    return jnp.where(u < 1e-3, u * (1.0 - u * (0.5 - u * (1.0 / 3.0))), jnp.log(1.0 + u))


def _softplus(x):
    return jnp.maximum(x, 0.0) + _log1p(jnp.exp(-jnp.abs(x)))


_GELU_K = 0.7978845608028654


def _gelu_and_grad(x):
    inner = _GELU_K * (x + 0.044715 * x * x * x)
    th = jnp.tanh(inner)
    g = 0.5 * x * (1.0 + th)
    dg = 0.5 * (1.0 + th) + 0.5 * x * (1.0 - th * th) * _GELU_K * (1.0 + 3.0 * 0.044715 * x * x)
    return g, dg


MATMUL_TK_MAX = 3584


def _matmul(a, b, *, mode, name, out_dtype=F32, add=None, epilogue=None, extra=None, tm=512, tn=1024, tk=2048,
            shard_axis=None):
    if mode == 'nn':
        (m, k), (k2, n) = a.shape, b.shape
    elif mode == 'nt':
        (m, k), (n, k2) = a.shape, b.shape
    else:
        (k, m), (k2, n) = a.shape, b.shape
    assert k == k2, (a.shape, b.shape, mode)
    if shard_axis is not None:
        rows_half = (m // N_CHIPS if shard_axis == 0 else m) // 2
        cols = n // N_CHIPS if shard_axis == 1 else n
        tm, tn = _tile(rows_half, tm), _tile(cols, tn)
    else:
        tm, tn = _tile(m, tm), _tile(n, tn)
    tk = _tile(k, tk)
    if k // tk > 2 * (-(-k // MATMUL_TK_MAX)):
        tk = _tile(k, MATMUL_TK_MAX)
    nk = k // tk
    dims = {'nn': (((1,), (0,)), ((), ())), 'nt': (((1,), (1,)), ((), ())), 'tn': (((0,), (0,)), ((), ()))}[mode]
    a_bytes, b_bytes = _nbytes(a.shape, a.dtype), _nbytes(b.shape, b.dtype)
    rows_outer = nk > 1 or a_bytes + (m // tm) * b_bytes <= b_bytes + (n // tn) * a_bytes

    def ij(g0, g1):
        return (g0, g1) if rows_outer else (g1, g0)

    def spec(shape, pick):
        return pl.BlockSpec(shape, lambda g0, g1, kk: pick(*ij(g0, g1), kk))

    a_spec = spec((tk, tm), lambda i, j, kk: (kk, i)) if mode == 'tn' else spec((tm, tk), lambda i, j, kk: (i, kk))
    b_spec = spec((tn, tk), lambda i, j, kk: (j, kk)) if mode == 'nt' else spec((tk, tn), lambda i, j, kk: (kk, j))
    o_spec = spec((tm, tn), lambda i, j, kk: (i, j))
    operands, in_specs = [a, b], [a_spec, b_spec]
    if add is not None:
        operands.append(add)
        in_specs.append(o_spec)
    if extra is not None:
        operands.append(extra)
        in_specs.append(o_spec)
    n_in = len(operands)
    if epilogue == 'relu2':
        out_shape = (jax.ShapeDtypeStruct((m, n), BF16), jax.ShapeDtypeStruct((m, n), BF16))
        out_specs = (o_spec, o_spec)
    elif shard_axis is not None:
        assert add is None and extra is None
        rb, cb = rows_half // tm, cols // tn

        def shard_block(i, j, kk):
            if shard_axis == 0:
                return (i % (2 * rb)) // rb, i // (2 * rb), i % rb, j
            return i // rb, j // cb, i % rb, j % cb

        out_shape = jax.ShapeDtypeStruct((2, N_CHIPS, rows_half, cols), out_dtype)
        out_specs = spec((None, None, tm, tn), shard_block)
    else:
        out_shape = jax.ShapeDtypeStruct((m, n), out_dtype)
        out_specs = o_spec

    def body(*refs):
        a_ref, b_ref = refs[0], refs[1]
        outs = refs[n_in:n_in + n_out]

        def finish(p):
            if add is not None:
                p = p + refs[2][...]
            if epilogue == 'relu2':
                ur = jnp.maximum(p, 0.0)
                outs[0][...] = ur.astype(BF16)
                outs[1][...] = (ur * ur).astype(BF16)
            elif epilogue == 'mul2x':
                outs[0][...] = (p * 2.0 * refs[n_in - 1][...].astype(F32)).astype(out_dtype)
            else:
                outs[0][...] = p.astype(out_dtype)

        prod = lax.dot_general(a_ref[...].astype(BF16), b_ref[...].astype(BF16), dims, preferred_element_type=F32)
        if nk == 1:
            finish(prod)
            return
        acc_ref = refs[-1]
        kk = pl.program_id(2)

        @pl.when(kk == 0)
        def _():
            acc_ref[...] = prod

        @pl.when((kk > 0) & (kk < nk - 1))
        def _():
            acc_ref[...] += prod

        @pl.when(kk == nk - 1)
        def _():
            finish(acc_ref[...] + prod)

    n_out = 2 if epilogue == 'relu2' else 1
    bb = (_nbytes((tm, tk), a.dtype) + _nbytes((tk, tn), b.dtype) + 3 * _nbytes((tm, tn), F32))
    grid = (m // tm, n // tn, nk) if rows_outer else (n // tn, m // tm, nk)
    return _pcall(body, name=name, grid=grid, in_specs=in_specs, out_specs=out_specs, out_shape=out_shape,
                  scratch_shapes=[pltpu.VMEM((tm, tn), F32)] if nk > 1 else [],
                  semantics=("parallel", "parallel", "arbitrary"), block_bytes=bb)(*operands)


def _row_tile(s, d, target_bytes=1 << 20):
    return _tile(s, max(SUBLANES, target_bytes // (4 * d)), SUBLANES)


def _rms_fwd(x, gain, name):
    s, d = x.shape
    tr = _row_tile(s, d)

    def body(x_ref, g_ref, h_ref):
        xv = x_ref[...]
        r = lax.rsqrt(jnp.mean(xv * xv, axis=-1, keepdims=True) + RMS_EPS)
        h_ref[...] = (xv * r * g_ref[...]).astype(BF16)

    row = pl.BlockSpec((tr, d), lambda i: (i, 0))
    return _pcall(body, name=name, grid=(s // tr,), in_specs=[row, pl.BlockSpec((1, d), lambda i: (0, 0))],
                  out_specs=row, out_shape=jax.ShapeDtypeStruct((s, d), BF16), semantics=("parallel",),
                  block_bytes=2 * tr * d * 4)(x, gain.reshape(1, d))


def _rms_bwd(x, gain, dh, dres, name):
    s, d = x.shape
    tr = _row_tile(s, d, 1 << 19)

    def body(x_ref, g_ref, dh_ref, dres_ref, dx_ref, dxb_ref, dg_ref):
        xv = x_ref[...]
        r = lax.rsqrt(jnp.mean(xv * xv, axis=-1, keepdims=True) + RMS_EPS)
        xh = xv * r
        dhv = dh_ref[...]
        dxh = dhv * g_ref[...]
        dx = dres_ref[...] + r * (dxh - xh * jnp.mean(dxh * xh, axis=-1, keepdims=True))
        dx_ref[...] = dx
        dxb_ref[...] = dx.astype(BF16)

        @pl.when(pl.program_id(0) == 0)
        def _():
            dg_ref[...] = jnp.zeros_like(dg_ref)

        dg_ref[...] += jnp.sum(dhv * xh, axis=0, keepdims=True)

    row = pl.BlockSpec((tr, d), lambda i: (i, 0))
    vec = pl.BlockSpec((1, d), lambda i: (0, 0))
    return _pcall(body, name=name, grid=(s // tr,), in_specs=[row, vec, row, row], out_specs=(row, row, vec),
                  out_shape=(jax.ShapeDtypeStruct((s, d), F32), jax.ShapeDtypeStruct((s, d), BF16),
                             jax.ShapeDtypeStruct((1, d), F32)),
                  semantics=("arbitrary",), block_bytes=5 * tr * d * 4)(x, gain.reshape(1, d), dh, dres)


def _loss_head(x, gain, target, name):
    s, d = x.shape
    tr = _row_tile(s, d, 1 << 19)

    def body(x_ref, g_ref, t_ref, loss_ref, dx_ref, dxb_ref, dg_ref):
        xv = x_ref[...]
        r = lax.rsqrt(jnp.mean(xv * xv, axis=-1, keepdims=True) + RMS_EPS)
        xh = xv * r
        gv = g_ref[...]
        err = xh * gv - t_ref[...]
        dy = err * (1.0 / d)
        dxh = dy * gv
        dx = r * (dxh - xh * jnp.mean(dxh * xh, axis=-1, keepdims=True))
        dx_ref[...] = dx
        dxb_ref[...] = dx.astype(BF16)

        @pl.when(pl.program_id(0) == 0)
        def _():
            dg_ref[...] = jnp.zeros_like(dg_ref)
            loss_ref[...] = jnp.zeros_like(loss_ref)

        dg_ref[...] += jnp.sum(dy * xh, axis=0, keepdims=True)
        part = jnp.sum(jnp.sum(err * err, axis=-1, keepdims=True), axis=0, keepdims=True) * (0.5 / d)
        loss_ref[...] += jnp.broadcast_to(part, loss_ref.shape)

    row = pl.BlockSpec((tr, d), lambda i: (i, 0))
    vec = pl.BlockSpec((1, d), lambda i: (0, 0))
    lspec = pl.BlockSpec((SUBLANES, LANES), lambda i: (0, 0))
    return _pcall(body, name=name, grid=(s // tr,), in_specs=[row, vec, row], out_specs=(lspec, row, row, vec),
                  out_shape=(jax.ShapeDtypeStruct((SUBLANES, LANES), F32), jax.ShapeDtypeStruct((s, d), F32),
                             jax.ShapeDtypeStruct((s, d), BF16), jax.ShapeDtypeStruct((1, d), F32)),
                  semantics=("arbitrary",), block_bytes=4 * tr * d * 4)(x, gain.reshape(1, d), target)


def _shift_down(xc, xp, s):
    tr = xc.shape[0]
    r = pltpu.roll(xc, s, 0)
    p = pltpu.roll(xp, s, 0)
    row8 = lax.broadcasted_iota(jnp.int32, (SUBLANES, xc.shape[1]), 0)
    head = jnp.where(row8 < s, p, r[:SUBLANES])
    if tr == SUBLANES:
        return head
    return jnp.concatenate([head, r[SUBLANES:]], axis=0)


def _shift_up(yc, yn, s):
    tr = yc.shape[0]
    u = pltpu.roll(yc, tr - s, 0)
    n = pltpu.roll(yn, SUBLANES - s, 0)
    row8 = lax.broadcasted_iota(jnp.int32, (SUBLANES, yc.shape[1]), 0)
    tail = jnp.where(row8 >= SUBLANES - s, n, u[tr - SUBLANES:])
    if tr == SUBLANES:
        return tail
    return jnp.concatenate([u[:tr - SUBLANES], tail], axis=0)


def _conv_apply(xc, xp, w):
    y = xc * w[CONV_WIDTH - 1:CONV_WIDTH, :]
    for s in range(1, CONV_WIDTH):
        y = y + _shift_down(xc, xp, s) * w[CONV_WIDTH - 1 - s:CONV_WIDTH - s, :]
    return y


def _halo_specs(tr, col_of):
    per = tr // SUBLANES
    cur = pl.BlockSpec((tr, LANES), lambda j, i: (i, col_of(j)))
    prev = pl.BlockSpec((SUBLANES, LANES), lambda j, i: (jnp.maximum(i * per - 1, 0), col_of(j)))
    return cur, prev


def _conv_bias_fwd(x_arr, x_col0, w, bias, name):
    s = x_arr.shape[0]
    ncb = w.shape[1] // LANES
    tr = _tile(s, CONV_ROW_TILE, SUBLANES)

    def body(cur_ref, prev_ref, w_ref, b_ref, o_ref):
        i = pl.program_id(1)
        xp = prev_ref[...] * (i > 0).astype(F32)
        o_ref[...] = _conv_apply(cur_ref[...], xp, w_ref[...]) + b_ref[...]

    cur, prev = _halo_specs(tr, lambda j: x_col0 + j)
    return _pcall(body, name=name, grid=(ncb, s // tr),
                  in_specs=[cur, prev, pl.BlockSpec((CONV_WIDTH, LANES), lambda j, i: (0, j)),
                            pl.BlockSpec((1, LANES), lambda j, i: (0, j))],
                  out_specs=pl.BlockSpec((tr, LANES), lambda j, i: (i, j)),
                  out_shape=jax.ShapeDtypeStruct((s, w.shape[1]), F32), semantics=("parallel", "parallel"),
                  block_bytes=3 * tr * LANES * 4)(x_arr, x_arr, w, bias.reshape(1, -1))


def _conv_bwd(dy, x_arr, x_col0, w, name):
    s, c = dy.shape
    ncb = c // LANES
    tr = _tile(s, CONV_ROW_TILE, SUBLANES)
    per = tr // SUBLANES
    ni = s // tr

    def body(dy_ref, dyn_ref, cur_ref, prev_ref, w_ref, dx_ref, dw_ref, db_ref):
        i = pl.program_id(1)
        dyv = dy_ref[...]
        dn = dyn_ref[...] * (i < ni - 1).astype(F32)
        xc = cur_ref[...]
        xp = prev_ref[...] * (i > 0).astype(F32)
        wv = w_ref[...]

        @pl.when(i == 0)
        def _():
            dw_ref[...] = jnp.zeros_like(dw_ref)
            db_ref[...] = jnp.zeros_like(db_ref)

        dx = dyv * wv[CONV_WIDTH - 1:CONV_WIDTH, :]
        dw_ref[CONV_WIDTH - 1:CONV_WIDTH, :] += jnp.sum(dyv * xc, axis=0, keepdims=True)
        for sh in range(1, CONV_WIDTH):
            j = CONV_WIDTH - 1 - sh
            dx = dx + _shift_up(dyv, dn, sh) * wv[j:j + 1, :]
            dw_ref[j:j + 1, :] += jnp.sum(dyv * _shift_down(xc, xp, sh), axis=0, keepdims=True)
        dx_ref[...] = dx
        db_ref[...] += jnp.sum(dyv, axis=0, keepdims=True)

    cur, prev = _halo_specs(tr, lambda j: x_col0 + j)
    dcur = pl.BlockSpec((tr, LANES), lambda j, i: (i, j))
    dnext = pl.BlockSpec((SUBLANES, LANES), lambda j, i: (jnp.minimum((i + 1) * per, s // SUBLANES - 1), j))
    return _pcall(body, name=name, grid=(ncb, ni),
                  in_specs=[dcur, dnext, cur, prev, pl.BlockSpec((CONV_WIDTH, LANES), lambda j, i: (0, j))],
                  out_specs=(dcur, pl.BlockSpec((CONV_WIDTH, LANES), lambda j, i: (0, j)),
                             pl.BlockSpec((1, LANES), lambda j, i: (0, j))),
                  out_shape=(jax.ShapeDtypeStruct((s, c), F32), jax.ShapeDtypeStruct((CONV_WIDTH, c), F32),
                             jax.ShapeDtypeStruct((1, c), F32)),
                  semantics=("parallel", "arbitrary"), block_bytes=4 * tr * LANES * 4)(dy, dy, x_arr, x_arr, w)


def _gdn_pre_fwd(proj, conv_w, heads, name):
    s = proj.shape[0]
    ncb = conv_w.shape[1] // LANES
    tr = _tile(s, CONV_ROW_TILE, SUBLANES)
    qscale = float(LANES) ** -0.5

    def body(cur_ref, prev_ref, w_ref, o_ref):
        j, i = pl.program_id(0), pl.program_id(1)
        xp = prev_ref[...] * (i > 0).astype(F32)
        cv = _conv_apply(cur_ref[...], xp, w_ref[...])
        sv = cv * _sigmoid(cv)
        nrm = lax.rsqrt(jnp.sum(sv * sv, axis=-1, keepdims=True) + L2_EPS)
        scale = jnp.where(j < heads, qscale, 1.0)
        o_ref[...] = jnp.where(j < 2 * heads, sv * nrm * scale, sv)

    cur, prev = _halo_specs(tr, lambda j: j)
    return _pcall(body, name=name, grid=(ncb, s // tr),
                  in_specs=[cur, prev, pl.BlockSpec((CONV_WIDTH, LANES), lambda j, i: (0, j))],
                  out_specs=pl.BlockSpec((tr, LANES), lambda j, i: (i, j)),
                  out_shape=jax.ShapeDtypeStruct((s, conv_w.shape[1]), F32), semantics=("parallel", "parallel"),
                  block_bytes=3 * tr * LANES * 4)(proj, proj, conv_w)


def _gdn_pre_bwd(proj, conv_w, dq, dk, dv, heads, name):
    s = proj.shape[0]
    ncb = conv_w.shape[1] // LANES
    tr = _tile(s, CONV_ROW_TILE, SUBLANES)
    qscale = float(LANES) ** -0.5

    def body(cur_ref, prev_ref, w_ref, dq_ref, dk_ref, dv_ref, o_ref):
        j, i = pl.program_id(0), pl.program_id(1)
        xp = prev_ref[...] * (i > 0).astype(F32)
        cv = _conv_apply(cur_ref[...], xp, w_ref[...])
        sg = _sigmoid(cv)
        sv = cv * sg
        nrm = lax.rsqrt(jnp.sum(sv * sv, axis=-1, keepdims=True) + L2_EPS)
        dv = jnp.where(j < heads, dq_ref[...], jnp.where(j < 2 * heads, dk_ref[...], dv_ref[...]))
        scale = jnp.where(j < heads, qscale, 1.0)
        dsn = scale * nrm * (dv - sv * (nrm * nrm) * jnp.sum(dv * sv, axis=-1, keepdims=True))
        ds = jnp.where(j < 2 * heads, dsn, dv)
        o_ref[...] = ds * (sg * (1.0 + cv * (1.0 - sg)))

    cur, prev = _halo_specs(tr, lambda j: j)
    blk = pl.BlockSpec((tr, LANES), lambda j, i: (i, j))

    def part(k):
        return pl.BlockSpec((tr, LANES), lambda j, i: (i, jnp.clip(j - k * heads, 0, heads - 1)))

    return _pcall(body, name=name, grid=(ncb, s // tr),
                  in_specs=[cur, prev, pl.BlockSpec((CONV_WIDTH, LANES), lambda j, i: (0, j)), part(0), part(1),
                            part(2)],
                  out_specs=blk, out_shape=jax.ShapeDtypeStruct((s, conv_w.shape[1]), F32),
                  semantics=("parallel", "parallel"), block_bytes=6 * tr * LANES * 4)(proj, proj, conv_w, dq, dk, dv)


def _tri_inverse(a_strict, block):
    n = a_strict.shape[0]
    ri = lax.broadcasted_iota(jnp.int32, (n, n), 0)
    ci = lax.broadcasted_iota(jnp.int32, (n, n), 1)
    p = jnp.where(ri == ci, 1.0, 0.0) - a_strict
    if block <= 2:
        return p
    xp = _dot(a_strict, a_strict)
    span = 2
    while True:
        p_next = p + _dot(p, xp)
        span *= 2
        if span >= block:
            return p_next
        xp = _dot(xp, xp)
        p = p_next


GDN_HEAD_GROUP = 4
_CHUNK_SHIFT = GDN_CHUNK.bit_length() - 1
_LANE_SHIFT = LANES.bit_length() - 1


def _stack_heads(ref, hb):
    return jnp.concatenate([ref[:, i * LANES:(i + 1) * LANES] for i in range(hb)], axis=0)


def _diag_blocks(x, hb):
    c = GDN_CHUNK
    return jnp.concatenate([x[i * c:(i + 1) * c, i * LANES:(i + 1) * LANES] for i in range(hb)], axis=0)


def _expand_blocks(y, hb):
    row_blk = lax.shift_right_logical(lax.broadcasted_iota(jnp.int32, y.shape, 0), _CHUNK_SHIFT)
    return jnp.concatenate([jnp.where(row_blk == j, y, 0.0) for j in range(hb)], axis=1)


def _gdn_group_terms(q, k, v, ab, alog, dtb, head0, hb, heads):
    c = GDN_CHUNK
    r = hb * c
    lane = lax.broadcasted_iota(jnp.int32, (1, LANES), 1)

    def column(lane0):
        return jnp.concatenate([jnp.sum(jnp.where(lane == lane0 + head0 + i, ab, 0.0), axis=1, keepdims=True)
                                for i in range(hb)], axis=0)

    def per_head(vec):
        return jnp.concatenate([jnp.broadcast_to(jnp.sum(jnp.where(lane == head0 + i, vec, 0.0), axis=1,
                                                         keepdims=True), (c, 1)) for i in range(hb)], axis=0)

    pre = column(0) + per_head(dtb)
    neg_ea = -jnp.exp(per_head(alog))
    g = neg_ea * _softplus(pre)
    beta = _sigmoid(column(heads))
    ri = lax.broadcasted_iota(jnp.int32, (r, r), 0)
    ci = lax.broadcasted_iota(jnp.int32, (r, r), 1)
    same = lax.shift_right_logical(ri, _CHUNK_SHIFT) == lax.shift_right_logical(ci, _CHUNK_SHIFT)
    eye = ri == ci
    causal = same & (ri >= ci)
    strict = same & (ri > ci)
    g_row = jnp.sum(jnp.where(eye, g, 0.0), axis=0, keepdims=True)
    gc_col = jnp.sum(jnp.where(causal, g_row, 0.0), axis=1, keepdims=True)
    gc_row = jnp.sum(jnp.where(same & (ri <= ci), g, 0.0), axis=0, keepdims=True)
    gl_col = jnp.sum(jnp.where(same, g_row, 0.0), axis=1, keepdims=True)
    decay = jnp.where(causal, jnp.exp(jnp.where(causal, gc_col - gc_row, 0.0)), 0.0)
    e_last_col = jnp.exp(gl_col)
    e_last_lanes = jnp.concatenate([jnp.broadcast_to(e_last_col[i * c:i * c + 1, :], (1, LANES))
                                    for i in range(hb)], axis=1)
    egc = jnp.exp(gc_col)
    ekl = jnp.exp(gl_col - gc_col)
    kb = k * beta
    vb = v * beta
    kk = _dot_nt(kb, k)
    a_strict = jnp.where(strict, kk * decay, 0.0)
    return dict(pre=pre, neg_ea=neg_ea, g=g, beta=beta, ri=ri, ci=ci, same=same, eye=eye, causal=causal,
                strict=strict, decay=decay, e_last_col=e_last_col, e_last_lanes=e_last_lanes, egc=egc, ekl=ekl,
                kb=kb, vb=vb, kk=kk, a_strict=a_strict, lane=lane)


def _gdn_head_group(heads):
    hb = GDN_HEAD_GROUP
    while heads % hb:
        hb //= 2
    return hb


def _gdn_fwd(qkv, proj, alog, dtb, heads, name):
    s = qkv.shape[0]
    c = GDN_CHUNK
    nc = s // c
    ab_blk = proj.shape[1] // LANES - 1
    hb = _gdn_head_group(heads)
    ng = heads // hb
    r = hb * c

    def body(q_ref, k_ref, v_ref, ab_ref, alog_ref, dtb_ref, o_ref, t_ref, s0_ref, state_ref):
        grp, ch = pl.program_id(0), pl.program_id(1)

        @pl.when(ch == 0)
        def _():
            state_ref[...] = jnp.zeros_like(state_ref)

        q, k, v = _stack_heads(q_ref, hb), _stack_heads(k_ref, hb), _stack_heads(v_ref, hb)
        tm = _gdn_group_terms(q, k, v, ab_ref[...], alog_ref[...], dtb_ref[...], grp * hb, hb, heads)
        t_inv = _tri_inverse(tm['a_strict'], c)
        u = _dot(t_inv, tm['vb'])
        w = _dot(t_inv, tm['kb'] * tm['egc'])
        qk = jnp.where(tm['causal'], _dot_nt(q, k) * tm['decay'], 0.0)
        st = state_ref[...]
        v_new = u - _diag_blocks(_dot(w, st), hb)
        out = _diag_blocks(_dot(q * tm['egc'], st), hb) + _dot(qk, v_new)
        for i in range(hb):
            o_ref[:, i * LANES:(i + 1) * LANES] = out[i * c:(i + 1) * c, :]
        t_ref[...] = t_inv
        s0_ref[...] = st
        state_ref[...] = st * tm['e_last_lanes'] + _dot_tn(k * tm['ekl'], _expand_blocks(v_new, hb))

    def blk(off):
        return pl.BlockSpec((c, hb * LANES), lambda g, n: (n, off * ng + g))

    vec = pl.BlockSpec((1, LANES), lambda g, n: (0, 0))
    return _pcall(
        body, name=name, grid=(ng, nc),
        in_specs=[blk(0), blk(1), blk(2), pl.BlockSpec((c, LANES), lambda g, n: (n, ab_blk)), vec, vec],
        out_specs=(blk(0), pl.BlockSpec((None, None, r, r), lambda g, n: (g, n, 0, 0)),
                   pl.BlockSpec((None, None, LANES, hb * LANES), lambda g, n: (g, n, 0, 0))),
        out_shape=(jax.ShapeDtypeStruct((s, heads * LANES), F32), jax.ShapeDtypeStruct((ng, nc, r, r), F32),
                   jax.ShapeDtypeStruct((ng, nc, LANES, hb * LANES), F32)),
        scratch_shapes=[pltpu.VMEM((LANES, hb * LANES), F32)], semantics=("parallel", "arbitrary"),
        block_bytes=8 * r * LANES * 4 + 2 * r * r * 4 + 2 * LANES * hb * LANES * 4)(qkv, qkv, qkv, proj, alog, dtb)


def _gdn_bwd(qkv, proj, alog, dtb, t_all, s0_all, d_o, heads, name):
    s = qkv.shape[0]
    c = GDN_CHUNK
    nc = s // c
    ab_blk = proj.shape[1] // LANES - 1

    hb = _gdn_head_group(heads)
    ng = heads // hb
    r = hb * c

    def body(q_ref, k_ref, v_ref, ab_ref, alog_ref, dtb_ref, t_ref, s0_ref, do_ref,
             dq_ref, dk_ref, dv_ref, dgb_ref, ds_ref):
        grp, step = pl.program_id(0), pl.program_id(1)

        @pl.when(step == 0)
        def _():
            ds_ref[...] = jnp.zeros_like(ds_ref)

        q, k, v = _stack_heads(q_ref, hb), _stack_heads(k_ref, hb), _stack_heads(v_ref, hb)
        do = _stack_heads(do_ref, hb)
        tm = _gdn_group_terms(q, k, v, ab_ref[...], alog_ref[...], dtb_ref[...], grp * hb, hb, heads)
        ri, ci, same, eye = tm['ri'], tm['ci'], tm['same'], tm['eye']
        causal, strict, decay = tm['causal'], tm['strict'], tm['decay']
        egc, ekl, kb, vb, beta = tm['egc'], tm['ekl'], tm['kb'], tm['vb'], tm['beta']
        t_inv = t_ref[...]
        st = s0_ref[...]
        ds_next = ds_ref[...]
        kbg = kb * egc
        u = _dot(t_inv, vb)
        w = _dot(t_inv, kbg)
        qkm = _dot_nt(q, k)
        qk = jnp.where(causal, qkm * decay, 0.0)
        v_new = u - _diag_blocks(_dot(w, st), hb)
        qd = q * egc
        kd = k * ekl
        do_x = _expand_blocks(do, hb)

        dqd = _dot_nt(do_x, st)
        dqk = jnp.where(causal, _dot_nt(do, v_new), 0.0)
        dvn = _dot_tn(qk, do) + _diag_blocks(_dot(kd, ds_next), hb)
        dkd = _dot_nt(_expand_blocks(v_new, hb), ds_next)
        sd = jnp.sum(st * ds_next, axis=0, keepdims=True)
        dgl = jnp.concatenate([jnp.broadcast_to(jnp.sum(sd[:, i * LANES:(i + 1) * LANES], axis=1, keepdims=True),
                                                (c, 1)) for i in range(hb)], axis=0) * tm['e_last_col']
        dvn_x = _expand_blocks(dvn, hb)
        dw = -_dot_nt(dvn_x, st)
        ds_ref[...] = _dot_tn(qd, do_x) + tm['e_last_lanes'] * ds_next - _dot_tn(w, dvn_x)
        dt = _dot_nt(dvn, vb) + _dot_nt(dw, kbg)
        dvb = _dot_tn(t_inv, dvn)
        dkbg = _dot_tn(t_inv, dw)
        da_m = jnp.where(strict, -_dot_tn(t_inv, _dot_nt(dt, t_inv)), 0.0)
        dad = da_m * decay
        dkb = _dot(dad, k) + dkbg * egc
        dqkd = dqk * decay
        dq = _dot(dqkd, k) + dqd * egc
        dk = _dot_tn(dad, kb) + _dot_tn(dqkd, q) + dkd * ekl + dkb * beta
        e_mat = (da_m * tm['kk'] + dqk * qkm) * decay
        s_kd = jnp.sum(dkd * kd, axis=1, keepdims=True)
        s_kd_row = jnp.sum(jnp.where(eye, s_kd, 0.0), axis=0, keepdims=True)
        dgl = dgl + jnp.sum(jnp.where(same, s_kd_row, 0.0), axis=1, keepdims=True)
        col_sum = jnp.sum(e_mat, axis=0, keepdims=True)
        col_sum_c = jnp.sum(jnp.where(eye, col_sum, 0.0), axis=1, keepdims=True)
        dgc = (jnp.sum(e_mat, axis=1, keepdims=True) - col_sum_c + jnp.sum(dqd * qd, axis=1, keepdims=True)
               - s_kd + jnp.sum(dkbg * kbg, axis=1, keepdims=True))
        row_c = lax.broadcasted_iota(jnp.int32, (r, 1), 0)
        dgc = dgc + jnp.where((row_c & (c - 1)) == c - 1, dgl, 0.0)
        dgc_row = jnp.sum(jnp.where(eye, dgc, 0.0), axis=0, keepdims=True)
        dg = jnp.sum(jnp.where(same & (ci >= ri), dgc_row, 0.0), axis=1, keepdims=True)
        dbeta = jnp.sum(dkb * k, axis=1, keepdims=True) + jnp.sum(dvb * v, axis=1, keepdims=True)
        da_pre = dg * tm['neg_ea'] * _sigmoid(tm['pre'])
        db_pre = dbeta * beta * (1.0 - beta)
        lane = tm['lane']
        head_row = grp * hb + lax.shift_right_logical(row_c, _CHUNK_SHIFT)
        dgb = (jnp.where(lane == head_row, da_pre, 0.0) + jnp.where(lane == heads + head_row, db_pre, 0.0)
               + jnp.where(lane == 2 * heads + head_row, dg * tm['g'], 0.0))
        dvv = dvb * beta
        for i in range(hb):
            cols, rows = slice(i * LANES, (i + 1) * LANES), slice(i * c, (i + 1) * c)
            dq_ref[:, cols] = dq[rows, :]
            dk_ref[:, cols] = dk[rows, :]
            dv_ref[:, cols] = dvv[rows, :]
            dgb_ref[:, cols] = dgb[rows, :]

    def blk(off):
        return pl.BlockSpec((c, hb * LANES), lambda g, n: (nc - 1 - n, off * ng + g))

    vec = pl.BlockSpec((1, LANES), lambda g, n: (0, 0))
    gw = heads * LANES
    dq, dk, dv, dgb = _pcall(
        body, name=name, grid=(ng, nc),
        in_specs=[blk(0), blk(1), blk(2), pl.BlockSpec((c, LANES), lambda g, n: (nc - 1 - n, ab_blk)),
                  vec, vec, pl.BlockSpec((None, None, r, r), lambda g, n: (g, nc - 1 - n, 0, 0)),
                  pl.BlockSpec((None, None, LANES, hb * LANES), lambda g, n: (g, nc - 1 - n, 0, 0)), blk(0)],
        out_specs=(blk(0), blk(0), blk(0), blk(0)),
        out_shape=tuple(jax.ShapeDtypeStruct((s, gw), F32) for _ in range(4)),
        scratch_shapes=[pltpu.VMEM((LANES, hb * LANES), F32)], semantics=("parallel", "arbitrary"),
        block_bytes=12 * r * LANES * 4 + 2 * r * r * 4 + 2 * LANES * hb * LANES * 4)(
            qkv, qkv, qkv, proj, alog, dtb, t_all, s0_all, d_o)
    return dq, dk, dv, dgb


def _gdn_post_fwd(o, proj, z_col0, gain, name):
    s, gw = o.shape
    heads = gw // LANES
    tr = _tile(s, CONV_ROW_TILE, SUBLANES)

    def body(o_ref, z_ref, g_ref, y_ref):
        ov, zv = o_ref[...], z_ref[...]
        r = lax.rsqrt(jnp.mean(ov * ov, axis=-1, keepdims=True) + RMS_EPS)
        y_ref[...] = (ov * r * g_ref[...] * (zv * _sigmoid(zv))).astype(BF16)

    blk = pl.BlockSpec((tr, LANES), lambda i, h: (i, h))
    return _pcall(body, name=name, grid=(s // tr, heads),
                  in_specs=[blk, pl.BlockSpec((tr, LANES), lambda i, h: (i, z_col0 + h)),
                            pl.BlockSpec((1, LANES), lambda i, h: (0, 0))],
                  out_specs=blk, out_shape=jax.ShapeDtypeStruct((s, gw), BF16), semantics=("parallel", "parallel"),
                  block_bytes=3 * tr * LANES * 4)(o, proj, gain.reshape(1, LANES))


def _gdn_post_bwd(o, proj, z_col0, gain, dy, name):
    s, gw = o.shape
    heads = gw // LANES
    tr = _tile(s, CONV_ROW_TILE, SUBLANES)

    def body(o_ref, z_ref, g_ref, dy_ref, do_ref, dz_ref, dg_ref):
        ov, zv, gv, dyv = o_ref[...], z_ref[...], g_ref[...], dy_ref[...]
        r = lax.rsqrt(jnp.mean(ov * ov, axis=-1, keepdims=True) + RMS_EPS)
        nv = ov * r
        sg = _sigmoid(zv)
        sz = zv * sg
        dn = dyv * gv * sz
        do_ref[...] = r * (dn - nv * jnp.mean(dn * nv, axis=-1, keepdims=True))
        dz_ref[...] = dyv * nv * gv * (sg * (1.0 + zv * (1.0 - sg)))

        @pl.when((pl.program_id(0) == 0) & (pl.program_id(1) == 0))
        def _():
            dg_ref[...] = jnp.zeros_like(dg_ref)

        dg_ref[...] += jnp.sum(dyv * nv * sz, axis=0, keepdims=True)

    blk = pl.BlockSpec((tr, LANES), lambda i, h: (i, h))
    vec = pl.BlockSpec((1, LANES), lambda i, h: (0, 0))
    return _pcall(body, name=name, grid=(s // tr, heads),
                  in_specs=[blk, pl.BlockSpec((tr, LANES), lambda i, h: (i, z_col0 + h)), vec, blk],
                  out_specs=(blk, blk, vec),
                  out_shape=(jax.ShapeDtypeStruct((s, gw), F32), jax.ShapeDtypeStruct((s, gw), F32),
                             jax.ShapeDtypeStruct((1, LANES), F32)),
                  semantics=("arbitrary", "arbitrary"), block_bytes=6 * tr * LANES * 4)(
                      o, proj, gain.reshape(1, LANES), dy)


def _dab_reduce(dgb, name):
    s, gw = dgb.shape
    heads = gw // LANES
    tr = _tile(s, 512, SUBLANES)

    def body(d_ref, o_ref, cs_ref):
        acc = d_ref[:, 0:LANES]
        for h in range(1, heads):
            acc = acc + d_ref[:, h * LANES:(h + 1) * LANES]
        o_ref[...] = acc

        @pl.when(pl.program_id(0) == 0)
        def _():
            cs_ref[...] = jnp.zeros_like(cs_ref)

        cs_ref[...] += jnp.sum(acc, axis=0, keepdims=True)

    return _pcall(body, name=name, grid=(s // tr,), in_specs=[pl.BlockSpec((tr, gw), lambda i: (i, 0))],
                  out_specs=(pl.BlockSpec((tr, LANES), lambda i: (i, 0)), pl.BlockSpec((1, LANES), lambda i: (0, 0))),
                  out_shape=(jax.ShapeDtypeStruct((s, LANES), F32), jax.ShapeDtypeStruct((1, LANES), F32)),
                  semantics=("arbitrary",), block_bytes=tr * gw * 4)(dgb)


def _lru_gates(xc, wa, wx, ba, bx, lam):
    r = _sigmoid(_dot(xc, wa) + ba)
    ig = _sigmoid(_dot(xc, wx) + bx)
    sp = _softplus(-lam)
    log_a = -LRU_C * r * sp
    a = jnp.exp(log_a)
    e2 = jnp.exp(2.0 * log_a)
    mult = jnp.sqrt(jnp.maximum(1.0 - e2, 0.0))
    return r, ig, sp, a, e2, mult


def _lru_fwd(xc, proj, y_col0, wa, wx, ba, bx, lam, name):
    s, lw = xc.shape
    nb = lw // LANES
    tr = _tile(s, 256, SUBLANES)

    def body(xc_ref, y_ref, wa_ref, wx_ref, ba_ref, bx_ref, lam_ref, h_ref, o_ref, carry_ref):
        @pl.when(pl.program_id(1) == 0)
        def _():
            carry_ref[...] = jnp.zeros_like(carry_ref)

        xv = xc_ref[...]
        _, ig, _, a, _, mult = _lru_gates(xv, wa_ref[...], wx_ref[...], ba_ref[...], bx_ref[...], lam_ref[...])
        b = mult * (ig * xv)
        row = lax.broadcasted_iota(jnp.int32, (tr, LANES), 0)
        sh = 1
        while sh < tr:
            keep = row >= sh
            b = a * jnp.where(keep, pltpu.roll(b, sh, 0), 0.0) + b
            a = a * jnp.where(keep, pltpu.roll(a, sh, 0), 1.0)
            sh *= 2
        hv = a * carry_ref[0:1, :] + b
        h_ref[...] = hv
        carry_ref[...] = jnp.broadcast_to(hv[tr - 1:tr, :], carry_ref.shape)
        gy, _ = _gelu_and_grad(y_ref[...])
        o_ref[...] = (hv * gy).astype(BF16)

    blk = pl.BlockSpec((tr, LANES), lambda n, i: (i, n))
    wspec = pl.BlockSpec((None, LANES, LANES), lambda n, i: (n, 0, 0))
    vec = pl.BlockSpec((1, LANES), lambda n, i: (0, n))
    return _pcall(body, name=name, grid=(nb, s // tr),
                  in_specs=[blk, pl.BlockSpec((tr, LANES), lambda n, i: (i, y_col0 + n)), wspec, wspec, vec, vec, vec],
                  out_specs=(blk, blk),
                  out_shape=(jax.ShapeDtypeStruct((s, lw), F32), jax.ShapeDtypeStruct((s, lw), BF16)),
                  scratch_shapes=[pltpu.VMEM((SUBLANES, LANES), F32)], semantics=("parallel", "arbitrary"),
                  block_bytes=8 * tr * LANES * 4)(xc, proj, wa, wx, ba.reshape(1, lw), bx.reshape(1, lw),
                                                  lam.reshape(1, lw))


def _lru_bwd(d_out, xc, hseq, proj, y_col0, wa, wx, ba, bx, lam, name):
    s, lw = xc.shape
    nb = lw // LANES
    tr = _tile(s, 256, SUBLANES)
    per = tr // SUBLANES
    ni = s // tr
    nrow8 = s // SUBLANES

    def body(do_ref, xc_ref, xn_ref, h_ref, hp_ref, y_ref, wa_ref, wx_ref, ba_ref, bx_ref, lam_ref,
             dxc_ref, dy_ref, dwa_ref, dwx_ref, dba_ref, dbx_ref, dlam_ref, carry_ref):
        step = pl.program_id(1)
        tile = ni - 1 - step

        @pl.when(step == 0)
        def _():
            carry_ref[...] = jnp.zeros_like(carry_ref)
            dwa_ref[...] = jnp.zeros_like(dwa_ref)
            dwx_ref[...] = jnp.zeros_like(dwx_ref)
            dba_ref[...] = jnp.zeros_like(dba_ref)
            dbx_ref[...] = jnp.zeros_like(dbx_ref)
            dlam_ref[...] = jnp.zeros_like(dlam_ref)

        wav, wxv, bav, bxv, lamv = wa_ref[...], wx_ref[...], ba_ref[...], bx_ref[...], lam_ref[...]
        xv = xc_ref[...]
        r, ig, sp, a, e2, mult = _lru_gates(xv, wav, wxv, bav, bxv, lamv)
        a_next = _lru_gates(xn_ref[...], wav, wxv, bav, bxv, lamv)[3] * (tile < ni - 1).astype(F32)
        hv = h_ref[...]
        h_prev = _shift_down(hv, hp_ref[...] * (tile > 0).astype(F32), 1)
        yv = y_ref[...]
        gy, dgy = _gelu_and_grad(yv)
        dov = do_ref[...]
        dy_ref[...] = dov * hv * dgy
        coef = _shift_up(a, a_next, 1)
        bb = dov * gy
        row = lax.broadcasted_iota(jnp.int32, (tr, LANES), 0)
        sh = 1
        while sh < tr:
            keep = row < tr - sh
            bb = coef * jnp.where(keep, pltpu.roll(bb, tr - sh, 0), 0.0) + bb
            coef = coef * jnp.where(keep, pltpu.roll(coef, tr - sh, 0), 1.0)
            sh *= 2
        lam_t = coef * carry_ref[0:1, :] + bb
        carry_ref[...] = jnp.broadcast_to(lam_t[0:1, :], carry_ref.shape)
        d_a = lam_t * h_prev
        d_mult = lam_t * (ig * xv)
        d_ix = lam_t * mult
        d_la = d_a * a - d_mult * e2 / jnp.maximum(mult, 1e-30)
        d_r = d_la * (-LRU_C * sp)
        dlam_ref[...] += jnp.sum(d_la * (LRU_C * r) * _sigmoid(-lamv), axis=0, keepdims=True)
        d_pa = d_r * r * (1.0 - r)
        d_px = (d_ix * xv) * ig * (1.0 - ig)
        dxc_ref[...] = d_ix * ig + _dot_nt(d_pa, wav) + _dot_nt(d_px, wxv)
        dwa_ref[...] += _dot_tn(xv, d_pa)
        dwx_ref[...] += _dot_tn(xv, d_px)
        dba_ref[...] += jnp.sum(d_pa, axis=0, keepdims=True)
        dbx_ref[...] += jnp.sum(d_px, axis=0, keepdims=True)

    blk = pl.BlockSpec((tr, LANES), lambda n, i: (ni - 1 - i, n))
    nxt = pl.BlockSpec((SUBLANES, LANES), lambda n, i: (jnp.minimum((ni - i) * per, nrow8 - 1), n))
    prv = pl.BlockSpec((SUBLANES, LANES), lambda n, i: (jnp.maximum((ni - 1 - i) * per - 1, 0), n))
    wspec = pl.BlockSpec((None, LANES, LANES), lambda n, i: (n, 0, 0))
    vec = pl.BlockSpec((1, LANES), lambda n, i: (0, n))
    return _pcall(
        body, name=name, grid=(nb, ni),
        in_specs=[blk, blk, nxt, blk, prv, pl.BlockSpec((tr, LANES), lambda n, i: (ni - 1 - i, y_col0 + n)),
                  wspec, wspec, vec, vec, vec],
        out_specs=(blk, blk, wspec, wspec, vec, vec, vec),
        out_shape=(jax.ShapeDtypeStruct((s, lw), F32), jax.ShapeDtypeStruct((s, lw), F32),
                   jax.ShapeDtypeStruct((nb, LANES, LANES), F32), jax.ShapeDtypeStruct((nb, LANES, LANES), F32),
                   jax.ShapeDtypeStruct((1, lw), F32), jax.ShapeDtypeStruct((1, lw), F32),
                   jax.ShapeDtypeStruct((1, lw), F32)),
        scratch_shapes=[pltpu.VMEM((SUBLANES, LANES), F32)], semantics=("parallel", "arbitrary"),
        block_bytes=12 * tr * LANES * 4)(d_out, xc, xc, hseq, hseq, proj, wa, wx, ba.reshape(1, lw),
                                         bx.reshape(1, lw), lam.reshape(1, lw))


def _merge_fwd(proj, gg_col0, gl_col0, bg, bl, name):
    s, d = bg.shape
    tr, tc = _tile(s, 256, SUBLANES), _tile(d, 1024)
    cb = tc // LANES

    def body(gg_ref, gl_ref, bg_ref, bl_ref, o_ref):
        o_ref[...] = (_sigmoid(gg_ref[...]) * bg_ref[...] + _sigmoid(gl_ref[...]) * bl_ref[...]).astype(BF16)

    blk = pl.BlockSpec((tr, tc), lambda i, j: (i, j))
    return _pcall(body, name=name, grid=(s // tr, d // tc),
                  in_specs=[pl.BlockSpec((tr, tc), lambda i, j: (i, gg_col0 // cb + j)),
                            pl.BlockSpec((tr, tc), lambda i, j: (i, gl_col0 // cb + j)), blk, blk],
                  out_specs=blk, out_shape=jax.ShapeDtypeStruct((s, d), BF16), semantics=("parallel", "parallel"),
                  block_bytes=5 * tr * tc * 4)(proj, proj, bg, bl)


def _merge_bwd(proj, gg_col0, gl_col0, bg, bl, dm, name):
    s, d = bg.shape
    tr, tc = _tile(s, 256, SUBLANES), _tile(d, 1024)
    cb = tc // LANES

    def body(gg_ref, gl_ref, bg_ref, bl_ref, dm_ref, dgg_ref, dgl_ref, dbg_ref, dbl_ref):
        dmv = dm_ref[...]
        sg, sl = _sigmoid(gg_ref[...]), _sigmoid(gl_ref[...])
        dgg_ref[...] = (dmv * bg_ref[...] * sg * (1.0 - sg)).astype(BF16)
        dgl_ref[...] = (dmv * bl_ref[...] * sl * (1.0 - sl)).astype(BF16)
        dbg_ref[...] = (dmv * sg).astype(BF16)
        dbl_ref[...] = (dmv * sl).astype(BF16)

    blk = pl.BlockSpec((tr, tc), lambda i, j: (i, j))
    sh = jax.ShapeDtypeStruct((s, d), BF16)
    return _pcall(body, name=name, grid=(s // tr, d // tc),
                  in_specs=[pl.BlockSpec((tr, tc), lambda i, j: (i, gg_col0 // cb + j)),
                            pl.BlockSpec((tr, tc), lambda i, j: (i, gl_col0 // cb + j)), blk, blk, blk],
                  out_specs=(blk, blk, blk, blk), out_shape=(sh, sh, sh, sh), semantics=("parallel", "parallel"),
                  block_bytes=8 * tr * tc * 4)(proj, proj, bg, bl, dm)


def _sum_slots(slots, name):
    n, r, c = slots.shape
    tr = _tile(r, max(2 * SUBLANES, (1 << 19) // (c * 4)), 2 * SUBLANES)

    def body(s_ref, o_ref):
        acc = s_ref[0].astype(F32)
        for q in range(1, n):
            acc = acc + s_ref[q].astype(F32)
        o_ref[...] = acc

    return _pcall(body, name=name, grid=(r // tr,), in_specs=[pl.BlockSpec((n, tr, c), lambda i: (0, i, 0))],
                  out_specs=pl.BlockSpec((tr, c), lambda i: (i, 0)), out_shape=jax.ShapeDtypeStruct((r, c), F32),
                  semantics=("parallel",), block_bytes=(n + 1) * tr * c * 4)(slots)


def _adamw(w, g_parts, m, v, name):
    r, c = w.shape
    np_ = len(g_parts)
    tr = _tile(r, max(SUBLANES, (1 << 20) // (c * 4)), SUBLANES)
    c1 = 1.0 - ADAM_B1 ** ADAM_STEP
    c2 = 1.0 - ADAM_B2 ** ADAM_STEP

    def body(*refs):
        w_ref, m_ref, v_ref = refs[0], refs[1 + np_], refs[2 + np_]
        g_ref, d_ref, nm_ref, nv_ref = refs[3 + np_:]
        g = refs[1][...]
        for p in range(1, np_):
            g = g + refs[1 + p][...]
        nm = ADAM_B1 * m_ref[...] + (1.0 - ADAM_B1) * g
        nv = ADAM_B2 * v_ref[...] + (1.0 - ADAM_B2) * (g * g)
        g_ref[...] = g
        nm_ref[...] = nm
        nv_ref[...] = nv
        d_ref[...] = -ADAM_LR * ((nm / c1) / (jnp.sqrt(nv / c2) + ADAM_EPS) + ADAM_WD * w_ref[...])

    blk = pl.BlockSpec((tr, c), lambda i: (i, 0))
    sh = jax.ShapeDtypeStruct((r, c), F32)
    return _pcall(body, name=name, grid=(r // tr,), in_specs=[blk] * (3 + np_), out_specs=(blk,) * 4,
                  out_shape=(sh,) * 4, semantics=("parallel",), block_bytes=(7 + np_) * tr * c * 4)(
                      w, *g_parts, m, v)


def _pair_sum(core, mine, theirs, name):
    _, n, r, c = mine.shape
    tr = _tile(r, max(2 * SUBLANES, (1 << 19) // (c * 4)), 2 * SUBLANES)

    def body(core_ref, a_ref, b_ref, o_ref):
        o_ref[...] = (a_ref[...].astype(F32) + b_ref[...].astype(F32)).astype(BF16)

    return _pcall(body, name=name, grid=(n, r // tr),
                  in_specs=[pl.BlockSpec((None, None, tr, c), lambda q, i, core_ref: (core_ref[0], q, i, 0)),
                            pl.BlockSpec((None, tr, c), lambda q, i, core_ref: (q, i, 0))],
                  out_specs=pl.BlockSpec((None, tr, c), lambda q, i, core_ref: (q, i, 0)),
                  out_shape=jax.ShapeDtypeStruct((n, r, c), BF16), semantics=("parallel", "parallel"),
                  block_bytes=3 * tr * c * 4, scalar_prefetch=1)(core, mine, theirs)


def _sum_landed(chip, landed, own, name):
    n, r, c = landed.shape
    tr = _tile(r, max(2 * SUBLANES, (1 << 19) // (c * 4)), 2 * SUBLANES)

    def body(chip_ref, l_ref, o_ref, t_ref):
        acc = o_ref[...].astype(F32)
        for q in range(n):
            acc = acc + l_ref[q].astype(F32)
        t_ref[...] = acc

    return _pcall(body, name=name, grid=(r // tr,),
                  in_specs=[pl.BlockSpec((n, tr, c), lambda i, chip_ref: (0, i, 0)),
                            pl.BlockSpec((None, tr, c), lambda i, chip_ref: (chip_ref[0], i, 0))],
                  out_specs=pl.BlockSpec((tr, c), lambda i, chip_ref: (i, 0)),
                  out_shape=jax.ShapeDtypeStruct((r, c), F32), semantics=("parallel",),
                  block_bytes=(n + 3) * tr * c * 4, scalar_prefetch=1)(chip, landed, own)


def _adamw_quarters(core, w, g_mine, g_other, m, v, name):
    nl, nh, r, c = w.shape
    tr = _tile(r, max(SUBLANES, (1 << 19) // (c * 4)), SUBLANES)
    c1 = 1.0 - ADAM_B1 ** ADAM_STEP
    c2 = 1.0 - ADAM_B2 ** ADAM_STEP

    def body(core_ref, w_ref, *refs):
        g_refs, (m_ref, v_ref, g_ref, d_ref, nm_ref, nv_ref) = refs[:2 * nl], refs[2 * nl:]
        mine = pl.program_id(1) == core_ref[0]
        g = jnp.where(mine, g_refs[0][...], g_refs[nl][...])
        for l in range(1, nl):
            g = jnp.where(pl.program_id(0) == l, jnp.where(mine, g_refs[l][...], g_refs[nl + l][...]), g)
        nm = ADAM_B1 * m_ref[...] + (1.0 - ADAM_B1) * g
        nv = ADAM_B2 * v_ref[...] + (1.0 - ADAM_B2) * (g * g)
        g_ref[...] = g
        nm_ref[...] = nm
        nv_ref[...] = nv
        d_ref[...] = -ADAM_LR * ((nm / c1) / (jnp.sqrt(nv / c2) + ADAM_EPS) + ADAM_WD * w_ref[...])

    blk = pl.BlockSpec((None, None, tr, c), lambda l, hf, i, core_ref: (l, hf, i, 0))
    gblk = pl.BlockSpec((tr, c), lambda l, hf, i, core_ref: (i, 0))
    sh = jax.ShapeDtypeStruct(w.shape, F32)
    return _pcall(body, name=name, grid=(nl, nh, r // tr), in_specs=[blk] + [gblk] * (2 * nl) + [blk, blk],
                  out_specs=(blk,) * 4, out_shape=(sh,) * 4, semantics=("parallel", "parallel", "parallel"),
                  block_bytes=(7 + 2 * nl) * tr * c * 4, scalar_prefetch=1)(core, w, *g_mine, *g_other, m, v)


HBM_SPEC = pl.BlockSpec(memory_space=pltpu.HBM)


def _other_chips(x, y):
    return [(1 - x, y), (x, 1 - y), (1 - x, 1 - y)]


SEM_SPEC = pl.BlockSpec(memory_space=pltpu.SEMAPHORE)
DATAFLOW_EFFECT = pltpu.SideEffectType.DATAFLOW_SIDE_EFFECTING


def _split_start(name, bufs, n_copies, build):
    nb = len(bufs)

    def body(*refs):
        starts, _ = build(refs[:nb], refs[nb], refs[nb + 1])
        for cp in starts:
            cp.start()
        refs[-1][...] = jnp.zeros_like(refs[-1])

    out = pl.pallas_call(
        body, name=name,
        out_shape=(pltpu.SemaphoreType.DMA((n_copies,)), pltpu.SemaphoreType.DMA((n_copies,)),
                   *[pltpu.HBM(b.shape, b.dtype) for b in bufs], jax.ShapeDtypeStruct((SUBLANES, LANES), F32)),
        in_specs=[HBM_SPEC] * nb,
        out_specs=(SEM_SPEC, SEM_SPEC, *[HBM_SPEC] * nb, pl.BlockSpec(memory_space=pltpu.VMEM)),
        input_output_aliases={i: 2 + i for i in range(nb)},
        compiler_params=pltpu.CompilerParams(has_side_effects=DATAFLOW_EFFECT),
    )(*[pltpu.with_memory_space_constraint(b, pltpu.HBM) for b in bufs])
    return out[0], out[1], list(out[2:2 + nb]), out[2 + nb]


def _split_wait(name, send_sems, recv_sems, bufs, after, build):
    nb = len(bufs)

    def body(*refs):
        starts, waits = build(refs[:nb], refs[nb], refs[nb + 1])
        for cp in starts:
            cp.wait_send()
        for cp in waits:
            cp.wait_recv()

    out = pl.pallas_call(
        body, name=name, out_shape=tuple(pltpu.HBM(b.shape, b.dtype) for b in bufs),
        in_specs=[HBM_SPEC] * nb + [SEM_SPEC, SEM_SPEC, pl.BlockSpec(memory_space=pl.ANY)],
        out_specs=tuple([HBM_SPEC] * nb), input_output_aliases={i: i for i in range(nb)},
        compiler_params=pltpu.CompilerParams(has_side_effects=DATAFLOW_EFFECT),
    )(*bufs, send_sems, recv_sems, after)
    return list(out)


def _gather_ici_copies(nt, refs, send_sems, recv_sems):
    srcs, lands = refs[:nt], refs[nt:]
    x, y, c = lax.axis_index("x"), lax.axis_index("y"), lax.axis_index("c")
    me = 2 * x + y
    starts, waits = [], []
    for t in range(nt):
        for j, (px, py) in enumerate(_other_chips(x, y)):
            def copy(slot, t=t, j=j, px=px, py=py):
                return pltpu.make_async_remote_copy(
                    src_ref=srcs[t].at[c], dst_ref=lands[t].at[slot].at[c], send_sem=send_sems.at[3 * t + j],
                    recv_sem=recv_sems.at[3 * t + j], device_id=(px, py, c), device_id_type=pl.DeviceIdType.MESH)
            starts.append(copy(me))
            waits.append(copy(2 * px + py))
    return starts, waits


def _gather_d2d_copies(nt, refs, send_sems, recv_sems):
    x, y, c = lax.axis_index("x"), lax.axis_index("y"), lax.axis_index("c")
    starts, waits = [], []
    for t in range(nt):
        for j, (px, py) in enumerate(_other_chips(x, y)):
            def copy(half, t=t, j=j, px=px, py=py):
                place = refs[t].at[2 * px + py].at[half]
                return pltpu.make_async_remote_copy(
                    src_ref=place, dst_ref=place, send_sem=send_sems.at[3 * t + j], recv_sem=recv_sems.at[3 * t + j],
                    device_id=(x, y, 1 - c), device_id_type=pl.DeviceIdType.MESH)
            starts.append(copy(c))
            waits.append(copy(1 - c))
    return starts, waits


def _scatter_ici_copies(nt, refs, send_sems, recv_sems):
    srcs, lands = refs[:nt], refs[nt:]
    x, y, c = lax.axis_index("x"), lax.axis_index("y"), lax.axis_index("c")
    me = 2 * x + y
    starts, waits = [], []
    for t in range(nt):
        for j, (px, py) in enumerate(_other_chips(x, y)):
            def copy(slot, t=t, j=j, px=px, py=py):
                return pltpu.make_async_remote_copy(
                    src_ref=srcs[t].at[2 * px + py], dst_ref=lands[t].at[slot], send_sem=send_sems.at[3 * t + j],
                    recv_sem=recv_sems.at[3 * t + j], device_id=(px, py, c), device_id_type=pl.DeviceIdType.MESH)
            starts.append(copy(me))
            waits.append(copy(2 * px + py))
    return starts, waits


def _sibling_exchange(arrs, other_layer, name):
    n = len(arrs)

    def body(*refs):
        ins, outs = refs[:n], refs[n:2 * n]
        send_sems, recv_sems = refs[2 * n:]
        c = lax.axis_index("c")
        sib = (lax.axis_index("x"), lax.axis_index("y"), 1 - c)
        copies = [pltpu.make_async_remote_copy(src_ref=ins[t].at[1 - c] if other_layer else ins[t], dst_ref=outs[t],
                                               send_sem=send_sems.at[t], recv_sem=recv_sems.at[t], device_id=sib,
                                               device_id_type=pl.DeviceIdType.MESH) for t in range(n)]
        for cp in copies:
            cp.start()
        for cp in copies:
            cp.wait_recv()
        for cp in copies:
            cp.wait_send()

    return pl.pallas_call(
        body, name=name, in_specs=[HBM_SPEC] * n, out_specs=(HBM_SPEC,) * n,
        out_shape=tuple(jax.ShapeDtypeStruct(a.shape[1:] if other_layer else a.shape, a.dtype) for a in arrs),
        scratch_shapes=[pltpu.SemaphoreType.DMA((n,)), pltpu.SemaphoreType.DMA((n,))])(*arrs)


def _all_devices_gather(buf, name):
    def body(in_ref, out_ref, send_sems, recv_sems, local_sem):
        x, y, c = lax.axis_index("x"), lax.axis_index("y"), lax.axis_index("c")
        me = 4 * x + 2 * y + c

        def peer(mask):
            px = 1 - x if mask & 4 else x
            py = 1 - y if mask & 2 else y
            pc = 1 - c if mask & 1 else c
            return px, py, pc

        def remote(mask, dst_slot):
            return pltpu.make_async_remote_copy(
                src_ref=in_ref, dst_ref=out_ref.at[dst_slot], send_sem=send_sems.at[mask - 1],
                recv_sem=recv_sems.at[mask - 1], device_id=peer(mask), device_id_type=pl.DeviceIdType.MESH)

        lc = pltpu.make_async_copy(in_ref, out_ref.at[me], local_sem)
        lc.start()
        sends = [remote(mask, me) for mask in range(1, N_DEVICES)]
        for cp in sends:
            cp.start()
        for mask in range(1, N_DEVICES):
            px, py, pc = peer(mask)
            remote(mask, 4 * px + 2 * py + pc).wait_recv()
        for cp in sends:
            cp.wait_send()
        lc.wait()

    return pl.pallas_call(
        body, name=name, in_specs=[HBM_SPEC], out_specs=HBM_SPEC,
        out_shape=jax.ShapeDtypeStruct((N_DEVICES,) + buf.shape, buf.dtype),
        scratch_shapes=[pltpu.SemaphoreType.DMA((N_DEVICES - 1,)), pltpu.SemaphoreType.DMA((N_DEVICES - 1,)),
                        pltpu.SemaphoreType.DMA])(buf)


def _pad_lanes(vec):
    return jnp.pad(vec.astype(F32), (0, LANES - vec.shape[0])).reshape(1, LANES)


def _layer_fwd(x, wl, fetch, dm, tag):
    heads, gw, lw, d = dm['heads'], dm['gw'], dm['lw'], dm['d']
    h = _rms_fwd(x, wl['attn_norm'], f"rms1_fwd{tag}")
    wl.update(fetch('in', h))
    proj = _matmul(h, wl['w_in_p'], mode='nn', name=f"proj{tag}")
    alog, dtb = _pad_lanes(wl['gdn_a_log']), _pad_lanes(wl['gdn_dt_bias'])
    qkv = _gdn_pre_fwd(proj, wl['gdn_conv_w'], heads, f"gdn_pre_fwd{tag}")
    o, t_all, s0_all = _gdn_fwd(qkv, proj, alog, dtb, heads, f"gdn_fwd{tag}")
    o_gdn = _gdn_post_fwd(o, proj, dm['z_blk'], wl['gdn_norm'], f"gdn_post_fwd{tag}")
    xc = _conv_bias_fwd(proj, dm['xb_blk'], wl['lru_conv_w'], wl['lru_conv_b'], f"lru_conv_fwd{tag}")
    hseq, o_lru = _lru_fwd(xc, proj, dm['yb_blk'], wl['lru_w_a'], wl['lru_w_x'], wl['lru_b_a'], wl['lru_b_x'],
                           wl['lru_lambda'], f"lru_fwd{tag}")
    wl.update(fetch('mix', o))
    bg = _matmul(o_gdn, wl['w_branch_gdn'], mode='nn', name=f"branch_gdn{tag}")
    bl = _matmul(o_lru, wl['w_branch_lru'], mode='nn', name=f"branch_lru{tag}")
    merged = _merge_fwd(proj, dm['gg_blk'], dm['gl_blk'], bg, bl, f"merge_fwd{tag}")
    wl.update(fetch('mlp', bg))
    x_mid = _matmul(merged, wl['w_out'], mode='nn', add=x, name=f"out_proj{tag}")
    h2 = _rms_fwd(x_mid, wl['mlp_norm'], f"rms2_fwd{tag}")
    ur, act = _matmul(h2, wl['w_up'], mode='nn', epilogue='relu2', name=f"mlp_up{tag}")
    x_out = _matmul(act, wl['w_down'], mode='nn', add=x_mid, name=f"mlp_down{tag}")
    saved = dict(x=x, h=h, proj=proj, qkv=qkv, o=o, t_all=t_all, s0_all=s0_all, o_gdn=o_gdn, xc=xc, hseq=hseq,
                 o_lru=o_lru, bg=bg, bl=bl, merged=merged, x_mid=x_mid, h2=h2, ur=ur, act=act, alog=alog, dtb=dtb)
    return x_out, saved


def _layer_bwd(dx_out, dx_out_b, wl, sv, hook, dm, tag):
    heads, gw, lw, d = dm['heads'], dm['gw'], dm['lw'], dm['d']
    g = {}
    du = _matmul(dx_out_b, wl['w_down'], mode='nt', epilogue='mul2x', extra=sv['ur'], out_dtype=BF16,
                 name=f"d_mlp_act{tag}")
    def dw(n, lhs, rhs):
        return _matmul(lhs, rhs, mode='tn', out_dtype=BF16, shard_axis=BIG_SHARD_AXIS[n] - 1, name=f"d{n}{tag}")

    g['w_down'] = dw('w_down', sv['act'], dx_out_b)
    g['w_up'] = dw('w_up', sv['h2'], du)
    hook('mlp', g, wl, 'mlp_norm')
    dh2 = _matmul(du, wl['w_up'], mode='nt', name=f"d_h2{tag}")
    dx_mid, dx_mid_b, g['mlp_norm'] = _rms_bwd(sv['x_mid'], wl['mlp_norm'], dh2, dx_out, f"rms2_bwd{tag}")
    dmerged = _matmul(dx_mid_b, wl['w_out'], mode='nt', name=f"d_merged{tag}")
    g['w_out'] = dw('w_out', sv['merged'], dx_mid_b)
    dgg, dgl, dbg, dbl = _merge_bwd(sv['proj'], dm['gg_blk'], dm['gl_blk'], sv['bg'], sv['bl'], dmerged,
                                    f"merge_bwd{tag}")
    g['w_branch_gdn'] = dw('w_branch_gdn', sv['o_gdn'], dbg)
    g['w_branch_lru'] = dw('w_branch_lru', sv['o_lru'], dbl)
    hook('mix', g, wl, 'gdn_norm')
    do_gdn = _matmul(dbg, wl['w_branch_gdn'], mode='nt', name=f"d_o_gdn{tag}")
    do_lru = _matmul(dbl, wl['w_branch_lru'], mode='nt', name=f"d_o_lru{tag}")
    d_o, dz, dgn = _gdn_post_bwd(sv['o'], sv['proj'], dm['z_blk'], wl['gdn_norm'], do_gdn, f"gdn_post_bwd{tag}")
    g['gdn_norm'] = dgn.reshape(-1)
    dq, dk, dv, dgb = _gdn_bwd(sv['qkv'], sv['proj'], sv['alog'], sv['dtb'], sv['t_all'], sv['s0_all'], d_o, heads,
                               f"gdn_bwd{tag}")
    dconv = _gdn_pre_bwd(sv['proj'], wl['gdn_conv_w'], dq, dk, dv, heads, f"gdn_pre_bwd{tag}")
    dqkv, g['gdn_conv_w'], _ = _conv_bwd(dconv, sv['proj'], 0, wl['gdn_conv_w'], f"gdn_conv_bwd{tag}")
    dab, dab_sum = _dab_reduce(dgb, f"dab_reduce{tag}")
    g['gdn_dt_bias'] = dab_sum[0, :heads]
    g['gdn_a_log'] = dab_sum[0, 2 * heads:3 * heads]
    dxc, dyb, g['lru_w_a'], g['lru_w_x'], dba, dbx, dlam = _lru_bwd(
        do_lru, sv['xc'], sv['hseq'], sv['proj'], dm['yb_blk'], wl['lru_w_a'], wl['lru_w_x'], wl['lru_b_a'],
        wl['lru_b_x'], wl['lru_lambda'], f"lru_bwd{tag}")
    g['lru_b_a'], g['lru_b_x'], g['lru_lambda'] = dba.reshape(-1), dbx.reshape(-1), dlam.reshape(-1)
    dxb, g['lru_conv_w'], dcb = _conv_bwd(dxc, sv['proj'], dm['xb_blk'], wl['lru_conv_w'], f"lru_conv_bwd{tag}")
    g['lru_conv_b'] = dcb.reshape(-1)
    dproj = jnp.concatenate([dqkv.astype(BF16), dz.astype(BF16), dxb.astype(BF16), dyb.astype(BF16), dgg, dgl,
                             dab.astype(BF16)], axis=1)
    g['w_in_p'] = _matmul(sv['h'], dproj, mode='tn', out_dtype=BF16, name=f"dw_in{tag}")
    hook('in', g, wl, 'attn_norm')
    dh = _matmul(dproj, wl['w_in_p'], mode='nt', name=f"d_h{tag}")
    dx_in, dx_in_b, g['attn_norm'] = _rms_bwd(sv['x'], wl['attn_norm'], dh, dx_mid, f"rms1_bwd{tag}")
    g['attn_norm'] = g['attn_norm'].reshape(-1)
    g['mlp_norm'] = g['mlp_norm'].reshape(-1)
    return dx_in, dx_in_b, g


def _dims(d, heads, lw):
    gw = heads * LANES
    nab = 2 * heads
    blk = dict(z_blk=3 * heads, xb_blk=4 * heads, yb_blk=4 * heads + lw // LANES)
    gg0 = 4 * gw + 2 * lw
    return dict(d=d, heads=heads, gw=gw, lw=lw, nab=nab, gg_blk=gg0 // LANES, gl_blk=(gg0 + d) // LANES,
                main=gg0 + 2 * d, np=gg0 + 2 * d + LANES, **blk)


def _pad_w_in(w_in, dm):
    c0 = 4 * dm['gw']
    nab = dm['nab']
    return jnp.concatenate([w_in[:, :c0], w_in[:, c0 + nab:], w_in[:, c0:c0 + nab],
                            jnp.zeros((w_in.shape[0], LANES - nab), w_in.dtype)], axis=1)


def _unpad_w_in(gp, dm):
    c0 = 4 * dm['gw']
    nab = dm['nab']
    main = dm['main']
    return jnp.concatenate([gp[:, :c0], gp[:, main:main + nab], gp[:, c0:main]], axis=1)


def _local_step(x, target, layers, fetchers, hooks, final_norm, dm):
    saved = []
    cur = x
    for li, wl in enumerate(layers):
        cur, sv = _layer_fwd(cur, wl, fetchers[li], dm, f"_l{li}")
        saved.append(sv)
    loss_blk, dx, dx_b, dfin = _loss_head(cur, final_norm, target, "loss_head")
    grads = [None] * len(layers)
    for li in reversed(range(len(layers))):
        dx, dx_b, grads[li] = _layer_bwd(dx, dx_b, layers[li], saved[li], hooks[li], dm, f"_l{li}")
    return loss_blk[0, 0], dx, grads, dfin.reshape(-1)


def kernel(x, attn_norm, w_in, gdn_conv_w, gdn_a_log, gdn_dt_bias, gdn_norm, lru_conv_w, lru_conv_b, lru_w_a, lru_b_a, lru_w_x, lru_b_x, lru_lambda, w_branch_gdn, w_branch_lru, w_out, mlp_norm, w_up, w_down, final_norm, loss_target, m_attn_norm, m_w_in, m_gdn_conv_w, m_gdn_a_log, m_gdn_dt_bias, m_gdn_norm, m_lru_conv_w, m_lru_conv_b, m_lru_w_a, m_lru_b_a, m_lru_w_x, m_lru_b_x, m_lru_lambda, m_w_branch_gdn, m_w_branch_lru, m_w_out, m_mlp_norm, m_w_up, m_w_down, m_final_norm, v_attn_norm, v_w_in, v_gdn_conv_w, v_gdn_a_log, v_gdn_dt_bias, v_gdn_norm, v_lru_conv_w, v_lru_conv_b, v_lru_w_a, v_lru_b_a, v_lru_w_x, v_lru_b_x, v_lru_lambda, v_w_branch_gdn, v_w_branch_lru, v_w_out, v_mlp_norm, v_w_up, v_w_down, v_final_norm):
    w = dict(attn_norm=attn_norm, w_in=w_in, gdn_conv_w=gdn_conv_w, gdn_a_log=gdn_a_log, gdn_dt_bias=gdn_dt_bias,
             gdn_norm=gdn_norm, lru_conv_w=lru_conv_w, lru_conv_b=lru_conv_b, lru_w_a=lru_w_a, lru_b_a=lru_b_a,
             lru_w_x=lru_w_x, lru_b_x=lru_b_x, lru_lambda=lru_lambda, w_branch_gdn=w_branch_gdn,
             w_branch_lru=w_branch_lru, w_out=w_out, mlp_norm=mlp_norm, w_up=w_up, w_down=w_down,
             final_norm=final_norm)
    m = dict(attn_norm=m_attn_norm, w_in=m_w_in, gdn_conv_w=m_gdn_conv_w, gdn_a_log=m_gdn_a_log,
             gdn_dt_bias=m_gdn_dt_bias, gdn_norm=m_gdn_norm, lru_conv_w=m_lru_conv_w, lru_conv_b=m_lru_conv_b,
             lru_w_a=m_lru_w_a, lru_b_a=m_lru_b_a, lru_w_x=m_lru_w_x, lru_b_x=m_lru_b_x, lru_lambda=m_lru_lambda,
             w_branch_gdn=m_w_branch_gdn, w_branch_lru=m_w_branch_lru, w_out=m_w_out, mlp_norm=m_mlp_norm,
             w_up=m_w_up, w_down=m_w_down, final_norm=m_final_norm)
    v = dict(attn_norm=v_attn_norm, w_in=v_w_in, gdn_conv_w=v_gdn_conv_w, gdn_a_log=v_gdn_a_log,
             gdn_dt_bias=v_gdn_dt_bias, gdn_norm=v_gdn_norm, lru_conv_w=v_lru_conv_w, lru_conv_b=v_lru_conv_b,
             lru_w_a=v_lru_w_a, lru_b_a=v_lru_b_a, lru_w_x=v_lru_w_x, lru_b_x=v_lru_b_x, lru_lambda=v_lru_lambda,
             w_branch_gdn=v_w_branch_gdn, w_branch_lru=v_w_branch_lru, w_out=v_w_out, mlp_norm=v_mlp_norm,
             w_up=v_w_up, w_down=v_w_down, final_norm=v_final_norm)
    n_layers = attn_norm.shape[0]
    d = x.shape[-1]
    heads = gdn_a_log.shape[-1]
    lw = lru_conv_b.shape[-1]
    dm = _dims(d, heads, lw)
    big_names = list(BIG_SHARD_AXIS)
    conv_names = list(CONV_SHARD_AXIS)
    chip = 2 * lax.axis_index("x") + lax.axis_index("y")

    conv_flat = jnp.concatenate([w[n].reshape(-1) for n in conv_names])
    conv_rows = -(-conv_flat.shape[0] // (SUBLANES * LANES)) * SUBLANES
    conv_buf = jnp.pad(conv_flat, (0, conv_rows * LANES - conv_flat.shape[0])).reshape(conv_rows, LANES)
    conv_all = _all_devices_gather(conv_buf, "conv_allgather").reshape(N_CHIPS, 2, -1)[:, 0]
    conv_full, off = {}, 0
    for n in conv_names:
        shard = w[n]
        parts = conv_all[:, off:off + shard.size].reshape((N_CHIPS,) + shard.shape)
        conv_full[n] = jnp.concatenate([parts[q] for q in range(N_CHIPS)], axis=CONV_SHARD_AXIS[n])
        off += shard.size

    def start_gather(li, group):
        halves, lands = [], []
        for n in GATHER_GROUPS[group]:
            s = w[n][li].astype(BF16)
            hv = s.reshape((2, s.shape[0] // 2) + s.shape[1:])
            halves.append(hv)
            lands.append(lax.dynamic_update_index_in_dim(lax.empty((N_CHIPS,) + hv.shape, BF16), hv, chip, 0))
        nt = len(halves)
        return _split_start(f"wgather_{group}_l{li}_ici_start", halves + lands, 3 * nt,
                            functools.partial(_gather_ici_copies, nt))

    pending = {(li, group): start_gather(li, group) for li in range(n_layers) for group in GATHER_GROUPS}

    def make_fetch(li):
        def fetch(group, after):
            names = GATHER_GROUPS[group]
            nt = len(names)
            send, recv, bufs, _ = pending.pop((li, group))
            bufs = _split_wait(f"wgather_{group}_l{li}_ici_wait", send, recv, bufs, after,
                               functools.partial(_gather_ici_copies, nt))
            send, recv, lands, token = _split_start(f"wgather_{group}_l{li}_d2d_start", bufs[nt:], 3 * nt,
                                                    functools.partial(_gather_d2d_copies, nt))
            lands = _split_wait(f"wgather_{group}_l{li}_d2d_wait", send, recv, lands, token,
                                functools.partial(_gather_d2d_copies, nt))
            out = {}
            for n, land in zip(names, lands):
                slots = land.reshape((N_CHIPS, 2 * land.shape[2]) + land.shape[3:])
                out[n] = jnp.concatenate([slots[q] for q in range(N_CHIPS)], axis=BIG_SHARD_AXIS[n] - 1)
            if 'w_in' in out:
                out['w_in_p'] = _pad_w_in(out.pop('w_in'), dm)
            return out
        return fetch

    layers = []
    for li in range(n_layers):
        wl = {n: w[n][li] for n in SMALL_NAMES if n != 'final_norm' and n not in CONV_SHARD_AXIS}
        for n in conv_names:
            wl[n] = conv_full[n][li]
        layers.append(wl)
    layers[0]['attn_norm'] = layers[0]['attn_norm'] + sum(handle[3][0, 0] for handle in pending.values())

    core = lax.axis_index("c").astype(jnp.int32).reshape(1)
    chip_op = chip.astype(jnp.int32).reshape(1)
    in_flight, reduced = {}, {}

    def reduce_begin(group, li, g):
        names = GATHER_GROUPS[group]
        nt = len(names)
        contrib = []
        for n in names:
            if n != 'w_in':
                contrib.append(g[n])
                continue
            pieces = jnp.stack(jnp.split(_unpad_w_in(g['w_in_p'], dm), N_CHIPS, axis=BIG_SHARD_AXIS[n] - 1), axis=0)
            rows_half = pieces.shape[1] // 2
            contrib.append(jnp.swapaxes(pieces.reshape((N_CHIPS, 2, rows_half) + pieces.shape[2:]), 0, 1))
        theirs = _sibling_exchange(contrib, True, f"gsend_{group}_l{li}")
        sums = [_pair_sum(core, mine, th, f"gpair_{n}_l{li}") for n, mine, th in zip(names, contrib, theirs)]
        lands = [jnp.zeros(sm.shape, BF16) for sm in sums]
        send, recv, bufs, token = _split_start(f"gscatter_{group}_l{li}_start", sums + lands, 3 * nt,
                                               functools.partial(_scatter_ici_copies, nt))
        in_flight[(group, li)] = (send, recv, bufs)
        return token

    def reduce_end(group, li, after):
        names = GATHER_GROUPS[group]
        nt = len(names)
        send, recv, bufs = in_flight.pop((group, li))
        bufs = _split_wait(f"gscatter_{group}_l{li}_wait", send, recv, bufs, after,
                           functools.partial(_scatter_ici_copies, nt))
        totals = [_sum_landed(chip_op, land, own, f"gtotal_{n}_l{li}")
                  for n, own, land in zip(names, bufs[:nt], bufs[nt:])]
        others = _sibling_exchange(totals, False, f"gswap_{group}_l{li}")
        for n, mine, other in zip(names, totals, others):
            reduced[(n, li)] = (mine, other)

    def make_hook(li):
        def hook(group, g, wl, gain):
            wl[gain] = wl[gain] + reduce_begin(group, li, g)[0, 0]
        return hook

    loss_local, dx, grads, dfin = _local_step(x[0], loss_target[0], layers, [make_fetch(li) for li in range(n_layers)],
                                              [make_hook(li) for li in range(n_layers)], final_norm, dm)
    loss = lax.psum(loss_local, MESH_AXES)
    for li in reversed(range(n_layers)):
        for group in reversed(list(GATHER_GROUPS)):
            reduce_end(group, li, dx)

    small_g = {n: jnp.stack([grads[li][n] for li in range(n_layers)], axis=0)
               for n in SMALL_NAMES if n != 'final_norm'}
    small_g['final_norm'] = dfin
    flat = jnp.concatenate([small_g[n].reshape(-1) for n in SMALL_NAMES])
    n_flat = flat.shape[0]
    row_unit = 32 * SUBLANES
    rows = -(-n_flat // (row_unit * LANES)) * row_unit
    buf = jnp.pad(flat, (0, rows * LANES - n_flat)).reshape(rows, LANES)
    everyone = _all_devices_gather(buf, "small_grad_allgather")
    small_sum = _sum_slots(everyone, "small_grad_sum").reshape(-1)
    small_red = {}
    off = 0
    for n in SMALL_NAMES:
        size = small_g[n].size
        small_red[n] = small_sum[off:off + size].reshape(small_g[n].shape)
        off += size
    for n, ax in CONV_SHARD_AXIS.items():
        width = w[n].shape[ax]
        small_red[n] = lax.dynamic_slice_in_dim(small_red[n], chip * width, width, axis=ax)

    out_g, out_d, out_m, out_v = {}, {}, {}, {}
    for n in big_names:
        quarters = (n_layers, 2, w[n].shape[1] // 2, w[n].shape[2])
        res = _adamw_quarters(core, w[n].reshape(quarters), [reduced[(n, li)][0] for li in range(n_layers)],
                              [reduced[(n, li)][1] for li in range(n_layers)], m[n].reshape(quarters),
                              v[n].reshape(quarters), f"adamw_{n}")
        out_g[n], out_d[n], out_m[n], out_v[n] = (r.reshape(w[n].shape) for r in res)

    def pack(tree):
        fl = jnp.concatenate([tree[n].reshape(-1) for n in SMALL_NAMES])
        return jnp.pad(fl, (0, rows * LANES - fl.shape[0])).reshape(rows, LANES)

    res = _adamw(pack(w), [pack(small_red)], pack(m), pack(v), "adamw_small")
    for r, dst in zip(res, (out_g, out_d, out_m, out_v)):
        fl = r.reshape(-1)
        off = 0
        for n in SMALL_NAMES:
            dst[n] = fl[off:off + w[n].size].reshape(w[n].shape)
            off += w[n].size

    return (loss, dx[None], *[out_g[n] for n in WEIGHT_NAMES], *[out_d[n] for n in WEIGHT_NAMES],
            *[out_m[n] for n in WEIGHT_NAMES], *[out_v[n] for n in WEIGHT_NAMES])
```

```python
import functools

import jax
import jax.numpy as jnp
from jax import lax
from jax.experimental import pallas as pl
from jax.experimental.pallas import tpu as pltpu

F32 = jnp.float32
BF16 = jnp.bfloat16

LANES = 128
SUBLANES = 8
VMEM_BYTES = 64 * 1024 * 1024
GDN_CHUNK = 64
CONV_WIDTH = 4
CONV_ROW_TILE = 2048
RMS_EPS = 1e-6
L2_EPS = 1e-6
LRU_C = 8.0
ADAM_LR = 0.001
ADAM_B1 = 0.9
ADAM_B2 = 0.999
ADAM_EPS = 1e-08
ADAM_WD = 0.01
ADAM_STEP = 10
MESH_AXES = ("x", "y", "c")
N_CHIPS = 4
N_DEVICES = 8

INPUT_NAMES = ['x', 'attn_norm', 'w_in', 'gdn_conv_w', 'gdn_a_log', 'gdn_dt_bias', 'gdn_norm', 'lru_conv_w',
               'lru_conv_b', 'lru_w_a', 'lru_b_a', 'lru_w_x', 'lru_b_x', 'lru_lambda', 'w_branch_gdn',
               'w_branch_lru', 'w_out', 'mlp_norm', 'w_up', 'w_down', 'final_norm']
WEIGHT_NAMES = INPUT_NAMES[1:]
BIG_SHARD_AXIS = {'w_in': 2, 'w_branch_gdn': 2, 'w_branch_lru': 2, 'w_out': 1, 'w_up': 2, 'w_down': 1}
CONV_SHARD_AXIS = {'gdn_conv_w': 2, 'lru_conv_w': 2}
GATHER_GROUPS = {'in': ['w_in'], 'mix': ['w_branch_gdn', 'w_branch_lru', 'w_out'], 'mlp': ['w_up', 'w_down']}
SMALL_NAMES = [n for n in WEIGHT_NAMES if n not in BIG_SHARD_AXIS]


def _tile(n, target, unit=LANES):
    best = None
    t = unit
    while t <= min(n, target):
        if n % t == 0:
            best = t
        t += unit
    return n if best is None else best


def _vmem_limit(block_bytes):
    return int(min(max(3 * block_bytes + (8 << 20), 24 << 20), VMEM_BYTES - (8 << 20)))


def _nbytes(shape, dtype):
    n = 1
    for s in shape:
        n *= s
    return n * jnp.dtype(dtype).itemsize


def _pcall(body, *, name, grid, in_specs, out_specs, out_shape, scratch_shapes=(), semantics=None, block_bytes=0,
           scalar_prefetch=0):
    params = dict(vmem_limit_bytes=_vmem_limit(block_bytes))
    if semantics is not None:
        params['dimension_semantics'] = semantics
    if scalar_prefetch:
        grid_spec = pltpu.PrefetchScalarGridSpec(num_scalar_prefetch=scalar_prefetch, grid=grid, in_specs=in_specs,
                                                 out_specs=out_specs, scratch_shapes=list(scratch_shapes))
        return pl.pallas_call(body, name=name, grid_spec=grid_spec, out_shape=out_shape,
                              compiler_params=pltpu.CompilerParams(**params))
    return pl.pallas_call(body, name=name, grid=grid, in_specs=in_specs, out_specs=out_specs, out_shape=out_shape,
                          scratch_shapes=list(scratch_shapes), compiler_params=pltpu.CompilerParams(**params))


def _dot(a, b):
    return jnp.dot(a.astype(BF16), b.astype(BF16), preferred_element_type=F32)


def _dot_nt(a, b):
    return lax.dot_general(a.astype(BF16), b.astype(BF16), (((1,), (1,)), ((), ())), preferred_element_type=F32)


def _dot_tn(a, b):
    return lax.dot_general(a.astype(BF16), b.astype(BF16), (((0,), (0,)), ((), ())), preferred_element_type=F32)


def _sigmoid(x):
    return 1.0 / (1.0 + jnp.exp(-x))


def _log1p(u):
    return jnp.where(u < 1e-3, u * (1.0 - u * (0.5 - u * (1.0 / 3.0))), jnp.log(1.0 + u))


def _softplus(x):
    return jnp.maximum(x, 0.0) + _log1p(jnp.exp(-jnp.abs(x)))


_GELU_K = 0.7978845608028654


def _gelu_and_grad(x):
    inner = _GELU_K * (x + 0.044715 * x * x * x)
    th = jnp.tanh(inner)
    g = 0.5 * x * (1.0 + th)
    dg = 0.5 * (1.0 + th) + 0.5 * x * (1.0 - th * th) * _GELU_K * (1.0 + 3.0 * 0.044715 * x * x)
    return g, dg


MATMUL_TK_MAX = 3584
PROJ_COL_TILE = 1536


def _matmul(a, b, *, mode, name, out_dtype=F32, add=None, epilogue=None, extra=None, tm=512, tn=1024, tk=2048,
            shard_axis=None):
    if mode == 'nn':
        (m, k), (k2, n) = a.shape, b.shape
    elif mode == 'nt':
        (m, k), (n, k2) = a.shape, b.shape
    else:
        (k, m), (k2, n) = a.shape, b.shape
    assert k == k2, (a.shape, b.shape, mode)
    if shard_axis is not None:
        rows_half = (m // N_CHIPS if shard_axis == 0 else m) // 2
        cols = n // N_CHIPS if shard_axis == 1 else n
        tm, tn = _tile(rows_half, tm), _tile(cols, tn)
    else:
        tm, tn = _tile(m, tm), _tile(n, tn)
    tk = _tile(k, tk)
    if k // tk > 2 * (-(-k // MATMUL_TK_MAX)):
        tk = _tile(k, MATMUL_TK_MAX)
    nk = k // tk
    dims = {'nn': (((1,), (0,)), ((), ())), 'nt': (((1,), (1,)), ((), ())), 'tn': (((0,), (0,)), ((), ()))}[mode]
    a_bytes, b_bytes = _nbytes(a.shape, a.dtype), _nbytes(b.shape, b.dtype)
    rows_outer = nk > 1 or a_bytes + (m // tm) * b_bytes <= b_bytes + (n // tn) * a_bytes

    def ij(g0, g1):
        return (g0, g1) if rows_outer else (g1, g0)

    def spec(shape, pick):
        return pl.BlockSpec(shape, lambda g0, g1, kk: pick(*ij(g0, g1), kk))

    a_spec = spec((tk, tm), lambda i, j, kk: (kk, i)) if mode == 'tn' else spec((tm, tk), lambda i, j, kk: (i, kk))
    b_spec = spec((tn, tk), lambda i, j, kk: (j, kk)) if mode == 'nt' else spec((tk, tn), lambda i, j, kk: (kk, j))
    o_spec = spec((tm, tn), lambda i, j, kk: (i, j))
    operands, in_specs = [a, b], [a_spec, b_spec]
    if add is not None:
        operands.append(add)
        in_specs.append(o_spec)
    if extra is not None:
        operands.append(extra)
        in_specs.append(o_spec)
    n_in = len(operands)
    if epilogue == 'relu2':
        out_shape = (jax.ShapeDtypeStruct((m, n), BF16), jax.ShapeDtypeStruct((m, n), BF16))
        out_specs = (o_spec, o_spec)
    elif shard_axis is not None:
        assert add is None and extra is None
        rb, cb = rows_half // tm, cols // tn

        def shard_block(i, j, kk):
            if shard_axis == 0:
                return (i % (2 * rb)) // rb, i // (2 * rb), i % rb, j
            return i // rb, j // cb, i % rb, j % cb

        out_shape = jax.ShapeDtypeStruct((2, N_CHIPS, rows_half, cols), out_dtype)
        out_specs = spec((None, None, tm, tn), shard_block)
    else:
        out_shape = jax.ShapeDtypeStruct((m, n), out_dtype)
        out_specs = o_spec

    def body(*refs):
        a_ref, b_ref = refs[0], refs[1]
        outs = refs[n_in:n_in + n_out]

        def finish(p):
            if add is not None:
                p = p + refs[2][...]
            if epilogue == 'relu2':
                ur = jnp.maximum(p, 0.0)
                outs[0][...] = ur.astype(BF16)
                outs[1][...] = (ur * ur).astype(BF16)
            elif epilogue == 'mul2x':
                outs[0][...] = (p * 2.0 * refs[n_in - 1][...].astype(F32)).astype(out_dtype)
            else:
                outs[0][...] = p.astype(out_dtype)

        prod = lax.dot_general(a_ref[...].astype(BF16), b_ref[...].astype(BF16), dims, preferred_element_type=F32)
        if nk == 1:
            finish(prod)
            return
        acc_ref = refs[-1]
        kk = pl.program_id(2)

        @pl.when(kk == 0)
        def _():
            acc_ref[...] = prod

        @pl.when((kk > 0) & (kk < nk - 1))
        def _():
            acc_ref[...] += prod

        @pl.when(kk == nk - 1)
        def _():
            finish(acc_ref[...] + prod)

    n_out = 2 if epilogue == 'relu2' else 1
    bb = (_nbytes((tm, tk), a.dtype) + _nbytes((tk, tn), b.dtype) + 3 * _nbytes((tm, tn), F32))
    grid = (m // tm, n // tn, nk) if rows_outer else (n // tn, m // tm, nk)
    return _pcall(body, name=name, grid=grid, in_specs=in_specs, out_specs=out_specs, out_shape=out_shape,
                  scratch_shapes=[pltpu.VMEM((tm, tn), F32)] if nk > 1 else [],
                  semantics=("parallel", "parallel", "arbitrary"), block_bytes=bb)(*operands)


def _row_tile(s, d, target_bytes=1 << 20):
    return _tile(s, max(SUBLANES, target_bytes // (4 * d)), SUBLANES)


def _rms_fwd(x, gain, name):
    s, d = x.shape
    tr = _row_tile(s, d)

    def body(x_ref, g_ref, h_ref):
        xv = x_ref[...]
        r = lax.rsqrt(jnp.mean(xv * xv, axis=-1, keepdims=True) + RMS_EPS)
        h_ref[...] = (xv * r * g_ref[...]).astype(BF16)

    row = pl.BlockSpec((tr, d), lambda i: (i, 0))
    return _pcall(body, name=name, grid=(s // tr,), in_specs=[row, pl.BlockSpec((1, d), lambda i: (0, 0))],
                  out_specs=row, out_shape=jax.ShapeDtypeStruct((s, d), BF16), semantics=("parallel",),
                  block_bytes=2 * tr * d * 4)(x, gain.reshape(1, d))


def _rms_bwd(x, gain, dh, dres, name):
    s, d = x.shape
    tr = _row_tile(s, d, 1 << 19)

    def body(x_ref, g_ref, dh_ref, dres_ref, dx_ref, dxb_ref, dg_ref):
        xv = x_ref[...]
        r = lax.rsqrt(jnp.mean(xv * xv, axis=-1, keepdims=True) + RMS_EPS)
        xh = xv * r
        dhv = dh_ref[...]
        dxh = dhv * g_ref[...]
        dx = dres_ref[...] + r * (dxh - xh * jnp.mean(dxh * xh, axis=-1, keepdims=True))
        dx_ref[...] = dx
        dxb_ref[...] = dx.astype(BF16)

        @pl.when(pl.program_id(0) == 0)
        def _():
            dg_ref[...] = jnp.zeros_like(dg_ref)

        dg_ref[...] += jnp.sum(dhv * xh, axis=0, keepdims=True)

    row = pl.BlockSpec((tr, d), lambda i: (i, 0))
    vec = pl.BlockSpec((1, d), lambda i: (0, 0))
    return _pcall(body, name=name, grid=(s // tr,), in_specs=[row, vec, row, row], out_specs=(row, row, vec),
                  out_shape=(jax.ShapeDtypeStruct((s, d), F32), jax.ShapeDtypeStruct((s, d), BF16),
                             jax.ShapeDtypeStruct((1, d), F32)),
                  semantics=("arbitrary",), block_bytes=5 * tr * d * 4)(x, gain.reshape(1, d), dh, dres)


def _loss_head(x, gain, target, name):
    s, d = x.shape
    tr = _row_tile(s, d, 1 << 19)

    def body(x_ref, g_ref, t_ref, loss_ref, dx_ref, dxb_ref, dg_ref):
        xv = x_ref[...]
        r = lax.rsqrt(jnp.mean(xv * xv, axis=-1, keepdims=True) + RMS_EPS)
        xh = xv * r
        gv = g_ref[...]
        err = xh * gv - t_ref[...]
        dy = err * (1.0 / d)
        dxh = dy * gv
        dx = r * (dxh - xh * jnp.mean(dxh * xh, axis=-1, keepdims=True))
        dx_ref[...] = dx
        dxb_ref[...] = dx.astype(BF16)

        @pl.when(pl.program_id(0) == 0)
        def _():
            dg_ref[...] = jnp.zeros_like(dg_ref)
            loss_ref[...] = jnp.zeros_like(loss_ref)

        dg_ref[...] += jnp.sum(dy * xh, axis=0, keepdims=True)
        part = jnp.sum(jnp.sum(err * err, axis=-1, keepdims=True), axis=0, keepdims=True) * (0.5 / d)
        loss_ref[...] += jnp.broadcast_to(part, loss_ref.shape)

    row = pl.BlockSpec((tr, d), lambda i: (i, 0))
    vec = pl.BlockSpec((1, d), lambda i: (0, 0))
    lspec = pl.BlockSpec((SUBLANES, LANES), lambda i: (0, 0))
    return _pcall(body, name=name, grid=(s // tr,), in_specs=[row, vec, row], out_specs=(lspec, row, row, vec),
                  out_shape=(jax.ShapeDtypeStruct((SUBLANES, LANES), F32), jax.ShapeDtypeStruct((s, d), F32),
                             jax.ShapeDtypeStruct((s, d), BF16), jax.ShapeDtypeStruct((1, d), F32)),
                  semantics=("arbitrary",), block_bytes=4 * tr * d * 4)(x, gain.reshape(1, d), target)


def _shift_down(xc, xp, s):
    tr = xc.shape[0]
    r = pltpu.roll(xc, s, 0)
    p = pltpu.roll(xp, s, 0)
    row8 = lax.broadcasted_iota(jnp.int32, (SUBLANES, xc.shape[1]), 0)
    head = jnp.where(row8 < s, p, r[:SUBLANES])
    if tr == SUBLANES:
        return head
    return jnp.concatenate([head, r[SUBLANES:]], axis=0)


def _shift_up(yc, yn, s):
    tr = yc.shape[0]
    u = pltpu.roll(yc, tr - s, 0)
    n = pltpu.roll(yn, SUBLANES - s, 0)
    row8 = lax.broadcasted_iota(jnp.int32, (SUBLANES, yc.shape[1]), 0)
    tail = jnp.where(row8 >= SUBLANES - s, n, u[tr - SUBLANES:])
    if tr == SUBLANES:
        return tail
    return jnp.concatenate([u[:tr - SUBLANES], tail], axis=0)


def _conv_apply(xc, xp, w):
    y = xc * w[CONV_WIDTH - 1:CONV_WIDTH, :]
    for s in range(1, CONV_WIDTH):
        y = y + _shift_down(xc, xp, s) * w[CONV_WIDTH - 1 - s:CONV_WIDTH - s, :]
    return y


def _halo_specs(tr, col_of):
    per = tr // SUBLANES
    cur = pl.BlockSpec((tr, LANES), lambda j, i: (i, col_of(j)))
    prev = pl.BlockSpec((SUBLANES, LANES), lambda j, i: (jnp.maximum(i * per - 1, 0), col_of(j)))
    return cur, prev


def _conv_bias_fwd(x_arr, x_col0, w, bias, name):
    s = x_arr.shape[0]
    ncb = w.shape[1] // LANES
    tr = _tile(s, CONV_ROW_TILE, SUBLANES)

    def body(cur_ref, prev_ref, w_ref, b_ref, o_ref):
        i = pl.program_id(1)
        xp = prev_ref[...] * (i > 0).astype(F32)
        o_ref[...] = _conv_apply(cur_ref[...], xp, w_ref[...]) + b_ref[...]

    cur, prev = _halo_specs(tr, lambda j: x_col0 + j)
    return _pcall(body, name=name, grid=(ncb, s // tr),
                  in_specs=[cur, prev, pl.BlockSpec((CONV_WIDTH, LANES), lambda j, i: (0, j)),
                            pl.BlockSpec((1, LANES), lambda j, i: (0, j))],
                  out_specs=pl.BlockSpec((tr, LANES), lambda j, i: (i, j)),
                  out_shape=jax.ShapeDtypeStruct((s, w.shape[1]), F32), semantics=("parallel", "parallel"),
                  block_bytes=3 * tr * LANES * 4)(x_arr, x_arr, w, bias.reshape(1, -1))


def _conv_bwd(dy, x_arr, x_col0, w, name):
    s, c = dy.shape
    ncb = c // LANES
    tr = _tile(s, CONV_ROW_TILE, SUBLANES)
    per = tr // SUBLANES
    ni = s // tr

    def body(dy_ref, dyn_ref, cur_ref, prev_ref, w_ref, dx_ref, dw_ref, db_ref):
        i = pl.program_id(1)
        dyv = dy_ref[...]
        dn = dyn_ref[...] * (i < ni - 1).astype(F32)
        xc = cur_ref[...]
        xp = prev_ref[...] * (i > 0).astype(F32)
        wv = w_ref[...]

        @pl.when(i == 0)
        def _():
            dw_ref[...] = jnp.zeros_like(dw_ref)
            db_ref[...] = jnp.zeros_like(db_ref)

        dx = dyv * wv[CONV_WIDTH - 1:CONV_WIDTH, :]
        dw_ref[CONV_WIDTH - 1:CONV_WIDTH, :] += jnp.sum(dyv * xc, axis=0, keepdims=True)
        for sh in range(1, CONV_WIDTH):
            j = CONV_WIDTH - 1 - sh
            dx = dx + _shift_up(dyv, dn, sh) * wv[j:j + 1, :]
            dw_ref[j:j + 1, :] += jnp.sum(dyv * _shift_down(xc, xp, sh), axis=0, keepdims=True)
        dx_ref[...] = dx
        db_ref[...] += jnp.sum(dyv, axis=0, keepdims=True)

    cur, prev = _halo_specs(tr, lambda j: x_col0 + j)
    dcur = pl.BlockSpec((tr, LANES), lambda j, i: (i, j))
    dnext = pl.BlockSpec((SUBLANES, LANES), lambda j, i: (jnp.minimum((i + 1) * per, s // SUBLANES - 1), j))
    return _pcall(body, name=name, grid=(ncb, ni),
                  in_specs=[dcur, dnext, cur, prev, pl.BlockSpec((CONV_WIDTH, LANES), lambda j, i: (0, j))],
                  out_specs=(dcur, pl.BlockSpec((CONV_WIDTH, LANES), lambda j, i: (0, j)),
                             pl.BlockSpec((1, LANES), lambda j, i: (0, j))),
                  out_shape=(jax.ShapeDtypeStruct((s, c), F32), jax.ShapeDtypeStruct((CONV_WIDTH, c), F32),
                             jax.ShapeDtypeStruct((1, c), F32)),
                  semantics=("parallel", "arbitrary"), block_bytes=4 * tr * LANES * 4)(dy, dy, x_arr, x_arr, w)


def _gdn_pre_fwd(proj, conv_w, heads, name):
    s = proj.shape[0]
    ncb = conv_w.shape[1] // LANES
    tr = _tile(s, CONV_ROW_TILE, SUBLANES)
    qscale = float(LANES) ** -0.5

    def body(cur_ref, prev_ref, w_ref, o_ref):
        j, i = pl.program_id(0), pl.program_id(1)
        xp = prev_ref[...] * (i > 0).astype(F32)
        cv = _conv_apply(cur_ref[...], xp, w_ref[...])
        sv = cv * _sigmoid(cv)
        nrm = lax.rsqrt(jnp.sum(sv * sv, axis=-1, keepdims=True) + L2_EPS)
        scale = jnp.where(j < heads, qscale, 1.0)
        o_ref[...] = jnp.where(j < 2 * heads, sv * nrm * scale, sv)

    cur, prev = _halo_specs(tr, lambda j: j)
    return _pcall(body, name=name, grid=(ncb, s // tr),
                  in_specs=[cur, prev, pl.BlockSpec((CONV_WIDTH, LANES), lambda j, i: (0, j))],
                  out_specs=pl.BlockSpec((tr, LANES), lambda j, i: (i, j)),
                  out_shape=jax.ShapeDtypeStruct((s, conv_w.shape[1]), F32), semantics=("parallel", "parallel"),
                  block_bytes=3 * tr * LANES * 4)(proj, proj, conv_w)


def _gdn_pre_bwd(proj, conv_w, dq, dk, dv, heads, name):
    s = proj.shape[0]
    ncb = conv_w.shape[1] // LANES
    tr = _tile(s, CONV_ROW_TILE, SUBLANES)
    qscale = float(LANES) ** -0.5

    def body(cur_ref, prev_ref, w_ref, dq_ref, dk_ref, dv_ref, o_ref):
        j, i = pl.program_id(0), pl.program_id(1)
        xp = prev_ref[...] * (i > 0).astype(F32)
        cv = _conv_apply(cur_ref[...], xp, w_ref[...])
        sg = _sigmoid(cv)
        sv = cv * sg
        nrm = lax.rsqrt(jnp.sum(sv * sv, axis=-1, keepdims=True) + L2_EPS)
        dv = jnp.where(j < heads, dq_ref[...], jnp.where(j < 2 * heads, dk_ref[...], dv_ref[...]))
        scale = jnp.where(j < heads, qscale, 1.0)
        dsn = scale * nrm * (dv - sv * (nrm * nrm) * jnp.sum(dv * sv, axis=-1, keepdims=True))
        ds = jnp.where(j < 2 * heads, dsn, dv)
        o_ref[...] = ds * (sg * (1.0 + cv * (1.0 - sg)))

    cur, prev = _halo_specs(tr, lambda j: j)
    blk = pl.BlockSpec((tr, LANES), lambda j, i: (i, j))

    def part(k):
        return pl.BlockSpec((tr, LANES), lambda j, i: (i, jnp.clip(j - k * heads, 0, heads - 1)))

    return _pcall(body, name=name, grid=(ncb, s // tr),
                  in_specs=[cur, prev, pl.BlockSpec((CONV_WIDTH, LANES), lambda j, i: (0, j)), part(0), part(1),
                            part(2)],
                  out_specs=blk, out_shape=jax.ShapeDtypeStruct((s, conv_w.shape[1]), F32),
                  semantics=("parallel", "parallel"), block_bytes=6 * tr * LANES * 4)(proj, proj, conv_w, dq, dk, dv)


def _dot_split(a, b, dims=(((1,), (0,)), ((), ()))):
    a_hi, b_hi = a.astype(BF16), b.astype(BF16)
    a_lo, b_lo = (a - a_hi.astype(F32)).astype(BF16), (b - b_hi.astype(F32)).astype(BF16)

    def dot(u, v):
        return lax.dot_general(u, v, dims, preferred_element_type=F32)

    return dot(a_hi, b_hi) + dot(a_hi, b_lo) + dot(a_lo, b_hi)


def _tri_inverse(a_strict, block):
    n = a_strict.shape[0]
    ri = lax.broadcasted_iota(jnp.int32, (n, n), 0)
    ci = lax.broadcasted_iota(jnp.int32, (n, n), 1)
    same8 = (ri >> 3) == (ci >> 3)
    sel = jnp.where((lax.broadcasted_iota(jnp.int32, (n, LANES), 0) & 7)
                    == lax.broadcasted_iota(jnp.int32, (n, LANES), 1), 1.0, 0.0)
    a8 = _dot_split(jnp.where(same8, a_strict, 0.0), sel)
    t8 = sel
    r_in = lax.broadcasted_iota(jnp.int32, (n, 1), 0) & 7
    for j in range(SUBLANES - 1):
        row_j = jnp.broadcast_to(t8.reshape(n // SUBLANES, SUBLANES, LANES)[:, j:j + 1, :],
                                 (n // SUBLANES, SUBLANES, LANES)).reshape(n, LANES)
        t8 = t8 - jnp.where(r_in > j, a8[:, j:j + 1] * row_j, 0.0)
    t = jnp.where(same8, _dot_split(t8, sel, (((1,), (1,)), ((), ()))), 0.0)
    size = SUBLANES
    while size < block:
        sh = size.bit_length() - 1
        lower_left = (((ri >> (sh + 1)) == (ci >> (sh + 1))) & (((ri >> sh) & 1) == 1) & (((ci >> sh) & 1) == 0))
        t = t - _dot_split(t, _dot_split(jnp.where(lower_left, a_strict, 0.0), t))
        size *= 2
    return t


GDN_HEAD_GROUP = 4
_CHUNK_SHIFT = GDN_CHUNK.bit_length() - 1
_LANE_SHIFT = LANES.bit_length() - 1


def _stack_heads(ref, hb):
    return jnp.concatenate([ref[:, i * LANES:(i + 1) * LANES] for i in range(hb)], axis=0)


def _diag_blocks(x, hb):
    c = GDN_CHUNK
    return jnp.concatenate([x[i * c:(i + 1) * c, i * LANES:(i + 1) * LANES] for i in range(hb)], axis=0)


def _expand_blocks(y, hb):
    row_blk = lax.shift_right_logical(lax.broadcasted_iota(jnp.int32, y.shape, 0), _CHUNK_SHIFT)
    return jnp.concatenate([jnp.where(row_blk == j, y, 0.0) for j in range(hb)], axis=1)


def _gdn_group_terms(q, k, v, ab, alog, dtb, head0, hb, heads):
    c = GDN_CHUNK
    r = hb * c
    lane = lax.broadcasted_iota(jnp.int32, (1, LANES), 1)

    def column(lane0):
        return jnp.concatenate([jnp.sum(jnp.where(lane == lane0 + head0 + i, ab, 0.0), axis=1, keepdims=True)
                                for i in range(hb)], axis=0)

    def per_head(vec):
        return jnp.concatenate([jnp.broadcast_to(jnp.sum(jnp.where(lane == head0 + i, vec, 0.0), axis=1,
                                                         keepdims=True), (c, 1)) for i in range(hb)], axis=0)

    pre = column(0) + per_head(dtb)
    neg_ea = -jnp.exp(per_head(alog))
    g = neg_ea * _softplus(pre)
    beta = _sigmoid(column(heads))
    ri = lax.broadcasted_iota(jnp.int32, (r, r), 0)
    ci = lax.broadcasted_iota(jnp.int32, (r, r), 1)
    same = lax.shift_right_logical(ri, _CHUNK_SHIFT) == lax.shift_right_logical(ci, _CHUNK_SHIFT)
    eye = ri == ci
    causal = same & (ri >= ci)
    strict = same & (ri > ci)
    g_row = jnp.sum(jnp.where(eye, g, 0.0), axis=0, keepdims=True)
    gc_col = jnp.sum(jnp.where(causal, g_row, 0.0), axis=1, keepdims=True)
    gc_row = jnp.sum(jnp.where(same & (ri <= ci), g, 0.0), axis=0, keepdims=True)
    gl_col = jnp.sum(jnp.where(same, g_row, 0.0), axis=1, keepdims=True)
    decay = jnp.where(causal, jnp.exp(jnp.where(causal, gc_col - gc_row, 0.0)), 0.0)
    e_last_col = jnp.exp(gl_col)
    e_last_lanes = jnp.concatenate([jnp.broadcast_to(e_last_col[i * c:i * c + 1, :], (1, LANES))
                                    for i in range(hb)], axis=1)
    egc = jnp.exp(gc_col)
    ekl = jnp.exp(gl_col - gc_col)
    kb = k * beta
    vb = v * beta
    kk = _dot_nt(kb, k)
    a_strict = jnp.where(strict, kk * decay, 0.0)
    return dict(pre=pre, neg_ea=neg_ea, g=g, beta=beta, ri=ri, ci=ci, same=same, eye=eye, causal=causal,
                strict=strict, decay=decay, e_last_col=e_last_col, e_last_lanes=e_last_lanes, egc=egc, ekl=ekl,
                kb=kb, vb=vb, kk=kk, a_strict=a_strict, lane=lane)


def _gdn_head_group(heads):
    hb = GDN_HEAD_GROUP
    while heads % hb:
        hb //= 2
    return hb


def _gdn_fwd(qkv, proj, ab_blk, alog, dtb, heads, name):
    s = qkv.shape[0]
    c = GDN_CHUNK
    nc = s // c
    hb = _gdn_head_group(heads)
    ng = heads // hb
    r = hb * c

    def body(q_ref, k_ref, v_ref, ab_ref, alog_ref, dtb_ref, o_ref, t_ref, s0_ref, state_ref):
        grp, ch = pl.program_id(0), pl.program_id(1)

        @pl.when(ch == 0)
        def _():
            state_ref[...] = jnp.zeros_like(state_ref)

        q, k, v = _stack_heads(q_ref, hb), _stack_heads(k_ref, hb), _stack_heads(v_ref, hb)
        tm = _gdn_group_terms(q, k, v, ab_ref[...], alog_ref[...], dtb_ref[...], grp * hb, hb, heads)
        t_inv = _tri_inverse(tm['a_strict'], c)
        u = _dot(t_inv, tm['vb'])
        w = _dot(t_inv, tm['kb'] * tm['egc'])
        qk = jnp.where(tm['causal'], _dot_nt(q, k) * tm['decay'], 0.0)
        st = state_ref[...]
        v_new = u - _diag_blocks(_dot(w, st), hb)
        out = _diag_blocks(_dot(q * tm['egc'], st), hb) + _dot(qk, v_new)
        for i in range(hb):
            o_ref[:, i * LANES:(i + 1) * LANES] = out[i * c:(i + 1) * c, :]
        t_ref[...] = t_inv
        s0_ref[...] = st
        state_ref[...] = st * tm['e_last_lanes'] + _dot_tn(k * tm['ekl'], _expand_blocks(v_new, hb))

    def blk(off):
        return pl.BlockSpec((c, hb * LANES), lambda g, n: (n, off * ng + g))

    vec = pl.BlockSpec((1, LANES), lambda g, n: (0, 0))
    return _pcall(
        body, name=name, grid=(ng, nc),
        in_specs=[blk(0), blk(1), blk(2), pl.BlockSpec((c, LANES), lambda g, n: (n, ab_blk)), vec, vec],
        out_specs=(blk(0), pl.BlockSpec((None, None, r, r), lambda g, n: (g, n, 0, 0)),
                   pl.BlockSpec((None, None, LANES, hb * LANES), lambda g, n: (g, n, 0, 0))),
        out_shape=(jax.ShapeDtypeStruct((s, heads * LANES), F32), jax.ShapeDtypeStruct((ng, nc, r, r), F32),
                   jax.ShapeDtypeStruct((ng, nc, LANES, hb * LANES), F32)),
        scratch_shapes=[pltpu.VMEM((LANES, hb * LANES), F32)], semantics=("parallel", "arbitrary"),
        block_bytes=8 * r * LANES * 4 + 2 * r * r * 4 + 2 * LANES * hb * LANES * 4)(qkv, qkv, qkv, proj, alog, dtb)


def _gdn_bwd(qkv, proj, ab_blk, alog, dtb, t_all, s0_all, d_o, heads, name):
    s = qkv.shape[0]
    c = GDN_CHUNK
    nc = s // c
    hb = _gdn_head_group(heads)
    ng = heads // hb
    r = hb * c

    def body(q_ref, k_ref, v_ref, ab_ref, alog_ref, dtb_ref, t_ref, s0_ref, do_ref,
             dq_ref, dk_ref, dv_ref, dgb_ref, ds_ref):
        grp, step = pl.program_id(0), pl.program_id(1)

        @pl.when(step == 0)
        def _():
            ds_ref[...] = jnp.zeros_like(ds_ref)

        q, k, v = _stack_heads(q_ref, hb), _stack_heads(k_ref, hb), _stack_heads(v_ref, hb)
        do = _stack_heads(do_ref, hb)
        tm = _gdn_group_terms(q, k, v, ab_ref[...], alog_ref[...], dtb_ref[...], grp * hb, hb, heads)
        ri, ci, same, eye = tm['ri'], tm['ci'], tm['same'], tm['eye']
        causal, strict, decay = tm['causal'], tm['strict'], tm['decay']
        egc, ekl, kb, vb, beta = tm['egc'], tm['ekl'], tm['kb'], tm['vb'], tm['beta']
        t_inv = t_ref[...]
        st = s0_ref[...]
        ds_next = ds_ref[...]
        kbg = kb * egc
        u = _dot(t_inv, vb)
        w = _dot(t_inv, kbg)
        qkm = _dot_nt(q, k)
        qk = jnp.where(causal, qkm * decay, 0.0)
        v_new = u - _diag_blocks(_dot(w, st), hb)
        qd = q * egc
        kd = k * ekl
        do_x = _expand_blocks(do, hb)

        dqd = _dot_nt(do_x, st)
        dqk = jnp.where(causal, _dot_nt(do, v_new), 0.0)
        dvn = _dot_tn(qk, do) + _diag_blocks(_dot(kd, ds_next), hb)
        dkd = _dot_nt(_expand_blocks(v_new, hb), ds_next)
        sd = jnp.sum(st * ds_next, axis=0, keepdims=True)
        dgl = jnp.concatenate([jnp.broadcast_to(jnp.sum(sd[:, i * LANES:(i + 1) * LANES], axis=1, keepdims=True),
                                                (c, 1)) for i in range(hb)], axis=0) * tm['e_last_col']
        dvn_x = _expand_blocks(dvn, hb)
        dw = -_dot_nt(dvn_x, st)
        ds_ref[...] = _dot_tn(qd, do_x) + tm['e_last_lanes'] * ds_next - _dot_tn(w, dvn_x)
        dt = _dot_nt(dvn, vb) + _dot_nt(dw, kbg)
        dvb = _dot_tn(t_inv, dvn)
        dkbg = _dot_tn(t_inv, dw)
        da_m = jnp.where(strict, -_dot_tn(t_inv, _dot_nt(dt, t_inv)), 0.0)
        dad = da_m * decay
        dkb = _dot(dad, k) + dkbg * egc
        dqkd = dqk * decay
        dq = _dot(dqkd, k) + dqd * egc
        dk = _dot_tn(dad, kb) + _dot_tn(dqkd, q) + dkd * ekl + dkb * beta
        e_mat = (da_m * tm['kk'] + dqk * qkm) * decay
        s_kd = jnp.sum(dkd * kd, axis=1, keepdims=True)
        s_kd_row = jnp.sum(jnp.where(eye, s_kd, 0.0), axis=0, keepdims=True)
        dgl = dgl + jnp.sum(jnp.where(same, s_kd_row, 0.0), axis=1, keepdims=True)
        col_sum = jnp.sum(e_mat, axis=0, keepdims=True)
        col_sum_c = jnp.sum(jnp.where(eye, col_sum, 0.0), axis=1, keepdims=True)
        dgc = (jnp.sum(e_mat, axis=1, keepdims=True) - col_sum_c + jnp.sum(dqd * qd, axis=1, keepdims=True)
               - s_kd + jnp.sum(dkbg * kbg, axis=1, keepdims=True))
        row_c = lax.broadcasted_iota(jnp.int32, (r, 1), 0)
        dgc = dgc + jnp.where((row_c & (c - 1)) == c - 1, dgl, 0.0)
        dgc_row = jnp.sum(jnp.where(eye, dgc, 0.0), axis=0, keepdims=True)
        dg = jnp.sum(jnp.where(same & (ci >= ri), dgc_row, 0.0), axis=1, keepdims=True)
        dbeta = jnp.sum(dkb * k, axis=1, keepdims=True) + jnp.sum(dvb * v, axis=1, keepdims=True)
        da_pre = dg * tm['neg_ea'] * _sigmoid(tm['pre'])
        db_pre = dbeta * beta * (1.0 - beta)
        lane = tm['lane']
        head_row = grp * hb + lax.shift_right_logical(row_c, _CHUNK_SHIFT)
        dgb = (jnp.where(lane == head_row, da_pre, 0.0) + jnp.where(lane == heads + head_row, db_pre, 0.0)
               + jnp.where(lane == 2 * heads + head_row, dg * tm['g'], 0.0))
        dvv = dvb * beta
        for i in range(hb):
            cols, rows = slice(i * LANES, (i + 1) * LANES), slice(i * c, (i + 1) * c)
            dq_ref[:, cols] = dq[rows, :]
            dk_ref[:, cols] = dk[rows, :]
            dv_ref[:, cols] = dvv[rows, :]
            dgb_ref[:, cols] = dgb[rows, :]

    def blk(off):
        return pl.BlockSpec((c, hb * LANES), lambda g, n: (nc - 1 - n, off * ng + g))

    vec = pl.BlockSpec((1, LANES), lambda g, n: (0, 0))
    gw = heads * LANES
    dq, dk, dv, dgb = _pcall(
        body, name=name, grid=(ng, nc),
        in_specs=[blk(0), blk(1), blk(2), pl.BlockSpec((c, LANES), lambda g, n: (nc - 1 - n, ab_blk)),
                  vec, vec, pl.BlockSpec((None, None, r, r), lambda g, n: (g, nc - 1 - n, 0, 0)),
                  pl.BlockSpec((None, None, LANES, hb * LANES), lambda g, n: (g, nc - 1 - n, 0, 0)), blk(0)],
        out_specs=(blk(0), blk(0), blk(0), blk(0)),
        out_shape=tuple(jax.ShapeDtypeStruct((s, gw), F32) for _ in range(4)),
        scratch_shapes=[pltpu.VMEM((LANES, hb * LANES), F32)], semantics=("parallel", "arbitrary"),
        block_bytes=12 * r * LANES * 4 + 2 * r * r * 4 + 2 * LANES * hb * LANES * 4)(
            qkv, qkv, qkv, proj, alog, dtb, t_all, s0_all, d_o)
    return dq, dk, dv, dgb


def _gdn_post_fwd(o, proj, z_col0, gain, name):
    s, gw = o.shape
    heads = gw // LANES
    tr = _tile(s, CONV_ROW_TILE, SUBLANES)

    def body(o_ref, z_ref, g_ref, y_ref):
        ov, zv = o_ref[...], z_ref[...]
        r = lax.rsqrt(jnp.mean(ov * ov, axis=-1, keepdims=True) + RMS_EPS)
        y_ref[...] = (ov * r * g_ref[...] * (zv * _sigmoid(zv))).astype(BF16)

    blk = pl.BlockSpec((tr, LANES), lambda i, h: (i, h))
    return _pcall(body, name=name, grid=(s // tr, heads),
                  in_specs=[blk, pl.BlockSpec((tr, LANES), lambda i, h: (i, z_col0 + h)),
                            pl.BlockSpec((1, LANES), lambda i, h: (0, 0))],
                  out_specs=blk, out_shape=jax.ShapeDtypeStruct((s, gw), BF16), semantics=("parallel", "parallel"),
                  block_bytes=3 * tr * LANES * 4)(o, proj, gain.reshape(1, LANES))


def _gdn_post_bwd(o, proj, z_col0, gain, dy, name):
    s, gw = o.shape
    heads = gw // LANES
    tr = _tile(s, CONV_ROW_TILE, SUBLANES)

    def body(o_ref, z_ref, g_ref, dy_ref, do_ref, dz_ref, dg_ref):
        ov, zv, gv, dyv = o_ref[...], z_ref[...], g_ref[...], dy_ref[...]
        r = lax.rsqrt(jnp.mean(ov * ov, axis=-1, keepdims=True) + RMS_EPS)
        nv = ov * r
        sg = _sigmoid(zv)
        sz = zv * sg
        dn = dyv * gv * sz
        do_ref[...] = r * (dn - nv * jnp.mean(dn * nv, axis=-1, keepdims=True))
        dz_ref[...] = dyv * nv * gv * (sg * (1.0 + zv * (1.0 - sg)))

        @pl.when((pl.program_id(0) == 0) & (pl.program_id(1) == 0))
        def _():
            dg_ref[...] = jnp.zeros_like(dg_ref)

        dg_ref[...] += jnp.sum(dyv * nv * sz, axis=0, keepdims=True)

    blk = pl.BlockSpec((tr, LANES), lambda i, h: (i, h))
    vec = pl.BlockSpec((1, LANES), lambda i, h: (0, 0))
    return _pcall(body, name=name, grid=(s // tr, heads),
                  in_specs=[blk, pl.BlockSpec((tr, LANES), lambda i, h: (i, z_col0 + h)), vec, blk],
                  out_specs=(blk, blk, vec),
                  out_shape=(jax.ShapeDtypeStruct((s, gw), F32), jax.ShapeDtypeStruct((s, gw), F32),
                             jax.ShapeDtypeStruct((1, LANES), F32)),
                  semantics=("arbitrary", "arbitrary"), block_bytes=6 * tr * LANES * 4)(
                      o, proj, gain.reshape(1, LANES), dy)


def _dab_reduce(dgb, name):
    s, gw = dgb.shape
    heads = gw // LANES
    tr = _tile(s, 512, SUBLANES)

    def body(d_ref, o_ref, cs_ref):
        acc = d_ref[:, 0:LANES]
        for h in range(1, heads):
            acc = acc + d_ref[:, h * LANES:(h + 1) * LANES]
        o_ref[...] = acc

        @pl.when(pl.program_id(0) == 0)
        def _():
            cs_ref[...] = jnp.zeros_like(cs_ref)

        cs_ref[...] += jnp.sum(acc, axis=0, keepdims=True)

    return _pcall(body, name=name, grid=(s // tr,), in_specs=[pl.BlockSpec((tr, gw), lambda i: (i, 0))],
                  out_specs=(pl.BlockSpec((tr, LANES), lambda i: (i, 0)), pl.BlockSpec((1, LANES), lambda i: (0, 0))),
                  out_shape=(jax.ShapeDtypeStruct((s, LANES), F32), jax.ShapeDtypeStruct((1, LANES), F32)),
                  semantics=("arbitrary",), block_bytes=tr * gw * 4)(dgb)


def _lru_gates(xc, wa, wx, ba, bx, lam):
    r = _sigmoid(_dot(xc, wa) + ba)
    ig = _sigmoid(_dot(xc, wx) + bx)
    sp = _softplus(-lam)
    log_a = -LRU_C * r * sp
    a = jnp.exp(log_a)
    e2 = jnp.exp(2.0 * log_a)
    mult = jnp.sqrt(jnp.maximum(1.0 - e2, 0.0))
    return r, ig, sp, a, e2, mult


def _lru_fwd(xc, proj, y_col0, wa, wx, ba, bx, lam, name):
    s, lw = xc.shape
    nb = lw // LANES
    tr = _tile(s, 256, SUBLANES)

    def body(xc_ref, y_ref, wa_ref, wx_ref, ba_ref, bx_ref, lam_ref, h_ref, o_ref, carry_ref):
        @pl.when(pl.program_id(1) == 0)
        def _():
            carry_ref[...] = jnp.zeros_like(carry_ref)

        xv = xc_ref[...]
        _, ig, _, a, _, mult = _lru_gates(xv, wa_ref[...], wx_ref[...], ba_ref[...], bx_ref[...], lam_ref[...])
        b = mult * (ig * xv)
        row = lax.broadcasted_iota(jnp.int32, (tr, LANES), 0)
        sh = 1
        while sh < tr:
            keep = row >= sh
            b = a * jnp.where(keep, pltpu.roll(b, sh, 0), 0.0) + b
            a = a * jnp.where(keep, pltpu.roll(a, sh, 0), 1.0)
            sh *= 2
        hv = a * carry_ref[0:1, :] + b
        h_ref[...] = hv
        carry_ref[...] = jnp.broadcast_to(hv[tr - 1:tr, :], carry_ref.shape)
        gy, _ = _gelu_and_grad(y_ref[...])
        o_ref[...] = (hv * gy).astype(BF16)

    blk = pl.BlockSpec((tr, LANES), lambda n, i: (i, n))
    wspec = pl.BlockSpec((None, LANES, LANES), lambda n, i: (n, 0, 0))
    vec = pl.BlockSpec((1, LANES), lambda n, i: (0, n))
    return _pcall(body, name=name, grid=(nb, s // tr),
                  in_specs=[blk, pl.BlockSpec((tr, LANES), lambda n, i: (i, y_col0 + n)), wspec, wspec, vec, vec, vec],
                  out_specs=(blk, blk),
                  out_shape=(jax.ShapeDtypeStruct((s, lw), F32), jax.ShapeDtypeStruct((s, lw), BF16)),
                  scratch_shapes=[pltpu.VMEM((SUBLANES, LANES), F32)], semantics=("parallel", "arbitrary"),
                  block_bytes=8 * tr * LANES * 4)(xc, proj, wa, wx, ba.reshape(1, lw), bx.reshape(1, lw),
                                                  lam.reshape(1, lw))


def _lru_bwd(d_out, xc, hseq, proj, y_col0, wa, wx, ba, bx, lam, name):
    s, lw = xc.shape
    nb = lw // LANES
    tr = _tile(s, 256, SUBLANES)
    per = tr // SUBLANES
    ni = s // tr
    nrow8 = s // SUBLANES

    def body(do_ref, xc_ref, xn_ref, h_ref, hp_ref, y_ref, wa_ref, wx_ref, ba_ref, bx_ref, lam_ref,
             dxc_ref, dy_ref, dwa_ref, dwx_ref, dba_ref, dbx_ref, dlam_ref, carry_ref):
        step = pl.program_id(1)
        tile = ni - 1 - step

        @pl.when(step == 0)
        def _():
            carry_ref[...] = jnp.zeros_like(carry_ref)
            dwa_ref[...] = jnp.zeros_like(dwa_ref)
            dwx_ref[...] = jnp.zeros_like(dwx_ref)
            dba_ref[...] = jnp.zeros_like(dba_ref)
            dbx_ref[...] = jnp.zeros_like(dbx_ref)
            dlam_ref[...] = jnp.zeros_like(dlam_ref)

        wav, wxv, bav, bxv, lamv = wa_ref[...], wx_ref[...], ba_ref[...], bx_ref[...], lam_ref[...]
        xv = xc_ref[...]
        r, ig, sp, a, e2, mult = _lru_gates(xv, wav, wxv, bav, bxv, lamv)
        a_next = _lru_gates(xn_ref[...], wav, wxv, bav, bxv, lamv)[3] * (tile < ni - 1).astype(F32)
        hv = h_ref[...]
        h_prev = _shift_down(hv, hp_ref[...] * (tile > 0).astype(F32), 1)
        yv = y_ref[...]
        gy, dgy = _gelu_and_grad(yv)
        dov = do_ref[...]
        dy_ref[...] = dov * hv * dgy
        coef = _shift_up(a, a_next, 1)
        bb = dov * gy
        row = lax.broadcasted_iota(jnp.int32, (tr, LANES), 0)
        sh = 1
        while sh < tr:
            keep = row < tr - sh
            bb = coef * jnp.where(keep, pltpu.roll(bb, tr - sh, 0), 0.0) + bb
            coef = coef * jnp.where(keep, pltpu.roll(coef, tr - sh, 0), 1.0)
            sh *= 2
        lam_t = coef * carry_ref[0:1, :] + bb
        carry_ref[...] = jnp.broadcast_to(lam_t[0:1, :], carry_ref.shape)
        d_a = lam_t * h_prev
        d_mult = lam_t * (ig * xv)
        d_ix = lam_t * mult
        d_la = d_a * a - d_mult * e2 / jnp.maximum(mult, 1e-30)
        d_r = d_la * (-LRU_C * sp)
        dlam_ref[...] += jnp.sum(d_la * (LRU_C * r) * _sigmoid(-lamv), axis=0, keepdims=True)
        d_pa = d_r * r * (1.0 - r)
        d_px = (d_ix * xv) * ig * (1.0 - ig)
        dxc_ref[...] = d_ix * ig + _dot_nt(d_pa, wav) + _dot_nt(d_px, wxv)
        dwa_ref[...] += _dot_tn(xv, d_pa)
        dwx_ref[...] += _dot_tn(xv, d_px)
        dba_ref[...] += jnp.sum(d_pa, axis=0, keepdims=True)
        dbx_ref[...] += jnp.sum(d_px, axis=0, keepdims=True)

    blk = pl.BlockSpec((tr, LANES), lambda n, i: (ni - 1 - i, n))
    nxt = pl.BlockSpec((SUBLANES, LANES), lambda n, i: (jnp.minimum((ni - i) * per, nrow8 - 1), n))
    prv = pl.BlockSpec((SUBLANES, LANES), lambda n, i: (jnp.maximum((ni - 1 - i) * per - 1, 0), n))
    wspec = pl.BlockSpec((None, LANES, LANES), lambda n, i: (n, 0, 0))
    vec = pl.BlockSpec((1, LANES), lambda n, i: (0, n))
    return _pcall(
        body, name=name, grid=(nb, ni),
        in_specs=[blk, blk, nxt, blk, prv, pl.BlockSpec((tr, LANES), lambda n, i: (ni - 1 - i, y_col0 + n)),
                  wspec, wspec, vec, vec, vec],
        out_specs=(blk, blk, wspec, wspec, vec, vec, vec),
        out_shape=(jax.ShapeDtypeStruct((s, lw), F32), jax.ShapeDtypeStruct((s, lw), F32),
                   jax.ShapeDtypeStruct((nb, LANES, LANES), F32), jax.ShapeDtypeStruct((nb, LANES, LANES), F32),
                   jax.ShapeDtypeStruct((1, lw), F32), jax.ShapeDtypeStruct((1, lw), F32),
                   jax.ShapeDtypeStruct((1, lw), F32)),
        scratch_shapes=[pltpu.VMEM((SUBLANES, LANES), F32)], semantics=("parallel", "arbitrary"),
        block_bytes=12 * tr * LANES * 4)(d_out, xc, xc, hseq, hseq, proj, wa, wx, ba.reshape(1, lw),
                                         bx.reshape(1, lw), lam.reshape(1, lw))


def _merge_fwd(proj, gg_col0, gl_col0, bg, bl, name):
    s, d = bg.shape
    tr, tc = _tile(s, 256, SUBLANES), _tile(d, 1024)
    cb = tc // LANES

    def body(gg_ref, gl_ref, bg_ref, bl_ref, o_ref):
        o_ref[...] = (_sigmoid(gg_ref[...]) * bg_ref[...] + _sigmoid(gl_ref[...]) * bl_ref[...]).astype(BF16)

    blk = pl.BlockSpec((tr, tc), lambda i, j: (i, j))
    return _pcall(body, name=name, grid=(s // tr, d // tc),
                  in_specs=[pl.BlockSpec((tr, tc), lambda i, j: (i, gg_col0 // cb + j)),
                            pl.BlockSpec((tr, tc), lambda i, j: (i, gl_col0 // cb + j)), blk, blk],
                  out_specs=blk, out_shape=jax.ShapeDtypeStruct((s, d), BF16), semantics=("parallel", "parallel"),
                  block_bytes=5 * tr * tc * 4)(proj, proj, bg, bl)


def _merge_bwd(proj, gg_col0, gl_col0, bg, bl, dm, name):
    s, d = bg.shape
    tr, tc = _tile(s, 256, SUBLANES), _tile(d, 1024)
    cb = tc // LANES

    def body(gg_ref, gl_ref, bg_ref, bl_ref, dm_ref, dgg_ref, dgl_ref, dbg_ref, dbl_ref):
        dmv = dm_ref[...]
        sg, sl = _sigmoid(gg_ref[...]), _sigmoid(gl_ref[...])
        dgg_ref[...] = (dmv * bg_ref[...] * sg * (1.0 - sg)).astype(BF16)
        dgl_ref[...] = (dmv * bl_ref[...] * sl * (1.0 - sl)).astype(BF16)
        dbg_ref[...] = (dmv * sg).astype(BF16)
        dbl_ref[...] = (dmv * sl).astype(BF16)

    blk = pl.BlockSpec((tr, tc), lambda i, j: (i, j))
    sh = jax.ShapeDtypeStruct((s, d), BF16)
    return _pcall(body, name=name, grid=(s // tr, d // tc),
                  in_specs=[pl.BlockSpec((tr, tc), lambda i, j: (i, gg_col0 // cb + j)),
                            pl.BlockSpec((tr, tc), lambda i, j: (i, gl_col0 // cb + j)), blk, blk, blk],
                  out_specs=(blk, blk, blk, blk), out_shape=(sh, sh, sh, sh), semantics=("parallel", "parallel"),
                  block_bytes=8 * tr * tc * 4)(proj, proj, bg, bl, dm)


def _sum_slots(slots, name):
    n, r, c = slots.shape
    tr = _tile(r, max(2 * SUBLANES, (1 << 19) // (c * 4)), 2 * SUBLANES)

    def body(s_ref, o_ref):
        acc = s_ref[0].astype(F32)
        for q in range(1, n):
            acc = acc + s_ref[q].astype(F32)
        o_ref[...] = acc

    return _pcall(body, name=name, grid=(r // tr,), in_specs=[pl.BlockSpec((n, tr, c), lambda i: (0, i, 0))],
                  out_specs=pl.BlockSpec((tr, c), lambda i: (i, 0)), out_shape=jax.ShapeDtypeStruct((r, c), F32),
                  semantics=("parallel",), block_bytes=(n + 1) * tr * c * 4)(slots)


def _adamw(w, g_parts, m, v, name):
    r, c = w.shape
    np_ = len(g_parts)
    tr = _tile(r, max(SUBLANES, (1 << 20) // (c * 4)), SUBLANES)
    c1 = 1.0 - ADAM_B1 ** ADAM_STEP
    c2 = 1.0 - ADAM_B2 ** ADAM_STEP

    def body(*refs):
        w_ref, m_ref, v_ref = refs[0], refs[1 + np_], refs[2 + np_]
        g_ref, d_ref, nm_ref, nv_ref = refs[3 + np_:]
        g = refs[1][...]
        for p in range(1, np_):
            g = g + refs[1 + p][...]
        nm = ADAM_B1 * m_ref[...] + (1.0 - ADAM_B1) * g
        nv = ADAM_B2 * v_ref[...] + (1.0 - ADAM_B2) * (g * g)
        g_ref[...] = g
        nm_ref[...] = nm
        nv_ref[...] = nv
        d_ref[...] = -ADAM_LR * ((nm / c1) / (jnp.sqrt(nv / c2) + ADAM_EPS) + ADAM_WD * w_ref[...])

    blk = pl.BlockSpec((tr, c), lambda i: (i, 0))
    sh = jax.ShapeDtypeStruct((r, c), F32)
    return _pcall(body, name=name, grid=(r // tr,), in_specs=[blk] * (3 + np_), out_specs=(blk,) * 4,
                  out_shape=(sh,) * 4, semantics=("parallel",), block_bytes=(7 + np_) * tr * c * 4)(
                      w, *g_parts, m, v)


def _pair_sum(core, mine, theirs, name):
    _, n, r, c = mine.shape
    tr = _tile(r, max(2 * SUBLANES, (1 << 19) // (c * 4)), 2 * SUBLANES)

    def body(core_ref, a_ref, b_ref, o_ref):
        o_ref[...] = (a_ref[...].astype(F32) + b_ref[...].astype(F32)).astype(BF16)

    return _pcall(body, name=name, grid=(n, r // tr),
                  in_specs=[pl.BlockSpec((None, None, tr, c), lambda q, i, core_ref: (core_ref[0], q, i, 0)),
                            pl.BlockSpec((None, tr, c), lambda q, i, core_ref: (q, i, 0))],
                  out_specs=pl.BlockSpec((None, tr, c), lambda q, i, core_ref: (q, i, 0)),
                  out_shape=jax.ShapeDtypeStruct((n, r, c), BF16), semantics=("parallel", "parallel"),
                  block_bytes=3 * tr * c * 4, scalar_prefetch=1)(core, mine, theirs)


def _sum_landed(chip, landed, own, name):
    n, r, c = landed.shape
    tr = _tile(r, max(2 * SUBLANES, (1 << 19) // (c * 4)), 2 * SUBLANES)

    def body(chip_ref, l_ref, o_ref, t_ref):
        acc = o_ref[...].astype(F32)
        for q in range(n):
            acc = acc + l_ref[q].astype(F32)
        t_ref[...] = acc

    return _pcall(body, name=name, grid=(r // tr,),
                  in_specs=[pl.BlockSpec((n, tr, c), lambda i, chip_ref: (0, i, 0)),
                            pl.BlockSpec((None, tr, c), lambda i, chip_ref: (chip_ref[0], i, 0))],
                  out_specs=pl.BlockSpec((tr, c), lambda i, chip_ref: (i, 0)),
                  out_shape=jax.ShapeDtypeStruct((r, c), F32), semantics=("parallel",),
                  block_bytes=(n + 3) * tr * c * 4, scalar_prefetch=1)(chip, landed, own)


def _adamw_quarters(core, w, g_mine, g_other, m, v, after, name):
    nl, nh, r, c = w.shape
    tr = _tile(r, max(SUBLANES, (1 << 19) // (c * 4)), SUBLANES)
    c1 = 1.0 - ADAM_B1 ** ADAM_STEP
    c2 = 1.0 - ADAM_B2 ** ADAM_STEP

    def body(core_ref, w_ref, *refs):
        g_refs, (m_ref, v_ref, _, g_ref, d_ref, nm_ref, nv_ref) = refs[:2 * nl], refs[2 * nl:]
        mine = pl.program_id(1) == core_ref[0]
        g = jnp.where(mine, g_refs[0][...], g_refs[nl][...])
        for l in range(1, nl):
            g = jnp.where(pl.program_id(0) == l, jnp.where(mine, g_refs[l][...], g_refs[nl + l][...]), g)
        nm = ADAM_B1 * m_ref[...] + (1.0 - ADAM_B1) * g
        nv = ADAM_B2 * v_ref[...] + (1.0 - ADAM_B2) * (g * g)
        g_ref[...] = g
        nm_ref[...] = nm
        nv_ref[...] = nv
        d_ref[...] = -ADAM_LR * ((nm / c1) / (jnp.sqrt(nv / c2) + ADAM_EPS) + ADAM_WD * w_ref[...])

    blk = pl.BlockSpec((None, None, tr, c), lambda l, hf, i, core_ref: (l, hf, i, 0))
    gblk = pl.BlockSpec((tr, c), lambda l, hf, i, core_ref: (i, 0))
    sh = jax.ShapeDtypeStruct(w.shape, F32)
    return _pcall(body, name=name, grid=(nl, nh, r // tr),
                  in_specs=[blk] + [gblk] * (2 * nl) + [blk, blk, pl.BlockSpec(memory_space=pl.ANY)],
                  out_specs=(blk,) * 4, out_shape=(sh,) * 4, semantics=("parallel", "parallel", "parallel"),
                  block_bytes=(7 + 2 * nl) * tr * c * 4, scalar_prefetch=1)(core, w, *g_mine, *g_other, m, v, after)


HBM_SPEC = pl.BlockSpec(memory_space=pltpu.HBM)


def _other_chips(x, y):
    return [(1 - x, y), (x, 1 - y), (1 - x, 1 - y)]


SEM_SPEC = pl.BlockSpec(memory_space=pltpu.SEMAPHORE)
DATAFLOW_EFFECT = pltpu.SideEffectType.DATAFLOW_SIDE_EFFECTING


def _split_start(name, bufs, n_copies, build):
    nb = len(bufs)

    def body(*refs):
        starts, _ = build(refs[:nb], refs[nb], refs[nb + 1])
        for cp in starts:
            cp.start()
        refs[-1][...] = jnp.zeros_like(refs[-1])

    out = pl.pallas_call(
        body, name=name,
        out_shape=(pltpu.SemaphoreType.DMA((n_copies,)), pltpu.SemaphoreType.DMA((n_copies,)),
                   *[pltpu.HBM(b.shape, b.dtype) for b in bufs], jax.ShapeDtypeStruct((SUBLANES, LANES), F32)),
        in_specs=[HBM_SPEC] * nb,
        out_specs=(SEM_SPEC, SEM_SPEC, *[HBM_SPEC] * nb, pl.BlockSpec(memory_space=pltpu.VMEM)),
        input_output_aliases={i: 2 + i for i in range(nb)},
        compiler_params=pltpu.CompilerParams(has_side_effects=DATAFLOW_EFFECT),
    )(*[pltpu.with_memory_space_constraint(b, pltpu.HBM) for b in bufs])
    return out[0], out[1], list(out[2:2 + nb]), out[2 + nb]


def _split_wait(name, send_sems, recv_sems, bufs, after, build):
    nb = len(bufs)

    def body(*refs):
        starts, waits = build(refs[:nb], refs[nb], refs[nb + 1])
        for cp in starts:
            cp.wait_send()
        for cp in waits:
            cp.wait_recv()

    out = pl.pallas_call(
        body, name=name, out_shape=tuple(pltpu.HBM(b.shape, b.dtype) for b in bufs),
        in_specs=[HBM_SPEC] * nb + [SEM_SPEC, SEM_SPEC, pl.BlockSpec(memory_space=pl.ANY)],
        out_specs=tuple([HBM_SPEC] * nb), input_output_aliases={i: i for i in range(nb)},
        compiler_params=pltpu.CompilerParams(has_side_effects=DATAFLOW_EFFECT),
    )(*bufs, send_sems, recv_sems, after)
    return list(out)


def _gather_ici_copies(nt, refs, send_sems, recv_sems):
    srcs, lands = refs[:nt], refs[nt:]
    x, y, c = lax.axis_index("x"), lax.axis_index("y"), lax.axis_index("c")
    me = 2 * x + y
    starts, waits = [], []
    for t in range(nt):
        for j, (px, py) in enumerate(_other_chips(x, y)):
            def copy(slot, t=t, j=j, px=px, py=py):
                return pltpu.make_async_remote_copy(
                    src_ref=srcs[t].at[c], dst_ref=lands[t].at[slot].at[c], send_sem=send_sems.at[3 * t + j],
                    recv_sem=recv_sems.at[3 * t + j], device_id=(px, py, c), device_id_type=pl.DeviceIdType.MESH)
            starts.append(copy(me))
            waits.append(copy(2 * px + py))
    return starts, waits


def _gather_d2d_copies(nt, refs, send_sems, recv_sems):
    x, y, c = lax.axis_index("x"), lax.axis_index("y"), lax.axis_index("c")
    starts, waits = [], []
    for t in range(nt):
        for j, (px, py) in enumerate(_other_chips(x, y)):
            def copy(half, t=t, j=j, px=px, py=py):
                place = refs[t].at[2 * px + py].at[half]
                return pltpu.make_async_remote_copy(
                    src_ref=place, dst_ref=place, send_sem=send_sems.at[3 * t + j], recv_sem=recv_sems.at[3 * t + j],
                    device_id=(x, y, 1 - c), device_id_type=pl.DeviceIdType.MESH)
            starts.append(copy(c))
            waits.append(copy(1 - c))
    return starts, waits


def _scatter_ici_copies(nt, refs, send_sems, recv_sems):
    srcs, lands = refs[:nt], refs[nt:]
    x, y, c = lax.axis_index("x"), lax.axis_index("y"), lax.axis_index("c")
    me = 2 * x + y
    starts, waits = [], []
    for t in range(nt):
        for j, (px, py) in enumerate(_other_chips(x, y)):
            def copy(slot, t=t, j=j, px=px, py=py):
                return pltpu.make_async_remote_copy(
                    src_ref=srcs[t].at[2 * px + py], dst_ref=lands[t].at[slot], send_sem=send_sems.at[3 * t + j],
                    recv_sem=recv_sems.at[3 * t + j], device_id=(px, py, c), device_id_type=pl.DeviceIdType.MESH)
            starts.append(copy(me))
            waits.append(copy(2 * px + py))
    return starts, waits


def _sibling_exchange(arrs, other_layer, name):
    n = len(arrs)

    def body(*refs):
        ins, outs = refs[:n], refs[n:2 * n]
        send_sems, recv_sems = refs[2 * n:]
        c = lax.axis_index("c")
        sib = (lax.axis_index("x"), lax.axis_index("y"), 1 - c)
        copies = [pltpu.make_async_remote_copy(src_ref=ins[t].at[1 - c] if other_layer else ins[t], dst_ref=outs[t],
                                               send_sem=send_sems.at[t], recv_sem=recv_sems.at[t], device_id=sib,
                                               device_id_type=pl.DeviceIdType.MESH) for t in range(n)]
        for cp in copies:
            cp.start()
        for cp in copies:
            cp.wait_recv()
        for cp in copies:
            cp.wait_send()

    return pl.pallas_call(
        body, name=name, in_specs=[HBM_SPEC] * n, out_specs=(HBM_SPEC,) * n,
        out_shape=tuple(jax.ShapeDtypeStruct(a.shape[1:] if other_layer else a.shape, a.dtype) for a in arrs),
        scratch_shapes=[pltpu.SemaphoreType.DMA((n,)), pltpu.SemaphoreType.DMA((n,))])(*arrs)


def _all_devices_copies(refs, send_sems, recv_sems):
    src, land = refs
    x, y, c = lax.axis_index("x"), lax.axis_index("y"), lax.axis_index("c")
    me = 4 * x + 2 * y + c
    starts, waits = [], []
    for mask in range(1, N_DEVICES):
        px = 1 - x if mask & 4 else x
        py = 1 - y if mask & 2 else y
        pc = 1 - c if mask & 1 else c

        def copy(slot, mask=mask, px=px, py=py, pc=pc):
            return pltpu.make_async_remote_copy(
                src_ref=src, dst_ref=land.at[slot], send_sem=send_sems.at[mask - 1], recv_sem=recv_sems.at[mask - 1],
                device_id=(px, py, pc), device_id_type=pl.DeviceIdType.MESH)
        starts.append(copy(me))
        waits.append(copy(4 * px + 2 * py + pc))
    return starts, waits


def _all_devices_gather(buf, name):
    def body(in_ref, out_ref, send_sems, recv_sems, local_sem):
        x, y, c = lax.axis_index("x"), lax.axis_index("y"), lax.axis_index("c")
        me = 4 * x + 2 * y + c

        def peer(mask):
            px = 1 - x if mask & 4 else x
            py = 1 - y if mask & 2 else y
            pc = 1 - c if mask & 1 else c
            return px, py, pc

        def remote(mask, dst_slot):
            return pltpu.make_async_remote_copy(
                src_ref=in_ref, dst_ref=out_ref.at[dst_slot], send_sem=send_sems.at[mask - 1],
                recv_sem=recv_sems.at[mask - 1], device_id=peer(mask), device_id_type=pl.DeviceIdType.MESH)

        lc = pltpu.make_async_copy(in_ref, out_ref.at[me], local_sem)
        lc.start()
        sends = [remote(mask, me) for mask in range(1, N_DEVICES)]
        for cp in sends:
            cp.start()
        for mask in range(1, N_DEVICES):
            px, py, pc = peer(mask)
            remote(mask, 4 * px + 2 * py + pc).wait_recv()
        for cp in sends:
            cp.wait_send()
        lc.wait()

    return pl.pallas_call(
        body, name=name, in_specs=[HBM_SPEC], out_specs=HBM_SPEC,
        out_shape=jax.ShapeDtypeStruct((N_DEVICES,) + buf.shape, buf.dtype),
        scratch_shapes=[pltpu.SemaphoreType.DMA((N_DEVICES - 1,)), pltpu.SemaphoreType.DMA((N_DEVICES - 1,)),
                        pltpu.SemaphoreType.DMA])(buf)


def _pad_lanes(vec):
    return jnp.pad(vec.astype(F32), (0, LANES - vec.shape[0])).reshape(1, LANES)


def _layer_fwd(x, wl, fetch, dm, tag):
    heads, gw, lw, d = dm['heads'], dm['gw'], dm['lw'], dm['d']
    h = _rms_fwd(x, wl['attn_norm'], f"rms1_fwd{tag}")
    wl.update(fetch('in', h))
    proj = _matmul(h, wl['w_in_p'], mode='nn', tn=PROJ_COL_TILE, name=f"proj{tag}")
    alog, dtb = _pad_lanes(wl['gdn_a_log']), _pad_lanes(wl['gdn_dt_bias'])
    qkv = _gdn_pre_fwd(proj, wl['gdn_conv_w'], heads, f"gdn_pre_fwd{tag}")
    o, t_all, s0_all = _gdn_fwd(qkv, proj, dm['ab_blk'], alog, dtb, heads, f"gdn_fwd{tag}")
    o_gdn = _gdn_post_fwd(o, proj, dm['z_blk'], wl['gdn_norm'], f"gdn_post_fwd{tag}")
    xc = _conv_bias_fwd(proj, dm['xb_blk'], wl['lru_conv_w'], wl['lru_conv_b'], f"lru_conv_fwd{tag}")
    hseq, o_lru = _lru_fwd(xc, proj, dm['yb_blk'], wl['lru_w_a'], wl['lru_w_x'], wl['lru_b_a'], wl['lru_b_x'],
                           wl['lru_lambda'], f"lru_fwd{tag}")
    wl.update(fetch('mix', o))
    bg = _matmul(o_gdn, wl['w_branch_gdn'], mode='nn', name=f"branch_gdn{tag}")
    bl = _matmul(o_lru, wl['w_branch_lru'], mode='nn', name=f"branch_lru{tag}")
    merged = _merge_fwd(proj, dm['gg_blk'], dm['gl_blk'], bg, bl, f"merge_fwd{tag}")
    wl.update(fetch('mlp', bg))
    x_mid = _matmul(merged, wl['w_out'], mode='nn', add=x, name=f"out_proj{tag}")
    h2 = _rms_fwd(x_mid, wl['mlp_norm'], f"rms2_fwd{tag}")
    ur, act = _matmul(h2, wl['w_up'], mode='nn', epilogue='relu2', name=f"mlp_up{tag}")
    x_out = _matmul(act, wl['w_down'], mode='nn', add=x_mid, name=f"mlp_down{tag}")
    saved = dict(x=x, h=h, proj=proj, qkv=qkv, o=o, t_all=t_all, s0_all=s0_all, o_gdn=o_gdn, xc=xc, hseq=hseq,
                 o_lru=o_lru, bg=bg, bl=bl, merged=merged, x_mid=x_mid, h2=h2, ur=ur, act=act, alog=alog, dtb=dtb)
    return x_out, saved


def _layer_bwd(dx_out, dx_out_b, wl, sv, hook, dm, tag):
    heads, gw, lw, d = dm['heads'], dm['gw'], dm['lw'], dm['d']
    g = {}
    du = _matmul(dx_out_b, wl['w_down'], mode='nt', epilogue='mul2x', extra=sv['ur'], out_dtype=BF16,
                 name=f"d_mlp_act{tag}")
    def dw(n, lhs, rhs):
        return _matmul(lhs, rhs, mode='tn', out_dtype=BF16, shard_axis=BIG_SHARD_AXIS[n] - 1, name=f"d{n}{tag}")

    g['w_down'] = dw('w_down', sv['act'], dx_out_b)
    g['w_up'] = dw('w_up', sv['h2'], du)
    hook('mlp', g, wl, 'mlp_norm')
    dh2 = _matmul(du, wl['w_up'], mode='nt', name=f"d_h2{tag}")
    dx_mid, dx_mid_b, g['mlp_norm'] = _rms_bwd(sv['x_mid'], wl['mlp_norm'], dh2, dx_out, f"rms2_bwd{tag}")
    dmerged = _matmul(dx_mid_b, wl['w_out'], mode='nt', name=f"d_merged{tag}")
    g['w_out'] = dw('w_out', sv['merged'], dx_mid_b)
    dgg, dgl, dbg, dbl = _merge_bwd(sv['proj'], dm['gg_blk'], dm['gl_blk'], sv['bg'], sv['bl'], dmerged,
                                    f"merge_bwd{tag}")
    g['w_branch_gdn'] = dw('w_branch_gdn', sv['o_gdn'], dbg)
    g['w_branch_lru'] = dw('w_branch_lru', sv['o_lru'], dbl)
    hook('mix', g, wl, 'gdn_norm')
    do_gdn = _matmul(dbg, wl['w_branch_gdn'], mode='nt', name=f"d_o_gdn{tag}")
    do_lru = _matmul(dbl, wl['w_branch_lru'], mode='nt', name=f"d_o_lru{tag}")
    d_o, dz, dgn = _gdn_post_bwd(sv['o'], sv['proj'], dm['z_blk'], wl['gdn_norm'], do_gdn, f"gdn_post_bwd{tag}")
    g['gdn_norm'] = dgn.reshape(-1)
    dq, dk, dv, dgb = _gdn_bwd(sv['qkv'], sv['proj'], dm['ab_blk'], sv['alog'], sv['dtb'], sv['t_all'], sv['s0_all'], d_o, heads,
                               f"gdn_bwd{tag}")
    dconv = _gdn_pre_bwd(sv['proj'], wl['gdn_conv_w'], dq, dk, dv, heads, f"gdn_pre_bwd{tag}")
    dqkv, g['gdn_conv_w'], _ = _conv_bwd(dconv, sv['proj'], 0, wl['gdn_conv_w'], f"gdn_conv_bwd{tag}")
    dab, dab_sum = _dab_reduce(dgb, f"dab_reduce{tag}")
    g['gdn_dt_bias'] = dab_sum[0, :heads]
    g['gdn_a_log'] = dab_sum[0, 2 * heads:3 * heads]
    dxc, dyb, g['lru_w_a'], g['lru_w_x'], dba, dbx, dlam = _lru_bwd(
        do_lru, sv['xc'], sv['hseq'], sv['proj'], dm['yb_blk'], wl['lru_w_a'], wl['lru_w_x'], wl['lru_b_a'],
        wl['lru_b_x'], wl['lru_lambda'], f"lru_bwd{tag}")
    g['lru_b_a'], g['lru_b_x'], g['lru_lambda'] = dba.reshape(-1), dbx.reshape(-1), dlam.reshape(-1)
    dxb, g['lru_conv_w'], dcb = _conv_bwd(dxc, sv['proj'], dm['xb_blk'], wl['lru_conv_w'], f"lru_conv_bwd{tag}")
    g['lru_conv_b'] = dcb.reshape(-1)
    dproj = jnp.concatenate([dqkv.astype(BF16), dz.astype(BF16), dxb.astype(BF16), dyb.astype(BF16), dgg, dgl,
                             dab.astype(BF16), jnp.zeros((dab.shape[0], dm['np'] - dm['main'] - LANES), BF16)],
                            axis=1)
    g['w_in_p'] = _matmul(sv['h'], dproj, mode='tn', out_dtype=BF16, tn=PROJ_COL_TILE, name=f"dw_in{tag}")
    hook('in', g, wl, 'attn_norm')
    dh = _matmul(dproj, wl['w_in_p'], mode='nt', name=f"d_h{tag}")
    dx_in, dx_in_b, g['attn_norm'] = _rms_bwd(sv['x'], wl['attn_norm'], dh, dx_mid, f"rms1_bwd{tag}")
    g['attn_norm'] = g['attn_norm'].reshape(-1)
    g['mlp_norm'] = g['mlp_norm'].reshape(-1)
    return dx_in, dx_in_b, g


def _dims(d, heads, lw):
    gw = heads * LANES
    nab = 2 * heads
    blk = dict(z_blk=3 * heads, xb_blk=4 * heads, yb_blk=4 * heads + lw // LANES)
    gg0 = 4 * gw + 2 * lw
    main = gg0 + 2 * d
    return dict(d=d, heads=heads, gw=gw, lw=lw, nab=nab, gg_blk=gg0 // LANES, gl_blk=(gg0 + d) // LANES,
                main=main, ab_blk=main // LANES, np=-(-(main + LANES) // PROJ_COL_TILE) * PROJ_COL_TILE, **blk)


def _pad_w_in(w_in, dm):
    c0 = 4 * dm['gw']
    nab = dm['nab']
    return jnp.concatenate([w_in[:, :c0], w_in[:, c0 + nab:], w_in[:, c0:c0 + nab],
                            jnp.zeros((w_in.shape[0], dm['np'] - dm['main'] - nab), w_in.dtype)], axis=1)


def _unpad_w_in(gp, dm):
    c0 = 4 * dm['gw']
    nab = dm['nab']
    main = dm['main']
    return jnp.concatenate([gp[:, :c0], gp[:, main:main + nab], gp[:, c0:main]], axis=1)


def _local_step(x, target, layers, fetchers, hooks, final_norm, dm):
    saved = []
    cur = x
    for li, wl in enumerate(layers):
        cur, sv = _layer_fwd(cur, wl, fetchers[li], dm, f"_l{li}")
        saved.append(sv)
    loss_blk, dx, dx_b, dfin = _loss_head(cur, final_norm, target, "loss_head")
    grads = [None] * len(layers)
    for li in reversed(range(len(layers))):
        dx, dx_b, grads[li] = _layer_bwd(dx, dx_b, layers[li], saved[li], hooks[li], dm, f"_l{li}")
    return loss_blk[0, 0], dx, grads, dfin.reshape(-1)


def kernel(x, attn_norm, w_in, gdn_conv_w, gdn_a_log, gdn_dt_bias, gdn_norm, lru_conv_w, lru_conv_b, lru_w_a, lru_b_a, lru_w_x, lru_b_x, lru_lambda, w_branch_gdn, w_branch_lru, w_out, mlp_norm, w_up, w_down, final_norm, loss_target, m_attn_norm, m_w_in, m_gdn_conv_w, m_gdn_a_log, m_gdn_dt_bias, m_gdn_norm, m_lru_conv_w, m_lru_conv_b, m_lru_w_a, m_lru_b_a, m_lru_w_x, m_lru_b_x, m_lru_lambda, m_w_branch_gdn, m_w_branch_lru, m_w_out, m_mlp_norm, m_w_up, m_w_down, m_final_norm, v_attn_norm, v_w_in, v_gdn_conv_w, v_gdn_a_log, v_gdn_dt_bias, v_gdn_norm, v_lru_conv_w, v_lru_conv_b, v_lru_w_a, v_lru_b_a, v_lru_w_x, v_lru_b_x, v_lru_lambda, v_w_branch_gdn, v_w_branch_lru, v_w_out, v_mlp_norm, v_w_up, v_w_down, v_final_norm):
    w = dict(attn_norm=attn_norm, w_in=w_in, gdn_conv_w=gdn_conv_w, gdn_a_log=gdn_a_log, gdn_dt_bias=gdn_dt_bias,
             gdn_norm=gdn_norm, lru_conv_w=lru_conv_w, lru_conv_b=lru_conv_b, lru_w_a=lru_w_a, lru_b_a=lru_b_a,
             lru_w_x=lru_w_x, lru_b_x=lru_b_x, lru_lambda=lru_lambda, w_branch_gdn=w_branch_gdn,
             w_branch_lru=w_branch_lru, w_out=w_out, mlp_norm=mlp_norm, w_up=w_up, w_down=w_down,
             final_norm=final_norm)
    m = dict(attn_norm=m_attn_norm, w_in=m_w_in, gdn_conv_w=m_gdn_conv_w, gdn_a_log=m_gdn_a_log,
             gdn_dt_bias=m_gdn_dt_bias, gdn_norm=m_gdn_norm, lru_conv_w=m_lru_conv_w, lru_conv_b=m_lru_conv_b,
             lru_w_a=m_lru_w_a, lru_b_a=m_lru_b_a, lru_w_x=m_lru_w_x, lru_b_x=m_lru_b_x, lru_lambda=m_lru_lambda,
             w_branch_gdn=m_w_branch_gdn, w_branch_lru=m_w_branch_lru, w_out=m_w_out, mlp_norm=m_mlp_norm,
             w_up=m_w_up, w_down=m_w_down, final_norm=m_final_norm)
    v = dict(attn_norm=v_attn_norm, w_in=v_w_in, gdn_conv_w=v_gdn_conv_w, gdn_a_log=v_gdn_a_log,
             gdn_dt_bias=v_gdn_dt_bias, gdn_norm=v_gdn_norm, lru_conv_w=v_lru_conv_w, lru_conv_b=v_lru_conv_b,
             lru_w_a=v_lru_w_a, lru_b_a=v_lru_b_a, lru_w_x=v_lru_w_x, lru_b_x=v_lru_b_x, lru_lambda=v_lru_lambda,
             w_branch_gdn=v_w_branch_gdn, w_branch_lru=v_w_branch_lru, w_out=v_w_out, mlp_norm=v_mlp_norm,
             w_up=v_w_up, w_down=v_w_down, final_norm=v_final_norm)
    n_layers = attn_norm.shape[0]
    d = x.shape[-1]
    heads = gdn_a_log.shape[-1]
    lw = lru_conv_b.shape[-1]
    dm = _dims(d, heads, lw)
    big_names = list(BIG_SHARD_AXIS)
    conv_names = list(CONV_SHARD_AXIS)
    chip = 2 * lax.axis_index("x") + lax.axis_index("y")

    conv_flat = jnp.concatenate([w[n].reshape(-1) for n in conv_names])
    conv_rows = -(-conv_flat.shape[0] // (SUBLANES * LANES)) * SUBLANES
    conv_buf = jnp.pad(conv_flat, (0, conv_rows * LANES - conv_flat.shape[0])).reshape(conv_rows, LANES)
    conv_all = _all_devices_gather(conv_buf, "conv_allgather").reshape(N_CHIPS, 2, -1)[:, 0]
    conv_full, off = {}, 0
    for n in conv_names:
        shard = w[n]
        parts = conv_all[:, off:off + shard.size].reshape((N_CHIPS,) + shard.shape)
        conv_full[n] = jnp.concatenate([parts[q] for q in range(N_CHIPS)], axis=CONV_SHARD_AXIS[n])
        off += shard.size

    def start_gather(li, group):
        halves, lands = [], []
        for n in GATHER_GROUPS[group]:
            s = w[n][li].astype(BF16)
            hv = s.reshape((2, s.shape[0] // 2) + s.shape[1:])
            halves.append(hv)
            lands.append(lax.dynamic_update_index_in_dim(lax.empty((N_CHIPS,) + hv.shape, BF16), hv, chip, 0))
        nt = len(halves)
        return _split_start(f"wgather_{group}_l{li}_ici_start", halves + lands, 3 * nt,
                            functools.partial(_gather_ici_copies, nt))

    pending = {(li, group): start_gather(li, group) for li in range(n_layers) for group in GATHER_GROUPS}

    def make_fetch(li):
        def fetch(group, after):
            names = GATHER_GROUPS[group]
            nt = len(names)
            send, recv, bufs, _ = pending.pop((li, group))
            bufs = _split_wait(f"wgather_{group}_l{li}_ici_wait", send, recv, bufs, after,
                               functools.partial(_gather_ici_copies, nt))
            send, recv, lands, token = _split_start(f"wgather_{group}_l{li}_d2d_start", bufs[nt:], 3 * nt,
                                                    functools.partial(_gather_d2d_copies, nt))
            lands = _split_wait(f"wgather_{group}_l{li}_d2d_wait", send, recv, lands, token,
                                functools.partial(_gather_d2d_copies, nt))
            out = {}
            for n, land in zip(names, lands):
                slots = land.reshape((N_CHIPS, 2 * land.shape[2]) + land.shape[3:])
                out[n] = jnp.concatenate([slots[q] for q in range(N_CHIPS)], axis=BIG_SHARD_AXIS[n] - 1)
            if 'w_in' in out:
                out['w_in_p'] = _pad_w_in(out.pop('w_in'), dm)
            return out
        return fetch

    layers = []
    for li in range(n_layers):
        wl = {n: w[n][li] for n in SMALL_NAMES if n != 'final_norm' and n not in CONV_SHARD_AXIS}
        for n in conv_names:
            wl[n] = conv_full[n][li]
        layers.append(wl)
    layers[0]['attn_norm'] = layers[0]['attn_norm'] + sum(handle[3][0, 0] for handle in pending.values())

    core = lax.axis_index("c").astype(jnp.int32).reshape(1)
    chip_op = chip.astype(jnp.int32).reshape(1)
    in_flight, reduced = {}, {}

    def reduce_begin(group, li, g):
        names = GATHER_GROUPS[group]
        nt = len(names)
        contrib = []
        for n in names:
            if n != 'w_in':
                contrib.append(g[n])
                continue
            pieces = jnp.stack(jnp.split(_unpad_w_in(g['w_in_p'], dm), N_CHIPS, axis=BIG_SHARD_AXIS[n] - 1), axis=0)
            rows_half = pieces.shape[1] // 2
            contrib.append(jnp.swapaxes(pieces.reshape((N_CHIPS, 2, rows_half) + pieces.shape[2:]), 0, 1))
        theirs = _sibling_exchange(contrib, True, f"gsend_{group}_l{li}")
        sums = [_pair_sum(core, mine, th, f"gpair_{n}_l{li}") for n, mine, th in zip(names, contrib, theirs)]
        lands = [jnp.zeros(sm.shape, BF16) for sm in sums]
        send, recv, bufs, token = _split_start(f"gscatter_{group}_l{li}_start", sums + lands, 3 * nt,
                                               functools.partial(_scatter_ici_copies, nt))
        in_flight[(group, li)] = (send, recv, bufs)
        return token

    def reduce_end(group, li, after):
        names = GATHER_GROUPS[group]
        nt = len(names)
        send, recv, bufs = in_flight.pop((group, li))
        bufs = _split_wait(f"gscatter_{group}_l{li}_wait", send, recv, bufs, after,
                           functools.partial(_scatter_ici_copies, nt))
        totals = [_sum_landed(chip_op, land, own, f"gtotal_{n}_l{li}")
                  for n, own, land in zip(names, bufs[:nt], bufs[nt:])]
        others = _sibling_exchange(totals, False, f"gswap_{group}_l{li}")
        for n, mine, other in zip(names, totals, others):
            reduced[(n, li)] = (mine, other)

    def make_hook(li):
        def hook(group, g, wl, gain):
            wl[gain] = wl[gain] + reduce_begin(group, li, g)[0, 0]
        return hook

    loss_local, dx, grads, dfin = _local_step(x[0], loss_target[0], layers, [make_fetch(li) for li in range(n_layers)],
                                              [make_hook(li) for li in range(n_layers)], final_norm, dm)
    loss = lax.psum(loss_local, MESH_AXES)

    small_g = {n: jnp.stack([grads[li][n] for li in range(n_layers)], axis=0)
               for n in SMALL_NAMES if n != 'final_norm'}
    small_g['final_norm'] = dfin
    flat = jnp.concatenate([small_g[n].reshape(-1) for n in SMALL_NAMES])
    n_flat = flat.shape[0]
    row_unit = 32 * SUBLANES
    rows = -(-n_flat // (row_unit * LANES)) * row_unit
    buf = jnp.pad(flat, (0, rows * LANES - n_flat)).reshape(rows, LANES)
    device = 2 * chip + lax.axis_index("c")
    slots = lax.dynamic_update_index_in_dim(lax.empty((N_DEVICES, rows, LANES), F32), buf, device, 0)
    small_send, small_recv, small_bufs, small_token = _split_start("small_grad_start", [buf, slots], N_DEVICES - 1,
                                                                   _all_devices_copies)

    out_g, out_d, out_m, out_v = {}, {}, {}, {}

    def adamw_big(n, after):
        quarters = (n_layers, 2, w[n].shape[1] // 2, w[n].shape[2])
        res = _adamw_quarters(core, w[n].reshape(quarters), [reduced[(n, li)][0] for li in range(n_layers)],
                              [reduced[(n, li)][1] for li in range(n_layers)], m[n].reshape(quarters),
                              v[n].reshape(quarters), after, f"adamw_{n}")
        out_g[n], out_d[n], out_m[n], out_v[n] = (r.reshape(w[n].shape) for r in res)
        return res[1]

    early = [group for group in GATHER_GROUPS if group != 'in']
    for li in reversed(range(n_layers)):
        for group in reversed(early):
            reduce_end(group, li, dx)
    last = small_token
    for group in early:
        for n in GATHER_GROUPS[group]:
            last = adamw_big(n, last)
    for li in reversed(range(n_layers)):
        reduce_end('in', li, last)
    last = adamw_big('w_in', last)
    small_bufs = _split_wait("small_grad_wait", small_send, small_recv, small_bufs, last, _all_devices_copies)
    small_sum = _sum_slots(small_bufs[1], "small_grad_sum").reshape(-1)
    small_red = {}
    off = 0
    for n in SMALL_NAMES:
        size = small_g[n].size
        small_red[n] = small_sum[off:off + size].reshape(small_g[n].shape)
        off += size
    for n, ax in CONV_SHARD_AXIS.items():
        width = w[n].shape[ax]
        small_red[n] = lax.dynamic_slice_in_dim(small_red[n], chip * width, width, axis=ax)

    def pack(tree):
        fl = jnp.concatenate([tree[n].reshape(-1) for n in SMALL_NAMES])
        return jnp.pad(fl, (0, rows * LANES - fl.shape[0])).reshape(rows, LANES)

    res = _adamw(pack(w), [pack(small_red)], pack(m), pack(v), "adamw_small")
    for r, dst in zip(res, (out_g, out_d, out_m, out_v)):
        fl = r.reshape(-1)
        off = 0
        for n in SMALL_NAMES:
            dst[n] = fl[off:off + w[n].size].reshape(w[n].shape)
            off += w[n].size

    return (loss, dx[None], *[out_g[n] for n in WEIGHT_NAMES], *[out_d[n] for n in WEIGHT_NAMES],
            *[out_m[n] for n in WEIGHT_NAMES], *[out_v[n] for n in WEIGHT_NAMES])
```

```python
import functools

import jax
import jax.numpy as jnp
from jax import lax
from jax.experimental import pallas as pl
from jax.experimental.pallas import tpu as pltpu

F32 = jnp.float32
BF16 = jnp.bfloat16

LANES = 128
SUBLANES = 8
VMEM_BYTES = 64 * 1024 * 1024
GDN_CHUNK = 64
CONV_WIDTH = 4
CONV_ROW_TILE = 2048
RMS_EPS = 1e-6
L2_EPS = 1e-6
LRU_C = 8.0
ADAM_LR = 0.001
ADAM_B1 = 0.9
ADAM_B2 = 0.999
ADAM_EPS = 1e-08
ADAM_WD = 0.01
ADAM_STEP = 10
MESH_AXES = ("x", "y", "c")
N_CHIPS = 4
N_DEVICES = 8

INPUT_NAMES = ['x', 'attn_norm', 'w_in', 'gdn_conv_w', 'gdn_a_log', 'gdn_dt_bias', 'gdn_norm', 'lru_conv_w',
               'lru_conv_b', 'lru_w_a', 'lru_b_a', 'lru_w_x', 'lru_b_x', 'lru_lambda', 'w_branch_gdn',
               'w_branch_lru', 'w_out', 'mlp_norm', 'w_up', 'w_down', 'final_norm']
WEIGHT_NAMES = INPUT_NAMES[1:]
BIG_SHARD_AXIS = {'w_in': 2, 'w_branch_gdn': 2, 'w_branch_lru': 2, 'w_out': 1, 'w_up': 2, 'w_down': 1}
CONV_SHARD_AXIS = {'gdn_conv_w': 2, 'lru_conv_w': 2}
GATHER_GROUPS = {'in': ['w_in'], 'mix': ['w_branch_gdn', 'w_branch_lru', 'w_out'], 'mlp': ['w_up', 'w_down']}
SMALL_NAMES = [n for n in WEIGHT_NAMES if n not in BIG_SHARD_AXIS]


def _tile(n, target, unit=LANES):
    best = None
    t = unit
    while t <= min(n, target):
        if n % t == 0:
            best = t
        t += unit
    return n if best is None else best


def _vmem_limit(block_bytes):
    return int(min(max(3 * block_bytes + (8 << 20), 24 << 20), VMEM_BYTES - (8 << 20)))


def _nbytes(shape, dtype):
    n = 1
    for s in shape:
        n *= s
    return n * jnp.dtype(dtype).itemsize


def _pcall(body, *, name, grid, in_specs, out_specs, out_shape, scratch_shapes=(), semantics=None, block_bytes=0,
           scalar_prefetch=0):
    params = dict(vmem_limit_bytes=_vmem_limit(block_bytes))
    if semantics is not None:
        params['dimension_semantics'] = semantics
    if scalar_prefetch:
        grid_spec = pltpu.PrefetchScalarGridSpec(num_scalar_prefetch=scalar_prefetch, grid=grid, in_specs=in_specs,
                                                 out_specs=out_specs, scratch_shapes=list(scratch_shapes))
        return pl.pallas_call(body, name=name, grid_spec=grid_spec, out_shape=out_shape,
                              compiler_params=pltpu.CompilerParams(**params))
    return pl.pallas_call(body, name=name, grid=grid, in_specs=in_specs, out_specs=out_specs, out_shape=out_shape,
                          scratch_shapes=list(scratch_shapes), compiler_params=pltpu.CompilerParams(**params))


def _dot(a, b):
    return jnp.dot(a.astype(BF16), b.astype(BF16), preferred_element_type=F32)


def _dot_nt(a, b):
    return lax.dot_general(a.astype(BF16), b.astype(BF16), (((1,), (1,)), ((), ())), preferred_element_type=F32)


def _dot_tn(a, b):
    return lax.dot_general(a.astype(BF16), b.astype(BF16), (((0,), (0,)), ((), ())), preferred_element_type=F32)


def _sigmoid(x):
    return 1.0 / (1.0 + jnp.exp(-x))


def _log1p(u):
    return jnp.where(u < 1e-3, u * (1.0 - u * (0.5 - u * (1.0 / 3.0))), jnp.log(1.0 + u))


def _softplus(x):
    return jnp.maximum(x, 0.0) + _log1p(jnp.exp(-jnp.abs(x)))


_GELU_K = 0.7978845608028654


def _gelu_and_grad(x):
    inner = _GELU_K * (x + 0.044715 * x * x * x)
    th = jnp.tanh(inner)
    g = 0.5 * x * (1.0 + th)
    dg = 0.5 * (1.0 + th) + 0.5 * x * (1.0 - th * th) * _GELU_K * (1.0 + 3.0 * 0.044715 * x * x)
    return g, dg


MATMUL_TK_MAX = 3584
PROJ_COL_TILE = 1536


def _matmul(a, b, *, mode, name, out_dtype=F32, add=None, epilogue=None, extra=None, tm=512, tn=1024, tk=2048,
            shard_axis=None):
    if mode == 'nn':
        (m, k), (k2, n) = a.shape, b.shape
    elif mode == 'nt':
        (m, k), (n, k2) = a.shape, b.shape
    else:
        (k, m), (k2, n) = a.shape, b.shape
    assert k == k2, (a.shape, b.shape, mode)
    if shard_axis is not None:
        rows_half = (m // N_CHIPS if shard_axis == 0 else m) // 2
        cols = n // N_CHIPS if shard_axis == 1 else n
        tm, tn = _tile(rows_half, tm), _tile(cols, tn)
    else:
        tm, tn = _tile(m, tm), _tile(n, tn)
    tk = _tile(k, tk)
    if k // tk > 2 * (-(-k // MATMUL_TK_MAX)):
        tk = _tile(k, MATMUL_TK_MAX)
    nk = k // tk
    dims = {'nn': (((1,), (0,)), ((), ())), 'nt': (((1,), (1,)), ((), ())), 'tn': (((0,), (0,)), ((), ()))}[mode]
    a_bytes, b_bytes = _nbytes(a.shape, a.dtype), _nbytes(b.shape, b.dtype)
    rows_outer = nk > 1 or a_bytes + (m // tm) * b_bytes <= b_bytes + (n // tn) * a_bytes

    def ij(g0, g1):
        return (g0, g1) if rows_outer else (g1, g0)

    def spec(shape, pick):
        return pl.BlockSpec(shape, lambda g0, g1, kk: pick(*ij(g0, g1), kk))

    a_spec = spec((tk, tm), lambda i, j, kk: (kk, i)) if mode == 'tn' else spec((tm, tk), lambda i, j, kk: (i, kk))
    b_spec = spec((tn, tk), lambda i, j, kk: (j, kk)) if mode == 'nt' else spec((tk, tn), lambda i, j, kk: (kk, j))
    o_spec = spec((tm, tn), lambda i, j, kk: (i, j))
    operands, in_specs = [a, b], [a_spec, b_spec]
    if add is not None:
        operands.append(add)
        in_specs.append(o_spec)
    if extra is not None:
        operands.append(extra)
        in_specs.append(o_spec)
    n_in = len(operands)
    if epilogue == 'relu2':
        out_shape = (jax.ShapeDtypeStruct((m, n), BF16), jax.ShapeDtypeStruct((m, n), BF16))
        out_specs = (o_spec, o_spec)
    elif shard_axis is not None:
        assert add is None and extra is None
        rb, cb = rows_half // tm, cols // tn

        def shard_block(i, j, kk):
            if shard_axis == 0:
                return (i % (2 * rb)) // rb, i // (2 * rb), i % rb, j
            return i // rb, j // cb, i % rb, j % cb

        out_shape = jax.ShapeDtypeStruct((2, N_CHIPS, rows_half, cols), out_dtype)
        out_specs = spec((None, None, tm, tn), shard_block)
    else:
        out_shape = jax.ShapeDtypeStruct((m, n), out_dtype)
        out_specs = o_spec

    def body(*refs):
        a_ref, b_ref = refs[0], refs[1]
        outs = refs[n_in:n_in + n_out]

        def finish(p):
            if add is not None:
                p = p + refs[2][...]
            if epilogue == 'relu2':
                ur = jnp.maximum(p, 0.0)
                outs[0][...] = ur.astype(BF16)
                outs[1][...] = (ur * ur).astype(BF16)
            elif epilogue == 'mul2x':
                outs[0][...] = (p * 2.0 * refs[n_in - 1][...].astype(F32)).astype(out_dtype)
            else:
                outs[0][...] = p.astype(out_dtype)

        prod = lax.dot_general(a_ref[...].astype(BF16), b_ref[...].astype(BF16), dims, preferred_element_type=F32)
        if nk == 1:
            finish(prod)
            return
        acc_ref = refs[-1]
        kk = pl.program_id(2)

        @pl.when(kk == 0)
        def _():
            acc_ref[...] = prod

        @pl.when((kk > 0) & (kk < nk - 1))
        def _():
            acc_ref[...] += prod

        @pl.when(kk == nk - 1)
        def _():
            finish(acc_ref[...] + prod)

    n_out = 2 if epilogue == 'relu2' else 1
    bb = (_nbytes((tm, tk), a.dtype) + _nbytes((tk, tn), b.dtype) + 3 * _nbytes((tm, tn), F32))
    grid = (m // tm, n // tn, nk) if rows_outer else (n // tn, m // tm, nk)
    return _pcall(body, name=name, grid=grid, in_specs=in_specs, out_specs=out_specs, out_shape=out_shape,
                  scratch_shapes=[pltpu.VMEM((tm, tn), F32)] if nk > 1 else [],
                  semantics=("parallel", "parallel", "arbitrary"), block_bytes=bb)(*operands)


def _row_tile(s, d, target_bytes=1 << 20):
    return _tile(s, max(SUBLANES, target_bytes // (4 * d)), SUBLANES)


def _rms_fwd(x, gain, name):
    s, d = x.shape
    tr = _row_tile(s, d)

    def body(x_ref, g_ref, h_ref):
        xv = x_ref[...]
        r = lax.rsqrt(jnp.mean(xv * xv, axis=-1, keepdims=True) + RMS_EPS)
        h_ref[...] = (xv * r * g_ref[...]).astype(BF16)

    row = pl.BlockSpec((tr, d), lambda i: (i, 0))
    return _pcall(body, name=name, grid=(s // tr,), in_specs=[row, pl.BlockSpec((1, d), lambda i: (0, 0))],
                  out_specs=row, out_shape=jax.ShapeDtypeStruct((s, d), BF16), semantics=("parallel",),
                  block_bytes=2 * tr * d * 4)(x, gain.reshape(1, d))


def _rms_bwd(x, gain, dh, dres, name):
    s, d = x.shape
    tr = _row_tile(s, d, 1 << 19)

    def body(x_ref, g_ref, dh_ref, dres_ref, dx_ref, dxb_ref, dg_ref):
        xv = x_ref[...]
        r = lax.rsqrt(jnp.mean(xv * xv, axis=-1, keepdims=True) + RMS_EPS)
        xh = xv * r
        dhv = dh_ref[...]
        dxh = dhv * g_ref[...]
        dx = dres_ref[...] + r * (dxh - xh * jnp.mean(dxh * xh, axis=-1, keepdims=True))
        dx_ref[...] = dx
        dxb_ref[...] = dx.astype(BF16)

        @pl.when(pl.program_id(0) == 0)
        def _():
            dg_ref[...] = jnp.zeros_like(dg_ref)

        dg_ref[...] += jnp.sum(dhv * xh, axis=0, keepdims=True)

    row = pl.BlockSpec((tr, d), lambda i: (i, 0))
    vec = pl.BlockSpec((1, d), lambda i: (0, 0))
    return _pcall(body, name=name, grid=(s // tr,), in_specs=[row, vec, row, row], out_specs=(row, row, vec),
                  out_shape=(jax.ShapeDtypeStruct((s, d), F32), jax.ShapeDtypeStruct((s, d), BF16),
                             jax.ShapeDtypeStruct((1, d), F32)),
                  semantics=("arbitrary",), block_bytes=5 * tr * d * 4)(x, gain.reshape(1, d), dh, dres)


def _loss_head(x, gain, target, name):
    s, d = x.shape
    tr = _row_tile(s, d, 1 << 19)

    def body(x_ref, g_ref, t_ref, loss_ref, dx_ref, dxb_ref, dg_ref):
        xv = x_ref[...]
        r = lax.rsqrt(jnp.mean(xv * xv, axis=-1, keepdims=True) + RMS_EPS)
        xh = xv * r
        gv = g_ref[...]
        err = xh * gv - t_ref[...]
        dy = err * (1.0 / d)
        dxh = dy * gv
        dx = r * (dxh - xh * jnp.mean(dxh * xh, axis=-1, keepdims=True))
        dx_ref[...] = dx
        dxb_ref[...] = dx.astype(BF16)

        @pl.when(pl.program_id(0) == 0)
        def _():
            dg_ref[...] = jnp.zeros_like(dg_ref)
            loss_ref[...] = jnp.zeros_like(loss_ref)

        dg_ref[...] += jnp.sum(dy * xh, axis=0, keepdims=True)
        part = jnp.sum(jnp.sum(err * err, axis=-1, keepdims=True), axis=0, keepdims=True) * (0.5 / d)
        loss_ref[...] += jnp.broadcast_to(part, loss_ref.shape)

    row = pl.BlockSpec((tr, d), lambda i: (i, 0))
    vec = pl.BlockSpec((1, d), lambda i: (0, 0))
    lspec = pl.BlockSpec((SUBLANES, LANES), lambda i: (0, 0))
    return _pcall(body, name=name, grid=(s // tr,), in_specs=[row, vec, row], out_specs=(lspec, row, row, vec),
                  out_shape=(jax.ShapeDtypeStruct((SUBLANES, LANES), F32), jax.ShapeDtypeStruct((s, d), F32),
                             jax.ShapeDtypeStruct((s, d), BF16), jax.ShapeDtypeStruct((1, d), F32)),
                  semantics=("arbitrary",), block_bytes=4 * tr * d * 4)(x, gain.reshape(1, d), target)


def _shift_down(xc, xp, s):
    tr = xc.shape[0]
    r = pltpu.roll(xc, s, 0)
    p = pltpu.roll(xp, s, 0)
    row8 = lax.broadcasted_iota(jnp.int32, (SUBLANES, xc.shape[1]), 0)
    head = jnp.where(row8 < s, p, r[:SUBLANES])
    if tr == SUBLANES:
        return head
    return jnp.concatenate([head, r[SUBLANES:]], axis=0)


def _shift_up(yc, yn, s):
    tr = yc.shape[0]
    u = pltpu.roll(yc, tr - s, 0)
    n = pltpu.roll(yn, SUBLANES - s, 0)
    row8 = lax.broadcasted_iota(jnp.int32, (SUBLANES, yc.shape[1]), 0)
    tail = jnp.where(row8 >= SUBLANES - s, n, u[tr - SUBLANES:])
    if tr == SUBLANES:
        return tail
    return jnp.concatenate([u[:tr - SUBLANES], tail], axis=0)


def _conv_apply(xc, xp, w):
    y = xc * w[CONV_WIDTH - 1:CONV_WIDTH, :]
    for s in range(1, CONV_WIDTH):
        y = y + _shift_down(xc, xp, s) * w[CONV_WIDTH - 1 - s:CONV_WIDTH - s, :]
    return y


def _halo_specs(tr, col_of):
    per = tr // SUBLANES
    cur = pl.BlockSpec((tr, LANES), lambda j, i: (i, col_of(j)))
    prev = pl.BlockSpec((SUBLANES, LANES), lambda j, i: (jnp.maximum(i * per - 1, 0), col_of(j)))
    return cur, prev


def _conv_bias_fwd(x_arr, x_col0, w, bias, name):
    s = x_arr.shape[0]
    ncb = w.shape[1] // LANES
    tr = _tile(s, CONV_ROW_TILE, SUBLANES)

    def body(cur_ref, prev_ref, w_ref, b_ref, o_ref):
        i = pl.program_id(1)
        xp = prev_ref[...] * (i > 0).astype(F32)
        o_ref[...] = _conv_apply(cur_ref[...], xp, w_ref[...]) + b_ref[...]

    cur, prev = _halo_specs(tr, lambda j: x_col0 + j)
    return _pcall(body, name=name, grid=(ncb, s // tr),
                  in_specs=[cur, prev, pl.BlockSpec((CONV_WIDTH, LANES), lambda j, i: (0, j)),
                            pl.BlockSpec((1, LANES), lambda j, i: (0, j))],
                  out_specs=pl.BlockSpec((tr, LANES), lambda j, i: (i, j)),
                  out_shape=jax.ShapeDtypeStruct((s, w.shape[1]), F32), semantics=("parallel", "parallel"),
                  block_bytes=3 * tr * LANES * 4)(x_arr, x_arr, w, bias.reshape(1, -1))


def _conv_bwd(dy, x_arr, x_col0, w, name):
    s, c = dy.shape
    ncb = c // LANES
    tr = _tile(s, CONV_ROW_TILE, SUBLANES)
    per = tr // SUBLANES
    ni = s // tr

    def body(dy_ref, dyn_ref, cur_ref, prev_ref, w_ref, dx_ref, dw_ref, db_ref):
        i = pl.program_id(1)
        dyv = dy_ref[...]
        dn = dyn_ref[...] * (i < ni - 1).astype(F32)
        xc = cur_ref[...]
        xp = prev_ref[...] * (i > 0).astype(F32)
        wv = w_ref[...]

        @pl.when(i == 0)
        def _():
            dw_ref[...] = jnp.zeros_like(dw_ref)
            db_ref[...] = jnp.zeros_like(db_ref)

        dx = dyv * wv[CONV_WIDTH - 1:CONV_WIDTH, :]
        dw_ref[CONV_WIDTH - 1:CONV_WIDTH, :] += jnp.sum(dyv * xc, axis=0, keepdims=True)
        for sh in range(1, CONV_WIDTH):
            j = CONV_WIDTH - 1 - sh
            dx = dx + _shift_up(dyv, dn, sh) * wv[j:j + 1, :]
            dw_ref[j:j + 1, :] += jnp.sum(dyv * _shift_down(xc, xp, sh), axis=0, keepdims=True)
        dx_ref[...] = dx
        db_ref[...] += jnp.sum(dyv, axis=0, keepdims=True)

    cur, prev = _halo_specs(tr, lambda j: x_col0 + j)
    dcur = pl.BlockSpec((tr, LANES), lambda j, i: (i, j))
    dnext = pl.BlockSpec((SUBLANES, LANES), lambda j, i: (jnp.minimum((i + 1) * per, s // SUBLANES - 1), j))
    return _pcall(body, name=name, grid=(ncb, ni),
                  in_specs=[dcur, dnext, cur, prev, pl.BlockSpec((CONV_WIDTH, LANES), lambda j, i: (0, j))],
                  out_specs=(dcur, pl.BlockSpec((CONV_WIDTH, LANES), lambda j, i: (0, j)),
                             pl.BlockSpec((1, LANES), lambda j, i: (0, j))),
                  out_shape=(jax.ShapeDtypeStruct((s, c), F32), jax.ShapeDtypeStruct((CONV_WIDTH, c), F32),
                             jax.ShapeDtypeStruct((1, c), F32)),
                  semantics=("parallel", "arbitrary"), block_bytes=4 * tr * LANES * 4)(dy, dy, x_arr, x_arr, w)


def _gdn_pre_fwd(proj, conv_w, heads, name):
    s = proj.shape[0]
    ncb = conv_w.shape[1] // LANES
    tr = _tile(s, CONV_ROW_TILE, SUBLANES)
    qscale = float(LANES) ** -0.5

    def body(cur_ref, prev_ref, w_ref, o_ref):
        j, i = pl.program_id(0), pl.program_id(1)
        xp = prev_ref[...] * (i > 0).astype(F32)
        cv = _conv_apply(cur_ref[...], xp, w_ref[...])
        sv = cv * _sigmoid(cv)
        nrm = lax.rsqrt(jnp.sum(sv * sv, axis=-1, keepdims=True) + L2_EPS)
        scale = jnp.where(j < heads, qscale, 1.0)
        o_ref[...] = jnp.where(j < 2 * heads, sv * nrm * scale, sv)

    cur, prev = _halo_specs(tr, lambda j: j)
    return _pcall(body, name=name, grid=(ncb, s // tr),
                  in_specs=[cur, prev, pl.BlockSpec((CONV_WIDTH, LANES), lambda j, i: (0, j))],
                  out_specs=pl.BlockSpec((tr, LANES), lambda j, i: (i, j)),
                  out_shape=jax.ShapeDtypeStruct((s, conv_w.shape[1]), F32), semantics=("parallel", "parallel"),
                  block_bytes=3 * tr * LANES * 4)(proj, proj, conv_w)


def _gdn_pre_bwd(proj, conv_w, dq, dk, dv, heads, name):
    s = proj.shape[0]
    ncb = conv_w.shape[1] // LANES
    tr = _tile(s, CONV_ROW_TILE, SUBLANES)
    qscale = float(LANES) ** -0.5

    def body(cur_ref, prev_ref, w_ref, dq_ref, dk_ref, dv_ref, o_ref):
        j, i = pl.program_id(0), pl.program_id(1)
        xp = prev_ref[...] * (i > 0).astype(F32)
        cv = _conv_apply(cur_ref[...], xp, w_ref[...])
        sg = _sigmoid(cv)
        sv = cv * sg
        nrm = lax.rsqrt(jnp.sum(sv * sv, axis=-1, keepdims=True) + L2_EPS)
        dv = jnp.where(j < heads, dq_ref[...], jnp.where(j < 2 * heads, dk_ref[...], dv_ref[...]))
        scale = jnp.where(j < heads, qscale, 1.0)
        dsn = scale * nrm * (dv - sv * (nrm * nrm) * jnp.sum(dv * sv, axis=-1, keepdims=True))
        ds = jnp.where(j < 2 * heads, dsn, dv)
        o_ref[...] = ds * (sg * (1.0 + cv * (1.0 - sg)))

    cur, prev = _halo_specs(tr, lambda j: j)
    blk = pl.BlockSpec((tr, LANES), lambda j, i: (i, j))

    def part(k):
        return pl.BlockSpec((tr, LANES), lambda j, i: (i, jnp.clip(j - k * heads, 0, heads - 1)))

    return _pcall(body, name=name, grid=(ncb, s // tr),
                  in_specs=[cur, prev, pl.BlockSpec((CONV_WIDTH, LANES), lambda j, i: (0, j)), part(0), part(1),
                            part(2)],
                  out_specs=blk, out_shape=jax.ShapeDtypeStruct((s, conv_w.shape[1]), F32),
                  semantics=("parallel", "parallel"), block_bytes=6 * tr * LANES * 4)(proj, proj, conv_w, dq, dk, dv)


def _dot_split(a, b, dims=(((1,), (0,)), ((), ()))):
    a_hi, b_hi = a.astype(BF16), b.astype(BF16)
    a_lo, b_lo = (a - a_hi.astype(F32)).astype(BF16), (b - b_hi.astype(F32)).astype(BF16)

    def dot(u, v):
        return lax.dot_general(u, v, dims, preferred_element_type=F32)

    return dot(a_hi, b_hi) + dot(a_hi, b_lo) + dot(a_lo, b_hi)


def _tri_inverse(a_strict, block):
    n = a_strict.shape[0]
    ri = lax.broadcasted_iota(jnp.int32, (n, n), 0)
    ci = lax.broadcasted_iota(jnp.int32, (n, n), 1)
    same8 = (ri >> 3) == (ci >> 3)
    sel = jnp.where((lax.broadcasted_iota(jnp.int32, (n, LANES), 0) & 7)
                    == lax.broadcasted_iota(jnp.int32, (n, LANES), 1), 1.0, 0.0)
    a8 = _dot_split(jnp.where(same8, a_strict, 0.0), sel)
    t8 = sel
    r_in = lax.broadcasted_iota(jnp.int32, (n, 1), 0) & 7
    for j in range(SUBLANES - 1):
        row_j = jnp.broadcast_to(t8.reshape(n // SUBLANES, SUBLANES, LANES)[:, j:j + 1, :],
                                 (n // SUBLANES, SUBLANES, LANES)).reshape(n, LANES)
        t8 = t8 - jnp.where(r_in > j, a8[:, j:j + 1] * row_j, 0.0)
    t = jnp.where(same8, _dot_split(t8, sel, (((1,), (1,)), ((), ()))), 0.0)
    size = SUBLANES
    while size < block:
        sh = size.bit_length() - 1
        lower_left = (((ri >> (sh + 1)) == (ci >> (sh + 1))) & (((ri >> sh) & 1) == 1) & (((ci >> sh) & 1) == 0))
        t = t - _dot_split(t, _dot_split(jnp.where(lower_left, a_strict, 0.0), t))
        size *= 2
    return t


GDN_HEAD_GROUP = 4
_CHUNK_SHIFT = GDN_CHUNK.bit_length() - 1
_LANE_SHIFT = LANES.bit_length() - 1


def _stack_heads(ref, hb):
    return jnp.concatenate([ref[:, i * LANES:(i + 1) * LANES] for i in range(hb)], axis=0)


def _diag_blocks(x, hb):
    c = GDN_CHUNK
    return jnp.concatenate([x[i * c:(i + 1) * c, i * LANES:(i + 1) * LANES] for i in range(hb)], axis=0)


def _expand_blocks(y, hb):
    row_blk = lax.shift_right_logical(lax.broadcasted_iota(jnp.int32, y.shape, 0), _CHUNK_SHIFT)
    return jnp.concatenate([jnp.where(row_blk == j, y, 0.0) for j in range(hb)], axis=1)


def _gdn_group_terms(q, k, v, ab, alog, dtb, head0, hb, heads):
    c = GDN_CHUNK
    r = hb * c
    lane = lax.broadcasted_iota(jnp.int32, (1, LANES), 1)

    def column(lane0):
        return jnp.concatenate([jnp.sum(jnp.where(lane == lane0 + head0 + i, ab, 0.0), axis=1, keepdims=True)
                                for i in range(hb)], axis=0)

    def per_head(vec):
        return jnp.concatenate([jnp.broadcast_to(jnp.sum(jnp.where(lane == head0 + i, vec, 0.0), axis=1,
                                                         keepdims=True), (c, 1)) for i in range(hb)], axis=0)

    pre = column(0) + per_head(dtb)
    neg_ea = -jnp.exp(per_head(alog))
    g = neg_ea * _softplus(pre)
    beta = _sigmoid(column(heads))
    ri = lax.broadcasted_iota(jnp.int32, (r, r), 0)
    ci = lax.broadcasted_iota(jnp.int32, (r, r), 1)
    same = lax.shift_right_logical(ri, _CHUNK_SHIFT) == lax.shift_right_logical(ci, _CHUNK_SHIFT)
    eye = ri == ci
    causal = same & (ri >= ci)
    strict = same & (ri > ci)
    g_row = jnp.sum(jnp.where(eye, g, 0.0), axis=0, keepdims=True)
    gc_col = jnp.sum(jnp.where(causal, g_row, 0.0), axis=1, keepdims=True)
    gc_row = jnp.sum(jnp.where(same & (ri <= ci), g, 0.0), axis=0, keepdims=True)
    gl_col = jnp.sum(jnp.where(same, g_row, 0.0), axis=1, keepdims=True)
    decay = jnp.where(causal, jnp.exp(jnp.where(causal, gc_col - gc_row, 0.0)), 0.0)
    e_last_col = jnp.exp(gl_col)
    e_last_lanes = jnp.concatenate([jnp.broadcast_to(e_last_col[i * c:i * c + 1, :], (1, LANES))
                                    for i in range(hb)], axis=1)
    egc = jnp.exp(gc_col)
    ekl = jnp.exp(gl_col - gc_col)
    kb = k * beta
    vb = v * beta
    kk = _dot_nt(kb, k)
    a_strict = jnp.where(strict, kk * decay, 0.0)
    return dict(pre=pre, neg_ea=neg_ea, g=g, beta=beta, ri=ri, ci=ci, same=same, eye=eye, causal=causal,
                strict=strict, decay=decay, e_last_col=e_last_col, e_last_lanes=e_last_lanes, egc=egc, ekl=ekl,
                kb=kb, vb=vb, kk=kk, a_strict=a_strict, lane=lane)


def _gdn_head_group(heads):
    hb = GDN_HEAD_GROUP
    while heads % hb:
        hb //= 2
    return hb


def _gdn_fwd(qkv, proj, ab_blk, alog, dtb, heads, name):
    s = qkv.shape[0]
    c = GDN_CHUNK
    nc = s // c
    hb = _gdn_head_group(heads)
    ng = heads // hb
    r = hb * c

    def body(q_ref, k_ref, v_ref, ab_ref, alog_ref, dtb_ref, o_ref, t_ref, s0_ref, state_ref):
        grp, ch = pl.program_id(0), pl.program_id(1)

        @pl.when(ch == 0)
        def _():
            state_ref[...] = jnp.zeros_like(state_ref)

        q, k, v = _stack_heads(q_ref, hb), _stack_heads(k_ref, hb), _stack_heads(v_ref, hb)
        tm = _gdn_group_terms(q, k, v, ab_ref[...], alog_ref[...], dtb_ref[...], grp * hb, hb, heads)
        t_inv = _tri_inverse(tm['a_strict'], c)
        u = _dot(t_inv, tm['vb'])
        w = _dot(t_inv, tm['kb'] * tm['egc'])
        qk = jnp.where(tm['causal'], _dot_nt(q, k) * tm['decay'], 0.0)
        st = state_ref[...]
        v_new = u - _diag_blocks(_dot(w, st), hb)
        out = _diag_blocks(_dot(q * tm['egc'], st), hb) + _dot(qk, v_new)
        for i in range(hb):
            o_ref[:, i * LANES:(i + 1) * LANES] = out[i * c:(i + 1) * c, :]
        t_ref[...] = t_inv
        s0_ref[...] = st
        state_ref[...] = st * tm['e_last_lanes'] + _dot_tn(k * tm['ekl'], _expand_blocks(v_new, hb))

    def blk(off):
        return pl.BlockSpec((c, hb * LANES), lambda g, n: (n, off * ng + g))

    vec = pl.BlockSpec((1, LANES), lambda g, n: (0, 0))
    return _pcall(
        body, name=name, grid=(ng, nc),
        in_specs=[blk(0), blk(1), blk(2), pl.BlockSpec((c, LANES), lambda g, n: (n, ab_blk)), vec, vec],
        out_specs=(blk(0), pl.BlockSpec((None, None, r, r), lambda g, n: (g, n, 0, 0)),
                   pl.BlockSpec((None, None, LANES, hb * LANES), lambda g, n: (g, n, 0, 0))),
        out_shape=(jax.ShapeDtypeStruct((s, heads * LANES), F32), jax.ShapeDtypeStruct((ng, nc, r, r), F32),
                   jax.ShapeDtypeStruct((ng, nc, LANES, hb * LANES), F32)),
        scratch_shapes=[pltpu.VMEM((LANES, hb * LANES), F32)], semantics=("parallel", "arbitrary"),
        block_bytes=8 * r * LANES * 4 + 2 * r * r * 4 + 2 * LANES * hb * LANES * 4)(qkv, qkv, qkv, proj, alog, dtb)


def _gdn_bwd(qkv, proj, ab_blk, alog, dtb, t_all, s0_all, d_o, heads, name):
    s = qkv.shape[0]
    c = GDN_CHUNK
    nc = s // c
    hb = _gdn_head_group(heads)
    ng = heads // hb
    r = hb * c

    def body(q_ref, k_ref, v_ref, ab_ref, alog_ref, dtb_ref, t_ref, s0_ref, do_ref,
             dq_ref, dk_ref, dv_ref, dgb_ref, ds_ref):
        grp, step = pl.program_id(0), pl.program_id(1)

        @pl.when(step == 0)
        def _():
            ds_ref[...] = jnp.zeros_like(ds_ref)

        q, k, v = _stack_heads(q_ref, hb), _stack_heads(k_ref, hb), _stack_heads(v_ref, hb)
        do = _stack_heads(do_ref, hb)
        tm = _gdn_group_terms(q, k, v, ab_ref[...], alog_ref[...], dtb_ref[...], grp * hb, hb, heads)
        ri, ci, same, eye = tm['ri'], tm['ci'], tm['same'], tm['eye']
        causal, strict, decay = tm['causal'], tm['strict'], tm['decay']
        egc, ekl, kb, vb, beta = tm['egc'], tm['ekl'], tm['kb'], tm['vb'], tm['beta']
        t_inv = t_ref[...]
        st = s0_ref[...]
        ds_next = ds_ref[...]
        kbg = kb * egc
        u = _dot(t_inv, vb)
        w = _dot(t_inv, kbg)
        qkm = _dot_nt(q, k)
        qk = jnp.where(causal, qkm * decay, 0.0)
        v_new = u - _diag_blocks(_dot(w, st), hb)
        qd = q * egc
        kd = k * ekl
        do_x = _expand_blocks(do, hb)

        dqd = _dot_nt(do_x, st)
        dqk = jnp.where(causal, _dot_nt(do, v_new), 0.0)
        dvn = _dot_tn(qk, do) + _diag_blocks(_dot(kd, ds_next), hb)
        dkd = _dot_nt(_expand_blocks(v_new, hb), ds_next)
        sd = jnp.sum(st * ds_next, axis=0, keepdims=True)
        dgl = jnp.concatenate([jnp.broadcast_to(jnp.sum(sd[:, i * LANES:(i + 1) * LANES], axis=1, keepdims=True),
                                                (c, 1)) for i in range(hb)], axis=0) * tm['e_last_col']
        dvn_x = _expand_blocks(dvn, hb)
        dw = -_dot_nt(dvn_x, st)
        ds_ref[...] = _dot_tn(qd, do_x) + tm['e_last_lanes'] * ds_next - _dot_tn(w, dvn_x)
        dt = _dot_nt(dvn, vb) + _dot_nt(dw, kbg)
        dvb = _dot_tn(t_inv, dvn)
        dkbg = _dot_tn(t_inv, dw)
        da_m = jnp.where(strict, -_dot_tn(t_inv, _dot_nt(dt, t_inv)), 0.0)
        dad = da_m * decay
        dkb = _dot(dad, k) + dkbg * egc
        dqkd = dqk * decay
        dq = _dot(dqkd, k) + dqd * egc
        dk = _dot_tn(dad, kb) + _dot_tn(dqkd, q) + dkd * ekl + dkb * beta
        e_mat = (da_m * tm['kk'] + dqk * qkm) * decay
        s_kd = jnp.sum(dkd * kd, axis=1, keepdims=True)
        s_kd_row = jnp.sum(jnp.where(eye, s_kd, 0.0), axis=0, keepdims=True)
        dgl = dgl + jnp.sum(jnp.where(same, s_kd_row, 0.0), axis=1, keepdims=True)
        col_sum = jnp.sum(e_mat, axis=0, keepdims=True)
        col_sum_c = jnp.sum(jnp.where(eye, col_sum, 0.0), axis=1, keepdims=True)
        dgc = (jnp.sum(e_mat, axis=1, keepdims=True) - col_sum_c + jnp.sum(dqd * qd, axis=1, keepdims=True)
               - s_kd + jnp.sum(dkbg * kbg, axis=1, keepdims=True))
        row_c = lax.broadcasted_iota(jnp.int32, (r, 1), 0)
        dgc = dgc + jnp.where((row_c & (c - 1)) == c - 1, dgl, 0.0)
        dgc_row = jnp.sum(jnp.where(eye, dgc, 0.0), axis=0, keepdims=True)
        dg = jnp.sum(jnp.where(same & (ci >= ri), dgc_row, 0.0), axis=1, keepdims=True)
        dbeta = jnp.sum(dkb * k, axis=1, keepdims=True) + jnp.sum(dvb * v, axis=1, keepdims=True)
        da_pre = dg * tm['neg_ea'] * _sigmoid(tm['pre'])
        db_pre = dbeta * beta * (1.0 - beta)
        lane = tm['lane']
        head_row = grp * hb + lax.shift_right_logical(row_c, _CHUNK_SHIFT)
        dgb = (jnp.where(lane == head_row, da_pre, 0.0) + jnp.where(lane == heads + head_row, db_pre, 0.0)
               + jnp.where(lane == 2 * heads + head_row, dg * tm['g'], 0.0))
        dvv = dvb * beta
        for i in range(hb):
            cols, rows = slice(i * LANES, (i + 1) * LANES), slice(i * c, (i + 1) * c)
            dq_ref[:, cols] = dq[rows, :]
            dk_ref[:, cols] = dk[rows, :]
            dv_ref[:, cols] = dvv[rows, :]
            dgb_ref[:, cols] = dgb[rows, :]

    def blk(off):
        return pl.BlockSpec((c, hb * LANES), lambda g, n: (nc - 1 - n, off * ng + g))

    vec = pl.BlockSpec((1, LANES), lambda g, n: (0, 0))
    gw = heads * LANES
    dq, dk, dv, dgb = _pcall(
        body, name=name, grid=(ng, nc),
        in_specs=[blk(0), blk(1), blk(2), pl.BlockSpec((c, LANES), lambda g, n: (nc - 1 - n, ab_blk)),
                  vec, vec, pl.BlockSpec((None, None, r, r), lambda g, n: (g, nc - 1 - n, 0, 0)),
                  pl.BlockSpec((None, None, LANES, hb * LANES), lambda g, n: (g, nc - 1 - n, 0, 0)), blk(0)],
        out_specs=(blk(0), blk(0), blk(0), blk(0)),
        out_shape=tuple(jax.ShapeDtypeStruct((s, gw), F32) for _ in range(4)),
        scratch_shapes=[pltpu.VMEM((LANES, hb * LANES), F32)], semantics=("parallel", "arbitrary"),
        block_bytes=12 * r * LANES * 4 + 2 * r * r * 4 + 2 * LANES * hb * LANES * 4)(
            qkv, qkv, qkv, proj, alog, dtb, t_all, s0_all, d_o)
    return dq, dk, dv, dgb


def _gdn_post_fwd(o, proj, z_col0, gain, name):
    s, gw = o.shape
    heads = gw // LANES
    tr = _tile(s, CONV_ROW_TILE, SUBLANES)

    def body(o_ref, z_ref, g_ref, y_ref):
        ov, zv = o_ref[...], z_ref[...]
        r = lax.rsqrt(jnp.mean(ov * ov, axis=-1, keepdims=True) + RMS_EPS)
        y_ref[...] = (ov * r * g_ref[...] * (zv * _sigmoid(zv))).astype(BF16)

    blk = pl.BlockSpec((tr, LANES), lambda i, h: (i, h))
    return _pcall(body, name=name, grid=(s // tr, heads),
                  in_specs=[blk, pl.BlockSpec((tr, LANES), lambda i, h: (i, z_col0 + h)),
                            pl.BlockSpec((1, LANES), lambda i, h: (0, 0))],
                  out_specs=blk, out_shape=jax.ShapeDtypeStruct((s, gw), BF16), semantics=("parallel", "parallel"),
                  block_bytes=3 * tr * LANES * 4)(o, proj, gain.reshape(1, LANES))


def _gdn_post_bwd(o, proj, z_col0, gain, dy, name):
    s, gw = o.shape
    heads = gw // LANES
    tr = _tile(s, CONV_ROW_TILE, SUBLANES)

    def body(o_ref, z_ref, g_ref, dy_ref, do_ref, dz_ref, dg_ref):
        ov, zv, gv, dyv = o_ref[...], z_ref[...], g_ref[...], dy_ref[...]
        r = lax.rsqrt(jnp.mean(ov * ov, axis=-1, keepdims=True) + RMS_EPS)
        nv = ov * r
        sg = _sigmoid(zv)
        sz = zv * sg
        dn = dyv * gv * sz
        do_ref[...] = r * (dn - nv * jnp.mean(dn * nv, axis=-1, keepdims=True))
        dz_ref[...] = dyv * nv * gv * (sg * (1.0 + zv * (1.0 - sg)))

        @pl.when((pl.program_id(0) == 0) & (pl.program_id(1) == 0))
        def _():
            dg_ref[...] = jnp.zeros_like(dg_ref)

        dg_ref[...] += jnp.sum(dyv * nv * sz, axis=0, keepdims=True)

    blk = pl.BlockSpec((tr, LANES), lambda i, h: (i, h))
    vec = pl.BlockSpec((1, LANES), lambda i, h: (0, 0))
    return _pcall(body, name=name, grid=(s // tr, heads),
                  in_specs=[blk, pl.BlockSpec((tr, LANES), lambda i, h: (i, z_col0 + h)), vec, blk],
                  out_specs=(blk, blk, vec),
                  out_shape=(jax.ShapeDtypeStruct((s, gw), F32), jax.ShapeDtypeStruct((s, gw), F32),
                             jax.ShapeDtypeStruct((1, LANES), F32)),
                  semantics=("arbitrary", "arbitrary"), block_bytes=6 * tr * LANES * 4)(
                      o, proj, gain.reshape(1, LANES), dy)


def _dab_reduce(dgb, name):
    s, gw = dgb.shape
    heads = gw // LANES
    tr = _tile(s, 512, SUBLANES)

    def body(d_ref, o_ref, cs_ref):
        acc = d_ref[:, 0:LANES]
        for h in range(1, heads):
            acc = acc + d_ref[:, h * LANES:(h + 1) * LANES]
        o_ref[...] = acc

        @pl.when(pl.program_id(0) == 0)
        def _():
            cs_ref[...] = jnp.zeros_like(cs_ref)

        cs_ref[...] += jnp.sum(acc, axis=0, keepdims=True)

    return _pcall(body, name=name, grid=(s // tr,), in_specs=[pl.BlockSpec((tr, gw), lambda i: (i, 0))],
                  out_specs=(pl.BlockSpec((tr, LANES), lambda i: (i, 0)), pl.BlockSpec((1, LANES), lambda i: (0, 0))),
                  out_shape=(jax.ShapeDtypeStruct((s, LANES), F32), jax.ShapeDtypeStruct((1, LANES), F32)),
                  semantics=("arbitrary",), block_bytes=tr * gw * 4)(dgb)


def _lru_gates(xc, wa, wx, ba, bx, lam):
    r = _sigmoid(_dot(xc, wa) + ba)
    ig = _sigmoid(_dot(xc, wx) + bx)
    sp = _softplus(-lam)
    log_a = -LRU_C * r * sp
    a = jnp.exp(log_a)
    e2 = jnp.exp(2.0 * log_a)
    mult = jnp.sqrt(jnp.maximum(1.0 - e2, 0.0))
    return r, ig, sp, a, e2, mult


def _lru_fwd(xc, proj, y_col0, wa, wx, ba, bx, lam, name):
    s, lw = xc.shape
    nb = lw // LANES
    tr = _tile(s, 256, SUBLANES)

    def body(xc_ref, y_ref, wa_ref, wx_ref, ba_ref, bx_ref, lam_ref, h_ref, o_ref, carry_ref):
        @pl.when(pl.program_id(1) == 0)
        def _():
            carry_ref[...] = jnp.zeros_like(carry_ref)

        xv = xc_ref[...]
        _, ig, _, a, _, mult = _lru_gates(xv, wa_ref[...], wx_ref[...], ba_ref[...], bx_ref[...], lam_ref[...])
        b = mult * (ig * xv)
        row = lax.broadcasted_iota(jnp.int32, (tr, LANES), 0)
        sh = 1
        while sh < tr:
            keep = row >= sh
            b = a * jnp.where(keep, pltpu.roll(b, sh, 0), 0.0) + b
            a = a * jnp.where(keep, pltpu.roll(a, sh, 0), 1.0)
            sh *= 2
        hv = a * carry_ref[0:1, :] + b
        h_ref[...] = hv
        carry_ref[...] = jnp.broadcast_to(hv[tr - 1:tr, :], carry_ref.shape)
        gy, _ = _gelu_and_grad(y_ref[...])
        o_ref[...] = (hv * gy).astype(BF16)

    blk = pl.BlockSpec((tr, LANES), lambda n, i: (i, n))
    wspec = pl.BlockSpec((None, LANES, LANES), lambda n, i: (n, 0, 0))
    vec = pl.BlockSpec((1, LANES), lambda n, i: (0, n))
    return _pcall(body, name=name, grid=(nb, s // tr),
                  in_specs=[blk, pl.BlockSpec((tr, LANES), lambda n, i: (i, y_col0 + n)), wspec, wspec, vec, vec, vec],
                  out_specs=(blk, blk),
                  out_shape=(jax.ShapeDtypeStruct((s, lw), F32), jax.ShapeDtypeStruct((s, lw), BF16)),
                  scratch_shapes=[pltpu.VMEM((SUBLANES, LANES), F32)], semantics=("parallel", "arbitrary"),
                  block_bytes=8 * tr * LANES * 4)(xc, proj, wa, wx, ba.reshape(1, lw), bx.reshape(1, lw),
                                                  lam.reshape(1, lw))


def _lru_bwd(d_out, xc, hseq, proj, y_col0, wa, wx, ba, bx, lam, name):
    s, lw = xc.shape
    nb = lw // LANES
    tr = _tile(s, 256, SUBLANES)
    per = tr // SUBLANES
    ni = s // tr
    nrow8 = s // SUBLANES

    def body(do_ref, xc_ref, xn_ref, h_ref, hp_ref, y_ref, wa_ref, wx_ref, ba_ref, bx_ref, lam_ref,
             dxc_ref, dy_ref, dwa_ref, dwx_ref, dba_ref, dbx_ref, dlam_ref, carry_ref):
        step = pl.program_id(1)
        tile = ni - 1 - step

        @pl.when(step == 0)
        def _():
            carry_ref[...] = jnp.zeros_like(carry_ref)
            dwa_ref[...] = jnp.zeros_like(dwa_ref)
            dwx_ref[...] = jnp.zeros_like(dwx_ref)
            dba_ref[...] = jnp.zeros_like(dba_ref)
            dbx_ref[...] = jnp.zeros_like(dbx_ref)
            dlam_ref[...] = jnp.zeros_like(dlam_ref)

        wav, wxv, bav, bxv, lamv = wa_ref[...], wx_ref[...], ba_ref[...], bx_ref[...], lam_ref[...]
        xv = xc_ref[...]
        r, ig, sp, a, e2, mult = _lru_gates(xv, wav, wxv, bav, bxv, lamv)
        a_next = _lru_gates(xn_ref[...], wav, wxv, bav, bxv, lamv)[3] * (tile < ni - 1).astype(F32)
        hv = h_ref[...]
        h_prev = _shift_down(hv, hp_ref[...] * (tile > 0).astype(F32), 1)
        yv = y_ref[...]
        gy, dgy = _gelu_and_grad(yv)
        dov = do_ref[...]
        dy_ref[...] = dov * hv * dgy
        coef = _shift_up(a, a_next, 1)
        bb = dov * gy
        row = lax.broadcasted_iota(jnp.int32, (tr, LANES), 0)
        sh = 1
        while sh < tr:
            keep = row < tr - sh
            bb = coef * jnp.where(keep, pltpu.roll(bb, tr - sh, 0), 0.0) + bb
            coef = coef * jnp.where(keep, pltpu.roll(coef, tr - sh, 0), 1.0)
            sh *= 2
        lam_t = coef * carry_ref[0:1, :] + bb
        carry_ref[...] = jnp.broadcast_to(lam_t[0:1, :], carry_ref.shape)
        d_a = lam_t * h_prev
        d_mult = lam_t * (ig * xv)
        d_ix = lam_t * mult
        d_la = d_a * a - d_mult * e2 / jnp.maximum(mult, 1e-30)
        d_r = d_la * (-LRU_C * sp)
        dlam_ref[...] += jnp.sum(d_la * (LRU_C * r) * _sigmoid(-lamv), axis=0, keepdims=True)
        d_pa = d_r * r * (1.0 - r)
        d_px = (d_ix * xv) * ig * (1.0 - ig)
        dxc_ref[...] = d_ix * ig + _dot_nt(d_pa, wav) + _dot_nt(d_px, wxv)
        dwa_ref[...] += _dot_tn(xv, d_pa)
        dwx_ref[...] += _dot_tn(xv, d_px)
        dba_ref[...] += jnp.sum(d_pa, axis=0, keepdims=True)
        dbx_ref[...] += jnp.sum(d_px, axis=0, keepdims=True)

    blk = pl.BlockSpec((tr, LANES), lambda n, i: (ni - 1 - i, n))
    nxt = pl.BlockSpec((SUBLANES, LANES), lambda n, i: (jnp.minimum((ni - i) * per, nrow8 - 1), n))
    prv = pl.BlockSpec((SUBLANES, LANES), lambda n, i: (jnp.maximum((ni - 1 - i) * per - 1, 0), n))
    wspec = pl.BlockSpec((None, LANES, LANES), lambda n, i: (n, 0, 0))
    vec = pl.BlockSpec((1, LANES), lambda n, i: (0, n))
    return _pcall(
        body, name=name, grid=(nb, ni),
        in_specs=[blk, blk, nxt, blk, prv, pl.BlockSpec((tr, LANES), lambda n, i: (ni - 1 - i, y_col0 + n)),
                  wspec, wspec, vec, vec, vec],
        out_specs=(blk, blk, wspec, wspec, vec, vec, vec),
        out_shape=(jax.ShapeDtypeStruct((s, lw), F32), jax.ShapeDtypeStruct((s, lw), F32),
                   jax.ShapeDtypeStruct((nb, LANES, LANES), F32), jax.ShapeDtypeStruct((nb, LANES, LANES), F32),
                   jax.ShapeDtypeStruct((1, lw), F32), jax.ShapeDtypeStruct((1, lw), F32),
                   jax.ShapeDtypeStruct((1, lw), F32)),
        scratch_shapes=[pltpu.VMEM((SUBLANES, LANES), F32)], semantics=("parallel", "arbitrary"),
        block_bytes=12 * tr * LANES * 4)(d_out, xc, xc, hseq, hseq, proj, wa, wx, ba.reshape(1, lw),
                                         bx.reshape(1, lw), lam.reshape(1, lw))


def _merge_fwd(proj, gg_col0, gl_col0, bg, bl, name):
    s, d = bg.shape
    tr, tc = _tile(s, 256, SUBLANES), _tile(d, 1024)
    cb = tc // LANES

    def body(gg_ref, gl_ref, bg_ref, bl_ref, o_ref):
        o_ref[...] = (_sigmoid(gg_ref[...]) * bg_ref[...] + _sigmoid(gl_ref[...]) * bl_ref[...]).astype(BF16)

    blk = pl.BlockSpec((tr, tc), lambda i, j: (i, j))
    return _pcall(body, name=name, grid=(s // tr, d // tc),
                  in_specs=[pl.BlockSpec((tr, tc), lambda i, j: (i, gg_col0 // cb + j)),
                            pl.BlockSpec((tr, tc), lambda i, j: (i, gl_col0 // cb + j)), blk, blk],
                  out_specs=blk, out_shape=jax.ShapeDtypeStruct((s, d), BF16), semantics=("parallel", "parallel"),
                  block_bytes=5 * tr * tc * 4)(proj, proj, bg, bl)


def _merge_bwd(proj, gg_col0, gl_col0, bg, bl, dm, name):
    s, d = bg.shape
    tr, tc = _tile(s, 256, SUBLANES), _tile(d, 1024)
    cb = tc // LANES

    def body(gg_ref, gl_ref, bg_ref, bl_ref, dm_ref, dgg_ref, dgl_ref, dbg_ref, dbl_ref):
        dmv = dm_ref[...]
        sg, sl = _sigmoid(gg_ref[...]), _sigmoid(gl_ref[...])
        dgg_ref[...] = (dmv * bg_ref[...] * sg * (1.0 - sg)).astype(BF16)
        dgl_ref[...] = (dmv * bl_ref[...] * sl * (1.0 - sl)).astype(BF16)
        dbg_ref[...] = (dmv * sg).astype(BF16)
        dbl_ref[...] = (dmv * sl).astype(BF16)

    blk = pl.BlockSpec((tr, tc), lambda i, j: (i, j))
    sh = jax.ShapeDtypeStruct((s, d), BF16)
    return _pcall(body, name=name, grid=(s // tr, d // tc),
                  in_specs=[pl.BlockSpec((tr, tc), lambda i, j: (i, gg_col0 // cb + j)),
                            pl.BlockSpec((tr, tc), lambda i, j: (i, gl_col0 // cb + j)), blk, blk, blk],
                  out_specs=(blk, blk, blk, blk), out_shape=(sh, sh, sh, sh), semantics=("parallel", "parallel"),
                  block_bytes=8 * tr * tc * 4)(proj, proj, bg, bl, dm)


def _sum_slots(slots, name):
    n, r, c = slots.shape
    tr = _tile(r, max(2 * SUBLANES, (1 << 19) // (c * 4)), 2 * SUBLANES)

    def body(s_ref, o_ref):
        acc = s_ref[0].astype(F32)
        for q in range(1, n):
            acc = acc + s_ref[q].astype(F32)
        o_ref[...] = acc

    return _pcall(body, name=name, grid=(r // tr,), in_specs=[pl.BlockSpec((n, tr, c), lambda i: (0, i, 0))],
                  out_specs=pl.BlockSpec((tr, c), lambda i: (i, 0)), out_shape=jax.ShapeDtypeStruct((r, c), F32),
                  semantics=("parallel",), block_bytes=(n + 1) * tr * c * 4)(slots)


def _adamw(w, g_parts, m, v, name):
    r, c = w.shape
    np_ = len(g_parts)
    tr = _tile(r, max(SUBLANES, (1 << 20) // (c * 4)), SUBLANES)
    c1 = 1.0 - ADAM_B1 ** ADAM_STEP
    c2 = 1.0 - ADAM_B2 ** ADAM_STEP

    def body(*refs):
        w_ref, m_ref, v_ref = refs[0], refs[1 + np_], refs[2 + np_]
        g_ref, d_ref, nm_ref, nv_ref = refs[3 + np_:]
        g = refs[1][...]
        for p in range(1, np_):
            g = g + refs[1 + p][...]
        nm = ADAM_B1 * m_ref[...] + (1.0 - ADAM_B1) * g
        nv = ADAM_B2 * v_ref[...] + (1.0 - ADAM_B2) * (g * g)
        g_ref[...] = g
        nm_ref[...] = nm
        nv_ref[...] = nv
        d_ref[...] = -ADAM_LR * ((nm / c1) / (jnp.sqrt(nv / c2) + ADAM_EPS) + ADAM_WD * w_ref[...])

    blk = pl.BlockSpec((tr, c), lambda i: (i, 0))
    sh = jax.ShapeDtypeStruct((r, c), F32)
    return _pcall(body, name=name, grid=(r // tr,), in_specs=[blk] * (3 + np_), out_specs=(blk,) * 4,
                  out_shape=(sh,) * 4, semantics=("parallel",), block_bytes=(7 + np_) * tr * c * 4)(
                      w, *g_parts, m, v)


def _pair_sum(core, mine, theirs, name):
    _, n, r, c = mine.shape
    tr = _tile(r, max(2 * SUBLANES, (1 << 19) // (c * 4)), 2 * SUBLANES)

    def body(core_ref, a_ref, b_ref, o_ref):
        o_ref[...] = (a_ref[...].astype(F32) + b_ref[...].astype(F32)).astype(BF16)

    return _pcall(body, name=name, grid=(n, r // tr),
                  in_specs=[pl.BlockSpec((None, None, tr, c), lambda q, i, core_ref: (core_ref[0], q, i, 0)),
                            pl.BlockSpec((None, tr, c), lambda q, i, core_ref: (q, i, 0))],
                  out_specs=pl.BlockSpec((None, tr, c), lambda q, i, core_ref: (q, i, 0)),
                  out_shape=jax.ShapeDtypeStruct((n, r, c), BF16), semantics=("parallel", "parallel"),
                  block_bytes=3 * tr * c * 4, scalar_prefetch=1)(core, mine, theirs)


def _sum_landed(chip, landed, own, name):
    n, r, c = landed.shape
    tr = _tile(r, max(2 * SUBLANES, (1 << 19) // (c * 4)), 2 * SUBLANES)

    def body(chip_ref, l_ref, o_ref, t_ref):
        acc = o_ref[...].astype(F32)
        for q in range(n):
            acc = acc + l_ref[q].astype(F32)
        t_ref[...] = acc

    return _pcall(body, name=name, grid=(r // tr,),
                  in_specs=[pl.BlockSpec((n, tr, c), lambda i, chip_ref: (0, i, 0)),
                            pl.BlockSpec((None, tr, c), lambda i, chip_ref: (chip_ref[0], i, 0))],
                  out_specs=pl.BlockSpec((tr, c), lambda i, chip_ref: (i, 0)),
                  out_shape=jax.ShapeDtypeStruct((r, c), F32), semantics=("parallel",),
                  block_bytes=(n + 3) * tr * c * 4, scalar_prefetch=1)(chip, landed, own)


def _adamw_quarters(core, w, g_mine, g_other, m, v, after, name):
    nl, nh, r, c = w.shape
    tr = _tile(r, max(SUBLANES, (1 << 19) // (c * 4)), SUBLANES)
    c1 = 1.0 - ADAM_B1 ** ADAM_STEP
    c2 = 1.0 - ADAM_B2 ** ADAM_STEP

    def body(core_ref, w_ref, *refs):
        g_refs, (m_ref, v_ref, _, g_ref, d_ref, nm_ref, nv_ref) = refs[:2 * nl], refs[2 * nl:]
        mine = pl.program_id(1) == core_ref[0]
        g = jnp.where(mine, g_refs[0][...], g_refs[nl][...])
        for l in range(1, nl):
            g = jnp.where(pl.program_id(0) == l, jnp.where(mine, g_refs[l][...], g_refs[nl + l][...]), g)
        nm = ADAM_B1 * m_ref[...] + (1.0 - ADAM_B1) * g
        nv = ADAM_B2 * v_ref[...] + (1.0 - ADAM_B2) * (g * g)
        g_ref[...] = g
        nm_ref[...] = nm
        nv_ref[...] = nv
        d_ref[...] = -ADAM_LR * ((nm / c1) / (jnp.sqrt(nv / c2) + ADAM_EPS) + ADAM_WD * w_ref[...])

    blk = pl.BlockSpec((None, None, tr, c), lambda l, hf, i, core_ref: (l, hf, i, 0))
    gblk = pl.BlockSpec((tr, c), lambda l, hf, i, core_ref: (i, 0))
    sh = jax.ShapeDtypeStruct(w.shape, F32)
    return _pcall(body, name=name, grid=(nl, nh, r // tr),
                  in_specs=[blk] + [gblk] * (2 * nl) + [blk, blk, pl.BlockSpec(memory_space=pl.ANY)],
                  out_specs=(blk,) * 4, out_shape=(sh,) * 4, semantics=("parallel", "parallel", "parallel"),
                  block_bytes=(7 + 2 * nl) * tr * c * 4, scalar_prefetch=1)(core, w, *g_mine, *g_other, m, v, after)


HBM_SPEC = pl.BlockSpec(memory_space=pltpu.HBM)


def _other_chips(x, y):
    return [(1 - x, y), (x, 1 - y), (1 - x, 1 - y)]


SEM_SPEC = pl.BlockSpec(memory_space=pltpu.SEMAPHORE)
DATAFLOW_EFFECT = pltpu.SideEffectType.DATAFLOW_SIDE_EFFECTING


def _split_start(name, bufs, n_copies, build):
    nb = len(bufs)

    def body(*refs):
        starts, _ = build(refs[:nb], refs[nb], refs[nb + 1])
        for cp in starts:
            cp.start()
        refs[-1][...] = jnp.zeros_like(refs[-1])

    out = pl.pallas_call(
        body, name=name,
        out_shape=(pltpu.SemaphoreType.DMA((n_copies,)), pltpu.SemaphoreType.DMA((n_copies,)),
                   *[pltpu.HBM(b.shape, b.dtype) for b in bufs], jax.ShapeDtypeStruct((SUBLANES, LANES), F32)),
        in_specs=[HBM_SPEC] * nb,
        out_specs=(SEM_SPEC, SEM_SPEC, *[HBM_SPEC] * nb, pl.BlockSpec(memory_space=pltpu.VMEM)),
        input_output_aliases={i: 2 + i for i in range(nb)},
        compiler_params=pltpu.CompilerParams(has_side_effects=DATAFLOW_EFFECT),
    )(*[pltpu.with_memory_space_constraint(b, pltpu.HBM) for b in bufs])
    return out[0], out[1], list(out[2:2 + nb]), out[2 + nb]


def _split_wait(name, send_sems, recv_sems, bufs, after, build):
    nb = len(bufs)

    def body(*refs):
        starts, waits = build(refs[:nb], refs[nb], refs[nb + 1])
        for cp in starts:
            cp.wait_send()
        for cp in waits:
            cp.wait_recv()

    out = pl.pallas_call(
        body, name=name, out_shape=tuple(pltpu.HBM(b.shape, b.dtype) for b in bufs),
        in_specs=[HBM_SPEC] * nb + [SEM_SPEC, SEM_SPEC, pl.BlockSpec(memory_space=pl.ANY)],
        out_specs=tuple([HBM_SPEC] * nb), input_output_aliases={i: i for i in range(nb)},
        compiler_params=pltpu.CompilerParams(has_side_effects=DATAFLOW_EFFECT),
    )(*bufs, send_sems, recv_sems, after)
    return list(out)


def _gather_ici_copies(nt, refs, send_sems, recv_sems):
    srcs, lands = refs[:nt], refs[nt:]
    x, y, c = lax.axis_index("x"), lax.axis_index("y"), lax.axis_index("c")
    me = 2 * x + y
    starts, waits = [], []
    for t in range(nt):
        for j, (px, py) in enumerate(_other_chips(x, y)):
            def copy(slot, t=t, j=j, px=px, py=py):
                return pltpu.make_async_remote_copy(
                    src_ref=srcs[t].at[c], dst_ref=lands[t].at[slot].at[c], send_sem=send_sems.at[3 * t + j],
                    recv_sem=recv_sems.at[3 * t + j], device_id=(px, py, c), device_id_type=pl.DeviceIdType.MESH)
            starts.append(copy(me))
            waits.append(copy(2 * px + py))
    return starts, waits


def _gather_d2d_copies(nt, refs, send_sems, recv_sems):
    x, y, c = lax.axis_index("x"), lax.axis_index("y"), lax.axis_index("c")
    starts, waits = [], []
    for t in range(nt):
        for j, (px, py) in enumerate(_other_chips(x, y)):
            def copy(half, t=t, j=j, px=px, py=py):
                place = refs[t].at[2 * px + py].at[half]
                return pltpu.make_async_remote_copy(
                    src_ref=place, dst_ref=place, send_sem=send_sems.at[3 * t + j], recv_sem=recv_sems.at[3 * t + j],
                    device_id=(x, y, 1 - c), device_id_type=pl.DeviceIdType.MESH)
            starts.append(copy(c))
            waits.append(copy(1 - c))
    return starts, waits


def _scatter_ici_copies(nt, refs, send_sems, recv_sems):
    srcs, lands = refs[:nt], refs[nt:]
    x, y, c = lax.axis_index("x"), lax.axis_index("y"), lax.axis_index("c")
    me = 2 * x + y
    starts, waits = [], []
    for t in range(nt):
        for j, (px, py) in enumerate(_other_chips(x, y)):
            def copy(slot, t=t, j=j, px=px, py=py):
                return pltpu.make_async_remote_copy(
                    src_ref=srcs[t].at[2 * px + py], dst_ref=lands[t].at[slot], send_sem=send_sems.at[3 * t + j],
                    recv_sem=recv_sems.at[3 * t + j], device_id=(px, py, c), device_id_type=pl.DeviceIdType.MESH)
            starts.append(copy(me))
            waits.append(copy(2 * px + py))
    return starts, waits


def _sibling_half_copies(nt, refs, send_sems, recv_sems):
    srcs, lands = refs[:nt], refs[nt:]
    x, y, c = lax.axis_index("x"), lax.axis_index("y"), lax.axis_index("c")
    copies = [pltpu.make_async_remote_copy(src_ref=srcs[t].at[1 - c], dst_ref=lands[t], send_sem=send_sems.at[t],
                                           recv_sem=recv_sems.at[t], device_id=(x, y, 1 - c),
                                           device_id_type=pl.DeviceIdType.MESH) for t in range(nt)]
    return copies, copies


def _sibling_exchange(arrs, other_layer, name):
    n = len(arrs)

    def body(*refs):
        ins, outs = refs[:n], refs[n:2 * n]
        send_sems, recv_sems = refs[2 * n:]
        c = lax.axis_index("c")
        sib = (lax.axis_index("x"), lax.axis_index("y"), 1 - c)
        copies = [pltpu.make_async_remote_copy(src_ref=ins[t].at[1 - c] if other_layer else ins[t], dst_ref=outs[t],
                                               send_sem=send_sems.at[t], recv_sem=recv_sems.at[t], device_id=sib,
                                               device_id_type=pl.DeviceIdType.MESH) for t in range(n)]
        for cp in copies:
            cp.start()
        for cp in copies:
            cp.wait_recv()
        for cp in copies:
            cp.wait_send()

    return pl.pallas_call(
        body, name=name, in_specs=[HBM_SPEC] * n, out_specs=(HBM_SPEC,) * n,
        out_shape=tuple(jax.ShapeDtypeStruct(a.shape[1:] if other_layer else a.shape, a.dtype) for a in arrs),
        scratch_shapes=[pltpu.SemaphoreType.DMA((n,)), pltpu.SemaphoreType.DMA((n,))])(*arrs)


def _all_devices_copies(refs, send_sems, recv_sems):
    src, land = refs
    x, y, c = lax.axis_index("x"), lax.axis_index("y"), lax.axis_index("c")
    me = 4 * x + 2 * y + c
    starts, waits = [], []
    for mask in range(1, N_DEVICES):
        px = 1 - x if mask & 4 else x
        py = 1 - y if mask & 2 else y
        pc = 1 - c if mask & 1 else c

        def copy(slot, mask=mask, px=px, py=py, pc=pc):
            return pltpu.make_async_remote_copy(
                src_ref=src, dst_ref=land.at[slot], send_sem=send_sems.at[mask - 1], recv_sem=recv_sems.at[mask - 1],
                device_id=(px, py, pc), device_id_type=pl.DeviceIdType.MESH)
        starts.append(copy(me))
        waits.append(copy(4 * px + 2 * py + pc))
    return starts, waits


def _all_devices_gather(buf, name):
    def body(in_ref, out_ref, send_sems, recv_sems, local_sem):
        x, y, c = lax.axis_index("x"), lax.axis_index("y"), lax.axis_index("c")
        me = 4 * x + 2 * y + c

        def peer(mask):
            px = 1 - x if mask & 4 else x
            py = 1 - y if mask & 2 else y
            pc = 1 - c if mask & 1 else c
            return px, py, pc

        def remote(mask, dst_slot):
            return pltpu.make_async_remote_copy(
                src_ref=in_ref, dst_ref=out_ref.at[dst_slot], send_sem=send_sems.at[mask - 1],
                recv_sem=recv_sems.at[mask - 1], device_id=peer(mask), device_id_type=pl.DeviceIdType.MESH)

        lc = pltpu.make_async_copy(in_ref, out_ref.at[me], local_sem)
        lc.start()
        sends = [remote(mask, me) for mask in range(1, N_DEVICES)]
        for cp in sends:
            cp.start()
        for mask in range(1, N_DEVICES):
            px, py, pc = peer(mask)
            remote(mask, 4 * px + 2 * py + pc).wait_recv()
        for cp in sends:
            cp.wait_send()
        lc.wait()

    return pl.pallas_call(
        body, name=name, in_specs=[HBM_SPEC], out_specs=HBM_SPEC,
        out_shape=jax.ShapeDtypeStruct((N_DEVICES,) + buf.shape, buf.dtype),
        scratch_shapes=[pltpu.SemaphoreType.DMA((N_DEVICES - 1,)), pltpu.SemaphoreType.DMA((N_DEVICES - 1,)),
                        pltpu.SemaphoreType.DMA])(buf)


def _pad_lanes(vec):
    return jnp.pad(vec.astype(F32), (0, LANES - vec.shape[0])).reshape(1, LANES)


def _layer_fwd(x, wl, fetch, dm, tag):
    heads, gw, lw, d = dm['heads'], dm['gw'], dm['lw'], dm['d']
    h = _rms_fwd(x, wl['attn_norm'], f"rms1_fwd{tag}")
    wl.update(fetch('in', h))
    proj = _matmul(h, wl['w_in_p'], mode='nn', tn=PROJ_COL_TILE, name=f"proj{tag}")
    alog, dtb = _pad_lanes(wl['gdn_a_log']), _pad_lanes(wl['gdn_dt_bias'])
    qkv = _gdn_pre_fwd(proj, wl['gdn_conv_w'], heads, f"gdn_pre_fwd{tag}")
    o, t_all, s0_all = _gdn_fwd(qkv, proj, dm['ab_blk'], alog, dtb, heads, f"gdn_fwd{tag}")
    o_gdn = _gdn_post_fwd(o, proj, dm['z_blk'], wl['gdn_norm'], f"gdn_post_fwd{tag}")
    xc = _conv_bias_fwd(proj, dm['xb_blk'], wl['lru_conv_w'], wl['lru_conv_b'], f"lru_conv_fwd{tag}")
    hseq, o_lru = _lru_fwd(xc, proj, dm['yb_blk'], wl['lru_w_a'], wl['lru_w_x'], wl['lru_b_a'], wl['lru_b_x'],
                           wl['lru_lambda'], f"lru_fwd{tag}")
    wl.update(fetch('mix', o))
    bg = _matmul(o_gdn, wl['w_branch_gdn'], mode='nn', name=f"branch_gdn{tag}")
    bl = _matmul(o_lru, wl['w_branch_lru'], mode='nn', name=f"branch_lru{tag}")
    merged = _merge_fwd(proj, dm['gg_blk'], dm['gl_blk'], bg, bl, f"merge_fwd{tag}")
    wl.update(fetch('mlp', bg))
    x_mid = _matmul(merged, wl['w_out'], mode='nn', add=x, name=f"out_proj{tag}")
    h2 = _rms_fwd(x_mid, wl['mlp_norm'], f"rms2_fwd{tag}")
    ur, act = _matmul(h2, wl['w_up'], mode='nn', epilogue='relu2', name=f"mlp_up{tag}")
    x_out = _matmul(act, wl['w_down'], mode='nn', add=x_mid, name=f"mlp_down{tag}")
    saved = dict(x=x, h=h, proj=proj, qkv=qkv, o=o, t_all=t_all, s0_all=s0_all, o_gdn=o_gdn, xc=xc, hseq=hseq,
                 o_lru=o_lru, bg=bg, bl=bl, merged=merged, x_mid=x_mid, h2=h2, ur=ur, act=act, alog=alog, dtb=dtb)
    return x_out, saved


def _layer_bwd(dx_out, dx_out_b, wl, sv, hook, dm, tag):
    heads, gw, lw, d = dm['heads'], dm['gw'], dm['lw'], dm['d']
    g = {}
    du = _matmul(dx_out_b, wl['w_down'], mode='nt', epilogue='mul2x', extra=sv['ur'], out_dtype=BF16,
                 name=f"d_mlp_act{tag}")
    def dw(n, lhs, rhs):
        return _matmul(lhs, rhs, mode='tn', out_dtype=BF16, shard_axis=BIG_SHARD_AXIS[n] - 1, name=f"d{n}{tag}")

    g['w_down'] = dw('w_down', sv['act'], dx_out_b)
    g['w_up'] = dw('w_up', sv['h2'], du)
    hook('mlp', g, wl, 'mlp_norm')
    dh2 = _matmul(du, wl['w_up'], mode='nt', name=f"d_h2{tag}")
    dx_mid, dx_mid_b, g['mlp_norm'] = _rms_bwd(sv['x_mid'], wl['mlp_norm'], dh2, dx_out, f"rms2_bwd{tag}")
    dmerged = _matmul(dx_mid_b, wl['w_out'], mode='nt', name=f"d_merged{tag}")
    g['w_out'] = dw('w_out', sv['merged'], dx_mid_b)
    dgg, dgl, dbg, dbl = _merge_bwd(sv['proj'], dm['gg_blk'], dm['gl_blk'], sv['bg'], sv['bl'], dmerged,
                                    f"merge_bwd{tag}")
    g['w_branch_gdn'] = dw('w_branch_gdn', sv['o_gdn'], dbg)
    g['w_branch_lru'] = dw('w_branch_lru', sv['o_lru'], dbl)
    hook('mix', g, wl, 'gdn_norm')
    do_gdn = _matmul(dbg, wl['w_branch_gdn'], mode='nt', name=f"d_o_gdn{tag}")
    do_lru = _matmul(dbl, wl['w_branch_lru'], mode='nt', name=f"d_o_lru{tag}")
    d_o, dz, dgn = _gdn_post_bwd(sv['o'], sv['proj'], dm['z_blk'], wl['gdn_norm'], do_gdn, f"gdn_post_bwd{tag}")
    g['gdn_norm'] = dgn.reshape(-1)
    dq, dk, dv, dgb = _gdn_bwd(sv['qkv'], sv['proj'], dm['ab_blk'], sv['alog'], sv['dtb'], sv['t_all'], sv['s0_all'], d_o, heads,
                               f"gdn_bwd{tag}")
    dconv = _gdn_pre_bwd(sv['proj'], wl['gdn_conv_w'], dq, dk, dv, heads, f"gdn_pre_bwd{tag}")
    dqkv, g['gdn_conv_w'], _ = _conv_bwd(dconv, sv['proj'], 0, wl['gdn_conv_w'], f"gdn_conv_bwd{tag}")
    dab, dab_sum = _dab_reduce(dgb, f"dab_reduce{tag}")
    g['gdn_dt_bias'] = dab_sum[0, :heads]
    g['gdn_a_log'] = dab_sum[0, 2 * heads:3 * heads]
    dxc, dyb, g['lru_w_a'], g['lru_w_x'], dba, dbx, dlam = _lru_bwd(
        do_lru, sv['xc'], sv['hseq'], sv['proj'], dm['yb_blk'], wl['lru_w_a'], wl['lru_w_x'], wl['lru_b_a'],
        wl['lru_b_x'], wl['lru_lambda'], f"lru_bwd{tag}")
    g['lru_b_a'], g['lru_b_x'], g['lru_lambda'] = dba.reshape(-1), dbx.reshape(-1), dlam.reshape(-1)
    dxb, g['lru_conv_w'], dcb = _conv_bwd(dxc, sv['proj'], dm['xb_blk'], wl['lru_conv_w'], f"lru_conv_bwd{tag}")
    g['lru_conv_b'] = dcb.reshape(-1)
    dproj = jnp.concatenate([dqkv.astype(BF16), dz.astype(BF16), dxb.astype(BF16), dyb.astype(BF16), dgg, dgl,
                             dab.astype(BF16), jnp.zeros((dab.shape[0], dm['np'] - dm['main'] - LANES), BF16)],
                            axis=1)
    g['w_in_p'] = _matmul(sv['h'], dproj, mode='tn', out_dtype=BF16, tn=PROJ_COL_TILE, name=f"dw_in{tag}")
    hook('in', g, wl, 'attn_norm')
    dh = _matmul(dproj, wl['w_in_p'], mode='nt', name=f"d_h{tag}")
    dx_in, dx_in_b, g['attn_norm'] = _rms_bwd(sv['x'], wl['attn_norm'], dh, dx_mid, f"rms1_bwd{tag}")
    g['attn_norm'] = g['attn_norm'].reshape(-1)
    g['mlp_norm'] = g['mlp_norm'].reshape(-1)
    return dx_in, dx_in_b, g


def _dims(d, heads, lw):
    gw = heads * LANES
    nab = 2 * heads
    blk = dict(z_blk=3 * heads, xb_blk=4 * heads, yb_blk=4 * heads + lw // LANES)
    gg0 = 4 * gw + 2 * lw
    main = gg0 + 2 * d
    return dict(d=d, heads=heads, gw=gw, lw=lw, nab=nab, gg_blk=gg0 // LANES, gl_blk=(gg0 + d) // LANES,
                main=main, ab_blk=main // LANES, np=-(-(main + LANES) // PROJ_COL_TILE) * PROJ_COL_TILE, **blk)


def _pad_w_in(w_in, dm):
    c0 = 4 * dm['gw']
    nab = dm['nab']
    return jnp.concatenate([w_in[:, :c0], w_in[:, c0 + nab:], w_in[:, c0:c0 + nab],
                            jnp.zeros((w_in.shape[0], dm['np'] - dm['main'] - nab), w_in.dtype)], axis=1)


def _unpad_w_in(gp, dm):
    c0 = 4 * dm['gw']
    nab = dm['nab']
    main = dm['main']
    return jnp.concatenate([gp[:, :c0], gp[:, main:main + nab], gp[:, c0:main]], axis=1)


def _local_step(x, target, layers, fetchers, hooks, final_norm, dm):
    saved = []
    cur = x
    for li, wl in enumerate(layers):
        cur, sv = _layer_fwd(cur, wl, fetchers[li], dm, f"_l{li}")
        saved.append(sv)
    loss_blk, dx, dx_b, dfin = _loss_head(cur, final_norm, target, "loss_head")
    grads = [None] * len(layers)
    for li in reversed(range(len(layers))):
        dx, dx_b, grads[li] = _layer_bwd(dx, dx_b, layers[li], saved[li], hooks[li], dm, f"_l{li}")
    return loss_blk[0, 0], dx, grads, dfin.reshape(-1)


def kernel(x, attn_norm, w_in, gdn_conv_w, gdn_a_log, gdn_dt_bias, gdn_norm, lru_conv_w, lru_conv_b, lru_w_a, lru_b_a, lru_w_x, lru_b_x, lru_lambda, w_branch_gdn, w_branch_lru, w_out, mlp_norm, w_up, w_down, final_norm, loss_target, m_attn_norm, m_w_in, m_gdn_conv_w, m_gdn_a_log, m_gdn_dt_bias, m_gdn_norm, m_lru_conv_w, m_lru_conv_b, m_lru_w_a, m_lru_b_a, m_lru_w_x, m_lru_b_x, m_lru_lambda, m_w_branch_gdn, m_w_branch_lru, m_w_out, m_mlp_norm, m_w_up, m_w_down, m_final_norm, v_attn_norm, v_w_in, v_gdn_conv_w, v_gdn_a_log, v_gdn_dt_bias, v_gdn_norm, v_lru_conv_w, v_lru_conv_b, v_lru_w_a, v_lru_b_a, v_lru_w_x, v_lru_b_x, v_lru_lambda, v_w_branch_gdn, v_w_branch_lru, v_w_out, v_mlp_norm, v_w_up, v_w_down, v_final_norm):
    w = dict(attn_norm=attn_norm, w_in=w_in, gdn_conv_w=gdn_conv_w, gdn_a_log=gdn_a_log, gdn_dt_bias=gdn_dt_bias,
             gdn_norm=gdn_norm, lru_conv_w=lru_conv_w, lru_conv_b=lru_conv_b, lru_w_a=lru_w_a, lru_b_a=lru_b_a,
             lru_w_x=lru_w_x, lru_b_x=lru_b_x, lru_lambda=lru_lambda, w_branch_gdn=w_branch_gdn,
             w_branch_lru=w_branch_lru, w_out=w_out, mlp_norm=mlp_norm, w_up=w_up, w_down=w_down,
             final_norm=final_norm)
    m = dict(attn_norm=m_attn_norm, w_in=m_w_in, gdn_conv_w=m_gdn_conv_w, gdn_a_log=m_gdn_a_log,
             gdn_dt_bias=m_gdn_dt_bias, gdn_norm=m_gdn_norm, lru_conv_w=m_lru_conv_w, lru_conv_b=m_lru_conv_b,
             lru_w_a=m_lru_w_a, lru_b_a=m_lru_b_a, lru_w_x=m_lru_w_x, lru_b_x=m_lru_b_x, lru_lambda=m_lru_lambda,
             w_branch_gdn=m_w_branch_gdn, w_branch_lru=m_w_branch_lru, w_out=m_w_out, mlp_norm=m_mlp_norm,
             w_up=m_w_up, w_down=m_w_down, final_norm=m_final_norm)
    v = dict(attn_norm=v_attn_norm, w_in=v_w_in, gdn_conv_w=v_gdn_conv_w, gdn_a_log=v_gdn_a_log,
             gdn_dt_bias=v_gdn_dt_bias, gdn_norm=v_gdn_norm, lru_conv_w=v_lru_conv_w, lru_conv_b=v_lru_conv_b,
             lru_w_a=v_lru_w_a, lru_b_a=v_lru_b_a, lru_w_x=v_lru_w_x, lru_b_x=v_lru_b_x, lru_lambda=v_lru_lambda,
             w_branch_gdn=v_w_branch_gdn, w_branch_lru=v_w_branch_lru, w_out=v_w_out, mlp_norm=v_mlp_norm,
             w_up=v_w_up, w_down=v_w_down, final_norm=v_final_norm)
    n_layers = attn_norm.shape[0]
    d = x.shape[-1]
    heads = gdn_a_log.shape[-1]
    lw = lru_conv_b.shape[-1]
    dm = _dims(d, heads, lw)
    big_names = list(BIG_SHARD_AXIS)
    conv_names = list(CONV_SHARD_AXIS)
    chip = 2 * lax.axis_index("x") + lax.axis_index("y")

    conv_flat = jnp.concatenate([w[n].reshape(-1) for n in conv_names])
    conv_rows = -(-conv_flat.shape[0] // (SUBLANES * LANES)) * SUBLANES
    conv_buf = jnp.pad(conv_flat, (0, conv_rows * LANES - conv_flat.shape[0])).reshape(conv_rows, LANES)
    conv_all = _all_devices_gather(conv_buf, "conv_allgather").reshape(N_CHIPS, 2, -1)[:, 0]
    conv_full, off = {}, 0
    for n in conv_names:
        shard = w[n]
        parts = conv_all[:, off:off + shard.size].reshape((N_CHIPS,) + shard.shape)
        conv_full[n] = jnp.concatenate([parts[q] for q in range(N_CHIPS)], axis=CONV_SHARD_AXIS[n])
        off += shard.size

    def start_gather(li, group):
        halves, lands = [], []
        for n in GATHER_GROUPS[group]:
            s = w[n][li].astype(BF16)
            hv = s.reshape((2, s.shape[0] // 2) + s.shape[1:])
            halves.append(hv)
            lands.append(lax.dynamic_update_index_in_dim(lax.empty((N_CHIPS,) + hv.shape, BF16), hv, chip, 0))
        nt = len(halves)
        return _split_start(f"wgather_{group}_l{li}_ici_start", halves + lands, 3 * nt,
                            functools.partial(_gather_ici_copies, nt))

    pending = {(li, group): start_gather(li, group) for li in range(n_layers) for group in GATHER_GROUPS}

    def make_fetch(li):
        def fetch(group, after):
            names = GATHER_GROUPS[group]
            nt = len(names)
            send, recv, bufs, _ = pending.pop((li, group))
            bufs = _split_wait(f"wgather_{group}_l{li}_ici_wait", send, recv, bufs, after,
                               functools.partial(_gather_ici_copies, nt))
            send, recv, lands, token = _split_start(f"wgather_{group}_l{li}_d2d_start", bufs[nt:], 3 * nt,
                                                    functools.partial(_gather_d2d_copies, nt))
            lands = _split_wait(f"wgather_{group}_l{li}_d2d_wait", send, recv, lands, token,
                                functools.partial(_gather_d2d_copies, nt))
            out = {}
            for n, land in zip(names, lands):
                slots = land.reshape((N_CHIPS, 2 * land.shape[2]) + land.shape[3:])
                out[n] = jnp.concatenate([slots[q] for q in range(N_CHIPS)], axis=BIG_SHARD_AXIS[n] - 1)
            if 'w_in' in out:
                out['w_in_p'] = _pad_w_in(out.pop('w_in'), dm)
            return out
        return fetch

    layers = []
    for li in range(n_layers):
        wl = {n: w[n][li] for n in SMALL_NAMES if n != 'final_norm' and n not in CONV_SHARD_AXIS}
        for n in conv_names:
            wl[n] = conv_full[n][li]
        layers.append(wl)
    layers[0]['attn_norm'] = layers[0]['attn_norm'] + sum(handle[3][0, 0] for handle in pending.values())

    core = lax.axis_index("c").astype(jnp.int32).reshape(1)
    chip_op = chip.astype(jnp.int32).reshape(1)
    sending, in_flight, reduced = {}, {}, {}

    def reduce_begin(group, li, g):
        names = GATHER_GROUPS[group]
        nt = len(names)
        contrib = []
        for n in names:
            if n != 'w_in':
                contrib.append(g[n])
                continue
            pieces = jnp.stack(jnp.split(_unpad_w_in(g['w_in_p'], dm), N_CHIPS, axis=BIG_SHARD_AXIS[n] - 1), axis=0)
            rows_half = pieces.shape[1] // 2
            contrib.append(jnp.swapaxes(pieces.reshape((N_CHIPS, 2, rows_half) + pieces.shape[2:]), 0, 1))
        theirs = [lax.empty(cb.shape[1:], BF16) for cb in contrib]
        send, recv, bufs, token = _split_start(f"gsend_{group}_l{li}_start", contrib + theirs, nt,
                                               functools.partial(_sibling_half_copies, nt))
        sending[(group, li)] = (send, recv, bufs)
        return token

    def reduce_scatter(group, li, after):
        names = GATHER_GROUPS[group]
        nt = len(names)
        send, recv, bufs = sending.pop((group, li))
        bufs = _split_wait(f"gsend_{group}_l{li}_wait", send, recv, bufs, after,
                           functools.partial(_sibling_half_copies, nt))
        sums = [_pair_sum(core, mine, th, f"gpair_{n}_l{li}") for n, mine, th in zip(names, bufs[:nt], bufs[nt:])]
        lands = [jnp.zeros(sm.shape, BF16) for sm in sums]
        send, recv, bufs, token = _split_start(f"gscatter_{group}_l{li}_start", sums + lands, 3 * nt,
                                               functools.partial(_scatter_ici_copies, nt))
        in_flight[(group, li)] = (send, recv, bufs)
        return token

    def reduce_end(jobs, after, name):
        keys, totals = [], []
        for group, li in jobs:
            names = GATHER_GROUPS[group]
            nt = len(names)
            send, recv, bufs = in_flight.pop((group, li))
            bufs = _split_wait(f"gscatter_{group}_l{li}_wait", send, recv, bufs, after,
                               functools.partial(_scatter_ici_copies, nt))
            totals += [_sum_landed(chip_op, land, own, f"gtotal_{n}_l{li}")
                       for n, own, land in zip(names, bufs[:nt], bufs[nt:])]
            keys += [(n, li) for n in names]
        others = _sibling_exchange(totals, False, name)
        for key, mine, other in zip(keys, totals, others):
            reduced[key] = (mine, other)

    order = [(group, li) for li in reversed(range(n_layers)) for group in reversed(list(GATHER_GROUPS))]

    def make_hook(li):
        def hook(group, g, wl, gain):
            at = order.index((group, li))
            latest = g['w_in_p'] if group == 'in' else g[GATHER_GROUPS[group][-1]]
            tie = reduce_begin(group, li, g)[0, 0]
            if at > 0:
                tie = tie + reduce_scatter(*order[at - 1], latest)[0, 0]
            wl[gain] = wl[gain] + tie
        return hook

    loss_local, dx, grads, dfin = _local_step(x[0], loss_target[0], layers, [make_fetch(li) for li in range(n_layers)],
                                              [make_hook(li) for li in range(n_layers)], final_norm, dm)
    loss = lax.psum(loss_local, MESH_AXES)
    last_scatter_token = reduce_scatter(*order[-1], dx)

    small_g = {n: jnp.stack([grads[li][n] for li in range(n_layers)], axis=0)
               for n in SMALL_NAMES if n != 'final_norm'}
    small_g['final_norm'] = dfin
    flat = jnp.concatenate([small_g[n].reshape(-1) for n in SMALL_NAMES])
    n_flat = flat.shape[0]
    row_unit = 32 * SUBLANES
    rows = -(-n_flat // (row_unit * LANES)) * row_unit
    buf = jnp.pad(flat, (0, rows * LANES - n_flat)).reshape(rows, LANES)
    device = 2 * chip + lax.axis_index("c")
    slots = lax.dynamic_update_index_in_dim(lax.empty((N_DEVICES, rows, LANES), F32), buf, device, 0)
    small_send, small_recv, small_bufs, small_token = _split_start("small_grad_start", [buf, slots], N_DEVICES - 1,
                                                                   _all_devices_copies)

    out_g, out_d, out_m, out_v = {}, {}, {}, {}

    def adamw_big(n, after):
        quarters = (n_layers, 2, w[n].shape[1] // 2, w[n].shape[2])
        res = _adamw_quarters(core, w[n].reshape(quarters), [reduced[(n, li)][0] for li in range(n_layers)],
                              [reduced[(n, li)][1] for li in range(n_layers)], m[n].reshape(quarters),
                              v[n].reshape(quarters), after, f"adamw_{n}")
        out_g[n], out_d[n], out_m[n], out_v[n] = (r.reshape(w[n].shape) for r in res)
        return res[1]

    early = [group for group in GATHER_GROUPS if group != 'in']
    last = small_token + last_scatter_token
    reduce_end([job for job in order if job[0] != 'in'], last, "gswap_early")
    for group in early:
        for n in GATHER_GROUPS[group]:
            last = adamw_big(n, last)
    reduce_end([job for job in order if job[0] == 'in'], last, "gswap_in")
    last = adamw_big('w_in', last)
    small_bufs = _split_wait("small_grad_wait", small_send, small_recv, small_bufs, last, _all_devices_copies)
    small_sum = _sum_slots(small_bufs[1], "small_grad_sum").reshape(-1)
    small_red = {}
    off = 0
    for n in SMALL_NAMES:
        size = small_g[n].size
        small_red[n] = small_sum[off:off + size].reshape(small_g[n].shape)
        off += size
    for n, ax in CONV_SHARD_AXIS.items():
        width = w[n].shape[ax]
        small_red[n] = lax.dynamic_slice_in_dim(small_red[n], chip * width, width, axis=ax)

    def pack(tree):
        fl = jnp.concatenate([tree[n].reshape(-1) for n in SMALL_NAMES])
        return jnp.pad(fl, (0, rows * LANES - fl.shape[0])).reshape(rows, LANES)

    res = _adamw(pack(w), [pack(small_red)], pack(m), pack(v), "adamw_small")
    for r, dst in zip(res, (out_g, out_d, out_m, out_v)):
        fl = r.reshape(-1)
        off = 0
        for n in SMALL_NAMES:
            dst[n] = fl[off:off + w[n].size].reshape(w[n].shape)
            off += w[n].size

    return (loss, dx[None], *[out_g[n] for n in WEIGHT_NAMES], *[out_d[n] for n in WEIGHT_NAMES],
            *[out_m[n] for n in WEIGHT_NAMES], *[out_v[n] for n in WEIGHT_NAMES])
```

```python
import functools

import jax
import jax.numpy as jnp
from jax import lax
from jax.experimental import pallas as pl
from jax.experimental.pallas import tpu as pltpu

F32 = jnp.float32
BF16 = jnp.bfloat16

LANES = 128
SUBLANES = 8
VMEM_BYTES = 64 * 1024 * 1024
GDN_CHUNK = 64
CONV_WIDTH = 4
CONV_ROW_TILE = 2048
RMS_EPS = 1e-6
L2_EPS = 1e-6
LRU_C = 8.0
ADAM_LR = 0.001
ADAM_B1 = 0.9
ADAM_B2 = 0.999
ADAM_EPS = 1e-08
ADAM_WD = 0.01
ADAM_STEP = 10
MESH_AXES = ("x", "y", "c")
N_CHIPS = 4
N_DEVICES = 8

INPUT_NAMES = ['x', 'attn_norm', 'w_in', 'gdn_conv_w', 'gdn_a_log', 'gdn_dt_bias', 'gdn_norm', 'lru_conv_w',
               'lru_conv_b', 'lru_w_a', 'lru_b_a', 'lru_w_x', 'lru_b_x', 'lru_lambda', 'w_branch_gdn',
               'w_branch_lru', 'w_out', 'mlp_norm', 'w_up', 'w_down', 'final_norm']
WEIGHT_NAMES = INPUT_NAMES[1:]
BIG_SHARD_AXIS = {'w_in': 2, 'w_branch_gdn': 2, 'w_branch_lru': 2, 'w_out': 1, 'w_up': 2, 'w_down': 1}
CONV_SHARD_AXIS = {'gdn_conv_w': 2, 'lru_conv_w': 2}
GATHER_GROUPS = {'in': ['w_in'], 'mix': ['w_branch_gdn', 'w_branch_lru', 'w_out'], 'mlp': ['w_up', 'w_down']}
SMALL_NAMES = [n for n in WEIGHT_NAMES if n not in BIG_SHARD_AXIS]
MID_NAMES = ['lru_w_a', 'lru_w_x']
PACK_NAMES = [n for n in SMALL_NAMES if n not in MID_NAMES]


def _tile(n, target, unit=LANES):
    best = None
    t = unit
    while t <= min(n, target):
        if n % t == 0:
            best = t
        t += unit
    return n if best is None else best


def _vmem_limit(block_bytes):
    return int(min(max(3 * block_bytes + (8 << 20), 24 << 20), VMEM_BYTES - (8 << 20)))


def _nbytes(shape, dtype):
    n = 1
    for s in shape:
        n *= s
    return n * jnp.dtype(dtype).itemsize


def _pcall(body, *, name, grid, in_specs, out_specs, out_shape, scratch_shapes=(), semantics=None, block_bytes=0,
           scalar_prefetch=0):
    params = dict(vmem_limit_bytes=_vmem_limit(block_bytes))
    if semantics is not None:
        params['dimension_semantics'] = semantics
    if scalar_prefetch:
        grid_spec = pltpu.PrefetchScalarGridSpec(num_scalar_prefetch=scalar_prefetch, grid=grid, in_specs=in_specs,
                                                 out_specs=out_specs, scratch_shapes=list(scratch_shapes))
        return pl.pallas_call(body, name=name, grid_spec=grid_spec, out_shape=out_shape,
                              compiler_params=pltpu.CompilerParams(**params))
    return pl.pallas_call(body, name=name, grid=grid, in_specs=in_specs, out_specs=out_specs, out_shape=out_shape,
                          scratch_shapes=list(scratch_shapes), compiler_params=pltpu.CompilerParams(**params))


def _dot(a, b):
    return jnp.dot(a.astype(BF16), b.astype(BF16), preferred_element_type=F32)


def _dot_nt(a, b):
    return lax.dot_general(a.astype(BF16), b.astype(BF16), (((1,), (1,)), ((), ())), preferred_element_type=F32)


def _dot_tn(a, b):
    return lax.dot_general(a.astype(BF16), b.astype(BF16), (((0,), (0,)), ((), ())), preferred_element_type=F32)


def _sigmoid(x):
    return 1.0 / (1.0 + jnp.exp(-x))


def _log1p(u):
    return jnp.where(u < 1e-3, u * (1.0 - u * (0.5 - u * (1.0 / 3.0))), jnp.log(1.0 + u))


def _softplus(x):
    return jnp.maximum(x, 0.0) + _log1p(jnp.exp(-jnp.abs(x)))


_GELU_K = 0.7978845608028654


def _gelu_and_grad(x):
    inner = _GELU_K * (x + 0.044715 * x * x * x)
    th = jnp.tanh(inner)
    g = 0.5 * x * (1.0 + th)
    dg = 0.5 * (1.0 + th) + 0.5 * x * (1.0 - th * th) * _GELU_K * (1.0 + 3.0 * 0.044715 * x * x)
    return g, dg


MATMUL_TK_MAX = 3584
PROJ_COL_TILE = 1536


def _matmul(a, b, *, mode, name, out_dtype=F32, add=None, epilogue=None, extra=None, tm=512, tn=1024, tk=2048,
            shard_axis=None):
    if mode == 'nn':
        (m, k), (k2, n) = a.shape, b.shape
    elif mode == 'nt':
        (m, k), (n, k2) = a.shape, b.shape
    else:
        (k, m), (k2, n) = a.shape, b.shape
    assert k == k2, (a.shape, b.shape, mode)
    if shard_axis is not None:
        rows_half = (m // N_CHIPS if shard_axis == 0 else m) // 2
        cols = n // N_CHIPS if shard_axis == 1 else n
        tm, tn = _tile(rows_half, tm), _tile(cols, tn)
    else:
        tm, tn = _tile(m, tm), _tile(n, tn)
    tk = _tile(k, tk)
    if k // tk > 2 * (-(-k // MATMUL_TK_MAX)):
        tk = _tile(k, MATMUL_TK_MAX)
    nk = k // tk
    dims = {'nn': (((1,), (0,)), ((), ())), 'nt': (((1,), (1,)), ((), ())), 'tn': (((0,), (0,)), ((), ()))}[mode]
    a_bytes, b_bytes = _nbytes(a.shape, a.dtype), _nbytes(b.shape, b.dtype)
    rows_outer = nk > 1 or a_bytes + (m // tm) * b_bytes <= b_bytes + (n // tn) * a_bytes

    def ij(g0, g1):
        return (g0, g1) if rows_outer else (g1, g0)

    def spec(shape, pick):
        return pl.BlockSpec(shape, lambda g0, g1, kk: pick(*ij(g0, g1), kk))

    a_spec = spec((tk, tm), lambda i, j, kk: (kk, i)) if mode == 'tn' else spec((tm, tk), lambda i, j, kk: (i, kk))
    b_spec = spec((tn, tk), lambda i, j, kk: (j, kk)) if mode == 'nt' else spec((tk, tn), lambda i, j, kk: (kk, j))
    o_spec = spec((tm, tn), lambda i, j, kk: (i, j))
    operands, in_specs = [a, b], [a_spec, b_spec]
    if add is not None:
        operands.append(add)
        in_specs.append(o_spec)
    if extra is not None:
        operands.append(extra)
        in_specs.append(o_spec)
    n_in = len(operands)
    if epilogue == 'relu2':
        out_shape = (jax.ShapeDtypeStruct((m, n), BF16), jax.ShapeDtypeStruct((m, n), BF16))
        out_specs = (o_spec, o_spec)
    elif shard_axis is not None:
        assert add is None and extra is None
        rb, cb = rows_half // tm, cols // tn

        def shard_block(i, j, kk):
            if shard_axis == 0:
                return (i % (2 * rb)) // rb, i // (2 * rb), i % rb, j
            return i // rb, j // cb, i % rb, j % cb

        out_shape = jax.ShapeDtypeStruct((2, N_CHIPS, rows_half, cols), out_dtype)
        out_specs = spec((None, None, tm, tn), shard_block)
    else:
        out_shape = jax.ShapeDtypeStruct((m, n), out_dtype)
        out_specs = o_spec

    def body(*refs):
        a_ref, b_ref = refs[0], refs[1]
        outs = refs[n_in:n_in + n_out]

        def finish(p):
            if add is not None:
                p = p + refs[2][...]
            if epilogue == 'relu2':
                ur = jnp.maximum(p, 0.0)
                outs[0][...] = ur.astype(BF16)
                outs[1][...] = (ur * ur).astype(BF16)
            elif epilogue == 'mul2x':
                outs[0][...] = (p * 2.0 * refs[n_in - 1][...].astype(F32)).astype(out_dtype)
            else:
                outs[0][...] = p.astype(out_dtype)

        prod = lax.dot_general(a_ref[...].astype(BF16), b_ref[...].astype(BF16), dims, preferred_element_type=F32)
        if nk == 1:
            finish(prod)
            return
        acc_ref = refs[-1]
        kk = pl.program_id(2)

        @pl.when(kk == 0)
        def _():
            acc_ref[...] = prod

        @pl.when((kk > 0) & (kk < nk - 1))
        def _():
            acc_ref[...] += prod

        @pl.when(kk == nk - 1)
        def _():
            finish(acc_ref[...] + prod)

    n_out = 2 if epilogue == 'relu2' else 1
    bb = (_nbytes((tm, tk), a.dtype) + _nbytes((tk, tn), b.dtype) + 3 * _nbytes((tm, tn), F32))
    grid = (m // tm, n // tn, nk) if rows_outer else (n // tn, m // tm, nk)
    return _pcall(body, name=name, grid=grid, in_specs=in_specs, out_specs=out_specs, out_shape=out_shape,
                  scratch_shapes=[pltpu.VMEM((tm, tn), F32)] if nk > 1 else [],
                  semantics=("parallel", "parallel", "arbitrary"), block_bytes=bb)(*operands)


def _row_tile(s, d, target_bytes=1 << 20):
    return _tile(s, max(SUBLANES, target_bytes // (4 * d)), SUBLANES)


def _rms_fwd(x, gain, name):
    s, d = x.shape
    tr = _row_tile(s, d)

    def body(x_ref, g_ref, h_ref):
        xv = x_ref[...]
        r = lax.rsqrt(jnp.mean(xv * xv, axis=-1, keepdims=True) + RMS_EPS)
        h_ref[...] = (xv * r * g_ref[...]).astype(BF16)

    row = pl.BlockSpec((tr, d), lambda i: (i, 0))
    return _pcall(body, name=name, grid=(s // tr,), in_specs=[row, pl.BlockSpec((1, d), lambda i: (0, 0))],
                  out_specs=row, out_shape=jax.ShapeDtypeStruct((s, d), BF16), semantics=("parallel",),
                  block_bytes=2 * tr * d * 4)(x, gain.reshape(1, d))


def _rms_bwd(x, gain, dh, dres, name):
    s, d = x.shape
    tr = _row_tile(s, d, 1 << 19)

    def body(x_ref, g_ref, dh_ref, dres_ref, dx_ref, dxb_ref, dg_ref):
        xv = x_ref[...]
        r = lax.rsqrt(jnp.mean(xv * xv, axis=-1, keepdims=True) + RMS_EPS)
        xh = xv * r
        dhv = dh_ref[...]
        dxh = dhv * g_ref[...]
        dx = dres_ref[...] + r * (dxh - xh * jnp.mean(dxh * xh, axis=-1, keepdims=True))
        dx_ref[...] = dx
        dxb_ref[...] = dx.astype(BF16)

        @pl.when(pl.program_id(0) == 0)
        def _():
            dg_ref[...] = jnp.zeros_like(dg_ref)

        dg_ref[...] += jnp.sum(dhv * xh, axis=0, keepdims=True)

    row = pl.BlockSpec((tr, d), lambda i: (i, 0))
    vec = pl.BlockSpec((1, d), lambda i: (0, 0))
    return _pcall(body, name=name, grid=(s // tr,), in_specs=[row, vec, row, row], out_specs=(row, row, vec),
                  out_shape=(jax.ShapeDtypeStruct((s, d), F32), jax.ShapeDtypeStruct((s, d), BF16),
                             jax.ShapeDtypeStruct((1, d), F32)),
                  semantics=("arbitrary",), block_bytes=5 * tr * d * 4)(x, gain.reshape(1, d), dh, dres)


def _loss_head(x, gain, target, name):
    s, d = x.shape
    tr = _row_tile(s, d, 1 << 19)

    def body(x_ref, g_ref, t_ref, loss_ref, dx_ref, dxb_ref, dg_ref):
        xv = x_ref[...]
        r = lax.rsqrt(jnp.mean(xv * xv, axis=-1, keepdims=True) + RMS_EPS)
        xh = xv * r
        gv = g_ref[...]
        err = xh * gv - t_ref[...]
        dy = err * (1.0 / d)
        dxh = dy * gv
        dx = r * (dxh - xh * jnp.mean(dxh * xh, axis=-1, keepdims=True))
        dx_ref[...] = dx
        dxb_ref[...] = dx.astype(BF16)

        @pl.when(pl.program_id(0) == 0)
        def _():
            dg_ref[...] = jnp.zeros_like(dg_ref)
            loss_ref[...] = jnp.zeros_like(loss_ref)

        dg_ref[...] += jnp.sum(dy * xh, axis=0, keepdims=True)
        part = jnp.sum(jnp.sum(err * err, axis=-1, keepdims=True), axis=0, keepdims=True) * (0.5 / d)
        loss_ref[...] += jnp.broadcast_to(part, loss_ref.shape)

    row = pl.BlockSpec((tr, d), lambda i: (i, 0))
    vec = pl.BlockSpec((1, d), lambda i: (0, 0))
    lspec = pl.BlockSpec((SUBLANES, LANES), lambda i: (0, 0))
    return _pcall(body, name=name, grid=(s // tr,), in_specs=[row, vec, row], out_specs=(lspec, row, row, vec),
                  out_shape=(jax.ShapeDtypeStruct((SUBLANES, LANES), F32), jax.ShapeDtypeStruct((s, d), F32),
                             jax.ShapeDtypeStruct((s, d), BF16), jax.ShapeDtypeStruct((1, d), F32)),
                  semantics=("arbitrary",), block_bytes=4 * tr * d * 4)(x, gain.reshape(1, d), target)


def _shift_down(xc, xp, s):
    tr = xc.shape[0]
    r = pltpu.roll(xc, s, 0)
    p = pltpu.roll(xp, s, 0)
    row8 = lax.broadcasted_iota(jnp.int32, (SUBLANES, xc.shape[1]), 0)
    head = jnp.where(row8 < s, p, r[:SUBLANES])
    if tr == SUBLANES:
        return head
    return jnp.concatenate([head, r[SUBLANES:]], axis=0)


def _shift_up(yc, yn, s):
    tr = yc.shape[0]
    u = pltpu.roll(yc, tr - s, 0)
    n = pltpu.roll(yn, SUBLANES - s, 0)
    row8 = lax.broadcasted_iota(jnp.int32, (SUBLANES, yc.shape[1]), 0)
    tail = jnp.where(row8 >= SUBLANES - s, n, u[tr - SUBLANES:])
    if tr == SUBLANES:
        return tail
    return jnp.concatenate([u[:tr - SUBLANES], tail], axis=0)


def _conv_apply(xc, xp, w):
    y = xc * w[CONV_WIDTH - 1:CONV_WIDTH, :]
    for s in range(1, CONV_WIDTH):
        y = y + _shift_down(xc, xp, s) * w[CONV_WIDTH - 1 - s:CONV_WIDTH - s, :]
    return y


def _halo_specs(tr, col_of):
    per = tr // SUBLANES
    cur = pl.BlockSpec((tr, LANES), lambda j, i: (i, col_of(j)))
    prev = pl.BlockSpec((SUBLANES, LANES), lambda j, i: (jnp.maximum(i * per - 1, 0), col_of(j)))
    return cur, prev


def _conv_bias_fwd(x_arr, x_col0, w, bias, name):
    s = x_arr.shape[0]
    ncb = w.shape[1] // LANES
    tr = _tile(s, CONV_ROW_TILE, SUBLANES)

    def body(cur_ref, prev_ref, w_ref, b_ref, o_ref):
        i = pl.program_id(1)
        xp = prev_ref[...] * (i > 0).astype(F32)
        o_ref[...] = _conv_apply(cur_ref[...], xp, w_ref[...]) + b_ref[...]

    cur, prev = _halo_specs(tr, lambda j: x_col0 + j)
    return _pcall(body, name=name, grid=(ncb, s // tr),
                  in_specs=[cur, prev, pl.BlockSpec((CONV_WIDTH, LANES), lambda j, i: (0, j)),
                            pl.BlockSpec((1, LANES), lambda j, i: (0, j))],
                  out_specs=pl.BlockSpec((tr, LANES), lambda j, i: (i, j)),
                  out_shape=jax.ShapeDtypeStruct((s, w.shape[1]), F32), semantics=("parallel", "parallel"),
                  block_bytes=3 * tr * LANES * 4)(x_arr, x_arr, w, bias.reshape(1, -1))


def _conv_bwd(dy, x_arr, x_col0, w, name):
    s, c = dy.shape
    ncb = c // LANES
    tr = _tile(s, CONV_ROW_TILE, SUBLANES)
    per = tr // SUBLANES
    ni = s // tr

    def body(dy_ref, dyn_ref, cur_ref, prev_ref, w_ref, dx_ref, dw_ref, db_ref):
        i = pl.program_id(1)
        dyv = dy_ref[...]
        dn = dyn_ref[...] * (i < ni - 1).astype(F32)
        xc = cur_ref[...]
        xp = prev_ref[...] * (i > 0).astype(F32)
        wv = w_ref[...]

        @pl.when(i == 0)
        def _():
            dw_ref[...] = jnp.zeros_like(dw_ref)
            db_ref[...] = jnp.zeros_like(db_ref)

        dx = dyv * wv[CONV_WIDTH - 1:CONV_WIDTH, :]
        dw_ref[CONV_WIDTH - 1:CONV_WIDTH, :] += jnp.sum(dyv * xc, axis=0, keepdims=True)
        for sh in range(1, CONV_WIDTH):
            j = CONV_WIDTH - 1 - sh
            dx = dx + _shift_up(dyv, dn, sh) * wv[j:j + 1, :]
            dw_ref[j:j + 1, :] += jnp.sum(dyv * _shift_down(xc, xp, sh), axis=0, keepdims=True)
        dx_ref[...] = dx
        db_ref[...] += jnp.sum(dyv, axis=0, keepdims=True)

    cur, prev = _halo_specs(tr, lambda j: x_col0 + j)
    dcur = pl.BlockSpec((tr, LANES), lambda j, i: (i, j))
    dnext = pl.BlockSpec((SUBLANES, LANES), lambda j, i: (jnp.minimum((i + 1) * per, s // SUBLANES - 1), j))
    return _pcall(body, name=name, grid=(ncb, ni),
                  in_specs=[dcur, dnext, cur, prev, pl.BlockSpec((CONV_WIDTH, LANES), lambda j, i: (0, j))],
                  out_specs=(dcur, pl.BlockSpec((CONV_WIDTH, LANES), lambda j, i: (0, j)),
                             pl.BlockSpec((1, LANES), lambda j, i: (0, j))),
                  out_shape=(jax.ShapeDtypeStruct((s, c), F32), jax.ShapeDtypeStruct((CONV_WIDTH, c), F32),
                             jax.ShapeDtypeStruct((1, c), F32)),
                  semantics=("parallel", "arbitrary"), block_bytes=4 * tr * LANES * 4)(dy, dy, x_arr, x_arr, w)


def _gdn_pre_fwd(proj, conv_w, heads, name):
    s = proj.shape[0]
    ncb = conv_w.shape[1] // LANES
    tr = _tile(s, CONV_ROW_TILE, SUBLANES)
    qscale = float(LANES) ** -0.5

    def body(cur_ref, prev_ref, w_ref, o_ref):
        j, i = pl.program_id(0), pl.program_id(1)
        xp = prev_ref[...] * (i > 0).astype(F32)
        cv = _conv_apply(cur_ref[...], xp, w_ref[...])
        sv = cv * _sigmoid(cv)
        nrm = lax.rsqrt(jnp.sum(sv * sv, axis=-1, keepdims=True) + L2_EPS)
        scale = jnp.where(j < heads, qscale, 1.0)
        o_ref[...] = jnp.where(j < 2 * heads, sv * nrm * scale, sv)

    cur, prev = _halo_specs(tr, lambda j: j)
    return _pcall(body, name=name, grid=(ncb, s // tr),
                  in_specs=[cur, prev, pl.BlockSpec((CONV_WIDTH, LANES), lambda j, i: (0, j))],
                  out_specs=pl.BlockSpec((tr, LANES), lambda j, i: (i, j)),
                  out_shape=jax.ShapeDtypeStruct((s, conv_w.shape[1]), F32), semantics=("parallel", "parallel"),
                  block_bytes=3 * tr * LANES * 4)(proj, proj, conv_w)


def _gdn_pre_bwd(proj, conv_w, dq, dk, dv, heads, name):
    s = proj.shape[0]
    ncb = conv_w.shape[1] // LANES
    tr = _tile(s, CONV_ROW_TILE, SUBLANES)
    qscale = float(LANES) ** -0.5

    def body(cur_ref, prev_ref, w_ref, dq_ref, dk_ref, dv_ref, o_ref):
        j, i = pl.program_id(0), pl.program_id(1)
        xp = prev_ref[...] * (i > 0).astype(F32)
        cv = _conv_apply(cur_ref[...], xp, w_ref[...])
        sg = _sigmoid(cv)
        sv = cv * sg
        nrm = lax.rsqrt(jnp.sum(sv * sv, axis=-1, keepdims=True) + L2_EPS)
        dv = jnp.where(j < heads, dq_ref[...], jnp.where(j < 2 * heads, dk_ref[...], dv_ref[...]))
        scale = jnp.where(j < heads, qscale, 1.0)
        dsn = scale * nrm * (dv - sv * (nrm * nrm) * jnp.sum(dv * sv, axis=-1, keepdims=True))
        ds = jnp.where(j < 2 * heads, dsn, dv)
        o_ref[...] = ds * (sg * (1.0 + cv * (1.0 - sg)))

    cur, prev = _halo_specs(tr, lambda j: j)
    blk = pl.BlockSpec((tr, LANES), lambda j, i: (i, j))

    def part(k):
        return pl.BlockSpec((tr, LANES), lambda j, i: (i, jnp.clip(j - k * heads, 0, heads - 1)))

    return _pcall(body, name=name, grid=(ncb, s // tr),
                  in_specs=[cur, prev, pl.BlockSpec((CONV_WIDTH, LANES), lambda j, i: (0, j)), part(0), part(1),
                            part(2)],
                  out_specs=blk, out_shape=jax.ShapeDtypeStruct((s, conv_w.shape[1]), F32),
                  semantics=("parallel", "parallel"), block_bytes=6 * tr * LANES * 4)(proj, proj, conv_w, dq, dk, dv)


def _dot_split(a, b, dims=(((1,), (0,)), ((), ()))):
    a_hi, b_hi = a.astype(BF16), b.astype(BF16)
    a_lo, b_lo = (a - a_hi.astype(F32)).astype(BF16), (b - b_hi.astype(F32)).astype(BF16)

    def dot(u, v):
        return lax.dot_general(u, v, dims, preferred_element_type=F32)

    return dot(a_hi, b_hi) + dot(a_hi, b_lo) + dot(a_lo, b_hi)


def _tri_inverse(a_strict, block):
    n = a_strict.shape[0]
    ri = lax.broadcasted_iota(jnp.int32, (n, n), 0)
    ci = lax.broadcasted_iota(jnp.int32, (n, n), 1)
    same8 = (ri >> 3) == (ci >> 3)
    sel = jnp.where((lax.broadcasted_iota(jnp.int32, (n, LANES), 0) & 7)
                    == lax.broadcasted_iota(jnp.int32, (n, LANES), 1), 1.0, 0.0)
    a8 = _dot_split(jnp.where(same8, a_strict, 0.0), sel)
    t8 = sel
    r_in = lax.broadcasted_iota(jnp.int32, (n, 1), 0) & 7
    for j in range(SUBLANES - 1):
        row_j = jnp.broadcast_to(t8.reshape(n // SUBLANES, SUBLANES, LANES)[:, j:j + 1, :],
                                 (n // SUBLANES, SUBLANES, LANES)).reshape(n, LANES)
        t8 = t8 - jnp.where(r_in > j, a8[:, j:j + 1] * row_j, 0.0)
    t = jnp.where(same8, _dot_split(t8, sel, (((1,), (1,)), ((), ()))), 0.0)
    size = SUBLANES
    while size < block:
        sh = size.bit_length() - 1
        lower_left = (((ri >> (sh + 1)) == (ci >> (sh + 1))) & (((ri >> sh) & 1) == 1) & (((ci >> sh) & 1) == 0))
        t = t - _dot_split(t, _dot_split(jnp.where(lower_left, a_strict, 0.0), t))
        size *= 2
    return t


GDN_HEAD_GROUP = 4
_CHUNK_SHIFT = GDN_CHUNK.bit_length() - 1
_LANE_SHIFT = LANES.bit_length() - 1


def _stack_heads(ref, hb):
    return jnp.concatenate([ref[:, i * LANES:(i + 1) * LANES] for i in range(hb)], axis=0)


def _diag_blocks(x, hb):
    c = GDN_CHUNK
    return jnp.concatenate([x[i * c:(i + 1) * c, i * LANES:(i + 1) * LANES] for i in range(hb)], axis=0)


def _expand_blocks(y, hb):
    row_blk = lax.shift_right_logical(lax.broadcasted_iota(jnp.int32, y.shape, 0), _CHUNK_SHIFT)
    return jnp.concatenate([jnp.where(row_blk == j, y, 0.0) for j in range(hb)], axis=1)


def _gdn_group_terms(q, k, v, ab, alog, dtb, head0, hb, heads):
    c = GDN_CHUNK
    r = hb * c
    lane = lax.broadcasted_iota(jnp.int32, (1, LANES), 1)

    def column(lane0):
        return jnp.concatenate([jnp.sum(jnp.where(lane == lane0 + head0 + i, ab, 0.0), axis=1, keepdims=True)
                                for i in range(hb)], axis=0)

    def per_head(vec):
        return jnp.concatenate([jnp.broadcast_to(jnp.sum(jnp.where(lane == head0 + i, vec, 0.0), axis=1,
                                                         keepdims=True), (c, 1)) for i in range(hb)], axis=0)

    pre = column(0) + per_head(dtb)
    neg_ea = -jnp.exp(per_head(alog))
    g = neg_ea * _softplus(pre)
    beta = _sigmoid(column(heads))
    ri = lax.broadcasted_iota(jnp.int32, (r, r), 0)
    ci = lax.broadcasted_iota(jnp.int32, (r, r), 1)
    same = lax.shift_right_logical(ri, _CHUNK_SHIFT) == lax.shift_right_logical(ci, _CHUNK_SHIFT)
    eye = ri == ci
    causal = same & (ri >= ci)
    strict = same & (ri > ci)
    g_row = jnp.sum(jnp.where(eye, g, 0.0), axis=0, keepdims=True)
    gc_col = jnp.sum(jnp.where(causal, g_row, 0.0), axis=1, keepdims=True)
    gc_row = jnp.sum(jnp.where(same & (ri <= ci), g, 0.0), axis=0, keepdims=True)
    gl_col = jnp.sum(jnp.where(same, g_row, 0.0), axis=1, keepdims=True)
    decay = jnp.where(causal, jnp.exp(jnp.where(causal, gc_col - gc_row, 0.0)), 0.0)
    e_last_col = jnp.exp(gl_col)
    e_last_lanes = jnp.concatenate([jnp.broadcast_to(e_last_col[i * c:i * c + 1, :], (1, LANES))
                                    for i in range(hb)], axis=1)
    egc = jnp.exp(gc_col)
    ekl = jnp.exp(gl_col - gc_col)
    kb = k * beta
    vb = v * beta
    kk = _dot_nt(kb, k)
    a_strict = jnp.where(strict, kk * decay, 0.0)
    return dict(pre=pre, neg_ea=neg_ea, g=g, beta=beta, ri=ri, ci=ci, same=same, eye=eye, causal=causal,
                strict=strict, decay=decay, e_last_col=e_last_col, e_last_lanes=e_last_lanes, egc=egc, ekl=ekl,
                kb=kb, vb=vb, kk=kk, a_strict=a_strict, lane=lane)


def _gdn_head_group(heads):
    hb = GDN_HEAD_GROUP
    while heads % hb:
        hb //= 2
    return hb


def _gdn_fwd(qkv, proj, ab_blk, alog, dtb, heads, name):
    s = qkv.shape[0]
    c = GDN_CHUNK
    nc = s // c
    hb = _gdn_head_group(heads)
    ng = heads // hb
    r = hb * c

    def body(q_ref, k_ref, v_ref, ab_ref, alog_ref, dtb_ref, o_ref, t_ref, s0_ref, state_ref):
        grp, ch = pl.program_id(0), pl.program_id(1)

        @pl.when(ch == 0)
        def _():
            state_ref[...] = jnp.zeros_like(state_ref)

        q, k, v = _stack_heads(q_ref, hb), _stack_heads(k_ref, hb), _stack_heads(v_ref, hb)
        tm = _gdn_group_terms(q, k, v, ab_ref[...], alog_ref[...], dtb_ref[...], grp * hb, hb, heads)
        t_inv = _tri_inverse(tm['a_strict'], c)
        u = _dot(t_inv, tm['vb'])
        w = _dot(t_inv, tm['kb'] * tm['egc'])
        qk = jnp.where(tm['causal'], _dot_nt(q, k) * tm['decay'], 0.0)
        st = state_ref[...]
        v_new = u - _diag_blocks(_dot(w, st), hb)
        out = _diag_blocks(_dot(q * tm['egc'], st), hb) + _dot(qk, v_new)
        for i in range(hb):
            o_ref[:, i * LANES:(i + 1) * LANES] = out[i * c:(i + 1) * c, :]
        t_ref[...] = t_inv
        s0_ref[...] = st
        state_ref[...] = st * tm['e_last_lanes'] + _dot_tn(k * tm['ekl'], _expand_blocks(v_new, hb))

    def blk(off):
        return pl.BlockSpec((c, hb * LANES), lambda g, n: (n, off * ng + g))

    vec = pl.BlockSpec((1, LANES), lambda g, n: (0, 0))
    return _pcall(
        body, name=name, grid=(ng, nc),
        in_specs=[blk(0), blk(1), blk(2), pl.BlockSpec((c, LANES), lambda g, n: (n, ab_blk)), vec, vec],
        out_specs=(blk(0), pl.BlockSpec((None, None, r, r), lambda g, n: (g, n, 0, 0)),
                   pl.BlockSpec((None, None, LANES, hb * LANES), lambda g, n: (g, n, 0, 0))),
        out_shape=(jax.ShapeDtypeStruct((s, heads * LANES), F32), jax.ShapeDtypeStruct((ng, nc, r, r), F32),
                   jax.ShapeDtypeStruct((ng, nc, LANES, hb * LANES), F32)),
        scratch_shapes=[pltpu.VMEM((LANES, hb * LANES), F32)], semantics=("parallel", "arbitrary"),
        block_bytes=8 * r * LANES * 4 + 2 * r * r * 4 + 2 * LANES * hb * LANES * 4)(qkv, qkv, qkv, proj, alog, dtb)


def _gdn_bwd(qkv, proj, ab_blk, alog, dtb, t_all, s0_all, d_o, heads, name):
    s = qkv.shape[0]
    c = GDN_CHUNK
    nc = s // c
    hb = _gdn_head_group(heads)
    ng = heads // hb
    r = hb * c

    def body(q_ref, k_ref, v_ref, ab_ref, alog_ref, dtb_ref, t_ref, s0_ref, do_ref,
             dq_ref, dk_ref, dv_ref, dgb_ref, ds_ref):
        grp, step = pl.program_id(0), pl.program_id(1)

        @pl.when(step == 0)
        def _():
            ds_ref[...] = jnp.zeros_like(ds_ref)

        q, k, v = _stack_heads(q_ref, hb), _stack_heads(k_ref, hb), _stack_heads(v_ref, hb)
        do = _stack_heads(do_ref, hb)
        tm = _gdn_group_terms(q, k, v, ab_ref[...], alog_ref[...], dtb_ref[...], grp * hb, hb, heads)
        ri, ci, same, eye = tm['ri'], tm['ci'], tm['same'], tm['eye']
        causal, strict, decay = tm['causal'], tm['strict'], tm['decay']
        egc, ekl, kb, vb, beta = tm['egc'], tm['ekl'], tm['kb'], tm['vb'], tm['beta']
        t_inv = t_ref[...]
        st = s0_ref[...]
        ds_next = ds_ref[...]
        kbg = kb * egc
        u = _dot(t_inv, vb)
        w = _dot(t_inv, kbg)
        qkm = _dot_nt(q, k)
        qk = jnp.where(causal, qkm * decay, 0.0)
        v_new = u - _diag_blocks(_dot(w, st), hb)
        qd = q * egc
        kd = k * ekl
        do_x = _expand_blocks(do, hb)

        dqd = _dot_nt(do_x, st)
        dqk = jnp.where(causal, _dot_nt(do, v_new), 0.0)
        dvn = _dot_tn(qk, do) + _diag_blocks(_dot(kd, ds_next), hb)
        dkd = _dot_nt(_expand_blocks(v_new, hb), ds_next)
        sd = jnp.sum(st * ds_next, axis=0, keepdims=True)
        dgl = jnp.concatenate([jnp.broadcast_to(jnp.sum(sd[:, i * LANES:(i + 1) * LANES], axis=1, keepdims=True),
                                                (c, 1)) for i in range(hb)], axis=0) * tm['e_last_col']
        dvn_x = _expand_blocks(dvn, hb)
        dw = -_dot_nt(dvn_x, st)
        ds_ref[...] = _dot_tn(qd, do_x) + tm['e_last_lanes'] * ds_next - _dot_tn(w, dvn_x)
        dt = _dot_nt(dvn, vb) + _dot_nt(dw, kbg)
        dvb = _dot_tn(t_inv, dvn)
        dkbg = _dot_tn(t_inv, dw)
        da_m = jnp.where(strict, -_dot_tn(t_inv, _dot_nt(dt, t_inv)), 0.0)
        dad = da_m * decay
        dkb = _dot(dad, k) + dkbg * egc
        dqkd = dqk * decay
        dq = _dot(dqkd, k) + dqd * egc
        dk = _dot_tn(dad, kb) + _dot_tn(dqkd, q) + dkd * ekl + dkb * beta
        e_mat = (da_m * tm['kk'] + dqk * qkm) * decay
        s_kd = jnp.sum(dkd * kd, axis=1, keepdims=True)
        s_kd_row = jnp.sum(jnp.where(eye, s_kd, 0.0), axis=0, keepdims=True)
        dgl = dgl + jnp.sum(jnp.where(same, s_kd_row, 0.0), axis=1, keepdims=True)
        col_sum = jnp.sum(e_mat, axis=0, keepdims=True)
        col_sum_c = jnp.sum(jnp.where(eye, col_sum, 0.0), axis=1, keepdims=True)
        dgc = (jnp.sum(e_mat, axis=1, keepdims=True) - col_sum_c + jnp.sum(dqd * qd, axis=1, keepdims=True)
               - s_kd + jnp.sum(dkbg * kbg, axis=1, keepdims=True))
        row_c = lax.broadcasted_iota(jnp.int32, (r, 1), 0)
        dgc = dgc + jnp.where((row_c & (c - 1)) == c - 1, dgl, 0.0)
        dgc_row = jnp.sum(jnp.where(eye, dgc, 0.0), axis=0, keepdims=True)
        dg = jnp.sum(jnp.where(same & (ci >= ri), dgc_row, 0.0), axis=1, keepdims=True)
        dbeta = jnp.sum(dkb * k, axis=1, keepdims=True) + jnp.sum(dvb * v, axis=1, keepdims=True)
        da_pre = dg * tm['neg_ea'] * _sigmoid(tm['pre'])
        db_pre = dbeta * beta * (1.0 - beta)
        lane = tm['lane']
        head_row = grp * hb + lax.shift_right_logical(row_c, _CHUNK_SHIFT)
        dgb = (jnp.where(lane == head_row, da_pre, 0.0) + jnp.where(lane == heads + head_row, db_pre, 0.0)
               + jnp.where(lane == 2 * heads + head_row, dg * tm['g'], 0.0))
        dvv = dvb * beta
        for i in range(hb):
            cols, rows = slice(i * LANES, (i + 1) * LANES), slice(i * c, (i + 1) * c)
            dq_ref[:, cols] = dq[rows, :]
            dk_ref[:, cols] = dk[rows, :]
            dv_ref[:, cols] = dvv[rows, :]
            dgb_ref[:, cols] = dgb[rows, :]

    def blk(off):
        return pl.BlockSpec((c, hb * LANES), lambda g, n: (nc - 1 - n, off * ng + g))

    vec = pl.BlockSpec((1, LANES), lambda g, n: (0, 0))
    gw = heads * LANES
    dq, dk, dv, dgb = _pcall(
        body, name=name, grid=(ng, nc),
        in_specs=[blk(0), blk(1), blk(2), pl.BlockSpec((c, LANES), lambda g, n: (nc - 1 - n, ab_blk)),
                  vec, vec, pl.BlockSpec((None, None, r, r), lambda g, n: (g, nc - 1 - n, 0, 0)),
                  pl.BlockSpec((None, None, LANES, hb * LANES), lambda g, n: (g, nc - 1 - n, 0, 0)), blk(0)],
        out_specs=(blk(0), blk(0), blk(0), blk(0)),
        out_shape=tuple(jax.ShapeDtypeStruct((s, gw), F32) for _ in range(4)),
        scratch_shapes=[pltpu.VMEM((LANES, hb * LANES), F32)], semantics=("parallel", "arbitrary"),
        block_bytes=12 * r * LANES * 4 + 2 * r * r * 4 + 2 * LANES * hb * LANES * 4)(
            qkv, qkv, qkv, proj, alog, dtb, t_all, s0_all, d_o)
    return dq, dk, dv, dgb


def _gdn_post_fwd(o, proj, z_col0, gain, name):
    s, gw = o.shape
    heads = gw // LANES
    tr = _tile(s, CONV_ROW_TILE, SUBLANES)

    def body(o_ref, z_ref, g_ref, y_ref):
        ov, zv = o_ref[...], z_ref[...]
        r = lax.rsqrt(jnp.mean(ov * ov, axis=-1, keepdims=True) + RMS_EPS)
        y_ref[...] = (ov * r * g_ref[...] * (zv * _sigmoid(zv))).astype(BF16)

    blk = pl.BlockSpec((tr, LANES), lambda i, h: (i, h))
    return _pcall(body, name=name, grid=(s // tr, heads),
                  in_specs=[blk, pl.BlockSpec((tr, LANES), lambda i, h: (i, z_col0 + h)),
                            pl.BlockSpec((1, LANES), lambda i, h: (0, 0))],
                  out_specs=blk, out_shape=jax.ShapeDtypeStruct((s, gw), BF16), semantics=("parallel", "parallel"),
                  block_bytes=3 * tr * LANES * 4)(o, proj, gain.reshape(1, LANES))


def _gdn_post_bwd(o, proj, z_col0, gain, dy, name):
    s, gw = o.shape
    heads = gw // LANES
    tr = _tile(s, CONV_ROW_TILE, SUBLANES)

    def body(o_ref, z_ref, g_ref, dy_ref, do_ref, dz_ref, dg_ref):
        ov, zv, gv, dyv = o_ref[...], z_ref[...], g_ref[...], dy_ref[...]
        r = lax.rsqrt(jnp.mean(ov * ov, axis=-1, keepdims=True) + RMS_EPS)
        nv = ov * r
        sg = _sigmoid(zv)
        sz = zv * sg
        dn = dyv * gv * sz
        do_ref[...] = r * (dn - nv * jnp.mean(dn * nv, axis=-1, keepdims=True))
        dz_ref[...] = dyv * nv * gv * (sg * (1.0 + zv * (1.0 - sg)))

        @pl.when((pl.program_id(0) == 0) & (pl.program_id(1) == 0))
        def _():
            dg_ref[...] = jnp.zeros_like(dg_ref)

        dg_ref[...] += jnp.sum(dyv * nv * sz, axis=0, keepdims=True)

    blk = pl.BlockSpec((tr, LANES), lambda i, h: (i, h))
    vec = pl.BlockSpec((1, LANES), lambda i, h: (0, 0))
    return _pcall(body, name=name, grid=(s // tr, heads),
                  in_specs=[blk, pl.BlockSpec((tr, LANES), lambda i, h: (i, z_col0 + h)), vec, blk],
                  out_specs=(blk, blk, vec),
                  out_shape=(jax.ShapeDtypeStruct((s, gw), F32), jax.ShapeDtypeStruct((s, gw), F32),
                             jax.ShapeDtypeStruct((1, LANES), F32)),
                  semantics=("arbitrary", "arbitrary"), block_bytes=6 * tr * LANES * 4)(
                      o, proj, gain.reshape(1, LANES), dy)


def _dab_reduce(dgb, name):
    s, gw = dgb.shape
    heads = gw // LANES
    tr = _tile(s, 512, SUBLANES)

    def body(d_ref, o_ref, cs_ref):
        acc = d_ref[:, 0:LANES]
        for h in range(1, heads):
            acc = acc + d_ref[:, h * LANES:(h + 1) * LANES]
        o_ref[...] = acc

        @pl.when(pl.program_id(0) == 0)
        def _():
            cs_ref[...] = jnp.zeros_like(cs_ref)

        cs_ref[...] += jnp.sum(acc, axis=0, keepdims=True)

    return _pcall(body, name=name, grid=(s // tr,), in_specs=[pl.BlockSpec((tr, gw), lambda i: (i, 0))],
                  out_specs=(pl.BlockSpec((tr, LANES), lambda i: (i, 0)), pl.BlockSpec((1, LANES), lambda i: (0, 0))),
                  out_shape=(jax.ShapeDtypeStruct((s, LANES), F32), jax.ShapeDtypeStruct((1, LANES), F32)),
                  semantics=("arbitrary",), block_bytes=tr * gw * 4)(dgb)


def _lru_gates(xc, wa, wx, ba, bx, lam):
    r = _sigmoid(_dot(xc, wa) + ba)
    ig = _sigmoid(_dot(xc, wx) + bx)
    sp = _softplus(-lam)
    log_a = -LRU_C * r * sp
    a = jnp.exp(log_a)
    e2 = jnp.exp(2.0 * log_a)
    mult = jnp.sqrt(jnp.maximum(1.0 - e2, 0.0))
    return r, ig, sp, a, e2, mult


def _lru_fwd(xc, proj, y_col0, wa, wx, ba, bx, lam, name):
    s, lw = xc.shape
    nb = lw // LANES
    tr = _tile(s, 256, SUBLANES)

    def body(xc_ref, y_ref, wa_ref, wx_ref, ba_ref, bx_ref, lam_ref, h_ref, o_ref, carry_ref):
        @pl.when(pl.program_id(1) == 0)
        def _():
            carry_ref[...] = jnp.zeros_like(carry_ref)

        xv = xc_ref[...]
        _, ig, _, a, _, mult = _lru_gates(xv, wa_ref[...], wx_ref[...], ba_ref[...], bx_ref[...], lam_ref[...])
        b = mult * (ig * xv)
        row = lax.broadcasted_iota(jnp.int32, (tr, LANES), 0)
        sh = 1
        while sh < tr:
            keep = row >= sh
            b = a * jnp.where(keep, pltpu.roll(b, sh, 0), 0.0) + b
            a = a * jnp.where(keep, pltpu.roll(a, sh, 0), 1.0)
            sh *= 2
        hv = a * carry_ref[0:1, :] + b
        h_ref[...] = hv
        carry_ref[...] = jnp.broadcast_to(hv[tr - 1:tr, :], carry_ref.shape)
        gy, _ = _gelu_and_grad(y_ref[...])
        o_ref[...] = (hv * gy).astype(BF16)

    blk = pl.BlockSpec((tr, LANES), lambda n, i: (i, n))
    wspec = pl.BlockSpec((None, LANES, LANES), lambda n, i: (n, 0, 0))
    vec = pl.BlockSpec((1, LANES), lambda n, i: (0, n))
    return _pcall(body, name=name, grid=(nb, s // tr),
                  in_specs=[blk, pl.BlockSpec((tr, LANES), lambda n, i: (i, y_col0 + n)), wspec, wspec, vec, vec, vec],
                  out_specs=(blk, blk),
                  out_shape=(jax.ShapeDtypeStruct((s, lw), F32), jax.ShapeDtypeStruct((s, lw), BF16)),
                  scratch_shapes=[pltpu.VMEM((SUBLANES, LANES), F32)], semantics=("parallel", "arbitrary"),
                  block_bytes=8 * tr * LANES * 4)(xc, proj, wa, wx, ba.reshape(1, lw), bx.reshape(1, lw),
                                                  lam.reshape(1, lw))


def _lru_bwd(d_out, xc, hseq, proj, y_col0, wa, wx, ba, bx, lam, name):
    s, lw = xc.shape
    nb = lw // LANES
    tr = _tile(s, 256, SUBLANES)
    per = tr // SUBLANES
    ni = s // tr
    nrow8 = s // SUBLANES

    def body(do_ref, xc_ref, xn_ref, h_ref, hp_ref, y_ref, wa_ref, wx_ref, ba_ref, bx_ref, lam_ref,
             dxc_ref, dy_ref, dwa_ref, dwx_ref, dba_ref, dbx_ref, dlam_ref, carry_ref):
        step = pl.program_id(1)
        tile = ni - 1 - step

        @pl.when(step == 0)
        def _():
            carry_ref[...] = jnp.zeros_like(carry_ref)
            dwa_ref[...] = jnp.zeros_like(dwa_ref)
            dwx_ref[...] = jnp.zeros_like(dwx_ref)
            dba_ref[...] = jnp.zeros_like(dba_ref)
            dbx_ref[...] = jnp.zeros_like(dbx_ref)
            dlam_ref[...] = jnp.zeros_like(dlam_ref)

        wav, wxv, bav, bxv, lamv = wa_ref[...], wx_ref[...], ba_ref[...], bx_ref[...], lam_ref[...]
        xv = xc_ref[...]
        r, ig, sp, a, e2, mult = _lru_gates(xv, wav, wxv, bav, bxv, lamv)
        a_next = _lru_gates(xn_ref[...], wav, wxv, bav, bxv, lamv)[3] * (tile < ni - 1).astype(F32)
        hv = h_ref[...]
        h_prev = _shift_down(hv, hp_ref[...] * (tile > 0).astype(F32), 1)
        yv = y_ref[...]
        gy, dgy = _gelu_and_grad(yv)
        dov = do_ref[...]
        dy_ref[...] = dov * hv * dgy
        coef = _shift_up(a, a_next, 1)
        bb = dov * gy
        row = lax.broadcasted_iota(jnp.int32, (tr, LANES), 0)
        sh = 1
        while sh < tr:
            keep = row < tr - sh
            bb = coef * jnp.where(keep, pltpu.roll(bb, tr - sh, 0), 0.0) + bb
            coef = coef * jnp.where(keep, pltpu.roll(coef, tr - sh, 0), 1.0)
            sh *= 2
        lam_t = coef * carry_ref[0:1, :] + bb
        carry_ref[...] = jnp.broadcast_to(lam_t[0:1, :], carry_ref.shape)
        d_a = lam_t * h_prev
        d_mult = lam_t * (ig * xv)
        d_ix = lam_t * mult
        d_la = d_a * a - d_mult * e2 / jnp.maximum(mult, 1e-30)
        d_r = d_la * (-LRU_C * sp)
        dlam_ref[...] += jnp.sum(d_la * (LRU_C * r) * _sigmoid(-lamv), axis=0, keepdims=True)
        d_pa = d_r * r * (1.0 - r)
        d_px = (d_ix * xv) * ig * (1.0 - ig)
        dxc_ref[...] = d_ix * ig + _dot_nt(d_pa, wav) + _dot_nt(d_px, wxv)
        dwa_ref[...] += _dot_tn(xv, d_pa)
        dwx_ref[...] += _dot_tn(xv, d_px)
        dba_ref[...] += jnp.sum(d_pa, axis=0, keepdims=True)
        dbx_ref[...] += jnp.sum(d_px, axis=0, keepdims=True)

    blk = pl.BlockSpec((tr, LANES), lambda n, i: (ni - 1 - i, n))
    nxt = pl.BlockSpec((SUBLANES, LANES), lambda n, i: (jnp.minimum((ni - i) * per, nrow8 - 1), n))
    prv = pl.BlockSpec((SUBLANES, LANES), lambda n, i: (jnp.maximum((ni - 1 - i) * per - 1, 0), n))
    wspec = pl.BlockSpec((None, LANES, LANES), lambda n, i: (n, 0, 0))
    vec = pl.BlockSpec((1, LANES), lambda n, i: (0, n))
    return _pcall(
        body, name=name, grid=(nb, ni),
        in_specs=[blk, blk, nxt, blk, prv, pl.BlockSpec((tr, LANES), lambda n, i: (ni - 1 - i, y_col0 + n)),
                  wspec, wspec, vec, vec, vec],
        out_specs=(blk, blk, wspec, wspec, vec, vec, vec),
        out_shape=(jax.ShapeDtypeStruct((s, lw), F32), jax.ShapeDtypeStruct((s, lw), F32),
                   jax.ShapeDtypeStruct((nb, LANES, LANES), F32), jax.ShapeDtypeStruct((nb, LANES, LANES), F32),
                   jax.ShapeDtypeStruct((1, lw), F32), jax.ShapeDtypeStruct((1, lw), F32),
                   jax.ShapeDtypeStruct((1, lw), F32)),
        scratch_shapes=[pltpu.VMEM((SUBLANES, LANES), F32)], semantics=("parallel", "arbitrary"),
        block_bytes=12 * tr * LANES * 4)(d_out, xc, xc, hseq, hseq, proj, wa, wx, ba.reshape(1, lw),
                                         bx.reshape(1, lw), lam.reshape(1, lw))


def _merge_fwd(proj, gg_col0, gl_col0, bg, bl, name):
    s, d = bg.shape
    tr, tc = _tile(s, 256, SUBLANES), _tile(d, 1024)
    cb = tc // LANES

    def body(gg_ref, gl_ref, bg_ref, bl_ref, o_ref):
        o_ref[...] = (_sigmoid(gg_ref[...]) * bg_ref[...] + _sigmoid(gl_ref[...]) * bl_ref[...]).astype(BF16)

    blk = pl.BlockSpec((tr, tc), lambda i, j: (i, j))
    return _pcall(body, name=name, grid=(s // tr, d // tc),
                  in_specs=[pl.BlockSpec((tr, tc), lambda i, j: (i, gg_col0 // cb + j)),
                            pl.BlockSpec((tr, tc), lambda i, j: (i, gl_col0 // cb + j)), blk, blk],
                  out_specs=blk, out_shape=jax.ShapeDtypeStruct((s, d), BF16), semantics=("parallel", "parallel"),
                  block_bytes=5 * tr * tc * 4)(proj, proj, bg, bl)


def _merge_bwd(proj, gg_col0, gl_col0, bg, bl, dm, name):
    s, d = bg.shape
    tr, tc = _tile(s, 256, SUBLANES), _tile(d, 1024)
    cb = tc // LANES

    def body(gg_ref, gl_ref, bg_ref, bl_ref, dm_ref, dgg_ref, dgl_ref, dbg_ref, dbl_ref):
        dmv = dm_ref[...]
        sg, sl = _sigmoid(gg_ref[...]), _sigmoid(gl_ref[...])
        dgg_ref[...] = (dmv * bg_ref[...] * sg * (1.0 - sg)).astype(BF16)
        dgl_ref[...] = (dmv * bl_ref[...] * sl * (1.0 - sl)).astype(BF16)
        dbg_ref[...] = (dmv * sg).astype(BF16)
        dbl_ref[...] = (dmv * sl).astype(BF16)

    blk = pl.BlockSpec((tr, tc), lambda i, j: (i, j))
    sh = jax.ShapeDtypeStruct((s, d), BF16)
    return _pcall(body, name=name, grid=(s // tr, d // tc),
                  in_specs=[pl.BlockSpec((tr, tc), lambda i, j: (i, gg_col0 // cb + j)),
                            pl.BlockSpec((tr, tc), lambda i, j: (i, gl_col0 // cb + j)), blk, blk, blk],
                  out_specs=(blk, blk, blk, blk), out_shape=(sh, sh, sh, sh), semantics=("parallel", "parallel"),
                  block_bytes=8 * tr * tc * 4)(proj, proj, bg, bl, dm)


def _sum_slots(slots, name):
    n, r, c = slots.shape
    tr = _tile(r, max(2 * SUBLANES, (1 << 19) // (c * 4)), 2 * SUBLANES)

    def body(s_ref, o_ref):
        acc = s_ref[0].astype(F32)
        for q in range(1, n):
            acc = acc + s_ref[q].astype(F32)
        o_ref[...] = acc

    return _pcall(body, name=name, grid=(r // tr,), in_specs=[pl.BlockSpec((n, tr, c), lambda i: (0, i, 0))],
                  out_specs=pl.BlockSpec((tr, c), lambda i: (i, 0)), out_shape=jax.ShapeDtypeStruct((r, c), F32),
                  semantics=("parallel",), block_bytes=(n + 1) * tr * c * 4)(slots)


def _adamw(w, g_parts, m, v, name):
    r, c = w.shape
    np_ = len(g_parts)
    tr = _tile(r, max(SUBLANES, (1 << 20) // (c * 4)), SUBLANES)
    c1 = 1.0 - ADAM_B1 ** ADAM_STEP
    c2 = 1.0 - ADAM_B2 ** ADAM_STEP

    def body(*refs):
        w_ref, m_ref, v_ref = refs[0], refs[1 + np_], refs[2 + np_]
        g_ref, d_ref, nm_ref, nv_ref = refs[3 + np_:]
        g = refs[1][...]
        for p in range(1, np_):
            g = g + refs[1 + p][...]
        nm = ADAM_B1 * m_ref[...] + (1.0 - ADAM_B1) * g
        nv = ADAM_B2 * v_ref[...] + (1.0 - ADAM_B2) * (g * g)
        g_ref[...] = g
        nm_ref[...] = nm
        nv_ref[...] = nv
        d_ref[...] = -ADAM_LR * ((nm / c1) / (jnp.sqrt(nv / c2) + ADAM_EPS) + ADAM_WD * w_ref[...])

    blk = pl.BlockSpec((tr, c), lambda i: (i, 0))
    sh = jax.ShapeDtypeStruct((r, c), F32)
    return _pcall(body, name=name, grid=(r // tr,), in_specs=[blk] * (3 + np_), out_specs=(blk,) * 4,
                  out_shape=(sh,) * 4, semantics=("parallel",), block_bytes=(7 + np_) * tr * c * 4)(
                      w, *g_parts, m, v)


def _pair_sum(core, mine, theirs, name):
    _, n, r, c = mine.shape
    tr = _tile(r, max(2 * SUBLANES, (1 << 19) // (c * 4)), 2 * SUBLANES)

    def body(core_ref, a_ref, b_ref, o_ref):
        o_ref[...] = (a_ref[...].astype(F32) + b_ref[...].astype(F32)).astype(BF16)

    return _pcall(body, name=name, grid=(n, r // tr),
                  in_specs=[pl.BlockSpec((None, None, tr, c), lambda q, i, core_ref: (core_ref[0], q, i, 0)),
                            pl.BlockSpec((None, tr, c), lambda q, i, core_ref: (q, i, 0))],
                  out_specs=pl.BlockSpec((None, tr, c), lambda q, i, core_ref: (q, i, 0)),
                  out_shape=jax.ShapeDtypeStruct((n, r, c), BF16), semantics=("parallel", "parallel"),
                  block_bytes=3 * tr * c * 4, scalar_prefetch=1)(core, mine, theirs)


def _sum_landed(chip, landed, own, name):
    n, r, c = landed.shape
    tr = _tile(r, max(2 * SUBLANES, (1 << 19) // (c * 4)), 2 * SUBLANES)

    def body(chip_ref, l_ref, o_ref, t_ref):
        acc = o_ref[...].astype(F32)
        for q in range(n):
            acc = acc + l_ref[q].astype(F32)
        t_ref[...] = acc

    return _pcall(body, name=name, grid=(r // tr,),
                  in_specs=[pl.BlockSpec((n, tr, c), lambda i, chip_ref: (0, i, 0)),
                            pl.BlockSpec((None, tr, c), lambda i, chip_ref: (chip_ref[0], i, 0))],
                  out_specs=pl.BlockSpec((tr, c), lambda i, chip_ref: (i, 0)),
                  out_shape=jax.ShapeDtypeStruct((r, c), F32), semantics=("parallel",),
                  block_bytes=(n + 3) * tr * c * 4, scalar_prefetch=1)(chip, landed, own)


def _adamw_quarters(core, w, g_mine, g_other, m, v, after, name):
    nl, nh, r, c = w.shape
    tr = _tile(r, max(SUBLANES, (1 << 19) // (c * 4)), SUBLANES)
    c1 = 1.0 - ADAM_B1 ** ADAM_STEP
    c2 = 1.0 - ADAM_B2 ** ADAM_STEP

    def body(core_ref, w_ref, *refs):
        g_refs, (m_ref, v_ref, _, g_ref, d_ref, nm_ref, nv_ref) = refs[:2 * nl], refs[2 * nl:]
        mine = pl.program_id(1) == core_ref[0]
        g = jnp.where(mine, g_refs[0][...], g_refs[nl][...])
        for l in range(1, nl):
            g = jnp.where(pl.program_id(0) == l, jnp.where(mine, g_refs[l][...], g_refs[nl + l][...]), g)
        nm = ADAM_B1 * m_ref[...] + (1.0 - ADAM_B1) * g
        nv = ADAM_B2 * v_ref[...] + (1.0 - ADAM_B2) * (g * g)
        g_ref[...] = g
        nm_ref[...] = nm
        nv_ref[...] = nv
        d_ref[...] = -ADAM_LR * ((nm / c1) / (jnp.sqrt(nv / c2) + ADAM_EPS) + ADAM_WD * w_ref[...])

    blk = pl.BlockSpec((None, None, tr, c), lambda l, hf, i, core_ref: (l, hf, i, 0))
    gblk = pl.BlockSpec((tr, c), lambda l, hf, i, core_ref: (i, 0))
    sh = jax.ShapeDtypeStruct(w.shape, F32)
    return _pcall(body, name=name, grid=(nl, nh, r // tr),
                  in_specs=[blk] + [gblk] * (2 * nl) + [blk, blk, pl.BlockSpec(memory_space=pl.ANY)],
                  out_specs=(blk,) * 4, out_shape=(sh,) * 4, semantics=("parallel", "parallel", "parallel"),
                  block_bytes=(7 + 2 * nl) * tr * c * 4, scalar_prefetch=1)(core, w, *g_mine, *g_other, m, v, after)


HBM_SPEC = pl.BlockSpec(memory_space=pltpu.HBM)


def _other_chips(x, y):
    return [(1 - x, y), (x, 1 - y), (1 - x, 1 - y)]


SEM_SPEC = pl.BlockSpec(memory_space=pltpu.SEMAPHORE)
DATAFLOW_EFFECT = pltpu.SideEffectType.DATAFLOW_SIDE_EFFECTING


def _split_start(name, bufs, n_copies, build):
    nb = len(bufs)

    def body(*refs):
        starts, _ = build(refs[:nb], refs[nb], refs[nb + 1])
        for cp in starts:
            cp.start()
        refs[-1][...] = jnp.zeros_like(refs[-1])

    out = pl.pallas_call(
        body, name=name,
        out_shape=(pltpu.SemaphoreType.DMA((n_copies,)), pltpu.SemaphoreType.DMA((n_copies,)),
                   *[pltpu.HBM(b.shape, b.dtype) for b in bufs], jax.ShapeDtypeStruct((SUBLANES, LANES), F32)),
        in_specs=[HBM_SPEC] * nb,
        out_specs=(SEM_SPEC, SEM_SPEC, *[HBM_SPEC] * nb, pl.BlockSpec(memory_space=pltpu.VMEM)),
        input_output_aliases={i: 2 + i for i in range(nb)},
        compiler_params=pltpu.CompilerParams(has_side_effects=DATAFLOW_EFFECT),
    )(*[pltpu.with_memory_space_constraint(b, pltpu.HBM) for b in bufs])
    return out[0], out[1], list(out[2:2 + nb]), out[2 + nb]


def _split_wait(name, send_sems, recv_sems, bufs, after, build):
    nb = len(bufs)

    def body(*refs):
        starts, waits = build(refs[:nb], refs[nb], refs[nb + 1])
        for cp in starts:
            cp.wait_send()
        for cp in waits:
            cp.wait_recv()

    out = pl.pallas_call(
        body, name=name, out_shape=tuple(pltpu.HBM(b.shape, b.dtype) for b in bufs),
        in_specs=[HBM_SPEC] * nb + [SEM_SPEC, SEM_SPEC, pl.BlockSpec(memory_space=pl.ANY)],
        out_specs=tuple([HBM_SPEC] * nb), input_output_aliases={i: i for i in range(nb)},
        compiler_params=pltpu.CompilerParams(has_side_effects=DATAFLOW_EFFECT),
    )(*bufs, send_sems, recv_sems, after)
    return list(out)


def _gather_ici_copies(nt, refs, send_sems, recv_sems):
    srcs, lands = refs[:nt], refs[nt:]
    x, y, c = lax.axis_index("x"), lax.axis_index("y"), lax.axis_index("c")
    me = 2 * x + y
    starts, waits = [], []
    for t in range(nt):
        for j, (px, py) in enumerate(_other_chips(x, y)):
            def copy(slot, t=t, j=j, px=px, py=py):
                return pltpu.make_async_remote_copy(
                    src_ref=srcs[t].at[c], dst_ref=lands[t].at[slot].at[c], send_sem=send_sems.at[3 * t + j],
                    recv_sem=recv_sems.at[3 * t + j], device_id=(px, py, c), device_id_type=pl.DeviceIdType.MESH)
            starts.append(copy(me))
            waits.append(copy(2 * px + py))
    return starts, waits


def _gather_d2d_copies(nt, refs, send_sems, recv_sems):
    x, y, c = lax.axis_index("x"), lax.axis_index("y"), lax.axis_index("c")
    starts, waits = [], []
    for t in range(nt):
        for j, (px, py) in enumerate(_other_chips(x, y)):
            def copy(half, t=t, j=j, px=px, py=py):
                place = refs[t].at[2 * px + py].at[half]
                return pltpu.make_async_remote_copy(
                    src_ref=place, dst_ref=place, send_sem=send_sems.at[3 * t + j], recv_sem=recv_sems.at[3 * t + j],
                    device_id=(x, y, 1 - c), device_id_type=pl.DeviceIdType.MESH)
            starts.append(copy(c))
            waits.append(copy(1 - c))
    return starts, waits


def _scatter_ici_copies(nt, refs, send_sems, recv_sems):
    srcs, lands = refs[:nt], refs[nt:]
    x, y, c = lax.axis_index("x"), lax.axis_index("y"), lax.axis_index("c")
    me = 2 * x + y
    starts, waits = [], []
    for t in range(nt):
        for j, (px, py) in enumerate(_other_chips(x, y)):
            def copy(slot, t=t, j=j, px=px, py=py):
                return pltpu.make_async_remote_copy(
                    src_ref=srcs[t].at[2 * px + py], dst_ref=lands[t].at[slot], send_sem=send_sems.at[3 * t + j],
                    recv_sem=recv_sems.at[3 * t + j], device_id=(px, py, c), device_id_type=pl.DeviceIdType.MESH)
            starts.append(copy(me))
            waits.append(copy(2 * px + py))
    return starts, waits


def _sibling_half_copies(nt, refs, send_sems, recv_sems):
    srcs, lands = refs[:nt], refs[nt:]
    x, y, c = lax.axis_index("x"), lax.axis_index("y"), lax.axis_index("c")
    copies = [pltpu.make_async_remote_copy(src_ref=srcs[t].at[1 - c], dst_ref=lands[t], send_sem=send_sems.at[t],
                                           recv_sem=recv_sems.at[t], device_id=(x, y, 1 - c),
                                           device_id_type=pl.DeviceIdType.MESH) for t in range(nt)]
    return copies, copies


def _sibling_exchange(arrs, other_layer, name):
    n = len(arrs)

    def body(*refs):
        ins, outs = refs[:n], refs[n:2 * n]
        send_sems, recv_sems = refs[2 * n:]
        c = lax.axis_index("c")
        sib = (lax.axis_index("x"), lax.axis_index("y"), 1 - c)
        copies = [pltpu.make_async_remote_copy(src_ref=ins[t].at[1 - c] if other_layer else ins[t], dst_ref=outs[t],
                                               send_sem=send_sems.at[t], recv_sem=recv_sems.at[t], device_id=sib,
                                               device_id_type=pl.DeviceIdType.MESH) for t in range(n)]
        for cp in copies:
            cp.start()
        for cp in copies:
            cp.wait_recv()
        for cp in copies:
            cp.wait_send()

    return pl.pallas_call(
        body, name=name, in_specs=[HBM_SPEC] * n, out_specs=(HBM_SPEC,) * n,
        out_shape=tuple(jax.ShapeDtypeStruct(a.shape[1:] if other_layer else a.shape, a.dtype) for a in arrs),
        scratch_shapes=[pltpu.SemaphoreType.DMA((n,)), pltpu.SemaphoreType.DMA((n,))])(*arrs)


def _all_devices_copies(nt, refs, send_sems, recv_sems):
    srcs, lands = refs[:nt], refs[nt:]
    x, y, c = lax.axis_index("x"), lax.axis_index("y"), lax.axis_index("c")
    me = 4 * x + 2 * y + c
    starts, waits = [], []
    for t in range(nt):
        for mask in range(1, N_DEVICES):
            px = 1 - x if mask & 4 else x
            py = 1 - y if mask & 2 else y
            pc = 1 - c if mask & 1 else c
            k = (N_DEVICES - 1) * t + mask - 1

            def copy(slot, t=t, k=k, px=px, py=py, pc=pc):
                return pltpu.make_async_remote_copy(
                    src_ref=srcs[t], dst_ref=lands[t].at[slot], send_sem=send_sems.at[k], recv_sem=recv_sems.at[k],
                    device_id=(px, py, pc), device_id_type=pl.DeviceIdType.MESH)
            starts.append(copy(me))
            waits.append(copy(4 * px + 2 * py + pc))
    return starts, waits


def _all_devices_gather(buf, name):
    def body(in_ref, out_ref, send_sems, recv_sems, local_sem):
        x, y, c = lax.axis_index("x"), lax.axis_index("y"), lax.axis_index("c")
        me = 4 * x + 2 * y + c

        def peer(mask):
            px = 1 - x if mask & 4 else x
            py = 1 - y if mask & 2 else y
            pc = 1 - c if mask & 1 else c
            return px, py, pc

        def remote(mask, dst_slot):
            return pltpu.make_async_remote_copy(
                src_ref=in_ref, dst_ref=out_ref.at[dst_slot], send_sem=send_sems.at[mask - 1],
                recv_sem=recv_sems.at[mask - 1], device_id=peer(mask), device_id_type=pl.DeviceIdType.MESH)

        lc = pltpu.make_async_copy(in_ref, out_ref.at[me], local_sem)
        lc.start()
        sends = [remote(mask, me) for mask in range(1, N_DEVICES)]
        for cp in sends:
            cp.start()
        for mask in range(1, N_DEVICES):
            px, py, pc = peer(mask)
            remote(mask, 4 * px + 2 * py + pc).wait_recv()
        for cp in sends:
            cp.wait_send()
        lc.wait()

    return pl.pallas_call(
        body, name=name, in_specs=[HBM_SPEC], out_specs=HBM_SPEC,
        out_shape=jax.ShapeDtypeStruct((N_DEVICES,) + buf.shape, buf.dtype),
        scratch_shapes=[pltpu.SemaphoreType.DMA((N_DEVICES - 1,)), pltpu.SemaphoreType.DMA((N_DEVICES - 1,)),
                        pltpu.SemaphoreType.DMA])(buf)


def _pad_lanes(vec):
    return jnp.pad(vec.astype(F32), (0, LANES - vec.shape[0])).reshape(1, LANES)


def _layer_fwd(x, wl, fetch, dm, tag):
    heads, gw, lw, d = dm['heads'], dm['gw'], dm['lw'], dm['d']
    h = _rms_fwd(x, wl['attn_norm'], f"rms1_fwd{tag}")
    wl.update(fetch('in', h))
    proj = _matmul(h, wl['w_in_p'], mode='nn', tn=PROJ_COL_TILE, name=f"proj{tag}")
    alog, dtb = _pad_lanes(wl['gdn_a_log']), _pad_lanes(wl['gdn_dt_bias'])
    qkv = _gdn_pre_fwd(proj, wl['gdn_conv_w'], heads, f"gdn_pre_fwd{tag}")
    o, t_all, s0_all = _gdn_fwd(qkv, proj, dm['ab_blk'], alog, dtb, heads, f"gdn_fwd{tag}")
    o_gdn = _gdn_post_fwd(o, proj, dm['z_blk'], wl['gdn_norm'], f"gdn_post_fwd{tag}")
    xc = _conv_bias_fwd(proj, dm['xb_blk'], wl['lru_conv_w'], wl['lru_conv_b'], f"lru_conv_fwd{tag}")
    hseq, o_lru = _lru_fwd(xc, proj, dm['yb_blk'], wl['lru_w_a'], wl['lru_w_x'], wl['lru_b_a'], wl['lru_b_x'],
                           wl['lru_lambda'], f"lru_fwd{tag}")
    wl.update(fetch('mix', o))
    bg = _matmul(o_gdn, wl['w_branch_gdn'], mode='nn', name=f"branch_gdn{tag}")
    bl = _matmul(o_lru, wl['w_branch_lru'], mode='nn', name=f"branch_lru{tag}")
    merged = _merge_fwd(proj, dm['gg_blk'], dm['gl_blk'], bg, bl, f"merge_fwd{tag}")
    wl.update(fetch('mlp', bg))
    x_mid = _matmul(merged, wl['w_out'], mode='nn', add=x, name=f"out_proj{tag}")
    h2 = _rms_fwd(x_mid, wl['mlp_norm'], f"rms2_fwd{tag}")
    ur, act = _matmul(h2, wl['w_up'], mode='nn', epilogue='relu2', name=f"mlp_up{tag}")
    x_out = _matmul(act, wl['w_down'], mode='nn', add=x_mid, name=f"mlp_down{tag}")
    saved = dict(x=x, h=h, proj=proj, qkv=qkv, o=o, t_all=t_all, s0_all=s0_all, o_gdn=o_gdn, xc=xc, hseq=hseq,
                 o_lru=o_lru, bg=bg, bl=bl, merged=merged, x_mid=x_mid, h2=h2, ur=ur, act=act, alog=alog, dtb=dtb)
    return x_out, saved


def _layer_bwd(dx_out, dx_out_b, wl, sv, hook, dm, tag):
    heads, gw, lw, d = dm['heads'], dm['gw'], dm['lw'], dm['d']
    g = {}
    du = _matmul(dx_out_b, wl['w_down'], mode='nt', epilogue='mul2x', extra=sv['ur'], out_dtype=BF16,
                 name=f"d_mlp_act{tag}")
    def dw(n, lhs, rhs):
        return _matmul(lhs, rhs, mode='tn', out_dtype=BF16, shard_axis=BIG_SHARD_AXIS[n] - 1, name=f"d{n}{tag}")

    g['w_down'] = dw('w_down', sv['act'], dx_out_b)
    g['w_up'] = dw('w_up', sv['h2'], du)
    hook('mlp', g, wl, 'mlp_norm')
    dh2 = _matmul(du, wl['w_up'], mode='nt', name=f"d_h2{tag}")
    dx_mid, dx_mid_b, g['mlp_norm'] = _rms_bwd(sv['x_mid'], wl['mlp_norm'], dh2, dx_out, f"rms2_bwd{tag}")
    dmerged = _matmul(dx_mid_b, wl['w_out'], mode='nt', name=f"d_merged{tag}")
    g['w_out'] = dw('w_out', sv['merged'], dx_mid_b)
    dgg, dgl, dbg, dbl = _merge_bwd(sv['proj'], dm['gg_blk'], dm['gl_blk'], sv['bg'], sv['bl'], dmerged,
                                    f"merge_bwd{tag}")
    g['w_branch_gdn'] = dw('w_branch_gdn', sv['o_gdn'], dbg)
    g['w_branch_lru'] = dw('w_branch_lru', sv['o_lru'], dbl)
    hook('mix', g, wl, 'gdn_norm')
    do_gdn = _matmul(dbg, wl['w_branch_gdn'], mode='nt', name=f"d_o_gdn{tag}")
    do_lru = _matmul(dbl, wl['w_branch_lru'], mode='nt', name=f"d_o_lru{tag}")
    d_o, dz, dgn = _gdn_post_bwd(sv['o'], sv['proj'], dm['z_blk'], wl['gdn_norm'], do_gdn, f"gdn_post_bwd{tag}")
    g['gdn_norm'] = dgn.reshape(-1)
    dq, dk, dv, dgb = _gdn_bwd(sv['qkv'], sv['proj'], dm['ab_blk'], sv['alog'], sv['dtb'], sv['t_all'], sv['s0_all'], d_o, heads,
                               f"gdn_bwd{tag}")
    dconv = _gdn_pre_bwd(sv['proj'], wl['gdn_conv_w'], dq, dk, dv, heads, f"gdn_pre_bwd{tag}")
    dqkv, g['gdn_conv_w'], _ = _conv_bwd(dconv, sv['proj'], 0, wl['gdn_conv_w'], f"gdn_conv_bwd{tag}")
    dab, dab_sum = _dab_reduce(dgb, f"dab_reduce{tag}")
    g['gdn_dt_bias'] = dab_sum[0, :heads]
    g['gdn_a_log'] = dab_sum[0, 2 * heads:3 * heads]
    dxc, dyb, g['lru_w_a'], g['lru_w_x'], dba, dbx, dlam = _lru_bwd(
        do_lru, sv['xc'], sv['hseq'], sv['proj'], dm['yb_blk'], wl['lru_w_a'], wl['lru_w_x'], wl['lru_b_a'],
        wl['lru_b_x'], wl['lru_lambda'], f"lru_bwd{tag}")
    g['lru_b_a'], g['lru_b_x'], g['lru_lambda'] = dba.reshape(-1), dbx.reshape(-1), dlam.reshape(-1)
    dxb, g['lru_conv_w'], dcb = _conv_bwd(dxc, sv['proj'], dm['xb_blk'], wl['lru_conv_w'], f"lru_conv_bwd{tag}")
    g['lru_conv_b'] = dcb.reshape(-1)
    dproj = jnp.concatenate([dqkv.astype(BF16), dz.astype(BF16), dxb.astype(BF16), dyb.astype(BF16), dgg, dgl,
                             dab.astype(BF16), jnp.zeros((dab.shape[0], dm['np'] - dm['main'] - LANES), BF16)],
                            axis=1)
    g['w_in_p'] = _matmul(sv['h'], dproj, mode='tn', out_dtype=BF16, tn=PROJ_COL_TILE, name=f"dw_in{tag}")
    hook('in', g, wl, 'attn_norm')
    dh = _matmul(dproj, wl['w_in_p'], mode='nt', name=f"d_h{tag}")
    dx_in, dx_in_b, g['attn_norm'] = _rms_bwd(sv['x'], wl['attn_norm'], dh, dx_mid, f"rms1_bwd{tag}")
    g['attn_norm'] = g['attn_norm'].reshape(-1)
    g['mlp_norm'] = g['mlp_norm'].reshape(-1)
    return dx_in, dx_in_b, g


def _dims(d, heads, lw):
    gw = heads * LANES
    nab = 2 * heads
    blk = dict(z_blk=3 * heads, xb_blk=4 * heads, yb_blk=4 * heads + lw // LANES)
    gg0 = 4 * gw + 2 * lw
    main = gg0 + 2 * d
    return dict(d=d, heads=heads, gw=gw, lw=lw, nab=nab, gg_blk=gg0 // LANES, gl_blk=(gg0 + d) // LANES,
                main=main, ab_blk=main // LANES, np=-(-(main + LANES) // PROJ_COL_TILE) * PROJ_COL_TILE, **blk)


def _pad_w_in(w_in, dm):
    c0 = 4 * dm['gw']
    nab = dm['nab']
    return jnp.concatenate([w_in[:, :c0], w_in[:, c0 + nab:], w_in[:, c0:c0 + nab],
                            jnp.zeros((w_in.shape[0], dm['np'] - dm['main'] - nab), w_in.dtype)], axis=1)


def _unpad_w_in(gp, dm):
    c0 = 4 * dm['gw']
    nab = dm['nab']
    main = dm['main']
    return jnp.concatenate([gp[:, :c0], gp[:, main:main + nab], gp[:, c0:main]], axis=1)


def _local_step(x, target, layers, fetchers, hooks, final_norm, dm):
    saved = []
    cur = x
    for li, wl in enumerate(layers):
        cur, sv = _layer_fwd(cur, wl, fetchers[li], dm, f"_l{li}")
        saved.append(sv)
    loss_blk, dx, dx_b, dfin = _loss_head(cur, final_norm, target, "loss_head")
    grads = [None] * len(layers)
    for li in reversed(range(len(layers))):
        dx, dx_b, grads[li] = _layer_bwd(dx, dx_b, layers[li], saved[li], hooks[li], dm, f"_l{li}")
    return loss_blk[0, 0], dx, grads, dfin.reshape(-1)


def kernel(x, attn_norm, w_in, gdn_conv_w, gdn_a_log, gdn_dt_bias, gdn_norm, lru_conv_w, lru_conv_b, lru_w_a, lru_b_a, lru_w_x, lru_b_x, lru_lambda, w_branch_gdn, w_branch_lru, w_out, mlp_norm, w_up, w_down, final_norm, loss_target, m_attn_norm, m_w_in, m_gdn_conv_w, m_gdn_a_log, m_gdn_dt_bias, m_gdn_norm, m_lru_conv_w, m_lru_conv_b, m_lru_w_a, m_lru_b_a, m_lru_w_x, m_lru_b_x, m_lru_lambda, m_w_branch_gdn, m_w_branch_lru, m_w_out, m_mlp_norm, m_w_up, m_w_down, m_final_norm, v_attn_norm, v_w_in, v_gdn_conv_w, v_gdn_a_log, v_gdn_dt_bias, v_gdn_norm, v_lru_conv_w, v_lru_conv_b, v_lru_w_a, v_lru_b_a, v_lru_w_x, v_lru_b_x, v_lru_lambda, v_w_branch_gdn, v_w_branch_lru, v_w_out, v_mlp_norm, v_w_up, v_w_down, v_final_norm):
    w = dict(attn_norm=attn_norm, w_in=w_in, gdn_conv_w=gdn_conv_w, gdn_a_log=gdn_a_log, gdn_dt_bias=gdn_dt_bias,
             gdn_norm=gdn_norm, lru_conv_w=lru_conv_w, lru_conv_b=lru_conv_b, lru_w_a=lru_w_a, lru_b_a=lru_b_a,
             lru_w_x=lru_w_x, lru_b_x=lru_b_x, lru_lambda=lru_lambda, w_branch_gdn=w_branch_gdn,
             w_branch_lru=w_branch_lru, w_out=w_out, mlp_norm=mlp_norm, w_up=w_up, w_down=w_down,
             final_norm=final_norm)
    m = dict(attn_norm=m_attn_norm, w_in=m_w_in, gdn_conv_w=m_gdn_conv_w, gdn_a_log=m_gdn_a_log,
             gdn_dt_bias=m_gdn_dt_bias, gdn_norm=m_gdn_norm, lru_conv_w=m_lru_conv_w, lru_conv_b=m_lru_conv_b,
             lru_w_a=m_lru_w_a, lru_b_a=m_lru_b_a, lru_w_x=m_lru_w_x, lru_b_x=m_lru_b_x, lru_lambda=m_lru_lambda,
             w_branch_gdn=m_w_branch_gdn, w_branch_lru=m_w_branch_lru, w_out=m_w_out, mlp_norm=m_mlp_norm,
             w_up=m_w_up, w_down=m_w_down, final_norm=m_final_norm)
    v = dict(attn_norm=v_attn_norm, w_in=v_w_in, gdn_conv_w=v_gdn_conv_w, gdn_a_log=v_gdn_a_log,
             gdn_dt_bias=v_gdn_dt_bias, gdn_norm=v_gdn_norm, lru_conv_w=v_lru_conv_w, lru_conv_b=v_lru_conv_b,
             lru_w_a=v_lru_w_a, lru_b_a=v_lru_b_a, lru_w_x=v_lru_w_x, lru_b_x=v_lru_b_x, lru_lambda=v_lru_lambda,
             w_branch_gdn=v_w_branch_gdn, w_branch_lru=v_w_branch_lru, w_out=v_w_out, mlp_norm=v_mlp_norm,
             w_up=v_w_up, w_down=v_w_down, final_norm=v_final_norm)
    n_layers = attn_norm.shape[0]
    d = x.shape[-1]
    heads = gdn_a_log.shape[-1]
    lw = lru_conv_b.shape[-1]
    dm = _dims(d, heads, lw)
    big_names = list(BIG_SHARD_AXIS)
    conv_names = list(CONV_SHARD_AXIS)
    chip = 2 * lax.axis_index("x") + lax.axis_index("y")

    conv_flat = jnp.concatenate([w[n].reshape(-1) for n in conv_names])
    conv_rows = -(-conv_flat.shape[0] // (SUBLANES * LANES)) * SUBLANES
    conv_buf = jnp.pad(conv_flat, (0, conv_rows * LANES - conv_flat.shape[0])).reshape(conv_rows, LANES)
    conv_all = _all_devices_gather(conv_buf, "conv_allgather").reshape(N_CHIPS, 2, -1)[:, 0]
    conv_full, off = {}, 0
    for n in conv_names:
        shard = w[n]
        parts = conv_all[:, off:off + shard.size].reshape((N_CHIPS,) + shard.shape)
        conv_full[n] = jnp.concatenate([parts[q] for q in range(N_CHIPS)], axis=CONV_SHARD_AXIS[n])
        off += shard.size

    def start_gather(li, group):
        halves, lands = [], []
        for n in GATHER_GROUPS[group]:
            s = w[n][li].astype(BF16)
            hv = s.reshape((2, s.shape[0] // 2) + s.shape[1:])
            halves.append(hv)
            lands.append(lax.dynamic_update_index_in_dim(lax.empty((N_CHIPS,) + hv.shape, BF16), hv, chip, 0))
        nt = len(halves)
        return _split_start(f"wgather_{group}_l{li}_ici_start", halves + lands, 3 * nt,
                            functools.partial(_gather_ici_copies, nt))

    pending = {(li, group): start_gather(li, group) for li in range(n_layers) for group in GATHER_GROUPS}

    swapping = {}

    def start_swap(li, group, after):
        nt = len(GATHER_GROUPS[group])
        send, recv, bufs, _ = pending.pop((li, group))
        bufs = _split_wait(f"wgather_{group}_l{li}_ici_wait", send, recv, bufs, after,
                           functools.partial(_gather_ici_copies, nt))
        swapping[(li, group)] = _split_start(f"wgather_{group}_l{li}_d2d_start", bufs[nt:], 3 * nt,
                                             functools.partial(_gather_d2d_copies, nt))
        return swapping[(li, group)][3]

    def make_fetch(li):
        def fetch(group, after):
            names = GATHER_GROUPS[group]
            nt = len(names)
            if (li, group) not in swapping:
                start_swap(li, group, after)
            send, recv, lands, token = swapping.pop((li, group))
            lands = _split_wait(f"wgather_{group}_l{li}_d2d_wait", send, recv, lands, token,
                                functools.partial(_gather_d2d_copies, nt))
            if (li, group) == (0, 'mix'):
                tie = sum(start_swap(l2, g2, lands[0])[0, 0] for l2, g2 in list(pending))
                layers[0]['mlp_norm'] = layers[0]['mlp_norm'] + tie
            out = {}
            for n, land in zip(names, lands):
                slots = land.reshape((N_CHIPS, 2 * land.shape[2]) + land.shape[3:])
                out[n] = jnp.concatenate([slots[q] for q in range(N_CHIPS)], axis=BIG_SHARD_AXIS[n] - 1)
            if 'w_in' in out:
                out['w_in_p'] = _pad_w_in(out.pop('w_in'), dm)
            return out
        return fetch

    layers = []
    for li in range(n_layers):
        wl = {n: w[n][li] for n in SMALL_NAMES if n != 'final_norm' and n not in CONV_SHARD_AXIS}
        for n in conv_names:
            wl[n] = conv_full[n][li]
        layers.append(wl)
    layers[0]['attn_norm'] = layers[0]['attn_norm'] + sum(handle[3][0, 0] for handle in pending.values())

    core = lax.axis_index("c").astype(jnp.int32).reshape(1)
    chip_op = chip.astype(jnp.int32).reshape(1)
    sending, in_flight, reduced = {}, {}, {}

    def reduce_begin(group, li, g):
        names = GATHER_GROUPS[group]
        nt = len(names)
        contrib = []
        for n in names:
            if n != 'w_in':
                contrib.append(g[n])
                continue
            pieces = jnp.stack(jnp.split(_unpad_w_in(g['w_in_p'], dm), N_CHIPS, axis=BIG_SHARD_AXIS[n] - 1), axis=0)
            rows_half = pieces.shape[1] // 2
            contrib.append(jnp.swapaxes(pieces.reshape((N_CHIPS, 2, rows_half) + pieces.shape[2:]), 0, 1))
        theirs = [lax.empty(cb.shape[1:], BF16) for cb in contrib]
        send, recv, bufs, token = _split_start(f"gsend_{group}_l{li}_start", contrib + theirs, nt,
                                               functools.partial(_sibling_half_copies, nt))
        sending[(group, li)] = (send, recv, bufs)
        return token

    def reduce_scatter(group, li, after):
        names = GATHER_GROUPS[group]
        nt = len(names)
        send, recv, bufs = sending.pop((group, li))
        bufs = _split_wait(f"gsend_{group}_l{li}_wait", send, recv, bufs, after,
                           functools.partial(_sibling_half_copies, nt))
        sums = [_pair_sum(core, mine, th, f"gpair_{n}_l{li}") for n, mine, th in zip(names, bufs[:nt], bufs[nt:])]
        lands = [jnp.zeros(sm.shape, BF16) for sm in sums]
        send, recv, bufs, token = _split_start(f"gscatter_{group}_l{li}_start", sums + lands, 3 * nt,
                                               functools.partial(_scatter_ici_copies, nt))
        in_flight[(group, li)] = (send, recv, bufs)
        return token

    def reduce_end(jobs, after, name):
        keys, totals = [], []
        for group, li in jobs:
            names = GATHER_GROUPS[group]
            nt = len(names)
            send, recv, bufs = in_flight.pop((group, li))
            bufs = _split_wait(f"gscatter_{group}_l{li}_wait", send, recv, bufs, after,
                               functools.partial(_scatter_ici_copies, nt))
            totals += [_sum_landed(chip_op, land, own, f"gtotal_{n}_l{li}")
                       for n, own, land in zip(names, bufs[:nt], bufs[nt:])]
            keys += [(n, li) for n in names]
        others = _sibling_exchange(totals, False, name)
        for key, mine, other in zip(keys, totals, others):
            reduced[key] = (mine, other)

    order = [(group, li) for li in reversed(range(n_layers)) for group in reversed(list(GATHER_GROUPS))]

    def make_hook(li):
        def hook(group, g, wl, gain):
            at = order.index((group, li))
            latest = g['w_in_p'] if group == 'in' else g[GATHER_GROUPS[group][-1]]
            tie = reduce_begin(group, li, g)[0, 0]
            if at > 0:
                tie = tie + reduce_scatter(*order[at - 1], latest)[0, 0]
            wl[gain] = wl[gain] + tie
        return hook

    loss_local, dx, grads, dfin = _local_step(x[0], loss_target[0], layers, [make_fetch(li) for li in range(n_layers)],
                                              [make_hook(li) for li in range(n_layers)], final_norm, dm)
    loss = lax.psum(loss_local, MESH_AXES)
    last_scatter_token = reduce_scatter(*order[-1], dx)

    small_g = {n: jnp.stack([grads[li][n] for li in range(n_layers)], axis=0)
               for n in SMALL_NAMES if n != 'final_norm'}
    small_g['final_norm'] = dfin
    flat = jnp.concatenate([small_g[n].reshape(-1) for n in PACK_NAMES])
    n_flat = flat.shape[0]
    row_unit = 32 * SUBLANES
    rows = -(-n_flat // (row_unit * LANES)) * row_unit
    small_srcs = [jnp.pad(flat, (0, rows * LANES - n_flat)).reshape(rows, LANES)]
    small_srcs += [small_g[n].reshape(-1, LANES) for n in MID_NAMES]
    device = 2 * chip + lax.axis_index("c")
    small_lands = [lax.dynamic_update_index_in_dim(lax.empty((N_DEVICES,) + b.shape, F32), b, device, 0)
                   for b in small_srcs]
    n_small = len(small_srcs)
    small_copies = functools.partial(_all_devices_copies, n_small)
    small_send, small_recv, small_bufs, small_token = _split_start(
        "small_grad_start", small_srcs + small_lands, (N_DEVICES - 1) * n_small, small_copies)

    out_g, out_d, out_m, out_v = {}, {}, {}, {}

    def adamw_big(n, after):
        quarters = (n_layers, 2, w[n].shape[1] // 2, w[n].shape[2])
        res = _adamw_quarters(core, w[n].reshape(quarters), [reduced[(n, li)][0] for li in range(n_layers)],
                              [reduced[(n, li)][1] for li in range(n_layers)], m[n].reshape(quarters),
                              v[n].reshape(quarters), after, f"adamw_{n}")
        out_g[n], out_d[n], out_m[n], out_v[n] = (r.reshape(w[n].shape) for r in res)
        return res[1]

    early = [group for group in GATHER_GROUPS if group != 'in']
    last = small_token + last_scatter_token
    reduce_end([job for job in order if job[0] != 'in'], last, "gswap_early")
    for group in early:
        for n in GATHER_GROUPS[group]:
            last = adamw_big(n, last)
    reduce_end([job for job in order if job[0] == 'in'], last, "gswap_in")
    last = adamw_big('w_in', last)
    small_bufs = _split_wait("small_grad_wait", small_send, small_recv, small_bufs, last, small_copies)
    small_sums = [_sum_slots(land, f"small_grad_sum_{k}") for k, land in enumerate(small_bufs[n_small:])]
    small_red = {}
    off = 0
    for n in PACK_NAMES:
        size = small_g[n].size
        small_red[n] = small_sums[0].reshape(-1)[off:off + size].reshape(small_g[n].shape)
        off += size
    for n, g2d in zip(MID_NAMES, small_sums[1:]):
        res = _adamw(w[n].reshape(g2d.shape), [g2d], m[n].reshape(g2d.shape), v[n].reshape(g2d.shape), f"adamw_{n}")
        out_g[n], out_d[n], out_m[n], out_v[n] = (r.reshape(w[n].shape) for r in res)
    for n, ax in CONV_SHARD_AXIS.items():
        width = w[n].shape[ax]
        small_red[n] = lax.dynamic_slice_in_dim(small_red[n], chip * width, width, axis=ax)

    def pack(tree):
        fl = jnp.concatenate([tree[n].reshape(-1) for n in PACK_NAMES])
        return jnp.pad(fl, (0, rows * LANES - fl.shape[0])).reshape(rows, LANES)

    res = _adamw(pack(w), [pack(small_red)], pack(m), pack(v), "adamw_small")
    for r, dst in zip(res, (out_g, out_d, out_m, out_v)):
        fl = r.reshape(-1)
        off = 0
        for n in PACK_NAMES:
            dst[n] = fl[off:off + w[n].size].reshape(w[n].shape)
            off += w[n].size

    return (loss, dx[None], *[out_g[n] for n in WEIGHT_NAMES], *[out_d[n] for n in WEIGHT_NAMES],
            *[out_m[n] for n in WEIGHT_NAMES], *[out_v[n] for n in WEIGHT_NAMES])
```

```python
import functools

import jax
import jax.numpy as jnp
from jax import lax
from jax.experimental import pallas as pl
from jax.experimental.pallas import tpu as pltpu

F32 = jnp.float32
BF16 = jnp.bfloat16

LANES = 128
SUBLANES = 8
VMEM_BYTES = 64 * 1024 * 1024
GDN_CHUNK = 64
CONV_WIDTH = 4
CONV_ROW_TILE = 2048
RMS_EPS = 1e-6
L2_EPS = 1e-6
LRU_C = 8.0
ADAM_LR = 0.001
ADAM_B1 = 0.9
ADAM_B2 = 0.999
ADAM_EPS = 1e-08
ADAM_WD = 0.01
ADAM_STEP = 10
MESH_AXES = ("x", "y", "c")
N_CHIPS = 4
N_DEVICES = 8

INPUT_NAMES = ['x', 'attn_norm', 'w_in', 'gdn_conv_w', 'gdn_a_log', 'gdn_dt_bias', 'gdn_norm', 'lru_conv_w',
               'lru_conv_b', 'lru_w_a', 'lru_b_a', 'lru_w_x', 'lru_b_x', 'lru_lambda', 'w_branch_gdn',
               'w_branch_lru', 'w_out', 'mlp_norm', 'w_up', 'w_down', 'final_norm']
WEIGHT_NAMES = INPUT_NAMES[1:]
BIG_SHARD_AXIS = {'w_in': 2, 'w_branch_gdn': 2, 'w_branch_lru': 2, 'w_out': 1, 'w_up': 2, 'w_down': 1}
CONV_SHARD_AXIS = {'gdn_conv_w': 2, 'lru_conv_w': 2}
GATHER_GROUPS = {'in': ['w_in'], 'mix': ['w_branch_gdn', 'w_branch_lru', 'w_out'], 'mlp': ['w_up', 'w_down']}
SMALL_NAMES = [n for n in WEIGHT_NAMES if n not in BIG_SHARD_AXIS]
MID_NAMES = ['lru_w_a', 'lru_w_x']
PACK_NAMES = [n for n in SMALL_NAMES if n not in MID_NAMES]


def _tile(n, target, unit=LANES):
    best = None
    t = unit
    while t <= min(n, target):
        if n % t == 0:
            best = t
        t += unit
    return n if best is None else best


def _vmem_limit(block_bytes):
    return int(min(max(3 * block_bytes + (8 << 20), 24 << 20), VMEM_BYTES - (8 << 20)))


def _nbytes(shape, dtype):
    n = 1
    for s in shape:
        n *= s
    return n * jnp.dtype(dtype).itemsize


def _pcall(body, *, name, grid, in_specs, out_specs, out_shape, scratch_shapes=(), semantics=None, block_bytes=0,
           scalar_prefetch=0):
    params = dict(vmem_limit_bytes=_vmem_limit(block_bytes))
    if semantics is not None:
        params['dimension_semantics'] = semantics
    if scalar_prefetch:
        grid_spec = pltpu.PrefetchScalarGridSpec(num_scalar_prefetch=scalar_prefetch, grid=grid, in_specs=in_specs,
                                                 out_specs=out_specs, scratch_shapes=list(scratch_shapes))
        return pl.pallas_call(body, name=name, grid_spec=grid_spec, out_shape=out_shape,
                              compiler_params=pltpu.CompilerParams(**params))
    return pl.pallas_call(body, name=name, grid=grid, in_specs=in_specs, out_specs=out_specs, out_shape=out_shape,
                          scratch_shapes=list(scratch_shapes), compiler_params=pltpu.CompilerParams(**params))


def _dot(a, b):
    return jnp.dot(a.astype(BF16), b.astype(BF16), preferred_element_type=F32)


def _dot_nt(a, b):
    return lax.dot_general(a.astype(BF16), b.astype(BF16), (((1,), (1,)), ((), ())), preferred_element_type=F32)


def _dot_tn(a, b):
    return lax.dot_general(a.astype(BF16), b.astype(BF16), (((0,), (0,)), ((), ())), preferred_element_type=F32)


def _sigmoid(x):
    return 1.0 / (1.0 + jnp.exp(-x))


def _log1p(u):
    return jnp.where(u < 1e-3, u * (1.0 - u * (0.5 - u * (1.0 / 3.0))), jnp.log(1.0 + u))


def _softplus(x):
    return jnp.maximum(x, 0.0) + _log1p(jnp.exp(-jnp.abs(x)))


_GELU_K = 0.7978845608028654


def _gelu_and_grad(x):
    inner = _GELU_K * (x + 0.044715 * x * x * x)
    th = jnp.tanh(inner)
    g = 0.5 * x * (1.0 + th)
    dg = 0.5 * (1.0 + th) + 0.5 * x * (1.0 - th * th) * _GELU_K * (1.0 + 3.0 * 0.044715 * x * x)
    return g, dg


MATMUL_TK_MAX = 3584
PROJ_COL_TILE = 1536


def _matmul(a, b, *, mode, name, out_dtype=F32, add=None, epilogue=None, extra=None, tm=512, tn=1024, tk=2048,
            shard_axis=None):
    if mode == 'nn':
        (m, k), (k2, n) = a.shape, b.shape
    elif mode == 'nt':
        (m, k), (n, k2) = a.shape, b.shape
    else:
        (k, m), (k2, n) = a.shape, b.shape
    assert k == k2, (a.shape, b.shape, mode)
    if shard_axis is not None:
        rows_half = (m // N_CHIPS if shard_axis == 0 else m) // 2
        cols = n // N_CHIPS if shard_axis == 1 else n
        tm, tn = _tile(rows_half, tm), _tile(cols, tn)
    else:
        tm, tn = _tile(m, tm), _tile(n, tn)
    tk = _tile(k, tk)
    if k // tk > 2 * (-(-k // MATMUL_TK_MAX)):
        tk = _tile(k, MATMUL_TK_MAX)
    nk = k // tk
    dims = {'nn': (((1,), (0,)), ((), ())), 'nt': (((1,), (1,)), ((), ())), 'tn': (((0,), (0,)), ((), ()))}[mode]
    a_bytes, b_bytes = _nbytes(a.shape, a.dtype), _nbytes(b.shape, b.dtype)
    rows_outer = nk > 1 or a_bytes + (m // tm) * b_bytes <= b_bytes + (n // tn) * a_bytes

    def ij(g0, g1):
        return (g0, g1) if rows_outer else (g1, g0)

    def spec(shape, pick):
        return pl.BlockSpec(shape, lambda g0, g1, kk: pick(*ij(g0, g1), kk))

    a_spec = spec((tk, tm), lambda i, j, kk: (kk, i)) if mode == 'tn' else spec((tm, tk), lambda i, j, kk: (i, kk))
    b_spec = spec((tn, tk), lambda i, j, kk: (j, kk)) if mode == 'nt' else spec((tk, tn), lambda i, j, kk: (kk, j))
    o_spec = spec((tm, tn), lambda i, j, kk: (i, j))
    operands, in_specs = [a, b], [a_spec, b_spec]
    if add is not None:
        operands.append(add)
        in_specs.append(o_spec)
    if extra is not None:
        operands.append(extra)
        in_specs.append(o_spec)
    n_in = len(operands)
    if epilogue == 'relu2':
        out_shape = (jax.ShapeDtypeStruct((m, n), BF16), jax.ShapeDtypeStruct((m, n), BF16))
        out_specs = (o_spec, o_spec)
    elif shard_axis is not None:
        assert add is None and extra is None
        rb, cb = rows_half // tm, cols // tn

        def shard_block(i, j, kk):
            if shard_axis == 0:
                return (i % (2 * rb)) // rb, i // (2 * rb), i % rb, j
            return i // rb, j // cb, i % rb, j % cb

        out_shape = jax.ShapeDtypeStruct((2, N_CHIPS, rows_half, cols), out_dtype)
        out_specs = spec((None, None, tm, tn), shard_block)
    else:
        out_shape = jax.ShapeDtypeStruct((m, n), out_dtype)
        out_specs = o_spec

    def body(*refs):
        a_ref, b_ref = refs[0], refs[1]
        outs = refs[n_in:n_in + n_out]

        def finish(p):
            if add is not None:
                p = p + refs[2][...]
            if epilogue == 'relu2':
                ur = jnp.maximum(p, 0.0)
                outs[0][...] = ur.astype(BF16)
                outs[1][...] = (ur * ur).astype(BF16)
            elif epilogue == 'mul2x':
                outs[0][...] = (p * 2.0 * refs[n_in - 1][...].astype(F32)).astype(out_dtype)
            else:
                outs[0][...] = p.astype(out_dtype)

        prod = lax.dot_general(a_ref[...].astype(BF16), b_ref[...].astype(BF16), dims, preferred_element_type=F32)
        if nk == 1:
            finish(prod)
            return
        acc_ref = refs[-1]
        kk = pl.program_id(2)

        @pl.when(kk == 0)
        def _():
            acc_ref[...] = prod

        @pl.when((kk > 0) & (kk < nk - 1))
        def _():
            acc_ref[...] += prod

        @pl.when(kk == nk - 1)
        def _():
            finish(acc_ref[...] + prod)

    n_out = 2 if epilogue == 'relu2' else 1
    bb = (_nbytes((tm, tk), a.dtype) + _nbytes((tk, tn), b.dtype) + 3 * _nbytes((tm, tn), F32))
    grid = (m // tm, n // tn, nk) if rows_outer else (n // tn, m // tm, nk)
    return _pcall(body, name=name, grid=grid, in_specs=in_specs, out_specs=out_specs, out_shape=out_shape,
                  scratch_shapes=[pltpu.VMEM((tm, tn), F32)] if nk > 1 else [],
                  semantics=("parallel", "parallel", "arbitrary"), block_bytes=bb)(*operands)


def _row_tile(s, d, target_bytes=1 << 20):
    return _tile(s, max(SUBLANES, target_bytes // (4 * d)), SUBLANES)


def _rms_fwd(x, gain, name):
    s, d = x.shape
    tr = _row_tile(s, d)

    def body(x_ref, g_ref, h_ref):
        xv = x_ref[...]
        r = lax.rsqrt(jnp.mean(xv * xv, axis=-1, keepdims=True) + RMS_EPS)
        h_ref[...] = (xv * r * g_ref[...]).astype(BF16)

    row = pl.BlockSpec((tr, d), lambda i: (i, 0))
    return _pcall(body, name=name, grid=(s // tr,), in_specs=[row, pl.BlockSpec((1, d), lambda i: (0, 0))],
                  out_specs=row, out_shape=jax.ShapeDtypeStruct((s, d), BF16), semantics=("parallel",),
                  block_bytes=2 * tr * d * 4)(x, gain.reshape(1, d))


def _rms_bwd(x, gain, dh, dres, name):
    s, d = x.shape
    tr = _row_tile(s, d, 1 << 19)

    def body(x_ref, g_ref, dh_ref, dres_ref, dx_ref, dxb_ref, dg_ref):
        xv = x_ref[...]
        r = lax.rsqrt(jnp.mean(xv * xv, axis=-1, keepdims=True) + RMS_EPS)
        xh = xv * r
        dhv = dh_ref[...]
        dxh = dhv * g_ref[...]
        dx = dres_ref[...] + r * (dxh - xh * jnp.mean(dxh * xh, axis=-1, keepdims=True))
        dx_ref[...] = dx
        dxb_ref[...] = dx.astype(BF16)

        @pl.when(pl.program_id(0) == 0)
        def _():
            dg_ref[...] = jnp.zeros_like(dg_ref)

        dg_ref[...] += jnp.sum(dhv * xh, axis=0, keepdims=True)

    row = pl.BlockSpec((tr, d), lambda i: (i, 0))
    vec = pl.BlockSpec((1, d), lambda i: (0, 0))
    return _pcall(body, name=name, grid=(s // tr,), in_specs=[row, vec, row, row], out_specs=(row, row, vec),
                  out_shape=(jax.ShapeDtypeStruct((s, d), F32), jax.ShapeDtypeStruct((s, d), BF16),
                             jax.ShapeDtypeStruct((1, d), F32)),
                  semantics=("arbitrary",), block_bytes=5 * tr * d * 4)(x, gain.reshape(1, d), dh, dres)


def _loss_head(x, gain, target, name):
    s, d = x.shape
    tr = _row_tile(s, d, 1 << 19)

    def body(x_ref, g_ref, t_ref, loss_ref, dx_ref, dxb_ref, dg_ref):
        xv = x_ref[...]
        r = lax.rsqrt(jnp.mean(xv * xv, axis=-1, keepdims=True) + RMS_EPS)
        xh = xv * r
        gv = g_ref[...]
        err = xh * gv - t_ref[...]
        dy = err * (1.0 / d)
        dxh = dy * gv
        dx = r * (dxh - xh * jnp.mean(dxh * xh, axis=-1, keepdims=True))
        dx_ref[...] = dx
        dxb_ref[...] = dx.astype(BF16)

        @pl.when(pl.program_id(0) == 0)
        def _():
            dg_ref[...] = jnp.zeros_like(dg_ref)
            loss_ref[...] = jnp.zeros_like(loss_ref)

        dg_ref[...] += jnp.sum(dy * xh, axis=0, keepdims=True)
        part = jnp.sum(jnp.sum(err * err, axis=-1, keepdims=True), axis=0, keepdims=True) * (0.5 / d)
        loss_ref[...] += jnp.broadcast_to(part, loss_ref.shape)

    row = pl.BlockSpec((tr, d), lambda i: (i, 0))
    vec = pl.BlockSpec((1, d), lambda i: (0, 0))
    lspec = pl.BlockSpec((SUBLANES, LANES), lambda i: (0, 0))
    return _pcall(body, name=name, grid=(s // tr,), in_specs=[row, vec, row], out_specs=(lspec, row, row, vec),
                  out_shape=(jax.ShapeDtypeStruct((SUBLANES, LANES), F32), jax.ShapeDtypeStruct((s, d), F32),
                             jax.ShapeDtypeStruct((s, d), BF16), jax.ShapeDtypeStruct((1, d), F32)),
                  semantics=("arbitrary",), block_bytes=4 * tr * d * 4)(x, gain.reshape(1, d), target)


def _shift_down(xc, xp, s):
    tr = xc.shape[0]
    r = pltpu.roll(xc, s, 0)
    p = pltpu.roll(xp, s, 0)
    row8 = lax.broadcasted_iota(jnp.int32, (SUBLANES, xc.shape[1]), 0)
    head = jnp.where(row8 < s, p, r[:SUBLANES])
    if tr == SUBLANES:
        return head
    return jnp.concatenate([head, r[SUBLANES:]], axis=0)


def _shift_up(yc, yn, s):
    tr = yc.shape[0]
    u = pltpu.roll(yc, tr - s, 0)
    n = pltpu.roll(yn, SUBLANES - s, 0)
    row8 = lax.broadcasted_iota(jnp.int32, (SUBLANES, yc.shape[1]), 0)
    tail = jnp.where(row8 >= SUBLANES - s, n, u[tr - SUBLANES:])
    if tr == SUBLANES:
        return tail
    return jnp.concatenate([u[:tr - SUBLANES], tail], axis=0)


def _conv_apply(xc, xp, w):
    y = xc * w[CONV_WIDTH - 1:CONV_WIDTH, :]
    for s in range(1, CONV_WIDTH):
        y = y + _shift_down(xc, xp, s) * w[CONV_WIDTH - 1 - s:CONV_WIDTH - s, :]
    return y


def _halo_specs(tr, col_of):
    per = tr // SUBLANES
    cur = pl.BlockSpec((tr, LANES), lambda j, i: (i, col_of(j)))
    prev = pl.BlockSpec((SUBLANES, LANES), lambda j, i: (jnp.maximum(i * per - 1, 0), col_of(j)))
    return cur, prev


def _conv_bias_fwd(x_arr, x_col0, w, bias, name):
    s = x_arr.shape[0]
    ncb = w.shape[1] // LANES
    tr = _tile(s, CONV_ROW_TILE, SUBLANES)

    def body(cur_ref, prev_ref, w_ref, b_ref, o_ref):
        i = pl.program_id(1)
        xp = prev_ref[...] * (i > 0).astype(F32)
        o_ref[...] = _conv_apply(cur_ref[...], xp, w_ref[...]) + b_ref[...]

    cur, prev = _halo_specs(tr, lambda j: x_col0 + j)
    return _pcall(body, name=name, grid=(ncb, s // tr),
                  in_specs=[cur, prev, pl.BlockSpec((CONV_WIDTH, LANES), lambda j, i: (0, j)),
                            pl.BlockSpec((1, LANES), lambda j, i: (0, j))],
                  out_specs=pl.BlockSpec((tr, LANES), lambda j, i: (i, j)),
                  out_shape=jax.ShapeDtypeStruct((s, w.shape[1]), F32), semantics=("parallel", "parallel"),
                  block_bytes=3 * tr * LANES * 4)(x_arr, x_arr, w, bias.reshape(1, -1))


def _conv_bwd(dy, x_arr, x_col0, w, name):
    s, c = dy.shape
    ncb = c // LANES
    tr = _tile(s, CONV_ROW_TILE, SUBLANES)
    per = tr // SUBLANES
    ni = s // tr

    def body(dy_ref, dyn_ref, cur_ref, prev_ref, w_ref, dx_ref, dw_ref, db_ref):
        i = pl.program_id(1)
        dyv = dy_ref[...]
        dn = dyn_ref[...] * (i < ni - 1).astype(F32)
        xc = cur_ref[...]
        xp = prev_ref[...] * (i > 0).astype(F32)
        wv = w_ref[...]

        @pl.when(i == 0)
        def _():
            dw_ref[...] = jnp.zeros_like(dw_ref)
            db_ref[...] = jnp.zeros_like(db_ref)

        dx = dyv * wv[CONV_WIDTH - 1:CONV_WIDTH, :]
        dw_ref[CONV_WIDTH - 1:CONV_WIDTH, :] += jnp.sum(dyv * xc, axis=0, keepdims=True)
        for sh in range(1, CONV_WIDTH):
            j = CONV_WIDTH - 1 - sh
            dx = dx + _shift_up(dyv, dn, sh) * wv[j:j + 1, :]
            dw_ref[j:j + 1, :] += jnp.sum(dyv * _shift_down(xc, xp, sh), axis=0, keepdims=True)
        dx_ref[...] = dx
        db_ref[...] += jnp.sum(dyv, axis=0, keepdims=True)

    cur, prev = _halo_specs(tr, lambda j: x_col0 + j)
    dcur = pl.BlockSpec((tr, LANES), lambda j, i: (i, j))
    dnext = pl.BlockSpec((SUBLANES, LANES), lambda j, i: (jnp.minimum((i + 1) * per, s // SUBLANES - 1), j))
    return _pcall(body, name=name, grid=(ncb, ni),
                  in_specs=[dcur, dnext, cur, prev, pl.BlockSpec((CONV_WIDTH, LANES), lambda j, i: (0, j))],
                  out_specs=(dcur, pl.BlockSpec((CONV_WIDTH, LANES), lambda j, i: (0, j)),
                             pl.BlockSpec((1, LANES), lambda j, i: (0, j))),
                  out_shape=(jax.ShapeDtypeStruct((s, c), F32), jax.ShapeDtypeStruct((CONV_WIDTH, c), F32),
                             jax.ShapeDtypeStruct((1, c), F32)),
                  semantics=("parallel", "arbitrary"), block_bytes=4 * tr * LANES * 4)(dy, dy, x_arr, x_arr, w)


def _gdn_pre_fwd(proj, conv_w, heads, name):
    s = proj.shape[0]
    ncb = conv_w.shape[1] // LANES
    tr = _tile(s, CONV_ROW_TILE, SUBLANES)
    qscale = float(LANES) ** -0.5

    def body(cur_ref, prev_ref, w_ref, o_ref):
        j, i = pl.program_id(0), pl.program_id(1)
        xp = prev_ref[...] * (i > 0).astype(F32)
        cv = _conv_apply(cur_ref[...], xp, w_ref[...])
        sv = cv * _sigmoid(cv)
        nrm = lax.rsqrt(jnp.sum(sv * sv, axis=-1, keepdims=True) + L2_EPS)
        scale = jnp.where(j < heads, qscale, 1.0)
        o_ref[...] = jnp.where(j < 2 * heads, sv * nrm * scale, sv)

    cur, prev = _halo_specs(tr, lambda j: j)
    return _pcall(body, name=name, grid=(ncb, s // tr),
                  in_specs=[cur, prev, pl.BlockSpec((CONV_WIDTH, LANES), lambda j, i: (0, j))],
                  out_specs=pl.BlockSpec((tr, LANES), lambda j, i: (i, j)),
                  out_shape=jax.ShapeDtypeStruct((s, conv_w.shape[1]), F32), semantics=("parallel", "parallel"),
                  block_bytes=3 * tr * LANES * 4)(proj, proj, conv_w)


def _gdn_pre_bwd(proj, conv_w, dq, dk, dv, heads, name):
    s = proj.shape[0]
    ncb = conv_w.shape[1] // LANES
    tr = _tile(s, CONV_ROW_TILE, SUBLANES)
    qscale = float(LANES) ** -0.5

    def body(cur_ref, prev_ref, w_ref, dq_ref, dk_ref, dv_ref, o_ref):
        j, i = pl.program_id(0), pl.program_id(1)
        xp = prev_ref[...] * (i > 0).astype(F32)
        cv = _conv_apply(cur_ref[...], xp, w_ref[...])
        sg = _sigmoid(cv)
        sv = cv * sg
        nrm = lax.rsqrt(jnp.sum(sv * sv, axis=-1, keepdims=True) + L2_EPS)
        dv = jnp.where(j < heads, dq_ref[...], jnp.where(j < 2 * heads, dk_ref[...], dv_ref[...]))
        scale = jnp.where(j < heads, qscale, 1.0)
        dsn = scale * nrm * (dv - sv * (nrm * nrm) * jnp.sum(dv * sv, axis=-1, keepdims=True))
        ds = jnp.where(j < 2 * heads, dsn, dv)
        o_ref[...] = ds * (sg * (1.0 + cv * (1.0 - sg)))

    cur, prev = _halo_specs(tr, lambda j: j)
    blk = pl.BlockSpec((tr, LANES), lambda j, i: (i, j))

    def part(k):
        return pl.BlockSpec((tr, LANES), lambda j, i: (i, jnp.clip(j - k * heads, 0, heads - 1)))

    return _pcall(body, name=name, grid=(ncb, s // tr),
                  in_specs=[cur, prev, pl.BlockSpec((CONV_WIDTH, LANES), lambda j, i: (0, j)), part(0), part(1),
                            part(2)],
                  out_specs=blk, out_shape=jax.ShapeDtypeStruct((s, conv_w.shape[1]), F32),
                  semantics=("parallel", "parallel"), block_bytes=6 * tr * LANES * 4)(proj, proj, conv_w, dq, dk, dv)


def _dot_split(a, b, dims=(((1,), (0,)), ((), ()))):
    a_hi, b_hi = a.astype(BF16), b.astype(BF16)
    a_lo, b_lo = (a - a_hi.astype(F32)).astype(BF16), (b - b_hi.astype(F32)).astype(BF16)

    def dot(u, v):
        return lax.dot_general(u, v, dims, preferred_element_type=F32)

    return dot(a_hi, b_hi) + dot(a_hi, b_lo) + dot(a_lo, b_hi)


def _tri_inverse(a_strict, block):
    n = a_strict.shape[0]
    ri = lax.broadcasted_iota(jnp.int32, (n, n), 0)
    ci = lax.broadcasted_iota(jnp.int32, (n, n), 1)
    same8 = (ri >> 3) == (ci >> 3)
    sel = jnp.where((lax.broadcasted_iota(jnp.int32, (n, LANES), 0) & 7)
                    == lax.broadcasted_iota(jnp.int32, (n, LANES), 1), 1.0, 0.0)
    a8 = _dot_split(jnp.where(same8, a_strict, 0.0), sel)
    t8 = sel
    r_in = lax.broadcasted_iota(jnp.int32, (n, 1), 0) & 7
    for j in range(SUBLANES - 1):
        row_j = jnp.broadcast_to(t8.reshape(n // SUBLANES, SUBLANES, LANES)[:, j:j + 1, :],
                                 (n // SUBLANES, SUBLANES, LANES)).reshape(n, LANES)
        t8 = t8 - jnp.where(r_in > j, a8[:, j:j + 1] * row_j, 0.0)
    t = jnp.where(same8, _dot_split(t8, sel, (((1,), (1,)), ((), ()))), 0.0)
    size = SUBLANES
    while size < block:
        sh = size.bit_length() - 1
        lower_left = (((ri >> (sh + 1)) == (ci >> (sh + 1))) & (((ri >> sh) & 1) == 1) & (((ci >> sh) & 1) == 0))
        t = t - _dot_split(t, _dot_split(jnp.where(lower_left, a_strict, 0.0), t))
        size *= 2
    return t


GDN_HEAD_GROUP = 4
_CHUNK_SHIFT = GDN_CHUNK.bit_length() - 1
_LANE_SHIFT = LANES.bit_length() - 1


def _stack_heads(ref, hb):
    return jnp.concatenate([ref[:, i * LANES:(i + 1) * LANES] for i in range(hb)], axis=0)


def _diag_blocks(x, hb):
    c = GDN_CHUNK
    return jnp.concatenate([x[i * c:(i + 1) * c, i * LANES:(i + 1) * LANES] for i in range(hb)], axis=0)


def _expand_blocks(y, hb):
    row_blk = lax.shift_right_logical(lax.broadcasted_iota(jnp.int32, y.shape, 0), _CHUNK_SHIFT)
    return jnp.concatenate([jnp.where(row_blk == j, y, 0.0) for j in range(hb)], axis=1)


def _gdn_group_terms(q, k, v, ab, alog, dtb, head0, hb, heads):
    c = GDN_CHUNK
    r = hb * c
    lane = lax.broadcasted_iota(jnp.int32, (1, LANES), 1)

    def column(lane0):
        return jnp.concatenate([jnp.sum(jnp.where(lane == lane0 + head0 + i, ab, 0.0), axis=1, keepdims=True)
                                for i in range(hb)], axis=0)

    def per_head(vec):
        return jnp.concatenate([jnp.broadcast_to(jnp.sum(jnp.where(lane == head0 + i, vec, 0.0), axis=1,
                                                         keepdims=True), (c, 1)) for i in range(hb)], axis=0)

    pre = column(0) + per_head(dtb)
    neg_ea = -jnp.exp(per_head(alog))
    g = neg_ea * _softplus(pre)
    beta = _sigmoid(column(heads))
    ri = lax.broadcasted_iota(jnp.int32, (r, r), 0)
    ci = lax.broadcasted_iota(jnp.int32, (r, r), 1)
    same = lax.shift_right_logical(ri, _CHUNK_SHIFT) == lax.shift_right_logical(ci, _CHUNK_SHIFT)
    eye = ri == ci
    causal = same & (ri >= ci)
    strict = same & (ri > ci)
    g_row = jnp.sum(jnp.where(eye, g, 0.0), axis=0, keepdims=True)
    gc_col = jnp.sum(jnp.where(causal, g_row, 0.0), axis=1, keepdims=True)
    gc_row = jnp.sum(jnp.where(same & (ri <= ci), g, 0.0), axis=0, keepdims=True)
    gl_col = jnp.sum(jnp.where(same, g_row, 0.0), axis=1, keepdims=True)
    decay = jnp.where(causal, jnp.exp(jnp.where(causal, gc_col - gc_row, 0.0)), 0.0)
    e_last_col = jnp.exp(gl_col)
    e_last_lanes = jnp.concatenate([jnp.broadcast_to(e_last_col[i * c:i * c + 1, :], (1, LANES))
                                    for i in range(hb)], axis=1)
    egc = jnp.exp(gc_col)
    ekl = jnp.exp(gl_col - gc_col)
    kb = k * beta
    vb = v * beta
    kk = _dot_nt(kb, k)
    a_strict = jnp.where(strict, kk * decay, 0.0)
    return dict(pre=pre, neg_ea=neg_ea, g=g, beta=beta, ri=ri, ci=ci, same=same, eye=eye, causal=causal,
                strict=strict, decay=decay, e_last_col=e_last_col, e_last_lanes=e_last_lanes, egc=egc, ekl=ekl,
                kb=kb, vb=vb, kk=kk, a_strict=a_strict, lane=lane)


def _gdn_head_group(heads):
    hb = GDN_HEAD_GROUP
    while heads % hb:
        hb //= 2
    return hb


def _gdn_fwd(qkv, proj, ab_blk, alog, dtb, heads, name):
    s = qkv.shape[0]
    c = GDN_CHUNK
    nc = s // c
    hb = _gdn_head_group(heads)
    ng = heads // hb
    r = hb * c

    def body(q_ref, k_ref, v_ref, ab_ref, alog_ref, dtb_ref, o_ref, t_ref, s0_ref, state_ref):
        grp, ch = pl.program_id(0), pl.program_id(1)

        @pl.when(ch == 0)
        def _():
            state_ref[...] = jnp.zeros_like(state_ref)

        q, k, v = _stack_heads(q_ref, hb), _stack_heads(k_ref, hb), _stack_heads(v_ref, hb)
        tm = _gdn_group_terms(q, k, v, ab_ref[...], alog_ref[...], dtb_ref[...], grp * hb, hb, heads)
        t_inv = _tri_inverse(tm['a_strict'], c)
        u = _dot(t_inv, tm['vb'])
        w = _dot(t_inv, tm['kb'] * tm['egc'])
        qk = jnp.where(tm['causal'], _dot_nt(q, k) * tm['decay'], 0.0)
        st = state_ref[...]
        v_new = u - _diag_blocks(_dot(w, st), hb)
        out = _diag_blocks(_dot(q * tm['egc'], st), hb) + _dot(qk, v_new)
        for i in range(hb):
            o_ref[:, i * LANES:(i + 1) * LANES] = out[i * c:(i + 1) * c, :]
        t_ref[...] = t_inv
        s0_ref[...] = st
        state_ref[...] = st * tm['e_last_lanes'] + _dot_tn(k * tm['ekl'], _expand_blocks(v_new, hb))

    def blk(off):
        return pl.BlockSpec((c, hb * LANES), lambda g, n: (n, off * ng + g))

    vec = pl.BlockSpec((1, LANES), lambda g, n: (0, 0))
    return _pcall(
        body, name=name, grid=(ng, nc),
        in_specs=[blk(0), blk(1), blk(2), pl.BlockSpec((c, LANES), lambda g, n: (n, ab_blk)), vec, vec],
        out_specs=(blk(0), pl.BlockSpec((None, None, r, r), lambda g, n: (g, n, 0, 0)),
                   pl.BlockSpec((None, None, LANES, hb * LANES), lambda g, n: (g, n, 0, 0))),
        out_shape=(jax.ShapeDtypeStruct((s, heads * LANES), F32), jax.ShapeDtypeStruct((ng, nc, r, r), F32),
                   jax.ShapeDtypeStruct((ng, nc, LANES, hb * LANES), F32)),
        scratch_shapes=[pltpu.VMEM((LANES, hb * LANES), F32)], semantics=("parallel", "arbitrary"),
        block_bytes=8 * r * LANES * 4 + 2 * r * r * 4 + 2 * LANES * hb * LANES * 4)(qkv, qkv, qkv, proj, alog, dtb)


def _gdn_bwd(qkv, proj, ab_blk, alog, dtb, t_all, s0_all, d_o, heads, name):
    s = qkv.shape[0]
    c = GDN_CHUNK
    nc = s // c
    hb = _gdn_head_group(heads)
    ng = heads // hb
    r = hb * c

    def body(q_ref, k_ref, v_ref, ab_ref, alog_ref, dtb_ref, t_ref, s0_ref, do_ref,
             dq_ref, dk_ref, dv_ref, dgb_ref, ds_ref):
        grp, step = pl.program_id(0), pl.program_id(1)

        @pl.when(step == 0)
        def _():
            ds_ref[...] = jnp.zeros_like(ds_ref)

        q, k, v = _stack_heads(q_ref, hb), _stack_heads(k_ref, hb), _stack_heads(v_ref, hb)
        do = _stack_heads(do_ref, hb)
        tm = _gdn_group_terms(q, k, v, ab_ref[...], alog_ref[...], dtb_ref[...], grp * hb, hb, heads)
        ri, ci, same, eye = tm['ri'], tm['ci'], tm['same'], tm['eye']
        causal, strict, decay = tm['causal'], tm['strict'], tm['decay']
        egc, ekl, kb, vb, beta = tm['egc'], tm['ekl'], tm['kb'], tm['vb'], tm['beta']
        t_inv = t_ref[...]
        st = s0_ref[...]
        ds_next = ds_ref[...]
        kbg = kb * egc
        u = _dot(t_inv, vb)
        w = _dot(t_inv, kbg)
        qkm = _dot_nt(q, k)
        qk = jnp.where(causal, qkm * decay, 0.0)
        v_new = u - _diag_blocks(_dot(w, st), hb)
        qd = q * egc
        kd = k * ekl
        do_x = _expand_blocks(do, hb)

        dqd = _dot_nt(do_x, st)
        dqk = jnp.where(causal, _dot_nt(do, v_new), 0.0)
        dvn = _dot_tn(qk, do) + _diag_blocks(_dot(kd, ds_next), hb)
        dkd = _dot_nt(_expand_blocks(v_new, hb), ds_next)
        sd = jnp.sum(st * ds_next, axis=0, keepdims=True)
        dgl = jnp.concatenate([jnp.broadcast_to(jnp.sum(sd[:, i * LANES:(i + 1) * LANES], axis=1, keepdims=True),
                                                (c, 1)) for i in range(hb)], axis=0) * tm['e_last_col']
        dvn_x = _expand_blocks(dvn, hb)
        dw = -_dot_nt(dvn_x, st)
        ds_ref[...] = _dot_tn(qd, do_x) + tm['e_last_lanes'] * ds_next - _dot_tn(w, dvn_x)
        dt = _dot_nt(dvn, vb) + _dot_nt(dw, kbg)
        dvb = _dot_tn(t_inv, dvn)
        dkbg = _dot_tn(t_inv, dw)
        da_m = jnp.where(strict, -_dot_tn(t_inv, _dot_nt(dt, t_inv)), 0.0)
        dad = da_m * decay
        dkb = _dot(dad, k) + dkbg * egc
        dqkd = dqk * decay
        dq = _dot(dqkd, k) + dqd * egc
        dk = _dot_tn(dad, kb) + _dot_tn(dqkd, q) + dkd * ekl + dkb * beta
        e_mat = (da_m * tm['kk'] + dqk * qkm) * decay
        s_kd = jnp.sum(dkd * kd, axis=1, keepdims=True)
        s_kd_row = jnp.sum(jnp.where(eye, s_kd, 0.0), axis=0, keepdims=True)
        dgl = dgl + jnp.sum(jnp.where(same, s_kd_row, 0.0), axis=1, keepdims=True)
        col_sum = jnp.sum(e_mat, axis=0, keepdims=True)
        col_sum_c = jnp.sum(jnp.where(eye, col_sum, 0.0), axis=1, keepdims=True)
        dgc = (jnp.sum(e_mat, axis=1, keepdims=True) - col_sum_c + jnp.sum(dqd * qd, axis=1, keepdims=True)
               - s_kd + jnp.sum(dkbg * kbg, axis=1, keepdims=True))
        row_c = lax.broadcasted_iota(jnp.int32, (r, 1), 0)
        dgc = dgc + jnp.where((row_c & (c - 1)) == c - 1, dgl, 0.0)
        dgc_row = jnp.sum(jnp.where(eye, dgc, 0.0), axis=0, keepdims=True)
        dg = jnp.sum(jnp.where(same & (ci >= ri), dgc_row, 0.0), axis=1, keepdims=True)
        dbeta = jnp.sum(dkb * k, axis=1, keepdims=True) + jnp.sum(dvb * v, axis=1, keepdims=True)
        da_pre = dg * tm['neg_ea'] * _sigmoid(tm['pre'])
        db_pre = dbeta * beta * (1.0 - beta)
        lane = tm['lane']
        head_row = grp * hb + lax.shift_right_logical(row_c, _CHUNK_SHIFT)
        dgb = (jnp.where(lane == head_row, da_pre, 0.0) + jnp.where(lane == heads + head_row, db_pre, 0.0)
               + jnp.where(lane == 2 * heads + head_row, dg * tm['g'], 0.0))
        dvv = dvb * beta
        for i in range(hb):
            cols, rows = slice(i * LANES, (i + 1) * LANES), slice(i * c, (i + 1) * c)
            dq_ref[:, cols] = dq[rows, :]
            dk_ref[:, cols] = dk[rows, :]
            dv_ref[:, cols] = dvv[rows, :]
            dgb_ref[:, cols] = dgb[rows, :]

    def blk(off):
        return pl.BlockSpec((c, hb * LANES), lambda g, n: (nc - 1 - n, off * ng + g))

    vec = pl.BlockSpec((1, LANES), lambda g, n: (0, 0))
    gw = heads * LANES
    dq, dk, dv, dgb = _pcall(
        body, name=name, grid=(ng, nc),
        in_specs=[blk(0), blk(1), blk(2), pl.BlockSpec((c, LANES), lambda g, n: (nc - 1 - n, ab_blk)),
                  vec, vec, pl.BlockSpec((None, None, r, r), lambda g, n: (g, nc - 1 - n, 0, 0)),
                  pl.BlockSpec((None, None, LANES, hb * LANES), lambda g, n: (g, nc - 1 - n, 0, 0)), blk(0)],
        out_specs=(blk(0), blk(0), blk(0), blk(0)),
        out_shape=tuple(jax.ShapeDtypeStruct((s, gw), F32) for _ in range(4)),
        scratch_shapes=[pltpu.VMEM((LANES, hb * LANES), F32)], semantics=("parallel", "arbitrary"),
        block_bytes=12 * r * LANES * 4 + 2 * r * r * 4 + 2 * LANES * hb * LANES * 4)(
            qkv, qkv, qkv, proj, alog, dtb, t_all, s0_all, d_o)
    return dq, dk, dv, dgb


def _gdn_post_fwd(o, proj, z_col0, gain, name):
    s, gw = o.shape
    heads = gw // LANES
    tr = _tile(s, CONV_ROW_TILE, SUBLANES)

    def body(o_ref, z_ref, g_ref, y_ref):
        ov, zv = o_ref[...], z_ref[...]
        r = lax.rsqrt(jnp.mean(ov * ov, axis=-1, keepdims=True) + RMS_EPS)
        y_ref[...] = (ov * r * g_ref[...] * (zv * _sigmoid(zv))).astype(BF16)

    blk = pl.BlockSpec((tr, LANES), lambda i, h: (i, h))
    return _pcall(body, name=name, grid=(s // tr, heads),
                  in_specs=[blk, pl.BlockSpec((tr, LANES), lambda i, h: (i, z_col0 + h)),
                            pl.BlockSpec((1, LANES), lambda i, h: (0, 0))],
                  out_specs=blk, out_shape=jax.ShapeDtypeStruct((s, gw), BF16), semantics=("parallel", "parallel"),
                  block_bytes=3 * tr * LANES * 4)(o, proj, gain.reshape(1, LANES))


def _gdn_post_bwd(o, proj, z_col0, gain, dy, name):
    s, gw = o.shape
    heads = gw // LANES
    tr = _tile(s, CONV_ROW_TILE, SUBLANES)

    def body(o_ref, z_ref, g_ref, dy_ref, do_ref, dz_ref, dg_ref):
        ov, zv, gv, dyv = o_ref[...], z_ref[...], g_ref[...], dy_ref[...]
        r = lax.rsqrt(jnp.mean(ov * ov, axis=-1, keepdims=True) + RMS_EPS)
        nv = ov * r
        sg = _sigmoid(zv)
        sz = zv * sg
        dn = dyv * gv * sz
        do_ref[...] = r * (dn - nv * jnp.mean(dn * nv, axis=-1, keepdims=True))
        dz_ref[...] = dyv * nv * gv * (sg * (1.0 + zv * (1.0 - sg)))

        @pl.when((pl.program_id(0) == 0) & (pl.program_id(1) == 0))
        def _():
            dg_ref[...] = jnp.zeros_like(dg_ref)

        dg_ref[...] += jnp.sum(dyv * nv * sz, axis=0, keepdims=True)

    blk = pl.BlockSpec((tr, LANES), lambda i, h: (i, h))
    vec = pl.BlockSpec((1, LANES), lambda i, h: (0, 0))
    return _pcall(body, name=name, grid=(s // tr, heads),
                  in_specs=[blk, pl.BlockSpec((tr, LANES), lambda i, h: (i, z_col0 + h)), vec, blk],
                  out_specs=(blk, blk, vec),
                  out_shape=(jax.ShapeDtypeStruct((s, gw), F32), jax.ShapeDtypeStruct((s, gw), F32),
                             jax.ShapeDtypeStruct((1, LANES), F32)),
                  semantics=("arbitrary", "arbitrary"), block_bytes=6 * tr * LANES * 4)(
                      o, proj, gain.reshape(1, LANES), dy)


def _dab_reduce(dgb, name):
    s, gw = dgb.shape
    heads = gw // LANES
    tr = _tile(s, 512, SUBLANES)

    def body(d_ref, o_ref, cs_ref):
        acc = d_ref[:, 0:LANES]
        for h in range(1, heads):
            acc = acc + d_ref[:, h * LANES:(h + 1) * LANES]
        o_ref[...] = acc

        @pl.when(pl.program_id(0) == 0)
        def _():
            cs_ref[...] = jnp.zeros_like(cs_ref)

        cs_ref[...] += jnp.sum(acc, axis=0, keepdims=True)

    return _pcall(body, name=name, grid=(s // tr,), in_specs=[pl.BlockSpec((tr, gw), lambda i: (i, 0))],
                  out_specs=(pl.BlockSpec((tr, LANES), lambda i: (i, 0)), pl.BlockSpec((1, LANES), lambda i: (0, 0))),
                  out_shape=(jax.ShapeDtypeStruct((s, LANES), F32), jax.ShapeDtypeStruct((1, LANES), F32)),
                  semantics=("arbitrary",), block_bytes=tr * gw * 4)(dgb)


def _lru_gates(xc, wa, wx, ba, bx, lam):
    r = _sigmoid(_dot(xc, wa) + ba)
    ig = _sigmoid(_dot(xc, wx) + bx)
    sp = _softplus(-lam)
    log_a = -LRU_C * r * sp
    a = jnp.exp(log_a)
    e2 = jnp.exp(2.0 * log_a)
    mult = jnp.sqrt(jnp.maximum(1.0 - e2, 0.0))
    return r, ig, sp, a, e2, mult


def _lru_fwd(xc, proj, y_col0, wa, wx, ba, bx, lam, name):
    s, lw = xc.shape
    nb = lw // LANES
    tr = _tile(s, 256, SUBLANES)

    def body(xc_ref, y_ref, wa_ref, wx_ref, ba_ref, bx_ref, lam_ref, h_ref, o_ref, carry_ref):
        @pl.when(pl.program_id(1) == 0)
        def _():
            carry_ref[...] = jnp.zeros_like(carry_ref)

        xv = xc_ref[...]
        _, ig, _, a, _, mult = _lru_gates(xv, wa_ref[...], wx_ref[...], ba_ref[...], bx_ref[...], lam_ref[...])
        b = mult * (ig * xv)
        row = lax.broadcasted_iota(jnp.int32, (tr, LANES), 0)
        sh = 1
        while sh < tr:
            keep = row >= sh
            b = a * jnp.where(keep, pltpu.roll(b, sh, 0), 0.0) + b
            a = a * jnp.where(keep, pltpu.roll(a, sh, 0), 1.0)
            sh *= 2
        hv = a * carry_ref[0:1, :] + b
        h_ref[...] = hv
        carry_ref[...] = jnp.broadcast_to(hv[tr - 1:tr, :], carry_ref.shape)
        gy, _ = _gelu_and_grad(y_ref[...])
        o_ref[...] = (hv * gy).astype(BF16)

    blk = pl.BlockSpec((tr, LANES), lambda n, i: (i, n))
    wspec = pl.BlockSpec((None, LANES, LANES), lambda n, i: (n, 0, 0))
    vec = pl.BlockSpec((1, LANES), lambda n, i: (0, n))
    return _pcall(body, name=name, grid=(nb, s // tr),
                  in_specs=[blk, pl.BlockSpec((tr, LANES), lambda n, i: (i, y_col0 + n)), wspec, wspec, vec, vec, vec],
                  out_specs=(blk, blk),
                  out_shape=(jax.ShapeDtypeStruct((s, lw), F32), jax.ShapeDtypeStruct((s, lw), BF16)),
                  scratch_shapes=[pltpu.VMEM((SUBLANES, LANES), F32)], semantics=("parallel", "arbitrary"),
                  block_bytes=8 * tr * LANES * 4)(xc, proj, wa, wx, ba.reshape(1, lw), bx.reshape(1, lw),
                                                  lam.reshape(1, lw))


def _lru_bwd(d_out, xc, hseq, proj, y_col0, wa, wx, ba, bx, lam, name):
    s, lw = xc.shape
    nb = lw // LANES
    tr = _tile(s, 256, SUBLANES)
    per = tr // SUBLANES
    ni = s // tr
    nrow8 = s // SUBLANES

    def body(do_ref, xc_ref, xn_ref, h_ref, hp_ref, y_ref, wa_ref, wx_ref, ba_ref, bx_ref, lam_ref,
             dxc_ref, dy_ref, dwa_ref, dwx_ref, dba_ref, dbx_ref, dlam_ref, carry_ref):
        step = pl.program_id(1)
        tile = ni - 1 - step

        @pl.when(step == 0)
        def _():
            carry_ref[...] = jnp.zeros_like(carry_ref)
            dwa_ref[...] = jnp.zeros_like(dwa_ref)
            dwx_ref[...] = jnp.zeros_like(dwx_ref)
            dba_ref[...] = jnp.zeros_like(dba_ref)
            dbx_ref[...] = jnp.zeros_like(dbx_ref)
            dlam_ref[...] = jnp.zeros_like(dlam_ref)

        wav, wxv, bav, bxv, lamv = wa_ref[...], wx_ref[...], ba_ref[...], bx_ref[...], lam_ref[...]
        xv = xc_ref[...]
        r, ig, sp, a, e2, mult = _lru_gates(xv, wav, wxv, bav, bxv, lamv)
        a_next = _lru_gates(xn_ref[...], wav, wxv, bav, bxv, lamv)[3] * (tile < ni - 1).astype(F32)
        hv = h_ref[...]
        h_prev = _shift_down(hv, hp_ref[...] * (tile > 0).astype(F32), 1)
        yv = y_ref[...]
        gy, dgy = _gelu_and_grad(yv)
        dov = do_ref[...]
        dy_ref[...] = dov * hv * dgy
        coef = _shift_up(a, a_next, 1)
        bb = dov * gy
        row = lax.broadcasted_iota(jnp.int32, (tr, LANES), 0)
        sh = 1
        while sh < tr:
            keep = row < tr - sh
            bb = coef * jnp.where(keep, pltpu.roll(bb, tr - sh, 0), 0.0) + bb
            coef = coef * jnp.where(keep, pltpu.roll(coef, tr - sh, 0), 1.0)
            sh *= 2
        lam_t = coef * carry_ref[0:1, :] + bb
        carry_ref[...] = jnp.broadcast_to(lam_t[0:1, :], carry_ref.shape)
        d_a = lam_t * h_prev
        d_mult = lam_t * (ig * xv)
        d_ix = lam_t * mult
        d_la = d_a * a - d_mult * e2 / jnp.maximum(mult, 1e-30)
        d_r = d_la * (-LRU_C * sp)
        dlam_ref[...] += jnp.sum(d_la * (LRU_C * r) * _sigmoid(-lamv), axis=0, keepdims=True)
        d_pa = d_r * r * (1.0 - r)
        d_px = (d_ix * xv) * ig * (1.0 - ig)
        dxc_ref[...] = d_ix * ig + _dot_nt(d_pa, wav) + _dot_nt(d_px, wxv)
        dwa_ref[...] += _dot_tn(xv, d_pa)
        dwx_ref[...] += _dot_tn(xv, d_px)
        dba_ref[...] += jnp.sum(d_pa, axis=0, keepdims=True)
        dbx_ref[...] += jnp.sum(d_px, axis=0, keepdims=True)

    blk = pl.BlockSpec((tr, LANES), lambda n, i: (ni - 1 - i, n))
    nxt = pl.BlockSpec((SUBLANES, LANES), lambda n, i: (jnp.minimum((ni - i) * per, nrow8 - 1), n))
    prv = pl.BlockSpec((SUBLANES, LANES), lambda n, i: (jnp.maximum((ni - 1 - i) * per - 1, 0), n))
    wspec = pl.BlockSpec((None, LANES, LANES), lambda n, i: (n, 0, 0))
    vec = pl.BlockSpec((1, LANES), lambda n, i: (0, n))
    return _pcall(
        body, name=name, grid=(nb, ni),
        in_specs=[blk, blk, nxt, blk, prv, pl.BlockSpec((tr, LANES), lambda n, i: (ni - 1 - i, y_col0 + n)),
                  wspec, wspec, vec, vec, vec],
        out_specs=(blk, blk, wspec, wspec, vec, vec, vec),
        out_shape=(jax.ShapeDtypeStruct((s, lw), F32), jax.ShapeDtypeStruct((s, lw), F32),
                   jax.ShapeDtypeStruct((nb, LANES, LANES), F32), jax.ShapeDtypeStruct((nb, LANES, LANES), F32),
                   jax.ShapeDtypeStruct((1, lw), F32), jax.ShapeDtypeStruct((1, lw), F32),
                   jax.ShapeDtypeStruct((1, lw), F32)),
        scratch_shapes=[pltpu.VMEM((SUBLANES, LANES), F32)], semantics=("parallel", "arbitrary"),
        block_bytes=12 * tr * LANES * 4)(d_out, xc, xc, hseq, hseq, proj, wa, wx, ba.reshape(1, lw),
                                         bx.reshape(1, lw), lam.reshape(1, lw))


def _merge_fwd(proj, gg_col0, gl_col0, bg, bl, name):
    s, d = bg.shape
    tr, tc = _tile(s, 256, SUBLANES), _tile(d, 1024)
    cb = tc // LANES

    def body(gg_ref, gl_ref, bg_ref, bl_ref, o_ref):
        o_ref[...] = (_sigmoid(gg_ref[...]) * bg_ref[...] + _sigmoid(gl_ref[...]) * bl_ref[...]).astype(BF16)

    blk = pl.BlockSpec((tr, tc), lambda i, j: (i, j))
    return _pcall(body, name=name, grid=(s // tr, d // tc),
                  in_specs=[pl.BlockSpec((tr, tc), lambda i, j: (i, gg_col0 // cb + j)),
                            pl.BlockSpec((tr, tc), lambda i, j: (i, gl_col0 // cb + j)), blk, blk],
                  out_specs=blk, out_shape=jax.ShapeDtypeStruct((s, d), BF16), semantics=("parallel", "parallel"),
                  block_bytes=5 * tr * tc * 4)(proj, proj, bg, bl)


def _merge_bwd(proj, gg_col0, gl_col0, bg, bl, dm, name):
    s, d = bg.shape
    tr, tc = _tile(s, 256, SUBLANES), _tile(d, 1024)
    cb = tc // LANES

    def body(gg_ref, gl_ref, bg_ref, bl_ref, dm_ref, dgg_ref, dgl_ref, dbg_ref, dbl_ref):
        dmv = dm_ref[...]
        sg, sl = _sigmoid(gg_ref[...]), _sigmoid(gl_ref[...])
        dgg_ref[...] = (dmv * bg_ref[...] * sg * (1.0 - sg)).astype(BF16)
        dgl_ref[...] = (dmv * bl_ref[...] * sl * (1.0 - sl)).astype(BF16)
        dbg_ref[...] = (dmv * sg).astype(BF16)
        dbl_ref[...] = (dmv * sl).astype(BF16)

    blk = pl.BlockSpec((tr, tc), lambda i, j: (i, j))
    sh = jax.ShapeDtypeStruct((s, d), BF16)
    return _pcall(body, name=name, grid=(s // tr, d // tc),
                  in_specs=[pl.BlockSpec((tr, tc), lambda i, j: (i, gg_col0 // cb + j)),
                            pl.BlockSpec((tr, tc), lambda i, j: (i, gl_col0 // cb + j)), blk, blk, blk],
                  out_specs=(blk, blk, blk, blk), out_shape=(sh, sh, sh, sh), semantics=("parallel", "parallel"),
                  block_bytes=8 * tr * tc * 4)(proj, proj, bg, bl, dm)


def _sum_slots(slots, name):
    n, r, c = slots.shape
    tr = _tile(r, max(2 * SUBLANES, (1 << 19) // (c * 4)), 2 * SUBLANES)

    def body(s_ref, o_ref):
        acc = s_ref[0].astype(F32)
        for q in range(1, n):
            acc = acc + s_ref[q].astype(F32)
        o_ref[...] = acc

    return _pcall(body, name=name, grid=(r // tr,), in_specs=[pl.BlockSpec((n, tr, c), lambda i: (0, i, 0))],
                  out_specs=pl.BlockSpec((tr, c), lambda i: (i, 0)), out_shape=jax.ShapeDtypeStruct((r, c), F32),
                  semantics=("parallel",), block_bytes=(n + 1) * tr * c * 4)(slots)


def _adamw(w, g_parts, m, v, name):
    r, c = w.shape
    np_ = len(g_parts)
    tr = _tile(r, max(SUBLANES, (1 << 20) // (c * 4)), SUBLANES)
    c1 = 1.0 - ADAM_B1 ** ADAM_STEP
    c2 = 1.0 - ADAM_B2 ** ADAM_STEP

    def body(*refs):
        w_ref, m_ref, v_ref = refs[0], refs[1 + np_], refs[2 + np_]
        g_ref, d_ref, nm_ref, nv_ref = refs[3 + np_:]
        g = refs[1][...]
        for p in range(1, np_):
            g = g + refs[1 + p][...]
        nm = ADAM_B1 * m_ref[...] + (1.0 - ADAM_B1) * g
        nv = ADAM_B2 * v_ref[...] + (1.0 - ADAM_B2) * (g * g)
        g_ref[...] = g
        nm_ref[...] = nm
        nv_ref[...] = nv
        d_ref[...] = -ADAM_LR * ((nm / c1) / (jnp.sqrt(nv / c2) + ADAM_EPS) + ADAM_WD * w_ref[...])

    blk = pl.BlockSpec((tr, c), lambda i: (i, 0))
    sh = jax.ShapeDtypeStruct((r, c), F32)
    return _pcall(body, name=name, grid=(r // tr,), in_specs=[blk] * (3 + np_), out_specs=(blk,) * 4,
                  out_shape=(sh,) * 4, semantics=("parallel",), block_bytes=(7 + np_) * tr * c * 4)(
                      w, *g_parts, m, v)


def _pair_sum(core, mine, theirs, name):
    _, n, r, c = mine.shape
    tr = _tile(r, max(2 * SUBLANES, (1 << 19) // (c * 4)), 2 * SUBLANES)

    def body(core_ref, a_ref, b_ref, o_ref):
        o_ref[...] = (a_ref[...].astype(F32) + b_ref[...].astype(F32)).astype(BF16)

    return _pcall(body, name=name, grid=(n, r // tr),
                  in_specs=[pl.BlockSpec((None, None, tr, c), lambda q, i, core_ref: (core_ref[0], q, i, 0)),
                            pl.BlockSpec((None, tr, c), lambda q, i, core_ref: (q, i, 0))],
                  out_specs=pl.BlockSpec((None, tr, c), lambda q, i, core_ref: (q, i, 0)),
                  out_shape=jax.ShapeDtypeStruct((n, r, c), BF16), semantics=("parallel", "parallel"),
                  block_bytes=3 * tr * c * 4, scalar_prefetch=1)(core, mine, theirs)


def _sum_landed(chip, landed, own, name):
    n, r, c = landed.shape
    tr = _tile(r, max(2 * SUBLANES, (1 << 19) // (c * 4)), 2 * SUBLANES)

    def body(chip_ref, l_ref, o_ref, t_ref):
        acc = o_ref[...].astype(F32)
        for q in range(n):
            acc = acc + l_ref[q].astype(F32)
        t_ref[...] = acc

    return _pcall(body, name=name, grid=(r // tr,),
                  in_specs=[pl.BlockSpec((n, tr, c), lambda i, chip_ref: (0, i, 0)),
                            pl.BlockSpec((None, tr, c), lambda i, chip_ref: (chip_ref[0], i, 0))],
                  out_specs=pl.BlockSpec((tr, c), lambda i, chip_ref: (i, 0)),
                  out_shape=jax.ShapeDtypeStruct((r, c), F32), semantics=("parallel",),
                  block_bytes=(n + 3) * tr * c * 4, scalar_prefetch=1)(chip, landed, own)


def _adamw_quarters(core, w, g_mine, g_other, m, v, after, name):
    nl, nh, r, c = w.shape
    tr = _tile(r, max(SUBLANES, (1 << 19) // (c * 4)), SUBLANES)
    c1 = 1.0 - ADAM_B1 ** ADAM_STEP
    c2 = 1.0 - ADAM_B2 ** ADAM_STEP

    def body(core_ref, w_ref, *refs):
        g_refs, (m_ref, v_ref, _, g_ref, d_ref, nm_ref, nv_ref) = refs[:2 * nl], refs[2 * nl:]
        mine = pl.program_id(1) == core_ref[0]
        g = jnp.where(mine, g_refs[0][...], g_refs[nl][...])
        for l in range(1, nl):
            g = jnp.where(pl.program_id(0) == l, jnp.where(mine, g_refs[l][...], g_refs[nl + l][...]), g)
        nm = ADAM_B1 * m_ref[...] + (1.0 - ADAM_B1) * g
        nv = ADAM_B2 * v_ref[...] + (1.0 - ADAM_B2) * (g * g)
        g_ref[...] = g
        nm_ref[...] = nm
        nv_ref[...] = nv
        d_ref[...] = -ADAM_LR * ((nm / c1) / (jnp.sqrt(nv / c2) + ADAM_EPS) + ADAM_WD * w_ref[...])

    blk = pl.BlockSpec((None, None, tr, c), lambda l, hf, i, core_ref: (l, hf, i, 0))
    gblk = pl.BlockSpec((tr, c), lambda l, hf, i, core_ref: (i, 0))
    sh = jax.ShapeDtypeStruct(w.shape, F32)
    return _pcall(body, name=name, grid=(nl, nh, r // tr),
                  in_specs=[blk] + [gblk] * (2 * nl) + [blk, blk, pl.BlockSpec(memory_space=pl.ANY)],
                  out_specs=(blk,) * 4, out_shape=(sh,) * 4, semantics=("parallel", "parallel", "parallel"),
                  block_bytes=(7 + 2 * nl) * tr * c * 4, scalar_prefetch=1)(core, w, *g_mine, *g_other, m, v, after)


HBM_SPEC = pl.BlockSpec(memory_space=pltpu.HBM)


def _other_chips(x, y):
    return [(1 - x, y), (x, 1 - y), (1 - x, 1 - y)]


SEM_SPEC = pl.BlockSpec(memory_space=pltpu.SEMAPHORE)
DATAFLOW_EFFECT = pltpu.SideEffectType.DATAFLOW_SIDE_EFFECTING


def _split_start(name, bufs, n_copies, build):
    nb = len(bufs)

    def body(*refs):
        starts, _ = build(refs[:nb], refs[nb], refs[nb + 1])
        for cp in starts:
            cp.start()
        refs[-1][...] = jnp.zeros_like(refs[-1])

    out = pl.pallas_call(
        body, name=name,
        out_shape=(pltpu.SemaphoreType.DMA((n_copies,)), pltpu.SemaphoreType.DMA((n_copies,)),
                   *[pltpu.HBM(b.shape, b.dtype) for b in bufs], jax.ShapeDtypeStruct((SUBLANES, LANES), F32)),
        in_specs=[HBM_SPEC] * nb,
        out_specs=(SEM_SPEC, SEM_SPEC, *[HBM_SPEC] * nb, pl.BlockSpec(memory_space=pltpu.VMEM)),
        input_output_aliases={i: 2 + i for i in range(nb)},
        compiler_params=pltpu.CompilerParams(has_side_effects=DATAFLOW_EFFECT),
    )(*[pltpu.with_memory_space_constraint(b, pltpu.HBM) for b in bufs])
    return out[0], out[1], list(out[2:2 + nb]), out[2 + nb]


def _split_wait(name, send_sems, recv_sems, bufs, after, build):
    nb = len(bufs)

    def body(*refs):
        starts, waits = build(refs[:nb], refs[nb], refs[nb + 1])
        for cp in starts:
            cp.wait_send()
        for cp in waits:
            cp.wait_recv()

    out = pl.pallas_call(
        body, name=name, out_shape=tuple(pltpu.HBM(b.shape, b.dtype) for b in bufs),
        in_specs=[HBM_SPEC] * nb + [SEM_SPEC, SEM_SPEC, pl.BlockSpec(memory_space=pl.ANY)],
        out_specs=tuple([HBM_SPEC] * nb), input_output_aliases={i: i for i in range(nb)},
        compiler_params=pltpu.CompilerParams(has_side_effects=DATAFLOW_EFFECT),
    )(*bufs, send_sems, recv_sems, after)
    return list(out)


def _gather_ici_copies(nt, refs, send_sems, recv_sems):
    srcs, lands = refs[:nt], refs[nt:]
    x, y, c = lax.axis_index("x"), lax.axis_index("y"), lax.axis_index("c")
    me = 2 * x + y
    starts, waits = [], []
    for t in range(nt):
        for j, (px, py) in enumerate(_other_chips(x, y)):
            def copy(slot, t=t, j=j, px=px, py=py):
                return pltpu.make_async_remote_copy(
                    src_ref=srcs[t].at[c], dst_ref=lands[t].at[slot].at[c], send_sem=send_sems.at[3 * t + j],
                    recv_sem=recv_sems.at[3 * t + j], device_id=(px, py, c), device_id_type=pl.DeviceIdType.MESH)
            starts.append(copy(me))
            waits.append(copy(2 * px + py))
    return starts, waits


def _gather_d2d_copies(nt, refs, send_sems, recv_sems):
    x, y, c = lax.axis_index("x"), lax.axis_index("y"), lax.axis_index("c")
    starts, waits = [], []
    for t in range(nt):
        for j, (px, py) in enumerate(_other_chips(x, y)):
            def copy(half, t=t, j=j, px=px, py=py):
                place = refs[t].at[2 * px + py].at[half]
                return pltpu.make_async_remote_copy(
                    src_ref=place, dst_ref=place, send_sem=send_sems.at[3 * t + j], recv_sem=recv_sems.at[3 * t + j],
                    device_id=(x, y, 1 - c), device_id_type=pl.DeviceIdType.MESH)
            starts.append(copy(c))
            waits.append(copy(1 - c))
    return starts, waits


def _scatter_ici_copies(nt, refs, send_sems, recv_sems):
    srcs, lands = refs[:nt], refs[nt:]
    x, y, c = lax.axis_index("x"), lax.axis_index("y"), lax.axis_index("c")
    me = 2 * x + y
    starts, waits = [], []
    for t in range(nt):
        for j, (px, py) in enumerate(_other_chips(x, y)):
            def copy(slot, t=t, j=j, px=px, py=py):
                return pltpu.make_async_remote_copy(
                    src_ref=srcs[t].at[2 * px + py], dst_ref=lands[t].at[slot], send_sem=send_sems.at[3 * t + j],
                    recv_sem=recv_sems.at[3 * t + j], device_id=(px, py, c), device_id_type=pl.DeviceIdType.MESH)
            starts.append(copy(me))
            waits.append(copy(2 * px + py))
    return starts, waits


def _sibling_half_copies(nt, refs, send_sems, recv_sems):
    srcs, lands = refs[:nt], refs[nt:]
    x, y, c = lax.axis_index("x"), lax.axis_index("y"), lax.axis_index("c")
    copies = [pltpu.make_async_remote_copy(src_ref=srcs[t].at[1 - c], dst_ref=lands[t], send_sem=send_sems.at[t],
                                           recv_sem=recv_sems.at[t], device_id=(x, y, 1 - c),
                                           device_id_type=pl.DeviceIdType.MESH) for t in range(nt)]
    return copies, copies


def _sibling_exchange(arrs, other_layer, name):
    n = len(arrs)

    def body(*refs):
        ins, outs = refs[:n], refs[n:2 * n]
        send_sems, recv_sems = refs[2 * n:]
        c = lax.axis_index("c")
        sib = (lax.axis_index("x"), lax.axis_index("y"), 1 - c)
        copies = [pltpu.make_async_remote_copy(src_ref=ins[t].at[1 - c] if other_layer else ins[t], dst_ref=outs[t],
                                               send_sem=send_sems.at[t], recv_sem=recv_sems.at[t], device_id=sib,
                                               device_id_type=pl.DeviceIdType.MESH) for t in range(n)]
        for cp in copies:
            cp.start()
        for cp in copies:
            cp.wait_recv()
        for cp in copies:
            cp.wait_send()

    return pl.pallas_call(
        body, name=name, in_specs=[HBM_SPEC] * n, out_specs=(HBM_SPEC,) * n,
        out_shape=tuple(jax.ShapeDtypeStruct(a.shape[1:] if other_layer else a.shape, a.dtype) for a in arrs),
        scratch_shapes=[pltpu.SemaphoreType.DMA((n,)), pltpu.SemaphoreType.DMA((n,))])(*arrs)


def _all_devices_copies(nt, refs, send_sems, recv_sems):
    srcs, lands = refs[:nt], refs[nt:]
    x, y, c = lax.axis_index("x"), lax.axis_index("y"), lax.axis_index("c")
    me = 4 * x + 2 * y + c
    starts, waits = [], []
    for t in range(nt):
        for mask in range(1, N_DEVICES):
            px = 1 - x if mask & 4 else x
            py = 1 - y if mask & 2 else y
            pc = 1 - c if mask & 1 else c
            k = (N_DEVICES - 1) * t + mask - 1

            def copy(slot, t=t, k=k, px=px, py=py, pc=pc):
                return pltpu.make_async_remote_copy(
                    src_ref=srcs[t], dst_ref=lands[t].at[slot], send_sem=send_sems.at[k], recv_sem=recv_sems.at[k],
                    device_id=(px, py, pc), device_id_type=pl.DeviceIdType.MESH)
            starts.append(copy(me))
            waits.append(copy(4 * px + 2 * py + pc))
    return starts, waits


def _all_devices_gather(buf, name):
    def body(in_ref, out_ref, send_sems, recv_sems, local_sem):
        x, y, c = lax.axis_index("x"), lax.axis_index("y"), lax.axis_index("c")
        me = 4 * x + 2 * y + c

        def peer(mask):
            px = 1 - x if mask & 4 else x
            py = 1 - y if mask & 2 else y
            pc = 1 - c if mask & 1 else c
            return px, py, pc

        def remote(mask, dst_slot):
            return pltpu.make_async_remote_copy(
                src_ref=in_ref, dst_ref=out_ref.at[dst_slot], send_sem=send_sems.at[mask - 1],
                recv_sem=recv_sems.at[mask - 1], device_id=peer(mask), device_id_type=pl.DeviceIdType.MESH)

        lc = pltpu.make_async_copy(in_ref, out_ref.at[me], local_sem)
        lc.start()
        sends = [remote(mask, me) for mask in range(1, N_DEVICES)]
        for cp in sends:
            cp.start()
        for mask in range(1, N_DEVICES):
            px, py, pc = peer(mask)
            remote(mask, 4 * px + 2 * py + pc).wait_recv()
        for cp in sends:
            cp.wait_send()
        lc.wait()

    return pl.pallas_call(
        body, name=name, in_specs=[HBM_SPEC], out_specs=HBM_SPEC,
        out_shape=jax.ShapeDtypeStruct((N_DEVICES,) + buf.shape, buf.dtype),
        scratch_shapes=[pltpu.SemaphoreType.DMA((N_DEVICES - 1,)), pltpu.SemaphoreType.DMA((N_DEVICES - 1,)),
                        pltpu.SemaphoreType.DMA])(buf)


def _pad_lanes(vec):
    return jnp.pad(vec.astype(F32), (0, LANES - vec.shape[0])).reshape(1, LANES)


def _layer_fwd(x, wl, fetch, dm, tag):
    heads, gw, lw, d = dm['heads'], dm['gw'], dm['lw'], dm['d']
    h = _rms_fwd(x, wl['attn_norm'], f"rms1_fwd{tag}")
    wl.update(fetch('in', h))
    proj = _matmul(h, wl['w_in_p'], mode='nn', tn=PROJ_COL_TILE, name=f"proj{tag}")
    alog, dtb = _pad_lanes(wl['gdn_a_log']), _pad_lanes(wl['gdn_dt_bias'])
    qkv = _gdn_pre_fwd(proj, wl['gdn_conv_w'], heads, f"gdn_pre_fwd{tag}")
    o, t_all, s0_all = _gdn_fwd(qkv, proj, dm['ab_blk'], alog, dtb, heads, f"gdn_fwd{tag}")
    o_gdn = _gdn_post_fwd(o, proj, dm['z_blk'], wl['gdn_norm'], f"gdn_post_fwd{tag}")
    xc = _conv_bias_fwd(proj, dm['xb_blk'], wl['lru_conv_w'], wl['lru_conv_b'], f"lru_conv_fwd{tag}")
    hseq, o_lru = _lru_fwd(xc, proj, dm['yb_blk'], wl['lru_w_a'], wl['lru_w_x'], wl['lru_b_a'], wl['lru_b_x'],
                           wl['lru_lambda'], f"lru_fwd{tag}")
    wl.update(fetch('mix', o))
    bg = _matmul(o_gdn, wl['w_branch_gdn'], mode='nn', name=f"branch_gdn{tag}")
    bl = _matmul(o_lru, wl['w_branch_lru'], mode='nn', name=f"branch_lru{tag}")
    merged = _merge_fwd(proj, dm['gg_blk'], dm['gl_blk'], bg, bl, f"merge_fwd{tag}")
    wl.update(fetch('mlp', bg))
    x_mid = _matmul(merged, wl['w_out'], mode='nn', add=x, name=f"out_proj{tag}")
    h2 = _rms_fwd(x_mid, wl['mlp_norm'], f"rms2_fwd{tag}")
    ur, act = _matmul(h2, wl['w_up'], mode='nn', epilogue='relu2', name=f"mlp_up{tag}")
    x_out = _matmul(act, wl['w_down'], mode='nn', add=x_mid, name=f"mlp_down{tag}")
    saved = dict(x=x, h=h, proj=proj, qkv=qkv, o=o, t_all=t_all, s0_all=s0_all, o_gdn=o_gdn, xc=xc, hseq=hseq,
                 o_lru=o_lru, bg=bg, bl=bl, merged=merged, x_mid=x_mid, h2=h2, ur=ur, act=act, alog=alog, dtb=dtb)
    return x_out, saved


def _layer_bwd(dx_out, dx_out_b, wl, sv, hook, dm, tag):
    heads, gw, lw, d = dm['heads'], dm['gw'], dm['lw'], dm['d']
    g = {}
    du = _matmul(dx_out_b, wl['w_down'], mode='nt', epilogue='mul2x', extra=sv['ur'], out_dtype=BF16,
                 name=f"d_mlp_act{tag}")
    def dw(n, lhs, rhs):
        return _matmul(lhs, rhs, mode='tn', out_dtype=BF16, shard_axis=BIG_SHARD_AXIS[n] - 1, name=f"d{n}{tag}")

    g['w_down'] = dw('w_down', sv['act'], dx_out_b)
    g['w_up'] = dw('w_up', sv['h2'], du)
    hook('mlp', g, wl, 'mlp_norm')
    dh2 = _matmul(du, wl['w_up'], mode='nt', name=f"d_h2{tag}")
    dx_mid, dx_mid_b, g['mlp_norm'] = _rms_bwd(sv['x_mid'], wl['mlp_norm'], dh2, dx_out, f"rms2_bwd{tag}")
    dmerged = _matmul(dx_mid_b, wl['w_out'], mode='nt', name=f"d_merged{tag}")
    g['w_out'] = dw('w_out', sv['merged'], dx_mid_b)
    dgg, dgl, dbg, dbl = _merge_bwd(sv['proj'], dm['gg_blk'], dm['gl_blk'], sv['bg'], sv['bl'], dmerged,
                                    f"merge_bwd{tag}")
    g['w_branch_gdn'] = dw('w_branch_gdn', sv['o_gdn'], dbg)
    g['w_branch_lru'] = dw('w_branch_lru', sv['o_lru'], dbl)
    hook('mix', g, wl, 'gdn_norm')
    do_gdn = _matmul(dbg, wl['w_branch_gdn'], mode='nt', name=f"d_o_gdn{tag}")
    do_lru = _matmul(dbl, wl['w_branch_lru'], mode='nt', name=f"d_o_lru{tag}")
    d_o, dz, dgn = _gdn_post_bwd(sv['o'], sv['proj'], dm['z_blk'], wl['gdn_norm'], do_gdn, f"gdn_post_bwd{tag}")
    g['gdn_norm'] = dgn.reshape(-1)
    dq, dk, dv, dgb = _gdn_bwd(sv['qkv'], sv['proj'], dm['ab_blk'], sv['alog'], sv['dtb'], sv['t_all'], sv['s0_all'], d_o, heads,
                               f"gdn_bwd{tag}")
    dconv = _gdn_pre_bwd(sv['proj'], wl['gdn_conv_w'], dq, dk, dv, heads, f"gdn_pre_bwd{tag}")
    dqkv, g['gdn_conv_w'], _ = _conv_bwd(dconv, sv['proj'], 0, wl['gdn_conv_w'], f"gdn_conv_bwd{tag}")
    dab, dab_sum = _dab_reduce(dgb, f"dab_reduce{tag}")
    g['gdn_dt_bias'] = dab_sum[0, :heads]
    g['gdn_a_log'] = dab_sum[0, 2 * heads:3 * heads]
    dxc, dyb, g['lru_w_a'], g['lru_w_x'], dba, dbx, dlam = _lru_bwd(
        do_lru, sv['xc'], sv['hseq'], sv['proj'], dm['yb_blk'], wl['lru_w_a'], wl['lru_w_x'], wl['lru_b_a'],
        wl['lru_b_x'], wl['lru_lambda'], f"lru_bwd{tag}")
    g['lru_b_a'], g['lru_b_x'], g['lru_lambda'] = dba.reshape(-1), dbx.reshape(-1), dlam.reshape(-1)
    dxb, g['lru_conv_w'], dcb = _conv_bwd(dxc, sv['proj'], dm['xb_blk'], wl['lru_conv_w'], f"lru_conv_bwd{tag}")
    g['lru_conv_b'] = dcb.reshape(-1)
    dproj = jnp.concatenate([dqkv.astype(BF16), dz.astype(BF16), dxb.astype(BF16), dyb.astype(BF16), dgg, dgl,
                             dab.astype(BF16), jnp.zeros((dab.shape[0], dm['np'] - dm['main'] - LANES), BF16)],
                            axis=1)
    g['w_in_p'] = _matmul(sv['h'], dproj, mode='tn', out_dtype=BF16, tn=PROJ_COL_TILE, name=f"dw_in{tag}")
    hook('in', g, wl, 'attn_norm')
    dh = _matmul(dproj, wl['w_in_p'], mode='nt', name=f"d_h{tag}")
    dx_in, dx_in_b, g['attn_norm'] = _rms_bwd(sv['x'], wl['attn_norm'], dh, dx_mid, f"rms1_bwd{tag}")
    g['attn_norm'] = g['attn_norm'].reshape(-1)
    g['mlp_norm'] = g['mlp_norm'].reshape(-1)
    return dx_in, dx_in_b, g


def _dims(d, heads, lw):
    gw = heads * LANES
    nab = 2 * heads
    blk = dict(z_blk=3 * heads, xb_blk=4 * heads, yb_blk=4 * heads + lw // LANES)
    gg0 = 4 * gw + 2 * lw
    main = gg0 + 2 * d
    return dict(d=d, heads=heads, gw=gw, lw=lw, nab=nab, gg_blk=gg0 // LANES, gl_blk=(gg0 + d) // LANES,
                main=main, ab_blk=main // LANES, np=-(-(main + LANES) // PROJ_COL_TILE) * PROJ_COL_TILE, **blk)


def _pad_w_in(w_in, dm):
    c0 = 4 * dm['gw']
    nab = dm['nab']
    return jnp.concatenate([w_in[:, :c0], w_in[:, c0 + nab:], w_in[:, c0:c0 + nab],
                            jnp.zeros((w_in.shape[0], dm['np'] - dm['main'] - nab), w_in.dtype)], axis=1)


def _unpad_w_in(gp, dm):
    c0 = 4 * dm['gw']
    nab = dm['nab']
    main = dm['main']
    return jnp.concatenate([gp[:, :c0], gp[:, main:main + nab], gp[:, c0:main]], axis=1)


def _local_step(x, target, layers, fetchers, hooks, final_norm, dm):
    saved = []
    cur = x
    for li, wl in enumerate(layers):
        cur, sv = _layer_fwd(cur, wl, fetchers[li], dm, f"_l{li}")
        saved.append(sv)
    loss_blk, dx, dx_b, dfin = _loss_head(cur, final_norm, target, "loss_head")
    grads = [None] * len(layers)
    for li in reversed(range(len(layers))):
        dx, dx_b, grads[li] = _layer_bwd(dx, dx_b, layers[li], saved[li], hooks[li], dm, f"_l{li}")
    return loss_blk[0, 0], dx, grads, dfin.reshape(-1)


def kernel(x, attn_norm, w_in, gdn_conv_w, gdn_a_log, gdn_dt_bias, gdn_norm, lru_conv_w, lru_conv_b, lru_w_a, lru_b_a, lru_w_x, lru_b_x, lru_lambda, w_branch_gdn, w_branch_lru, w_out, mlp_norm, w_up, w_down, final_norm, loss_target, m_attn_norm, m_w_in, m_gdn_conv_w, m_gdn_a_log, m_gdn_dt_bias, m_gdn_norm, m_lru_conv_w, m_lru_conv_b, m_lru_w_a, m_lru_b_a, m_lru_w_x, m_lru_b_x, m_lru_lambda, m_w_branch_gdn, m_w_branch_lru, m_w_out, m_mlp_norm, m_w_up, m_w_down, m_final_norm, v_attn_norm, v_w_in, v_gdn_conv_w, v_gdn_a_log, v_gdn_dt_bias, v_gdn_norm, v_lru_conv_w, v_lru_conv_b, v_lru_w_a, v_lru_b_a, v_lru_w_x, v_lru_b_x, v_lru_lambda, v_w_branch_gdn, v_w_branch_lru, v_w_out, v_mlp_norm, v_w_up, v_w_down, v_final_norm):
    w = dict(attn_norm=attn_norm, w_in=w_in, gdn_conv_w=gdn_conv_w, gdn_a_log=gdn_a_log, gdn_dt_bias=gdn_dt_bias,
             gdn_norm=gdn_norm, lru_conv_w=lru_conv_w, lru_conv_b=lru_conv_b, lru_w_a=lru_w_a, lru_b_a=lru_b_a,
             lru_w_x=lru_w_x, lru_b_x=lru_b_x, lru_lambda=lru_lambda, w_branch_gdn=w_branch_gdn,
             w_branch_lru=w_branch_lru, w_out=w_out, mlp_norm=mlp_norm, w_up=w_up, w_down=w_down,
             final_norm=final_norm)
    m = dict(attn_norm=m_attn_norm, w_in=m_w_in, gdn_conv_w=m_gdn_conv_w, gdn_a_log=m_gdn_a_log,
             gdn_dt_bias=m_gdn_dt_bias, gdn_norm=m_gdn_norm, lru_conv_w=m_lru_conv_w, lru_conv_b=m_lru_conv_b,
             lru_w_a=m_lru_w_a, lru_b_a=m_lru_b_a, lru_w_x=m_lru_w_x, lru_b_x=m_lru_b_x, lru_lambda=m_lru_lambda,
             w_branch_gdn=m_w_branch_gdn, w_branch_lru=m_w_branch_lru, w_out=m_w_out, mlp_norm=m_mlp_norm,
             w_up=m_w_up, w_down=m_w_down, final_norm=m_final_norm)
    v = dict(attn_norm=v_attn_norm, w_in=v_w_in, gdn_conv_w=v_gdn_conv_w, gdn_a_log=v_gdn_a_log,
             gdn_dt_bias=v_gdn_dt_bias, gdn_norm=v_gdn_norm, lru_conv_w=v_lru_conv_w, lru_conv_b=v_lru_conv_b,
             lru_w_a=v_lru_w_a, lru_b_a=v_lru_b_a, lru_w_x=v_lru_w_x, lru_b_x=v_lru_b_x, lru_lambda=v_lru_lambda,
             w_branch_gdn=v_w_branch_gdn, w_branch_lru=v_w_branch_lru, w_out=v_w_out, mlp_norm=v_mlp_norm,
             w_up=v_w_up, w_down=v_w_down, final_norm=v_final_norm)
    n_layers = attn_norm.shape[0]
    d = x.shape[-1]
    heads = gdn_a_log.shape[-1]
    lw = lru_conv_b.shape[-1]
    dm = _dims(d, heads, lw)
    big_names = list(BIG_SHARD_AXIS)
    conv_names = list(CONV_SHARD_AXIS)
    chip = 2 * lax.axis_index("x") + lax.axis_index("y")

    conv_flat = jnp.concatenate([w[n].reshape(-1) for n in conv_names])
    conv_rows = -(-conv_flat.shape[0] // (SUBLANES * LANES)) * SUBLANES
    conv_buf = jnp.pad(conv_flat, (0, conv_rows * LANES - conv_flat.shape[0])).reshape(conv_rows, LANES)
    conv_all = _all_devices_gather(conv_buf, "conv_allgather").reshape(N_CHIPS, 2, -1)[:, 0]
    conv_full, off = {}, 0
    for n in conv_names:
        shard = w[n]
        parts = conv_all[:, off:off + shard.size].reshape((N_CHIPS,) + shard.shape)
        conv_full[n] = jnp.concatenate([parts[q] for q in range(N_CHIPS)], axis=CONV_SHARD_AXIS[n])
        off += shard.size

    def start_gather(li, group):
        halves, lands = [], []
        for n in GATHER_GROUPS[group]:
            s = w[n][li].astype(BF16)
            hv = s.reshape((2, s.shape[0] // 2) + s.shape[1:])
            halves.append(hv)
            lands.append(lax.dynamic_update_index_in_dim(lax.empty((N_CHIPS,) + hv.shape, BF16), hv, chip, 0))
        nt = len(halves)
        return _split_start(f"wgather_{group}_l{li}_ici_start", halves + lands, 3 * nt,
                            functools.partial(_gather_ici_copies, nt))

    pending = {(li, group): start_gather(li, group) for li in range(n_layers) for group in GATHER_GROUPS}

    swapping = {}

    def start_swap(li, group, after):
        nt = len(GATHER_GROUPS[group])
        send, recv, bufs, _ = pending.pop((li, group))
        bufs = _split_wait(f"wgather_{group}_l{li}_ici_wait", send, recv, bufs, after,
                           functools.partial(_gather_ici_copies, nt))
        swapping[(li, group)] = _split_start(f"wgather_{group}_l{li}_d2d_start", bufs[nt:], 3 * nt,
                                             functools.partial(_gather_d2d_copies, nt))
        return swapping[(li, group)][3]

    def make_fetch(li):
        def fetch(group, after):
            names = GATHER_GROUPS[group]
            nt = len(names)
            if (li, group) not in swapping:
                start_swap(li, group, after)
            send, recv, lands, _ = swapping.pop((li, group))
            lands = _split_wait(f"wgather_{group}_l{li}_d2d_wait", send, recv, lands, after,
                                functools.partial(_gather_d2d_copies, nt))
            if (li, group) == (0, 'mix'):
                tie = sum(start_swap(l2, g2, lands[0])[0, 0] for l2, g2 in list(pending))
                layers[0]['mlp_norm'] = layers[0]['mlp_norm'] + tie
            out = {}
            for n, land in zip(names, lands):
                slots = land.reshape((N_CHIPS, 2 * land.shape[2]) + land.shape[3:])
                out[n] = jnp.concatenate([slots[q] for q in range(N_CHIPS)], axis=BIG_SHARD_AXIS[n] - 1)
            if 'w_in' in out:
                out['w_in_p'] = _pad_w_in(out.pop('w_in'), dm)
            return out
        return fetch

    layers = []
    for li in range(n_layers):
        wl = {n: w[n][li] for n in SMALL_NAMES if n != 'final_norm' and n not in CONV_SHARD_AXIS}
        for n in conv_names:
            wl[n] = conv_full[n][li]
        layers.append(wl)
    layers[0]['attn_norm'] = layers[0]['attn_norm'] + sum(handle[3][0, 0] for handle in pending.values())

    core = lax.axis_index("c").astype(jnp.int32).reshape(1)
    chip_op = chip.astype(jnp.int32).reshape(1)
    sending, in_flight, reduced = {}, {}, {}

    def reduce_begin(group, li, g):
        names = GATHER_GROUPS[group]
        nt = len(names)
        contrib = []
        for n in names:
            if n != 'w_in':
                contrib.append(g[n])
                continue
            pieces = jnp.stack(jnp.split(_unpad_w_in(g['w_in_p'], dm), N_CHIPS, axis=BIG_SHARD_AXIS[n] - 1), axis=0)
            rows_half = pieces.shape[1] // 2
            contrib.append(jnp.swapaxes(pieces.reshape((N_CHIPS, 2, rows_half) + pieces.shape[2:]), 0, 1))
        theirs = [lax.empty(cb.shape[1:], BF16) for cb in contrib]
        send, recv, bufs, token = _split_start(f"gsend_{group}_l{li}_start", contrib + theirs, nt,
                                               functools.partial(_sibling_half_copies, nt))
        sending[(group, li)] = (send, recv, bufs)
        return token

    def reduce_scatter(group, li, after):
        names = GATHER_GROUPS[group]
        nt = len(names)
        send, recv, bufs = sending.pop((group, li))
        bufs = _split_wait(f"gsend_{group}_l{li}_wait", send, recv, bufs, after,
                           functools.partial(_sibling_half_copies, nt))
        sums = [_pair_sum(core, mine, th, f"gpair_{n}_l{li}") for n, mine, th in zip(names, bufs[:nt], bufs[nt:])]
        lands = [jnp.zeros(sm.shape, BF16) for sm in sums]
        send, recv, bufs, token = _split_start(f"gscatter_{group}_l{li}_start", sums + lands, 3 * nt,
                                               functools.partial(_scatter_ici_copies, nt))
        in_flight[(group, li)] = (send, recv, bufs)
        return token

    def reduce_end(jobs, after, name):
        keys, totals = [], []
        for group, li in jobs:
            names = GATHER_GROUPS[group]
            nt = len(names)
            send, recv, bufs = in_flight.pop((group, li))
            bufs = _split_wait(f"gscatter_{group}_l{li}_wait", send, recv, bufs, after,
                               functools.partial(_scatter_ici_copies, nt))
            totals += [_sum_landed(chip_op, land, own, f"gtotal_{n}_l{li}")
                       for n, own, land in zip(names, bufs[:nt], bufs[nt:])]
            keys += [(n, li) for n in names]
        others = _sibling_exchange(totals, False, name)
        for key, mine, other in zip(keys, totals, others):
            reduced[key] = (mine, other)

    order = [(group, li) for li in reversed(range(n_layers)) for group in reversed(list(GATHER_GROUPS))]

    def make_hook(li):
        def hook(group, g, wl, gain):
            at = order.index((group, li))
            latest = g['w_in_p'] if group == 'in' else g[GATHER_GROUPS[group][-1]]
            tie = reduce_begin(group, li, g)[0, 0]
            if at > 0:
                tie = tie + reduce_scatter(*order[at - 1], latest)[0, 0]
            wl[gain] = wl[gain] + tie
        return hook

    loss_local, dx, grads, dfin = _local_step(x[0], loss_target[0], layers, [make_fetch(li) for li in range(n_layers)],
                                              [make_hook(li) for li in range(n_layers)], final_norm, dm)
    loss = lax.psum(loss_local, MESH_AXES)
    last_scatter_token = reduce_scatter(*order[-1], dx)

    small_g = {n: jnp.stack([grads[li][n] for li in range(n_layers)], axis=0)
               for n in SMALL_NAMES if n != 'final_norm'}
    small_g['final_norm'] = dfin
    flat = jnp.concatenate([small_g[n].reshape(-1) for n in PACK_NAMES])
    n_flat = flat.shape[0]
    row_unit = 32 * SUBLANES
    rows = -(-n_flat // (row_unit * LANES)) * row_unit
    small_srcs = [jnp.pad(flat, (0, rows * LANES - n_flat)).reshape(rows, LANES)]
    small_srcs += [small_g[n].reshape(-1, LANES) for n in MID_NAMES]
    device = 2 * chip + lax.axis_index("c")
    small_lands = [lax.dynamic_update_index_in_dim(lax.empty((N_DEVICES,) + b.shape, F32), b, device, 0)
                   for b in small_srcs]
    n_small = len(small_srcs)
    small_copies = functools.partial(_all_devices_copies, n_small)
    small_send, small_recv, small_bufs, small_token = _split_start(
        "small_grad_start", small_srcs + small_lands, (N_DEVICES - 1) * n_small, small_copies)

    out_g, out_d, out_m, out_v = {}, {}, {}, {}

    def adamw_big(n, after):
        quarters = (n_layers, 2, w[n].shape[1] // 2, w[n].shape[2])
        res = _adamw_quarters(core, w[n].reshape(quarters), [reduced[(n, li)][0] for li in range(n_layers)],
                              [reduced[(n, li)][1] for li in range(n_layers)], m[n].reshape(quarters),
                              v[n].reshape(quarters), after, f"adamw_{n}")
        out_g[n], out_d[n], out_m[n], out_v[n] = (r.reshape(w[n].shape) for r in res)
        return res[1]

    early = [group for group in GATHER_GROUPS if group != 'in']
    last = small_token + last_scatter_token
    reduce_end([job for job in order if job[0] != 'in'], last, "gswap_early")
    for group in early:
        for n in GATHER_GROUPS[group]:
            last = adamw_big(n, last)
    reduce_end([job for job in order if job[0] == 'in'], last, "gswap_in")
    last = adamw_big('w_in', last)
    small_bufs = _split_wait("small_grad_wait", small_send, small_recv, small_bufs, last, small_copies)
    small_sums = [_sum_slots(land, f"small_grad_sum_{k}") for k, land in enumerate(small_bufs[n_small:])]
    small_red = {}
    off = 0
    for n in PACK_NAMES:
        size = small_g[n].size
        small_red[n] = small_sums[0].reshape(-1)[off:off + size].reshape(small_g[n].shape)
        off += size
    for n, g2d in zip(MID_NAMES, small_sums[1:]):
        res = _adamw(w[n].reshape(g2d.shape), [g2d], m[n].reshape(g2d.shape), v[n].reshape(g2d.shape), f"adamw_{n}")
        out_g[n], out_d[n], out_m[n], out_v[n] = (r.reshape(w[n].shape) for r in res)
    for n, ax in CONV_SHARD_AXIS.items():
        width = w[n].shape[ax]
        small_red[n] = lax.dynamic_slice_in_dim(small_red[n], chip * width, width, axis=ax)

    def pack(tree):
        fl = jnp.concatenate([tree[n].reshape(-1) for n in PACK_NAMES])
        return jnp.pad(fl, (0, rows * LANES - fl.shape[0])).reshape(rows, LANES)

    res = _adamw(pack(w), [pack(small_red)], pack(m), pack(v), "adamw_small")
    for r, dst in zip(res, (out_g, out_d, out_m, out_v)):
        fl = r.reshape(-1)
        off = 0
        for n in PACK_NAMES:
            dst[n] = fl[off:off + w[n].size].reshape(w[n].shape)
            off += w[n].size

    return (loss, dx[None], *[out_g[n] for n in WEIGHT_NAMES], *[out_d[n] for n in WEIGHT_NAMES],
            *[out_m[n] for n in WEIGHT_NAMES], *[out_v[n] for n in WEIGHT_NAMES])
```

```python
import functools

import jax
import jax.numpy as jnp
from jax import lax
from jax.experimental import pallas as pl
from jax.experimental.pallas import tpu as pltpu

F32 = jnp.float32
BF16 = jnp.bfloat16

LANES = 128
SUBLANES = 8
VMEM_BYTES = 64 * 1024 * 1024
GDN_CHUNK = 64
CONV_WIDTH = 4
CONV_ROW_TILE = 2048
RMS_EPS = 1e-6
L2_EPS = 1e-6
LRU_C = 8.0
ADAM_LR = 0.001
ADAM_B1 = 0.9
ADAM_B2 = 0.999
ADAM_EPS = 1e-08
ADAM_WD = 0.01
ADAM_STEP = 10
MESH_AXES = ("x", "y", "c")
N_CHIPS = 4
N_DEVICES = 8

INPUT_NAMES = ['x', 'attn_norm', 'w_in', 'gdn_conv_w', 'gdn_a_log', 'gdn_dt_bias', 'gdn_norm', 'lru_conv_w',
               'lru_conv_b', 'lru_w_a', 'lru_b_a', 'lru_w_x', 'lru_b_x', 'lru_lambda', 'w_branch_gdn',
               'w_branch_lru', 'w_out', 'mlp_norm', 'w_up', 'w_down', 'final_norm']
WEIGHT_NAMES = INPUT_NAMES[1:]
BIG_SHARD_AXIS = {'w_in': 2, 'w_branch_gdn': 2, 'w_branch_lru': 2, 'w_out': 1, 'w_up': 2, 'w_down': 1}
CONV_SHARD_AXIS = {'gdn_conv_w': 2, 'lru_conv_w': 2}
GATHER_GROUPS = {'in': ['w_in'], 'mix': ['w_branch_gdn', 'w_branch_lru', 'w_out'], 'mlp': ['w_up', 'w_down']}
SMALL_NAMES = [n for n in WEIGHT_NAMES if n not in BIG_SHARD_AXIS]
MID_NAMES = ['lru_w_a', 'lru_w_x']
PACK_NAMES = [n for n in SMALL_NAMES if n not in MID_NAMES]


def _tile(n, target, unit=LANES):
    best = None
    t = unit
    while t <= min(n, target):
        if n % t == 0:
            best = t
        t += unit
    return n if best is None else best


def _vmem_limit(block_bytes):
    return int(min(max(3 * block_bytes + (8 << 20), 24 << 20), VMEM_BYTES - (8 << 20)))


def _nbytes(shape, dtype):
    n = 1
    for s in shape:
        n *= s
    return n * jnp.dtype(dtype).itemsize


def _pcall(body, *, name, grid, in_specs, out_specs, out_shape, scratch_shapes=(), semantics=None, block_bytes=0,
           scalar_prefetch=0):
    params = dict(vmem_limit_bytes=_vmem_limit(block_bytes))
    if semantics is not None:
        params['dimension_semantics'] = semantics
    if scalar_prefetch:
        grid_spec = pltpu.PrefetchScalarGridSpec(num_scalar_prefetch=scalar_prefetch, grid=grid, in_specs=in_specs,
                                                 out_specs=out_specs, scratch_shapes=list(scratch_shapes))
        return pl.pallas_call(body, name=name, grid_spec=grid_spec, out_shape=out_shape,
                              compiler_params=pltpu.CompilerParams(**params))
    return pl.pallas_call(body, name=name, grid=grid, in_specs=in_specs, out_specs=out_specs, out_shape=out_shape,
                          scratch_shapes=list(scratch_shapes), compiler_params=pltpu.CompilerParams(**params))


def _dot(a, b):
    return jnp.dot(a.astype(BF16), b.astype(BF16), preferred_element_type=F32)


def _dot_nt(a, b):
    return lax.dot_general(a.astype(BF16), b.astype(BF16), (((1,), (1,)), ((), ())), preferred_element_type=F32)


def _dot_tn(a, b):
    return lax.dot_general(a.astype(BF16), b.astype(BF16), (((0,), (0,)), ((), ())), preferred_element_type=F32)


def _sigmoid(x):
    return 1.0 / (1.0 + jnp.exp(-x))


def _log1p(u):
    return jnp.where(u < 1e-3, u * (1.0 - u * (0.5 - u * (1.0 / 3.0))), jnp.log(1.0 + u))


def _softplus(x):
    return jnp.maximum(x, 0.0) + _log1p(jnp.exp(-jnp.abs(x)))


_GELU_K = 0.7978845608028654


def _gelu_and_grad(x):
    inner = _GELU_K * (x + 0.044715 * x * x * x)
    th = jnp.tanh(inner)
    g = 0.5 * x * (1.0 + th)
    dg = 0.5 * (1.0 + th) + 0.5 * x * (1.0 - th * th) * _GELU_K * (1.0 + 3.0 * 0.044715 * x * x)
    return g, dg


MATMUL_TK_MAX = 3584
PROJ_COL_TILE = 1536


def _matmul(a, b, *, mode, name, out_dtype=F32, add=None, epilogue=None, extra=None, tm=512, tn=1024, tk=2048,
            shard_axis=None):
    if mode == 'nn':
        (m, k), (k2, n) = a.shape, b.shape
    elif mode == 'nt':
        (m, k), (n, k2) = a.shape, b.shape
    else:
        (k, m), (k2, n) = a.shape, b.shape
    assert k == k2, (a.shape, b.shape, mode)
    if shard_axis is not None:
        rows_half = (m // N_CHIPS if shard_axis == 0 else m) // 2
        cols = n // N_CHIPS if shard_axis == 1 else n
        tm, tn = _tile(rows_half, tm), _tile(cols, tn)
    else:
        tm, tn = _tile(m, tm), _tile(n, tn)
    tk = _tile(k, tk)
    if k // tk > 2 * (-(-k // MATMUL_TK_MAX)):
        tk = _tile(k, MATMUL_TK_MAX)
    nk = k // tk
    dims = {'nn': (((1,), (0,)), ((), ())), 'nt': (((1,), (1,)), ((), ())), 'tn': (((0,), (0,)), ((), ()))}[mode]
    a_bytes, b_bytes = _nbytes(a.shape, a.dtype), _nbytes(b.shape, b.dtype)
    rows_outer = nk > 1 or a_bytes + (m // tm) * b_bytes <= b_bytes + (n // tn) * a_bytes

    def ij(g0, g1):
        return (g0, g1) if rows_outer else (g1, g0)

    def spec(shape, pick):
        return pl.BlockSpec(shape, lambda g0, g1, kk: pick(*ij(g0, g1), kk))

    a_spec = spec((tk, tm), lambda i, j, kk: (kk, i)) if mode == 'tn' else spec((tm, tk), lambda i, j, kk: (i, kk))
    b_spec = spec((tn, tk), lambda i, j, kk: (j, kk)) if mode == 'nt' else spec((tk, tn), lambda i, j, kk: (kk, j))
    o_spec = spec((tm, tn), lambda i, j, kk: (i, j))
    operands, in_specs = [a, b], [a_spec, b_spec]
    if add is not None:
        operands.append(add)
        in_specs.append(o_spec)
    if extra is not None:
        operands.append(extra)
        in_specs.append(o_spec)
    n_in = len(operands)
    if epilogue == 'relu2':
        out_shape = (jax.ShapeDtypeStruct((m, n), BF16), jax.ShapeDtypeStruct((m, n), BF16))
        out_specs = (o_spec, o_spec)
    elif shard_axis is not None:
        assert add is None and extra is None
        rb, cb = rows_half // tm, cols // tn

        def shard_block(i, j, kk):
            if shard_axis == 0:
                return (i % (2 * rb)) // rb, i // (2 * rb), i % rb, j
            return i // rb, j // cb, i % rb, j % cb

        out_shape = jax.ShapeDtypeStruct((2, N_CHIPS, rows_half, cols), out_dtype)
        out_specs = spec((None, None, tm, tn), shard_block)
    else:
        out_shape = jax.ShapeDtypeStruct((m, n), out_dtype)
        out_specs = o_spec

    def body(*refs):
        a_ref, b_ref = refs[0], refs[1]
        outs = refs[n_in:n_in + n_out]

        def finish(p):
            if add is not None:
                p = p + refs[2][...]
            if epilogue == 'relu2':
                ur = jnp.maximum(p, 0.0)
                outs[0][...] = ur.astype(BF16)
                outs[1][...] = (ur * ur).astype(BF16)
            elif epilogue == 'mul2x':
                outs[0][...] = (p * 2.0 * refs[n_in - 1][...].astype(F32)).astype(out_dtype)
            else:
                outs[0][...] = p.astype(out_dtype)

        prod = lax.dot_general(a_ref[...].astype(BF16), b_ref[...].astype(BF16), dims, preferred_element_type=F32)
        if nk == 1:
            finish(prod)
            return
        acc_ref = refs[-1]
        kk = pl.program_id(2)

        @pl.when(kk == 0)
        def _():
            acc_ref[...] = prod

        @pl.when((kk > 0) & (kk < nk - 1))
        def _():
            acc_ref[...] += prod

        @pl.when(kk == nk - 1)
        def _():
            finish(acc_ref[...] + prod)

    n_out = 2 if epilogue == 'relu2' else 1
    bb = (_nbytes((tm, tk), a.dtype) + _nbytes((tk, tn), b.dtype) + 3 * _nbytes((tm, tn), F32))
    grid = (m // tm, n // tn, nk) if rows_outer else (n // tn, m // tm, nk)
    return _pcall(body, name=name, grid=grid, in_specs=in_specs, out_specs=out_specs, out_shape=out_shape,
                  scratch_shapes=[pltpu.VMEM((tm, tn), F32)] if nk > 1 else [],
                  semantics=("parallel", "parallel", "arbitrary"), block_bytes=bb)(*operands)


def _row_tile(s, d, target_bytes=1 << 20):
    return _tile(s, max(SUBLANES, target_bytes // (4 * d)), SUBLANES)


def _rms_fwd(x, gain, name):
    s, d = x.shape
    tr = _row_tile(s, d)

    def body(x_ref, g_ref, h_ref):
        xv = x_ref[...]
        r = lax.rsqrt(jnp.mean(xv * xv, axis=-1, keepdims=True) + RMS_EPS)
        h_ref[...] = (xv * r * g_ref[...]).astype(BF16)

    row = pl.BlockSpec((tr, d), lambda i: (i, 0))
    return _pcall(body, name=name, grid=(s // tr,), in_specs=[row, pl.BlockSpec((1, d), lambda i: (0, 0))],
                  out_specs=row, out_shape=jax.ShapeDtypeStruct((s, d), BF16), semantics=("parallel",),
                  block_bytes=2 * tr * d * 4)(x, gain.reshape(1, d))


def _rms_bwd(x, gain, dh, dres, name):
    s, d = x.shape
    tr = _row_tile(s, d, 1 << 19)

    def body(x_ref, g_ref, dh_ref, dres_ref, dx_ref, dxb_ref, dg_ref):
        xv = x_ref[...]
        r = lax.rsqrt(jnp.mean(xv * xv, axis=-1, keepdims=True) + RMS_EPS)
        xh = xv * r
        dhv = dh_ref[...]
        dxh = dhv * g_ref[...]
        dx = dres_ref[...] + r * (dxh - xh * jnp.mean(dxh * xh, axis=-1, keepdims=True))
        dx_ref[...] = dx
        dxb_ref[...] = dx.astype(BF16)

        @pl.when(pl.program_id(0) == 0)
        def _():
            dg_ref[...] = jnp.zeros_like(dg_ref)

        dg_ref[...] += jnp.sum(dhv * xh, axis=0, keepdims=True)

    row = pl.BlockSpec((tr, d), lambda i: (i, 0))
    vec = pl.BlockSpec((1, d), lambda i: (0, 0))
    return _pcall(body, name=name, grid=(s // tr,), in_specs=[row, vec, row, row], out_specs=(row, row, vec),
                  out_shape=(jax.ShapeDtypeStruct((s, d), F32), jax.ShapeDtypeStruct((s, d), BF16),
                             jax.ShapeDtypeStruct((1, d), F32)),
                  semantics=("arbitrary",), block_bytes=5 * tr * d * 4)(x, gain.reshape(1, d), dh, dres)


def _loss_head(x, gain, target, name):
    s, d = x.shape
    tr = _row_tile(s, d, 1 << 19)

    def body(x_ref, g_ref, t_ref, loss_ref, dx_ref, dxb_ref, dg_ref):
        xv = x_ref[...]
        r = lax.rsqrt(jnp.mean(xv * xv, axis=-1, keepdims=True) + RMS_EPS)
        xh = xv * r
        gv = g_ref[...]
        err = xh * gv - t_ref[...]
        dy = err * (1.0 / d)
        dxh = dy * gv
        dx = r * (dxh - xh * jnp.mean(dxh * xh, axis=-1, keepdims=True))
        dx_ref[...] = dx
        dxb_ref[...] = dx.astype(BF16)

        @pl.when(pl.program_id(0) == 0)
        def _():
            dg_ref[...] = jnp.zeros_like(dg_ref)
            loss_ref[...] = jnp.zeros_like(loss_ref)

        dg_ref[...] += jnp.sum(dy * xh, axis=0, keepdims=True)
        part = jnp.sum(jnp.sum(err * err, axis=-1, keepdims=True), axis=0, keepdims=True) * (0.5 / d)
        loss_ref[...] += jnp.broadcast_to(part, loss_ref.shape)

    row = pl.BlockSpec((tr, d), lambda i: (i, 0))
    vec = pl.BlockSpec((1, d), lambda i: (0, 0))
    lspec = pl.BlockSpec((SUBLANES, LANES), lambda i: (0, 0))
    return _pcall(body, name=name, grid=(s // tr,), in_specs=[row, vec, row], out_specs=(lspec, row, row, vec),
                  out_shape=(jax.ShapeDtypeStruct((SUBLANES, LANES), F32), jax.ShapeDtypeStruct((s, d), F32),
                             jax.ShapeDtypeStruct((s, d), BF16), jax.ShapeDtypeStruct((1, d), F32)),
                  semantics=("arbitrary",), block_bytes=4 * tr * d * 4)(x, gain.reshape(1, d), target)


def _shift_down(xc, xp, s):
    tr = xc.shape[0]
    r = pltpu.roll(xc, s, 0)
    p = pltpu.roll(xp, s, 0)
    row8 = lax.broadcasted_iota(jnp.int32, (SUBLANES, xc.shape[1]), 0)
    head = jnp.where(row8 < s, p, r[:SUBLANES])
    if tr == SUBLANES:
        return head
    return jnp.concatenate([head, r[SUBLANES:]], axis=0)


def _shift_up(yc, yn, s):
    tr = yc.shape[0]
    u = pltpu.roll(yc, tr - s, 0)
    n = pltpu.roll(yn, SUBLANES - s, 0)
    row8 = lax.broadcasted_iota(jnp.int32, (SUBLANES, yc.shape[1]), 0)
    tail = jnp.where(row8 >= SUBLANES - s, n, u[tr - SUBLANES:])
    if tr == SUBLANES:
        return tail
    return jnp.concatenate([u[:tr - SUBLANES], tail], axis=0)


def _conv_apply(xc, xp, w):
    y = xc * w[CONV_WIDTH - 1:CONV_WIDTH, :]
    for s in range(1, CONV_WIDTH):
        y = y + _shift_down(xc, xp, s) * w[CONV_WIDTH - 1 - s:CONV_WIDTH - s, :]
    return y


def _halo_specs(tr, col_of):
    per = tr // SUBLANES
    cur = pl.BlockSpec((tr, LANES), lambda j, i: (i, col_of(j)))
    prev = pl.BlockSpec((SUBLANES, LANES), lambda j, i: (jnp.maximum(i * per - 1, 0), col_of(j)))
    return cur, prev


def _conv_bias_fwd(x_arr, x_col0, w, bias, name):
    s = x_arr.shape[0]
    ncb = w.shape[1] // LANES
    tr = _tile(s, CONV_ROW_TILE, SUBLANES)

    def body(cur_ref, prev_ref, w_ref, b_ref, o_ref):
        i = pl.program_id(1)
        xp = prev_ref[...] * (i > 0).astype(F32)
        o_ref[...] = _conv_apply(cur_ref[...], xp, w_ref[...]) + b_ref[...]

    cur, prev = _halo_specs(tr, lambda j: x_col0 + j)
    return _pcall(body, name=name, grid=(ncb, s // tr),
                  in_specs=[cur, prev, pl.BlockSpec((CONV_WIDTH, LANES), lambda j, i: (0, j)),
                            pl.BlockSpec((1, LANES), lambda j, i: (0, j))],
                  out_specs=pl.BlockSpec((tr, LANES), lambda j, i: (i, j)),
                  out_shape=jax.ShapeDtypeStruct((s, w.shape[1]), F32), semantics=("parallel", "parallel"),
                  block_bytes=3 * tr * LANES * 4)(x_arr, x_arr, w, bias.reshape(1, -1))


def _conv_bwd(dy, x_arr, x_col0, w, name):
    s, c = dy.shape
    ncb = c // LANES
    tr = _tile(s, CONV_ROW_TILE, SUBLANES)
    per = tr // SUBLANES
    ni = s // tr

    def body(dy_ref, dyn_ref, cur_ref, prev_ref, w_ref, dx_ref, dw_ref, db_ref):
        i = pl.program_id(1)
        dyv = dy_ref[...]
        dn = dyn_ref[...] * (i < ni - 1).astype(F32)
        xc = cur_ref[...]
        xp = prev_ref[...] * (i > 0).astype(F32)
        wv = w_ref[...]

        @pl.when(i == 0)
        def _():
            dw_ref[...] = jnp.zeros_like(dw_ref)
            db_ref[...] = jnp.zeros_like(db_ref)

        dx = dyv * wv[CONV_WIDTH - 1:CONV_WIDTH, :]
        dw_ref[CONV_WIDTH - 1:CONV_WIDTH, :] += jnp.sum(dyv * xc, axis=0, keepdims=True)
        for sh in range(1, CONV_WIDTH):
            j = CONV_WIDTH - 1 - sh
            dx = dx + _shift_up(dyv, dn, sh) * wv[j:j + 1, :]
            dw_ref[j:j + 1, :] += jnp.sum(dyv * _shift_down(xc, xp, sh), axis=0, keepdims=True)
        dx_ref[...] = dx
        db_ref[...] += jnp.sum(dyv, axis=0, keepdims=True)

    cur, prev = _halo_specs(tr, lambda j: x_col0 + j)
    dcur = pl.BlockSpec((tr, LANES), lambda j, i: (i, j))
    dnext = pl.BlockSpec((SUBLANES, LANES), lambda j, i: (jnp.minimum((i + 1) * per, s // SUBLANES - 1), j))
    return _pcall(body, name=name, grid=(ncb, ni),
                  in_specs=[dcur, dnext, cur, prev, pl.BlockSpec((CONV_WIDTH, LANES), lambda j, i: (0, j))],
                  out_specs=(dcur, pl.BlockSpec((CONV_WIDTH, LANES), lambda j, i: (0, j)),
                             pl.BlockSpec((1, LANES), lambda j, i: (0, j))),
                  out_shape=(jax.ShapeDtypeStruct((s, c), F32), jax.ShapeDtypeStruct((CONV_WIDTH, c), F32),
                             jax.ShapeDtypeStruct((1, c), F32)),
                  semantics=("parallel", "arbitrary"), block_bytes=4 * tr * LANES * 4)(dy, dy, x_arr, x_arr, w)


def _gdn_pre_fwd(proj, conv_w, heads, name):
    s = proj.shape[0]
    ncb = conv_w.shape[1] // LANES
    tr = _tile(s, CONV_ROW_TILE, SUBLANES)
    qscale = float(LANES) ** -0.5

    def body(cur_ref, prev_ref, w_ref, o_ref):
        j, i = pl.program_id(0), pl.program_id(1)
        xp = prev_ref[...] * (i > 0).astype(F32)
        cv = _conv_apply(cur_ref[...], xp, w_ref[...])
        sv = cv * _sigmoid(cv)
        nrm = lax.rsqrt(jnp.sum(sv * sv, axis=-1, keepdims=True) + L2_EPS)
        scale = jnp.where(j < heads, qscale, 1.0)
        o_ref[...] = jnp.where(j < 2 * heads, sv * nrm * scale, sv)

    cur, prev = _halo_specs(tr, lambda j: j)
    return _pcall(body, name=name, grid=(ncb, s // tr),
                  in_specs=[cur, prev, pl.BlockSpec((CONV_WIDTH, LANES), lambda j, i: (0, j))],
                  out_specs=pl.BlockSpec((tr, LANES), lambda j, i: (i, j)),
                  out_shape=jax.ShapeDtypeStruct((s, conv_w.shape[1]), F32), semantics=("parallel", "parallel"),
                  block_bytes=3 * tr * LANES * 4)(proj, proj, conv_w)


def _gdn_pre_bwd(proj, conv_w, dq, dk, dv, heads, name):
    s = proj.shape[0]
    ncb = conv_w.shape[1] // LANES
    tr = _tile(s, CONV_ROW_TILE, SUBLANES)
    qscale = float(LANES) ** -0.5

    def body(cur_ref, prev_ref, w_ref, dq_ref, dk_ref, dv_ref, o_ref):
        j, i = pl.program_id(0), pl.program_id(1)
        xp = prev_ref[...] * (i > 0).astype(F32)
        cv = _conv_apply(cur_ref[...], xp, w_ref[...])
        sg = _sigmoid(cv)
        sv = cv * sg
        nrm = lax.rsqrt(jnp.sum(sv * sv, axis=-1, keepdims=True) + L2_EPS)
        dv = jnp.where(j < heads, dq_ref[...], jnp.where(j < 2 * heads, dk_ref[...], dv_ref[...]))
        scale = jnp.where(j < heads, qscale, 1.0)
        dsn = scale * nrm * (dv - sv * (nrm * nrm) * jnp.sum(dv * sv, axis=-1, keepdims=True))
        ds = jnp.where(j < 2 * heads, dsn, dv)
        o_ref[...] = ds * (sg * (1.0 + cv * (1.0 - sg)))

    cur, prev = _halo_specs(tr, lambda j: j)
    blk = pl.BlockSpec((tr, LANES), lambda j, i: (i, j))

    def part(k):
        return pl.BlockSpec((tr, LANES), lambda j, i: (i, jnp.clip(j - k * heads, 0, heads - 1)))

    return _pcall(body, name=name, grid=(ncb, s // tr),
                  in_specs=[cur, prev, pl.BlockSpec((CONV_WIDTH, LANES), lambda j, i: (0, j)), part(0), part(1),
                            part(2)],
                  out_specs=blk, out_shape=jax.ShapeDtypeStruct((s, conv_w.shape[1]), F32),
                  semantics=("parallel", "parallel"), block_bytes=6 * tr * LANES * 4)(proj, proj, conv_w, dq, dk, dv)


def _dot_split(a, b, dims=(((1,), (0,)), ((), ()))):
    a_hi, b_hi = a.astype(BF16), b.astype(BF16)
    a_lo, b_lo = (a - a_hi.astype(F32)).astype(BF16), (b - b_hi.astype(F32)).astype(BF16)

    def dot(u, v):
        return lax.dot_general(u, v, dims, preferred_element_type=F32)

    return dot(a_hi, b_hi) + dot(a_hi, b_lo) + dot(a_lo, b_hi)


def _tri_inverse(a_strict, block):
    n = a_strict.shape[0]
    ri = lax.broadcasted_iota(jnp.int32, (n, n), 0)
    ci = lax.broadcasted_iota(jnp.int32, (n, n), 1)
    same8 = (ri >> 3) == (ci >> 3)
    sel = jnp.where((lax.broadcasted_iota(jnp.int32, (n, LANES), 0) & 7)
                    == lax.broadcasted_iota(jnp.int32, (n, LANES), 1), 1.0, 0.0)
    a8 = _dot_split(jnp.where(same8, a_strict, 0.0), sel)
    t8 = sel
    r_in = lax.broadcasted_iota(jnp.int32, (n, 1), 0) & 7
    for j in range(SUBLANES - 1):
        row_j = jnp.broadcast_to(t8.reshape(n // SUBLANES, SUBLANES, LANES)[:, j:j + 1, :],
                                 (n // SUBLANES, SUBLANES, LANES)).reshape(n, LANES)
        t8 = t8 - jnp.where(r_in > j, a8[:, j:j + 1] * row_j, 0.0)
    t = jnp.where(same8, _dot_split(t8, sel, (((1,), (1,)), ((), ()))), 0.0)
    size = SUBLANES
    while size < block:
        sh = size.bit_length() - 1
        lower_left = (((ri >> (sh + 1)) == (ci >> (sh + 1))) & (((ri >> sh) & 1) == 1) & (((ci >> sh) & 1) == 0))
        t = t - _dot_split(t, _dot_split(jnp.where(lower_left, a_strict, 0.0), t))
        size *= 2
    return t


GDN_HEAD_GROUP = 4
_CHUNK_SHIFT = GDN_CHUNK.bit_length() - 1
_LANE_SHIFT = LANES.bit_length() - 1


def _stack_heads(ref, hb):
    return jnp.concatenate([ref[:, i * LANES:(i + 1) * LANES] for i in range(hb)], axis=0)


def _diag_blocks(x, hb):
    c = GDN_CHUNK
    return jnp.concatenate([x[i * c:(i + 1) * c, i * LANES:(i + 1) * LANES] for i in range(hb)], axis=0)


def _expand_blocks(y, hb):
    row_blk = lax.shift_right_logical(lax.broadcasted_iota(jnp.int32, y.shape, 0), _CHUNK_SHIFT)
    return jnp.concatenate([jnp.where(row_blk == j, y, 0.0) for j in range(hb)], axis=1)


def _gdn_group_terms(q, k, v, ab, alog, dtb, head0, hb, heads):
    c = GDN_CHUNK
    r = hb * c
    lane = lax.broadcasted_iota(jnp.int32, (1, LANES), 1)

    def column(lane0):
        return jnp.concatenate([jnp.sum(jnp.where(lane == lane0 + head0 + i, ab, 0.0), axis=1, keepdims=True)
                                for i in range(hb)], axis=0)

    def per_head(vec):
        return jnp.concatenate([jnp.broadcast_to(jnp.sum(jnp.where(lane == head0 + i, vec, 0.0), axis=1,
                                                         keepdims=True), (c, 1)) for i in range(hb)], axis=0)

    pre = column(0) + per_head(dtb)
    neg_ea = -jnp.exp(per_head(alog))
    g = neg_ea * _softplus(pre)
    beta = _sigmoid(column(heads))
    ri = lax.broadcasted_iota(jnp.int32, (r, r), 0)
    ci = lax.broadcasted_iota(jnp.int32, (r, r), 1)
    same = lax.shift_right_logical(ri, _CHUNK_SHIFT) == lax.shift_right_logical(ci, _CHUNK_SHIFT)
    eye = ri == ci
    causal = same & (ri >= ci)
    strict = same & (ri > ci)
    g_row = jnp.sum(jnp.where(eye, g, 0.0), axis=0, keepdims=True)
    gc_col = jnp.sum(jnp.where(causal, g_row, 0.0), axis=1, keepdims=True)
    gc_row = jnp.sum(jnp.where(same & (ri <= ci), g, 0.0), axis=0, keepdims=True)
    gl_col = jnp.sum(jnp.where(same, g_row, 0.0), axis=1, keepdims=True)
    decay = jnp.where(causal, jnp.exp(jnp.where(causal, gc_col - gc_row, 0.0)), 0.0)
    e_last_col = jnp.exp(gl_col)
    e_last_lanes = jnp.concatenate([jnp.broadcast_to(e_last_col[i * c:i * c + 1, :], (1, LANES))
                                    for i in range(hb)], axis=1)
    egc = jnp.exp(gc_col)
    ekl = jnp.exp(gl_col - gc_col)
    kb = k * beta
    vb = v * beta
    kk = _dot_nt(kb, k)
    a_strict = jnp.where(strict, kk * decay, 0.0)
    return dict(pre=pre, neg_ea=neg_ea, g=g, beta=beta, ri=ri, ci=ci, same=same, eye=eye, causal=causal,
                strict=strict, decay=decay, e_last_col=e_last_col, e_last_lanes=e_last_lanes, egc=egc, ekl=ekl,
                kb=kb, vb=vb, kk=kk, a_strict=a_strict, lane=lane)


def _gdn_head_group(heads):
    hb = GDN_HEAD_GROUP
    while heads % hb:
        hb //= 2
    return hb


def _gdn_fwd(qkv, proj, ab_blk, alog, dtb, heads, name):
    s = qkv.shape[0]
    c = GDN_CHUNK
    nc = s // c
    hb = _gdn_head_group(heads)
    ng = heads // hb
    r = hb * c

    def body(q_ref, k_ref, v_ref, ab_ref, alog_ref, dtb_ref, o_ref, t_ref, s0_ref, state_ref):
        grp, ch = pl.program_id(0), pl.program_id(1)

        @pl.when(ch == 0)
        def _():
            state_ref[...] = jnp.zeros_like(state_ref)

        q, k, v = _stack_heads(q_ref, hb), _stack_heads(k_ref, hb), _stack_heads(v_ref, hb)
        tm = _gdn_group_terms(q, k, v, ab_ref[...], alog_ref[...], dtb_ref[...], grp * hb, hb, heads)
        t_inv = _tri_inverse(tm['a_strict'], c)
        u = _dot(t_inv, tm['vb'])
        w = _dot(t_inv, tm['kb'] * tm['egc'])
        qk = jnp.where(tm['causal'], _dot_nt(q, k) * tm['decay'], 0.0)
        st = state_ref[...]
        v_new = u - _diag_blocks(_dot(w, st), hb)
        out = _diag_blocks(_dot(q * tm['egc'], st), hb) + _dot(qk, v_new)
        for i in range(hb):
            o_ref[:, i * LANES:(i + 1) * LANES] = out[i * c:(i + 1) * c, :]
        t_ref[...] = t_inv
        s0_ref[...] = st
        state_ref[...] = st * tm['e_last_lanes'] + _dot_tn(k * tm['ekl'], _expand_blocks(v_new, hb))

    def blk(off):
        return pl.BlockSpec((c, hb * LANES), lambda g, n: (n, off * ng + g))

    vec = pl.BlockSpec((1, LANES), lambda g, n: (0, 0))
    return _pcall(
        body, name=name, grid=(ng, nc),
        in_specs=[blk(0), blk(1), blk(2), pl.BlockSpec((c, LANES), lambda g, n: (n, ab_blk)), vec, vec],
        out_specs=(blk(0), pl.BlockSpec((None, None, r, r), lambda g, n: (g, n, 0, 0)),
                   pl.BlockSpec((None, None, LANES, hb * LANES), lambda g, n: (g, n, 0, 0))),
        out_shape=(jax.ShapeDtypeStruct((s, heads * LANES), F32), jax.ShapeDtypeStruct((ng, nc, r, r), F32),
                   jax.ShapeDtypeStruct((ng, nc, LANES, hb * LANES), F32)),
        scratch_shapes=[pltpu.VMEM((LANES, hb * LANES), F32)], semantics=("parallel", "arbitrary"),
        block_bytes=8 * r * LANES * 4 + 2 * r * r * 4 + 2 * LANES * hb * LANES * 4)(qkv, qkv, qkv, proj, alog, dtb)


def _gdn_bwd(qkv, proj, ab_blk, alog, dtb, t_all, s0_all, d_o, heads, name):
    s = qkv.shape[0]
    c = GDN_CHUNK
    nc = s // c
    hb = _gdn_head_group(heads)
    ng = heads // hb
    r = hb * c

    def body(q_ref, k_ref, v_ref, ab_ref, alog_ref, dtb_ref, t_ref, s0_ref, do_ref,
             dq_ref, dk_ref, dv_ref, dgb_ref, ds_ref):
        grp, step = pl.program_id(0), pl.program_id(1)

        @pl.when(step == 0)
        def _():
            ds_ref[...] = jnp.zeros_like(ds_ref)

        q, k, v = _stack_heads(q_ref, hb), _stack_heads(k_ref, hb), _stack_heads(v_ref, hb)
        do = _stack_heads(do_ref, hb)
        tm = _gdn_group_terms(q, k, v, ab_ref[...], alog_ref[...], dtb_ref[...], grp * hb, hb, heads)
        ri, ci, same, eye = tm['ri'], tm['ci'], tm['same'], tm['eye']
        causal, strict, decay = tm['causal'], tm['strict'], tm['decay']
        egc, ekl, kb, vb, beta = tm['egc'], tm['ekl'], tm['kb'], tm['vb'], tm['beta']
        t_inv = t_ref[...]
        st = s0_ref[...]
        ds_next = ds_ref[...]
        kbg = kb * egc
        u = _dot(t_inv, vb)
        w = _dot(t_inv, kbg)
        qkm = _dot_nt(q, k)
        qk = jnp.where(causal, qkm * decay, 0.0)
        v_new = u - _diag_blocks(_dot(w, st), hb)
        qd = q * egc
        kd = k * ekl
        do_x = _expand_blocks(do, hb)

        dqd = _dot_nt(do_x, st)
        dqk = jnp.where(causal, _dot_nt(do, v_new), 0.0)
        dvn = _dot_tn(qk, do) + _diag_blocks(_dot(kd, ds_next), hb)
        dkd = _dot_nt(_expand_blocks(v_new, hb), ds_next)
        sd = jnp.sum(st * ds_next, axis=0, keepdims=True)
        dgl = jnp.concatenate([jnp.broadcast_to(jnp.sum(sd[:, i * LANES:(i + 1) * LANES], axis=1, keepdims=True),
                                                (c, 1)) for i in range(hb)], axis=0) * tm['e_last_col']
        dvn_x = _expand_blocks(dvn, hb)
        dw = -_dot_nt(dvn_x, st)
        ds_ref[...] = _dot_tn(qd, do_x) + tm['e_last_lanes'] * ds_next - _dot_tn(w, dvn_x)
        dt = _dot_nt(dvn, vb) + _dot_nt(dw, kbg)
        dvb = _dot_tn(t_inv, dvn)
        dkbg = _dot_tn(t_inv, dw)
        da_m = jnp.where(strict, -_dot_tn(t_inv, _dot_nt(dt, t_inv)), 0.0)
        dad = da_m * decay
        dkb = _dot(dad, k) + dkbg * egc
        dqkd = dqk * decay
        dq = _dot(dqkd, k) + dqd * egc
        dk = _dot_tn(dad, kb) + _dot_tn(dqkd, q) + dkd * ekl + dkb * beta
        e_mat = (da_m * tm['kk'] + dqk * qkm) * decay
        s_kd = jnp.sum(dkd * kd, axis=1, keepdims=True)
        s_kd_row = jnp.sum(jnp.where(eye, s_kd, 0.0), axis=0, keepdims=True)
        dgl = dgl + jnp.sum(jnp.where(same, s_kd_row, 0.0), axis=1, keepdims=True)
        col_sum = jnp.sum(e_mat, axis=0, keepdims=True)
        col_sum_c = jnp.sum(jnp.where(eye, col_sum, 0.0), axis=1, keepdims=True)
        dgc = (jnp.sum(e_mat, axis=1, keepdims=True) - col_sum_c + jnp.sum(dqd * qd, axis=1, keepdims=True)
               - s_kd + jnp.sum(dkbg * kbg, axis=1, keepdims=True))
        row_c = lax.broadcasted_iota(jnp.int32, (r, 1), 0)
        dgc = dgc + jnp.where((row_c & (c - 1)) == c - 1, dgl, 0.0)
        dgc_row = jnp.sum(jnp.where(eye, dgc, 0.0), axis=0, keepdims=True)
        dg = jnp.sum(jnp.where(same & (ci >= ri), dgc_row, 0.0), axis=1, keepdims=True)
        dbeta = jnp.sum(dkb * k, axis=1, keepdims=True) + jnp.sum(dvb * v, axis=1, keepdims=True)
        da_pre = dg * tm['neg_ea'] * _sigmoid(tm['pre'])
        db_pre = dbeta * beta * (1.0 - beta)
        lane = tm['lane']
        head_row = grp * hb + lax.shift_right_logical(row_c, _CHUNK_SHIFT)
        dgb = (jnp.where(lane == head_row, da_pre, 0.0) + jnp.where(lane == heads + head_row, db_pre, 0.0)
               + jnp.where(lane == 2 * heads + head_row, dg * tm['g'], 0.0))
        dvv = dvb * beta
        for i in range(hb):
            cols, rows = slice(i * LANES, (i + 1) * LANES), slice(i * c, (i + 1) * c)
            dq_ref[:, cols] = dq[rows, :]
            dk_ref[:, cols] = dk[rows, :]
            dv_ref[:, cols] = dvv[rows, :]
            dgb_ref[:, cols] = dgb[rows, :]

    def blk(off):
        return pl.BlockSpec((c, hb * LANES), lambda g, n: (nc - 1 - n, off * ng + g))

    vec = pl.BlockSpec((1, LANES), lambda g, n: (0, 0))
    gw = heads * LANES
    dq, dk, dv, dgb = _pcall(
        body, name=name, grid=(ng, nc),
        in_specs=[blk(0), blk(1), blk(2), pl.BlockSpec((c, LANES), lambda g, n: (nc - 1 - n, ab_blk)),
                  vec, vec, pl.BlockSpec((None, None, r, r), lambda g, n: (g, nc - 1 - n, 0, 0)),
                  pl.BlockSpec((None, None, LANES, hb * LANES), lambda g, n: (g, nc - 1 - n, 0, 0)), blk(0)],
        out_specs=(blk(0), blk(0), blk(0), blk(0)),
        out_shape=tuple(jax.ShapeDtypeStruct((s, gw), F32) for _ in range(4)),
        scratch_shapes=[pltpu.VMEM((LANES, hb * LANES), F32)], semantics=("parallel", "arbitrary"),
        block_bytes=12 * r * LANES * 4 + 2 * r * r * 4 + 2 * LANES * hb * LANES * 4)(
            qkv, qkv, qkv, proj, alog, dtb, t_all, s0_all, d_o)
    return dq, dk, dv, dgb


def _gdn_post_fwd(o, proj, z_col0, gain, name):
    s, gw = o.shape
    heads = gw // LANES
    tr = _tile(s, CONV_ROW_TILE, SUBLANES)

    def body(o_ref, z_ref, g_ref, y_ref):
        ov, zv = o_ref[...], z_ref[...]
        r = lax.rsqrt(jnp.mean(ov * ov, axis=-1, keepdims=True) + RMS_EPS)
        y_ref[...] = (ov * r * g_ref[...] * (zv * _sigmoid(zv))).astype(BF16)

    blk = pl.BlockSpec((tr, LANES), lambda i, h: (i, h))
    return _pcall(body, name=name, grid=(s // tr, heads),
                  in_specs=[blk, pl.BlockSpec((tr, LANES), lambda i, h: (i, z_col0 + h)),
                            pl.BlockSpec((1, LANES), lambda i, h: (0, 0))],
                  out_specs=blk, out_shape=jax.ShapeDtypeStruct((s, gw), BF16), semantics=("parallel", "parallel"),
                  block_bytes=3 * tr * LANES * 4)(o, proj, gain.reshape(1, LANES))


def _gdn_post_bwd(o, proj, z_col0, gain, dy, name):
    s, gw = o.shape
    heads = gw // LANES
    tr = _tile(s, CONV_ROW_TILE, SUBLANES)

    def body(o_ref, z_ref, g_ref, dy_ref, do_ref, dz_ref, dg_ref):
        ov, zv, gv, dyv = o_ref[...], z_ref[...], g_ref[...], dy_ref[...]
        r = lax.rsqrt(jnp.mean(ov * ov, axis=-1, keepdims=True) + RMS_EPS)
        nv = ov * r
        sg = _sigmoid(zv)
        sz = zv * sg
        dn = dyv * gv * sz
        do_ref[...] = r * (dn - nv * jnp.mean(dn * nv, axis=-1, keepdims=True))
        dz_ref[...] = dyv * nv * gv * (sg * (1.0 + zv * (1.0 - sg)))

        @pl.when((pl.program_id(0) == 0) & (pl.program_id(1) == 0))
        def _():
            dg_ref[...] = jnp.zeros_like(dg_ref)

        dg_ref[...] += jnp.sum(dyv * nv * sz, axis=0, keepdims=True)

    blk = pl.BlockSpec((tr, LANES), lambda i, h: (i, h))
    vec = pl.BlockSpec((1, LANES), lambda i, h: (0, 0))
    return _pcall(body, name=name, grid=(s // tr, heads),
                  in_specs=[blk, pl.BlockSpec((tr, LANES), lambda i, h: (i, z_col0 + h)), vec, blk],
                  out_specs=(blk, blk, vec),
                  out_shape=(jax.ShapeDtypeStruct((s, gw), F32), jax.ShapeDtypeStruct((s, gw), F32),
                             jax.ShapeDtypeStruct((1, LANES), F32)),
                  semantics=("arbitrary", "arbitrary"), block_bytes=6 * tr * LANES * 4)(
                      o, proj, gain.reshape(1, LANES), dy)


def _dab_reduce(dgb, name):
    s, gw = dgb.shape
    heads = gw // LANES
    tr = _tile(s, 512, SUBLANES)

    def body(d_ref, o_ref, cs_ref):
        acc = d_ref[:, 0:LANES]
        for h in range(1, heads):
            acc = acc + d_ref[:, h * LANES:(h + 1) * LANES]
        o_ref[...] = acc

        @pl.when(pl.program_id(0) == 0)
        def _():
            cs_ref[...] = jnp.zeros_like(cs_ref)

        cs_ref[...] += jnp.sum(acc, axis=0, keepdims=True)

    return _pcall(body, name=name, grid=(s // tr,), in_specs=[pl.BlockSpec((tr, gw), lambda i: (i, 0))],
                  out_specs=(pl.BlockSpec((tr, LANES), lambda i: (i, 0)), pl.BlockSpec((1, LANES), lambda i: (0, 0))),
                  out_shape=(jax.ShapeDtypeStruct((s, LANES), F32), jax.ShapeDtypeStruct((1, LANES), F32)),
                  semantics=("arbitrary",), block_bytes=tr * gw * 4)(dgb)


def _lru_gates(xc, wa, wx, ba, bx, lam):
    r = _sigmoid(_dot(xc, wa) + ba)
    ig = _sigmoid(_dot(xc, wx) + bx)
    sp = _softplus(-lam)
    log_a = -LRU_C * r * sp
    a = jnp.exp(log_a)
    e2 = jnp.exp(2.0 * log_a)
    mult = jnp.sqrt(jnp.maximum(1.0 - e2, 0.0))
    return r, ig, sp, a, e2, mult


def _lru_fwd(xc, proj, y_col0, wa, wx, ba, bx, lam, name):
    s, lw = xc.shape
    nb = lw // LANES
    tr = _tile(s, 256, SUBLANES)

    def body(xc_ref, y_ref, wa_ref, wx_ref, ba_ref, bx_ref, lam_ref, h_ref, o_ref, carry_ref):
        @pl.when(pl.program_id(1) == 0)
        def _():
            carry_ref[...] = jnp.zeros_like(carry_ref)

        xv = xc_ref[...]
        _, ig, _, a, _, mult = _lru_gates(xv, wa_ref[...], wx_ref[...], ba_ref[...], bx_ref[...], lam_ref[...])
        b = mult * (ig * xv)
        row = lax.broadcasted_iota(jnp.int32, (tr, LANES), 0)
        sh = 1
        while sh < tr:
            keep = row >= sh
            b = a * jnp.where(keep, pltpu.roll(b, sh, 0), 0.0) + b
            a = a * jnp.where(keep, pltpu.roll(a, sh, 0), 1.0)
            sh *= 2
        hv = a * carry_ref[0:1, :] + b
        h_ref[...] = hv
        carry_ref[...] = jnp.broadcast_to(hv[tr - 1:tr, :], carry_ref.shape)
        gy, _ = _gelu_and_grad(y_ref[...])
        o_ref[...] = (hv * gy).astype(BF16)

    blk = pl.BlockSpec((tr, LANES), lambda n, i: (i, n))
    wspec = pl.BlockSpec((None, LANES, LANES), lambda n, i: (n, 0, 0))
    vec = pl.BlockSpec((1, LANES), lambda n, i: (0, n))
    return _pcall(body, name=name, grid=(nb, s // tr),
                  in_specs=[blk, pl.BlockSpec((tr, LANES), lambda n, i: (i, y_col0 + n)), wspec, wspec, vec, vec, vec],
                  out_specs=(blk, blk),
                  out_shape=(jax.ShapeDtypeStruct((s, lw), F32), jax.ShapeDtypeStruct((s, lw), BF16)),
                  scratch_shapes=[pltpu.VMEM((SUBLANES, LANES), F32)], semantics=("parallel", "arbitrary"),
                  block_bytes=8 * tr * LANES * 4)(xc, proj, wa, wx, ba.reshape(1, lw), bx.reshape(1, lw),
                                                  lam.reshape(1, lw))


def _lru_bwd(d_out, xc, hseq, proj, y_col0, wa, wx, ba, bx, lam, name):
    s, lw = xc.shape
    nb = lw // LANES
    tr = _tile(s, 256, SUBLANES)
    per = tr // SUBLANES
    ni = s // tr
    nrow8 = s // SUBLANES

    def body(do_ref, xc_ref, xn_ref, h_ref, hp_ref, y_ref, wa_ref, wx_ref, ba_ref, bx_ref, lam_ref,
             dxc_ref, dy_ref, dwa_ref, dwx_ref, dba_ref, dbx_ref, dlam_ref, carry_ref):
        step = pl.program_id(1)
        tile = ni - 1 - step

        @pl.when(step == 0)
        def _():
            carry_ref[...] = jnp.zeros_like(carry_ref)
            dwa_ref[...] = jnp.zeros_like(dwa_ref)
            dwx_ref[...] = jnp.zeros_like(dwx_ref)
            dba_ref[...] = jnp.zeros_like(dba_ref)
            dbx_ref[...] = jnp.zeros_like(dbx_ref)
            dlam_ref[...] = jnp.zeros_like(dlam_ref)

        wav, wxv, bav, bxv, lamv = wa_ref[...], wx_ref[...], ba_ref[...], bx_ref[...], lam_ref[...]
        xv = xc_ref[...]
        r, ig, sp, a, e2, mult = _lru_gates(xv, wav, wxv, bav, bxv, lamv)
        a_next = _lru_gates(xn_ref[...], wav, wxv, bav, bxv, lamv)[3] * (tile < ni - 1).astype(F32)
        hv = h_ref[...]
        h_prev = _shift_down(hv, hp_ref[...] * (tile > 0).astype(F32), 1)
        yv = y_ref[...]
        gy, dgy = _gelu_and_grad(yv)
        dov = do_ref[...]
        dy_ref[...] = dov * hv * dgy
        coef = _shift_up(a, a_next, 1)
        bb = dov * gy
        row = lax.broadcasted_iota(jnp.int32, (tr, LANES), 0)
        sh = 1
        while sh < tr:
            keep = row < tr - sh
            bb = coef * jnp.where(keep, pltpu.roll(bb, tr - sh, 0), 0.0) + bb
            coef = coef * jnp.where(keep, pltpu.roll(coef, tr - sh, 0), 1.0)
            sh *= 2
        lam_t = coef * carry_ref[0:1, :] + bb
        carry_ref[...] = jnp.broadcast_to(lam_t[0:1, :], carry_ref.shape)
        d_a = lam_t * h_prev
        d_mult = lam_t * (ig * xv)
        d_ix = lam_t * mult
        d_la = d_a * a - d_mult * e2 / jnp.maximum(mult, 1e-30)
        d_r = d_la * (-LRU_C * sp)
        dlam_ref[...] += jnp.sum(d_la * (LRU_C * r) * _sigmoid(-lamv), axis=0, keepdims=True)
        d_pa = d_r * r * (1.0 - r)
        d_px = (d_ix * xv) * ig * (1.0 - ig)
        dxc_ref[...] = d_ix * ig + _dot_nt(d_pa, wav) + _dot_nt(d_px, wxv)
        dwa_ref[...] += _dot_tn(xv, d_pa)
        dwx_ref[...] += _dot_tn(xv, d_px)
        dba_ref[...] += jnp.sum(d_pa, axis=0, keepdims=True)
        dbx_ref[...] += jnp.sum(d_px, axis=0, keepdims=True)

    blk = pl.BlockSpec((tr, LANES), lambda n, i: (ni - 1 - i, n))
    nxt = pl.BlockSpec((SUBLANES, LANES), lambda n, i: (jnp.minimum((ni - i) * per, nrow8 - 1), n))
    prv = pl.BlockSpec((SUBLANES, LANES), lambda n, i: (jnp.maximum((ni - 1 - i) * per - 1, 0), n))
    wspec = pl.BlockSpec((None, LANES, LANES), lambda n, i: (n, 0, 0))
    vec = pl.BlockSpec((1, LANES), lambda n, i: (0, n))
    return _pcall(
        body, name=name, grid=(nb, ni),
        in_specs=[blk, blk, nxt, blk, prv, pl.BlockSpec((tr, LANES), lambda n, i: (ni - 1 - i, y_col0 + n)),
                  wspec, wspec, vec, vec, vec],
        out_specs=(blk, blk, wspec, wspec, vec, vec, vec),
        out_shape=(jax.ShapeDtypeStruct((s, lw), F32), jax.ShapeDtypeStruct((s, lw), F32),
                   jax.ShapeDtypeStruct((nb, LANES, LANES), F32), jax.ShapeDtypeStruct((nb, LANES, LANES), F32),
                   jax.ShapeDtypeStruct((1, lw), F32), jax.ShapeDtypeStruct((1, lw), F32),
                   jax.ShapeDtypeStruct((1, lw), F32)),
        scratch_shapes=[pltpu.VMEM((SUBLANES, LANES), F32)], semantics=("parallel", "arbitrary"),
        block_bytes=12 * tr * LANES * 4)(d_out, xc, xc, hseq, hseq, proj, wa, wx, ba.reshape(1, lw),
                                         bx.reshape(1, lw), lam.reshape(1, lw))


def _merge_fwd(proj, gg_col0, gl_col0, bg, bl, name):
    s, d = bg.shape
    tr, tc = _tile(s, 256, SUBLANES), _tile(d, 1024)
    cb = tc // LANES

    def body(gg_ref, gl_ref, bg_ref, bl_ref, o_ref):
        o_ref[...] = (_sigmoid(gg_ref[...]) * bg_ref[...] + _sigmoid(gl_ref[...]) * bl_ref[...]).astype(BF16)

    blk = pl.BlockSpec((tr, tc), lambda i, j: (i, j))
    return _pcall(body, name=name, grid=(s // tr, d // tc),
                  in_specs=[pl.BlockSpec((tr, tc), lambda i, j: (i, gg_col0 // cb + j)),
                            pl.BlockSpec((tr, tc), lambda i, j: (i, gl_col0 // cb + j)), blk, blk],
                  out_specs=blk, out_shape=jax.ShapeDtypeStruct((s, d), BF16), semantics=("parallel", "parallel"),
                  block_bytes=5 * tr * tc * 4)(proj, proj, bg, bl)


def _merge_bwd(proj, gg_col0, gl_col0, bg, bl, dm, name):
    s, d = bg.shape
    tr, tc = _tile(s, 256, SUBLANES), _tile(d, 1024)
    cb = tc // LANES

    def body(gg_ref, gl_ref, bg_ref, bl_ref, dm_ref, dgg_ref, dgl_ref, dbg_ref, dbl_ref):
        dmv = dm_ref[...]
        sg, sl = _sigmoid(gg_ref[...]), _sigmoid(gl_ref[...])
        dgg_ref[...] = (dmv * bg_ref[...] * sg * (1.0 - sg)).astype(BF16)
        dgl_ref[...] = (dmv * bl_ref[...] * sl * (1.0 - sl)).astype(BF16)
        dbg_ref[...] = (dmv * sg).astype(BF16)
        dbl_ref[...] = (dmv * sl).astype(BF16)

    blk = pl.BlockSpec((tr, tc), lambda i, j: (i, j))
    sh = jax.ShapeDtypeStruct((s, d), BF16)
    return _pcall(body, name=name, grid=(s // tr, d // tc),
                  in_specs=[pl.BlockSpec((tr, tc), lambda i, j: (i, gg_col0 // cb + j)),
                            pl.BlockSpec((tr, tc), lambda i, j: (i, gl_col0 // cb + j)), blk, blk, blk],
                  out_specs=(blk, blk, blk, blk), out_shape=(sh, sh, sh, sh), semantics=("parallel", "parallel"),
                  block_bytes=8 * tr * tc * 4)(proj, proj, bg, bl, dm)


def _sum_slots(slots, name):
    n, r, c = slots.shape
    tr = _tile(r, max(2 * SUBLANES, (1 << 19) // (c * 4)), 2 * SUBLANES)

    def body(s_ref, o_ref):
        acc = s_ref[0].astype(F32)
        for q in range(1, n):
            acc = acc + s_ref[q].astype(F32)
        o_ref[...] = acc

    return _pcall(body, name=name, grid=(r // tr,), in_specs=[pl.BlockSpec((n, tr, c), lambda i: (0, i, 0))],
                  out_specs=pl.BlockSpec((tr, c), lambda i: (i, 0)), out_shape=jax.ShapeDtypeStruct((r, c), F32),
                  semantics=("parallel",), block_bytes=(n + 1) * tr * c * 4)(slots)


def _adamw(w, g_parts, m, v, name):
    r, c = w.shape
    np_ = len(g_parts)
    tr = _tile(r, max(SUBLANES, (1 << 20) // (c * 4)), SUBLANES)
    c1 = 1.0 - ADAM_B1 ** ADAM_STEP
    c2 = 1.0 - ADAM_B2 ** ADAM_STEP

    def body(*refs):
        w_ref, m_ref, v_ref = refs[0], refs[1 + np_], refs[2 + np_]
        g_ref, d_ref, nm_ref, nv_ref = refs[3 + np_:]
        g = refs[1][...]
        for p in range(1, np_):
            g = g + refs[1 + p][...]
        nm = ADAM_B1 * m_ref[...] + (1.0 - ADAM_B1) * g
        nv = ADAM_B2 * v_ref[...] + (1.0 - ADAM_B2) * (g * g)
        g_ref[...] = g
        nm_ref[...] = nm
        nv_ref[...] = nv
        d_ref[...] = -ADAM_LR * ((nm / c1) / (jnp.sqrt(nv / c2) + ADAM_EPS) + ADAM_WD * w_ref[...])

    blk = pl.BlockSpec((tr, c), lambda i: (i, 0))
    sh = jax.ShapeDtypeStruct((r, c), F32)
    return _pcall(body, name=name, grid=(r // tr,), in_specs=[blk] * (3 + np_), out_specs=(blk,) * 4,
                  out_shape=(sh,) * 4, semantics=("parallel",), block_bytes=(7 + np_) * tr * c * 4)(
                      w, *g_parts, m, v)


def _pair_sum(core, mine, theirs, name):
    _, n, r, c = mine.shape
    tr = _tile(r, max(2 * SUBLANES, (1 << 19) // (c * 4)), 2 * SUBLANES)

    def body(core_ref, a_ref, b_ref, o_ref):
        o_ref[...] = (a_ref[...].astype(F32) + b_ref[...].astype(F32)).astype(BF16)

    return _pcall(body, name=name, grid=(n, r // tr),
                  in_specs=[pl.BlockSpec((None, None, tr, c), lambda q, i, core_ref: (core_ref[0], q, i, 0)),
                            pl.BlockSpec((None, tr, c), lambda q, i, core_ref: (q, i, 0))],
                  out_specs=pl.BlockSpec((None, tr, c), lambda q, i, core_ref: (q, i, 0)),
                  out_shape=jax.ShapeDtypeStruct((n, r, c), BF16), semantics=("parallel", "parallel"),
                  block_bytes=3 * tr * c * 4, scalar_prefetch=1)(core, mine, theirs)


def _sum_landed(chip, landed, own, name):
    n, r, c = landed.shape
    tr = _tile(r, max(2 * SUBLANES, (1 << 19) // (c * 4)), 2 * SUBLANES)

    def body(chip_ref, l_ref, o_ref, t_ref):
        acc = o_ref[...].astype(F32)
        for q in range(n):
            acc = acc + l_ref[q].astype(F32)
        t_ref[...] = acc

    return _pcall(body, name=name, grid=(r // tr,),
                  in_specs=[pl.BlockSpec((n, tr, c), lambda i, chip_ref: (0, i, 0)),
                            pl.BlockSpec((None, tr, c), lambda i, chip_ref: (chip_ref[0], i, 0))],
                  out_specs=pl.BlockSpec((tr, c), lambda i, chip_ref: (i, 0)),
                  out_shape=jax.ShapeDtypeStruct((r, c), F32), semantics=("parallel",),
                  block_bytes=(n + 3) * tr * c * 4, scalar_prefetch=1)(chip, landed, own)


def _adamw_quarters(core, w, g_mine, g_other, m, v, after, name):
    nl, nh, r, c = w.shape
    tr = _tile(r, max(SUBLANES, (1 << 19) // (c * 4)), SUBLANES)
    c1 = 1.0 - ADAM_B1 ** ADAM_STEP
    c2 = 1.0 - ADAM_B2 ** ADAM_STEP

    def body(core_ref, w_ref, *refs):
        g_refs, (m_ref, v_ref, _, g_ref, d_ref, nm_ref, nv_ref) = refs[:2 * nl], refs[2 * nl:]
        mine = pl.program_id(1) == core_ref[0]
        g = jnp.where(mine, g_refs[0][...], g_refs[nl][...])
        for l in range(1, nl):
            g = jnp.where(pl.program_id(0) == l, jnp.where(mine, g_refs[l][...], g_refs[nl + l][...]), g)
        nm = ADAM_B1 * m_ref[...] + (1.0 - ADAM_B1) * g
        nv = ADAM_B2 * v_ref[...] + (1.0 - ADAM_B2) * (g * g)
        g_ref[...] = g
        nm_ref[...] = nm
        nv_ref[...] = nv
        d_ref[...] = -ADAM_LR * ((nm / c1) / (jnp.sqrt(nv / c2) + ADAM_EPS) + ADAM_WD * w_ref[...])

    blk = pl.BlockSpec((None, None, tr, c), lambda l, hf, i, core_ref: (l, hf, i, 0))
    gblk = pl.BlockSpec((tr, c), lambda l, hf, i, core_ref: (i, 0))
    sh = jax.ShapeDtypeStruct(w.shape, F32)
    return _pcall(body, name=name, grid=(nl, nh, r // tr),
                  in_specs=[blk] + [gblk] * (2 * nl) + [blk, blk, pl.BlockSpec(memory_space=pl.ANY)],
                  out_specs=(blk,) * 4, out_shape=(sh,) * 4, semantics=("parallel", "parallel", "parallel"),
                  block_bytes=(7 + 2 * nl) * tr * c * 4, scalar_prefetch=1)(core, w, *g_mine, *g_other, m, v, after)


HBM_SPEC = pl.BlockSpec(memory_space=pltpu.HBM)


def _other_chips(x, y):
    return [(1 - x, y), (x, 1 - y), (1 - x, 1 - y)]


SEM_SPEC = pl.BlockSpec(memory_space=pltpu.SEMAPHORE)
DATAFLOW_EFFECT = pltpu.SideEffectType.DATAFLOW_SIDE_EFFECTING


def _split_start(name, bufs, n_copies, build):
    nb = len(bufs)

    def body(*refs):
        starts, _ = build(refs[:nb], refs[nb], refs[nb + 1])
        for cp in starts:
            cp.start()
        refs[-1][...] = jnp.zeros_like(refs[-1])

    out = pl.pallas_call(
        body, name=name,
        out_shape=(pltpu.SemaphoreType.DMA((n_copies,)), pltpu.SemaphoreType.DMA((n_copies,)),
                   *[pltpu.HBM(b.shape, b.dtype) for b in bufs], jax.ShapeDtypeStruct((SUBLANES, LANES), F32)),
        in_specs=[HBM_SPEC] * nb,
        out_specs=(SEM_SPEC, SEM_SPEC, *[HBM_SPEC] * nb, pl.BlockSpec(memory_space=pltpu.VMEM)),
        input_output_aliases={i: 2 + i for i in range(nb)},
        compiler_params=pltpu.CompilerParams(has_side_effects=DATAFLOW_EFFECT),
    )(*[pltpu.with_memory_space_constraint(b, pltpu.HBM) for b in bufs])
    return out[0], out[1], list(out[2:2 + nb]), out[2 + nb]


def _split_wait(name, send_sems, recv_sems, bufs, after, build):
    nb = len(bufs)

    def body(*refs):
        starts, waits = build(refs[:nb], refs[nb], refs[nb + 1])
        for cp in starts:
            cp.wait_send()
        for cp in waits:
            cp.wait_recv()

    out = pl.pallas_call(
        body, name=name, out_shape=tuple(pltpu.HBM(b.shape, b.dtype) for b in bufs),
        in_specs=[HBM_SPEC] * nb + [SEM_SPEC, SEM_SPEC, pl.BlockSpec(memory_space=pl.ANY)],
        out_specs=tuple([HBM_SPEC] * nb), input_output_aliases={i: i for i in range(nb)},
        compiler_params=pltpu.CompilerParams(has_side_effects=DATAFLOW_EFFECT),
    )(*bufs, send_sems, recv_sems, after)
    return list(out)


def _gather_ici_copies(nt, refs, send_sems, recv_sems):
    srcs, lands = refs[:nt], refs[nt:]
    x, y, c = lax.axis_index("x"), lax.axis_index("y"), lax.axis_index("c")
    me = 2 * x + y
    starts, waits = [], []
    for t in range(nt):
        for j, (px, py) in enumerate(_other_chips(x, y)):
            def copy(slot, t=t, j=j, px=px, py=py):
                return pltpu.make_async_remote_copy(
                    src_ref=srcs[t].at[c], dst_ref=lands[t].at[slot].at[c], send_sem=send_sems.at[3 * t + j],
                    recv_sem=recv_sems.at[3 * t + j], device_id=(px, py, c), device_id_type=pl.DeviceIdType.MESH)
            starts.append(copy(me))
            waits.append(copy(2 * px + py))
    return starts, waits


def _gather_d2d_copies(nt, refs, send_sems, recv_sems):
    x, y, c = lax.axis_index("x"), lax.axis_index("y"), lax.axis_index("c")
    starts, waits = [], []
    for t in range(nt):
        for j, (px, py) in enumerate(_other_chips(x, y)):
            def copy(half, t=t, j=j, px=px, py=py):
                place = refs[t].at[2 * px + py].at[half]
                return pltpu.make_async_remote_copy(
                    src_ref=place, dst_ref=place, send_sem=send_sems.at[3 * t + j], recv_sem=recv_sems.at[3 * t + j],
                    device_id=(x, y, 1 - c), device_id_type=pl.DeviceIdType.MESH)
            starts.append(copy(c))
            waits.append(copy(1 - c))
    return starts, waits


def _scatter_ici_copies(nt, refs, send_sems, recv_sems):
    srcs, lands = refs[:nt], refs[nt:]
    x, y, c = lax.axis_index("x"), lax.axis_index("y"), lax.axis_index("c")
    me = 2 * x + y
    starts, waits = [], []
    for t in range(nt):
        for j, (px, py) in enumerate(_other_chips(x, y)):
            def copy(slot, t=t, j=j, px=px, py=py):
                return pltpu.make_async_remote_copy(
                    src_ref=srcs[t].at[2 * px + py], dst_ref=lands[t].at[slot], send_sem=send_sems.at[3 * t + j],
                    recv_sem=recv_sems.at[3 * t + j], device_id=(px, py, c), device_id_type=pl.DeviceIdType.MESH)
            starts.append(copy(me))
            waits.append(copy(2 * px + py))
    return starts, waits


def _sibling_half_copies(nt, refs, send_sems, recv_sems):
    srcs, lands = refs[:nt], refs[nt:]
    x, y, c = lax.axis_index("x"), lax.axis_index("y"), lax.axis_index("c")
    copies = [pltpu.make_async_remote_copy(src_ref=srcs[t].at[1 - c], dst_ref=lands[t], send_sem=send_sems.at[t],
                                           recv_sem=recv_sems.at[t], device_id=(x, y, 1 - c),
                                           device_id_type=pl.DeviceIdType.MESH) for t in range(nt)]
    return copies, copies


def _sibling_exchange(arrs, name):
    n = len(arrs)

    def body(*refs):
        ins, outs = refs[:n], refs[n:2 * n]
        send_sems, recv_sems = refs[2 * n:]
        sib = (lax.axis_index("x"), lax.axis_index("y"), 1 - lax.axis_index("c"))
        copies = [pltpu.make_async_remote_copy(src_ref=ins[t], dst_ref=outs[t], send_sem=send_sems.at[t],
                                               recv_sem=recv_sems.at[t], device_id=sib,
                                               device_id_type=pl.DeviceIdType.MESH) for t in range(n)]
        for cp in copies:
            cp.start()
        for cp in copies:
            cp.wait_recv()
        for cp in copies:
            cp.wait_send()

    return pl.pallas_call(
        body, name=name, in_specs=[HBM_SPEC] * n, out_specs=(HBM_SPEC,) * n,
        out_shape=tuple(jax.ShapeDtypeStruct(a.shape, a.dtype) for a in arrs),
        scratch_shapes=[pltpu.SemaphoreType.DMA((n,)), pltpu.SemaphoreType.DMA((n,))])(*arrs)


def _all_devices_copies(nt, refs, send_sems, recv_sems):
    srcs, lands = refs[:nt], refs[nt:]
    x, y, c = lax.axis_index("x"), lax.axis_index("y"), lax.axis_index("c")
    me = 4 * x + 2 * y + c
    starts, waits = [], []
    for t in range(nt):
        for mask in range(1, N_DEVICES):
            px = 1 - x if mask & 4 else x
            py = 1 - y if mask & 2 else y
            pc = 1 - c if mask & 1 else c
            k = (N_DEVICES - 1) * t + mask - 1

            def copy(slot, t=t, k=k, px=px, py=py, pc=pc):
                return pltpu.make_async_remote_copy(
                    src_ref=srcs[t], dst_ref=lands[t].at[slot], send_sem=send_sems.at[k], recv_sem=recv_sems.at[k],
                    device_id=(px, py, pc), device_id_type=pl.DeviceIdType.MESH)
            starts.append(copy(me))
            waits.append(copy(4 * px + 2 * py + pc))
    return starts, waits


def _all_devices_gather(buf, name):
    def body(in_ref, out_ref, send_sems, recv_sems, local_sem):
        x, y, c = lax.axis_index("x"), lax.axis_index("y"), lax.axis_index("c")
        me = 4 * x + 2 * y + c

        def peer(mask):
            px = 1 - x if mask & 4 else x
            py = 1 - y if mask & 2 else y
            pc = 1 - c if mask & 1 else c
            return px, py, pc

        def remote(mask, dst_slot):
            return pltpu.make_async_remote_copy(
                src_ref=in_ref, dst_ref=out_ref.at[dst_slot], send_sem=send_sems.at[mask - 1],
                recv_sem=recv_sems.at[mask - 1], device_id=peer(mask), device_id_type=pl.DeviceIdType.MESH)

        lc = pltpu.make_async_copy(in_ref, out_ref.at[me], local_sem)
        lc.start()
        sends = [remote(mask, me) for mask in range(1, N_DEVICES)]
        for cp in sends:
            cp.start()
        for mask in range(1, N_DEVICES):
            px, py, pc = peer(mask)
            remote(mask, 4 * px + 2 * py + pc).wait_recv()
        for cp in sends:
            cp.wait_send()
        lc.wait()

    return pl.pallas_call(
        body, name=name, in_specs=[HBM_SPEC], out_specs=HBM_SPEC,
        out_shape=jax.ShapeDtypeStruct((N_DEVICES,) + buf.shape, buf.dtype),
        scratch_shapes=[pltpu.SemaphoreType.DMA((N_DEVICES - 1,)), pltpu.SemaphoreType.DMA((N_DEVICES - 1,)),
                        pltpu.SemaphoreType.DMA])(buf)


def _pad_lanes(vec):
    return jnp.pad(vec.astype(F32), (0, LANES - vec.shape[0])).reshape(1, LANES)


def _layer_fwd(x, wl, fetch, dm, tag):
    heads, gw, lw, d = dm['heads'], dm['gw'], dm['lw'], dm['d']
    h = _rms_fwd(x, wl['attn_norm'], f"rms1_fwd{tag}")
    wl.update(fetch('in', h))
    proj = _matmul(h, wl['w_in_p'], mode='nn', tn=PROJ_COL_TILE, name=f"proj{tag}")
    alog, dtb = _pad_lanes(wl['gdn_a_log']), _pad_lanes(wl['gdn_dt_bias'])
    qkv = _gdn_pre_fwd(proj, wl['gdn_conv_w'], heads, f"gdn_pre_fwd{tag}")
    o, t_all, s0_all = _gdn_fwd(qkv, proj, dm['ab_blk'], alog, dtb, heads, f"gdn_fwd{tag}")
    o_gdn = _gdn_post_fwd(o, proj, dm['z_blk'], wl['gdn_norm'], f"gdn_post_fwd{tag}")
    xc = _conv_bias_fwd(proj, dm['xb_blk'], wl['lru_conv_w'], wl['lru_conv_b'], f"lru_conv_fwd{tag}")
    hseq, o_lru = _lru_fwd(xc, proj, dm['yb_blk'], wl['lru_w_a'], wl['lru_w_x'], wl['lru_b_a'], wl['lru_b_x'],
                           wl['lru_lambda'], f"lru_fwd{tag}")
    wl.update(fetch('mix', o))
    bg = _matmul(o_gdn, wl['w_branch_gdn'], mode='nn', name=f"branch_gdn{tag}")
    bl = _matmul(o_lru, wl['w_branch_lru'], mode='nn', name=f"branch_lru{tag}")
    merged = _merge_fwd(proj, dm['gg_blk'], dm['gl_blk'], bg, bl, f"merge_fwd{tag}")
    wl.update(fetch('mlp', bg))
    x_mid = _matmul(merged, wl['w_out'], mode='nn', add=x, name=f"out_proj{tag}")
    h2 = _rms_fwd(x_mid, wl['mlp_norm'], f"rms2_fwd{tag}")
    ur, act = _matmul(h2, wl['w_up'], mode='nn', epilogue='relu2', name=f"mlp_up{tag}")
    x_out = _matmul(act, wl['w_down'], mode='nn', add=x_mid, name=f"mlp_down{tag}")
    saved = dict(x=x, h=h, proj=proj, qkv=qkv, o=o, t_all=t_all, s0_all=s0_all, o_gdn=o_gdn, xc=xc, hseq=hseq,
                 o_lru=o_lru, bg=bg, bl=bl, merged=merged, x_mid=x_mid, h2=h2, ur=ur, act=act, alog=alog, dtb=dtb)
    return x_out, saved


def _layer_bwd(dx_out, dx_out_b, wl, sv, hook, dm, tag):
    heads, gw, lw, d = dm['heads'], dm['gw'], dm['lw'], dm['d']
    g = {}
    du = _matmul(dx_out_b, wl['w_down'], mode='nt', epilogue='mul2x', extra=sv['ur'], out_dtype=BF16,
                 name=f"d_mlp_act{tag}")
    def dw(n, lhs, rhs):
        return _matmul(lhs, rhs, mode='tn', out_dtype=BF16, shard_axis=BIG_SHARD_AXIS[n] - 1, name=f"d{n}{tag}")

    g['w_down'] = dw('w_down', sv['act'], dx_out_b)
    g['w_up'] = dw('w_up', sv['h2'], du)
    hook('mlp', g, wl, 'mlp_norm')
    dh2 = _matmul(du, wl['w_up'], mode='nt', name=f"d_h2{tag}")
    dx_mid, dx_mid_b, g['mlp_norm'] = _rms_bwd(sv['x_mid'], wl['mlp_norm'], dh2, dx_out, f"rms2_bwd{tag}")
    dmerged = _matmul(dx_mid_b, wl['w_out'], mode='nt', name=f"d_merged{tag}")
    g['w_out'] = dw('w_out', sv['merged'], dx_mid_b)
    dgg, dgl, dbg, dbl = _merge_bwd(sv['proj'], dm['gg_blk'], dm['gl_blk'], sv['bg'], sv['bl'], dmerged,
                                    f"merge_bwd{tag}")
    g['w_branch_gdn'] = dw('w_branch_gdn', sv['o_gdn'], dbg)
    g['w_branch_lru'] = dw('w_branch_lru', sv['o_lru'], dbl)
    hook('mix', g, wl, 'gdn_norm')
    do_gdn = _matmul(dbg, wl['w_branch_gdn'], mode='nt', name=f"d_o_gdn{tag}")
    do_lru = _matmul(dbl, wl['w_branch_lru'], mode='nt', name=f"d_o_lru{tag}")
    d_o, dz, dgn = _gdn_post_bwd(sv['o'], sv['proj'], dm['z_blk'], wl['gdn_norm'], do_gdn, f"gdn_post_bwd{tag}")
    g['gdn_norm'] = dgn.reshape(-1)
    dq, dk, dv, dgb = _gdn_bwd(sv['qkv'], sv['proj'], dm['ab_blk'], sv['alog'], sv['dtb'], sv['t_all'], sv['s0_all'], d_o, heads,
                               f"gdn_bwd{tag}")
    dconv = _gdn_pre_bwd(sv['proj'], wl['gdn_conv_w'], dq, dk, dv, heads, f"gdn_pre_bwd{tag}")
    dqkv, g['gdn_conv_w'], _ = _conv_bwd(dconv, sv['proj'], 0, wl['gdn_conv_w'], f"gdn_conv_bwd{tag}")
    dab, dab_sum = _dab_reduce(dgb, f"dab_reduce{tag}")
    g['gdn_dt_bias'] = dab_sum[0, :heads]
    g['gdn_a_log'] = dab_sum[0, 2 * heads:3 * heads]
    dxc, dyb, g['lru_w_a'], g['lru_w_x'], dba, dbx, dlam = _lru_bwd(
        do_lru, sv['xc'], sv['hseq'], sv['proj'], dm['yb_blk'], wl['lru_w_a'], wl['lru_w_x'], wl['lru_b_a'],
        wl['lru_b_x'], wl['lru_lambda'], f"lru_bwd{tag}")
    g['lru_b_a'], g['lru_b_x'], g['lru_lambda'] = dba.reshape(-1), dbx.reshape(-1), dlam.reshape(-1)
    dxb, g['lru_conv_w'], dcb = _conv_bwd(dxc, sv['proj'], dm['xb_blk'], wl['lru_conv_w'], f"lru_conv_bwd{tag}")
    g['lru_conv_b'] = dcb.reshape(-1)
    dproj = jnp.concatenate([dqkv.astype(BF16), dz.astype(BF16), dxb.astype(BF16), dyb.astype(BF16), dgg, dgl,
                             dab.astype(BF16), jnp.zeros((dab.shape[0], dm['np'] - dm['main'] - LANES), BF16)],
                            axis=1)
    g['w_in_p'] = _matmul(sv['h'], dproj, mode='tn', out_dtype=BF16, tn=PROJ_COL_TILE, name=f"dw_in{tag}")
    hook('in', g, wl, 'attn_norm')
    dh = _matmul(dproj, wl['w_in_p'], mode='nt', name=f"d_h{tag}")
    dx_in, dx_in_b, g['attn_norm'] = _rms_bwd(sv['x'], wl['attn_norm'], dh, dx_mid, f"rms1_bwd{tag}")
    g['attn_norm'] = g['attn_norm'].reshape(-1)
    g['mlp_norm'] = g['mlp_norm'].reshape(-1)
    return dx_in, dx_in_b, g


def _dims(d, heads, lw):
    gw = heads * LANES
    nab = 2 * heads
    blk = dict(z_blk=3 * heads, xb_blk=4 * heads, yb_blk=4 * heads + lw // LANES)
    gg0 = 4 * gw + 2 * lw
    main = gg0 + 2 * d
    return dict(d=d, heads=heads, gw=gw, lw=lw, nab=nab, gg_blk=gg0 // LANES, gl_blk=(gg0 + d) // LANES,
                main=main, ab_blk=main // LANES, np=-(-(main + LANES) // PROJ_COL_TILE) * PROJ_COL_TILE, **blk)


def _pad_w_in(w_in, dm):
    c0 = 4 * dm['gw']
    nab = dm['nab']
    return jnp.concatenate([w_in[:, :c0], w_in[:, c0 + nab:], w_in[:, c0:c0 + nab],
                            jnp.zeros((w_in.shape[0], dm['np'] - dm['main'] - nab), w_in.dtype)], axis=1)


def _unpad_w_in(gp, dm):
    c0 = 4 * dm['gw']
    nab = dm['nab']
    main = dm['main']
    return jnp.concatenate([gp[:, :c0], gp[:, main:main + nab], gp[:, c0:main]], axis=1)


def _local_step(x, target, layers, fetchers, hooks, final_norm, dm):
    saved = []
    cur = x
    for li, wl in enumerate(layers):
        cur, sv = _layer_fwd(cur, wl, fetchers[li], dm, f"_l{li}")
        saved.append(sv)
    loss_blk, dx, dx_b, dfin = _loss_head(cur, final_norm, target, "loss_head")
    grads = [None] * len(layers)
    for li in reversed(range(len(layers))):
        dx, dx_b, grads[li] = _layer_bwd(dx, dx_b, layers[li], saved[li], hooks[li], dm, f"_l{li}")
    return loss_blk[0, 0], dx, grads, dfin.reshape(-1)


def kernel(x, attn_norm, w_in, gdn_conv_w, gdn_a_log, gdn_dt_bias, gdn_norm, lru_conv_w, lru_conv_b, lru_w_a, lru_b_a, lru_w_x, lru_b_x, lru_lambda, w_branch_gdn, w_branch_lru, w_out, mlp_norm, w_up, w_down, final_norm, loss_target, m_attn_norm, m_w_in, m_gdn_conv_w, m_gdn_a_log, m_gdn_dt_bias, m_gdn_norm, m_lru_conv_w, m_lru_conv_b, m_lru_w_a, m_lru_b_a, m_lru_w_x, m_lru_b_x, m_lru_lambda, m_w_branch_gdn, m_w_branch_lru, m_w_out, m_mlp_norm, m_w_up, m_w_down, m_final_norm, v_attn_norm, v_w_in, v_gdn_conv_w, v_gdn_a_log, v_gdn_dt_bias, v_gdn_norm, v_lru_conv_w, v_lru_conv_b, v_lru_w_a, v_lru_b_a, v_lru_w_x, v_lru_b_x, v_lru_lambda, v_w_branch_gdn, v_w_branch_lru, v_w_out, v_mlp_norm, v_w_up, v_w_down, v_final_norm):
    w = dict(attn_norm=attn_norm, w_in=w_in, gdn_conv_w=gdn_conv_w, gdn_a_log=gdn_a_log, gdn_dt_bias=gdn_dt_bias,
             gdn_norm=gdn_norm, lru_conv_w=lru_conv_w, lru_conv_b=lru_conv_b, lru_w_a=lru_w_a, lru_b_a=lru_b_a,
             lru_w_x=lru_w_x, lru_b_x=lru_b_x, lru_lambda=lru_lambda, w_branch_gdn=w_branch_gdn,
             w_branch_lru=w_branch_lru, w_out=w_out, mlp_norm=mlp_norm, w_up=w_up, w_down=w_down,
             final_norm=final_norm)
    m = dict(attn_norm=m_attn_norm, w_in=m_w_in, gdn_conv_w=m_gdn_conv_w, gdn_a_log=m_gdn_a_log,
             gdn_dt_bias=m_gdn_dt_bias, gdn_norm=m_gdn_norm, lru_conv_w=m_lru_conv_w, lru_conv_b=m_lru_conv_b,
             lru_w_a=m_lru_w_a, lru_b_a=m_lru_b_a, lru_w_x=m_lru_w_x, lru_b_x=m_lru_b_x, lru_lambda=m_lru_lambda,
             w_branch_gdn=m_w_branch_gdn, w_branch_lru=m_w_branch_lru, w_out=m_w_out, mlp_norm=m_mlp_norm,
             w_up=m_w_up, w_down=m_w_down, final_norm=m_final_norm)
    v = dict(attn_norm=v_attn_norm, w_in=v_w_in, gdn_conv_w=v_gdn_conv_w, gdn_a_log=v_gdn_a_log,
             gdn_dt_bias=v_gdn_dt_bias, gdn_norm=v_gdn_norm, lru_conv_w=v_lru_conv_w, lru_conv_b=v_lru_conv_b,
             lru_w_a=v_lru_w_a, lru_b_a=v_lru_b_a, lru_w_x=v_lru_w_x, lru_b_x=v_lru_b_x, lru_lambda=v_lru_lambda,
             w_branch_gdn=v_w_branch_gdn, w_branch_lru=v_w_branch_lru, w_out=v_w_out, mlp_norm=v_mlp_norm,
             w_up=v_w_up, w_down=v_w_down, final_norm=v_final_norm)
    n_layers = attn_norm.shape[0]
    d = x.shape[-1]
    heads = gdn_a_log.shape[-1]
    lw = lru_conv_b.shape[-1]
    dm = _dims(d, heads, lw)
    big_names = list(BIG_SHARD_AXIS)
    conv_names = list(CONV_SHARD_AXIS)
    chip = 2 * lax.axis_index("x") + lax.axis_index("y")

    conv_flat = jnp.concatenate([w[n].reshape(-1) for n in conv_names])
    conv_rows = -(-conv_flat.shape[0] // (SUBLANES * LANES)) * SUBLANES
    conv_buf = jnp.pad(conv_flat, (0, conv_rows * LANES - conv_flat.shape[0])).reshape(conv_rows, LANES)
    conv_all = _all_devices_gather(conv_buf, "conv_allgather").reshape(N_CHIPS, 2, -1)[:, 0]
    conv_full, off = {}, 0
    for n in conv_names:
        shard = w[n]
        parts = conv_all[:, off:off + shard.size].reshape((N_CHIPS,) + shard.shape)
        conv_full[n] = jnp.concatenate([parts[q] for q in range(N_CHIPS)], axis=CONV_SHARD_AXIS[n])
        off += shard.size

    def start_gather(li, group):
        halves, lands = [], []
        for n in GATHER_GROUPS[group]:
            s = w[n][li].astype(BF16)
            hv = s.reshape((2, s.shape[0] // 2) + s.shape[1:])
            halves.append(hv)
            lands.append(lax.dynamic_update_index_in_dim(lax.empty((N_CHIPS,) + hv.shape, BF16), hv, chip, 0))
        nt = len(halves)
        return _split_start(f"wgather_{group}_l{li}_ici_start", halves + lands, 3 * nt,
                            functools.partial(_gather_ici_copies, nt))

    pending = {(li, group): start_gather(li, group) for li in range(n_layers) for group in GATHER_GROUPS}

    swapping = {}

    def start_swap(li, group, after):
        nt = len(GATHER_GROUPS[group])
        send, recv, bufs, _ = pending.pop((li, group))
        bufs = _split_wait(f"wgather_{group}_l{li}_ici_wait", send, recv, bufs, after,
                           functools.partial(_gather_ici_copies, nt))
        swapping[(li, group)] = _split_start(f"wgather_{group}_l{li}_d2d_start", bufs[nt:], 3 * nt,
                                             functools.partial(_gather_d2d_copies, nt))
        return swapping[(li, group)][3]

    def make_fetch(li):
        def fetch(group, after):
            names = GATHER_GROUPS[group]
            nt = len(names)
            if (li, group) not in swapping:
                start_swap(li, group, after)
            send, recv, lands, _ = swapping.pop((li, group))
            lands = _split_wait(f"wgather_{group}_l{li}_d2d_wait", send, recv, lands, after,
                                functools.partial(_gather_d2d_copies, nt))
            if (li, group) == (0, 'mix'):
                tie = sum(start_swap(l2, g2, lands[0])[0, 0] for l2, g2 in list(pending))
                layers[0]['mlp_norm'] = layers[0]['mlp_norm'] + tie
            out = {}
            for n, land in zip(names, lands):
                slots = land.reshape((N_CHIPS, 2 * land.shape[2]) + land.shape[3:])
                out[n] = jnp.concatenate([slots[q] for q in range(N_CHIPS)], axis=BIG_SHARD_AXIS[n] - 1)
            if 'w_in' in out:
                out['w_in_p'] = _pad_w_in(out.pop('w_in'), dm)
            return out
        return fetch

    layers = []
    for li in range(n_layers):
        wl = {n: w[n][li] for n in SMALL_NAMES if n != 'final_norm' and n not in CONV_SHARD_AXIS}
        for n in conv_names:
            wl[n] = conv_full[n][li]
        layers.append(wl)
    layers[0]['attn_norm'] = layers[0]['attn_norm'] + sum(handle[3][0, 0] for handle in pending.values())

    core = lax.axis_index("c").astype(jnp.int32).reshape(1)
    chip_op = chip.astype(jnp.int32).reshape(1)
    sending, in_flight, reduced = {}, {}, {}

    def reduce_begin(group, li, g):
        names = GATHER_GROUPS[group]
        nt = len(names)
        contrib = []
        for n in names:
            if n != 'w_in':
                contrib.append(g[n])
                continue
            pieces = jnp.stack(jnp.split(_unpad_w_in(g['w_in_p'], dm), N_CHIPS, axis=BIG_SHARD_AXIS[n] - 1), axis=0)
            rows_half = pieces.shape[1] // 2
            contrib.append(jnp.swapaxes(pieces.reshape((N_CHIPS, 2, rows_half) + pieces.shape[2:]), 0, 1))
        theirs = [lax.empty(cb.shape[1:], BF16) for cb in contrib]
        send, recv, bufs, token = _split_start(f"gsend_{group}_l{li}_start", contrib + theirs, nt,
                                               functools.partial(_sibling_half_copies, nt))
        sending[(group, li)] = (send, recv, bufs)
        return token

    def reduce_scatter(group, li, after):
        names = GATHER_GROUPS[group]
        nt = len(names)
        send, recv, bufs = sending.pop((group, li))
        bufs = _split_wait(f"gsend_{group}_l{li}_wait", send, recv, bufs, after,
                           functools.partial(_sibling_half_copies, nt))
        sums = [_pair_sum(core, mine, th, f"gpair_{n}_l{li}") for n, mine, th in zip(names, bufs[:nt], bufs[nt:])]
        lands = [jnp.zeros(sm.shape, BF16) for sm in sums]
        send, recv, bufs, token = _split_start(f"gscatter_{group}_l{li}_start", sums + lands, 3 * nt,
                                               functools.partial(_scatter_ici_copies, nt))
        in_flight[(group, li)] = (send, recv, bufs)
        return token

    def reduce_end(jobs, after, name):
        keys, totals = [], []
        for group, li in jobs:
            names = GATHER_GROUPS[group]
            nt = len(names)
            send, recv, bufs = in_flight.pop((group, li))
            bufs = _split_wait(f"gscatter_{group}_l{li}_wait", send, recv, bufs, after,
                               functools.partial(_scatter_ici_copies, nt))
            totals += [_sum_landed(chip_op, land, own, f"gtotal_{n}_l{li}")
                       for n, own, land in zip(names, bufs[:nt], bufs[nt:])]
            keys += [(n, li) for n in names]
        others = _sibling_exchange(totals, name)
        for key, mine, other in zip(keys, totals, others):
            reduced[key] = (mine, other)

    order = [(group, li) for li in reversed(range(n_layers)) for group in reversed(list(GATHER_GROUPS))]

    def make_hook(li):
        def hook(group, g, wl, gain):
            at = order.index((group, li))
            latest = g['w_in_p'] if group == 'in' else g[GATHER_GROUPS[group][-1]]
            tie = reduce_begin(group, li, g)[0, 0]
            if at > 0:
                tie = tie + reduce_scatter(*order[at - 1], latest)[0, 0]
            wl[gain] = wl[gain] + tie
        return hook

    loss_local, dx, grads, dfin = _local_step(x[0], loss_target[0], layers, [make_fetch(li) for li in range(n_layers)],
                                              [make_hook(li) for li in range(n_layers)], final_norm, dm)
    loss = lax.psum(loss_local, MESH_AXES)
    last_scatter_token = reduce_scatter(*order[-1], dx)

    small_g = {n: jnp.stack([grads[li][n] for li in range(n_layers)], axis=0)
               for n in SMALL_NAMES if n != 'final_norm'}
    small_g['final_norm'] = dfin
    flat = jnp.concatenate([small_g[n].reshape(-1) for n in PACK_NAMES])
    n_flat = flat.shape[0]
    row_unit = 32 * SUBLANES
    rows = -(-n_flat // (row_unit * LANES)) * row_unit
    small_srcs = [jnp.pad(flat, (0, rows * LANES - n_flat)).reshape(rows, LANES)]
    small_srcs += [small_g[n].reshape(-1, LANES) for n in MID_NAMES]
    device = 2 * chip + lax.axis_index("c")
    small_lands = [lax.dynamic_update_index_in_dim(lax.empty((N_DEVICES,) + b.shape, F32), b, device, 0)
                   for b in small_srcs]
    n_small = len(small_srcs)
    small_copies = functools.partial(_all_devices_copies, n_small)
    small_send, small_recv, small_bufs, small_token = _split_start(
        "small_grad_start", small_srcs + small_lands, (N_DEVICES - 1) * n_small, small_copies)

    out_g, out_d, out_m, out_v = {}, {}, {}, {}

    def adamw_big(n, after):
        quarters = (n_layers, 2, w[n].shape[1] // 2, w[n].shape[2])
        res = _adamw_quarters(core, w[n].reshape(quarters), [reduced[(n, li)][0] for li in range(n_layers)],
                              [reduced[(n, li)][1] for li in range(n_layers)], m[n].reshape(quarters),
                              v[n].reshape(quarters), after, f"adamw_{n}")
        out_g[n], out_d[n], out_m[n], out_v[n] = (r.reshape(w[n].shape) for r in res)
        return res[1]

    early = [group for group in GATHER_GROUPS if group != 'in']
    last = small_token + last_scatter_token
    reduce_end([job for job in order if job[0] != 'in'], last, "gswap_early")
    for group in early:
        for n in GATHER_GROUPS[group]:
            last = adamw_big(n, last)
    reduce_end([job for job in order if job[0] == 'in'], last, "gswap_in")
    last = adamw_big('w_in', last)
    small_bufs = _split_wait("small_grad_wait", small_send, small_recv, small_bufs, last, small_copies)
    small_sums = [_sum_slots(land, f"small_grad_sum_{k}") for k, land in enumerate(small_bufs[n_small:])]
    small_red = {}
    off = 0
    for n in PACK_NAMES:
        size = small_g[n].size
        small_red[n] = small_sums[0].reshape(-1)[off:off + size].reshape(small_g[n].shape)
        off += size
    for n, g2d in zip(MID_NAMES, small_sums[1:]):
        res = _adamw(w[n].reshape(g2d.shape), [g2d], m[n].reshape(g2d.shape), v[n].reshape(g2d.shape), f"adamw_{n}")
        out_g[n], out_d[n], out_m[n], out_v[n] = (r.reshape(w[n].shape) for r in res)
    for n, ax in CONV_SHARD_AXIS.items():
        width = w[n].shape[ax]
        small_red[n] = lax.dynamic_slice_in_dim(small_red[n], chip * width, width, axis=ax)

    def pack(tree):
        fl = jnp.concatenate([tree[n].reshape(-1) for n in PACK_NAMES])
        return jnp.pad(fl, (0, rows * LANES - fl.shape[0])).reshape(rows, LANES)

    res = _adamw(pack(w), [pack(small_red)], pack(m), pack(v), "adamw_small")
    for r, dst in zip(res, (out_g, out_d, out_m, out_v)):
        fl = r.reshape(-1)
        off = 0
        for n in PACK_NAMES:
            dst[n] = fl[off:off + w[n].size].reshape(w[n].shape)
            off += w[n].size

    return (loss, dx[None], *[out_g[n] for n in WEIGHT_NAMES], *[out_d[n] for n in WEIGHT_NAMES],
            *[out_m[n] for n in WEIGHT_NAMES], *[out_v[n] for n in WEIGHT_NAMES])
```

```python
import functools

import jax
import jax.numpy as jnp
from jax import lax
from jax.experimental import pallas as pl
from jax.experimental.pallas import tpu as pltpu

F32 = jnp.float32
BF16 = jnp.bfloat16

LANES = 128
SUBLANES = 8
VMEM_BYTES = 64 * 1024 * 1024
GDN_CHUNK = 64
CONV_WIDTH = 4
CONV_ROW_TILE = 2048
RMS_EPS = 1e-6
L2_EPS = 1e-6
LRU_C = 8.0
ADAM_LR = 0.001
ADAM_B1 = 0.9
ADAM_B2 = 0.999
ADAM_EPS = 1e-08
ADAM_WD = 0.01
ADAM_STEP = 10
MESH_AXES = ("x", "y", "c")
N_CHIPS = 4
N_DEVICES = 8

INPUT_NAMES = ['x', 'attn_norm', 'w_in', 'gdn_conv_w', 'gdn_a_log', 'gdn_dt_bias', 'gdn_norm', 'lru_conv_w',
               'lru_conv_b', 'lru_w_a', 'lru_b_a', 'lru_w_x', 'lru_b_x', 'lru_lambda', 'w_branch_gdn',
               'w_branch_lru', 'w_out', 'mlp_norm', 'w_up', 'w_down', 'final_norm']
WEIGHT_NAMES = INPUT_NAMES[1:]
BIG_SHARD_AXIS = {'w_in': 2, 'w_branch_gdn': 2, 'w_branch_lru': 2, 'w_out': 1, 'w_up': 2, 'w_down': 1}
CONV_SHARD_AXIS = {'gdn_conv_w': 2, 'lru_conv_w': 2}
GATHER_GROUPS = {'in': ['w_in'], 'mix': ['w_branch_gdn', 'w_branch_lru', 'w_out'], 'mlp': ['w_up', 'w_down']}
SMALL_NAMES = [n for n in WEIGHT_NAMES if n not in BIG_SHARD_AXIS]
MID_NAMES = ['lru_w_a', 'lru_w_x']
PACK_NAMES = [n for n in SMALL_NAMES if n not in MID_NAMES]


def _tile(n, target, unit=LANES):
    best = None
    t = unit
    while t <= min(n, target):
        if n % t == 0:
            best = t
        t += unit
    return n if best is None else best


def _vmem_limit(block_bytes):
    return int(min(max(3 * block_bytes + (8 << 20), 24 << 20), VMEM_BYTES - (8 << 20)))


def _nbytes(shape, dtype):
    n = 1
    for s in shape:
        n *= s
    return n * jnp.dtype(dtype).itemsize


def _pcall(body, *, name, grid, in_specs, out_specs, out_shape, scratch_shapes=(), semantics=None, block_bytes=0,
           scalar_prefetch=0):
    params = dict(vmem_limit_bytes=_vmem_limit(block_bytes))
    if semantics is not None:
        params['dimension_semantics'] = semantics
    if scalar_prefetch:
        grid_spec = pltpu.PrefetchScalarGridSpec(num_scalar_prefetch=scalar_prefetch, grid=grid, in_specs=in_specs,
                                                 out_specs=out_specs, scratch_shapes=list(scratch_shapes))
        return pl.pallas_call(body, name=name, grid_spec=grid_spec, out_shape=out_shape,
                              compiler_params=pltpu.CompilerParams(**params))
    return pl.pallas_call(body, name=name, grid=grid, in_specs=in_specs, out_specs=out_specs, out_shape=out_shape,
                          scratch_shapes=list(scratch_shapes), compiler_params=pltpu.CompilerParams(**params))


def _dot(a, b):
    return jnp.dot(a.astype(BF16), b.astype(BF16), preferred_element_type=F32)


def _dot_nt(a, b):
    return lax.dot_general(a.astype(BF16), b.astype(BF16), (((1,), (1,)), ((), ())), preferred_element_type=F32)


def _dot_tn(a, b):
    return lax.dot_general(a.astype(BF16), b.astype(BF16), (((0,), (0,)), ((), ())), preferred_element_type=F32)


def _sigmoid(x):
    return 1.0 / (1.0 + jnp.exp(-x))


def _log1p(u):
    return jnp.where(u < 1e-3, u * (1.0 - u * (0.5 - u * (1.0 / 3.0))), jnp.log(1.0 + u))


def _softplus(x):
    return jnp.maximum(x, 0.0) + _log1p(jnp.exp(-jnp.abs(x)))


_GELU_K = 0.7978845608028654


def _gelu_and_grad(x):
    inner = _GELU_K * (x + 0.044715 * x * x * x)
    th = jnp.tanh(inner)
    g = 0.5 * x * (1.0 + th)
    dg = 0.5 * (1.0 + th) + 0.5 * x * (1.0 - th * th) * _GELU_K * (1.0 + 3.0 * 0.044715 * x * x)
    return g, dg


MATMUL_TK_MAX = 3584
PROJ_COL_TILE = 1536


def _matmul(a, b, *, mode, name, out_dtype=F32, add=None, epilogue=None, extra=None, tm=512, tn=1024, tk=2048,
            shard_axis=None):
    if mode == 'nn':
        (m, k), (k2, n) = a.shape, b.shape
    elif mode == 'nt':
        (m, k), (n, k2) = a.shape, b.shape
    else:
        (k, m), (k2, n) = a.shape, b.shape
    assert k == k2, (a.shape, b.shape, mode)
    if shard_axis is not None:
        rows_half = (m // N_CHIPS if shard_axis == 0 else m) // 2
        cols = n // N_CHIPS if shard_axis == 1 else n
        tm, tn = _tile(rows_half, tm), _tile(cols, tn)
    else:
        tm, tn = _tile(m, tm), _tile(n, tn)
    tk = _tile(k, tk)
    if k // tk > 2 * (-(-k // MATMUL_TK_MAX)):
        tk = _tile(k, MATMUL_TK_MAX)
    nk = k // tk
    dims = {'nn': (((1,), (0,)), ((), ())), 'nt': (((1,), (1,)), ((), ())), 'tn': (((0,), (0,)), ((), ()))}[mode]
    a_bytes, b_bytes = _nbytes(a.shape, a.dtype), _nbytes(b.shape, b.dtype)
    rows_outer = nk > 1 or a_bytes + (m // tm) * b_bytes <= b_bytes + (n // tn) * a_bytes

    def ij(g0, g1):
        return (g0, g1) if rows_outer else (g1, g0)

    def spec(shape, pick):
        return pl.BlockSpec(shape, lambda g0, g1, kk: pick(*ij(g0, g1), kk))

    a_spec = spec((tk, tm), lambda i, j, kk: (kk, i)) if mode == 'tn' else spec((tm, tk), lambda i, j, kk: (i, kk))
    b_spec = spec((tn, tk), lambda i, j, kk: (j, kk)) if mode == 'nt' else spec((tk, tn), lambda i, j, kk: (kk, j))
    o_spec = spec((tm, tn), lambda i, j, kk: (i, j))
    operands, in_specs = [a, b], [a_spec, b_spec]
    if add is not None:
        operands.append(add)
        in_specs.append(o_spec)
    if extra is not None:
        operands.append(extra)
        in_specs.append(o_spec)
    n_in = len(operands)
    if epilogue == 'relu2':
        out_shape = (jax.ShapeDtypeStruct((m, n), BF16), jax.ShapeDtypeStruct((m, n), BF16))
        out_specs = (o_spec, o_spec)
    elif shard_axis is not None:
        assert add is None and extra is None
        rb, cb = rows_half // tm, cols // tn

        def shard_block(i, j, kk):
            if shard_axis == 0:
                return (i % (2 * rb)) // rb, i // (2 * rb), i % rb, j
            return i // rb, j // cb, i % rb, j % cb

        out_shape = jax.ShapeDtypeStruct((2, N_CHIPS, rows_half, cols), out_dtype)
        out_specs = spec((None, None, tm, tn), shard_block)
    else:
        out_shape = jax.ShapeDtypeStruct((m, n), out_dtype)
        out_specs = o_spec

    def body(*refs):
        a_ref, b_ref = refs[0], refs[1]
        outs = refs[n_in:n_in + n_out]

        def finish(p):
            if add is not None:
                p = p + refs[2][...]
            if epilogue == 'relu2':
                ur = jnp.maximum(p, 0.0)
                outs[0][...] = ur.astype(BF16)
                outs[1][...] = (ur * ur).astype(BF16)
            elif epilogue == 'mul2x':
                outs[0][...] = (p * 2.0 * refs[n_in - 1][...].astype(F32)).astype(out_dtype)
            else:
                outs[0][...] = p.astype(out_dtype)

        prod = lax.dot_general(a_ref[...].astype(BF16), b_ref[...].astype(BF16), dims, preferred_element_type=F32)
        if nk == 1:
            finish(prod)
            return
        acc_ref = refs[-1]
        kk = pl.program_id(2)

        @pl.when(kk == 0)
        def _():
            acc_ref[...] = prod

        @pl.when((kk > 0) & (kk < nk - 1))
        def _():
            acc_ref[...] += prod

        @pl.when(kk == nk - 1)
        def _():
            finish(acc_ref[...] + prod)

    n_out = 2 if epilogue == 'relu2' else 1
    bb = (_nbytes((tm, tk), a.dtype) + _nbytes((tk, tn), b.dtype) + 3 * _nbytes((tm, tn), F32))
    grid = (m // tm, n // tn, nk) if rows_outer else (n // tn, m // tm, nk)
    return _pcall(body, name=name, grid=grid, in_specs=in_specs, out_specs=out_specs, out_shape=out_shape,
                  scratch_shapes=[pltpu.VMEM((tm, tn), F32)] if nk > 1 else [],
                  semantics=("parallel", "parallel", "arbitrary"), block_bytes=bb)(*operands)


def _row_tile(s, d, target_bytes=1 << 20):
    return _tile(s, max(SUBLANES, target_bytes // (4 * d)), SUBLANES)


def _rms_fwd(x, gain, name):
    s, d = x.shape
    tr = _row_tile(s, d)

    def body(x_ref, g_ref, h_ref):
        xv = x_ref[...]
        r = lax.rsqrt(jnp.mean(xv * xv, axis=-1, keepdims=True) + RMS_EPS)
        h_ref[...] = (xv * r * g_ref[...]).astype(BF16)

    row = pl.BlockSpec((tr, d), lambda i: (i, 0))
    return _pcall(body, name=name, grid=(s // tr,), in_specs=[row, pl.BlockSpec((1, d), lambda i: (0, 0))],
                  out_specs=row, out_shape=jax.ShapeDtypeStruct((s, d), BF16), semantics=("parallel",),
                  block_bytes=2 * tr * d * 4)(x, gain.reshape(1, d))


def _rms_bwd(x, gain, dh, dres, name):
    s, d = x.shape
    tr = _row_tile(s, d, 1 << 19)

    def body(x_ref, g_ref, dh_ref, dres_ref, dx_ref, dxb_ref, dg_ref):
        xv = x_ref[...]
        r = lax.rsqrt(jnp.mean(xv * xv, axis=-1, keepdims=True) + RMS_EPS)
        xh = xv * r
        dhv = dh_ref[...]
        dxh = dhv * g_ref[...]
        dx = dres_ref[...] + r * (dxh - xh * jnp.mean(dxh * xh, axis=-1, keepdims=True))
        dx_ref[...] = dx
        dxb_ref[...] = dx.astype(BF16)

        @pl.when(pl.program_id(0) == 0)
        def _():
            dg_ref[...] = jnp.zeros_like(dg_ref)

        dg_ref[...] += jnp.sum(dhv * xh, axis=0, keepdims=True)

    row = pl.BlockSpec((tr, d), lambda i: (i, 0))
    vec = pl.BlockSpec((1, d), lambda i: (0, 0))
    return _pcall(body, name=name, grid=(s // tr,), in_specs=[row, vec, row, row], out_specs=(row, row, vec),
                  out_shape=(jax.ShapeDtypeStruct((s, d), F32), jax.ShapeDtypeStruct((s, d), BF16),
                             jax.ShapeDtypeStruct((1, d), F32)),
                  semantics=("arbitrary",), block_bytes=5 * tr * d * 4)(x, gain.reshape(1, d), dh, dres)


def _loss_head(x, gain, target, name):
    s, d = x.shape
    tr = _row_tile(s, d, 1 << 19)

    def body(x_ref, g_ref, t_ref, loss_ref, dx_ref, dxb_ref, dg_ref):
        xv = x_ref[...]
        r = lax.rsqrt(jnp.mean(xv * xv, axis=-1, keepdims=True) + RMS_EPS)
        xh = xv * r
        gv = g_ref[...]
        err = xh * gv - t_ref[...]
        dy = err * (1.0 / d)
        dxh = dy * gv
        dx = r * (dxh - xh * jnp.mean(dxh * xh, axis=-1, keepdims=True))
        dx_ref[...] = dx
        dxb_ref[...] = dx.astype(BF16)

        @pl.when(pl.program_id(0) == 0)
        def _():
            dg_ref[...] = jnp.zeros_like(dg_ref)
            loss_ref[...] = jnp.zeros_like(loss_ref)

        dg_ref[...] += jnp.sum(dy * xh, axis=0, keepdims=True)
        part = jnp.sum(jnp.sum(err * err, axis=-1, keepdims=True), axis=0, keepdims=True) * (0.5 / d)
        loss_ref[...] += jnp.broadcast_to(part, loss_ref.shape)

    row = pl.BlockSpec((tr, d), lambda i: (i, 0))
    vec = pl.BlockSpec((1, d), lambda i: (0, 0))
    lspec = pl.BlockSpec((SUBLANES, LANES), lambda i: (0, 0))
    return _pcall(body, name=name, grid=(s // tr,), in_specs=[row, vec, row], out_specs=(lspec, row, row, vec),
                  out_shape=(jax.ShapeDtypeStruct((SUBLANES, LANES), F32), jax.ShapeDtypeStruct((s, d), F32),
                             jax.ShapeDtypeStruct((s, d), BF16), jax.ShapeDtypeStruct((1, d), F32)),
                  semantics=("arbitrary",), block_bytes=4 * tr * d * 4)(x, gain.reshape(1, d), target)


def _shift_down(xc, xp, s):
    tr = xc.shape[0]
    r = pltpu.roll(xc, s, 0)
    p = pltpu.roll(xp, s, 0)
    row8 = lax.broadcasted_iota(jnp.int32, (SUBLANES, xc.shape[1]), 0)
    head = jnp.where(row8 < s, p, r[:SUBLANES])
    if tr == SUBLANES:
        return head
    return jnp.concatenate([head, r[SUBLANES:]], axis=0)


def _shift_up(yc, yn, s):
    tr = yc.shape[0]
    u = pltpu.roll(yc, tr - s, 0)
    n = pltpu.roll(yn, SUBLANES - s, 0)
    row8 = lax.broadcasted_iota(jnp.int32, (SUBLANES, yc.shape[1]), 0)
    tail = jnp.where(row8 >= SUBLANES - s, n, u[tr - SUBLANES:])
    if tr == SUBLANES:
        return tail
    return jnp.concatenate([u[:tr - SUBLANES], tail], axis=0)


def _conv_apply(xc, xp, w):
    y = xc * w[CONV_WIDTH - 1:CONV_WIDTH, :]
    for s in range(1, CONV_WIDTH):
        y = y + _shift_down(xc, xp, s) * w[CONV_WIDTH - 1 - s:CONV_WIDTH - s, :]
    return y


def _halo_specs(tr, col_of):
    per = tr // SUBLANES
    cur = pl.BlockSpec((tr, LANES), lambda j, i: (i, col_of(j)))
    prev = pl.BlockSpec((SUBLANES, LANES), lambda j, i: (jnp.maximum(i * per - 1, 0), col_of(j)))
    return cur, prev


def _conv_bias_fwd(x_arr, x_col0, w, bias, name):
    s = x_arr.shape[0]
    ncb = w.shape[1] // LANES
    tr = _tile(s, CONV_ROW_TILE, SUBLANES)

    def body(cur_ref, prev_ref, w_ref, b_ref, o_ref):
        i = pl.program_id(1)
        xp = prev_ref[...] * (i > 0).astype(F32)
        o_ref[...] = _conv_apply(cur_ref[...], xp, w_ref[...]) + b_ref[...]

    cur, prev = _halo_specs(tr, lambda j: x_col0 + j)
    return _pcall(body, name=name, grid=(ncb, s // tr),
                  in_specs=[cur, prev, pl.BlockSpec((CONV_WIDTH, LANES), lambda j, i: (0, j)),
                            pl.BlockSpec((1, LANES), lambda j, i: (0, j))],
                  out_specs=pl.BlockSpec((tr, LANES), lambda j, i: (i, j)),
                  out_shape=jax.ShapeDtypeStruct((s, w.shape[1]), F32), semantics=("parallel", "parallel"),
                  block_bytes=3 * tr * LANES * 4)(x_arr, x_arr, w, bias.reshape(1, -1))


def _conv_bwd(dy, x_arr, x_col0, w, name):
    s, c = dy.shape
    ncb = c // LANES
    tr = _tile(s, CONV_ROW_TILE, SUBLANES)
    per = tr // SUBLANES
    ni = s // tr

    def body(dy_ref, dyn_ref, cur_ref, prev_ref, w_ref, dx_ref, dw_ref, db_ref):
        i = pl.program_id(1)
        dyv = dy_ref[...]
        dn = dyn_ref[...] * (i < ni - 1).astype(F32)
        xc = cur_ref[...]
        xp = prev_ref[...] * (i > 0).astype(F32)
        wv = w_ref[...]

        @pl.when(i == 0)
        def _():
            dw_ref[...] = jnp.zeros_like(dw_ref)
            db_ref[...] = jnp.zeros_like(db_ref)

        dx = dyv * wv[CONV_WIDTH - 1:CONV_WIDTH, :]
        dw_ref[CONV_WIDTH - 1:CONV_WIDTH, :] += jnp.sum(dyv * xc, axis=0, keepdims=True)
        for sh in range(1, CONV_WIDTH):
            j = CONV_WIDTH - 1 - sh
            dx = dx + _shift_up(dyv, dn, sh) * wv[j:j + 1, :]
            dw_ref[j:j + 1, :] += jnp.sum(dyv * _shift_down(xc, xp, sh), axis=0, keepdims=True)
        dx_ref[...] = dx
        db_ref[...] += jnp.sum(dyv, axis=0, keepdims=True)

    cur, prev = _halo_specs(tr, lambda j: x_col0 + j)
    dcur = pl.BlockSpec((tr, LANES), lambda j, i: (i, j))
    dnext = pl.BlockSpec((SUBLANES, LANES), lambda j, i: (jnp.minimum((i + 1) * per, s // SUBLANES - 1), j))
    return _pcall(body, name=name, grid=(ncb, ni),
                  in_specs=[dcur, dnext, cur, prev, pl.BlockSpec((CONV_WIDTH, LANES), lambda j, i: (0, j))],
                  out_specs=(dcur, pl.BlockSpec((CONV_WIDTH, LANES), lambda j, i: (0, j)),
                             pl.BlockSpec((1, LANES), lambda j, i: (0, j))),
                  out_shape=(jax.ShapeDtypeStruct((s, c), F32), jax.ShapeDtypeStruct((CONV_WIDTH, c), F32),
                             jax.ShapeDtypeStruct((1, c), F32)),
                  semantics=("parallel", "arbitrary"), block_bytes=4 * tr * LANES * 4)(dy, dy, x_arr, x_arr, w)


def _gdn_pre_fwd(proj, conv_w, heads, name):
    s = proj.shape[0]
    ncb = conv_w.shape[1] // LANES
    tr = _tile(s, CONV_ROW_TILE, SUBLANES)
    qscale = float(LANES) ** -0.5

    def body(cur_ref, prev_ref, w_ref, o_ref):
        j, i = pl.program_id(0), pl.program_id(1)
        xp = prev_ref[...] * (i > 0).astype(F32)
        cv = _conv_apply(cur_ref[...], xp, w_ref[...])
        sv = cv * _sigmoid(cv)
        nrm = lax.rsqrt(jnp.sum(sv * sv, axis=-1, keepdims=True) + L2_EPS)
        scale = jnp.where(j < heads, qscale, 1.0)
        o_ref[...] = jnp.where(j < 2 * heads, sv * nrm * scale, sv)

    cur, prev = _halo_specs(tr, lambda j: j)
    return _pcall(body, name=name, grid=(ncb, s // tr),
                  in_specs=[cur, prev, pl.BlockSpec((CONV_WIDTH, LANES), lambda j, i: (0, j))],
                  out_specs=pl.BlockSpec((tr, LANES), lambda j, i: (i, j)),
                  out_shape=jax.ShapeDtypeStruct((s, conv_w.shape[1]), F32), semantics=("parallel", "parallel"),
                  block_bytes=3 * tr * LANES * 4)(proj, proj, conv_w)


def _gdn_pre_bwd(proj, conv_w, dq, dk, dv, heads, name):
    s = proj.shape[0]
    ncb = conv_w.shape[1] // LANES
    tr = _tile(s, CONV_ROW_TILE, SUBLANES)
    qscale = float(LANES) ** -0.5

    def body(cur_ref, prev_ref, w_ref, dq_ref, dk_ref, dv_ref, o_ref):
        j, i = pl.program_id(0), pl.program_id(1)
        xp = prev_ref[...] * (i > 0).astype(F32)
        cv = _conv_apply(cur_ref[...], xp, w_ref[...])
        sg = _sigmoid(cv)
        sv = cv * sg
        nrm = lax.rsqrt(jnp.sum(sv * sv, axis=-1, keepdims=True) + L2_EPS)
        dv = jnp.where(j < heads, dq_ref[...], jnp.where(j < 2 * heads, dk_ref[...], dv_ref[...]))
        scale = jnp.where(j < heads, qscale, 1.0)
        dsn = scale * nrm * (dv - sv * (nrm * nrm) * jnp.sum(dv * sv, axis=-1, keepdims=True))
        ds = jnp.where(j < 2 * heads, dsn, dv)
        o_ref[...] = ds * (sg * (1.0 + cv * (1.0 - sg)))

    cur, prev = _halo_specs(tr, lambda j: j)
    blk = pl.BlockSpec((tr, LANES), lambda j, i: (i, j))

    def part(k):
        return pl.BlockSpec((tr, LANES), lambda j, i: (i, jnp.clip(j - k * heads, 0, heads - 1)))

    return _pcall(body, name=name, grid=(ncb, s // tr),
                  in_specs=[cur, prev, pl.BlockSpec((CONV_WIDTH, LANES), lambda j, i: (0, j)), part(0), part(1),
                            part(2)],
                  out_specs=blk, out_shape=jax.ShapeDtypeStruct((s, conv_w.shape[1]), F32),
                  semantics=("parallel", "parallel"), block_bytes=6 * tr * LANES * 4)(proj, proj, conv_w, dq, dk, dv)


def _dot_split(a, b, dims=(((1,), (0,)), ((), ()))):
    a_hi, b_hi = a.astype(BF16), b.astype(BF16)
    a_lo, b_lo = (a - a_hi.astype(F32)).astype(BF16), (b - b_hi.astype(F32)).astype(BF16)

    def dot(u, v):
        return lax.dot_general(u, v, dims, preferred_element_type=F32)

    return dot(a_hi, b_hi) + dot(a_hi, b_lo) + dot(a_lo, b_hi)


def _tri_inverse(a_strict, block):
    n = a_strict.shape[0]
    ri = lax.broadcasted_iota(jnp.int32, (n, n), 0)
    ci = lax.broadcasted_iota(jnp.int32, (n, n), 1)
    same8 = (ri >> 3) == (ci >> 3)
    sel = jnp.where((lax.broadcasted_iota(jnp.int32, (n, LANES), 0) & 7)
                    == lax.broadcasted_iota(jnp.int32, (n, LANES), 1), 1.0, 0.0)
    a8 = _dot_split(jnp.where(same8, a_strict, 0.0), sel)
    t8 = sel
    r_in = lax.broadcasted_iota(jnp.int32, (n, 1), 0) & 7
    for j in range(SUBLANES - 1):
        row_j = jnp.broadcast_to(t8.reshape(n // SUBLANES, SUBLANES, LANES)[:, j:j + 1, :],
                                 (n // SUBLANES, SUBLANES, LANES)).reshape(n, LANES)
        t8 = t8 - jnp.where(r_in > j, a8[:, j:j + 1] * row_j, 0.0)
    t = jnp.where(same8, _dot_split(t8, sel, (((1,), (1,)), ((), ()))), 0.0)
    size = SUBLANES
    while size < block:
        sh = size.bit_length() - 1
        lower_left = (((ri >> (sh + 1)) == (ci >> (sh + 1))) & (((ri >> sh) & 1) == 1) & (((ci >> sh) & 1) == 0))
        t = t - _dot(t, _dot(jnp.where(lower_left, a_strict, 0.0), t))
        size *= 2
    return t


GDN_HEAD_GROUP = 4
_CHUNK_SHIFT = GDN_CHUNK.bit_length() - 1
_LANE_SHIFT = LANES.bit_length() - 1


def _stack_heads(ref, hb):
    return jnp.concatenate([ref[:, i * LANES:(i + 1) * LANES] for i in range(hb)], axis=0)


def _diag_blocks(x, hb):
    c = GDN_CHUNK
    return jnp.concatenate([x[i * c:(i + 1) * c, i * LANES:(i + 1) * LANES] for i in range(hb)], axis=0)


def _expand_blocks(y, hb):
    row_blk = lax.shift_right_logical(lax.broadcasted_iota(jnp.int32, y.shape, 0), _CHUNK_SHIFT)
    return jnp.concatenate([jnp.where(row_blk == j, y, 0.0) for j in range(hb)], axis=1)


def _gdn_group_terms(q, k, v, ab, alog, dtb, head0, hb, heads):
    c = GDN_CHUNK
    r = hb * c
    lane = lax.broadcasted_iota(jnp.int32, (1, LANES), 1)

    def column(lane0):
        return jnp.concatenate([jnp.sum(jnp.where(lane == lane0 + head0 + i, ab, 0.0), axis=1, keepdims=True)
                                for i in range(hb)], axis=0)

    def per_head(vec):
        return jnp.concatenate([jnp.broadcast_to(jnp.sum(jnp.where(lane == head0 + i, vec, 0.0), axis=1,
                                                         keepdims=True), (c, 1)) for i in range(hb)], axis=0)

    pre = column(0) + per_head(dtb)
    neg_ea = -jnp.exp(per_head(alog))
    g = neg_ea * _softplus(pre)
    beta = _sigmoid(column(heads))
    ri = lax.broadcasted_iota(jnp.int32, (r, r), 0)
    ci = lax.broadcasted_iota(jnp.int32, (r, r), 1)
    same = lax.shift_right_logical(ri, _CHUNK_SHIFT) == lax.shift_right_logical(ci, _CHUNK_SHIFT)
    eye = ri == ci
    causal = same & (ri >= ci)
    strict = same & (ri > ci)
    g_row = jnp.sum(jnp.where(eye, g, 0.0), axis=0, keepdims=True)
    gc_col = jnp.sum(jnp.where(causal, g_row, 0.0), axis=1, keepdims=True)
    gc_row = jnp.sum(jnp.where(same & (ri <= ci), g, 0.0), axis=0, keepdims=True)
    gl_col = jnp.sum(jnp.where(same, g_row, 0.0), axis=1, keepdims=True)
    decay = jnp.where(causal, jnp.exp(jnp.where(causal, gc_col - gc_row, 0.0)), 0.0)
    e_last_col = jnp.exp(gl_col)
    e_last_lanes = jnp.concatenate([jnp.broadcast_to(e_last_col[i * c:i * c + 1, :], (1, LANES))
                                    for i in range(hb)], axis=1)
    egc = jnp.exp(gc_col)
    ekl = jnp.exp(gl_col - gc_col)
    kb = k * beta
    vb = v * beta
    kk = _dot_nt(kb, k)
    a_strict = jnp.where(strict, kk * decay, 0.0)
    return dict(pre=pre, neg_ea=neg_ea, g=g, beta=beta, ri=ri, ci=ci, same=same, eye=eye, causal=causal,
                strict=strict, decay=decay, e_last_col=e_last_col, e_last_lanes=e_last_lanes, egc=egc, ekl=ekl,
                kb=kb, vb=vb, kk=kk, a_strict=a_strict, lane=lane)


def _gdn_head_group(heads):
    hb = GDN_HEAD_GROUP
    while heads % hb:
        hb //= 2
    return hb


def _gdn_fwd(qkv, proj, ab_blk, alog, dtb, heads, name):
    s = qkv.shape[0]
    c = GDN_CHUNK
    nc = s // c
    hb = _gdn_head_group(heads)
    ng = heads // hb
    r = hb * c

    def body(q_ref, k_ref, v_ref, ab_ref, alog_ref, dtb_ref, o_ref, t_ref, s0_ref, state_ref):
        grp, ch = pl.program_id(0), pl.program_id(1)

        @pl.when(ch == 0)
        def _():
            state_ref[...] = jnp.zeros_like(state_ref)

        q, k, v = _stack_heads(q_ref, hb), _stack_heads(k_ref, hb), _stack_heads(v_ref, hb)
        tm = _gdn_group_terms(q, k, v, ab_ref[...], alog_ref[...], dtb_ref[...], grp * hb, hb, heads)
        t_inv = _tri_inverse(tm['a_strict'], c)
        u = _dot(t_inv, tm['vb'])
        w = _dot(t_inv, tm['kb'] * tm['egc'])
        qk = jnp.where(tm['causal'], _dot_nt(q, k) * tm['decay'], 0.0)
        st = state_ref[...]
        v_new = u - _diag_blocks(_dot(w, st), hb)
        out = _diag_blocks(_dot(q * tm['egc'], st), hb) + _dot(qk, v_new)
        for i in range(hb):
            o_ref[:, i * LANES:(i + 1) * LANES] = out[i * c:(i + 1) * c, :]
        t_ref[...] = t_inv
        s0_ref[...] = st
        state_ref[...] = st * tm['e_last_lanes'] + _dot_tn(k * tm['ekl'], _expand_blocks(v_new, hb))

    def blk(off):
        return pl.BlockSpec((c, hb * LANES), lambda g, n: (n, off * ng + g))

    vec = pl.BlockSpec((1, LANES), lambda g, n: (0, 0))
    return _pcall(
        body, name=name, grid=(ng, nc),
        in_specs=[blk(0), blk(1), blk(2), pl.BlockSpec((c, LANES), lambda g, n: (n, ab_blk)), vec, vec],
        out_specs=(blk(0), pl.BlockSpec((None, None, r, r), lambda g, n: (g, n, 0, 0)),
                   pl.BlockSpec((None, None, LANES, hb * LANES), lambda g, n: (g, n, 0, 0))),
        out_shape=(jax.ShapeDtypeStruct((s, heads * LANES), F32), jax.ShapeDtypeStruct((ng, nc, r, r), F32),
                   jax.ShapeDtypeStruct((ng, nc, LANES, hb * LANES), F32)),
        scratch_shapes=[pltpu.VMEM((LANES, hb * LANES), F32)], semantics=("parallel", "arbitrary"),
        block_bytes=8 * r * LANES * 4 + 2 * r * r * 4 + 2 * LANES * hb * LANES * 4)(qkv, qkv, qkv, proj, alog, dtb)


def _gdn_bwd(qkv, proj, ab_blk, alog, dtb, t_all, s0_all, d_o, heads, name):
    s = qkv.shape[0]
    c = GDN_CHUNK
    nc = s // c
    hb = _gdn_head_group(heads)
    ng = heads // hb
    r = hb * c

    def body(q_ref, k_ref, v_ref, ab_ref, alog_ref, dtb_ref, t_ref, s0_ref, do_ref,
             dq_ref, dk_ref, dv_ref, dgb_ref, ds_ref):
        grp, step = pl.program_id(0), pl.program_id(1)

        @pl.when(step == 0)
        def _():
            ds_ref[...] = jnp.zeros_like(ds_ref)

        q, k, v = _stack_heads(q_ref, hb), _stack_heads(k_ref, hb), _stack_heads(v_ref, hb)
        do = _stack_heads(do_ref, hb)
        tm = _gdn_group_terms(q, k, v, ab_ref[...], alog_ref[...], dtb_ref[...], grp * hb, hb, heads)
        ri, ci, same, eye = tm['ri'], tm['ci'], tm['same'], tm['eye']
        causal, strict, decay = tm['causal'], tm['strict'], tm['decay']
        egc, ekl, kb, vb, beta = tm['egc'], tm['ekl'], tm['kb'], tm['vb'], tm['beta']
        t_inv = t_ref[...]
        st = s0_ref[...]
        ds_next = ds_ref[...]
        kbg = kb * egc
        u = _dot(t_inv, vb)
        w = _dot(t_inv, kbg)
        qkm = _dot_nt(q, k)
        qk = jnp.where(causal, qkm * decay, 0.0)
        v_new = u - _diag_blocks(_dot(w, st), hb)
        qd = q * egc
        kd = k * ekl
        do_x = _expand_blocks(do, hb)

        dqd = _dot_nt(do_x, st)
        dqk = jnp.where(causal, _dot_nt(do, v_new), 0.0)
        dvn = _dot_tn(qk, do) + _diag_blocks(_dot(kd, ds_next), hb)
        dkd = _dot_nt(_expand_blocks(v_new, hb), ds_next)
        sd = jnp.sum(st * ds_next, axis=0, keepdims=True)
        dgl = jnp.concatenate([jnp.broadcast_to(jnp.sum(sd[:, i * LANES:(i + 1) * LANES], axis=1, keepdims=True),
                                                (c, 1)) for i in range(hb)], axis=0) * tm['e_last_col']
        dvn_x = _expand_blocks(dvn, hb)
        dw = -_dot_nt(dvn_x, st)
        ds_ref[...] = _dot_tn(qd, do_x) + tm['e_last_lanes'] * ds_next - _dot_tn(w, dvn_x)
        dt = _dot_nt(dvn, vb) + _dot_nt(dw, kbg)
        dvb = _dot_tn(t_inv, dvn)
        dkbg = _dot_tn(t_inv, dw)
        da_m = jnp.where(strict, -_dot_tn(t_inv, _dot_nt(dt, t_inv)), 0.0)
        dad = da_m * decay
        dkb = _dot(dad, k) + dkbg * egc
        dqkd = dqk * decay
        dq = _dot(dqkd, k) + dqd * egc
        dk = _dot_tn(dad, kb) + _dot_tn(dqkd, q) + dkd * ekl + dkb * beta
        e_mat = (da_m * tm['kk'] + dqk * qkm) * decay
        s_kd = jnp.sum(dkd * kd, axis=1, keepdims=True)
        s_kd_row = jnp.sum(jnp.where(eye, s_kd, 0.0), axis=0, keepdims=True)
        dgl = dgl + jnp.sum(jnp.where(same, s_kd_row, 0.0), axis=1, keepdims=True)
        col_sum = jnp.sum(e_mat, axis=0, keepdims=True)
        col_sum_c = jnp.sum(jnp.where(eye, col_sum, 0.0), axis=1, keepdims=True)
        dgc = (jnp.sum(e_mat, axis=1, keepdims=True) - col_sum_c + jnp.sum(dqd * qd, axis=1, keepdims=True)
               - s_kd + jnp.sum(dkbg * kbg, axis=1, keepdims=True))
        row_c = lax.broadcasted_iota(jnp.int32, (r, 1), 0)
        dgc = dgc + jnp.where((row_c & (c - 1)) == c - 1, dgl, 0.0)
        dgc_row = jnp.sum(jnp.where(eye, dgc, 0.0), axis=0, keepdims=True)
        dg = jnp.sum(jnp.where(same & (ci >= ri), dgc_row, 0.0), axis=1, keepdims=True)
        dbeta = jnp.sum(dkb * k, axis=1, keepdims=True) + jnp.sum(dvb * v, axis=1, keepdims=True)
        da_pre = dg * tm['neg_ea'] * _sigmoid(tm['pre'])
        db_pre = dbeta * beta * (1.0 - beta)
        lane = tm['lane']
        head_row = grp * hb + lax.shift_right_logical(row_c, _CHUNK_SHIFT)
        dgb = (jnp.where(lane == head_row, da_pre, 0.0) + jnp.where(lane == heads + head_row, db_pre, 0.0)
               + jnp.where(lane == 2 * heads + head_row, dg * tm['g'], 0.0))
        dvv = dvb * beta
        for i in range(hb):
            cols, rows = slice(i * LANES, (i + 1) * LANES), slice(i * c, (i + 1) * c)
            dq_ref[:, cols] = dq[rows, :]
            dk_ref[:, cols] = dk[rows, :]
            dv_ref[:, cols] = dvv[rows, :]
            dgb_ref[:, cols] = dgb[rows, :]

    def blk(off):
        return pl.BlockSpec((c, hb * LANES), lambda g, n: (nc - 1 - n, off * ng + g))

    vec = pl.BlockSpec((1, LANES), lambda g, n: (0, 0))
    gw = heads * LANES
    dq, dk, dv, dgb = _pcall(
        body, name=name, grid=(ng, nc),
        in_specs=[blk(0), blk(1), blk(2), pl.BlockSpec((c, LANES), lambda g, n: (nc - 1 - n, ab_blk)),
                  vec, vec, pl.BlockSpec((None, None, r, r), lambda g, n: (g, nc - 1 - n, 0, 0)),
                  pl.BlockSpec((None, None, LANES, hb * LANES), lambda g, n: (g, nc - 1 - n, 0, 0)), blk(0)],
        out_specs=(blk(0), blk(0), blk(0), blk(0)),
        out_shape=tuple(jax.ShapeDtypeStruct((s, gw), F32) for _ in range(4)),
        scratch_shapes=[pltpu.VMEM((LANES, hb * LANES), F32)], semantics=("parallel", "arbitrary"),
        block_bytes=12 * r * LANES * 4 + 2 * r * r * 4 + 2 * LANES * hb * LANES * 4)(
            qkv, qkv, qkv, proj, alog, dtb, t_all, s0_all, d_o)
    return dq, dk, dv, dgb


def _gdn_post_fwd(o, proj, z_col0, gain, name):
    s, gw = o.shape
    heads = gw // LANES
    tr = _tile(s, CONV_ROW_TILE, SUBLANES)

    def body(o_ref, z_ref, g_ref, y_ref):
        ov, zv = o_ref[...], z_ref[...]
        r = lax.rsqrt(jnp.mean(ov * ov, axis=-1, keepdims=True) + RMS_EPS)
        y_ref[...] = (ov * r * g_ref[...] * (zv * _sigmoid(zv))).astype(BF16)

    blk = pl.BlockSpec((tr, LANES), lambda i, h: (i, h))
    return _pcall(body, name=name, grid=(s // tr, heads),
                  in_specs=[blk, pl.BlockSpec((tr, LANES), lambda i, h: (i, z_col0 + h)),
                            pl.BlockSpec((1, LANES), lambda i, h: (0, 0))],
                  out_specs=blk, out_shape=jax.ShapeDtypeStruct((s, gw), BF16), semantics=("parallel", "parallel"),
                  block_bytes=3 * tr * LANES * 4)(o, proj, gain.reshape(1, LANES))


def _gdn_post_bwd(o, proj, z_col0, gain, dy, name):
    s, gw = o.shape
    heads = gw // LANES
    tr = _tile(s, CONV_ROW_TILE, SUBLANES)

    def body(o_ref, z_ref, g_ref, dy_ref, do_ref, dz_ref, dg_ref):
        ov, zv, gv, dyv = o_ref[...], z_ref[...], g_ref[...], dy_ref[...]
        r = lax.rsqrt(jnp.mean(ov * ov, axis=-1, keepdims=True) + RMS_EPS)
        nv = ov * r
        sg = _sigmoid(zv)
        sz = zv * sg
        dn = dyv * gv * sz
        do_ref[...] = r * (dn - nv * jnp.mean(dn * nv, axis=-1, keepdims=True))
        dz_ref[...] = dyv * nv * gv * (sg * (1.0 + zv * (1.0 - sg)))

        @pl.when((pl.program_id(0) == 0) & (pl.program_id(1) == 0))
        def _():
            dg_ref[...] = jnp.zeros_like(dg_ref)

        dg_ref[...] += jnp.sum(dyv * nv * sz, axis=0, keepdims=True)

    blk = pl.BlockSpec((tr, LANES), lambda i, h: (i, h))
    vec = pl.BlockSpec((1, LANES), lambda i, h: (0, 0))
    return _pcall(body, name=name, grid=(s // tr, heads),
                  in_specs=[blk, pl.BlockSpec((tr, LANES), lambda i, h: (i, z_col0 + h)), vec, blk],
                  out_specs=(blk, blk, vec),
                  out_shape=(jax.ShapeDtypeStruct((s, gw), F32), jax.ShapeDtypeStruct((s, gw), F32),
                             jax.ShapeDtypeStruct((1, LANES), F32)),
                  semantics=("arbitrary", "arbitrary"), block_bytes=6 * tr * LANES * 4)(
                      o, proj, gain.reshape(1, LANES), dy)


def _dab_reduce(dgb, name):
    s, gw = dgb.shape
    heads = gw // LANES
    tr = _tile(s, 512, SUBLANES)

    def body(d_ref, o_ref, cs_ref):
        acc = d_ref[:, 0:LANES]
        for h in range(1, heads):
            acc = acc + d_ref[:, h * LANES:(h + 1) * LANES]
        o_ref[...] = acc

        @pl.when(pl.program_id(0) == 0)
        def _():
            cs_ref[...] = jnp.zeros_like(cs_ref)

        cs_ref[...] += jnp.sum(acc, axis=0, keepdims=True)

    return _pcall(body, name=name, grid=(s // tr,), in_specs=[pl.BlockSpec((tr, gw), lambda i: (i, 0))],
                  out_specs=(pl.BlockSpec((tr, LANES), lambda i: (i, 0)), pl.BlockSpec((1, LANES), lambda i: (0, 0))),
                  out_shape=(jax.ShapeDtypeStruct((s, LANES), F32), jax.ShapeDtypeStruct((1, LANES), F32)),
                  semantics=("arbitrary",), block_bytes=tr * gw * 4)(dgb)


def _lru_gates(xc, wa, wx, ba, bx, lam):
    r = _sigmoid(_dot(xc, wa) + ba)
    ig = _sigmoid(_dot(xc, wx) + bx)
    sp = _softplus(-lam)
    log_a = -LRU_C * r * sp
    a = jnp.exp(log_a)
    e2 = jnp.exp(2.0 * log_a)
    mult = jnp.sqrt(jnp.maximum(1.0 - e2, 0.0))
    return r, ig, sp, a, e2, mult


def _lru_fwd(xc, proj, y_col0, wa, wx, ba, bx, lam, name):
    s, lw = xc.shape
    nb = lw // LANES
    tr = _tile(s, 256, SUBLANES)

    def body(xc_ref, y_ref, wa_ref, wx_ref, ba_ref, bx_ref, lam_ref, h_ref, o_ref, carry_ref):
        @pl.when(pl.program_id(1) == 0)
        def _():
            carry_ref[...] = jnp.zeros_like(carry_ref)

        xv = xc_ref[...]
        _, ig, _, a, _, mult = _lru_gates(xv, wa_ref[...], wx_ref[...], ba_ref[...], bx_ref[...], lam_ref[...])
        b = mult * (ig * xv)
        row = lax.broadcasted_iota(jnp.int32, (tr, LANES), 0)
        sh = 1
        while sh < tr:
            keep = row >= sh
            b = a * jnp.where(keep, pltpu.roll(b, sh, 0), 0.0) + b
            a = a * jnp.where(keep, pltpu.roll(a, sh, 0), 1.0)
            sh *= 2
        hv = a * carry_ref[0:1, :] + b
        h_ref[...] = hv
        carry_ref[...] = jnp.broadcast_to(hv[tr - 1:tr, :], carry_ref.shape)
        gy, _ = _gelu_and_grad(y_ref[...])
        o_ref[...] = (hv * gy).astype(BF16)

    blk = pl.BlockSpec((tr, LANES), lambda n, i: (i, n))
    wspec = pl.BlockSpec((None, LANES, LANES), lambda n, i: (n, 0, 0))
    vec = pl.BlockSpec((1, LANES), lambda n, i: (0, n))
    return _pcall(body, name=name, grid=(nb, s // tr),
                  in_specs=[blk, pl.BlockSpec((tr, LANES), lambda n, i: (i, y_col0 + n)), wspec, wspec, vec, vec, vec],
                  out_specs=(blk, blk),
                  out_shape=(jax.ShapeDtypeStruct((s, lw), F32), jax.ShapeDtypeStruct((s, lw), BF16)),
                  scratch_shapes=[pltpu.VMEM((SUBLANES, LANES), F32)], semantics=("parallel", "arbitrary"),
                  block_bytes=8 * tr * LANES * 4)(xc, proj, wa, wx, ba.reshape(1, lw), bx.reshape(1, lw),
                                                  lam.reshape(1, lw))


def _lru_bwd(d_out, xc, hseq, proj, y_col0, wa, wx, ba, bx, lam, name):
    s, lw = xc.shape
    nb = lw // LANES
    tr = _tile(s, 256, SUBLANES)
    per = tr // SUBLANES
    ni = s // tr
    nrow8 = s // SUBLANES

    def body(do_ref, xc_ref, xn_ref, h_ref, hp_ref, y_ref, wa_ref, wx_ref, ba_ref, bx_ref, lam_ref,
             dxc_ref, dy_ref, dwa_ref, dwx_ref, dba_ref, dbx_ref, dlam_ref, carry_ref):
        step = pl.program_id(1)
        tile = ni - 1 - step

        @pl.when(step == 0)
        def _():
            carry_ref[...] = jnp.zeros_like(carry_ref)
            dwa_ref[...] = jnp.zeros_like(dwa_ref)
            dwx_ref[...] = jnp.zeros_like(dwx_ref)
            dba_ref[...] = jnp.zeros_like(dba_ref)
            dbx_ref[...] = jnp.zeros_like(dbx_ref)
            dlam_ref[...] = jnp.zeros_like(dlam_ref)

        wav, wxv, bav, bxv, lamv = wa_ref[...], wx_ref[...], ba_ref[...], bx_ref[...], lam_ref[...]
        xv = xc_ref[...]
        r, ig, sp, a, e2, mult = _lru_gates(xv, wav, wxv, bav, bxv, lamv)
        a_next = _lru_gates(xn_ref[...], wav, wxv, bav, bxv, lamv)[3] * (tile < ni - 1).astype(F32)
        hv = h_ref[...]
        h_prev = _shift_down(hv, hp_ref[...] * (tile > 0).astype(F32), 1)
        yv = y_ref[...]
        gy, dgy = _gelu_and_grad(yv)
        dov = do_ref[...]
        dy_ref[...] = dov * hv * dgy
        coef = _shift_up(a, a_next, 1)
        bb = dov * gy
        row = lax.broadcasted_iota(jnp.int32, (tr, LANES), 0)
        sh = 1
        while sh < tr:
            keep = row < tr - sh
            bb = coef * jnp.where(keep, pltpu.roll(bb, tr - sh, 0), 0.0) + bb
            coef = coef * jnp.where(keep, pltpu.roll(coef, tr - sh, 0), 1.0)
            sh *= 2
        lam_t = coef * carry_ref[0:1, :] + bb
        carry_ref[...] = jnp.broadcast_to(lam_t[0:1, :], carry_ref.shape)
        d_a = lam_t * h_prev
        d_mult = lam_t * (ig * xv)
        d_ix = lam_t * mult
        d_la = d_a * a - d_mult * e2 / jnp.maximum(mult, 1e-30)
        d_r = d_la * (-LRU_C * sp)
        dlam_ref[...] += jnp.sum(d_la * (LRU_C * r) * _sigmoid(-lamv), axis=0, keepdims=True)
        d_pa = d_r * r * (1.0 - r)
        d_px = (d_ix * xv) * ig * (1.0 - ig)
        dxc_ref[...] = d_ix * ig + _dot_nt(d_pa, wav) + _dot_nt(d_px, wxv)
        dwa_ref[...] += _dot_tn(xv, d_pa)
        dwx_ref[...] += _dot_tn(xv, d_px)
        dba_ref[...] += jnp.sum(d_pa, axis=0, keepdims=True)
        dbx_ref[...] += jnp.sum(d_px, axis=0, keepdims=True)

    blk = pl.BlockSpec((tr, LANES), lambda n, i: (ni - 1 - i, n))
    nxt = pl.BlockSpec((SUBLANES, LANES), lambda n, i: (jnp.minimum((ni - i) * per, nrow8 - 1), n))
    prv = pl.BlockSpec((SUBLANES, LANES), lambda n, i: (jnp.maximum((ni - 1 - i) * per - 1, 0), n))
    wspec = pl.BlockSpec((None, LANES, LANES), lambda n, i: (n, 0, 0))
    vec = pl.BlockSpec((1, LANES), lambda n, i: (0, n))
    return _pcall(
        body, name=name, grid=(nb, ni),
        in_specs=[blk, blk, nxt, blk, prv, pl.BlockSpec((tr, LANES), lambda n, i: (ni - 1 - i, y_col0 + n)),
                  wspec, wspec, vec, vec, vec],
        out_specs=(blk, blk, wspec, wspec, vec, vec, vec),
        out_shape=(jax.ShapeDtypeStruct((s, lw), F32), jax.ShapeDtypeStruct((s, lw), F32),
                   jax.ShapeDtypeStruct((nb, LANES, LANES), F32), jax.ShapeDtypeStruct((nb, LANES, LANES), F32),
                   jax.ShapeDtypeStruct((1, lw), F32), jax.ShapeDtypeStruct((1, lw), F32),
                   jax.ShapeDtypeStruct((1, lw), F32)),
        scratch_shapes=[pltpu.VMEM((SUBLANES, LANES), F32)], semantics=("parallel", "arbitrary"),
        block_bytes=12 * tr * LANES * 4)(d_out, xc, xc, hseq, hseq, proj, wa, wx, ba.reshape(1, lw),
                                         bx.reshape(1, lw), lam.reshape(1, lw))


def _merge_fwd(proj, gg_col0, gl_col0, bg, bl, name):
    s, d = bg.shape
    tr, tc = _tile(s, 256, SUBLANES), _tile(d, 1024)
    cb = tc // LANES

    def body(gg_ref, gl_ref, bg_ref, bl_ref, o_ref):
        o_ref[...] = (_sigmoid(gg_ref[...]) * bg_ref[...] + _sigmoid(gl_ref[...]) * bl_ref[...]).astype(BF16)

    blk = pl.BlockSpec((tr, tc), lambda i, j: (i, j))
    return _pcall(body, name=name, grid=(s // tr, d // tc),
                  in_specs=[pl.BlockSpec((tr, tc), lambda i, j: (i, gg_col0 // cb + j)),
                            pl.BlockSpec((tr, tc), lambda i, j: (i, gl_col0 // cb + j)), blk, blk],
                  out_specs=blk, out_shape=jax.ShapeDtypeStruct((s, d), BF16), semantics=("parallel", "parallel"),
                  block_bytes=5 * tr * tc * 4)(proj, proj, bg, bl)


def _merge_bwd(proj, gg_col0, gl_col0, bg, bl, dm, name):
    s, d = bg.shape
    tr, tc = _tile(s, 256, SUBLANES), _tile(d, 1024)
    cb = tc // LANES

    def body(gg_ref, gl_ref, bg_ref, bl_ref, dm_ref, dgg_ref, dgl_ref, dbg_ref, dbl_ref):
        dmv = dm_ref[...]
        sg, sl = _sigmoid(gg_ref[...]), _sigmoid(gl_ref[...])
        dgg_ref[...] = (dmv * bg_ref[...] * sg * (1.0 - sg)).astype(BF16)
        dgl_ref[...] = (dmv * bl_ref[...] * sl * (1.0 - sl)).astype(BF16)
        dbg_ref[...] = (dmv * sg).astype(BF16)
        dbl_ref[...] = (dmv * sl).astype(BF16)

    blk = pl.BlockSpec((tr, tc), lambda i, j: (i, j))
    sh = jax.ShapeDtypeStruct((s, d), BF16)
    return _pcall(body, name=name, grid=(s // tr, d // tc),
                  in_specs=[pl.BlockSpec((tr, tc), lambda i, j: (i, gg_col0 // cb + j)),
                            pl.BlockSpec((tr, tc), lambda i, j: (i, gl_col0 // cb + j)), blk, blk, blk],
                  out_specs=(blk, blk, blk, blk), out_shape=(sh, sh, sh, sh), semantics=("parallel", "parallel"),
                  block_bytes=8 * tr * tc * 4)(proj, proj, bg, bl, dm)


def _sum_slots(slots, name):
    n, r, c = slots.shape
    tr = _tile(r, max(2 * SUBLANES, (1 << 19) // (c * 4)), 2 * SUBLANES)

    def body(s_ref, o_ref):
        acc = s_ref[0].astype(F32)
        for q in range(1, n):
            acc = acc + s_ref[q].astype(F32)
        o_ref[...] = acc

    return _pcall(body, name=name, grid=(r // tr,), in_specs=[pl.BlockSpec((n, tr, c), lambda i: (0, i, 0))],
                  out_specs=pl.BlockSpec((tr, c), lambda i: (i, 0)), out_shape=jax.ShapeDtypeStruct((r, c), F32),
                  semantics=("parallel",), block_bytes=(n + 1) * tr * c * 4)(slots)


def _adamw(w, g_parts, m, v, name):
    r, c = w.shape
    np_ = len(g_parts)
    tr = _tile(r, max(SUBLANES, (1 << 20) // (c * 4)), SUBLANES)
    c1 = 1.0 - ADAM_B1 ** ADAM_STEP
    c2 = 1.0 - ADAM_B2 ** ADAM_STEP

    def body(*refs):
        w_ref, m_ref, v_ref = refs[0], refs[1 + np_], refs[2 + np_]
        g_ref, d_ref, nm_ref, nv_ref = refs[3 + np_:]
        g = refs[1][...]
        for p in range(1, np_):
            g = g + refs[1 + p][...]
        nm = ADAM_B1 * m_ref[...] + (1.0 - ADAM_B1) * g
        nv = ADAM_B2 * v_ref[...] + (1.0 - ADAM_B2) * (g * g)
        g_ref[...] = g
        nm_ref[...] = nm
        nv_ref[...] = nv
        d_ref[...] = -ADAM_LR * ((nm / c1) / (jnp.sqrt(nv / c2) + ADAM_EPS) + ADAM_WD * w_ref[...])

    blk = pl.BlockSpec((tr, c), lambda i: (i, 0))
    sh = jax.ShapeDtypeStruct((r, c), F32)
    return _pcall(body, name=name, grid=(r // tr,), in_specs=[blk] * (3 + np_), out_specs=(blk,) * 4,
                  out_shape=(sh,) * 4, semantics=("parallel",), block_bytes=(7 + np_) * tr * c * 4)(
                      w, *g_parts, m, v)


def _pair_sum(core, mine, theirs, name):
    _, n, r, c = mine.shape
    tr = _tile(r, max(2 * SUBLANES, (1 << 19) // (c * 4)), 2 * SUBLANES)

    def body(core_ref, a_ref, b_ref, o_ref):
        o_ref[...] = (a_ref[...].astype(F32) + b_ref[...].astype(F32)).astype(BF16)

    return _pcall(body, name=name, grid=(n, r // tr),
                  in_specs=[pl.BlockSpec((None, None, tr, c), lambda q, i, core_ref: (core_ref[0], q, i, 0)),
                            pl.BlockSpec((None, tr, c), lambda q, i, core_ref: (q, i, 0))],
                  out_specs=pl.BlockSpec((None, tr, c), lambda q, i, core_ref: (q, i, 0)),
                  out_shape=jax.ShapeDtypeStruct((n, r, c), BF16), semantics=("parallel", "parallel"),
                  block_bytes=3 * tr * c * 4, scalar_prefetch=1)(core, mine, theirs)


def _sum_landed(chip, landed, own, name):
    n, r, c = landed.shape
    tr = _tile(r, max(2 * SUBLANES, (1 << 19) // (c * 4)), 2 * SUBLANES)

    def body(chip_ref, l_ref, o_ref, t_ref):
        acc = o_ref[...].astype(F32)
        for q in range(n):
            acc = acc + l_ref[q].astype(F32)
        t_ref[...] = acc

    return _pcall(body, name=name, grid=(r // tr,),
                  in_specs=[pl.BlockSpec((n, tr, c), lambda i, chip_ref: (0, i, 0)),
                            pl.BlockSpec((None, tr, c), lambda i, chip_ref: (chip_ref[0], i, 0))],
                  out_specs=pl.BlockSpec((tr, c), lambda i, chip_ref: (i, 0)),
                  out_shape=jax.ShapeDtypeStruct((r, c), F32), semantics=("parallel",),
                  block_bytes=(n + 3) * tr * c * 4, scalar_prefetch=1)(chip, landed, own)


def _adamw_quarters(core, w, g_mine, g_other, m, v, after, name):
    nl, nh, r, c = w.shape
    tr = _tile(r, max(SUBLANES, (1 << 19) // (c * 4)), SUBLANES)
    c1 = 1.0 - ADAM_B1 ** ADAM_STEP
    c2 = 1.0 - ADAM_B2 ** ADAM_STEP

    def body(core_ref, w_ref, *refs):
        g_refs, (m_ref, v_ref, _, g_ref, d_ref, nm_ref, nv_ref) = refs[:2 * nl], refs[2 * nl:]
        mine = pl.program_id(1) == core_ref[0]
        g = jnp.where(mine, g_refs[0][...], g_refs[nl][...])
        for l in range(1, nl):
            g = jnp.where(pl.program_id(0) == l, jnp.where(mine, g_refs[l][...], g_refs[nl + l][...]), g)
        nm = ADAM_B1 * m_ref[...] + (1.0 - ADAM_B1) * g
        nv = ADAM_B2 * v_ref[...] + (1.0 - ADAM_B2) * (g * g)
        g_ref[...] = g
        nm_ref[...] = nm
        nv_ref[...] = nv
        d_ref[...] = -ADAM_LR * ((nm / c1) / (jnp.sqrt(nv / c2) + ADAM_EPS) + ADAM_WD * w_ref[...])

    blk = pl.BlockSpec((None, None, tr, c), lambda l, hf, i, core_ref: (l, hf, i, 0))
    gblk = pl.BlockSpec((tr, c), lambda l, hf, i, core_ref: (i, 0))
    sh = jax.ShapeDtypeStruct(w.shape, F32)
    return _pcall(body, name=name, grid=(nl, nh, r // tr),
                  in_specs=[blk] + [gblk] * (2 * nl) + [blk, blk, pl.BlockSpec(memory_space=pl.ANY)],
                  out_specs=(blk,) * 4, out_shape=(sh,) * 4, semantics=("parallel", "parallel", "parallel"),
                  block_bytes=(7 + 2 * nl) * tr * c * 4, scalar_prefetch=1)(core, w, *g_mine, *g_other, m, v, after)


HBM_SPEC = pl.BlockSpec(memory_space=pltpu.HBM)


def _other_chips(x, y):
    return [(1 - x, y), (x, 1 - y), (1 - x, 1 - y)]


SEM_SPEC = pl.BlockSpec(memory_space=pltpu.SEMAPHORE)
DATAFLOW_EFFECT = pltpu.SideEffectType.DATAFLOW_SIDE_EFFECTING


def _split_start(name, bufs, n_copies, build):
    nb = len(bufs)

    def body(*refs):
        starts, _ = build(refs[:nb], refs[nb], refs[nb + 1])
        for cp in starts:
            cp.start()
        refs[-1][...] = jnp.zeros_like(refs[-1])

    out = pl.pallas_call(
        body, name=name,
        out_shape=(pltpu.SemaphoreType.DMA((n_copies,)), pltpu.SemaphoreType.DMA((n_copies,)),
                   *[pltpu.HBM(b.shape, b.dtype) for b in bufs], jax.ShapeDtypeStruct((SUBLANES, LANES), F32)),
        in_specs=[HBM_SPEC] * nb,
        out_specs=(SEM_SPEC, SEM_SPEC, *[HBM_SPEC] * nb, pl.BlockSpec(memory_space=pltpu.VMEM)),
        input_output_aliases={i: 2 + i for i in range(nb)},
        compiler_params=pltpu.CompilerParams(has_side_effects=DATAFLOW_EFFECT),
    )(*[pltpu.with_memory_space_constraint(b, pltpu.HBM) for b in bufs])
    return out[0], out[1], list(out[2:2 + nb]), out[2 + nb]


def _split_wait(name, send_sems, recv_sems, bufs, after, build):
    nb = len(bufs)

    def body(*refs):
        starts, waits = build(refs[:nb], refs[nb], refs[nb + 1])
        for cp in starts:
            cp.wait_send()
        for cp in waits:
            cp.wait_recv()

    out = pl.pallas_call(
        body, name=name, out_shape=tuple(pltpu.HBM(b.shape, b.dtype) for b in bufs),
        in_specs=[HBM_SPEC] * nb + [SEM_SPEC, SEM_SPEC, pl.BlockSpec(memory_space=pl.ANY)],
        out_specs=tuple([HBM_SPEC] * nb), input_output_aliases={i: i for i in range(nb)},
        compiler_params=pltpu.CompilerParams(has_side_effects=DATAFLOW_EFFECT),
    )(*bufs, send_sems, recv_sems, after)
    return list(out)


def _gather_ici_copies(nt, refs, send_sems, recv_sems):
    srcs, lands = refs[:nt], refs[nt:]
    x, y, c = lax.axis_index("x"), lax.axis_index("y"), lax.axis_index("c")
    me = 2 * x + y
    starts, waits = [], []
    for t in range(nt):
        for j, (px, py) in enumerate(_other_chips(x, y)):
            def copy(slot, t=t, j=j, px=px, py=py):
                return pltpu.make_async_remote_copy(
                    src_ref=srcs[t].at[c], dst_ref=lands[t].at[slot].at[c], send_sem=send_sems.at[3 * t + j],
                    recv_sem=recv_sems.at[3 * t + j], device_id=(px, py, c), device_id_type=pl.DeviceIdType.MESH)
            starts.append(copy(me))
            waits.append(copy(2 * px + py))
    return starts, waits


def _gather_d2d_copies(nt, refs, send_sems, recv_sems):
    x, y, c = lax.axis_index("x"), lax.axis_index("y"), lax.axis_index("c")
    starts, waits = [], []
    for t in range(nt):
        for j, (px, py) in enumerate(_other_chips(x, y)):
            def copy(half, t=t, j=j, px=px, py=py):
                place = refs[t].at[2 * px + py].at[half]
                return pltpu.make_async_remote_copy(
                    src_ref=place, dst_ref=place, send_sem=send_sems.at[3 * t + j], recv_sem=recv_sems.at[3 * t + j],
                    device_id=(x, y, 1 - c), device_id_type=pl.DeviceIdType.MESH)
            starts.append(copy(c))
            waits.append(copy(1 - c))
    return starts, waits


def _scatter_ici_copies(nt, refs, send_sems, recv_sems):
    srcs, lands = refs[:nt], refs[nt:]
    x, y, c = lax.axis_index("x"), lax.axis_index("y"), lax.axis_index("c")
    me = 2 * x + y
    starts, waits = [], []
    for t in range(nt):
        for j, (px, py) in enumerate(_other_chips(x, y)):
            def copy(slot, t=t, j=j, px=px, py=py):
                return pltpu.make_async_remote_copy(
                    src_ref=srcs[t].at[2 * px + py], dst_ref=lands[t].at[slot], send_sem=send_sems.at[3 * t + j],
                    recv_sem=recv_sems.at[3 * t + j], device_id=(px, py, c), device_id_type=pl.DeviceIdType.MESH)
            starts.append(copy(me))
            waits.append(copy(2 * px + py))
    return starts, waits


def _sibling_half_copies(nt, refs, send_sems, recv_sems):
    srcs, lands = refs[:nt], refs[nt:]
    x, y, c = lax.axis_index("x"), lax.axis_index("y"), lax.axis_index("c")
    copies = [pltpu.make_async_remote_copy(src_ref=srcs[t].at[1 - c], dst_ref=lands[t], send_sem=send_sems.at[t],
                                           recv_sem=recv_sems.at[t], device_id=(x, y, 1 - c),
                                           device_id_type=pl.DeviceIdType.MESH) for t in range(nt)]
    return copies, copies


def _sibling_exchange(arrs, name):
    n = len(arrs)

    def body(*refs):
        ins, outs = refs[:n], refs[n:2 * n]
        send_sems, recv_sems = refs[2 * n:]
        sib = (lax.axis_index("x"), lax.axis_index("y"), 1 - lax.axis_index("c"))
        copies = [pltpu.make_async_remote_copy(src_ref=ins[t], dst_ref=outs[t], send_sem=send_sems.at[t],
                                               recv_sem=recv_sems.at[t], device_id=sib,
                                               device_id_type=pl.DeviceIdType.MESH) for t in range(n)]
        for cp in copies:
            cp.start()
        for cp in copies:
            cp.wait_recv()
        for cp in copies:
            cp.wait_send()

    return pl.pallas_call(
        body, name=name, in_specs=[HBM_SPEC] * n, out_specs=(HBM_SPEC,) * n,
        out_shape=tuple(jax.ShapeDtypeStruct(a.shape, a.dtype) for a in arrs),
        scratch_shapes=[pltpu.SemaphoreType.DMA((n,)), pltpu.SemaphoreType.DMA((n,))])(*arrs)


def _all_devices_copies(nt, refs, send_sems, recv_sems):
    srcs, lands = refs[:nt], refs[nt:]
    x, y, c = lax.axis_index("x"), lax.axis_index("y"), lax.axis_index("c")
    me = 4 * x + 2 * y + c
    starts, waits = [], []
    for t in range(nt):
        for mask in range(1, N_DEVICES):
            px = 1 - x if mask & 4 else x
            py = 1 - y if mask & 2 else y
            pc = 1 - c if mask & 1 else c
            k = (N_DEVICES - 1) * t + mask - 1

            def copy(slot, t=t, k=k, px=px, py=py, pc=pc):
                return pltpu.make_async_remote_copy(
                    src_ref=srcs[t], dst_ref=lands[t].at[slot], send_sem=send_sems.at[k], recv_sem=recv_sems.at[k],
                    device_id=(px, py, pc), device_id_type=pl.DeviceIdType.MESH)
            starts.append(copy(me))
            waits.append(copy(4 * px + 2 * py + pc))
    return starts, waits


def _all_devices_gather(buf, name):
    def body(in_ref, out_ref, send_sems, recv_sems, local_sem):
        x, y, c = lax.axis_index("x"), lax.axis_index("y"), lax.axis_index("c")
        me = 4 * x + 2 * y + c

        def peer(mask):
            px = 1 - x if mask & 4 else x
            py = 1 - y if mask & 2 else y
            pc = 1 - c if mask & 1 else c
            return px, py, pc

        def remote(mask, dst_slot):
            return pltpu.make_async_remote_copy(
                src_ref=in_ref, dst_ref=out_ref.at[dst_slot], send_sem=send_sems.at[mask - 1],
                recv_sem=recv_sems.at[mask - 1], device_id=peer(mask), device_id_type=pl.DeviceIdType.MESH)

        lc = pltpu.make_async_copy(in_ref, out_ref.at[me], local_sem)
        lc.start()
        sends = [remote(mask, me) for mask in range(1, N_DEVICES)]
        for cp in sends:
            cp.start()
        for mask in range(1, N_DEVICES):
            px, py, pc = peer(mask)
            remote(mask, 4 * px + 2 * py + pc).wait_recv()
        for cp in sends:
            cp.wait_send()
        lc.wait()

    return pl.pallas_call(
        body, name=name, in_specs=[HBM_SPEC], out_specs=HBM_SPEC,
        out_shape=jax.ShapeDtypeStruct((N_DEVICES,) + buf.shape, buf.dtype),
        scratch_shapes=[pltpu.SemaphoreType.DMA((N_DEVICES - 1,)), pltpu.SemaphoreType.DMA((N_DEVICES - 1,)),
                        pltpu.SemaphoreType.DMA])(buf)


def _pad_lanes(vec):
    return jnp.pad(vec.astype(F32), (0, LANES - vec.shape[0])).reshape(1, LANES)


def _layer_fwd(x, wl, fetch, dm, tag):
    heads, gw, lw, d = dm['heads'], dm['gw'], dm['lw'], dm['d']
    h = _rms_fwd(x, wl['attn_norm'], f"rms1_fwd{tag}")
    wl.update(fetch('in', h))
    proj = _matmul(h, wl['w_in_p'], mode='nn', tn=PROJ_COL_TILE, name=f"proj{tag}")
    alog, dtb = _pad_lanes(wl['gdn_a_log']), _pad_lanes(wl['gdn_dt_bias'])
    qkv = _gdn_pre_fwd(proj, wl['gdn_conv_w'], heads, f"gdn_pre_fwd{tag}")
    o, t_all, s0_all = _gdn_fwd(qkv, proj, dm['ab_blk'], alog, dtb, heads, f"gdn_fwd{tag}")
    o_gdn = _gdn_post_fwd(o, proj, dm['z_blk'], wl['gdn_norm'], f"gdn_post_fwd{tag}")
    xc = _conv_bias_fwd(proj, dm['xb_blk'], wl['lru_conv_w'], wl['lru_conv_b'], f"lru_conv_fwd{tag}")
    hseq, o_lru = _lru_fwd(xc, proj, dm['yb_blk'], wl['lru_w_a'], wl['lru_w_x'], wl['lru_b_a'], wl['lru_b_x'],
                           wl['lru_lambda'], f"lru_fwd{tag}")
    wl.update(fetch('mix', o))
    bg = _matmul(o_gdn, wl['w_branch_gdn'], mode='nn', name=f"branch_gdn{tag}")
    bl = _matmul(o_lru, wl['w_branch_lru'], mode='nn', name=f"branch_lru{tag}")
    merged = _merge_fwd(proj, dm['gg_blk'], dm['gl_blk'], bg, bl, f"merge_fwd{tag}")
    wl.update(fetch('mlp', bg))
    x_mid = _matmul(merged, wl['w_out'], mode='nn', add=x, name=f"out_proj{tag}")
    h2 = _rms_fwd(x_mid, wl['mlp_norm'], f"rms2_fwd{tag}")
    ur, act = _matmul(h2, wl['w_up'], mode='nn', epilogue='relu2', name=f"mlp_up{tag}")
    x_out = _matmul(act, wl['w_down'], mode='nn', add=x_mid, name=f"mlp_down{tag}")
    saved = dict(x=x, h=h, proj=proj, qkv=qkv, o=o, t_all=t_all, s0_all=s0_all, o_gdn=o_gdn, xc=xc, hseq=hseq,
                 o_lru=o_lru, bg=bg, bl=bl, merged=merged, x_mid=x_mid, h2=h2, ur=ur, act=act, alog=alog, dtb=dtb)
    return x_out, saved


def _layer_bwd(dx_out, dx_out_b, wl, sv, hook, dm, tag):
    heads, gw, lw, d = dm['heads'], dm['gw'], dm['lw'], dm['d']
    g = {}
    du = _matmul(dx_out_b, wl['w_down'], mode='nt', epilogue='mul2x', extra=sv['ur'], out_dtype=BF16,
                 name=f"d_mlp_act{tag}")
    def dw(n, lhs, rhs):
        return _matmul(lhs, rhs, mode='tn', out_dtype=BF16, shard_axis=BIG_SHARD_AXIS[n] - 1, name=f"d{n}{tag}")

    g['w_down'] = dw('w_down', sv['act'], dx_out_b)
    g['w_up'] = dw('w_up', sv['h2'], du)
    hook('mlp', g, wl, 'mlp_norm')
    dh2 = _matmul(du, wl['w_up'], mode='nt', name=f"d_h2{tag}")
    dx_mid, dx_mid_b, g['mlp_norm'] = _rms_bwd(sv['x_mid'], wl['mlp_norm'], dh2, dx_out, f"rms2_bwd{tag}")
    dmerged = _matmul(dx_mid_b, wl['w_out'], mode='nt', name=f"d_merged{tag}")
    g['w_out'] = dw('w_out', sv['merged'], dx_mid_b)
    dgg, dgl, dbg, dbl = _merge_bwd(sv['proj'], dm['gg_blk'], dm['gl_blk'], sv['bg'], sv['bl'], dmerged,
                                    f"merge_bwd{tag}")
    g['w_branch_gdn'] = dw('w_branch_gdn', sv['o_gdn'], dbg)
    g['w_branch_lru'] = dw('w_branch_lru', sv['o_lru'], dbl)
    hook('mix', g, wl, 'gdn_norm')
    do_gdn = _matmul(dbg, wl['w_branch_gdn'], mode='nt', name=f"d_o_gdn{tag}")
    do_lru = _matmul(dbl, wl['w_branch_lru'], mode='nt', name=f"d_o_lru{tag}")
    d_o, dz, dgn = _gdn_post_bwd(sv['o'], sv['proj'], dm['z_blk'], wl['gdn_norm'], do_gdn, f"gdn_post_bwd{tag}")
    g['gdn_norm'] = dgn.reshape(-1)
    dq, dk, dv, dgb = _gdn_bwd(sv['qkv'], sv['proj'], dm['ab_blk'], sv['alog'], sv['dtb'], sv['t_all'], sv['s0_all'], d_o, heads,
                               f"gdn_bwd{tag}")
    dconv = _gdn_pre_bwd(sv['proj'], wl['gdn_conv_w'], dq, dk, dv, heads, f"gdn_pre_bwd{tag}")
    dqkv, g['gdn_conv_w'], _ = _conv_bwd(dconv, sv['proj'], 0, wl['gdn_conv_w'], f"gdn_conv_bwd{tag}")
    dab, dab_sum = _dab_reduce(dgb, f"dab_reduce{tag}")
    g['gdn_dt_bias'] = dab_sum[0, :heads]
    g['gdn_a_log'] = dab_sum[0, 2 * heads:3 * heads]
    dxc, dyb, g['lru_w_a'], g['lru_w_x'], dba, dbx, dlam = _lru_bwd(
        do_lru, sv['xc'], sv['hseq'], sv['proj'], dm['yb_blk'], wl['lru_w_a'], wl['lru_w_x'], wl['lru_b_a'],
        wl['lru_b_x'], wl['lru_lambda'], f"lru_bwd{tag}")
    g['lru_b_a'], g['lru_b_x'], g['lru_lambda'] = dba.reshape(-1), dbx.reshape(-1), dlam.reshape(-1)
    dxb, g['lru_conv_w'], dcb = _conv_bwd(dxc, sv['proj'], dm['xb_blk'], wl['lru_conv_w'], f"lru_conv_bwd{tag}")
    g['lru_conv_b'] = dcb.reshape(-1)
    dproj = jnp.concatenate([dqkv.astype(BF16), dz.astype(BF16), dxb.astype(BF16), dyb.astype(BF16), dgg, dgl,
                             dab.astype(BF16), jnp.zeros((dab.shape[0], dm['np'] - dm['main'] - LANES), BF16)],
                            axis=1)
    g['w_in_p'] = _matmul(sv['h'], dproj, mode='tn', out_dtype=BF16, tn=PROJ_COL_TILE, name=f"dw_in{tag}")
    hook('in', g, wl, 'attn_norm')
    dh = _matmul(dproj, wl['w_in_p'], mode='nt', name=f"d_h{tag}")
    dx_in, dx_in_b, g['attn_norm'] = _rms_bwd(sv['x'], wl['attn_norm'], dh, dx_mid, f"rms1_bwd{tag}")
    g['attn_norm'] = g['attn_norm'].reshape(-1)
    g['mlp_norm'] = g['mlp_norm'].reshape(-1)
    return dx_in, dx_in_b, g


def _dims(d, heads, lw):
    gw = heads * LANES
    nab = 2 * heads
    blk = dict(z_blk=3 * heads, xb_blk=4 * heads, yb_blk=4 * heads + lw // LANES)
    gg0 = 4 * gw + 2 * lw
    main = gg0 + 2 * d
    return dict(d=d, heads=heads, gw=gw, lw=lw, nab=nab, gg_blk=gg0 // LANES, gl_blk=(gg0 + d) // LANES,
                main=main, ab_blk=main // LANES, np=-(-(main + LANES) // PROJ_COL_TILE) * PROJ_COL_TILE, **blk)


def _pad_w_in(w_in, dm):
    c0 = 4 * dm['gw']
    nab = dm['nab']
    return jnp.concatenate([w_in[:, :c0], w_in[:, c0 + nab:], w_in[:, c0:c0 + nab],
                            jnp.zeros((w_in.shape[0], dm['np'] - dm['main'] - nab), w_in.dtype)], axis=1)


def _unpad_w_in(gp, dm):
    c0 = 4 * dm['gw']
    nab = dm['nab']
    main = dm['main']
    return jnp.concatenate([gp[:, :c0], gp[:, main:main + nab], gp[:, c0:main]], axis=1)


def _local_step(x, target, layers, fetchers, hooks, final_norm, dm):
    saved = []
    cur = x
    for li, wl in enumerate(layers):
        cur, sv = _layer_fwd(cur, wl, fetchers[li], dm, f"_l{li}")
        saved.append(sv)
    loss_blk, dx, dx_b, dfin = _loss_head(cur, final_norm, target, "loss_head")
    grads = [None] * len(layers)
    for li in reversed(range(len(layers))):
        dx, dx_b, grads[li] = _layer_bwd(dx, dx_b, layers[li], saved[li], hooks[li], dm, f"_l{li}")
    return loss_blk[0, 0], dx, grads, dfin.reshape(-1)


def kernel(x, attn_norm, w_in, gdn_conv_w, gdn_a_log, gdn_dt_bias, gdn_norm, lru_conv_w, lru_conv_b, lru_w_a, lru_b_a, lru_w_x, lru_b_x, lru_lambda, w_branch_gdn, w_branch_lru, w_out, mlp_norm, w_up, w_down, final_norm, loss_target, m_attn_norm, m_w_in, m_gdn_conv_w, m_gdn_a_log, m_gdn_dt_bias, m_gdn_norm, m_lru_conv_w, m_lru_conv_b, m_lru_w_a, m_lru_b_a, m_lru_w_x, m_lru_b_x, m_lru_lambda, m_w_branch_gdn, m_w_branch_lru, m_w_out, m_mlp_norm, m_w_up, m_w_down, m_final_norm, v_attn_norm, v_w_in, v_gdn_conv_w, v_gdn_a_log, v_gdn_dt_bias, v_gdn_norm, v_lru_conv_w, v_lru_conv_b, v_lru_w_a, v_lru_b_a, v_lru_w_x, v_lru_b_x, v_lru_lambda, v_w_branch_gdn, v_w_branch_lru, v_w_out, v_mlp_norm, v_w_up, v_w_down, v_final_norm):
    w = dict(attn_norm=attn_norm, w_in=w_in, gdn_conv_w=gdn_conv_w, gdn_a_log=gdn_a_log, gdn_dt_bias=gdn_dt_bias,
             gdn_norm=gdn_norm, lru_conv_w=lru_conv_w, lru_conv_b=lru_conv_b, lru_w_a=lru_w_a, lru_b_a=lru_b_a,
             lru_w_x=lru_w_x, lru_b_x=lru_b_x, lru_lambda=lru_lambda, w_branch_gdn=w_branch_gdn,
             w_branch_lru=w_branch_lru, w_out=w_out, mlp_norm=mlp_norm, w_up=w_up, w_down=w_down,
             final_norm=final_norm)
    m = dict(attn_norm=m_attn_norm, w_in=m_w_in, gdn_conv_w=m_gdn_conv_w, gdn_a_log=m_gdn_a_log,
             gdn_dt_bias=m_gdn_dt_bias, gdn_norm=m_gdn_norm, lru_conv_w=m_lru_conv_w, lru_conv_b=m_lru_conv_b,
             lru_w_a=m_lru_w_a, lru_b_a=m_lru_b_a, lru_w_x=m_lru_w_x, lru_b_x=m_lru_b_x, lru_lambda=m_lru_lambda,
             w_branch_gdn=m_w_branch_gdn, w_branch_lru=m_w_branch_lru, w_out=m_w_out, mlp_norm=m_mlp_norm,
             w_up=m_w_up, w_down=m_w_down, final_norm=m_final_norm)
    v = dict(attn_norm=v_attn_norm, w_in=v_w_in, gdn_conv_w=v_gdn_conv_w, gdn_a_log=v_gdn_a_log,
             gdn_dt_bias=v_gdn_dt_bias, gdn_norm=v_gdn_norm, lru_conv_w=v_lru_conv_w, lru_conv_b=v_lru_conv_b,
             lru_w_a=v_lru_w_a, lru_b_a=v_lru_b_a, lru_w_x=v_lru_w_x, lru_b_x=v_lru_b_x, lru_lambda=v_lru_lambda,
             w_branch_gdn=v_w_branch_gdn, w_branch_lru=v_w_branch_lru, w_out=v_w_out, mlp_norm=v_mlp_norm,
             w_up=v_w_up, w_down=v_w_down, final_norm=v_final_norm)
    n_layers = attn_norm.shape[0]
    d = x.shape[-1]
    heads = gdn_a_log.shape[-1]
    lw = lru_conv_b.shape[-1]
    dm = _dims(d, heads, lw)
    big_names = list(BIG_SHARD_AXIS)
    conv_names = list(CONV_SHARD_AXIS)
    chip = 2 * lax.axis_index("x") + lax.axis_index("y")

    conv_flat = jnp.concatenate([w[n].reshape(-1) for n in conv_names])
    conv_rows = -(-conv_flat.shape[0] // (SUBLANES * LANES)) * SUBLANES
    conv_buf = jnp.pad(conv_flat, (0, conv_rows * LANES - conv_flat.shape[0])).reshape(conv_rows, LANES)
    conv_all = _all_devices_gather(conv_buf, "conv_allgather").reshape(N_CHIPS, 2, -1)[:, 0]
    conv_full, off = {}, 0
    for n in conv_names:
        shard = w[n]
        parts = conv_all[:, off:off + shard.size].reshape((N_CHIPS,) + shard.shape)
        conv_full[n] = jnp.concatenate([parts[q] for q in range(N_CHIPS)], axis=CONV_SHARD_AXIS[n])
        off += shard.size

    def start_gather(li, group):
        halves, lands = [], []
        for n in GATHER_GROUPS[group]:
            s = w[n][li].astype(BF16)
            hv = s.reshape((2, s.shape[0] // 2) + s.shape[1:])
            halves.append(hv)
            lands.append(lax.dynamic_update_index_in_dim(lax.empty((N_CHIPS,) + hv.shape, BF16), hv, chip, 0))
        nt = len(halves)
        return _split_start(f"wgather_{group}_l{li}_ici_start", halves + lands, 3 * nt,
                            functools.partial(_gather_ici_copies, nt))

    pending = {(li, group): start_gather(li, group) for li in range(n_layers) for group in GATHER_GROUPS}

    swapping = {}

    def start_swap(li, group, after):
        nt = len(GATHER_GROUPS[group])
        send, recv, bufs, _ = pending.pop((li, group))
        bufs = _split_wait(f"wgather_{group}_l{li}_ici_wait", send, recv, bufs, after,
                           functools.partial(_gather_ici_copies, nt))
        swapping[(li, group)] = _split_start(f"wgather_{group}_l{li}_d2d_start", bufs[nt:], 3 * nt,
                                             functools.partial(_gather_d2d_copies, nt))
        return swapping[(li, group)][3]

    def make_fetch(li):
        def fetch(group, after):
            names = GATHER_GROUPS[group]
            nt = len(names)
            if (li, group) not in swapping:
                start_swap(li, group, after)
            send, recv, lands, _ = swapping.pop((li, group))
            lands = _split_wait(f"wgather_{group}_l{li}_d2d_wait", send, recv, lands, after,
                                functools.partial(_gather_d2d_copies, nt))
            if (li, group) == (0, 'mix'):
                tie = sum(start_swap(l2, g2, lands[0])[0, 0] for l2, g2 in list(pending))
                layers[0]['mlp_norm'] = layers[0]['mlp_norm'] + tie
            out = {}
            for n, land in zip(names, lands):
                slots = land.reshape((N_CHIPS, 2 * land.shape[2]) + land.shape[3:])
                out[n] = jnp.concatenate([slots[q] for q in range(N_CHIPS)], axis=BIG_SHARD_AXIS[n] - 1)
            if 'w_in' in out:
                out['w_in_p'] = _pad_w_in(out.pop('w_in'), dm)
            return out
        return fetch

    layers = []
    for li in range(n_layers):
        wl = {n: w[n][li] for n in SMALL_NAMES if n != 'final_norm' and n not in CONV_SHARD_AXIS}
        for n in conv_names:
            wl[n] = conv_full[n][li]
        layers.append(wl)
    layers[0]['attn_norm'] = layers[0]['attn_norm'] + sum(handle[3][0, 0] for handle in pending.values())

    core = lax.axis_index("c").astype(jnp.int32).reshape(1)
    chip_op = chip.astype(jnp.int32).reshape(1)
    sending, in_flight, reduced = {}, {}, {}

    def reduce_begin(group, li, g):
        names = GATHER_GROUPS[group]
        nt = len(names)
        contrib = []
        for n in names:
            if n != 'w_in':
                contrib.append(g[n])
                continue
            pieces = jnp.stack(jnp.split(_unpad_w_in(g['w_in_p'], dm), N_CHIPS, axis=BIG_SHARD_AXIS[n] - 1), axis=0)
            rows_half = pieces.shape[1] // 2
            contrib.append(jnp.swapaxes(pieces.reshape((N_CHIPS, 2, rows_half) + pieces.shape[2:]), 0, 1))
        theirs = [lax.empty(cb.shape[1:], BF16) for cb in contrib]
        send, recv, bufs, token = _split_start(f"gsend_{group}_l{li}_start", contrib + theirs, nt,
                                               functools.partial(_sibling_half_copies, nt))
        sending[(group, li)] = (send, recv, bufs)
        return token

    def reduce_scatter(group, li, after):
        names = GATHER_GROUPS[group]
        nt = len(names)
        send, recv, bufs = sending.pop((group, li))
        bufs = _split_wait(f"gsend_{group}_l{li}_wait", send, recv, bufs, after,
                           functools.partial(_sibling_half_copies, nt))
        sums = [_pair_sum(core, mine, th, f"gpair_{n}_l{li}") for n, mine, th in zip(names, bufs[:nt], bufs[nt:])]
        lands = [jnp.zeros(sm.shape, BF16) for sm in sums]
        send, recv, bufs, token = _split_start(f"gscatter_{group}_l{li}_start", sums + lands, 3 * nt,
                                               functools.partial(_scatter_ici_copies, nt))
        in_flight[(group, li)] = (send, recv, bufs)
        return token

    def reduce_end(jobs, after, name):
        keys, totals = [], []
        for group, li in jobs:
            names = GATHER_GROUPS[group]
            nt = len(names)
            send, recv, bufs = in_flight.pop((group, li))
            bufs = _split_wait(f"gscatter_{group}_l{li}_wait", send, recv, bufs, after,
                               functools.partial(_scatter_ici_copies, nt))
            totals += [_sum_landed(chip_op, land, own, f"gtotal_{n}_l{li}")
                       for n, own, land in zip(names, bufs[:nt], bufs[nt:])]
            keys += [(n, li) for n in names]
        others = _sibling_exchange(totals, name)
        for key, mine, other in zip(keys, totals, others):
            reduced[key] = (mine, other)

    order = [(group, li) for li in reversed(range(n_layers)) for group in reversed(list(GATHER_GROUPS))]

    def make_hook(li):
        def hook(group, g, wl, gain):
            at = order.index((group, li))
            latest = g['w_in_p'] if group == 'in' else g[GATHER_GROUPS[group][-1]]
            tie = reduce_begin(group, li, g)[0, 0]
            if at > 0:
                tie = tie + reduce_scatter(*order[at - 1], latest)[0, 0]
            wl[gain] = wl[gain] + tie
        return hook

    loss_local, dx, grads, dfin = _local_step(x[0], loss_target[0], layers, [make_fetch(li) for li in range(n_layers)],
                                              [make_hook(li) for li in range(n_layers)], final_norm, dm)
    loss = lax.psum(loss_local, MESH_AXES)
    last_scatter_token = reduce_scatter(*order[-1], dx)

    small_g = {n: jnp.stack([grads[li][n] for li in range(n_layers)], axis=0)
               for n in SMALL_NAMES if n != 'final_norm'}
    small_g['final_norm'] = dfin
    flat = jnp.concatenate([small_g[n].reshape(-1) for n in PACK_NAMES])
    n_flat = flat.shape[0]
    row_unit = 32 * SUBLANES
    rows = -(-n_flat // (row_unit * LANES)) * row_unit
    small_srcs = [jnp.pad(flat, (0, rows * LANES - n_flat)).reshape(rows, LANES)]
    small_srcs += [small_g[n].reshape(-1, LANES) for n in MID_NAMES]
    device = 2 * chip + lax.axis_index("c")
    small_lands = [lax.dynamic_update_index_in_dim(lax.empty((N_DEVICES,) + b.shape, F32), b, device, 0)
                   for b in small_srcs]
    n_small = len(small_srcs)
    small_copies = functools.partial(_all_devices_copies, n_small)
    small_send, small_recv, small_bufs, small_token = _split_start(
        "small_grad_start", small_srcs + small_lands, (N_DEVICES - 1) * n_small, small_copies)

    out_g, out_d, out_m, out_v = {}, {}, {}, {}

    def adamw_big(n, after):
        quarters = (n_layers, 2, w[n].shape[1] // 2, w[n].shape[2])
        res = _adamw_quarters(core, w[n].reshape(quarters), [reduced[(n, li)][0] for li in range(n_layers)],
                              [reduced[(n, li)][1] for li in range(n_layers)], m[n].reshape(quarters),
                              v[n].reshape(quarters), after, f"adamw_{n}")
        out_g[n], out_d[n], out_m[n], out_v[n] = (r.reshape(w[n].shape) for r in res)
        return res[1]

    early = [group for group in GATHER_GROUPS if group != 'in']
    last = small_token + last_scatter_token
    reduce_end([job for job in order if job[0] != 'in'], last, "gswap_early")
    for group in early:
        for n in GATHER_GROUPS[group]:
            last = adamw_big(n, last)
    reduce_end([job for job in order if job[0] == 'in'], last, "gswap_in")
    last = adamw_big('w_in', last)
    small_bufs = _split_wait("small_grad_wait", small_send, small_recv, small_bufs, last, small_copies)
    small_sums = [_sum_slots(land, f"small_grad_sum_{k}") for k, land in enumerate(small_bufs[n_small:])]
    small_red = {}
    off = 0
    for n in PACK_NAMES:
        size = small_g[n].size
        small_red[n] = small_sums[0].reshape(-1)[off:off + size].reshape(small_g[n].shape)
        off += size
    for n, g2d in zip(MID_NAMES, small_sums[1:]):
        res = _adamw(w[n].reshape(g2d.shape), [g2d], m[n].reshape(g2d.shape), v[n].reshape(g2d.shape), f"adamw_{n}")
        out_g[n], out_d[n], out_m[n], out_v[n] = (r.reshape(w[n].shape) for r in res)
    for n, ax in CONV_SHARD_AXIS.items():
        width = w[n].shape[ax]
        small_red[n] = lax.dynamic_slice_in_dim(small_red[n], chip * width, width, axis=ax)

    def pack(tree):
        fl = jnp.concatenate([tree[n].reshape(-1) for n in PACK_NAMES])
        return jnp.pad(fl, (0, rows * LANES - fl.shape[0])).reshape(rows, LANES)

    res = _adamw(pack(w), [pack(small_red)], pack(m), pack(v), "adamw_small")
    for r, dst in zip(res, (out_g, out_d, out_m, out_v)):
        fl = r.reshape(-1)
        off = 0
        for n in PACK_NAMES:
            dst[n] = fl[off:off + w[n].size].reshape(w[n].shape)
            off += w[n].size

    return (loss, dx[None], *[out_g[n] for n in WEIGHT_NAMES], *[out_d[n] for n in WEIGHT_NAMES],
            *[out_m[n] for n in WEIGHT_NAMES], *[out_v[n] for n in WEIGHT_NAMES])
```
